```python
import jax
import jax.numpy as jnp
from jax import lax
import numpy as np


D_MODEL = 1024
BATCH = 8
SEQ = 4096
DEPTH = 2

HEAD_DIM = 64
N_HEADS_A = 4
N_HEADS_C = 4
N_HEADS_D = 4
CONV_CH = 4 * HEAD_DIM
CONV_K = 3
MIX_WIDTH = (N_HEADS_A + N_HEADS_C + N_HEADS_D) * HEAD_DIM + CONV_CH
N_GROUPS = MIX_WIDTH // HEAD_DIM
BLOCK = 128
CMP_LEN = 32
CMP_STRIDE = 16
CMP_HIDDEN = 256
SEL_BLOCK = 64
N_SELECT = 16
NSA_WINDOW = 512
N_NSA_BRANCH = 3
DILATED_CONFIGS = ((128, 1), (512, 4), (2048, 16))
D_FF = -(-(8 * D_MODEL) // (3 * 256)) * 256
A_COLS = N_HEADS_A * HEAD_DIM + 6 * HEAD_DIM + N_NSA_BRANCH * N_HEADS_A
IN_COLS = A_COLS + 3 * CONV_CH + 3 * N_HEADS_C * HEAD_DIM + 3 * N_HEADS_D * HEAD_DIM
COL_WIDTHS = (N_HEADS_A * HEAD_DIM, HEAD_DIM, HEAD_DIM, HEAD_DIM, HEAD_DIM, HEAD_DIM, HEAD_DIM,
              N_NSA_BRANCH * N_HEADS_A,
              CONV_CH, CONV_CH, CONV_CH,
              N_HEADS_C * HEAD_DIM, N_HEADS_C * HEAD_DIM, N_HEADS_C * HEAD_DIM,
              N_HEADS_D * HEAD_DIM, N_HEADS_D * HEAD_DIM, N_HEADS_D * HEAD_DIM)
NEG_INF = -1e30
FORCE_SCORE = 1e6
RMS_EPS = 1e-6

kernel_name = 'hybrid_nsa_conv_stickbreak_dilated'


def rms_norm(x, g):
    xf = x.astype(jnp.float32)
    y = xf * lax.rsqrt(jnp.mean(xf * xf, axis=-1, keepdims=True) + RMS_EPS)
    return (y * g.astype(jnp.float32)).astype(x.dtype)


def to_heads(t, n):
    b, s, _ = t.shape
    return t.reshape(b, s, n, HEAD_DIM).transpose(0, 2, 1, 3)


def alibi_slopes(n):
    return jnp.exp2(-8.0 * jnp.arange(1, n + 1, dtype=jnp.float32) / n)


def banded_attention(q, k, v, max_back, dist_scale, slopes):
    n, h, l, hd = q.shape
    g = k.shape[1]
    r = h // g
    nb = l // BLOCK
    n_prev = -(-max_back // BLOCK)
    w = (n_prev + 1) * BLOCK
    qb = q.reshape(n, g, r, nb, BLOCK, hd)

    def band(t):
        tb = t.reshape(n, g, nb, BLOCK, hd)
        tp = jnp.pad(tb, ((0, 0), (0, 0), (n_prev, 0), (0, 0), (0, 0)))
        return jnp.concatenate([tp[:, :, i:i + nb] for i in range(n_prev + 1)], axis=3)

    kb, vb = band(k), band(v)
    rows = jnp.arange(BLOCK)
    diff = rows[:, None] + n_prev * BLOCK - jnp.arange(w)[None, :]
    key_idx = jnp.arange(nb)[:, None, None] * BLOCK + rows[None, :, None] - diff[None]
    mask = (diff >= 0) & (diff <= max_back) & (key_idx >= 0)
    s = jnp.einsum('ngrbqd,ngbkd->ngrbqk', qb, kb).astype(jnp.float32) * (HEAD_DIM ** -0.5)
    s = s - slopes.reshape(g, r)[None, :, :, None, None, None] * (diff * dist_scale).astype(jnp.float32)
    s = jnp.where(mask, s, NEG_INF)
    lse = jax.nn.logsumexp(s, axis=-1)
    p = jnp.exp(s - lse[..., None])
    o = jnp.einsum('ngrbqk,ngbkd->ngrbqd', p.astype(v.dtype), vb)
    return o.reshape(n, h, l, hd), lse.reshape(n, h, l)


def compress_tokens(t, pe, w1, w2):
    b, s, _ = t.shape
    n_cmp = (s - CMP_LEN) // CMP_STRIDE + 1
    idx = jnp.arange(n_cmp)[:, None] * CMP_STRIDE + jnp.arange(CMP_LEN)[None, :]
    blocks = (t[:, idx] + pe).reshape(b, n_cmp, CMP_LEN * HEAD_DIM)
    return jax.nn.gelu(blocks @ w1) @ w2


def nsa_mixer(q_tok, kc_tok, vc_tok, ks_tok, vs_tok, kw_tok, vw_tok, gate_logits, b_gate,
              g_q, g_kc, g_ks, g_kw, pe_k, pe_v, w1_k, w2_k, w1_v, w2_v, slopes):
    b, s, _ = q_tok.shape
    h = N_HEADS_A
    scale = HEAD_DIM ** -0.5
    q = rms_norm(to_heads(q_tok, h), g_q)
    kc = rms_norm(compress_tokens(kc_tok, pe_k, w1_k, w2_k), g_kc)
    vc = compress_tokens(vc_tok, pe_v, w1_v, w2_v)
    n_cmp = kc.shape[1]
    cmp_end = jnp.arange(n_cmp) * CMP_STRIDE + CMP_LEN - 1
    n_sel = s // SEL_BLOCK
    top = min(N_SELECT, n_sel)
    ratio = SEL_BLOCK // CMP_STRIDE
    ks = rms_norm(ks_tok, g_ks).reshape(b, n_sel, SEL_BLOCK, HEAD_DIM)
    vs = vs_tok.reshape(b, n_sel, SEL_BLOCK, HEAD_DIM)
    blk_ids = jnp.arange(n_sel)
    bidx = jnp.arange(b)[:, None, None]
    nc = s // BLOCK
    q_chunks = q.reshape(b, h, nc, BLOCK, HEAD_DIM).transpose(2, 0, 1, 3, 4)

    def chunk(args):
        qi, c = args
        t = c * BLOCK + jnp.arange(BLOCK)
        dist_c = t[:, None] - cmp_end[None, :]
        vis = dist_c >= 0
        sc = jnp.einsum('bhqd,bnd->bhqn', qi, kc).astype(jnp.float32) * scale
        sc = jnp.where(vis, sc - slopes[:, None, None] * dist_c.astype(jnp.float32), NEG_INF)
        p_cmp = jax.nn.softmax(sc, axis=-1) * vis
        o_cmp = jnp.einsum('bhqn,bnd->bhqd', p_cmp.astype(vc.dtype), vc)
        imp = jnp.pad(p_cmp.sum(1), ((0, 0), (0, 0), (0, ratio * n_sel - n_cmp)))
        imp_ov = imp
        for sh in range(1, CMP_LEN // CMP_STRIDE):
            imp_ov = imp_ov + jnp.pad(imp[..., :-sh], ((0, 0), (0, 0), (sh, 0)))
        imp_blk = imp_ov.reshape(b, BLOCK, n_sel, ratio).sum(-1)
        cur = t // SEL_BLOCK
        forced = (blk_ids[None] == 0) | (blk_ids[None] == cur[:, None]) | (blk_ids[None] == cur[:, None] - 1)
        valid = blk_ids[None] * SEL_BLOCK <= t[:, None]
        score = jnp.where(forced, FORCE_SCORE, jnp.where(valid, imp_blk, -FORCE_SCORE))
        _, idx = lax.top_k(score, top)
        kg = ks[bidx, idx]
        vg = vs[bidx, idx]
        pos = idx[..., None] * SEL_BLOCK + jnp.arange(SEL_BLOCK)
        dist_s = t[None, :, None, None] - pos
        ss = jnp.einsum('bhqd,bqkjd->bhqkj', qi, kg).astype(jnp.float32) * scale
        ss = ss - slopes[None, :, None, None, None] * dist_s[:, None].astype(jnp.float32)
        ss = jnp.where((dist_s >= 0)[:, None], ss, NEG_INF)
        p_slc = jax.nn.softmax(ss.reshape(b, h, BLOCK, top * SEL_BLOCK), axis=-1).reshape(ss.shape)
        o_slc = jnp.einsum('bhqkj,bqkjd->bhqd', p_slc.astype(vg.dtype), vg)
        return o_cmp, o_slc

    o_cmp, o_slc = lax.map(chunk, (q_chunks, jnp.arange(nc)))
    o_cmp = o_cmp.transpose(1, 2, 0, 3, 4).reshape(b, h, s, HEAD_DIM)
    o_slc = o_slc.transpose(1, 2, 0, 3, 4).reshape(b, h, s, HEAD_DIM)
    o_win, _ = banded_attention(q, rms_norm(kw_tok, g_kw)[:, None], vw_tok[:, None], NSA_WINDOW - 1, 1, slopes)
    gates = jax.nn.sigmoid(gate_logits + b_gate).reshape(b, s, h, N_NSA_BRANCH).transpose(0, 2, 1, 3)
    return gates[..., 0:1] * o_cmp + gates[..., 1:2] * o_slc + gates[..., 2:3] * o_win


def short_conv_mixer(gate_b, gate_c, u, conv_w):
    y = lax.conv_general_dilated(gate_c * u, conv_w[:, None, :], window_strides=(1,),
                                 padding=[(CONV_K - 1, 0)], dimension_numbers=('NWC', 'WIO', 'NWC'),
                                 feature_group_count=CONV_CH)
    return gate_b * y


def stick_breaking_attention(q, k, v):
    b, h, s, hd = q.shape
    nc = s // BLOCK
    q_chunks = q.reshape(b, h, nc, BLOCK, hd).transpose(2, 0, 1, 3, 4)
    key_pos = jnp.arange(s)

    def chunk(args):
        qi, c = args
        t = c * BLOCK + jnp.arange(BLOCK)
        z = jnp.einsum('bhqd,bhsd->bhqs', qi, k).astype(jnp.float32) * (hd ** -0.5)
        past = key_pos[None, :] < t[:, None]
        neg_log_keep = jnp.where(past, jax.nn.softplus(z), 0.0)
        between = lax.cumsum(neg_log_keep, axis=3, reverse=True) - neg_log_keep
        attn = jnp.where(past, jnp.exp(jax.nn.log_sigmoid(z) - between), 0.0)
        return jnp.einsum('bhqs,bhsd->bhqd', attn.astype(v.dtype), v)

    o = lax.map(chunk, (q_chunks, jnp.arange(nc)))
    return o.transpose(1, 2, 0, 3, 4).reshape(b, h, s, hd)


def dilated_attention(q, k, v, window, dil, slopes):
    b, h, s, hd = q.shape
    span = dil * BLOCK
    sp = -(-s // span) * span
    l = sp // dil

    def fold(t):
        t = jnp.pad(t, ((0, 0), (0, 0), (0, sp - s), (0, 0)))
        return t.reshape(b, h, l, dil, hd).transpose(0, 3, 1, 2, 4).reshape(b * dil, h, l, hd)

    o, lse = banded_attention(fold(q), fold(k), fold(v), window // dil, dil, slopes)
    o = o.reshape(b, dil, h, l, hd).transpose(0, 2, 3, 1, 4).reshape(b, h, sp, hd)[:, :, :s]
    lse = lse.reshape(b, dil, h, l).transpose(0, 2, 3, 1).reshape(b, h, sp)[:, :, :s]
    return o, lse


def hybrid_layer(x, g_mix, w_in, b_gate, g_q_nsa, g_k_cmp, g_k_slc, g_k_win, pe_k_cmp, pe_v_cmp,
                 w1_k_cmp, w2_k_cmp, w1_v_cmp, w2_v_cmp, conv_w, g_q_dil, g_k_dil, g_out, w_out,
                 g_ffn, w_gate, w_up, w_down):
    b, s, _ = x.shape
    hn = rms_norm(x, g_mix)
    proj = hn @ w_in
    cuts = [int(c) for c in np.cumsum(COL_WIDTHS)[:-1]]
    (qa, kca, vca, ksa, vsa, kwa, vwa, gta, cvb, cvc, cvu,
     qc, kc, vc, qd, kd, vd) = jnp.split(proj, cuts, axis=-1)
    slopes = alibi_slopes(N_HEADS_A + N_HEADS_D)
    slopes_a, slopes_d = slopes[0::2], slopes[1::2]
    o_a = nsa_mixer(qa, kca, vca, ksa, vsa, kwa, vwa, gta, b_gate, g_q_nsa, g_k_cmp, g_k_slc, g_k_win,
                    pe_k_cmp, pe_v_cmp, w1_k_cmp, w2_k_cmp, w1_v_cmp, w2_v_cmp, slopes_a)
    o_b = short_conv_mixer(cvb, cvc, cvu, conv_w).reshape(b, s, CONV_CH // HEAD_DIM, HEAD_DIM)
    o_c = stick_breaking_attention(to_heads(qc, N_HEADS_C), to_heads(kc, N_HEADS_C), to_heads(vc, N_HEADS_C))
    qdh = rms_norm(to_heads(qd, N_HEADS_D), g_q_dil)
    kdh = rms_norm(to_heads(kd, N_HEADS_D), g_k_dil)
    vdh = to_heads(vd, N_HEADS_D)
    outs, lses = [], []
    for window, dil in DILATED_CONFIGS:
        o_i, lse_i = dilated_attention(qdh, kdh, vdh, window, dil, slopes_d)
        outs.append(o_i)
        lses.append(lse_i)
    mix_w = jax.nn.softmax(jnp.stack(lses, axis=0), axis=0)
    o_d = jnp.sum(mix_w[..., None].astype(vdh.dtype) * jnp.stack(outs, axis=0), axis=0)
    groups = jnp.concatenate([o_a.transpose(0, 2, 1, 3), o_b, o_c.transpose(0, 2, 1, 3),
                              o_d.transpose(0, 2, 1, 3)], axis=2)
    groups = rms_norm(groups, g_out.reshape(N_GROUPS, HEAD_DIM)).reshape(b, s, MIX_WIDTH)
    x = x + groups @ w_out
    h2 = rms_norm(x, g_ffn)
    return x + (jax.nn.silu(h2 @ w_gate) * (h2 @ w_up)) @ w_down


def setup_inputs(seed: int = 0) -> dict:
    key = jax.random.key(seed)
    ks = jax.random.split(key, 23)
    L = DEPTH
    hd = HEAD_DIM

    def nrm(k, shape, scale):
        return jax.random.normal(k, shape, jnp.float32) * scale

    def gain(k, shape):
        return 1.0 + 0.02 * jax.random.normal(k, shape, jnp.float32)

    return {
        'x': nrm(ks[0], (BATCH, SEQ, D_MODEL), 1.0),
        'g_mix': gain(ks[1], (L, D_MODEL)),
        'w_in': nrm(ks[2], (L, D_MODEL, IN_COLS), D_MODEL ** -0.5),
        'b_gate': nrm(ks[3], (L, N_NSA_BRANCH * N_HEADS_A), 0.1),
        'g_q_nsa': gain(ks[4], (L, hd)),
        'g_k_cmp': gain(ks[5], (L, hd)),
        'g_k_slc': gain(ks[6], (L, hd)),
        'g_k_win': gain(ks[7], (L, hd)),
        'pe_k_cmp': nrm(ks[8], (L, CMP_LEN, hd), 0.1),
        'pe_v_cmp': nrm(ks[9], (L, CMP_LEN, hd), 0.1),
        'w1_k_cmp': nrm(ks[10], (L, CMP_LEN * hd, CMP_HIDDEN), (CMP_LEN * hd) ** -0.5),
        'w2_k_cmp': nrm(ks[11], (L, CMP_HIDDEN, hd), CMP_HIDDEN ** -0.5),
        'w1_v_cmp': nrm(ks[12], (L, CMP_LEN * hd, CMP_HIDDEN), (CMP_LEN * hd) ** -0.5),
        'w2_v_cmp': nrm(ks[13], (L, CMP_HIDDEN, hd), CMP_HIDDEN ** -0.5),
        'conv_w': nrm(ks[14], (L, CONV_K, CONV_CH), CONV_K ** -0.5),
        'g_q_dil': gain(ks[15], (L, hd)),
        'g_k_dil': gain(ks[16], (L, hd)),
        'g_out': gain(ks[17], (L, MIX_WIDTH)),
        'w_out': nrm(ks[18], (L, MIX_WIDTH, D_MODEL), MIX_WIDTH ** -0.5),
        'g_ffn': gain(ks[19], (L, D_MODEL)),
        'w_gate': nrm(ks[20], (L, D_MODEL, D_FF), D_MODEL ** -0.5),
        'w_up': nrm(ks[21], (L, D_MODEL, D_FF), D_MODEL ** -0.5),
        'w_down': nrm(ks[22], (L, D_FF, D_MODEL), D_FF ** -0.5),
    }


def reference(x, g_mix, w_in, b_gate, g_q_nsa, g_k_cmp, g_k_slc, g_k_win, pe_k_cmp, pe_v_cmp,
              w1_k_cmp, w2_k_cmp, w1_v_cmp, w2_v_cmp, conv_w, g_q_dil, g_k_dil, g_out, w_out,
              g_ffn, w_gate, w_up, w_down):
    for l in range(DEPTH):
        x = hybrid_layer(x, g_mix[l], w_in[l], b_gate[l], g_q_nsa[l], g_k_cmp[l], g_k_slc[l], g_k_win[l],
                         pe_k_cmp[l], pe_v_cmp[l], w1_k_cmp[l], w2_k_cmp[l], w1_v_cmp[l], w2_v_cmp[l],
                         conv_w[l], g_q_dil[l], g_k_dil[l], g_out[l], w_out[l], g_ffn[l],
                         w_gate[l], w_up[l], w_down[l])
    return x
```

```python
import functools

import numpy as np
import jax
import jax.numpy as jnp
from jax import lax
from jax.experimental import pallas as pl
from jax.experimental.pallas import tpu as pltpu

F32 = jnp.float32
BF16 = jnp.bfloat16

HEAD_DIM = 64
N_HEADS = 4
GROUP_W = N_HEADS * HEAD_DIM
CONV_K = 3
CMP_LEN = 32
CMP_STRIDE = 16
SEL_BLOCK = 64
N_SELECT = 16
NSA_WINDOW = 512
DILATED_CONFIGS = ((128, 1), (512, 4), (2048, 16))
NEG_INF = -1e30
FORCE_SCORE = 1e6
RMS_EPS = 1e-6

TQ = 128
TK = 256
LANES = 128
VMEM_LIMIT = 52 * 1024 * 1024

COL_QA, COL_KCA, COL_VCA, COL_KSV, COL_KWV, COL_GATE = 0, 256, 320, 384, 512, 640
COL_CVB, COL_CVC, COL_CVU = 768, 1024, 1280
COL_QC, COL_KC, COL_VC = 1536, 1792, 2048
COL_QD, COL_KD, COL_VD = 2304, 2560, 2816
P_COLS = 3072
N_GATES = 12

_NT = (((1,), (1,)), ((), ()))


def _alibi_slopes():
    s = [2.0 ** (-8.0 * i / 8) for i in range(1, 9)]
    return tuple(s[0::2]), tuple(s[1::2])


SLOPES_A, SLOPES_D = _alibi_slopes()


def _rms(x, g):
    return x * lax.rsqrt(jnp.mean(x * x, axis=-1, keepdims=True) + RMS_EPS) * g


def _key_aug(pos):
    rows = pos.shape[0]
    lane = lax.broadcasted_iota(jnp.int32, (rows, HEAD_DIM), 1)
    hi = (pos >> 6).astype(F32)
    lo = (pos & 63).astype(F32)
    return jnp.where(lane == 0, hi, jnp.where(lane == 1, lo, jnp.where(lane < 4, 1.0, 0.0)))


def _query_aug(t, slope):
    rows = t.shape[0]
    lane = lax.broadcasted_iota(jnp.int32, (rows, HEAD_DIM), 1)
    hi = (t >> 6).astype(F32) * (-64.0 * slope)
    lo = (t & 63).astype(F32) * (-slope)
    return jnp.where(lane == 0, 64.0 * slope,
                     jnp.where(lane == 1, slope, jnp.where(lane == 2, hi, jnp.where(lane == 3, lo, 0.0))))


def _prep_queries(q, g, t0, slopes):
    tpos = t0 + lax.broadcasted_iota(jnp.int32, (q.shape[0], 1), 0)
    out = []
    for h in range(N_HEADS):
        qh = q[:, h * HEAD_DIM:(h + 1) * HEAD_DIM]
        if g is not None:
            qh = _rms(qh, g)
        qh = qh * (HEAD_DIM ** -0.5)
        if slopes is None:
            aug = jnp.zeros_like(qh)
        else:
            aug = _query_aug(tpos, slopes[h])
        out.append(jnp.concatenate([qh, aug], axis=1).astype(BF16))
    return out


def _inproj_kernel(x_ref, g_ref, w_ref, o_ref, *, nchunk):
    h = _rms(x_ref[...], g_ref[...]).astype(BF16)
    cw = P_COLS // nchunk
    for c in range(nchunk):
        o_ref[:, c * cw:(c + 1) * cw] = jnp.dot(h, w_ref[:, c * cw:(c + 1) * cw], preferred_element_type=F32)


def _inproj(x2, g, w):
    t, d = x2.shape
    tm = 512
    return pl.pallas_call(
        functools.partial(_inproj_kernel, nchunk=6),
        grid=(t // tm,),
        in_specs=[pl.BlockSpec((tm, d), lambda i: (i, 0)),
                  pl.BlockSpec((1, d), lambda i: (0, 0)),
                  pl.BlockSpec((d, P_COLS), lambda i: (0, 0))],
        out_specs=pl.BlockSpec((tm, P_COLS), lambda i: (i, 0)),
        out_shape=jax.ShapeDtypeStruct((t, P_COLS), F32),
        compiler_params=pltpu.CompilerParams(dimension_semantics=("parallel",), vmem_limit_bytes=VMEM_LIMIT),
        name="inproj",
    )(x2, g, w)


def _gelu_tanh(x):
    return x * (0.5 * (1.0 + jnp.tanh(np.sqrt(2.0 / np.pi).astype(np.float32) * (x + 0.044715 * (x * x * x)))))


def _compress_kernel(ck_ref, cv_ref, pek_ref, pev_ref, w1k_ref, w2k_ref, w1v_ref, w2v_ref, gk_ref,
                     kc_ref, vc_ref):
    nc = ck_ref.shape[1]
    half = CMP_STRIDE * HEAD_DIM

    def mlp(c_ref, pe_ref, w1_ref, w2_ref):
        c = c_ref[0]
        a = jnp.dot((c + pe_ref[0:1, :]).astype(BF16), w1_ref[0:half, :], preferred_element_type=F32)
        b = jnp.dot((c + pe_ref[1:2, :]).astype(BF16), w1_ref[half:2 * half, :], preferred_element_type=F32)
        hid = a + pltpu.roll(b, nc - 1, 0)
        return jnp.dot(_gelu_tanh(hid).astype(BF16), w2_ref[...], preferred_element_type=F32)

    kc = _rms(mlp(ck_ref, pek_ref, w1k_ref, w2k_ref), gk_ref[...])
    n = lax.broadcasted_iota(jnp.int32, (nc, 1), 0)
    kc_ref[0] = jnp.concatenate([kc, _key_aug(n * CMP_STRIDE + (CMP_LEN - 1))], axis=1).astype(BF16)
    vc_ref[0] = mlp(cv_ref, pev_ref, w1v_ref, w2v_ref).astype(BF16)


def _compress(ck, cv, pek, pev, w1k, w2k, w1v, w2v, gk):
    b, nc, cw = ck.shape
    hid = w1k.shape[1]
    full2 = lambda shape: pl.BlockSpec(shape, lambda i: (0, 0))
    return pl.pallas_call(
        _compress_kernel,
        grid=(b,),
        in_specs=[pl.BlockSpec((1, nc, cw), lambda i: (i, 0, 0)),
                  pl.BlockSpec((1, nc, cw), lambda i: (i, 0, 0)),
                  full2((2, cw)), full2((2, cw)),
                  full2((2 * cw, hid)), full2((hid, HEAD_DIM)),
                  full2((2 * cw, hid)), full2((hid, HEAD_DIM)),
                  full2((1, HEAD_DIM))],
        out_specs=[pl.BlockSpec((1, nc, LANES), lambda i: (i, 0, 0)),
                   pl.BlockSpec((1, nc, HEAD_DIM), lambda i: (i, 0, 0))],
        out_shape=[jax.ShapeDtypeStruct((b, nc, LANES), BF16),
                   jax.ShapeDtypeStruct((b, nc, HEAD_DIM), BF16)],
        compiler_params=pltpu.CompilerParams(dimension_semantics=("parallel",), vmem_limit_bytes=VMEM_LIMIT),
        name="nsa_compress",
    )(ck, cv, pek, pev, w1k, w2k, w1v, w2v, gk)


def _flash_step(kp, qt, vt, bias, carry):
    m, l, acc = carry
    s = lax.dot_general(kp, qt, _NT, preferred_element_type=F32) + bias
    m_new = jnp.maximum(m, jnp.max(s, axis=0, keepdims=True))
    alpha = jnp.exp(m - m_new)
    p = jnp.exp(s - m_new)
    l = alpha * l + jnp.sum(p, axis=0, keepdims=True)
    acc = alpha * acc + jnp.dot(vt, p.astype(BF16), preferred_element_type=F32)
    return m_new, l, acc


def _flash_init():
    return (jnp.full((1, TQ), NEG_INF, F32), jnp.zeros((1, TQ), F32), jnp.zeros((HEAD_DIM, TQ), F32))


def _nsa_kernel(q_ref, kc_ref, vc_ref, ksv_ref, gq_ref, gks_ref, ocmp_ref, oslc_ref,
                ksp_s, vst_s, imp_s, sel_s, bias_s, ot_s, *, seq):
    qi = pl.program_id(1)
    nc = seq // CMP_STRIDE
    nsel = seq // SEL_BLOCK
    nkt = seq // TK

    @pl.when(qi == 0)
    def _prep():
        for c in range(nkt):
            rows = slice(c * TK, (c + 1) * TK)
            blk = ksv_ref[rows, :]
            pos = c * TK + lax.broadcasted_iota(jnp.int32, (TK, 1), 0)
            kn = _rms(blk[:, :HEAD_DIM], gks_ref[...])
            ksp_s[rows, :] = jnp.concatenate([kn, _key_aug(pos)], axis=1).astype(BF16)
            vst_s[c] = blk.T[HEAD_DIM:, :].astype(BF16)
        imp_s[0:8, :] = jnp.zeros((8, TQ), F32)

    t0 = qi * TQ
    qh = _prep_queries(q_ref[...], gq_ref[...], t0, SLOPES_A)

    kc = kc_ref[0]
    vc = vc_ref[0]
    tq_col = t0 + lax.broadcasted_iota(jnp.int32, (TQ, nc), 0)
    n_idx = lax.broadcasted_iota(jnp.int32, (TQ, nc), 1)
    vis = (tq_col >= n_idx * CMP_STRIDE + (CMP_LEN - 1)) & (n_idx < nc - 1)
    visf = vis.astype(F32)
    imp = jnp.zeros((TQ, nc), F32)
    for h in range(N_HEADS):
        sc = lax.dot_general(qh[h], kc, _NT, preferred_element_type=F32)
        sc = jnp.where(vis, sc, NEG_INF)
        e = jnp.exp(sc - jnp.max(sc, axis=-1, keepdims=True))
        p = e / jnp.sum(e, axis=-1, keepdims=True) * visf
        ocmp_ref[:, h * HEAD_DIM:(h + 1) * HEAD_DIM] = jnp.dot(p.astype(BF16), vc, preferred_element_type=F32)
        imp = imp + p

    imp_s[8:8 + nc, :] = imp.T
    r = [imp_s[pl.ds(8 + k, nsel, stride=4), :] for k in range(4)]
    rm1 = imp_s[pl.ds(7, nsel, stride=4), :]
    imp_blk = rm1 + 2.0 * (r[0] + r[1] + r[2]) + r[3]
    blk = lax.broadcasted_iota(jnp.int32, (nsel, TQ), 0)
    tl = t0 + lax.broadcasted_iota(jnp.int32, (nsel, TQ), 1)
    cur = tl >> 6
    forced = (blk == 0) | (blk == cur) | (blk == cur - 1)
    valid = blk * SEL_BLOCK <= tl
    score = jnp.where(forced, FORCE_SCORE, jnp.where(valid, imp_blk, -FORCE_SCORE))
    rank = jnp.zeros((nsel, TQ), jnp.int32)
    for j in range(nsel):
        row = score[j:j + 1, :]
        beats = (row > score) | ((row == score) & (blk > j))
        rank = rank + beats.astype(jnp.int32)
    sel_s[...] = jnp.where(rank < min(N_SELECT, nsel), 0.0, NEG_INF)

    n_tiles = qi // (TK // TQ) + 1

    def fill(b_, carry):
        off = pl.multiple_of(b_ * SEL_BLOCK, SEL_BLOCK)
        bias_s[pl.ds(off, SEL_BLOCK), :] = jnp.broadcast_to(sel_s[pl.ds(b_, 1), :], (SEL_BLOCK, TQ))
        return carry

    lax.fori_loop(0, n_tiles * (TK // SEL_BLOCK), fill, 0)
    last = pl.multiple_of((n_tiles - 1) * TK, TK)
    kpos = last + lax.broadcasted_iota(jnp.int32, (TK, TQ), 0)
    tq = t0 + lax.broadcasted_iota(jnp.int32, (TK, TQ), 1)
    bias_s[pl.ds(last, TK), :] = jnp.where(kpos <= tq, bias_s[pl.ds(last, TK), :], NEG_INF)

    for h in range(N_HEADS):
        qt = qh[h]

        def body(kt, carry):
            off = pl.multiple_of(kt * TK, TK)
            return _flash_step(ksp_s[pl.ds(off, TK), :], qt, vst_s[kt], bias_s[pl.ds(off, TK), :], carry)

        _, l, acc = lax.fori_loop(0, n_tiles, body, _flash_init())
        ot_s[h * HEAD_DIM:(h + 1) * HEAD_DIM, :] = acc / l
    oslc_ref[...] = ot_s[...].T


def _nsa(p2, kc, vc, gq, gks, batch, seq):
    nq = seq // TQ
    nc = seq // CMP_STRIDE
    t = batch * seq
    return pl.pallas_call(
        functools.partial(_nsa_kernel, seq=seq),
        grid=(batch, nq),
        in_specs=[pl.BlockSpec((TQ, GROUP_W), lambda b, i: (b * nq + i, COL_QA // GROUP_W)),
                  pl.BlockSpec((1, nc, LANES), lambda b, i: (b, 0, 0)),
                  pl.BlockSpec((1, nc, HEAD_DIM), lambda b, i: (b, 0, 0)),
                  pl.BlockSpec((seq, LANES), lambda b, i: (b, COL_KSV // LANES)),
                  pl.BlockSpec((1, HEAD_DIM), lambda b, i: (0, 0)),
                  pl.BlockSpec((1, HEAD_DIM), lambda b, i: (0, 0))],
        out_specs=[pl.BlockSpec((TQ, GROUP_W), lambda b, i: (b * nq + i, 0)),
                   pl.BlockSpec((TQ, GROUP_W), lambda b, i: (b * nq + i, 0))],
        out_shape=[jax.ShapeDtypeStruct((t, GROUP_W), F32), jax.ShapeDtypeStruct((t, GROUP_W), F32)],
        scratch_shapes=[pltpu.VMEM((seq, LANES), BF16),
                        pltpu.VMEM((seq // TK, HEAD_DIM, TK), BF16),
                        pltpu.VMEM((8 + nc, TQ), F32),
                        pltpu.VMEM((seq // SEL_BLOCK, TQ), F32),
                        pltpu.VMEM((seq, TQ), F32),
                        pltpu.VMEM((GROUP_W, TQ), F32)],
        compiler_params=pltpu.CompilerParams(dimension_semantics=("parallel", "arbitrary"),
                                             vmem_limit_bytes=VMEM_LIMIT),
        name="nsa_cmp_slc",
    )(p2, kc, vc, p2, gq, gks)


def _band_bias_tiles(window, mult_fn):
    nd = (window + TK - 1) // TQ + 2
    kk = np.arange(TK)[:, None]
    qq = np.arange(TQ)[None, :]
    tiles = np.empty((nd, TK, TQ), np.float32)
    for di in range(nd):
        d = (di - 1) * TQ + qq - kk
        mult = mult_fn(d)
        tiles[di] = np.where(mult > 0, np.log(np.maximum(mult, 1)), NEG_INF)
    return tiles


def _window_mult(d):
    return ((d >= 0) & (d <= NSA_WINDOW - 1)).astype(np.float64)


def _dilated_mult(d):
    m = np.zeros(d.shape, np.float64)
    for window, dil in DILATED_CONFIGS:
        m += ((d >= 0) & (d <= window) & (d % dil == 0)).astype(np.float64)
    return m


def _banded_kernel(*refs, seq, window, slopes, shared_kv):
    if shared_kv:
        q_ref, kv_ref, gq_ref, gk_ref, bias_ref, o_ref, kp_s, vt_s, ot_s = refs
    else:
        q_ref, k_ref, v_ref, gq_ref, gk_ref, bias_ref, o_ref, kp_s, vt_s, ot_s = refs
    qi = pl.program_id(1)
    nkt = seq // TK

    @pl.when(qi == 0)
    def _prep():
        for c in range(nkt):
            rows = slice(c * TK, (c + 1) * TK)
            aug = _key_aug(c * TK + lax.broadcasted_iota(jnp.int32, (TK, 1), 0))
            if shared_kv:
                blk = kv_ref[rows, :]
                kn = _rms(blk[:, :HEAD_DIM], gk_ref[...])
                kp_s[rows, :] = jnp.concatenate([kn, aug], axis=1).astype(BF16)
                vt_s[c] = blk.T[HEAD_DIM:, :].astype(BF16)
            else:
                kb = k_ref[rows, :]
                parts = []
                for h in range(N_HEADS):
                    parts += [_rms(kb[:, h * HEAD_DIM:(h + 1) * HEAD_DIM], gk_ref[...]), aug]
                kp_s[rows, :] = jnp.concatenate(parts, axis=1).astype(BF16)
                vt_s[c] = v_ref[rows, :].T.astype(BF16)

    t0 = qi * TQ
    qh = _prep_queries(q_ref[...], gq_ref[...], t0, slopes)
    kt_lo = jnp.maximum(t0 - window, 0) // TK
    kt_hi = qi // (TK // TQ) + 1
    for h in range(N_HEADS):
        qt = qh[h]
        hk = 0 if shared_kv else h

        def body(kt, carry):
            off = pl.multiple_of(kt * TK, TK)
            kp = kp_s[pl.ds(off, TK), hk * LANES:(hk + 1) * LANES]
            vt = vt_s[kt, hk * HEAD_DIM:(hk + 1) * HEAD_DIM, :]
            return _flash_step(kp, qt, vt, bias_ref[qi - kt * (TK // TQ) + 1], carry)

        _, l, acc = lax.fori_loop(kt_lo, kt_hi, body, _flash_init())
        ot_s[h * HEAD_DIM:(h + 1) * HEAD_DIM, :] = acc / l
    o_ref[...] = ot_s[...].T


def _banded(p2, gq, gk, bias_tiles, batch, seq, *, window, slopes, shared_kv, col_q, col_k, col_v, name):
    nq = seq // TQ
    t = batch * seq
    nd = bias_tiles.shape[0]
    n_kv = 1 if shared_kv else N_HEADS
    q_spec = pl.BlockSpec((TQ, GROUP_W), lambda b, i: (b * nq + i, col_q // GROUP_W))
    small = pl.BlockSpec((1, HEAD_DIM), lambda b, i: (0, 0))
    bias_spec = pl.BlockSpec((nd, TK, TQ), lambda b, i: (0, 0, 0))
    if shared_kv:
        kv_specs = [pl.BlockSpec((seq, LANES), lambda b, i: (b, col_k // LANES))]
        operands = (p2, p2, gq, gk, bias_tiles)
    else:
        kv_specs = [pl.BlockSpec((seq, GROUP_W), lambda b, i: (b, col_k // GROUP_W)),
                    pl.BlockSpec((seq, GROUP_W), lambda b, i: (b, col_v // GROUP_W))]
        operands = (p2, p2, p2, gq, gk, bias_tiles)
    return pl.pallas_call(
        functools.partial(_banded_kernel, seq=seq, window=window, slopes=slopes, shared_kv=shared_kv),
        grid=(batch, nq),
        in_specs=[q_spec] + kv_specs + [small, small, bias_spec],
        out_specs=pl.BlockSpec((TQ, GROUP_W), lambda b, i: (b * nq + i, 0)),
        out_shape=jax.ShapeDtypeStruct((t, GROUP_W), F32),
        scratch_shapes=[pltpu.VMEM((seq, n_kv * LANES), BF16),
                        pltpu.VMEM((seq // TK, n_kv * HEAD_DIM, TK), BF16),
                        pltpu.VMEM((GROUP_W, TQ), F32)],
        compiler_params=pltpu.CompilerParams(dimension_semantics=("parallel", "arbitrary"),
                                             vmem_limit_bytes=VMEM_LIMIT),
        name=name,
    )(*operands)


def _stick_kernel(q_ref, k_ref, v_ref, o_ref, kp_s, vt_s, ot_s, *, seq):
    qi = pl.program_id(1)
    nkt = seq // TK

    @pl.when(qi == 0)
    def _prep():
        zeros = jnp.zeros((TK, HEAD_DIM), F32)
        for c in range(nkt):
            rows = slice(c * TK, (c + 1) * TK)
            kb = k_ref[rows, :]
            parts = []
            for h in range(N_HEADS):
                parts += [kb[:, h * HEAD_DIM:(h + 1) * HEAD_DIM], zeros]
            kp_s[rows, :] = jnp.concatenate(parts, axis=1).astype(BF16)
            vt_s[c] = v_ref[rows, :].T.astype(BF16)

    t0 = qi * TQ
    qh = _prep_queries(q_ref[...], None, t0, None)
    upper = (lax.broadcasted_iota(jnp.int32, (TK, TK), 1) > lax.broadcasted_iota(jnp.int32, (TK, TK), 0)).astype(BF16)
    last = qi // (TK // TQ)
    kpos = last * TK + lax.broadcasted_iota(jnp.int32, (TK, TQ), 0)
    past = kpos < t0 + lax.broadcasted_iota(jnp.int32, (TK, TQ), 1)

    for h in range(N_HEADS):
        qt = qh[h]

        def step(kt, carry, masked):
            later, acc = carry
            off = pl.multiple_of(kt * TK, TK)
            z = lax.dot_general(kp_s[pl.ds(off, TK), h * LANES:(h + 1) * LANES], qt, _NT,
                                preferred_element_type=F32)
            sp = jnp.maximum(z, 0.0) + jnp.log1p(jnp.exp(-jnp.abs(z)))
            spm = jnp.where(past, sp, 0.0) if masked else sp
            hi = spm.astype(BF16)
            lo = (spm - hi.astype(F32)).astype(BF16)
            within = (jnp.dot(upper, hi, preferred_element_type=F32)
                      + jnp.dot(upper, lo, preferred_element_type=F32))
            attn = jnp.exp(z - sp - within - later)
            if masked:
                attn = jnp.where(past, attn, 0.0)
            acc = acc + jnp.dot(vt_s[kt, h * HEAD_DIM:(h + 1) * HEAD_DIM, :], attn.astype(BF16),
                                preferred_element_type=F32)
            return later + jnp.sum(spm, axis=0, keepdims=True), acc

        carry = step(last, (jnp.zeros((1, TQ), F32), jnp.zeros((HEAD_DIM, TQ), F32)), True)
        _, acc = lax.fori_loop(0, last, lambda i, c: step(last - 1 - i, c, False), carry)
        ot_s[h * HEAD_DIM:(h + 1) * HEAD_DIM, :] = acc
    o_ref[...] = ot_s[...].T


def _stick(p2, batch, seq):
    nq = seq // TQ
    t = batch * seq
    return pl.pallas_call(
        functools.partial(_stick_kernel, seq=seq),
        grid=(batch, nq),
        in_specs=[pl.BlockSpec((TQ, GROUP_W), lambda b, i: (b * nq + i, COL_QC // GROUP_W)),
                  pl.BlockSpec((seq, GROUP_W), lambda b, i: (b, COL_KC // GROUP_W)),
                  pl.BlockSpec((seq, GROUP_W), lambda b, i: (b, COL_VC // GROUP_W))],
        out_specs=pl.BlockSpec((TQ, GROUP_W), lambda b, i: (b * nq + i, 0)),
        out_shape=jax.ShapeDtypeStruct((t, GROUP_W), F32),
        scratch_shapes=[pltpu.VMEM((seq, N_HEADS * LANES), BF16),
                        pltpu.VMEM((seq // TK, GROUP_W, TK), BF16),
                        pltpu.VMEM((GROUP_W, TQ), F32)],
        compiler_params=pltpu.CompilerParams(dimension_semantics=("parallel", "arbitrary"),
                                             vmem_limit_bytes=VMEM_LIMIT),
        name="stick_breaking",
    )(p2, p2, p2)


def _mixout_kernel(x_ref, ocmp_ref, oslc_ref, owin_ref, gate_ref, cvb_ref, cvc_ref, cvu_ref, pc_ref, pu_ref,
                   oc_ref, od_ref, bg_ref, cw_ref, gout_ref, wout_ref, o_ref, *, tiles_per_seq):
    i = pl.program_id(0)
    tm = x_ref.shape[0]
    gates = jax.nn.sigmoid(gate_ref[...] + bg_ref[...])
    ocmp, oslc, owin = ocmp_ref[...], oslc_ref[...], owin_ref[...]
    groups = []
    for h in range(N_HEADS):
        cols = slice(h * HEAD_DIM, (h + 1) * HEAD_DIM)
        groups.append(gates[:, 3 * h:3 * h + 1] * ocmp[:, cols]
                      + gates[:, 3 * h + 1:3 * h + 2] * oslc[:, cols]
                      + gates[:, 3 * h + 2:3 * h + 3] * owin[:, cols])
    cu = cvc_ref[...] * cvu_ref[...]
    prev = jnp.where(i % tiles_per_seq == 0, 0.0, pc_ref[...] * pu_ref[...])
    full = jnp.concatenate([prev, cu], axis=0)
    back1 = pltpu.roll(full, 1, 0)[8:, :]
    back2 = pltpu.roll(full, 2, 0)[8:, :]
    cw = cw_ref[...]
    ob = cvb_ref[...] * (cw[0:1, :] * back2 + cw[1:2, :] * back1 + cw[2:3, :] * cu)
    oc, od = oc_ref[...], od_ref[...]
    for src in (ob, oc, od):
        for h in range(N_HEADS):
            groups.append(src[:, h * HEAD_DIM:(h + 1) * HEAD_DIM])
    gout = gout_ref[...]
    normed = [_rms(g, gout[:, k * HEAD_DIM:(k + 1) * HEAD_DIM]) for k, g in enumerate(groups)]
    mixed = jnp.concatenate(normed, axis=1).astype(BF16)
    o_ref[...] = x_ref[...] + jnp.dot(mixed, wout_ref[...], preferred_element_type=F32)


def _mixout(x2, p2, ocmp, oslc, owin, oc, od, bg, cw, gout, wout, seq):
    t, d = x2.shape
    tm = 256
    rows = lambda w, col: pl.BlockSpec((tm, w), lambda i: (i, col // w))
    prev8 = lambda col: pl.BlockSpec((8, GROUP_W), lambda i: (jnp.maximum(i * (tm // 8) - 1, 0), col // GROUP_W))
    const = lambda shape: pl.BlockSpec(shape, lambda i: (0, 0))
    return pl.pallas_call(
        functools.partial(_mixout_kernel, tiles_per_seq=seq // tm),
        grid=(t // tm,),
        in_specs=[rows(d, 0), rows(GROUP_W, 0), rows(GROUP_W, 0), rows(GROUP_W, 0),
                  rows(LANES, COL_GATE), rows(GROUP_W, COL_CVB), rows(GROUP_W, COL_CVC), rows(GROUP_W, COL_CVU),
                  prev8(COL_CVC), prev8(COL_CVU),
                  rows(GROUP_W, 0), rows(GROUP_W, 0),
                  const((1, LANES)), const((8, GROUP_W)), const((1, d)), const((d, d))],
        out_specs=rows(d, 0),
        out_shape=jax.ShapeDtypeStruct((t, d), F32),
        compiler_params=pltpu.CompilerParams(dimension_semantics=("parallel",), vmem_limit_bytes=VMEM_LIMIT),
        name="mixout",
    )(x2, ocmp, oslc, owin, p2, p2, p2, p2, p2, p2, oc, od, bg, cw, gout, wout)


def _ffn_kernel(x_ref, g_ref, wg_ref, wu_ref, wd_ref, o_ref, h_s):
    j = pl.program_id(1)

    @pl.when(j == 0)
    def _start():
        x = x_ref[...]
        h_s[...] = _rms(x, g_ref[...]).astype(BF16)
        o_ref[...] = x

    h = h_s[...]
    a = jnp.dot(h, wg_ref[...], preferred_element_type=F32)
    u = jnp.dot(h, wu_ref[...], preferred_element_type=F32)
    act = (a * jax.nn.sigmoid(a) * u).astype(BF16)
    o_ref[...] += jnp.dot(act, wd_ref[...], preferred_element_type=F32)


def _ffn(x2, g, wg, wu, wd):
    t, d = x2.shape
    dff = wg.shape[1]
    tm, tf = 1024, 256
    return pl.pallas_call(
        _ffn_kernel,
        grid=(t // tm, dff // tf),
        in_specs=[pl.BlockSpec((tm, d), lambda i, j: (i, 0)),
                  pl.BlockSpec((1, d), lambda i, j: (0, 0)),
                  pl.BlockSpec((d, tf), lambda i, j: (0, j)),
                  pl.BlockSpec((d, tf), lambda i, j: (0, j)),
                  pl.BlockSpec((tf, d), lambda i, j: (j, 0))],
        out_specs=pl.BlockSpec((tm, d), lambda i, j: (i, 0)),
        out_shape=jax.ShapeDtypeStruct((t, d), F32),
        scratch_shapes=[pltpu.VMEM((tm, d), BF16)],
        compiler_params=pltpu.CompilerParams(dimension_semantics=("parallel", "arbitrary"),
                                             vmem_limit_bytes=VMEM_LIMIT),
        name="ffn",
    )(x2, g, wg, wu, wd)


def _permute_w_in(w_in):
    d = w_in.shape[0]
    a = w_in[:, :COL_GATE]
    gate = w_in[:, COL_GATE:COL_GATE + N_GATES]
    rest = w_in[:, COL_GATE + N_GATES:]
    pad = jnp.zeros((d, LANES - N_GATES), w_in.dtype)
    return jnp.concatenate([a, gate, pad, rest], axis=1).astype(BF16)


def _layer(x2, batch, seq, g_mix, w_in, b_gate, g_q_nsa, g_k_cmp, g_k_slc, g_k_win, pe_k_cmp, pe_v_cmp,
           w1_k_cmp, w2_k_cmp, w1_v_cmp, w2_v_cmp, conv_w, g_q_dil, g_k_dil, g_out, w_out,
           g_ffn, w_gate, w_up, w_down):
    d = x2.shape[1]
    row = lambda v: v.reshape(1, -1)
    p2 = _inproj(x2, row(g_mix), _permute_w_in(w_in))

    nc = seq // CMP_STRIDE
    cw = CMP_STRIDE * HEAD_DIM
    ck = p2[:, COL_KCA:COL_KCA + HEAD_DIM].reshape(batch, nc, cw)
    cv = p2[:, COL_VCA:COL_VCA + HEAD_DIM].reshape(batch, nc, cw)
    kc, vc = _compress(ck, cv, pe_k_cmp.reshape(2, cw), pe_v_cmp.reshape(2, cw),
                       w1_k_cmp.astype(BF16), w2_k_cmp.astype(BF16), w1_v_cmp.astype(BF16), w2_v_cmp.astype(BF16),
                       row(g_k_cmp))
    ocmp, oslc = _nsa(p2, kc, vc, row(g_q_nsa), row(g_k_slc), batch, seq)
    owin = _banded(p2, row(g_q_nsa), row(g_k_win), jnp.asarray(_band_bias_tiles(NSA_WINDOW - 1, _window_mult)),
                   batch, seq, window=NSA_WINDOW - 1, slopes=SLOPES_A, shared_kv=True,
                   col_q=COL_QA, col_k=COL_KWV, col_v=COL_KWV, name="nsa_window")
    od = _banded(p2, row(g_q_dil), row(g_k_dil),
                 jnp.asarray(_band_bias_tiles(DILATED_CONFIGS[-1][0], _dilated_mult)),
                 batch, seq, window=DILATED_CONFIGS[-1][0], slopes=SLOPES_D, shared_kv=False,
                 col_q=COL_QD, col_k=COL_KD, col_v=COL_VD, name="dilated")
    oc = _stick(p2, batch, seq)

    bg = jnp.zeros((1, LANES), F32).at[0, :N_GATES].set(b_gate)
    cwp = jnp.zeros((8, GROUP_W), F32).at[:CONV_K].set(conv_w)
    x1 = _mixout(x2, p2, ocmp, oslc, owin, oc, od, bg, cwp, row(g_out), w_out.astype(BF16), seq)
    return _ffn(x1, row(g_ffn), w_gate.astype(BF16), w_up.astype(BF16), w_down.astype(BF16))


def kernel(x, g_mix, w_in, b_gate, g_q_nsa, g_k_cmp, g_k_slc, g_k_win, pe_k_cmp, pe_v_cmp, w1_k_cmp, w2_k_cmp,
           w1_v_cmp, w2_v_cmp, conv_w, g_q_dil, g_k_dil, g_out, w_out, g_ffn, w_gate, w_up, w_down):
    batch, seq, d = x.shape
    assert seq % (CMP_STRIDE * LANES) == 0 and d % LANES == 0
    x2 = x.reshape(batch * seq, d)
    params = (g_mix, w_in, b_gate, g_q_nsa, g_k_cmp, g_k_slc, g_k_win, pe_k_cmp, pe_v_cmp, w1_k_cmp, w2_k_cmp,
              w1_v_cmp, w2_v_cmp, conv_w, g_q_dil, g_k_dil, g_out, w_out, g_ffn, w_gate, w_up, w_down)
    for layer in range(g_mix.shape[0]):
        x2 = _layer(x2, batch, seq, *[p[layer] for p in params])
    return x2.reshape(batch, seq, d)
```

```python
import functools

import numpy as np
import jax
import jax.numpy as jnp
from jax import lax
from jax.experimental import pallas as pl
from jax.experimental.pallas import tpu as pltpu

F32 = jnp.float32
BF16 = jnp.bfloat16

HEAD_DIM = 64
N_HEADS = 4
GROUP_W = N_HEADS * HEAD_DIM
CONV_K = 3
CMP_LEN = 32
CMP_STRIDE = 16
SEL_BLOCK = 64
N_SELECT = 16
NSA_WINDOW = 512
DILATED_CONFIGS = ((128, 1), (512, 4), (2048, 16))
NEG_INF = -1e30
FORCE_SCORE = 1e6
RMS_EPS = 1e-6

TQ = 256
TK = 256
LANES = 128
VMEM_LIMIT = 52 * 1024 * 1024

COL_QA, COL_KVC, COL_KSV, COL_KWV, COL_GATE = 0, 256, 384, 512, 640
COL_CVB, COL_CVC, COL_CVU = 768, 1024, 1280
COL_QC, COL_KC, COL_VC = 1536, 1792, 2048
COL_QD, COL_KD, COL_VD = 2304, 2560, 2816
P_COLS = 3072
N_GATES = 12

_NT = (((1,), (1,)), ((), ()))


def _alibi_slopes():
    s = [2.0 ** (-8.0 * i / 8) for i in range(1, 9)]
    return tuple(s[0::2]), tuple(s[1::2])


SLOPES_A, SLOPES_D = _alibi_slopes()


def _rms(x, g):
    return x * lax.rsqrt(jnp.mean(x * x, axis=-1, keepdims=True) + RMS_EPS) * g


def _key_aug(pos):
    rows = pos.shape[0]
    lane = lax.broadcasted_iota(jnp.int32, (rows, HEAD_DIM), 1)
    hi = (pos >> 6).astype(F32)
    lo = (pos & 63).astype(F32)
    return jnp.where(lane == 0, hi, jnp.where(lane == 1, lo, jnp.where(lane < 4, 1.0, 0.0)))


def _query_aug(t, slope):
    rows = t.shape[0]
    lane = lax.broadcasted_iota(jnp.int32, (rows, HEAD_DIM), 1)
    hi = (t >> 6).astype(F32) * (-64.0 * slope)
    lo = (t & 63).astype(F32) * (-slope)
    return jnp.where(lane == 0, 64.0 * slope,
                     jnp.where(lane == 1, slope, jnp.where(lane == 2, hi, jnp.where(lane == 3, lo, 0.0))))


def _prep_queries(q, g, t0, slopes, qp_s):
    tpos = t0 + lax.broadcasted_iota(jnp.int32, (q.shape[0], 1), 0)
    for h in range(N_HEADS):
        qh = q[:, h * HEAD_DIM:(h + 1) * HEAD_DIM]
        if g is not None:
            qh = _rms(qh, g)
        qh = qh * (HEAD_DIM ** -0.5)
        aug = jnp.zeros_like(qh) if slopes is None else _query_aug(tpos, slopes[h])
        qp_s[h] = jnp.concatenate([qh, aug], axis=1).astype(BF16)


def _inproj_kernel(x_ref, g_ref, w_ref, o_ref, *, nchunk):
    h = _rms(x_ref[...], g_ref[...]).astype(BF16)
    cw = P_COLS // nchunk
    for c in range(nchunk):
        o_ref[:, c * cw:(c + 1) * cw] = jnp.dot(h, w_ref[:, c * cw:(c + 1) * cw], preferred_element_type=F32)


def _inproj(x2, g, w):
    t, d = x2.shape
    tm = 512
    return pl.pallas_call(
        functools.partial(_inproj_kernel, nchunk=6),
        grid=(t // tm,),
        in_specs=[pl.BlockSpec((tm, d), lambda i: (i, 0)),
                  pl.BlockSpec((1, d), lambda i: (0, 0)),
                  pl.BlockSpec((d, P_COLS), lambda i: (0, 0))],
        out_specs=pl.BlockSpec((tm, P_COLS), lambda i: (i, 0)),
        out_shape=jax.ShapeDtypeStruct((t, P_COLS), F32),
        compiler_params=pltpu.CompilerParams(dimension_semantics=("parallel",), vmem_limit_bytes=VMEM_LIMIT),
        name="inproj",
    )(x2, g, w)


def _gelu_tanh(x):
    return x * (0.5 * (1.0 + jnp.tanh(np.sqrt(2.0 / np.pi).astype(np.float32) * (x + 0.044715 * (x * x * x)))))


def _compress_kernel(kv_ref, pe_ref, w1_ref, w2_ref, gk_ref, kc_ref, vc_ref):
    nc = kv_ref.shape[0] // CMP_STRIDE
    first = None
    second = None
    for j in range(CMP_STRIDE):
        tok = kv_ref[pl.ds(j, nc, stride=CMP_STRIDE), :]
        a = jnp.dot((tok + pe_ref[j:j + 1, :]).astype(BF16), w1_ref[j], preferred_element_type=F32)
        b = jnp.dot((tok + pe_ref[CMP_STRIDE + j:CMP_STRIDE + j + 1, :]).astype(BF16), w1_ref[CMP_STRIDE + j],
                    preferred_element_type=F32)
        first = a if first is None else first + a
        second = b if second is None else second + b
    hid = first + pltpu.roll(second, nc - 1, 0)
    out = jnp.dot(_gelu_tanh(hid).astype(BF16), w2_ref[...], preferred_element_type=F32)
    kc = _rms(out[:, :HEAD_DIM], gk_ref[...])
    n = lax.broadcasted_iota(jnp.int32, (nc, 1), 0)
    kc_ref[0] = jnp.concatenate([kc, _key_aug(n * CMP_STRIDE + (CMP_LEN - 1))], axis=1).astype(BF16)
    vc_ref[0] = out[:, HEAD_DIM:].astype(BF16)


def _compress(p2, pe, w1, w2, gk, batch, seq):
    nc = seq // CMP_STRIDE
    hid2 = w1.shape[2]
    return pl.pallas_call(
        _compress_kernel,
        grid=(batch,),
        in_specs=[pl.BlockSpec((seq, LANES), lambda i: (i, COL_KVC // LANES)),
                  pl.BlockSpec((CMP_LEN, LANES), lambda i: (0, 0)),
                  pl.BlockSpec((CMP_LEN, LANES, hid2), lambda i: (0, 0, 0)),
                  pl.BlockSpec((hid2, LANES), lambda i: (0, 0)),
                  pl.BlockSpec((1, HEAD_DIM), lambda i: (0, 0))],
        out_specs=[pl.BlockSpec((1, nc, LANES), lambda i: (i, 0, 0)),
                   pl.BlockSpec((1, nc, HEAD_DIM), lambda i: (i, 0, 0))],
        out_shape=[jax.ShapeDtypeStruct((batch, nc, LANES), BF16),
                   jax.ShapeDtypeStruct((batch, nc, HEAD_DIM), BF16)],
        compiler_params=pltpu.CompilerParams(dimension_semantics=("parallel",), vmem_limit_bytes=VMEM_LIMIT),
        name="nsa_compress",
    )(p2, pe, w1, w2, gk)


def _compress_weights(pe_k, pe_v, w1_k, w2_k, w1_v, w2_v):
    hid = w1_k.shape[1]
    w1k = w1_k.reshape(CMP_LEN, HEAD_DIM, hid)
    w1v = w1_v.reshape(CMP_LEN, HEAD_DIM, hid)
    z1 = jnp.zeros_like(w1k)
    w1 = jnp.concatenate([jnp.concatenate([w1k, z1], axis=2), jnp.concatenate([z1, w1v], axis=2)], axis=1)
    z2 = jnp.zeros_like(w2_k)
    w2 = jnp.concatenate([jnp.concatenate([w2_k, z2], axis=1), jnp.concatenate([z2, w2_v], axis=1)], axis=0)
    pe = jnp.concatenate([pe_k, pe_v], axis=1)
    return pe, w1.astype(BF16), w2.astype(BF16)


def _flash_heads_step(kp_of, vt_of, bias, qp_s, s_s, acc_s, carry):
    ms, ls = carry
    new_m, new_l = [], []
    for h in range(N_HEADS):
        s = lax.dot_general(kp_of(h), qp_s[h], _NT, preferred_element_type=F32) + bias
        s_s[h] = s
        new_m.append(jnp.maximum(ms[h], jnp.max(s, axis=0, keepdims=True)))
    for h in range(N_HEADS):
        alpha = jnp.exp(ms[h] - new_m[h])
        p = jnp.exp(s_s[h] - new_m[h])
        new_l.append(alpha * ls[h] + jnp.sum(p, axis=0, keepdims=True))
        acc_s[h] = alpha * acc_s[h] + jnp.dot(vt_of(h), p.astype(BF16), preferred_element_type=F32)
    return tuple(new_m), tuple(new_l)


def _flash_init(acc_s):
    acc_s[...] = jnp.zeros(acc_s.shape, F32)
    return (tuple(jnp.full((1, TQ), NEG_INF, F32) for _ in range(N_HEADS)),
            tuple(jnp.zeros((1, TQ), F32) for _ in range(N_HEADS)))


def _flash_finish(acc_s, ls, o_ref):
    out = jnp.concatenate([acc_s[h] * (1.0 / ls[h]) for h in range(N_HEADS)], axis=0)
    o_ref[...] = out.T


def _nsa_kernel(q_ref, kc_ref, vc_ref, ksv_ref, gq_ref, gks_ref, ocmp_ref, oslc_ref,
                ksp_s, vst_s, qp_s, imp_s, sel_s, s_s, acc_s, *, seq):
    qi = pl.program_id(1)
    nc = seq // CMP_STRIDE
    nsel = seq // SEL_BLOCK
    nkt = seq // TK

    @pl.when(qi == 0)
    def _prep():
        for c in range(nkt):
            rows = slice(c * TK, (c + 1) * TK)
            blk = ksv_ref[rows, :]
            pos = c * TK + lax.broadcasted_iota(jnp.int32, (TK, 1), 0)
            kn = _rms(blk[:, :HEAD_DIM], gks_ref[...])
            ksp_s[rows, :] = jnp.concatenate([kn, _key_aug(pos)], axis=1).astype(BF16)
            vst_s[c] = blk.T[HEAD_DIM:, :].astype(BF16)
        imp_s[:, 0:8, :] = jnp.zeros((TQ // LANES, 8, LANES), F32)

    t0 = qi * TQ
    _prep_queries(q_ref[...], gq_ref[...], t0, SLOPES_A, qp_s)

    kc = kc_ref[0]
    vc = vc_ref[0]
    tq_col = t0 + lax.broadcasted_iota(jnp.int32, (TQ, nc), 0)
    n_idx = lax.broadcasted_iota(jnp.int32, (TQ, nc), 1)
    vis = (tq_col >= n_idx * CMP_STRIDE + (CMP_LEN - 1)) & (n_idx < nc - 1)
    visf = vis.astype(F32)
    imp = jnp.zeros((TQ, nc), F32)
    for h in range(N_HEADS):
        sc = lax.dot_general(qp_s[h], kc, _NT, preferred_element_type=F32)
        sc = jnp.where(vis, sc, NEG_INF)
        e = jnp.exp(sc - jnp.max(sc, axis=-1, keepdims=True))
        p = e / jnp.sum(e, axis=-1, keepdims=True) * visf
        ocmp_ref[:, h * HEAD_DIM:(h + 1) * HEAD_DIM] = jnp.dot(p.astype(BF16), vc, preferred_element_type=F32)
        imp = imp + p

    imp_t = imp.T
    halves = []
    for half in range(TQ // LANES):
        imp_s[half, 8:8 + nc, :] = imp_t[:, half * LANES:(half + 1) * LANES]
        r = [imp_s[half, pl.ds(8 + k, nsel, stride=4), :] for k in range(4)]
        rm1 = imp_s[half, pl.ds(7, nsel, stride=4), :]
        halves.append(rm1 + 2.0 * (r[0] + r[1] + r[2]) + r[3])
    imp_blk = jnp.concatenate(halves, axis=1)
    blk = lax.broadcasted_iota(jnp.int32, (nsel, TQ), 0)
    tl = t0 + lax.broadcasted_iota(jnp.int32, (nsel, TQ), 1)
    cur = tl >> 6
    forced = (blk == 0) | (blk == cur) | (blk == cur - 1)
    valid = blk * SEL_BLOCK <= tl
    score = jnp.where(forced, FORCE_SCORE, jnp.where(valid, imp_blk, -FORCE_SCORE))
    rank = jnp.zeros((nsel, TQ), jnp.int32)
    for j in range(nsel):
        row = score[j:j + 1, :]
        beats = (row > score) | ((row == score) & (blk > j))
        rank = rank + beats.astype(jnp.int32)
    sel_s[...] = jnp.where(rank < min(N_SELECT, nsel), 0.0, NEG_INF)

    def block_bias(kt):
        per_tile = TK // SEL_BLOCK
        return jnp.concatenate([jnp.broadcast_to(sel_s[pl.ds(kt * per_tile + j, 1), :], (SEL_BLOCK, TQ))
                                for j in range(per_tile)], axis=0)

    def step(kt, bias, carry):
        off = pl.multiple_of(kt * TK, TK)
        kp = ksp_s[pl.ds(off, TK), :]
        return _flash_heads_step(lambda h: kp, lambda h: vst_s[kt], bias, qp_s, s_s, acc_s, carry)

    carry = lax.fori_loop(0, qi, lambda kt, c: step(kt, block_bias(kt), c), _flash_init(acc_s))
    causal = lax.broadcasted_iota(jnp.int32, (TK, TQ), 0) <= lax.broadcasted_iota(jnp.int32, (TK, TQ), 1)
    _, ls = step(qi, jnp.where(causal, block_bias(qi), NEG_INF), carry)
    _flash_finish(acc_s, ls, oslc_ref)


def _nsa(p2, kc, vc, gq, gks, batch, seq):
    nq = seq // TQ
    nc = seq // CMP_STRIDE
    t = batch * seq
    return pl.pallas_call(
        functools.partial(_nsa_kernel, seq=seq),
        grid=(batch, nq),
        in_specs=[pl.BlockSpec((TQ, GROUP_W), lambda b, i: (b * nq + i, COL_QA // GROUP_W)),
                  pl.BlockSpec((1, nc, LANES), lambda b, i: (b, 0, 0)),
                  pl.BlockSpec((1, nc, HEAD_DIM), lambda b, i: (b, 0, 0)),
                  pl.BlockSpec((seq, LANES), lambda b, i: (b, COL_KSV // LANES)),
                  pl.BlockSpec((1, HEAD_DIM), lambda b, i: (0, 0)),
                  pl.BlockSpec((1, HEAD_DIM), lambda b, i: (0, 0))],
        out_specs=[pl.BlockSpec((TQ, GROUP_W), lambda b, i: (b * nq + i, 0)),
                   pl.BlockSpec((TQ, GROUP_W), lambda b, i: (b * nq + i, 0))],
        out_shape=[jax.ShapeDtypeStruct((t, GROUP_W), F32), jax.ShapeDtypeStruct((t, GROUP_W), F32)],
        scratch_shapes=[pltpu.VMEM((seq, LANES), BF16),
                        pltpu.VMEM((seq // TK, HEAD_DIM, TK), BF16),
                        pltpu.VMEM((N_HEADS, TQ, LANES), BF16),
                        pltpu.VMEM((TQ // LANES, 8 + nc, LANES), F32),
                        pltpu.VMEM((seq // SEL_BLOCK, TQ), F32),
                        pltpu.VMEM((N_HEADS, TK, TQ), F32),
                        pltpu.VMEM((N_HEADS, HEAD_DIM, TQ), F32)],
        compiler_params=pltpu.CompilerParams(dimension_semantics=("parallel", "arbitrary"),
                                             vmem_limit_bytes=VMEM_LIMIT),
        name="nsa_cmp_slc",
    )(p2, kc, vc, p2, gq, gks)


def _band_bias_tiles(window, mult_fn):
    nd = (window + TK - 1) // TK + 1
    kk = np.arange(TK)[:, None]
    qq = np.arange(TQ)[None, :]
    tiles = np.empty((nd, TK, TQ), np.float32)
    for di in range(nd):
        d = di * TK + qq - kk
        mult = mult_fn(d)
        tiles[di] = np.where(mult > 0, np.log(np.maximum(mult, 1)), NEG_INF)
    return tiles


def _window_mult(d):
    return ((d >= 0) & (d <= NSA_WINDOW - 1)).astype(np.float64)


def _dilated_mult(d):
    m = np.zeros(d.shape, np.float64)
    for window, dil in DILATED_CONFIGS:
        m += ((d >= 0) & (d <= window) & (d % dil == 0)).astype(np.float64)
    return m


def _banded_kernel(*refs, seq, window, slopes, shared_kv):
    if shared_kv:
        q_ref, kv_ref, gq_ref, gk_ref, bias_ref, o_ref, kp_s, vt_s, qp_s, s_s, acc_s = refs
    else:
        q_ref, k_ref, v_ref, gq_ref, gk_ref, bias_ref, o_ref, kp_s, vt_s, qp_s, s_s, acc_s = refs
    qi = pl.program_id(1)
    nkt = seq // TK

    @pl.when(qi == 0)
    def _prep():
        for c in range(nkt):
            rows = slice(c * TK, (c + 1) * TK)
            aug = _key_aug(c * TK + lax.broadcasted_iota(jnp.int32, (TK, 1), 0))
            if shared_kv:
                blk = kv_ref[rows, :]
                kn = _rms(blk[:, :HEAD_DIM], gk_ref[...])
                kp_s[rows, :] = jnp.concatenate([kn, aug], axis=1).astype(BF16)
                vt_s[c] = blk.T[HEAD_DIM:, :].astype(BF16)
            else:
                kb = k_ref[rows, :]
                parts = []
                for h in range(N_HEADS):
                    parts += [_rms(kb[:, h * HEAD_DIM:(h + 1) * HEAD_DIM], gk_ref[...]), aug]
                kp_s[rows, :] = jnp.concatenate(parts, axis=1).astype(BF16)
                vt_s[c] = v_ref[rows, :].T.astype(BF16)

    t0 = qi * TQ
    _prep_queries(q_ref[...], gq_ref[...], t0, slopes, qp_s)
    kt_lo = jnp.maximum(t0 - window, 0) // TK

    def body(kt, carry):
        off = pl.multiple_of(kt * TK, TK)
        if shared_kv:
            kp = kp_s[pl.ds(off, TK), :]
            kp_of = lambda h: kp
            vt_of = lambda h: vt_s[kt]
        else:
            kp_of = lambda h: kp_s[pl.ds(off, TK), h * LANES:(h + 1) * LANES]
            vt_of = lambda h: vt_s[kt, h * HEAD_DIM:(h + 1) * HEAD_DIM, :]
        return _flash_heads_step(kp_of, vt_of, bias_ref[qi - kt], qp_s, s_s, acc_s, carry)

    _, ls = lax.fori_loop(kt_lo, qi + 1, body, _flash_init(acc_s))
    _flash_finish(acc_s, ls, o_ref)


def _banded(p2, gq, gk, bias_tiles, batch, seq, *, window, slopes, shared_kv, col_q, col_k, col_v, name):
    nq = seq // TQ
    t = batch * seq
    nd = bias_tiles.shape[0]
    n_kv = 1 if shared_kv else N_HEADS
    q_spec = pl.BlockSpec((TQ, GROUP_W), lambda b, i: (b * nq + i, col_q // GROUP_W))
    small = pl.BlockSpec((1, HEAD_DIM), lambda b, i: (0, 0))
    bias_spec = pl.BlockSpec((nd, TK, TQ), lambda b, i: (0, 0, 0))
    if shared_kv:
        kv_specs = [pl.BlockSpec((seq, LANES), lambda b, i: (b, col_k // LANES))]
        operands = (p2, p2, gq, gk, bias_tiles)
    else:
        kv_specs = [pl.BlockSpec((seq, GROUP_W), lambda b, i: (b, col_k // GROUP_W)),
                    pl.BlockSpec((seq, GROUP_W), lambda b, i: (b, col_v // GROUP_W))]
        operands = (p2, p2, p2, gq, gk, bias_tiles)
    return pl.pallas_call(
        functools.partial(_banded_kernel, seq=seq, window=window, slopes=slopes, shared_kv=shared_kv),
        grid=(batch, nq),
        in_specs=[q_spec] + kv_specs + [small, small, bias_spec],
        out_specs=pl.BlockSpec((TQ, GROUP_W), lambda b, i: (b * nq + i, 0)),
        out_shape=jax.ShapeDtypeStruct((t, GROUP_W), F32),
        scratch_shapes=[pltpu.VMEM((seq, n_kv * LANES), BF16),
                        pltpu.VMEM((seq // TK, n_kv * HEAD_DIM, TK), BF16),
                        pltpu.VMEM((N_HEADS, TQ, LANES), BF16),
                        pltpu.VMEM((N_HEADS, TK, TQ), F32),
                        pltpu.VMEM((N_HEADS, HEAD_DIM, TQ), F32)],
        compiler_params=pltpu.CompilerParams(dimension_semantics=("parallel", "arbitrary"),
                                             vmem_limit_bytes=VMEM_LIMIT),
        name=name,
    )(*operands)


def _stick_kernel(q_ref, k_ref, v_ref, o_ref, kp_s, vt_s, qp_s, hl_s, e_s, acc_s, *, seq):
    qi = pl.program_id(1)
    nkt = seq // TK

    @pl.when(qi == 0)
    def _prep():
        zeros = jnp.zeros((TK, HEAD_DIM), F32)
        for c in range(nkt):
            rows = slice(c * TK, (c + 1) * TK)
            kb = k_ref[rows, :]
            parts = []
            for h in range(N_HEADS):
                parts += [kb[:, h * HEAD_DIM:(h + 1) * HEAD_DIM], zeros]
            kp_s[rows, :] = jnp.concatenate(parts, axis=1).astype(BF16)
            vt_s[c] = v_ref[rows, :].T.astype(BF16)

    _prep_queries(q_ref[...], None, qi * TQ, None, qp_s)
    col = lax.broadcasted_iota(jnp.int32, (TK, 2 * TK), 1)
    upper2 = ((col & (TK - 1)) > lax.broadcasted_iota(jnp.int32, (TK, 2 * TK), 0)).astype(BF16)
    past = lax.broadcasted_iota(jnp.int32, (TK, TQ), 0) < lax.broadcasted_iota(jnp.int32, (TK, TQ), 1)
    acc_s[...] = jnp.zeros(acc_s.shape, F32)

    def step(kt, laters, masked):
        off = pl.multiple_of(kt * TK, TK)
        out = []
        for h in range(N_HEADS):
            z = lax.dot_general(kp_s[pl.ds(off, TK), h * LANES:(h + 1) * LANES], qp_s[h], _NT,
                                preferred_element_type=F32)
            sp = jnp.maximum(z, 0.0) + jnp.log(1.0 + jnp.exp(-jnp.abs(z)))
            spm = jnp.where(past, sp, 0.0) if masked else sp
            hi = spm.astype(BF16)
            hl_s[h, 0:TK, :] = hi
            hl_s[h, TK:2 * TK, :] = (spm - hi.astype(F32)).astype(BF16)
            e_s[h] = z - sp
            out.append(laters[h] + jnp.sum(spm, axis=0, keepdims=True))
        for h in range(N_HEADS):
            within = jnp.dot(upper2, hl_s[h], preferred_element_type=F32)
            attn = jnp.exp(e_s[h] - within - laters[h])
            if masked:
                attn = jnp.where(past, attn, 0.0)
            acc_s[h] += jnp.dot(vt_s[kt, h * HEAD_DIM:(h + 1) * HEAD_DIM, :], attn.astype(BF16),
                                preferred_element_type=F32)
        return tuple(out)

    laters = step(qi, tuple(jnp.zeros((1, TQ), F32) for _ in range(N_HEADS)), True)
    lax.fori_loop(0, qi, lambda i, c: step(qi - 1 - i, c, False), laters)
    o_ref[...] = jnp.concatenate([acc_s[h] for h in range(N_HEADS)], axis=0).T


def _stick(p2, batch, seq):
    nq = seq // TQ
    t = batch * seq
    return pl.pallas_call(
        functools.partial(_stick_kernel, seq=seq),
        grid=(batch, nq),
        in_specs=[pl.BlockSpec((TQ, GROUP_W), lambda b, i: (b * nq + i, COL_QC // GROUP_W)),
                  pl.BlockSpec((seq, GROUP_W), lambda b, i: (b, COL_KC // GROUP_W)),
                  pl.BlockSpec((seq, GROUP_W), lambda b, i: (b, COL_VC // GROUP_W))],
        out_specs=pl.BlockSpec((TQ, GROUP_W), lambda b, i: (b * nq + i, 0)),
        out_shape=jax.ShapeDtypeStruct((t, GROUP_W), F32),
        scratch_shapes=[pltpu.VMEM((seq, N_HEADS * LANES), BF16),
                        pltpu.VMEM((seq // TK, GROUP_W, TK), BF16),
                        pltpu.VMEM((N_HEADS, TQ, LANES), BF16),
                        pltpu.VMEM((N_HEADS, 2 * TK, TQ), BF16),
                        pltpu.VMEM((N_HEADS, TK, TQ), F32),
                        pltpu.VMEM((N_HEADS, HEAD_DIM, TQ), F32)],
        compiler_params=pltpu.CompilerParams(dimension_semantics=("parallel", "arbitrary"),
                                             vmem_limit_bytes=VMEM_LIMIT),
        name="stick_breaking",
    )(p2, p2, p2)


def _mixout_kernel(x_ref, ocmp_ref, oslc_ref, owin_ref, gate_ref, cvb_ref, cvc_ref, cvu_ref, pc_ref, pu_ref,
                   oc_ref, od_ref, bg_ref, cw_ref, gout_ref, wout_ref, o_ref, *, tiles_per_seq):
    i = pl.program_id(0)
    gates = jax.nn.sigmoid(gate_ref[...] + bg_ref[...])
    ocmp, oslc, owin = ocmp_ref[...], oslc_ref[...], owin_ref[...]
    groups = []
    for h in range(N_HEADS):
        cols = slice(h * HEAD_DIM, (h + 1) * HEAD_DIM)
        groups.append(gates[:, 3 * h:3 * h + 1] * ocmp[:, cols]
                      + gates[:, 3 * h + 1:3 * h + 2] * oslc[:, cols]
                      + gates[:, 3 * h + 2:3 * h + 3] * owin[:, cols])
    cu = cvc_ref[...] * cvu_ref[...]
    prev = jnp.where(i % tiles_per_seq == 0, 0.0, pc_ref[...] * pu_ref[...])
    full = jnp.concatenate([prev, cu], axis=0)
    back1 = pltpu.roll(full, 1, 0)[8:, :]
    back2 = pltpu.roll(full, 2, 0)[8:, :]
    cw = cw_ref[...]
    ob = cvb_ref[...] * (cw[0:1, :] * back2 + cw[1:2, :] * back1 + cw[2:3, :] * cu)
    oc, od = oc_ref[...], od_ref[...]
    for src in (ob, oc, od):
        for h in range(N_HEADS):
            groups.append(src[:, h * HEAD_DIM:(h + 1) * HEAD_DIM])
    gout = gout_ref[...]
    normed = [_rms(g, gout[:, k * HEAD_DIM:(k + 1) * HEAD_DIM]) for k, g in enumerate(groups)]
    mixed = jnp.concatenate(normed, axis=1).astype(BF16)
    o_ref[...] = x_ref[...] + jnp.dot(mixed, wout_ref[...], preferred_element_type=F32)


def _mixout(x2, p2, ocmp, oslc, owin, oc, od, bg, cw, gout, wout, seq):
    t, d = x2.shape
    tm = 256
    rows = lambda w, col: pl.BlockSpec((tm, w), lambda i: (i, col // w))
    prev8 = lambda col: pl.BlockSpec((8, GROUP_W), lambda i: (jnp.maximum(i * (tm // 8) - 1, 0), col // GROUP_W))
    const = lambda shape: pl.BlockSpec(shape, lambda i: (0, 0))
    return pl.pallas_call(
        functools.partial(_mixout_kernel, tiles_per_seq=seq // tm),
        grid=(t // tm,),
        in_specs=[rows(d, 0), rows(GROUP_W, 0), rows(GROUP_W, 0), rows(GROUP_W, 0),
                  rows(LANES, COL_GATE), rows(GROUP_W, COL_CVB), rows(GROUP_W, COL_CVC), rows(GROUP_W, COL_CVU),
                  prev8(COL_CVC), prev8(COL_CVU),
                  rows(GROUP_W, 0), rows(GROUP_W, 0),
                  const((1, LANES)), const((8, GROUP_W)), const((1, d)), const((d, d))],
        out_specs=rows(d, 0),
        out_shape=jax.ShapeDtypeStruct((t, d), F32),
        compiler_params=pltpu.CompilerParams(dimension_semantics=("parallel",), vmem_limit_bytes=VMEM_LIMIT),
        name="mixout",
    )(x2, ocmp, oslc, owin, p2, p2, p2, p2, p2, p2, oc, od, bg, cw, gout, wout)


def _ffn_kernel(x_ref, g_ref, wg_ref, wu_ref, wd_ref, o_ref, h_s):
    j = pl.program_id(1)

    @pl.when(j == 0)
    def _start():
        x = x_ref[...]
        h_s[...] = _rms(x, g_ref[...]).astype(BF16)
        o_ref[...] = x

    h = h_s[...]
    a = jnp.dot(h, wg_ref[...], preferred_element_type=F32)
    u = jnp.dot(h, wu_ref[...], preferred_element_type=F32)
    act = (a * jax.nn.sigmoid(a) * u).astype(BF16)
    o_ref[...] += jnp.dot(act, wd_ref[...], preferred_element_type=F32)


def _ffn(x2, g, wg, wu, wd):
    t, d = x2.shape
    dff = wg.shape[1]
    tm, tf = 1024, 256
    return pl.pallas_call(
        _ffn_kernel,
        grid=(t // tm, dff // tf),
        in_specs=[pl.BlockSpec((tm, d), lambda i, j: (i, 0)),
                  pl.BlockSpec((1, d), lambda i, j: (0, 0)),
                  pl.BlockSpec((d, tf), lambda i, j: (0, j)),
                  pl.BlockSpec((d, tf), lambda i, j: (0, j)),
                  pl.BlockSpec((tf, d), lambda i, j: (j, 0))],
        out_specs=pl.BlockSpec((tm, d), lambda i, j: (i, 0)),
        out_shape=jax.ShapeDtypeStruct((t, d), F32),
        scratch_shapes=[pltpu.VMEM((tm, d), BF16)],
        compiler_params=pltpu.CompilerParams(dimension_semantics=("parallel", "arbitrary"),
                                             vmem_limit_bytes=VMEM_LIMIT),
        name="ffn",
    )(x2, g, wg, wu, wd)


def _permute_w_in(w_in):
    d = w_in.shape[0]
    a = w_in[:, :COL_GATE]
    gate = w_in[:, COL_GATE:COL_GATE + N_GATES]
    rest = w_in[:, COL_GATE + N_GATES:]
    pad = jnp.zeros((d, LANES - N_GATES), w_in.dtype)
    return jnp.concatenate([a, gate, pad, rest], axis=1).astype(BF16)


def _layer(x2, batch, seq, g_mix, w_in, b_gate, g_q_nsa, g_k_cmp, g_k_slc, g_k_win, pe_k_cmp, pe_v_cmp,
           w1_k_cmp, w2_k_cmp, w1_v_cmp, w2_v_cmp, conv_w, g_q_dil, g_k_dil, g_out, w_out,
           g_ffn, w_gate, w_up, w_down):
    row = lambda v: v.reshape(1, -1)
    p2 = _inproj(x2, row(g_mix), _permute_w_in(w_in))

    pe, w1, w2 = _compress_weights(pe_k_cmp, pe_v_cmp, w1_k_cmp, w2_k_cmp, w1_v_cmp, w2_v_cmp)
    kc, vc = _compress(p2, pe, w1, w2, row(g_k_cmp), batch, seq)
    ocmp, oslc = _nsa(p2, kc, vc, row(g_q_nsa), row(g_k_slc), batch, seq)
    owin = _banded(p2, row(g_q_nsa), row(g_k_win), jnp.asarray(_band_bias_tiles(NSA_WINDOW - 1, _window_mult)),
                   batch, seq, window=NSA_WINDOW - 1, slopes=SLOPES_A, shared_kv=True,
                   col_q=COL_QA, col_k=COL_KWV, col_v=COL_KWV, name="nsa_window")
    od = _banded(p2, row(g_q_dil), row(g_k_dil),
                 jnp.asarray(_band_bias_tiles(DILATED_CONFIGS[-1][0], _dilated_mult)),
                 batch, seq, window=DILATED_CONFIGS[-1][0], slopes=SLOPES_D, shared_kv=False,
                 col_q=COL_QD, col_k=COL_KD, col_v=COL_VD, name="dilated")
    oc = _stick(p2, batch, seq)

    bg = jnp.zeros((1, LANES), F32).at[0, :N_GATES].set(b_gate)
    cwp = jnp.zeros((8, GROUP_W), F32).at[:CONV_K].set(conv_w)
    x1 = _mixout(x2, p2, ocmp, oslc, owin, oc, od, bg, cwp, row(g_out), w_out.astype(BF16), seq)
    return _ffn(x1, row(g_ffn), w_gate.astype(BF16), w_up.astype(BF16), w_down.astype(BF16))


def kernel(x, g_mix, w_in, b_gate, g_q_nsa, g_k_cmp, g_k_slc, g_k_win, pe_k_cmp, pe_v_cmp, w1_k_cmp, w2_k_cmp,
           w1_v_cmp, w2_v_cmp, conv_w, g_q_dil, g_k_dil, g_out, w_out, g_ffn, w_gate, w_up, w_down):
    batch, seq, d = x.shape
    assert seq % (CMP_STRIDE * LANES) == 0 and d % LANES == 0
    x2 = x.reshape(batch * seq, d)
    params = (g_mix, w_in, b_gate, g_q_nsa, g_k_cmp, g_k_slc, g_k_win, pe_k_cmp, pe_v_cmp, w1_k_cmp, w2_k_cmp,
              w1_v_cmp, w2_v_cmp, conv_w, g_q_dil, g_k_dil, g_out, w_out, g_ffn, w_gate, w_up, w_down)
    for layer in range(g_mix.shape[0]):
        x2 = _layer(x2, batch, seq, *[p[layer] for p in params])
    return x2.reshape(batch, seq, d)
```

```python
import functools

import numpy as np
import jax
import jax.numpy as jnp
from jax import lax
from jax.experimental import pallas as pl
from jax.experimental.pallas import tpu as pltpu

F32 = jnp.float32
BF16 = jnp.bfloat16

HEAD_DIM = 64
N_HEADS = 4
GROUP_W = N_HEADS * HEAD_DIM
CONV_K = 3
CMP_LEN = 32
CMP_STRIDE = 16
SEL_BLOCK = 64
N_SELECT = 16
NSA_WINDOW = 512
DILATED_CONFIGS = ((128, 1), (512, 4), (2048, 16))
NEG_INF = -1e30
FORCE_SCORE = 1e6
RMS_EPS = 1e-6

TQ = 256
TK = 256
LANES = 128
SUM_PAD = 16
LOG2E = 1.4426950408889634
VMEM_LIMIT = 52 * 1024 * 1024

COL_QA, COL_KVC, COL_KSV, COL_KWV, COL_GATE = 0, 256, 384, 512, 640
COL_CVB, COL_CVC, COL_CVU = 768, 1024, 1280
COL_QC, COL_KC, COL_VC = 1536, 1792, 2048
COL_QD, COL_KD, COL_VD = 2304, 2560, 2816
P_COLS = 3072
N_GATES = 12

_NT = (((1,), (1,)), ((), ()))


def _alibi_slopes():
    s = [2.0 ** (-8.0 * i / 8) for i in range(1, 9)]
    return tuple(s[0::2]), tuple(s[1::2])


SLOPES_A, SLOPES_D = _alibi_slopes()


def _rms(x, g):
    return x * lax.rsqrt(jnp.mean(x * x, axis=-1, keepdims=True) + RMS_EPS) * g


def _key_aug(pos):
    rows = pos.shape[0]
    lane = lax.broadcasted_iota(jnp.int32, (rows, HEAD_DIM), 1)
    hi = (pos >> 6).astype(F32)
    lo = (pos & 63).astype(F32)
    return jnp.where(lane == 0, hi, jnp.where(lane == 1, lo, jnp.where(lane < 4, 1.0, 0.0)))


def _query_aug(t, slope):
    rows = t.shape[0]
    lane = lax.broadcasted_iota(jnp.int32, (rows, HEAD_DIM), 1)
    hi = (t >> 6).astype(F32) * (-64.0 * slope)
    lo = (t & 63).astype(F32) * (-slope)
    return jnp.where(lane == 0, 64.0 * slope,
                     jnp.where(lane == 1, slope, jnp.where(lane == 2, hi, jnp.where(lane == 3, lo, 0.0))))


def _prep_queries(q, g, t0, slopes, qp_s):
    tpos = t0 + lax.broadcasted_iota(jnp.int32, (q.shape[0], 1), 0)
    for h in range(N_HEADS):
        qh = q[:, h * HEAD_DIM:(h + 1) * HEAD_DIM]
        if g is not None:
            qh = _rms(qh, g)
        qh = qh * (HEAD_DIM ** -0.5)
        aug = jnp.zeros_like(qh) if slopes is None else _query_aug(tpos, slopes[h])
        qp_s[h] = jnp.concatenate([qh, aug], axis=1).astype(BF16)


def _inproj_kernel(x_ref, g_ref, w_ref, o_ref, *, nchunk):
    h = _rms(x_ref[...], g_ref[...]).astype(BF16)
    cw = P_COLS // nchunk
    for c in range(nchunk):
        o_ref[:, c * cw:(c + 1) * cw] = jnp.dot(h, w_ref[:, c * cw:(c + 1) * cw], preferred_element_type=F32)


def _inproj(x2, g, w):
    t, d = x2.shape
    tm = 512
    return pl.pallas_call(
        functools.partial(_inproj_kernel, nchunk=6),
        grid=(t // tm,),
        in_specs=[pl.BlockSpec((tm, d), lambda i: (i, 0)),
                  pl.BlockSpec((1, d), lambda i: (0, 0)),
                  pl.BlockSpec((d, P_COLS), lambda i: (0, 0))],
        out_specs=pl.BlockSpec((tm, P_COLS), lambda i: (i, 0)),
        out_shape=jax.ShapeDtypeStruct((t, P_COLS), F32),
        compiler_params=pltpu.CompilerParams(dimension_semantics=("parallel",), vmem_limit_bytes=VMEM_LIMIT),
        name="inproj",
    )(x2, g, w)


def _gelu_tanh(x):
    return x * (0.5 * (1.0 + jnp.tanh(np.sqrt(2.0 / np.pi).astype(np.float32) * (x + 0.044715 * (x * x * x)))))


def _compress_kernel(kv_ref, pe_ref, w1_ref, w2_ref, gk_ref, kc_ref, vc_ref):
    nc = kv_ref.shape[0] // CMP_STRIDE
    first = None
    second = None
    for j in range(CMP_STRIDE):
        tok = kv_ref[pl.ds(j, nc, stride=CMP_STRIDE), :]
        a = jnp.dot((tok + pe_ref[j:j + 1, :]).astype(BF16), w1_ref[j], preferred_element_type=F32)
        b = jnp.dot((tok + pe_ref[CMP_STRIDE + j:CMP_STRIDE + j + 1, :]).astype(BF16), w1_ref[CMP_STRIDE + j],
                    preferred_element_type=F32)
        first = a if first is None else first + a
        second = b if second is None else second + b
    hid = first + pltpu.roll(second, nc - 1, 0)
    out = jnp.dot(_gelu_tanh(hid).astype(BF16), w2_ref[...], preferred_element_type=F32)
    kc = _rms(out[:, :HEAD_DIM], gk_ref[...])
    n = lax.broadcasted_iota(jnp.int32, (nc, 1), 0)
    kc_ref[0] = jnp.concatenate([kc, _key_aug(n * CMP_STRIDE + (CMP_LEN - 1))], axis=1).astype(BF16)
    vc_ref[0] = out[:, HEAD_DIM:].astype(BF16)


def _compress(p2, pe, w1, w2, gk, batch, seq):
    nc = seq // CMP_STRIDE
    hid2 = w1.shape[2]
    return pl.pallas_call(
        _compress_kernel,
        grid=(batch,),
        in_specs=[pl.BlockSpec((seq, LANES), lambda i: (i, COL_KVC // LANES)),
                  pl.BlockSpec((CMP_LEN, LANES), lambda i: (0, 0)),
                  pl.BlockSpec((CMP_LEN, LANES, hid2), lambda i: (0, 0, 0)),
                  pl.BlockSpec((hid2, LANES), lambda i: (0, 0)),
                  pl.BlockSpec((1, HEAD_DIM), lambda i: (0, 0))],
        out_specs=[pl.BlockSpec((1, nc, LANES), lambda i: (i, 0, 0)),
                   pl.BlockSpec((1, nc, HEAD_DIM), lambda i: (i, 0, 0))],
        out_shape=[jax.ShapeDtypeStruct((batch, nc, LANES), BF16),
                   jax.ShapeDtypeStruct((batch, nc, HEAD_DIM), BF16)],
        compiler_params=pltpu.CompilerParams(dimension_semantics=("parallel",), vmem_limit_bytes=VMEM_LIMIT),
        name="nsa_compress",
    )(p2, pe, w1, w2, gk)


def _compress_weights(pe_k, pe_v, w1_k, w2_k, w1_v, w2_v):
    hid = w1_k.shape[1]
    w1k = w1_k.reshape(CMP_LEN, HEAD_DIM, hid)
    w1v = w1_v.reshape(CMP_LEN, HEAD_DIM, hid)
    z1 = jnp.zeros_like(w1k)
    w1 = jnp.concatenate([jnp.concatenate([w1k, z1], axis=2), jnp.concatenate([z1, w1v], axis=2)], axis=1)
    z2 = jnp.zeros_like(w2_k)
    w2 = jnp.concatenate([jnp.concatenate([w2_k, z2], axis=1), jnp.concatenate([z2, w2_v], axis=1)], axis=0)
    pe = jnp.concatenate([pe_k, pe_v], axis=1)
    return pe, w1.astype(BF16), w2.astype(BF16)


def _flash_scores(kp_of, bias, qp_s, s_s, m_cur):
    m_next = []
    for h in range(N_HEADS):
        s = lax.dot_general(kp_of(h), qp_s[h], _NT, preferred_element_type=F32) + bias
        s_s[h] = s
        m_next.append(jnp.maximum(m_cur[h], jnp.max(s, axis=0, keepdims=True)))
    return tuple(m_next)


def _flash_accum(vt_of, s_s, acc_s, m_prev, m_cur, ls):
    new_l = []
    for h in range(N_HEADS):
        alpha = jnp.exp(m_prev[h] - m_cur[h])
        p = jnp.exp(s_s[h] - m_cur[h])
        new_l.append(alpha * ls[h] + jnp.sum(p, axis=0, keepdims=True))
        acc_s[h] = alpha * acc_s[h] + jnp.dot(vt_of(h), p.astype(BF16), preferred_element_type=F32)
    return tuple(new_l)


def _flash_accum_and_scores(kp_of, bias, vt_of, qp_s, s_s, acc_s, m_prev, m_cur, ls):
    m_next, new_l = [], []
    for h in range(N_HEADS):
        s_new = lax.dot_general(kp_of(h), qp_s[h], _NT, preferred_element_type=F32) + bias
        alpha = jnp.exp(m_prev[h] - m_cur[h])
        p = jnp.exp(s_s[h] - m_cur[h])
        new_l.append(alpha * ls[h] + jnp.sum(p, axis=0, keepdims=True))
        acc_s[h] = alpha * acc_s[h] + jnp.dot(vt_of(h), p.astype(BF16), preferred_element_type=F32)
        s_s[h] = s_new
        m_next.append(jnp.maximum(m_cur[h], jnp.max(s_new, axis=0, keepdims=True)))
    return tuple(m_next), tuple(new_l)


def _flash_reverse(first_kt, n_tiles, kp_fn, bias_fn, vt_fn, qp_s, s_s, acc_s, o_ref):
    acc_s[...] = jnp.zeros(acc_s.shape, F32)
    m0 = tuple(jnp.full((1, TQ), NEG_INF, F32) for _ in range(N_HEADS))
    l0 = tuple(jnp.zeros((1, TQ), F32) for _ in range(N_HEADS))
    m1 = _flash_scores(kp_fn(first_kt), bias_fn(first_kt, True), qp_s, s_s, m0)

    def body(i, carry):
        m_prev, m_cur, ls = carry
        kt = first_kt - i
        m_next, ls = _flash_accum_and_scores(kp_fn(kt - 1), bias_fn(kt - 1, False), vt_fn(kt), qp_s, s_s, acc_s,
                                             m_prev, m_cur, ls)
        return m_cur, m_next, ls

    m_prev, m_cur, ls = lax.fori_loop(0, n_tiles - 1, body, (m0, m1, l0))
    ls = _flash_accum(vt_fn(first_kt - (n_tiles - 1)), s_s, acc_s, m_prev, m_cur, ls)
    out = jnp.concatenate([acc_s[h] * (1.0 / ls[h]) for h in range(N_HEADS)], axis=0)
    o_ref[...] = out.T


def _nsa_kernel(q_ref, kc_ref, vc_ref, ksv_ref, gq_ref, gks_ref, ocmp_ref, oslc_ref,
                ksp_s, vst_s, qp_s, imp_s, sel_s, s_s, acc_s, *, seq):
    qi = pl.program_id(1)
    nc = seq // CMP_STRIDE
    nsel = seq // SEL_BLOCK
    nkt = seq // TK

    @pl.when(qi == 0)
    def _prep():
        for c in range(nkt):
            rows = slice(c * TK, (c + 1) * TK)
            blk = ksv_ref[rows, :]
            pos = c * TK + lax.broadcasted_iota(jnp.int32, (TK, 1), 0)
            kn = _rms(blk[:, :HEAD_DIM], gks_ref[...])
            ksp_s[rows, :] = jnp.concatenate([kn, _key_aug(pos)], axis=1).astype(BF16)
            vst_s[c] = blk.T[HEAD_DIM:, :].astype(BF16)
        imp_s[:, 0:8, :] = jnp.zeros((TQ // LANES, 8, LANES), F32)

    t0 = qi * TQ
    _prep_queries(q_ref[...], gq_ref[...], t0, SLOPES_A, qp_s)

    kc = kc_ref[0]
    vc = vc_ref[0]
    tq_col = t0 + lax.broadcasted_iota(jnp.int32, (TQ, nc), 0)
    n_idx = lax.broadcasted_iota(jnp.int32, (TQ, nc), 1)
    vis = (tq_col >= n_idx * CMP_STRIDE + (CMP_LEN - 1)) & (n_idx < nc - 1)
    visf = vis.astype(F32)
    imp = jnp.zeros((TQ, nc), F32)
    for h in range(N_HEADS):
        sc = lax.dot_general(qp_s[h], kc, _NT, preferred_element_type=F32)
        sc = jnp.where(vis, sc, NEG_INF)
        e = jnp.exp(sc - jnp.max(sc, axis=-1, keepdims=True))
        p = e / jnp.sum(e, axis=-1, keepdims=True) * visf
        ocmp_ref[:, h * HEAD_DIM:(h + 1) * HEAD_DIM] = jnp.dot(p.astype(BF16), vc, preferred_element_type=F32)
        imp = imp + p

    imp_t = imp.T
    halves = []
    for half in range(TQ // LANES):
        imp_s[half, 8:8 + nc, :] = imp_t[:, half * LANES:(half + 1) * LANES]
        r = [imp_s[half, pl.ds(8 + k, nsel, stride=4), :] for k in range(4)]
        rm1 = imp_s[half, pl.ds(7, nsel, stride=4), :]
        halves.append(rm1 + 2.0 * (r[0] + r[1] + r[2]) + r[3])
    imp_blk = jnp.concatenate(halves, axis=1)
    blk = lax.broadcasted_iota(jnp.int32, (nsel, TQ), 0)
    tl = t0 + lax.broadcasted_iota(jnp.int32, (nsel, TQ), 1)
    cur = tl >> 6
    forced = (blk == 0) | (blk == cur) | (blk == cur - 1)
    valid = blk * SEL_BLOCK <= tl
    score = jnp.where(forced, FORCE_SCORE, jnp.where(valid, imp_blk, -FORCE_SCORE))
    rank = jnp.zeros((nsel, TQ), jnp.int32)
    for j in range(nsel):
        row = score[j:j + 1, :]
        beats = (row > score) | ((row == score) & (blk > j))
        rank = rank + beats.astype(jnp.int32)
    sel_s[...] = jnp.where(rank < min(N_SELECT, nsel), 0.0, NEG_INF)

    def block_bias(kt):
        per_tile = TK // SEL_BLOCK
        return jnp.concatenate([jnp.broadcast_to(sel_s[pl.ds(kt * per_tile + j, 1), :], (SEL_BLOCK, TQ))
                                for j in range(per_tile)], axis=0)

    causal = lax.broadcasted_iota(jnp.int32, (TK, TQ), 0) <= lax.broadcasted_iota(jnp.int32, (TK, TQ), 1)

    def kp_fn(kt):
        kp = ksp_s[pl.ds(pl.multiple_of(kt * TK, TK), TK), :]
        return lambda h: kp

    def bias_fn(kt, first):
        return jnp.where(causal, block_bias(kt), NEG_INF) if first else block_bias(kt)

    _flash_reverse(qi, qi + 1, kp_fn, bias_fn, lambda kt: (lambda h: vst_s[kt]), qp_s, s_s, acc_s, oslc_ref)


def _nsa(p2, kc, vc, gq, gks, batch, seq):
    nq = seq // TQ
    nc = seq // CMP_STRIDE
    t = batch * seq
    return pl.pallas_call(
        functools.partial(_nsa_kernel, seq=seq),
        grid=(batch, nq),
        in_specs=[pl.BlockSpec((TQ, GROUP_W), lambda b, i: (b * nq + i, COL_QA // GROUP_W)),
                  pl.BlockSpec((1, nc, LANES), lambda b, i: (b, 0, 0)),
                  pl.BlockSpec((1, nc, HEAD_DIM), lambda b, i: (b, 0, 0)),
                  pl.BlockSpec((seq, LANES), lambda b, i: (b, COL_KSV // LANES)),
                  pl.BlockSpec((1, HEAD_DIM), lambda b, i: (0, 0)),
                  pl.BlockSpec((1, HEAD_DIM), lambda b, i: (0, 0))],
        out_specs=[pl.BlockSpec((TQ, GROUP_W), lambda b, i: (b * nq + i, 0)),
                   pl.BlockSpec((TQ, GROUP_W), lambda b, i: (b * nq + i, 0))],
        out_shape=[jax.ShapeDtypeStruct((t, GROUP_W), F32), jax.ShapeDtypeStruct((t, GROUP_W), F32)],
        scratch_shapes=[pltpu.VMEM((seq, LANES), BF16),
                        pltpu.VMEM((seq // TK, HEAD_DIM, TK), BF16),
                        pltpu.VMEM((N_HEADS, TQ, LANES), BF16),
                        pltpu.VMEM((TQ // LANES, 8 + nc, LANES), F32),
                        pltpu.VMEM((seq // SEL_BLOCK, TQ), F32),
                        pltpu.VMEM((N_HEADS, TK, TQ), F32),
                        pltpu.VMEM((N_HEADS, HEAD_DIM, TQ), F32)],
        compiler_params=pltpu.CompilerParams(dimension_semantics=("parallel", "arbitrary"),
                                             vmem_limit_bytes=VMEM_LIMIT),
        name="nsa_cmp_slc",
    )(p2, kc, vc, p2, gq, gks)


def _band_bias_tiles(window, mult_fn):
    nd = (window + TK - 1) // TK + 1
    kk = np.arange(TK)[:, None]
    qq = np.arange(TQ)[None, :]
    tiles = np.empty((nd, TK, TQ), np.float32)
    for di in range(nd):
        d = di * TK + qq - kk
        mult = mult_fn(d)
        tiles[di] = np.where(mult > 0, np.log(np.maximum(mult, 1)), NEG_INF)
    return tiles


def _window_mult(d):
    return ((d >= 0) & (d <= NSA_WINDOW - 1)).astype(np.float64)


def _dilated_mult(d):
    m = np.zeros(d.shape, np.float64)
    for window, dil in DILATED_CONFIGS:
        m += ((d >= 0) & (d <= window) & (d % dil == 0)).astype(np.float64)
    return m


def _banded_kernel(*refs, seq, window, slopes, shared_kv):
    if shared_kv:
        q_ref, kv_ref, gq_ref, gk_ref, bias_ref, o_ref, kp_s, vt_s, qp_s, s_s, acc_s = refs
    else:
        q_ref, k_ref, v_ref, gq_ref, gk_ref, bias_ref, o_ref, kp_s, vt_s, qp_s, s_s, acc_s = refs
    qi = pl.program_id(1)
    nkt = seq // TK

    @pl.when(qi == 0)
    def _prep():
        for c in range(nkt):
            rows = slice(c * TK, (c + 1) * TK)
            aug = _key_aug(c * TK + lax.broadcasted_iota(jnp.int32, (TK, 1), 0))
            if shared_kv:
                blk = kv_ref[rows, :]
                kn = _rms(blk[:, :HEAD_DIM], gk_ref[...])
                kp_s[rows, :] = jnp.concatenate([kn, aug], axis=1).astype(BF16)
                vt_s[c] = blk.T[HEAD_DIM:, :].astype(BF16)
            else:
                kb = k_ref[rows, :]
                parts = []
                for h in range(N_HEADS):
                    parts += [_rms(kb[:, h * HEAD_DIM:(h + 1) * HEAD_DIM], gk_ref[...]), aug]
                kp_s[rows, :] = jnp.concatenate(parts, axis=1).astype(BF16)
                vt_s[c] = v_ref[rows, :].T.astype(BF16)

    t0 = qi * TQ
    _prep_queries(q_ref[...], gq_ref[...], t0, slopes, qp_s)
    kt_lo = jnp.maximum(t0 - window, 0) // TK

    def kp_fn(kt):
        off = pl.multiple_of(kt * TK, TK)
        if shared_kv:
            kp = kp_s[pl.ds(off, TK), :]
            return lambda h: kp
        return lambda h: kp_s[pl.ds(off, TK), h * LANES:(h + 1) * LANES]

    def vt_fn(kt):
        if shared_kv:
            return lambda h: vt_s[kt]
        return lambda h: vt_s[kt, h * HEAD_DIM:(h + 1) * HEAD_DIM, :]

    _flash_reverse(qi, qi + 1 - kt_lo, kp_fn, lambda kt, first: bias_ref[qi - kt], vt_fn, qp_s, s_s, acc_s, o_ref)


def _banded(p2, gq, gk, bias_tiles, batch, seq, *, window, slopes, shared_kv, col_q, col_k, col_v, name):
    nq = seq // TQ
    t = batch * seq
    nd = bias_tiles.shape[0]
    n_kv = 1 if shared_kv else N_HEADS
    q_spec = pl.BlockSpec((TQ, GROUP_W), lambda b, i: (b * nq + i, col_q // GROUP_W))
    small = pl.BlockSpec((1, HEAD_DIM), lambda b, i: (0, 0))
    bias_spec = pl.BlockSpec((nd, TK, TQ), lambda b, i: (0, 0, 0))
    if shared_kv:
        kv_specs = [pl.BlockSpec((seq, LANES), lambda b, i: (b, col_k // LANES))]
        operands = (p2, p2, gq, gk, bias_tiles)
    else:
        kv_specs = [pl.BlockSpec((seq, GROUP_W), lambda b, i: (b, col_k // GROUP_W)),
                    pl.BlockSpec((seq, GROUP_W), lambda b, i: (b, col_v // GROUP_W))]
        operands = (p2, p2, p2, gq, gk, bias_tiles)
    return pl.pallas_call(
        functools.partial(_banded_kernel, seq=seq, window=window, slopes=slopes, shared_kv=shared_kv),
        grid=(batch, nq),
        in_specs=[q_spec] + kv_specs + [small, small, bias_spec],
        out_specs=pl.BlockSpec((TQ, GROUP_W), lambda b, i: (b * nq + i, 0)),
        out_shape=jax.ShapeDtypeStruct((t, GROUP_W), F32),
        scratch_shapes=[pltpu.VMEM((seq, n_kv * LANES), BF16),
                        pltpu.VMEM((seq // TK, n_kv * HEAD_DIM, TK), BF16),
                        pltpu.VMEM((N_HEADS, TQ, LANES), BF16),
                        pltpu.VMEM((N_HEADS, TK, TQ), F32),
                        pltpu.VMEM((N_HEADS, HEAD_DIM, TQ), F32)],
        compiler_params=pltpu.CompilerParams(dimension_semantics=("parallel", "arbitrary"),
                                             vmem_limit_bytes=VMEM_LIMIT),
        name=name,
    )(*operands)


def _stick_kernel(q_ref, k_ref, v_ref, o_ref, kp_s, vt_s, qp_s, hl_s, e_s, acc_s, *, seq):
    qi = pl.program_id(1)
    nkt = seq // TK

    @pl.when(qi == 0)
    def _prep():
        zeros = jnp.zeros((TK, HEAD_DIM), F32)
        for c in range(nkt):
            rows = slice(c * TK, (c + 1) * TK)
            kb = k_ref[rows, :]
            parts = []
            for h in range(N_HEADS):
                parts += [kb[:, h * HEAD_DIM:(h + 1) * HEAD_DIM], zeros]
            kp_s[rows, :] = jnp.concatenate(parts, axis=1).astype(BF16)
            vt_s[c] = v_ref[rows, :].T.astype(BF16)

    _prep_queries(q_ref[...], None, qi * TQ, None, qp_s)
    half = TK // 2
    col = lax.broadcasted_iota(jnp.int32, (half + SUM_PAD, TK), 1) & (half - 1)
    srow = lax.broadcasted_iota(jnp.int32, (half + SUM_PAD, TK), 0)
    sums = ((srow == half) | ((srow < half) & (col > srow))).astype(BF16)
    past = lax.broadcasted_iota(jnp.int32, (TK, TQ), 0) < lax.broadcasted_iota(jnp.int32, (TK, TQ), 1)
    acc_s[...] = jnp.zeros(acc_s.shape, F32)

    def logits(kt, h):
        off = pl.multiple_of(kt * TK, TK)
        return lax.dot_general(kp_s[pl.ds(off, TK), h * LANES:(h + 1) * LANES], qp_s[h], _NT,
                               preferred_element_type=F32)

    def stage(h, z, masked):
        sp = jnp.maximum(z, 0.0) + jnp.log(1.0 + jnp.exp2(jnp.abs(z) * (-LOG2E)))
        spm = jnp.where(past, sp, 0.0) if masked else sp
        hi = spm.astype(BF16)
        lo = (spm - hi.astype(F32)).astype(BF16)
        for b in range(2):
            hl_s[h, b, 0:half, :] = hi[b * half:(b + 1) * half, :]
            hl_s[h, b, half:TK, :] = lo[b * half:(b + 1) * half, :]
        logsig = z - sp
        e_s[h] = jnp.where(past, logsig, NEG_INF) if masked else logsig

    def tails(h):
        return [jnp.dot(sums, hl_s[h, b], preferred_element_type=F32) for b in range(2)]

    def consume(kt, h, w, later):
        later_lo = later + w[1][half:half + 1, :]
        attn = jnp.concatenate([jnp.exp(e_s[h, 0:half, :] - w[0][0:half, :] - later_lo),
                                jnp.exp(e_s[h, half:TK, :] - w[1][0:half, :] - later)], axis=0)
        acc_s[h] += jnp.dot(vt_s[kt, h * HEAD_DIM:(h + 1) * HEAD_DIM, :], attn.astype(BF16),
                            preferred_element_type=F32)
        return later_lo + w[0][half:half + 1, :]

    ahead = 2
    for h in range(N_HEADS):
        stage(h, logits(qi, h), True)

    def body(i, laters):
        kt = qi - i
        w = {h: tails(h) for h in range(ahead)}
        z = {h: logits(kt - 1, h) for h in range(ahead)}
        out = []
        for h in range(N_HEADS):
            out.append(consume(kt, h, w.pop(h), laters[h]))
            stage(h, z.pop(h), False)
            if h + ahead < N_HEADS:
                w[h + ahead] = tails(h + ahead)
                z[h + ahead] = logits(kt - 1, h + ahead)
        return tuple(out)

    laters = lax.fori_loop(0, qi, body, tuple(jnp.zeros((1, TQ), F32) for _ in range(N_HEADS)))
    w = {h: tails(h) for h in range(ahead)}
    for h in range(N_HEADS):
        consume(0, h, w.pop(h), laters[h])
        if h + ahead < N_HEADS:
            w[h + ahead] = tails(h + ahead)
    o_ref[...] = jnp.concatenate([acc_s[h] for h in range(N_HEADS)], axis=0).T


def _stick(p2, batch, seq):
    nq = seq // TQ
    t = batch * seq
    return pl.pallas_call(
        functools.partial(_stick_kernel, seq=seq),
        grid=(batch, nq),
        in_specs=[pl.BlockSpec((TQ, GROUP_W), lambda b, i: (b * nq + i, COL_QC // GROUP_W)),
                  pl.BlockSpec((seq, GROUP_W), lambda b, i: (b, COL_KC // GROUP_W)),
                  pl.BlockSpec((seq, GROUP_W), lambda b, i: (b, COL_VC // GROUP_W))],
        out_specs=pl.BlockSpec((TQ, GROUP_W), lambda b, i: (b * nq + i, 0)),
        out_shape=jax.ShapeDtypeStruct((t, GROUP_W), F32),
        scratch_shapes=[pltpu.VMEM((seq, N_HEADS * LANES), BF16),
                        pltpu.VMEM((seq // TK, GROUP_W, TK), BF16),
                        pltpu.VMEM((N_HEADS, TQ, LANES), BF16),
                        pltpu.VMEM((N_HEADS, 2, TK, TQ), BF16),
                        pltpu.VMEM((N_HEADS, TK, TQ), F32),
                        pltpu.VMEM((N_HEADS, HEAD_DIM, TQ), F32)],
        compiler_params=pltpu.CompilerParams(dimension_semantics=("parallel", "arbitrary"),
                                             vmem_limit_bytes=VMEM_LIMIT),
        name="stick_breaking",
    )(p2, p2, p2)


def _mixout_kernel(x_ref, ocmp_ref, oslc_ref, owin_ref, gate_ref, cvb_ref, cvc_ref, cvu_ref, pc_ref, pu_ref,
                   oc_ref, od_ref, bg_ref, cw_ref, gout_ref, wout_ref, o_ref, *, tiles_per_seq):
    i = pl.program_id(0)
    gates = jax.nn.sigmoid(gate_ref[...] + bg_ref[...])
    ocmp, oslc, owin = ocmp_ref[...], oslc_ref[...], owin_ref[...]
    groups = []
    for h in range(N_HEADS):
        cols = slice(h * HEAD_DIM, (h + 1) * HEAD_DIM)
        groups.append(gates[:, 3 * h:3 * h + 1] * ocmp[:, cols]
                      + gates[:, 3 * h + 1:3 * h + 2] * oslc[:, cols]
                      + gates[:, 3 * h + 2:3 * h + 3] * owin[:, cols])
    cu = cvc_ref[...] * cvu_ref[...]
    prev = jnp.where(i % tiles_per_seq == 0, 0.0, pc_ref[...] * pu_ref[...])
    full = jnp.concatenate([prev, cu], axis=0)
    back1 = pltpu.roll(full, 1, 0)[8:, :]
    back2 = pltpu.roll(full, 2, 0)[8:, :]
    cw = cw_ref[...]
    ob = cvb_ref[...] * (cw[0:1, :] * back2 + cw[1:2, :] * back1 + cw[2:3, :] * cu)
    oc, od = oc_ref[...], od_ref[...]
    for src in (ob, oc, od):
        for h in range(N_HEADS):
            groups.append(src[:, h * HEAD_DIM:(h + 1) * HEAD_DIM])
    gout = gout_ref[...]
    normed = [_rms(g, gout[:, k * HEAD_DIM:(k + 1) * HEAD_DIM]) for k, g in enumerate(groups)]
    mixed = jnp.concatenate(normed, axis=1).astype(BF16)
    o_ref[...] = x_ref[...] + jnp.dot(mixed, wout_ref[...], preferred_element_type=F32)


def _mixout(x2, p2, ocmp, oslc, owin, oc, od, bg, cw, gout, wout, seq):
    t, d = x2.shape
    tm = 256
    rows = lambda w, col: pl.BlockSpec((tm, w), lambda i: (i, col // w))
    prev8 = lambda col: pl.BlockSpec((8, GROUP_W), lambda i: (jnp.maximum(i * (tm // 8) - 1, 0), col // GROUP_W))
    const = lambda shape: pl.BlockSpec(shape, lambda i: (0, 0))
    return pl.pallas_call(
        functools.partial(_mixout_kernel, tiles_per_seq=seq // tm),
        grid=(t // tm,),
        in_specs=[rows(d, 0), rows(GROUP_W, 0), rows(GROUP_W, 0), rows(GROUP_W, 0),
                  rows(LANES, COL_GATE), rows(GROUP_W, COL_CVB), rows(GROUP_W, COL_CVC), rows(GROUP_W, COL_CVU),
                  prev8(COL_CVC), prev8(COL_CVU),
                  rows(GROUP_W, 0), rows(GROUP_W, 0),
                  const((1, LANES)), const((8, GROUP_W)), const((1, d)), const((d, d))],
        out_specs=rows(d, 0),
        out_shape=jax.ShapeDtypeStruct((t, d), F32),
        compiler_params=pltpu.CompilerParams(dimension_semantics=("parallel",), vmem_limit_bytes=VMEM_LIMIT),
        name="mixout",
    )(x2, ocmp, oslc, owin, p2, p2, p2, p2, p2, p2, oc, od, bg, cw, gout, wout)


def _ffn_kernel(x_ref, g_ref, wg_ref, wu_ref, wd_ref, o_ref, h_s):
    j = pl.program_id(1)

    @pl.when(j == 0)
    def _start():
        x = x_ref[...]
        h_s[...] = _rms(x, g_ref[...]).astype(BF16)
        o_ref[...] = x

    h = h_s[...]
    a = jnp.dot(h, wg_ref[...], preferred_element_type=F32)
    u = jnp.dot(h, wu_ref[...], preferred_element_type=F32)
    act = (a * jax.nn.sigmoid(a) * u).astype(BF16)
    o_ref[...] += jnp.dot(act, wd_ref[...], preferred_element_type=F32)


def _ffn(x2, g, wg, wu, wd):
    t, d = x2.shape
    dff = wg.shape[1]
    tm, tf = 1024, 256
    return pl.pallas_call(
        _ffn_kernel,
        grid=(t // tm, dff // tf),
        in_specs=[pl.BlockSpec((tm, d), lambda i, j: (i, 0)),
                  pl.BlockSpec((1, d), lambda i, j: (0, 0)),
                  pl.BlockSpec((d, tf), lambda i, j: (0, j)),
                  pl.BlockSpec((d, tf), lambda i, j: (0, j)),
                  pl.BlockSpec((tf, d), lambda i, j: (j, 0))],
        out_specs=pl.BlockSpec((tm, d), lambda i, j: (i, 0)),
        out_shape=jax.ShapeDtypeStruct((t, d), F32),
        scratch_shapes=[pltpu.VMEM((tm, d), BF16)],
        compiler_params=pltpu.CompilerParams(dimension_semantics=("parallel", "arbitrary"),
                                             vmem_limit_bytes=VMEM_LIMIT),
        name="ffn",
    )(x2, g, wg, wu, wd)


def _permute_w_in(w_in):
    d = w_in.shape[0]
    a = w_in[:, :COL_GATE]
    gate = w_in[:, COL_GATE:COL_GATE + N_GATES]
    rest = w_in[:, COL_GATE + N_GATES:]
    pad = jnp.zeros((d, LANES - N_GATES), w_in.dtype)
    return jnp.concatenate([a, gate, pad, rest], axis=1).astype(BF16)


def _layer(x2, batch, seq, g_mix, w_in, b_gate, g_q_nsa, g_k_cmp, g_k_slc, g_k_win, pe_k_cmp, pe_v_cmp,
           w1_k_cmp, w2_k_cmp, w1_v_cmp, w2_v_cmp, conv_w, g_q_dil, g_k_dil, g_out, w_out,
           g_ffn, w_gate, w_up, w_down):
    row = lambda v: v.reshape(1, -1)
    p2 = _inproj(x2, row(g_mix), _permute_w_in(w_in))

    pe, w1, w2 = _compress_weights(pe_k_cmp, pe_v_cmp, w1_k_cmp, w2_k_cmp, w1_v_cmp, w2_v_cmp)
    kc, vc = _compress(p2, pe, w1, w2, row(g_k_cmp), batch, seq)
    ocmp, oslc = _nsa(p2, kc, vc, row(g_q_nsa), row(g_k_slc), batch, seq)
    owin = _banded(p2, row(g_q_nsa), row(g_k_win), jnp.asarray(_band_bias_tiles(NSA_WINDOW - 1, _window_mult)),
                   batch, seq, window=NSA_WINDOW - 1, slopes=SLOPES_A, shared_kv=True,
                   col_q=COL_QA, col_k=COL_KWV, col_v=COL_KWV, name="nsa_window")
    od = _banded(p2, row(g_q_dil), row(g_k_dil),
                 jnp.asarray(_band_bias_tiles(DILATED_CONFIGS[-1][0], _dilated_mult)),
                 batch, seq, window=DILATED_CONFIGS[-1][0], slopes=SLOPES_D, shared_kv=False,
                 col_q=COL_QD, col_k=COL_KD, col_v=COL_VD, name="dilated")
    oc = _stick(p2, batch, seq)

    bg = jnp.zeros((1, LANES), F32).at[0, :N_GATES].set(b_gate)
    cwp = jnp.zeros((8, GROUP_W), F32).at[:CONV_K].set(conv_w)
    x1 = _mixout(x2, p2, ocmp, oslc, owin, oc, od, bg, cwp, row(g_out), w_out.astype(BF16), seq)
    return _ffn(x1, row(g_ffn), w_gate.astype(BF16), w_up.astype(BF16), w_down.astype(BF16))


def kernel(x, g_mix, w_in, b_gate, g_q_nsa, g_k_cmp, g_k_slc, g_k_win, pe_k_cmp, pe_v_cmp, w1_k_cmp, w2_k_cmp,
           w1_v_cmp, w2_v_cmp, conv_w, g_q_dil, g_k_dil, g_out, w_out, g_ffn, w_gate, w_up, w_down):
    batch, seq, d = x.shape
    assert seq % (CMP_STRIDE * LANES) == 0 and d % LANES == 0
    x2 = x.reshape(batch * seq, d)
    params = (g_mix, w_in, b_gate, g_q_nsa, g_k_cmp, g_k_slc, g_k_win, pe_k_cmp, pe_v_cmp, w1_k_cmp, w2_k_cmp,
              w1_v_cmp, w2_v_cmp, conv_w, g_q_dil, g_k_dil, g_out, w_out, g_ffn, w_gate, w_up, w_down)
    for layer in range(g_mix.shape[0]):
        x2 = _layer(x2, batch, seq, *[p[layer] for p in params])
    return x2.reshape(batch, seq, d)
```

```python
import functools

import numpy as np
import jax
import jax.numpy as jnp
from jax import lax
from jax.experimental import pallas as pl
from jax.experimental.pallas import tpu as pltpu

F32 = jnp.float32
BF16 = jnp.bfloat16

HEAD_DIM = 64
N_HEADS = 4
GROUP_W = N_HEADS * HEAD_DIM
CONV_K = 3
CMP_LEN = 32
CMP_STRIDE = 16
SEL_BLOCK = 64
N_SELECT = 16
NSA_WINDOW = 512
DILATED_CONFIGS = ((128, 1), (512, 4), (2048, 16))
NEG_INF = -1e30
FORCE_SCORE = 1e6
RMS_EPS = 1e-6

TQ = 256
TK = 256
LANES = 128
SUM_PAD = 16
LOG2E = 1.4426950408889634
VMEM_LIMIT = 52 * 1024 * 1024

COL_QA, COL_KVC, COL_KSV, COL_KWV, COL_GATE = 0, 256, 384, 512, 640
COL_CVB, COL_CVC, COL_CVU = 768, 1024, 1280
COL_QC, COL_KC, COL_VC = 1536, 1792, 2048
COL_QD, COL_KD, COL_VD = 2304, 2560, 2816
P_COLS = 3072
N_GATES = 12

_NT = (((1,), (1,)), ((), ()))


def _alibi_slopes():
    s = [2.0 ** (-8.0 * i / 8) for i in range(1, 9)]
    return tuple(s[0::2]), tuple(s[1::2])


SLOPES_A, SLOPES_D = _alibi_slopes()


def _rms(x, g):
    return x * lax.rsqrt(jnp.mean(x * x, axis=-1, keepdims=True) + RMS_EPS) * g


def _key_aug(pos):
    rows = pos.shape[0]
    lane = lax.broadcasted_iota(jnp.int32, (rows, HEAD_DIM), 1)
    hi = (pos >> 6).astype(F32)
    lo = (pos & 63).astype(F32)
    return jnp.where(lane == 0, hi, jnp.where(lane == 1, lo, jnp.where(lane < 4, 1.0, 0.0)))


def _query_aug(t, slope):
    rows = t.shape[0]
    lane = lax.broadcasted_iota(jnp.int32, (rows, HEAD_DIM), 1)
    hi = (t >> 6).astype(F32) * (-64.0 * slope)
    lo = (t & 63).astype(F32) * (-slope)
    return jnp.where(lane == 0, 64.0 * slope,
                     jnp.where(lane == 1, slope, jnp.where(lane == 2, hi, jnp.where(lane == 3, lo, 0.0))))


def _prep_queries(q, g, t0, slopes, qp_s):
    tpos = t0 + lax.broadcasted_iota(jnp.int32, (q.shape[0], 1), 0)
    for h in range(N_HEADS):
        qh = q[:, h * HEAD_DIM:(h + 1) * HEAD_DIM]
        if g is not None:
            qh = _rms(qh, g)
        qh = qh * (HEAD_DIM ** -0.5)
        aug = jnp.zeros_like(qh) if slopes is None else _query_aug(tpos, slopes[h])
        qp_s[h] = jnp.concatenate([qh, aug], axis=1).astype(BF16)


def _inproj_kernel(x_ref, g_ref, w_ref, o_ref, *, nchunk):
    h = _rms(x_ref[...], g_ref[...]).astype(BF16)
    cw = P_COLS // nchunk
    for c in range(nchunk):
        o_ref[:, c * cw:(c + 1) * cw] = jnp.dot(h, w_ref[:, c * cw:(c + 1) * cw], preferred_element_type=F32)


def _inproj(x2, g, w):
    t, d = x2.shape
    tm = 512
    return pl.pallas_call(
        functools.partial(_inproj_kernel, nchunk=6),
        grid=(t // tm,),
        in_specs=[pl.BlockSpec((tm, d), lambda i: (i, 0)),
                  pl.BlockSpec((1, d), lambda i: (0, 0)),
                  pl.BlockSpec((d, P_COLS), lambda i: (0, 0))],
        out_specs=pl.BlockSpec((tm, P_COLS), lambda i: (i, 0)),
        out_shape=jax.ShapeDtypeStruct((t, P_COLS), F32),
        compiler_params=pltpu.CompilerParams(dimension_semantics=("parallel",), vmem_limit_bytes=VMEM_LIMIT),
        name="inproj",
    )(x2, g, w)


def _gelu_tanh(x):
    return x * (0.5 * (1.0 + jnp.tanh(np.sqrt(2.0 / np.pi).astype(np.float32) * (x + 0.044715 * (x * x * x)))))


def _compress_kernel(kv_ref, pe_ref, w1_ref, w2_ref, gk_ref, kc_ref, vct_ref):
    nc = kv_ref.shape[0] // CMP_STRIDE
    first = None
    second = None
    for j in range(CMP_STRIDE):
        tok = kv_ref[pl.ds(j, nc, stride=CMP_STRIDE), :]
        a = jnp.dot((tok + pe_ref[j:j + 1, :]).astype(BF16), w1_ref[j], preferred_element_type=F32)
        b = jnp.dot((tok + pe_ref[CMP_STRIDE + j:CMP_STRIDE + j + 1, :]).astype(BF16), w1_ref[CMP_STRIDE + j],
                    preferred_element_type=F32)
        first = a if first is None else first + a
        second = b if second is None else second + b
    hid = first + pltpu.roll(second, nc - 1, 0)
    out = jnp.dot(_gelu_tanh(hid).astype(BF16), w2_ref[...], preferred_element_type=F32)
    kc = _rms(out[:, :HEAD_DIM], gk_ref[...])
    n = lax.broadcasted_iota(jnp.int32, (nc, 1), 0)
    kc_ref[0] = jnp.concatenate([kc, _key_aug(n * CMP_STRIDE + (CMP_LEN - 1))], axis=1).astype(BF16)
    vct_ref[0] = out.T[HEAD_DIM:, :].astype(BF16)


def _compress(p2, pe, w1, w2, gk, batch, seq):
    nc = seq // CMP_STRIDE
    hid2 = w1.shape[2]
    return pl.pallas_call(
        _compress_kernel,
        grid=(batch,),
        in_specs=[pl.BlockSpec((seq, LANES), lambda i: (i, COL_KVC // LANES)),
                  pl.BlockSpec((CMP_LEN, LANES), lambda i: (0, 0)),
                  pl.BlockSpec((CMP_LEN, LANES, hid2), lambda i: (0, 0, 0)),
                  pl.BlockSpec((hid2, LANES), lambda i: (0, 0)),
                  pl.BlockSpec((1, HEAD_DIM), lambda i: (0, 0))],
        out_specs=[pl.BlockSpec((1, nc, LANES), lambda i: (i, 0, 0)),
                   pl.BlockSpec((1, HEAD_DIM, nc), lambda i: (i, 0, 0))],
        out_shape=[jax.ShapeDtypeStruct((batch, nc, LANES), BF16),
                   jax.ShapeDtypeStruct((batch, HEAD_DIM, nc), BF16)],
        compiler_params=pltpu.CompilerParams(dimension_semantics=("parallel",), vmem_limit_bytes=VMEM_LIMIT),
        name="nsa_compress",
    )(p2, pe, w1, w2, gk)


def _compress_weights(pe_k, pe_v, w1_k, w2_k, w1_v, w2_v):
    hid = w1_k.shape[1]
    w1k = w1_k.reshape(CMP_LEN, HEAD_DIM, hid)
    w1v = w1_v.reshape(CMP_LEN, HEAD_DIM, hid)
    z1 = jnp.zeros_like(w1k)
    w1 = jnp.concatenate([jnp.concatenate([w1k, z1], axis=2), jnp.concatenate([z1, w1v], axis=2)], axis=1)
    z2 = jnp.zeros_like(w2_k)
    w2 = jnp.concatenate([jnp.concatenate([w2_k, z2], axis=1), jnp.concatenate([z2, w2_v], axis=1)], axis=0)
    pe = jnp.concatenate([pe_k, pe_v], axis=1)
    return pe, w1.astype(BF16), w2.astype(BF16)


def _flash_scores(kp_of, bias, qp_s, s_s, m_cur):
    m_next = []
    for h in range(N_HEADS):
        s = lax.dot_general(kp_of(h), qp_s[h], _NT, preferred_element_type=F32) + bias
        s_s[h] = s
        m_next.append(jnp.maximum(m_cur[h], jnp.max(s, axis=0, keepdims=True)))
    return tuple(m_next)


def _flash_accum(vt_of, s_s, acc_s, m_prev, m_cur, ls):
    new_l = []
    for h in range(N_HEADS):
        alpha = jnp.exp(m_prev[h] - m_cur[h])
        p = jnp.exp(s_s[h] - m_cur[h])
        new_l.append(alpha * ls[h] + jnp.sum(p, axis=0, keepdims=True))
        acc_s[h] = alpha * acc_s[h] + jnp.dot(vt_of(h), p.astype(BF16), preferred_element_type=F32)
    return tuple(new_l)


def _flash_accum_and_scores(kp_of, bias, vt_of, qp_s, s_s, acc_s, m_prev, m_cur, ls):
    m_next, new_l = [], []
    for h in range(N_HEADS):
        s_new = lax.dot_general(kp_of(h), qp_s[h], _NT, preferred_element_type=F32) + bias
        alpha = jnp.exp(m_prev[h] - m_cur[h])
        p = jnp.exp(s_s[h] - m_cur[h])
        new_l.append(alpha * ls[h] + jnp.sum(p, axis=0, keepdims=True))
        acc_s[h] = alpha * acc_s[h] + jnp.dot(vt_of(h), p.astype(BF16), preferred_element_type=F32)
        s_s[h] = s_new
        m_next.append(jnp.maximum(m_cur[h], jnp.max(s_new, axis=0, keepdims=True)))
    return tuple(m_next), tuple(new_l)


def _flash_reverse(first_kt, n_tiles, kp_fn, bias_fn, vt_fn, qp_s, s_s, acc_s, o_ref):
    acc_s[...] = jnp.zeros(acc_s.shape, F32)
    m0 = tuple(jnp.full((1, TQ), NEG_INF, F32) for _ in range(N_HEADS))
    l0 = tuple(jnp.zeros((1, TQ), F32) for _ in range(N_HEADS))
    m1 = _flash_scores(kp_fn(first_kt), bias_fn(first_kt, True), qp_s, s_s, m0)

    def body(i, carry):
        m_prev, m_cur, ls = carry
        kt = first_kt - i
        m_next, ls = _flash_accum_and_scores(kp_fn(kt - 1), bias_fn(kt - 1, False), vt_fn(kt), qp_s, s_s, acc_s,
                                             m_prev, m_cur, ls)
        return m_cur, m_next, ls

    m_prev, m_cur, ls = lax.fori_loop(0, n_tiles - 1, body, (m0, m1, l0))
    ls = _flash_accum(vt_fn(first_kt - (n_tiles - 1)), s_s, acc_s, m_prev, m_cur, ls)
    out = jnp.concatenate([acc_s[h] * (1.0 / ls[h]) for h in range(N_HEADS)], axis=0)
    o_ref[...] = out.T


def _nsa_kernel(q_ref, kc_ref, vct_ref, ksv_ref, gq_ref, gks_ref, ocmp_ref, oslc_ref,
                ksp_s, vst_s, qp_s, imp_s, sel_s, s_s, acc_s, *, seq):
    qi = pl.program_id(1)
    nc = seq // CMP_STRIDE
    nsel = seq // SEL_BLOCK
    nkt = seq // TK

    @pl.when(qi == 0)
    def _prep():
        for c in range(nkt):
            rows = slice(c * TK, (c + 1) * TK)
            blk = ksv_ref[rows, :]
            pos = c * TK + lax.broadcasted_iota(jnp.int32, (TK, 1), 0)
            kn = _rms(blk[:, :HEAD_DIM], gks_ref[...])
            ksp_s[rows, :] = jnp.concatenate([kn, _key_aug(pos)], axis=1).astype(BF16)
            vst_s[c] = blk.T[HEAD_DIM:, :].astype(BF16)
        imp_s[:, 0:8, :] = jnp.zeros((TQ // LANES, 8, LANES), F32)

    t0 = qi * TQ
    _prep_queries(q_ref[...], gq_ref[...], t0, SLOPES_A, qp_s)

    kc = kc_ref[0]
    vct = vct_ref[0]
    tq_row = t0 + lax.broadcasted_iota(jnp.int32, (nc, TQ), 1)
    n_idx = lax.broadcasted_iota(jnp.int32, (nc, TQ), 0)
    vis = (tq_row >= n_idx * CMP_STRIDE + (CMP_LEN - 1)) & (n_idx < nc - 1)
    vis_bias = jnp.where(vis, 0.0, NEG_INF)
    sees_any = (t0 + lax.broadcasted_iota(jnp.int32, (1, TQ), 1) >= CMP_LEN - 1).astype(F32)
    imp = jnp.zeros((nc, TQ), F32)
    ocmp_t = []
    for h in range(N_HEADS):
        sc = lax.dot_general(kc, qp_s[h], _NT, preferred_element_type=F32) + vis_bias
        e = jnp.exp(sc - jnp.max(sc, axis=0, keepdims=True))
        p = e * (sees_any / jnp.sum(e, axis=0, keepdims=True))
        ocmp_t.append(jnp.dot(vct, p.astype(BF16), preferred_element_type=F32))
        imp = imp + p
    ocmp_ref[...] = jnp.concatenate(ocmp_t, axis=0).T

    halves = []
    for half in range(TQ // LANES):
        imp_s[half, 8:8 + nc, :] = imp[:, half * LANES:(half + 1) * LANES]
        r = [imp_s[half, pl.ds(8 + k, nsel, stride=4), :] for k in range(4)]
        rm1 = imp_s[half, pl.ds(7, nsel, stride=4), :]
        halves.append(rm1 + 2.0 * (r[0] + r[1] + r[2]) + r[3])
    imp_blk = jnp.concatenate(halves, axis=1)
    blk = lax.broadcasted_iota(jnp.int32, (nsel, TQ), 0)
    tl = t0 + lax.broadcasted_iota(jnp.int32, (nsel, TQ), 1)
    cur = tl >> 6
    forced = (blk == 0) | (blk == cur) | (blk == cur - 1)
    valid = blk * SEL_BLOCK <= tl
    score = jnp.where(forced, FORCE_SCORE, jnp.where(valid, imp_blk, -FORCE_SCORE))
    sub = 8
    for g in range(nsel // sub):
        mine = score[g * sub:(g + 1) * sub, :]
        blk_g = g * sub + lax.broadcasted_iota(jnp.int32, (sub, TQ), 0)
        rank = jnp.zeros((sub, TQ), F32)
        for j in range(nsel):
            row = score[j:j + 1, :]
            if j < g * sub:
                beats = row >= mine
            elif j >= (g + 1) * sub:
                beats = row > mine
            else:
                beats = (row > mine) | ((row == mine) & (blk_g > j))
            rank = rank + jnp.where(beats, 1.0, 0.0)
        sel_s[g * sub:(g + 1) * sub, :] = jnp.where(rank < min(N_SELECT, nsel), 0.0, NEG_INF)

    def block_bias(kt):
        per_tile = TK // SEL_BLOCK
        return jnp.concatenate([jnp.broadcast_to(sel_s[pl.ds(kt * per_tile + j, 1), :], (SEL_BLOCK, TQ))
                                for j in range(per_tile)], axis=0)

    causal = lax.broadcasted_iota(jnp.int32, (TK, TQ), 0) <= lax.broadcasted_iota(jnp.int32, (TK, TQ), 1)

    def kp_fn(kt):
        kp = ksp_s[pl.ds(pl.multiple_of(kt * TK, TK), TK), :]
        return lambda h: kp

    def bias_fn(kt, first):
        return jnp.where(causal, block_bias(kt), NEG_INF) if first else block_bias(kt)

    _flash_reverse(qi, qi + 1, kp_fn, bias_fn, lambda kt: (lambda h: vst_s[kt]), qp_s, s_s, acc_s, oslc_ref)


def _nsa(p2, kc, vc, gq, gks, batch, seq):
    nq = seq // TQ
    nc = seq // CMP_STRIDE
    t = batch * seq
    return pl.pallas_call(
        functools.partial(_nsa_kernel, seq=seq),
        grid=(batch, nq),
        in_specs=[pl.BlockSpec((TQ, GROUP_W), lambda b, i: (b * nq + i, COL_QA // GROUP_W)),
                  pl.BlockSpec((1, nc, LANES), lambda b, i: (b, 0, 0)),
                  pl.BlockSpec((1, HEAD_DIM, nc), lambda b, i: (b, 0, 0)),
                  pl.BlockSpec((seq, LANES), lambda b, i: (b, COL_KSV // LANES)),
                  pl.BlockSpec((1, HEAD_DIM), lambda b, i: (0, 0)),
                  pl.BlockSpec((1, HEAD_DIM), lambda b, i: (0, 0))],
        out_specs=[pl.BlockSpec((TQ, GROUP_W), lambda b, i: (b * nq + i, 0)),
                   pl.BlockSpec((TQ, GROUP_W), lambda b, i: (b * nq + i, 0))],
        out_shape=[jax.ShapeDtypeStruct((t, GROUP_W), F32), jax.ShapeDtypeStruct((t, GROUP_W), F32)],
        scratch_shapes=[pltpu.VMEM((seq, LANES), BF16),
                        pltpu.VMEM((seq // TK, HEAD_DIM, TK), BF16),
                        pltpu.VMEM((N_HEADS, TQ, LANES), BF16),
                        pltpu.VMEM((TQ // LANES, 8 + nc, LANES), F32),
                        pltpu.VMEM((seq // SEL_BLOCK, TQ), F32),
                        pltpu.VMEM((N_HEADS, TK, TQ), F32),
                        pltpu.VMEM((N_HEADS, HEAD_DIM, TQ), F32)],
        compiler_params=pltpu.CompilerParams(dimension_semantics=("parallel", "arbitrary"),
                                             vmem_limit_bytes=VMEM_LIMIT),
        name="nsa_cmp_slc",
    )(p2, kc, vc, p2, gq, gks)


def _band_bias_tiles(window, mult_fn):
    nd = (window + TK - 1) // TK + 1
    kk = np.arange(TK)[:, None]
    qq = np.arange(TQ)[None, :]
    tiles = np.empty((nd, TK, TQ), np.float32)
    for di in range(nd):
        d = di * TK + qq - kk
        mult = mult_fn(d)
        tiles[di] = np.where(mult > 0, np.log(np.maximum(mult, 1)), NEG_INF)
    return tiles


def _window_mult(d):
    return ((d >= 0) & (d <= NSA_WINDOW - 1)).astype(np.float64)


def _dilated_mult(d):
    m = np.zeros(d.shape, np.float64)
    for window, dil in DILATED_CONFIGS:
        m += ((d >= 0) & (d <= window) & (d % dil == 0)).astype(np.float64)
    return m


def _banded_kernel(*refs, seq, window, slopes, shared_kv):
    if shared_kv:
        q_ref, kv_ref, gq_ref, gk_ref, bias_ref, o_ref, kp_s, vt_s, qp_s, s_s, acc_s = refs
    else:
        q_ref, k_ref, v_ref, gq_ref, gk_ref, bias_ref, o_ref, kp_s, vt_s, qp_s, s_s, acc_s = refs
    qi = pl.program_id(1)
    nkt = seq // TK

    @pl.when(qi == 0)
    def _prep():
        for c in range(nkt):
            rows = slice(c * TK, (c + 1) * TK)
            aug = _key_aug(c * TK + lax.broadcasted_iota(jnp.int32, (TK, 1), 0))
            if shared_kv:
                blk = kv_ref[rows, :]
                kn = _rms(blk[:, :HEAD_DIM], gk_ref[...])
                kp_s[rows, :] = jnp.concatenate([kn, aug], axis=1).astype(BF16)
                vt_s[c] = blk.T[HEAD_DIM:, :].astype(BF16)
            else:
                kb = k_ref[rows, :]
                parts = []
                for h in range(N_HEADS):
                    parts += [_rms(kb[:, h * HEAD_DIM:(h + 1) * HEAD_DIM], gk_ref[...]), aug]
                kp_s[rows, :] = jnp.concatenate(parts, axis=1).astype(BF16)
                vt_s[c] = v_ref[rows, :].T.astype(BF16)

    t0 = qi * TQ
    _prep_queries(q_ref[...], gq_ref[...], t0, slopes, qp_s)
    kt_lo = jnp.maximum(t0 - window, 0) // TK

    def kp_fn(kt):
        off = pl.multiple_of(kt * TK, TK)
        if shared_kv:
            kp = kp_s[pl.ds(off, TK), :]
            return lambda h: kp
        return lambda h: kp_s[pl.ds(off, TK), h * LANES:(h + 1) * LANES]

    def vt_fn(kt):
        if shared_kv:
            return lambda h: vt_s[kt]
        return lambda h: vt_s[kt, h * HEAD_DIM:(h + 1) * HEAD_DIM, :]

    _flash_reverse(qi, qi + 1 - kt_lo, kp_fn, lambda kt, first: bias_ref[qi - kt], vt_fn, qp_s, s_s, acc_s, o_ref)


def _banded(p2, gq, gk, bias_tiles, batch, seq, *, window, slopes, shared_kv, col_q, col_k, col_v, name):
    nq = seq // TQ
    t = batch * seq
    nd = bias_tiles.shape[0]
    n_kv = 1 if shared_kv else N_HEADS
    q_spec = pl.BlockSpec((TQ, GROUP_W), lambda b, i: (b * nq + i, col_q // GROUP_W))
    small = pl.BlockSpec((1, HEAD_DIM), lambda b, i: (0, 0))
    bias_spec = pl.BlockSpec((nd, TK, TQ), lambda b, i: (0, 0, 0))
    if shared_kv:
        kv_specs = [pl.BlockSpec((seq, LANES), lambda b, i: (b, col_k // LANES))]
        operands = (p2, p2, gq, gk, bias_tiles)
    else:
        kv_specs = [pl.BlockSpec((seq, GROUP_W), lambda b, i: (b, col_k // GROUP_W)),
                    pl.BlockSpec((seq, GROUP_W), lambda b, i: (b, col_v // GROUP_W))]
        operands = (p2, p2, p2, gq, gk, bias_tiles)
    return pl.pallas_call(
        functools.partial(_banded_kernel, seq=seq, window=window, slopes=slopes, shared_kv=shared_kv),
        grid=(batch, nq),
        in_specs=[q_spec] + kv_specs + [small, small, bias_spec],
        out_specs=pl.BlockSpec((TQ, GROUP_W), lambda b, i: (b * nq + i, 0)),
        out_shape=jax.ShapeDtypeStruct((t, GROUP_W), F32),
        scratch_shapes=[pltpu.VMEM((seq, n_kv * LANES), BF16),
                        pltpu.VMEM((seq // TK, n_kv * HEAD_DIM, TK), BF16),
                        pltpu.VMEM((N_HEADS, TQ, LANES), BF16),
                        pltpu.VMEM((N_HEADS, TK, TQ), F32),
                        pltpu.VMEM((N_HEADS, HEAD_DIM, TQ), F32)],
        compiler_params=pltpu.CompilerParams(dimension_semantics=("parallel", "arbitrary"),
                                             vmem_limit_bytes=VMEM_LIMIT),
        name=name,
    )(*operands)


def _stick_kernel(q_ref, k_ref, v_ref, o_ref, kp_s, vt_s, qp_s, hl_s, e_s, acc_s, *, seq):
    qi = pl.program_id(1)
    nkt = seq // TK

    @pl.when(qi == 0)
    def _prep():
        zeros = jnp.zeros((TK, HEAD_DIM), F32)
        for c in range(nkt):
            rows = slice(c * TK, (c + 1) * TK)
            kb = k_ref[rows, :]
            parts = []
            for h in range(N_HEADS):
                parts += [kb[:, h * HEAD_DIM:(h + 1) * HEAD_DIM], zeros]
            kp_s[rows, :] = jnp.concatenate(parts, axis=1).astype(BF16)
            vt_s[c] = v_ref[rows, :].T.astype(BF16)

    _prep_queries(q_ref[...], None, qi * TQ, None, qp_s)
    half = TK // 2
    col = lax.broadcasted_iota(jnp.int32, (half + SUM_PAD, TK), 1) & (half - 1)
    srow = lax.broadcasted_iota(jnp.int32, (half + SUM_PAD, TK), 0)
    sums = ((srow == half) | ((srow < half) & (col > srow))).astype(BF16)
    past = lax.broadcasted_iota(jnp.int32, (TK, TQ), 0) < lax.broadcasted_iota(jnp.int32, (TK, TQ), 1)
    acc_s[...] = jnp.zeros(acc_s.shape, F32)

    def logits(kt, h):
        off = pl.multiple_of(kt * TK, TK)
        return lax.dot_general(kp_s[pl.ds(off, TK), h * LANES:(h + 1) * LANES], qp_s[h], _NT,
                               preferred_element_type=F32)

    def stage(h, z, masked):
        sp = jnp.maximum(z, 0.0) + jnp.log(1.0 + jnp.exp2(jnp.abs(z) * (-LOG2E)))
        spm = jnp.where(past, sp, 0.0) if masked else sp
        hi = spm.astype(BF16)
        lo = (spm - hi.astype(F32)).astype(BF16)
        for b in range(2):
            hl_s[h, b, 0:half, :] = hi[b * half:(b + 1) * half, :]
            hl_s[h, b, half:TK, :] = lo[b * half:(b + 1) * half, :]
        logsig = z - sp
        e_s[h] = jnp.where(past, logsig, NEG_INF) if masked else logsig

    def tails(h):
        return [jnp.dot(sums, hl_s[h, b], preferred_element_type=F32) for b in range(2)]

    def consume(kt, h, w, later):
        later_lo = later + w[1][half:half + 1, :]
        attn = jnp.concatenate([jnp.exp(e_s[h, 0:half, :] - w[0][0:half, :] - later_lo),
                                jnp.exp(e_s[h, half:TK, :] - w[1][0:half, :] - later)], axis=0)
        acc_s[h] += jnp.dot(vt_s[kt, h * HEAD_DIM:(h + 1) * HEAD_DIM, :], attn.astype(BF16),
                            preferred_element_type=F32)
        return later_lo + w[0][half:half + 1, :]

    ahead = 2
    for h in range(N_HEADS):
        stage(h, logits(qi, h), True)

    def body(i, laters):
        kt = qi - i
        w = {h: tails(h) for h in range(ahead)}
        z = {h: logits(kt - 1, h) for h in range(ahead)}
        out = []
        for h in range(N_HEADS):
            out.append(consume(kt, h, w.pop(h), laters[h]))
            stage(h, z.pop(h), False)
            if h + ahead < N_HEADS:
                w[h + ahead] = tails(h + ahead)
                z[h + ahead] = logits(kt - 1, h + ahead)
        return tuple(out)

    laters = lax.fori_loop(0, qi, body, tuple(jnp.zeros((1, TQ), F32) for _ in range(N_HEADS)))
    w = {h: tails(h) for h in range(ahead)}
    for h in range(N_HEADS):
        consume(0, h, w.pop(h), laters[h])
        if h + ahead < N_HEADS:
            w[h + ahead] = tails(h + ahead)
    o_ref[...] = jnp.concatenate([acc_s[h] for h in range(N_HEADS)], axis=0).T


def _stick(p2, batch, seq):
    nq = seq // TQ
    t = batch * seq
    return pl.pallas_call(
        functools.partial(_stick_kernel, seq=seq),
        grid=(batch, nq),
        in_specs=[pl.BlockSpec((TQ, GROUP_W), lambda b, i: (b * nq + i, COL_QC // GROUP_W)),
                  pl.BlockSpec((seq, GROUP_W), lambda b, i: (b, COL_KC // GROUP_W)),
                  pl.BlockSpec((seq, GROUP_W), lambda b, i: (b, COL_VC // GROUP_W))],
        out_specs=pl.BlockSpec((TQ, GROUP_W), lambda b, i: (b * nq + i, 0)),
        out_shape=jax.ShapeDtypeStruct((t, GROUP_W), F32),
        scratch_shapes=[pltpu.VMEM((seq, N_HEADS * LANES), BF16),
                        pltpu.VMEM((seq // TK, GROUP_W, TK), BF16),
                        pltpu.VMEM((N_HEADS, TQ, LANES), BF16),
                        pltpu.VMEM((N_HEADS, 2, TK, TQ), BF16),
                        pltpu.VMEM((N_HEADS, TK, TQ), F32),
                        pltpu.VMEM((N_HEADS, HEAD_DIM, TQ), F32)],
        compiler_params=pltpu.CompilerParams(dimension_semantics=("parallel", "arbitrary"),
                                             vmem_limit_bytes=VMEM_LIMIT),
        name="stick_breaking",
    )(p2, p2, p2)


def _mixout_kernel(x_ref, ocmp_ref, oslc_ref, owin_ref, gate_ref, cvb_ref, cvc_ref, cvu_ref, pc_ref, pu_ref,
                   oc_ref, od_ref, bg_ref, cw_ref, gout_ref, wout_ref, o_ref, *, tiles_per_seq):
    i = pl.program_id(0)
    d = x_ref.shape[1]

    def exact_dot(a, sel):
        hi = a.astype(BF16)
        lo = (a - hi.astype(F32)).astype(BF16)
        return jnp.dot(hi, sel, preferred_element_type=F32) + jnp.dot(lo, sel, preferred_element_type=F32)

    gates = jax.nn.sigmoid(gate_ref[...] + bg_ref[...])
    src = lax.broadcasted_iota(jnp.int32, (LANES, GROUP_W), 0)
    head3 = (lax.broadcasted_iota(jnp.int32, (LANES, GROUP_W), 1) >> 6) * 3
    oa = (exact_dot(gates, (src == head3).astype(BF16)) * ocmp_ref[...]
          + exact_dot(gates, (src == head3 + 1).astype(BF16)) * oslc_ref[...]
          + exact_dot(gates, (src == head3 + 2).astype(BF16)) * owin_ref[...])
    cu = cvc_ref[...] * cvu_ref[...]
    prev = jnp.where(i % tiles_per_seq == 0, 0.0, pc_ref[...] * pu_ref[...])
    full = jnp.concatenate([prev, cu], axis=0)
    back1 = pltpu.roll(full, 1, 0)[8:, :]
    back2 = pltpu.roll(full, 2, 0)[8:, :]
    cw = cw_ref[...]
    ob = cvb_ref[...] * (cw[0:1, :] * back2 + cw[1:2, :] * back1 + cw[2:3, :] * cu)
    groups = jnp.concatenate([oa, ob, oc_ref[...], od_ref[...]], axis=1)
    n_groups = d // HEAD_DIM
    gather = ((lax.broadcasted_iota(jnp.int32, (d, LANES), 0) >> 6)
              == lax.broadcasted_iota(jnp.int32, (d, LANES), 1)).astype(BF16)
    spread = (lax.broadcasted_iota(jnp.int32, (LANES, d), 0)
              == (lax.broadcasted_iota(jnp.int32, (LANES, d), 1) >> 6)).astype(BF16)
    ssq = exact_dot(groups * groups, gather)
    inv = lax.rsqrt(ssq * (1.0 / HEAD_DIM) + RMS_EPS)
    mixed = (groups * exact_dot(inv, spread) * gout_ref[...]).astype(BF16)
    assert n_groups <= LANES
    o_ref[...] = x_ref[...] + jnp.dot(mixed, wout_ref[...], preferred_element_type=F32)


def _mixout(x2, p2, ocmp, oslc, owin, oc, od, bg, cw, gout, wout, seq):
    t, d = x2.shape
    tm = 256
    rows = lambda w, col: pl.BlockSpec((tm, w), lambda i: (i, col // w))
    prev8 = lambda col: pl.BlockSpec((8, GROUP_W), lambda i: (jnp.maximum(i * (tm // 8) - 1, 0), col // GROUP_W))
    const = lambda shape: pl.BlockSpec(shape, lambda i: (0, 0))
    return pl.pallas_call(
        functools.partial(_mixout_kernel, tiles_per_seq=seq // tm),
        grid=(t // tm,),
        in_specs=[rows(d, 0), rows(GROUP_W, 0), rows(GROUP_W, 0), rows(GROUP_W, 0),
                  rows(LANES, COL_GATE), rows(GROUP_W, COL_CVB), rows(GROUP_W, COL_CVC), rows(GROUP_W, COL_CVU),
                  prev8(COL_CVC), prev8(COL_CVU),
                  rows(GROUP_W, 0), rows(GROUP_W, 0),
                  const((1, LANES)), const((8, GROUP_W)), const((1, d)), const((d, d))],
        out_specs=rows(d, 0),
        out_shape=jax.ShapeDtypeStruct((t, d), F32),
        compiler_params=pltpu.CompilerParams(dimension_semantics=("parallel",), vmem_limit_bytes=VMEM_LIMIT),
        name="mixout",
    )(x2, ocmp, oslc, owin, p2, p2, p2, p2, p2, p2, oc, od, bg, cw, gout, wout)


def _ffn_kernel(x_ref, g_ref, wg_ref, wu_ref, wd_ref, o_ref, h_s):
    j = pl.program_id(1)

    @pl.when(j == 0)
    def _start():
        x = x_ref[...]
        h_s[...] = _rms(x, g_ref[...]).astype(BF16)
        o_ref[...] = x

    h = h_s[...]
    a = jnp.dot(h, wg_ref[...], preferred_element_type=F32)
    u = jnp.dot(h, wu_ref[...], preferred_element_type=F32)
    act = (a * jax.nn.sigmoid(a) * u).astype(BF16)
    o_ref[...] += jnp.dot(act, wd_ref[...], preferred_element_type=F32)


def _ffn(x2, g, wg, wu, wd):
    t, d = x2.shape
    dff = wg.shape[1]
    tm, tf = 1024, 256
    return pl.pallas_call(
        _ffn_kernel,
        grid=(t // tm, dff // tf),
        in_specs=[pl.BlockSpec((tm, d), lambda i, j: (i, 0)),
                  pl.BlockSpec((1, d), lambda i, j: (0, 0)),
                  pl.BlockSpec((d, tf), lambda i, j: (0, j)),
                  pl.BlockSpec((d, tf), lambda i, j: (0, j)),
                  pl.BlockSpec((tf, d), lambda i, j: (j, 0))],
        out_specs=pl.BlockSpec((tm, d), lambda i, j: (i, 0)),
        out_shape=jax.ShapeDtypeStruct((t, d), F32),
        scratch_shapes=[pltpu.VMEM((tm, d), BF16)],
        compiler_params=pltpu.CompilerParams(dimension_semantics=("parallel", "arbitrary"),
                                             vmem_limit_bytes=VMEM_LIMIT),
        name="ffn",
    )(x2, g, wg, wu, wd)


def _permute_w_in_kernel(w_ref, o_ref):
    rows = w_ref.shape[0]
    n_rest = P_COLS - COL_CVB
    o_ref[:, 0:COL_GATE] = w_ref[:, 0:COL_GATE].astype(BF16)
    lane = lax.broadcasted_iota(jnp.int32, (rows, LANES), 1)
    o_ref[:, COL_GATE:COL_GATE + LANES] = jnp.where(lane < N_GATES, w_ref[:, COL_GATE:COL_GATE + LANES], 0.0).astype(BF16)
    o_ref[:, COL_CVB:P_COLS] = w_ref[:, COL_GATE + N_GATES:COL_GATE + N_GATES + n_rest].astype(BF16)


def _permute_w_in(w_in):
    d, cols = w_in.shape
    tr = 256
    return pl.pallas_call(
        _permute_w_in_kernel,
        grid=(d // tr,),
        in_specs=[pl.BlockSpec((tr, cols), lambda i: (i, 0))],
        out_specs=pl.BlockSpec((tr, P_COLS), lambda i: (i, 0)),
        out_shape=jax.ShapeDtypeStruct((d, P_COLS), BF16),
        compiler_params=pltpu.CompilerParams(dimension_semantics=("parallel",), vmem_limit_bytes=VMEM_LIMIT),
        name="permute_w_in",
    )(w_in)


def _layer(x2, batch, seq, g_mix, w_in, b_gate, g_q_nsa, g_k_cmp, g_k_slc, g_k_win, pe_k_cmp, pe_v_cmp,
           w1_k_cmp, w2_k_cmp, w1_v_cmp, w2_v_cmp, conv_w, g_q_dil, g_k_dil, g_out, w_out,
           g_ffn, w_gate, w_up, w_down):
    row = lambda v: v.reshape(1, -1)
    p2 = _inproj(x2, row(g_mix), _permute_w_in(w_in))

    pe, w1, w2 = _compress_weights(pe_k_cmp, pe_v_cmp, w1_k_cmp, w2_k_cmp, w1_v_cmp, w2_v_cmp)
    kc, vc = _compress(p2, pe, w1, w2, row(g_k_cmp), batch, seq)
    ocmp, oslc = _nsa(p2, kc, vc, row(g_q_nsa), row(g_k_slc), batch, seq)
    owin = _banded(p2, row(g_q_nsa), row(g_k_win), jnp.asarray(_band_bias_tiles(NSA_WINDOW - 1, _window_mult)),
                   batch, seq, window=NSA_WINDOW - 1, slopes=SLOPES_A, shared_kv=True,
                   col_q=COL_QA, col_k=COL_KWV, col_v=COL_KWV, name="nsa_window")
    od = _banded(p2, row(g_q_dil), row(g_k_dil),
                 jnp.asarray(_band_bias_tiles(DILATED_CONFIGS[-1][0], _dilated_mult)),
                 batch, seq, window=DILATED_CONFIGS[-1][0], slopes=SLOPES_D, shared_kv=False,
                 col_q=COL_QD, col_k=COL_KD, col_v=COL_VD, name="dilated")
    oc = _stick(p2, batch, seq)

    bg = jnp.zeros((1, LANES), F32).at[0, :N_GATES].set(b_gate)
    cwp = jnp.zeros((8, GROUP_W), F32).at[:CONV_K].set(conv_w)
    x1 = _mixout(x2, p2, ocmp, oslc, owin, oc, od, bg, cwp, row(g_out), w_out.astype(BF16), seq)
    return _ffn(x1, row(g_ffn), w_gate.astype(BF16), w_up.astype(BF16), w_down.astype(BF16))


def kernel(x, g_mix, w_in, b_gate, g_q_nsa, g_k_cmp, g_k_slc, g_k_win, pe_k_cmp, pe_v_cmp, w1_k_cmp, w2_k_cmp,
           w1_v_cmp, w2_v_cmp, conv_w, g_q_dil, g_k_dil, g_out, w_out, g_ffn, w_gate, w_up, w_down):
    batch, seq, d = x.shape
    assert seq % (CMP_STRIDE * LANES) == 0 and d % LANES == 0
    x2 = x.reshape(batch * seq, d)
    params = (g_mix, w_in, b_gate, g_q_nsa, g_k_cmp, g_k_slc, g_k_win, pe_k_cmp, pe_v_cmp, w1_k_cmp, w2_k_cmp,
              w1_v_cmp, w2_v_cmp, conv_w, g_q_dil, g_k_dil, g_out, w_out, g_ffn, w_gate, w_up, w_down)
    for layer in range(g_mix.shape[0]):
        x2 = _layer(x2, batch, seq, *[p[layer] for p in params])
    return x2.reshape(batch, seq, d)
```

```python
import functools

import numpy as np
import jax
import jax.numpy as jnp
from jax import lax
from jax.experimental import pallas as pl
from jax.experimental.pallas import tpu as pltpu

F32 = jnp.float32
BF16 = jnp.bfloat16

HEAD_DIM = 64
N_HEADS = 4
GROUP_W = N_HEADS * HEAD_DIM
CONV_K = 3
CMP_LEN = 32
CMP_STRIDE = 16
SEL_BLOCK = 64
N_SELECT = 16
NSA_WINDOW = 512
DILATED_CONFIGS = ((128, 1), (512, 4), (2048, 16))
NEG_INF = -1e30
FORCE_SCORE = 1e6
RMS_EPS = 1e-6

TQ = 256
TK = 256
LANES = 128
SUM_PAD = 16
LOG2E = 1.4426950408889634
VMEM_LIMIT = 52 * 1024 * 1024

COL_QA, COL_KVC, COL_KSV, COL_KWV, COL_GATE = 0, 256, 384, 512, 640
COL_CVB, COL_CVC, COL_CVU = 768, 1024, 1280
COL_QC, COL_KC, COL_VC = 1536, 1792, 2048
COL_QD, COL_KD, COL_VD = 2304, 2560, 2816
P_COLS = 3072
N_GATES = 12

_NT = (((1,), (1,)), ((), ()))


def _alibi_slopes():
    s = [2.0 ** (-8.0 * i / 8) for i in range(1, 9)]
    return tuple(s[0::2]), tuple(s[1::2])


SLOPES_A, SLOPES_D = _alibi_slopes()


def _rms(x, g):
    return x * lax.rsqrt(jnp.mean(x * x, axis=-1, keepdims=True) + RMS_EPS) * g


def _key_aug(pos):
    rows = pos.shape[0]
    lane = lax.broadcasted_iota(jnp.int32, (rows, HEAD_DIM), 1)
    hi = (pos >> 6).astype(F32)
    lo = (pos & 63).astype(F32)
    return jnp.where(lane == 0, hi, jnp.where(lane == 1, lo, jnp.where(lane < 4, 1.0, 0.0)))


def _query_aug(t, slope):
    rows = t.shape[0]
    lane = lax.broadcasted_iota(jnp.int32, (rows, HEAD_DIM), 1)
    hi = (t >> 6).astype(F32) * (-64.0 * slope)
    lo = (t & 63).astype(F32) * (-slope)
    return jnp.where(lane == 0, 64.0 * slope,
                     jnp.where(lane == 1, slope, jnp.where(lane == 2, hi, jnp.where(lane == 3, lo, 0.0))))


def _prep_queries(q, g, t0, slopes, qp_s):
    tpos = t0 + lax.broadcasted_iota(jnp.int32, (q.shape[0], 1), 0)
    for h in range(N_HEADS):
        qh = q[:, h * HEAD_DIM:(h + 1) * HEAD_DIM]
        if g is not None:
            qh = _rms(qh, g)
        qh = qh * (HEAD_DIM ** -0.5)
        aug = jnp.zeros_like(qh) if slopes is None else _query_aug(tpos, slopes[h])
        qp_s[h] = jnp.concatenate([qh, aug], axis=1).astype(BF16)


def _inproj_kernel(x_ref, g_ref, w_ref, o_ref, *, nchunk):
    h = _rms(x_ref[...], g_ref[...]).astype(BF16)
    cw = P_COLS // nchunk
    for c in range(nchunk):
        o_ref[:, c * cw:(c + 1) * cw] = jnp.dot(h, w_ref[:, c * cw:(c + 1) * cw], preferred_element_type=F32)


def _inproj(x2, g, w):
    t, d = x2.shape
    tm = 512
    return pl.pallas_call(
        functools.partial(_inproj_kernel, nchunk=6),
        grid=(t // tm,),
        in_specs=[pl.BlockSpec((tm, d), lambda i: (i, 0)),
                  pl.BlockSpec((1, d), lambda i: (0, 0)),
                  pl.BlockSpec((d, P_COLS), lambda i: (0, 0))],
        out_specs=pl.BlockSpec((tm, P_COLS), lambda i: (i, 0)),
        out_shape=jax.ShapeDtypeStruct((t, P_COLS), F32),
        compiler_params=pltpu.CompilerParams(dimension_semantics=("parallel",), vmem_limit_bytes=VMEM_LIMIT),
        name="inproj",
    )(x2, g, w)


def _gelu_tanh(x):
    return x * (0.5 * (1.0 + jnp.tanh(np.sqrt(2.0 / np.pi).astype(np.float32) * (x + 0.044715 * (x * x * x)))))


def _compress_kernel(kv_ref, pe_ref, w1_ref, w2_ref, gk_ref, kc_ref, vct_ref):
    nc = kv_ref.shape[0] // CMP_STRIDE
    first = None
    second = None
    for j in range(CMP_STRIDE):
        tok = kv_ref[pl.ds(j, nc, stride=CMP_STRIDE), :]
        a = jnp.dot((tok + pe_ref[j:j + 1, :]).astype(BF16), w1_ref[j], preferred_element_type=F32)
        b = jnp.dot((tok + pe_ref[CMP_STRIDE + j:CMP_STRIDE + j + 1, :]).astype(BF16), w1_ref[CMP_STRIDE + j],
                    preferred_element_type=F32)
        first = a if first is None else first + a
        second = b if second is None else second + b
    hid = first + pltpu.roll(second, nc - 1, 0)
    out = jnp.dot(_gelu_tanh(hid).astype(BF16), w2_ref[...], preferred_element_type=F32)
    kc = _rms(out[:, :HEAD_DIM], gk_ref[...])
    n = lax.broadcasted_iota(jnp.int32, (nc, 1), 0)
    kc_ref[0] = jnp.concatenate([kc, _key_aug(n * CMP_STRIDE + (CMP_LEN - 1))], axis=1).astype(BF16)
    vct_ref[0] = out.T[HEAD_DIM:, :].astype(BF16)


def _compress(p2, pe, w1, w2, gk, batch, seq):
    nc = seq // CMP_STRIDE
    hid2 = w1.shape[2]
    return pl.pallas_call(
        _compress_kernel,
        grid=(batch,),
        in_specs=[pl.BlockSpec((seq, LANES), lambda i: (i, COL_KVC // LANES)),
                  pl.BlockSpec((CMP_LEN, LANES), lambda i: (0, 0)),
                  pl.BlockSpec((CMP_LEN, LANES, hid2), lambda i: (0, 0, 0)),
                  pl.BlockSpec((hid2, LANES), lambda i: (0, 0)),
                  pl.BlockSpec((1, HEAD_DIM), lambda i: (0, 0))],
        out_specs=[pl.BlockSpec((1, nc, LANES), lambda i: (i, 0, 0)),
                   pl.BlockSpec((1, HEAD_DIM, nc), lambda i: (i, 0, 0))],
        out_shape=[jax.ShapeDtypeStruct((batch, nc, LANES), BF16),
                   jax.ShapeDtypeStruct((batch, HEAD_DIM, nc), BF16)],
        compiler_params=pltpu.CompilerParams(dimension_semantics=("parallel",), vmem_limit_bytes=VMEM_LIMIT),
        name="nsa_compress",
    )(p2, pe, w1, w2, gk)


def _compress_weights(pe_k, pe_v, w1_k, w2_k, w1_v, w2_v):
    hid = w1_k.shape[1]
    w1k = w1_k.reshape(CMP_LEN, HEAD_DIM, hid)
    w1v = w1_v.reshape(CMP_LEN, HEAD_DIM, hid)
    z1 = jnp.zeros_like(w1k)
    w1 = jnp.concatenate([jnp.concatenate([w1k, z1], axis=2), jnp.concatenate([z1, w1v], axis=2)], axis=1)
    z2 = jnp.zeros_like(w2_k)
    w2 = jnp.concatenate([jnp.concatenate([w2_k, z2], axis=1), jnp.concatenate([z2, w2_v], axis=1)], axis=0)
    pe = jnp.concatenate([pe_k, pe_v], axis=1)
    return pe, w1.astype(BF16), w2.astype(BF16)


def _flash_scores(kp_of, bias, qp_s, s_s, m_cur):
    m_next = []
    for h in range(N_HEADS):
        s = lax.dot_general(kp_of(h), qp_s[h], _NT, preferred_element_type=F32) + bias
        s_s[h] = s
        m_next.append(jnp.maximum(m_cur[h], jnp.max(s, axis=0, keepdims=True)))
    return tuple(m_next)


def _flash_accum(vt_of, s_s, acc_s, m_prev, m_cur, ls):
    new_l = []
    for h in range(N_HEADS):
        alpha = jnp.exp(m_prev[h] - m_cur[h])
        p = jnp.exp(s_s[h] - m_cur[h])
        new_l.append(alpha * ls[h] + jnp.sum(p, axis=0, keepdims=True))
        acc_s[h] = alpha * acc_s[h] + jnp.dot(vt_of(h), p.astype(BF16), preferred_element_type=F32)
    return tuple(new_l)


def _flash_accum_and_scores(kp_of, bias, vt_of, qp_s, s_s, acc_s, m_prev, m_cur, ls):
    m_next, new_l = [], []
    for h in range(N_HEADS):
        s_new = lax.dot_general(kp_of(h), qp_s[h], _NT, preferred_element_type=F32) + bias
        alpha = jnp.exp(m_prev[h] - m_cur[h])
        p = jnp.exp(s_s[h] - m_cur[h])
        new_l.append(alpha * ls[h] + jnp.sum(p, axis=0, keepdims=True))
        acc_s[h] = alpha * acc_s[h] + jnp.dot(vt_of(h), p.astype(BF16), preferred_element_type=F32)
        s_s[h] = s_new
        m_next.append(jnp.maximum(m_cur[h], jnp.max(s_new, axis=0, keepdims=True)))
    return tuple(m_next), tuple(new_l)


def _flash_reverse(first_kt, n_tiles, kp_fn, bias_fn, vt_fn, qp_s, s_s, acc_s, o_ref):
    acc_s[...] = jnp.zeros(acc_s.shape, F32)
    m0 = tuple(jnp.full((1, TQ), NEG_INF, F32) for _ in range(N_HEADS))
    l0 = tuple(jnp.zeros((1, TQ), F32) for _ in range(N_HEADS))
    m1 = _flash_scores(kp_fn(first_kt), bias_fn(first_kt, True), qp_s, s_s, m0)

    def body(i, carry):
        m_prev, m_cur, ls = carry
        kt = first_kt - i
        m_next, ls = _flash_accum_and_scores(kp_fn(kt - 1), bias_fn(kt - 1, False), vt_fn(kt), qp_s, s_s, acc_s,
                                             m_prev, m_cur, ls)
        return m_cur, m_next, ls

    m_prev, m_cur, ls = lax.fori_loop(0, n_tiles - 1, body, (m0, m1, l0))
    ls = _flash_accum(vt_fn(first_kt - (n_tiles - 1)), s_s, acc_s, m_prev, m_cur, ls)
    out = jnp.concatenate([acc_s[h] * (1.0 / ls[h]) for h in range(N_HEADS)], axis=0)
    o_ref[...] = out.T


def _nsa_kernel(q_ref, kc_ref, vct_ref, ksv_ref, gq_ref, gks_ref, ocmp_ref, oslc_ref,
                ksp_s, vst_s, qp_s, imp_s, sel_s, s_s, acc_s, *, seq):
    qi = pl.program_id(1)
    nc = seq // CMP_STRIDE
    nsel = seq // SEL_BLOCK
    nkt = seq // TK

    @pl.when(qi == 0)
    def _prep():
        for c in range(nkt):
            rows = slice(c * TK, (c + 1) * TK)
            blk = ksv_ref[rows, :]
            pos = c * TK + lax.broadcasted_iota(jnp.int32, (TK, 1), 0)
            kn = _rms(blk[:, :HEAD_DIM], gks_ref[...])
            ksp_s[rows, :] = jnp.concatenate([kn, _key_aug(pos)], axis=1).astype(BF16)
            vst_s[c] = blk.T[HEAD_DIM:, :].astype(BF16)
        imp_s[:, 0:8, :] = jnp.zeros((TQ // LANES, 8, LANES), F32)

    t0 = qi * TQ
    _prep_queries(q_ref[...], gq_ref[...], t0, SLOPES_A, qp_s)

    kc = kc_ref[0]
    vct = vct_ref[0]
    tq_row = t0 + lax.broadcasted_iota(jnp.int32, (nc, TQ), 1)
    n_idx = lax.broadcasted_iota(jnp.int32, (nc, TQ), 0)
    vis = (tq_row >= n_idx * CMP_STRIDE + (CMP_LEN - 1)) & (n_idx < nc - 1)
    vis_bias = jnp.where(vis, 0.0, NEG_INF)
    sees_any = (t0 + lax.broadcasted_iota(jnp.int32, (1, TQ), 1) >= CMP_LEN - 1).astype(F32)
    imp = jnp.zeros((nc, TQ), F32)
    ocmp_t = []
    for h in range(N_HEADS):
        sc = lax.dot_general(kc, qp_s[h], _NT, preferred_element_type=F32) + vis_bias
        e = jnp.exp(sc - jnp.max(sc, axis=0, keepdims=True))
        p = e * (sees_any / jnp.sum(e, axis=0, keepdims=True))
        ocmp_t.append(jnp.dot(vct, p.astype(BF16), preferred_element_type=F32))
        imp = imp + p
    ocmp_ref[...] = jnp.concatenate(ocmp_t, axis=0).T

    halves = []
    for half in range(TQ // LANES):
        imp_s[half, 8:8 + nc, :] = imp[:, half * LANES:(half + 1) * LANES]
        r = [imp_s[half, pl.ds(8 + k, nsel, stride=4), :] for k in range(4)]
        rm1 = imp_s[half, pl.ds(7, nsel, stride=4), :]
        halves.append(rm1 + 2.0 * (r[0] + r[1] + r[2]) + r[3])
    imp_blk = jnp.concatenate(halves, axis=1)
    blk = lax.broadcasted_iota(jnp.int32, (nsel, TQ), 0)
    tl = t0 + lax.broadcasted_iota(jnp.int32, (nsel, TQ), 1)
    cur = tl >> 6
    forced = (blk == 0) | (blk == cur) | (blk == cur - 1)
    valid = blk * SEL_BLOCK <= tl
    score = jnp.where(forced, FORCE_SCORE, jnp.where(valid, imp_blk, -FORCE_SCORE))
    sub = 8
    for g in range(nsel // sub):
        mine = score[g * sub:(g + 1) * sub, :]
        blk_g = g * sub + lax.broadcasted_iota(jnp.int32, (sub, TQ), 0)
        rank = jnp.zeros((sub, TQ), F32)
        for j in range(nsel):
            row = score[j:j + 1, :]
            if j < g * sub:
                beats = row >= mine
            elif j >= (g + 1) * sub:
                beats = row > mine
            else:
                beats = (row > mine) | ((row == mine) & (blk_g > j))
            rank = rank + jnp.where(beats, 1.0, 0.0)
        sel_s[g * sub:(g + 1) * sub, :] = jnp.where(rank < min(N_SELECT, nsel), 0.0, NEG_INF)

    def block_bias(kt):
        per_tile = TK // SEL_BLOCK
        return jnp.concatenate([jnp.broadcast_to(sel_s[pl.ds(kt * per_tile + j, 1), :], (SEL_BLOCK, TQ))
                                for j in range(per_tile)], axis=0)

    causal = lax.broadcasted_iota(jnp.int32, (TK, TQ), 0) <= lax.broadcasted_iota(jnp.int32, (TK, TQ), 1)

    def kp_fn(kt):
        kp = ksp_s[pl.ds(pl.multiple_of(kt * TK, TK), TK), :]
        return lambda h: kp

    def bias_fn(kt, first):
        return jnp.where(causal, block_bias(kt), NEG_INF) if first else block_bias(kt)

    _flash_reverse(qi, qi + 1, kp_fn, bias_fn, lambda kt: (lambda h: vst_s[kt]), qp_s, s_s, acc_s, oslc_ref)


def _nsa(p2, kc, vc, gq, gks, batch, seq):
    nq = seq // TQ
    nc = seq // CMP_STRIDE
    t = batch * seq
    return pl.pallas_call(
        functools.partial(_nsa_kernel, seq=seq),
        grid=(batch, nq),
        in_specs=[pl.BlockSpec((TQ, GROUP_W), lambda b, i: (b * nq + i, COL_QA // GROUP_W)),
                  pl.BlockSpec((1, nc, LANES), lambda b, i: (b, 0, 0)),
                  pl.BlockSpec((1, HEAD_DIM, nc), lambda b, i: (b, 0, 0)),
                  pl.BlockSpec((seq, LANES), lambda b, i: (b, COL_KSV // LANES)),
                  pl.BlockSpec((1, HEAD_DIM), lambda b, i: (0, 0)),
                  pl.BlockSpec((1, HEAD_DIM), lambda b, i: (0, 0))],
        out_specs=[pl.BlockSpec((TQ, GROUP_W), lambda b, i: (b * nq + i, 0)),
                   pl.BlockSpec((TQ, GROUP_W), lambda b, i: (b * nq + i, 0))],
        out_shape=[jax.ShapeDtypeStruct((t, GROUP_W), F32), jax.ShapeDtypeStruct((t, GROUP_W), F32)],
        scratch_shapes=[pltpu.VMEM((seq, LANES), BF16),
                        pltpu.VMEM((seq // TK, HEAD_DIM, TK), BF16),
                        pltpu.VMEM((N_HEADS, TQ, LANES), BF16),
                        pltpu.VMEM((TQ // LANES, 8 + nc, LANES), F32),
                        pltpu.VMEM((seq // SEL_BLOCK, TQ), F32),
                        pltpu.VMEM((N_HEADS, TK, TQ), F32),
                        pltpu.VMEM((N_HEADS, HEAD_DIM, TQ), F32)],
        compiler_params=pltpu.CompilerParams(dimension_semantics=("parallel", "arbitrary"),
                                             vmem_limit_bytes=VMEM_LIMIT),
        name="nsa_cmp_slc",
    )(p2, kc, vc, p2, gq, gks)


def _band_bias_tiles(window, mult_fn):
    nd = (window + TK - 1) // TK + 1
    kk = np.arange(TK)[:, None]
    qq = np.arange(TQ)[None, :]
    tiles = np.empty((nd, TK, TQ), np.float32)
    for di in range(nd):
        d = di * TK + qq - kk
        mult = mult_fn(d)
        tiles[di] = np.where(mult > 0, np.log(np.maximum(mult, 1)), NEG_INF)
    return tiles


def _window_mult(d):
    return ((d >= 0) & (d <= NSA_WINDOW - 1)).astype(np.float64)


def _dilated_mult(d):
    m = np.zeros(d.shape, np.float64)
    for window, dil in DILATED_CONFIGS:
        m += ((d >= 0) & (d <= window) & (d % dil == 0)).astype(np.float64)
    return m


def _banded_kernel(*refs, seq, window, slopes, shared_kv):
    if shared_kv:
        q_ref, kv_ref, gq_ref, gk_ref, bias_ref, o_ref, kp_s, vt_s, qp_s, s_s, acc_s = refs
    else:
        q_ref, k_ref, v_ref, gq_ref, gk_ref, bias_ref, o_ref, kp_s, vt_s, qp_s, s_s, acc_s = refs
    qi = pl.program_id(1)
    nkt = seq // TK

    @pl.when(qi == 0)
    def _prep():
        for c in range(nkt):
            rows = slice(c * TK, (c + 1) * TK)
            aug = _key_aug(c * TK + lax.broadcasted_iota(jnp.int32, (TK, 1), 0))
            if shared_kv:
                blk = kv_ref[rows, :]
                kn = _rms(blk[:, :HEAD_DIM], gk_ref[...])
                kp_s[rows, :] = jnp.concatenate([kn, aug], axis=1).astype(BF16)
                vt_s[c] = blk.T[HEAD_DIM:, :].astype(BF16)
            else:
                kb = k_ref[rows, :]
                parts = []
                for h in range(N_HEADS):
                    parts += [_rms(kb[:, h * HEAD_DIM:(h + 1) * HEAD_DIM], gk_ref[...]), aug]
                kp_s[rows, :] = jnp.concatenate(parts, axis=1).astype(BF16)
                vt_s[c] = v_ref[rows, :].T.astype(BF16)

    t0 = qi * TQ
    _prep_queries(q_ref[...], gq_ref[...], t0, slopes, qp_s)
    kt_lo = jnp.maximum(t0 - window, 0) // TK

    def kp_fn(kt):
        off = pl.multiple_of(kt * TK, TK)
        if shared_kv:
            kp = kp_s[pl.ds(off, TK), :]
            return lambda h: kp
        return lambda h: kp_s[pl.ds(off, TK), h * LANES:(h + 1) * LANES]

    def vt_fn(kt):
        if shared_kv:
            return lambda h: vt_s[kt]
        return lambda h: vt_s[kt, h * HEAD_DIM:(h + 1) * HEAD_DIM, :]

    _flash_reverse(qi, qi + 1 - kt_lo, kp_fn, lambda kt, first: bias_ref[qi - kt], vt_fn, qp_s, s_s, acc_s, o_ref)


def _banded(p2, gq, gk, bias_tiles, batch, seq, *, window, slopes, shared_kv, col_q, col_k, col_v, name):
    nq = seq // TQ
    t = batch * seq
    nd = bias_tiles.shape[0]
    n_kv = 1 if shared_kv else N_HEADS
    q_spec = pl.BlockSpec((TQ, GROUP_W), lambda b, i: (b * nq + i, col_q // GROUP_W))
    small = pl.BlockSpec((1, HEAD_DIM), lambda b, i: (0, 0))
    bias_spec = pl.BlockSpec((nd, TK, TQ), lambda b, i: (0, 0, 0))
    if shared_kv:
        kv_specs = [pl.BlockSpec((seq, LANES), lambda b, i: (b, col_k // LANES))]
        operands = (p2, p2, gq, gk, bias_tiles)
    else:
        kv_specs = [pl.BlockSpec((seq, GROUP_W), lambda b, i: (b, col_k // GROUP_W)),
                    pl.BlockSpec((seq, GROUP_W), lambda b, i: (b, col_v // GROUP_W))]
        operands = (p2, p2, p2, gq, gk, bias_tiles)
    return pl.pallas_call(
        functools.partial(_banded_kernel, seq=seq, window=window, slopes=slopes, shared_kv=shared_kv),
        grid=(batch, nq),
        in_specs=[q_spec] + kv_specs + [small, small, bias_spec],
        out_specs=pl.BlockSpec((TQ, GROUP_W), lambda b, i: (b * nq + i, 0)),
        out_shape=jax.ShapeDtypeStruct((t, GROUP_W), F32),
        scratch_shapes=[pltpu.VMEM((seq, n_kv * LANES), BF16),
                        pltpu.VMEM((seq // TK, n_kv * HEAD_DIM, TK), BF16),
                        pltpu.VMEM((N_HEADS, TQ, LANES), BF16),
                        pltpu.VMEM((N_HEADS, TK, TQ), F32),
                        pltpu.VMEM((N_HEADS, HEAD_DIM, TQ), F32)],
        compiler_params=pltpu.CompilerParams(dimension_semantics=("parallel", "arbitrary"),
                                             vmem_limit_bytes=VMEM_LIMIT),
        name=name,
    )(*operands)


def _stick_kernel(q_ref, k_ref, v_ref, o_ref, kp_s, vt_s, qp_s, hl_s, e_s, acc_s, *, seq):
    qi = pl.program_id(1)
    nkt = seq // TK

    @pl.when(qi == 0)
    def _prep():
        zeros = jnp.zeros((TK, HEAD_DIM), F32)
        for c in range(nkt):
            rows = slice(c * TK, (c + 1) * TK)
            kb = k_ref[rows, :]
            parts = []
            for h in range(N_HEADS):
                parts += [kb[:, h * HEAD_DIM:(h + 1) * HEAD_DIM], zeros]
            kp_s[rows, :] = jnp.concatenate(parts, axis=1).astype(BF16)
            vt_s[c] = v_ref[rows, :].T.astype(BF16)

    _prep_queries(q_ref[...], None, qi * TQ, None, qp_s)
    half = TK // 2
    col = lax.broadcasted_iota(jnp.int32, (half + SUM_PAD, TK), 1) & (half - 1)
    srow = lax.broadcasted_iota(jnp.int32, (half + SUM_PAD, TK), 0)
    sums = ((srow == half) | ((srow < half) & (col > srow))).astype(BF16)
    past = lax.broadcasted_iota(jnp.int32, (TK, TQ), 0) < lax.broadcasted_iota(jnp.int32, (TK, TQ), 1)
    acc_s[...] = jnp.zeros(acc_s.shape, F32)

    def logits(kt, h):
        off = pl.multiple_of(kt * TK, TK)
        return lax.dot_general(kp_s[pl.ds(off, TK), h * LANES:(h + 1) * LANES], qp_s[h], _NT,
                               preferred_element_type=F32)

    def stage(h, z, masked):
        sp = jnp.maximum(z, 0.0) + jnp.log(1.0 + jnp.exp2(jnp.abs(z) * (-LOG2E)))
        spm = jnp.where(past, sp, 0.0) if masked else sp
        hi = spm.astype(BF16)
        lo = (spm - hi.astype(F32)).astype(BF16)
        for b in range(2):
            hl_s[h, b, 0:half, :] = hi[b * half:(b + 1) * half, :]
            hl_s[h, b, half:TK, :] = lo[b * half:(b + 1) * half, :]
        logsig = z - sp
        e_s[h] = jnp.where(past, logsig, NEG_INF) if masked else logsig

    def tails(h):
        return [jnp.dot(sums, hl_s[h, b], preferred_element_type=F32) for b in range(2)]

    def consume(kt, h, w, later):
        later_lo = later + w[1][half:half + 1, :]
        attn = jnp.concatenate([jnp.exp(e_s[h, 0:half, :] - w[0][0:half, :] - later_lo),
                                jnp.exp(e_s[h, half:TK, :] - w[1][0:half, :] - later)], axis=0)
        acc_s[h] += jnp.dot(vt_s[kt, h * HEAD_DIM:(h + 1) * HEAD_DIM, :], attn.astype(BF16),
                            preferred_element_type=F32)
        return later_lo + w[0][half:half + 1, :]

    ahead = 2
    for h in range(N_HEADS):
        stage(h, logits(qi, h), True)

    def body(i, laters):
        kt = qi - i
        w = {h: tails(h) for h in range(ahead)}
        z = {h: logits(kt - 1, h) for h in range(ahead)}
        out = []
        for h in range(N_HEADS):
            out.append(consume(kt, h, w.pop(h), laters[h]))
            stage(h, z.pop(h), False)
            if h + ahead < N_HEADS:
                w[h + ahead] = tails(h + ahead)
                z[h + ahead] = logits(kt - 1, h + ahead)
        return tuple(out)

    laters = lax.fori_loop(0, qi, body, tuple(jnp.zeros((1, TQ), F32) for _ in range(N_HEADS)))
    w = {h: tails(h) for h in range(ahead)}
    for h in range(N_HEADS):
        consume(0, h, w.pop(h), laters[h])
        if h + ahead < N_HEADS:
            w[h + ahead] = tails(h + ahead)
    o_ref[...] = jnp.concatenate([acc_s[h] for h in range(N_HEADS)], axis=0).T


def _stick(p2, batch, seq):
    nq = seq // TQ
    t = batch * seq
    return pl.pallas_call(
        functools.partial(_stick_kernel, seq=seq),
        grid=(batch, nq),
        in_specs=[pl.BlockSpec((TQ, GROUP_W), lambda b, i: (b * nq + i, COL_QC // GROUP_W)),
                  pl.BlockSpec((seq, GROUP_W), lambda b, i: (b, COL_KC // GROUP_W)),
                  pl.BlockSpec((seq, GROUP_W), lambda b, i: (b, COL_VC // GROUP_W))],
        out_specs=pl.BlockSpec((TQ, GROUP_W), lambda b, i: (b * nq + i, 0)),
        out_shape=jax.ShapeDtypeStruct((t, GROUP_W), F32),
        scratch_shapes=[pltpu.VMEM((seq, N_HEADS * LANES), BF16),
                        pltpu.VMEM((seq // TK, GROUP_W, TK), BF16),
                        pltpu.VMEM((N_HEADS, TQ, LANES), BF16),
                        pltpu.VMEM((N_HEADS, 2, TK, TQ), BF16),
                        pltpu.VMEM((N_HEADS, TK, TQ), F32),
                        pltpu.VMEM((N_HEADS, HEAD_DIM, TQ), F32)],
        compiler_params=pltpu.CompilerParams(dimension_semantics=("parallel", "arbitrary"),
                                             vmem_limit_bytes=VMEM_LIMIT),
        name="stick_breaking",
    )(p2, p2, p2)


def _mixout_kernel(x_ref, ocmp_ref, oslc_ref, owin_ref, gate_ref, cvb_ref, cvc_ref, cvu_ref, pc_ref, pu_ref,
                   oc_ref, od_ref, bg_ref, cw_ref, gout_ref, wout_ref, o_ref, *, tiles_per_seq):
    i = pl.program_id(0)
    d = x_ref.shape[1]

    def exact_dot(a, sel):
        hi = a.astype(BF16)
        lo = (a - hi.astype(F32)).astype(BF16)
        return jnp.dot(hi, sel, preferred_element_type=F32) + jnp.dot(lo, sel, preferred_element_type=F32)

    gates = jax.nn.sigmoid(gate_ref[...] + bg_ref[...])
    src = lax.broadcasted_iota(jnp.int32, (LANES, GROUP_W), 0)
    head3 = (lax.broadcasted_iota(jnp.int32, (LANES, GROUP_W), 1) >> 6) * 3
    oa = (exact_dot(gates, (src == head3).astype(BF16)) * ocmp_ref[...]
          + exact_dot(gates, (src == head3 + 1).astype(BF16)) * oslc_ref[...]
          + exact_dot(gates, (src == head3 + 2).astype(BF16)) * owin_ref[...])
    cu = cvc_ref[...] * cvu_ref[...]
    prev = jnp.where(i % tiles_per_seq == 0, 0.0, pc_ref[...] * pu_ref[...])
    full = jnp.concatenate([prev, cu], axis=0)
    back1 = pltpu.roll(full, 1, 0)[8:, :]
    back2 = pltpu.roll(full, 2, 0)[8:, :]
    cw = cw_ref[...]
    ob = cvb_ref[...] * (cw[0:1, :] * back2 + cw[1:2, :] * back1 + cw[2:3, :] * cu)
    groups = jnp.concatenate([oa, ob, oc_ref[...], od_ref[...]], axis=1)
    n_groups = d // HEAD_DIM
    gather = ((lax.broadcasted_iota(jnp.int32, (d, LANES), 0) >> 6)
              == lax.broadcasted_iota(jnp.int32, (d, LANES), 1)).astype(BF16)
    spread = (lax.broadcasted_iota(jnp.int32, (LANES, d), 0)
              == (lax.broadcasted_iota(jnp.int32, (LANES, d), 1) >> 6)).astype(BF16)
    ssq = exact_dot(groups * groups, gather)
    inv = lax.rsqrt(ssq * (1.0 / HEAD_DIM) + RMS_EPS)
    mixed = (groups * exact_dot(inv, spread) * gout_ref[...]).astype(BF16)
    assert n_groups <= LANES
    o_ref[...] = x_ref[...] + jnp.dot(mixed, wout_ref[...], preferred_element_type=F32)


def _mixout(x2, p2, ocmp, oslc, owin, oc, od, bg, cw, gout, wout, seq):
    t, d = x2.shape
    tm = 256
    rows = lambda w, col: pl.BlockSpec((tm, w), lambda i: (i, col // w))
    prev8 = lambda col: pl.BlockSpec((8, GROUP_W), lambda i: (jnp.maximum(i * (tm // 8) - 1, 0), col // GROUP_W))
    const = lambda shape: pl.BlockSpec(shape, lambda i: (0, 0))
    return pl.pallas_call(
        functools.partial(_mixout_kernel, tiles_per_seq=seq // tm),
        grid=(t // tm,),
        in_specs=[rows(d, 0), rows(GROUP_W, 0), rows(GROUP_W, 0), rows(GROUP_W, 0),
                  rows(LANES, COL_GATE), rows(GROUP_W, COL_CVB), rows(GROUP_W, COL_CVC), rows(GROUP_W, COL_CVU),
                  prev8(COL_CVC), prev8(COL_CVU),
                  rows(GROUP_W, 0), rows(GROUP_W, 0),
                  const((1, LANES)), const((8, GROUP_W)), const((1, d)), const((d, d))],
        out_specs=rows(d, 0),
        out_shape=jax.ShapeDtypeStruct((t, d), F32),
        compiler_params=pltpu.CompilerParams(dimension_semantics=("parallel",), vmem_limit_bytes=VMEM_LIMIT),
        name="mixout",
    )(x2, ocmp, oslc, owin, p2, p2, p2, p2, p2, p2, oc, od, bg, cw, gout, wout)


def _ffn_kernel(x_ref, g_ref, wg_ref, wu_ref, wd_ref, o_ref, act_s, *, ff_chunk, out_chunk):
    h = _rms(x_ref[...], g_ref[...]).astype(BF16)
    dff = wg_ref.shape[1]
    for c in range(dff // ff_chunk):
        cols = slice(c * ff_chunk, (c + 1) * ff_chunk)
        a = jnp.dot(h, wg_ref[:, cols], preferred_element_type=F32)
        u = jnp.dot(h, wu_ref[:, cols], preferred_element_type=F32)
        act_s[:, cols] = (a * jax.nn.sigmoid(a) * u).astype(BF16)
    d = o_ref.shape[1]
    for c in range(d // out_chunk):
        cols = slice(c * out_chunk, (c + 1) * out_chunk)
        o_ref[:, cols] = x_ref[:, cols] + jnp.dot(act_s[...], wd_ref[:, cols], preferred_element_type=F32)


def _ffn(x2, g, wg, wu, wd):
    t, d = x2.shape
    dff = wg.shape[1]
    tm = 512
    resident = lambda shape: pl.BlockSpec(shape, lambda i: (0, 0), pipeline_mode=pl.Buffered(1))
    return pl.pallas_call(
        functools.partial(_ffn_kernel, ff_chunk=256, out_chunk=256),
        grid=(t // tm,),
        in_specs=[pl.BlockSpec((tm, d), lambda i: (i, 0)),
                  pl.BlockSpec((1, d), lambda i: (0, 0)),
                  resident((d, dff)), resident((d, dff)), resident((dff, d))],
        out_specs=pl.BlockSpec((tm, d), lambda i: (i, 0)),
        out_shape=jax.ShapeDtypeStruct((t, d), F32),
        scratch_shapes=[pltpu.VMEM((tm, dff), BF16)],
        compiler_params=pltpu.CompilerParams(dimension_semantics=("parallel",), vmem_limit_bytes=VMEM_LIMIT),
        name="ffn",
    )(x2, g, wg, wu, wd)


def _permute_w_in_kernel(w_ref, o_ref):
    rows = w_ref.shape[0]
    n_rest = P_COLS - COL_CVB
    o_ref[:, 0:COL_GATE] = w_ref[:, 0:COL_GATE].astype(BF16)
    lane = lax.broadcasted_iota(jnp.int32, (rows, LANES), 1)
    o_ref[:, COL_GATE:COL_GATE + LANES] = jnp.where(lane < N_GATES, w_ref[:, COL_GATE:COL_GATE + LANES], 0.0).astype(BF16)
    o_ref[:, COL_CVB:P_COLS] = w_ref[:, COL_GATE + N_GATES:COL_GATE + N_GATES + n_rest].astype(BF16)


def _permute_w_in(w_in):
    d, cols = w_in.shape
    tr = 256
    return pl.pallas_call(
        _permute_w_in_kernel,
        grid=(d // tr,),
        in_specs=[pl.BlockSpec((tr, cols), lambda i: (i, 0))],
        out_specs=pl.BlockSpec((tr, P_COLS), lambda i: (i, 0)),
        out_shape=jax.ShapeDtypeStruct((d, P_COLS), BF16),
        compiler_params=pltpu.CompilerParams(dimension_semantics=("parallel",), vmem_limit_bytes=VMEM_LIMIT),
        name="permute_w_in",
    )(w_in)


def _layer(x2, batch, seq, g_mix, w_in, b_gate, g_q_nsa, g_k_cmp, g_k_slc, g_k_win, pe_k_cmp, pe_v_cmp,
           w1_k_cmp, w2_k_cmp, w1_v_cmp, w2_v_cmp, conv_w, g_q_dil, g_k_dil, g_out, w_out,
           g_ffn, w_gate, w_up, w_down):
    row = lambda v: v.reshape(1, -1)
    p2 = _inproj(x2, row(g_mix), _permute_w_in(w_in))

    pe, w1, w2 = _compress_weights(pe_k_cmp, pe_v_cmp, w1_k_cmp, w2_k_cmp, w1_v_cmp, w2_v_cmp)
    kc, vc = _compress(p2, pe, w1, w2, row(g_k_cmp), batch, seq)
    ocmp, oslc = _nsa(p2, kc, vc, row(g_q_nsa), row(g_k_slc), batch, seq)
    owin = _banded(p2, row(g_q_nsa), row(g_k_win), jnp.asarray(_band_bias_tiles(NSA_WINDOW - 1, _window_mult)),
                   batch, seq, window=NSA_WINDOW - 1, slopes=SLOPES_A, shared_kv=True,
                   col_q=COL_QA, col_k=COL_KWV, col_v=COL_KWV, name="nsa_window")
    od = _banded(p2, row(g_q_dil), row(g_k_dil),
                 jnp.asarray(_band_bias_tiles(DILATED_CONFIGS[-1][0], _dilated_mult)),
                 batch, seq, window=DILATED_CONFIGS[-1][0], slopes=SLOPES_D, shared_kv=False,
                 col_q=COL_QD, col_k=COL_KD, col_v=COL_VD, name="dilated")
    oc = _stick(p2, batch, seq)

    bg = jnp.zeros((1, LANES), F32).at[0, :N_GATES].set(b_gate)
    cwp = jnp.zeros((8, GROUP_W), F32).at[:CONV_K].set(conv_w)
    x1 = _mixout(x2, p2, ocmp, oslc, owin, oc, od, bg, cwp, row(g_out), w_out.astype(BF16), seq)
    return _ffn(x1, row(g_ffn), w_gate.astype(BF16), w_up.astype(BF16), w_down.astype(BF16))


def kernel(x, g_mix, w_in, b_gate, g_q_nsa, g_k_cmp, g_k_slc, g_k_win, pe_k_cmp, pe_v_cmp, w1_k_cmp, w2_k_cmp,
           w1_v_cmp, w2_v_cmp, conv_w, g_q_dil, g_k_dil, g_out, w_out, g_ffn, w_gate, w_up, w_down):
    batch, seq, d = x.shape
    assert seq % (CMP_STRIDE * LANES) == 0 and d % LANES == 0
    x2 = x.reshape(batch * seq, d)
    params = (g_mix, w_in, b_gate, g_q_nsa, g_k_cmp, g_k_slc, g_k_win, pe_k_cmp, pe_v_cmp, w1_k_cmp, w2_k_cmp,
              w1_v_cmp, w2_v_cmp, conv_w, g_q_dil, g_k_dil, g_out, w_out, g_ffn, w_gate, w_up, w_down)
    for layer in range(g_mix.shape[0]):
        x2 = _layer(x2, batch, seq, *[p[layer] for p in params])
    return x2.reshape(batch, seq, d)
```

```python
import functools

import numpy as np
import jax
import jax.numpy as jnp
from jax import lax
from jax.experimental import pallas as pl
from jax.experimental.pallas import tpu as pltpu

F32 = jnp.float32
BF16 = jnp.bfloat16

HEAD_DIM = 64
N_HEADS = 4
GROUP_W = N_HEADS * HEAD_DIM
CONV_K = 3
CMP_LEN = 32
CMP_STRIDE = 16
SEL_BLOCK = 64
N_SELECT = 16
NSA_WINDOW = 512
DILATED_CONFIGS = ((128, 1), (512, 4), (2048, 16))
NEG_INF = -1e30
FORCE_SCORE = 1e6
RMS_EPS = 1e-6

TQ = 256
TK = 256
LANES = 128
SUM_PAD = 16
LOG2E = 1.4426950408889634
VMEM_LIMIT = 52 * 1024 * 1024

COL_QA, COL_KVC, COL_KSV, COL_KWV, COL_GATE = 0, 256, 384, 512, 640
COL_CVB, COL_CVC, COL_CVU = 768, 1024, 1280
COL_QC, COL_KC, COL_VC = 1536, 1792, 2048
COL_QD, COL_KD, COL_VD = 2304, 2560, 2816
P_COLS = 3072
N_GATES = 12

_NT = (((1,), (1,)), ((), ()))


def _alibi_slopes():
    s = [2.0 ** (-8.0 * i / 8) for i in range(1, 9)]
    return tuple(s[0::2]), tuple(s[1::2])


SLOPES_A, SLOPES_D = _alibi_slopes()


def _rms(x, g):
    return x * lax.rsqrt(jnp.mean(x * x, axis=-1, keepdims=True) + RMS_EPS) * g


def _exact_dot(a, sel):
    hi = a.astype(BF16)
    lo = (a - hi.astype(F32)).astype(BF16)
    return jnp.dot(hi, sel, preferred_element_type=F32) + jnp.dot(lo, sel, preferred_element_type=F32)


def _group_rms(x, g):
    w = x.shape[1]
    same = ((lax.broadcasted_iota(jnp.int32, (w, w), 0) >> 6)
            == (lax.broadcasted_iota(jnp.int32, (w, w), 1) >> 6)).astype(BF16)
    ssq = _exact_dot(x * x, same)
    return x * lax.rsqrt(ssq * (1.0 / HEAD_DIM) + RMS_EPS) * g


EVEN_AUG, ODD_AUG = HEAD_DIM, 0


def _key_aug(pos, first):
    lane = lax.broadcasted_iota(jnp.int32, (pos.shape[0], LANES), 1) - first
    hi = (pos >> 6).astype(F32)
    lo = (pos & 63).astype(F32)
    return jnp.where(lane == 0, hi, jnp.where(lane == 1, lo, jnp.where((lane == 2) | (lane == 3), 1.0, 0.0)))


def _query_aug(t, slope, first):
    lane = lax.broadcasted_iota(jnp.int32, (t.shape[0], LANES), 1) - first
    hi = (t >> 6).astype(F32) * (-64.0 * slope)
    lo = (t & 63).astype(F32) * (-slope)
    return jnp.where(lane == 0, 64.0 * slope,
                     jnp.where(lane == 1, slope, jnp.where(lane == 2, hi, jnp.where(lane == 3, lo, 0.0))))


def _pair_keys(pair, pos):
    lane = lax.broadcasted_iota(jnp.int32, pair.shape, 1)
    aug_e = 0.0 if pos is None else _key_aug(pos, EVEN_AUG)
    aug_o = 0.0 if pos is None else _key_aug(pos, ODD_AUG)
    return (jnp.where(lane < HEAD_DIM, pair, aug_e).astype(BF16),
            jnp.where(lane >= HEAD_DIM, pair, aug_o).astype(BF16))


def _prep_queries(q, g, t0, slopes, qp_s):
    qn = (q if g is None else _group_rms(q, g)) * (HEAD_DIM ** -0.5)
    tpos = t0 + lax.broadcasted_iota(jnp.int32, (q.shape[0], 1), 0)
    lane = lax.broadcasted_iota(jnp.int32, (q.shape[0], LANES), 1)
    for i in range(N_HEADS // 2):
        pair = qn[:, i * LANES:(i + 1) * LANES]
        if slopes is None:
            qp_s[i] = pair.astype(BF16)
            continue
        even = jnp.where((lane >= EVEN_AUG) & (lane < EVEN_AUG + 4), _query_aug(tpos, slopes[2 * i], EVEN_AUG), pair)
        odd = jnp.where(lane < ODD_AUG + 4, _query_aug(tpos, slopes[2 * i + 1], ODD_AUG), pair)
        qp_s[2 * i] = even.astype(BF16)
        qp_s[2 * i + 1] = odd.astype(BF16)


def _inproj_kernel(x_ref, g_ref, w_ref, o_ref, *, nchunk):
    h = _rms(x_ref[...], g_ref[...]).astype(BF16)
    cw = P_COLS // nchunk
    for c in range(nchunk):
        o_ref[:, c * cw:(c + 1) * cw] = jnp.dot(h, w_ref[:, c * cw:(c + 1) * cw], preferred_element_type=F32)


def _inproj(x2, g, w):
    t, d = x2.shape
    tm = 512
    return pl.pallas_call(
        functools.partial(_inproj_kernel, nchunk=6),
        grid=(t // tm,),
        in_specs=[pl.BlockSpec((tm, d), lambda i: (i, 0)),
                  pl.BlockSpec((1, d), lambda i: (0, 0)),
                  pl.BlockSpec((d, P_COLS), lambda i: (0, 0))],
        out_specs=pl.BlockSpec((tm, P_COLS), lambda i: (i, 0)),
        out_shape=jax.ShapeDtypeStruct((t, P_COLS), F32),
        compiler_params=pltpu.CompilerParams(dimension_semantics=("parallel",), vmem_limit_bytes=VMEM_LIMIT),
        name="inproj",
    )(x2, g, w)


def _gelu_tanh(x):
    return x * (0.5 * (1.0 + jnp.tanh(np.sqrt(2.0 / np.pi).astype(np.float32) * (x + 0.044715 * (x * x * x)))))


def _compress_kernel(kv_ref, pe_ref, w1_ref, w2_ref, gk_ref, kc_ref, vct_ref):
    nc = kv_ref.shape[0] // CMP_STRIDE
    first = None
    second = None
    for j in range(CMP_STRIDE):
        tok = kv_ref[pl.ds(j, nc, stride=CMP_STRIDE), :]
        a = jnp.dot((tok + pe_ref[j:j + 1, :]).astype(BF16), w1_ref[j], preferred_element_type=F32)
        b = jnp.dot((tok + pe_ref[CMP_STRIDE + j:CMP_STRIDE + j + 1, :]).astype(BF16), w1_ref[CMP_STRIDE + j],
                    preferred_element_type=F32)
        first = a if first is None else first + a
        second = b if second is None else second + b
    hid = first + pltpu.roll(second, nc - 1, 0)
    out = jnp.dot(_gelu_tanh(hid).astype(BF16), w2_ref[...], preferred_element_type=F32)
    kn = _group_rms(out, gk_ref[...])
    end = lax.broadcasted_iota(jnp.int32, (nc, 1), 0) * CMP_STRIDE + (CMP_LEN - 1)
    even, _ = _pair_keys(kn, end)
    _, odd = _pair_keys(pltpu.roll(kn, HEAD_DIM, 1), end)
    kc_ref[0] = jnp.concatenate([even, odd], axis=1)
    vct_ref[0] = out.T[HEAD_DIM:, :].astype(BF16)


def _compress(p2, pe, w1, w2, gk, batch, seq):
    nc = seq // CMP_STRIDE
    hid2 = w1.shape[2]
    return pl.pallas_call(
        _compress_kernel,
        grid=(batch,),
        in_specs=[pl.BlockSpec((seq, LANES), lambda i: (i, COL_KVC // LANES)),
                  pl.BlockSpec((CMP_LEN, LANES), lambda i: (0, 0)),
                  pl.BlockSpec((CMP_LEN, LANES, hid2), lambda i: (0, 0, 0)),
                  pl.BlockSpec((hid2, LANES), lambda i: (0, 0)),
                  pl.BlockSpec((1, LANES), lambda i: (0, 0))],
        out_specs=[pl.BlockSpec((1, nc, 2 * LANES), lambda i: (i, 0, 0)),
                   pl.BlockSpec((1, HEAD_DIM, nc), lambda i: (i, 0, 0))],
        out_shape=[jax.ShapeDtypeStruct((batch, nc, 2 * LANES), BF16),
                   jax.ShapeDtypeStruct((batch, HEAD_DIM, nc), BF16)],
        compiler_params=pltpu.CompilerParams(dimension_semantics=("parallel",), vmem_limit_bytes=VMEM_LIMIT),
        name="nsa_compress",
    )(p2, pe, w1, w2, gk)


def _compress_weights(pe_k, pe_v, w1_k, w2_k, w1_v, w2_v):
    hid = w1_k.shape[1]
    w1k = w1_k.reshape(CMP_LEN, HEAD_DIM, hid)
    w1v = w1_v.reshape(CMP_LEN, HEAD_DIM, hid)
    z1 = jnp.zeros_like(w1k)
    w1 = jnp.concatenate([jnp.concatenate([w1k, z1], axis=2), jnp.concatenate([z1, w1v], axis=2)], axis=1)
    z2 = jnp.zeros_like(w2_k)
    w2 = jnp.concatenate([jnp.concatenate([w2_k, z2], axis=1), jnp.concatenate([z2, w2_v], axis=1)], axis=0)
    pe = jnp.concatenate([pe_k, pe_v], axis=1)
    return pe, w1.astype(BF16), w2.astype(BF16)


def _flash_scores(kp_of, bias, qp_s, s_s, m_cur):
    m_next = []
    for h in range(N_HEADS):
        s = lax.dot_general(kp_of(h), qp_s[h], _NT, preferred_element_type=F32) + bias
        s_s[h] = s
        m_next.append(jnp.maximum(m_cur[h], jnp.max(s, axis=0, keepdims=True)))
    return tuple(m_next)


def _flash_accum(vt_of, s_s, acc_s, m_prev, m_cur, ls):
    new_l = []
    for h in range(N_HEADS):
        alpha = jnp.exp(m_prev[h] - m_cur[h])
        p = jnp.exp(s_s[h] - m_cur[h])
        new_l.append(alpha * ls[h] + jnp.sum(p, axis=0, keepdims=True))
        acc_s[h] = alpha * acc_s[h] + jnp.dot(vt_of(h), p.astype(BF16), preferred_element_type=F32)
    return tuple(new_l)


def _flash_accum_and_scores(kp_of, bias, vt_of, qp_s, s_s, acc_s, m_prev, m_cur, ls):
    m_next, new_l = [], []
    for h in range(N_HEADS):
        s_new = lax.dot_general(kp_of(h), qp_s[h], _NT, preferred_element_type=F32) + bias
        alpha = jnp.exp(m_prev[h] - m_cur[h])
        p = jnp.exp(s_s[h] - m_cur[h])
        new_l.append(alpha * ls[h] + jnp.sum(p, axis=0, keepdims=True))
        acc_s[h] = alpha * acc_s[h] + jnp.dot(vt_of(h), p.astype(BF16), preferred_element_type=F32)
        s_s[h] = s_new
        m_next.append(jnp.maximum(m_cur[h], jnp.max(s_new, axis=0, keepdims=True)))
    return tuple(m_next), tuple(new_l)


def _flash_reverse(first_kt, n_tiles, kp_fn, bias_fn, vt_fn, qp_s, s_s, acc_s, o_ref):
    acc_s[...] = jnp.zeros(acc_s.shape, F32)
    m0 = tuple(jnp.full((1, TQ), NEG_INF, F32) for _ in range(N_HEADS))
    l0 = tuple(jnp.zeros((1, TQ), F32) for _ in range(N_HEADS))
    m1 = _flash_scores(kp_fn(first_kt), bias_fn(first_kt, True), qp_s, s_s, m0)

    def body(i, carry):
        m_prev, m_cur, ls = carry
        kt = first_kt - i
        m_next, ls = _flash_accum_and_scores(kp_fn(kt - 1), bias_fn(kt - 1, False), vt_fn(kt), qp_s, s_s, acc_s,
                                             m_prev, m_cur, ls)
        return m_cur, m_next, ls

    m_prev, m_cur, ls = lax.fori_loop(0, n_tiles - 1, body, (m0, m1, l0))
    ls = _flash_accum(vt_fn(first_kt - (n_tiles - 1)), s_s, acc_s, m_prev, m_cur, ls)
    o_ref[0] = jnp.concatenate([acc_s[h] * (1.0 / ls[h]) for h in range(N_HEADS)], axis=0)


def _nsa_kernel(q_ref, kc_ref, vct_ref, ksv_ref, gq_ref, gks_ref, ocmp_ref, oslc_ref,
                ksp_s, vst_s, qp_s, imp_s, sel_s, s_s, acc_s, *, seq):
    qi = pl.program_id(1)
    nc = seq // CMP_STRIDE
    nsel = seq // SEL_BLOCK
    nkt = seq // TK

    @pl.when(qi == 0)
    def _prep():
        for c in range(nkt):
            rows = slice(c * TK, (c + 1) * TK)
            blk = ksv_ref[rows, :]
            pos = c * TK + lax.broadcasted_iota(jnp.int32, (TK, 1), 0)
            kn = _group_rms(blk, gks_ref[...])
            ksp_s[rows, 0:LANES], _ = _pair_keys(kn, pos)
            _, ksp_s[rows, LANES:2 * LANES] = _pair_keys(pltpu.roll(kn, HEAD_DIM, 1), pos)
            vst_s[c] = blk.T[HEAD_DIM:, :].astype(BF16)
        imp_s[:, 0:8, :] = jnp.zeros((TQ // LANES, 8, LANES), F32)

    t0 = qi * TQ
    _prep_queries(q_ref[...], gq_ref[...], t0, SLOPES_A, qp_s)

    kc = [kc_ref[0, :, 0:LANES], kc_ref[0, :, LANES:2 * LANES]]
    vct = vct_ref[0]
    tq_row = t0 + lax.broadcasted_iota(jnp.int32, (nc, TQ), 1)
    n_idx = lax.broadcasted_iota(jnp.int32, (nc, TQ), 0)
    vis = (tq_row >= n_idx * CMP_STRIDE + (CMP_LEN - 1)) & (n_idx < nc - 1)
    vis_bias = jnp.where(vis, 0.0, NEG_INF)
    sees_any = (t0 + lax.broadcasted_iota(jnp.int32, (1, TQ), 1) >= CMP_LEN - 1).astype(F32)
    imp = jnp.zeros((nc, TQ), F32)
    ocmp_t = []
    for h in range(N_HEADS):
        sc = lax.dot_general(kc[h % 2], qp_s[h], _NT, preferred_element_type=F32) + vis_bias
        e = jnp.exp(sc - jnp.max(sc, axis=0, keepdims=True))
        p = e * (sees_any / jnp.sum(e, axis=0, keepdims=True))
        ocmp_t.append(jnp.dot(vct, p.astype(BF16), preferred_element_type=F32))
        imp = imp + p
    ocmp_ref[0] = jnp.concatenate(ocmp_t, axis=0)

    halves = []
    for half in range(TQ // LANES):
        imp_s[half, 8:8 + nc, :] = imp[:, half * LANES:(half + 1) * LANES]
        r = [imp_s[half, pl.ds(8 + k, nsel, stride=4), :] for k in range(4)]
        rm1 = imp_s[half, pl.ds(7, nsel, stride=4), :]
        halves.append(rm1 + 2.0 * (r[0] + r[1] + r[2]) + r[3])
    imp_blk = jnp.concatenate(halves, axis=1)
    blk = lax.broadcasted_iota(jnp.int32, (nsel, TQ), 0)
    tl = t0 + lax.broadcasted_iota(jnp.int32, (nsel, TQ), 1)
    cur = tl >> 6
    forced = (blk == 0) | (blk == cur) | (blk == cur - 1)
    valid = blk * SEL_BLOCK <= tl
    score = jnp.where(forced, FORCE_SCORE, jnp.where(valid, imp_blk, -FORCE_SCORE))
    sub = 8
    for g in range(nsel // sub):
        mine = score[g * sub:(g + 1) * sub, :]
        blk_g = g * sub + lax.broadcasted_iota(jnp.int32, (sub, TQ), 0)
        rank = jnp.zeros((sub, TQ), F32)
        for j in range(nsel):
            row = score[j:j + 1, :]
            if j < g * sub:
                beats = row >= mine
            elif j >= (g + 1) * sub:
                beats = row > mine
            else:
                beats = (row > mine) | ((row == mine) & (blk_g > j))
            rank = rank + jnp.where(beats, 1.0, 0.0)
        sel_s[g * sub:(g + 1) * sub, :] = jnp.where(rank < min(N_SELECT, nsel), 0.0, NEG_INF)

    def block_bias(kt):
        per_tile = TK // SEL_BLOCK
        return jnp.concatenate([jnp.broadcast_to(sel_s[pl.ds(kt * per_tile + j, 1), :], (SEL_BLOCK, TQ))
                                for j in range(per_tile)], axis=0)

    causal = lax.broadcasted_iota(jnp.int32, (TK, TQ), 0) <= lax.broadcasted_iota(jnp.int32, (TK, TQ), 1)

    def kp_fn(kt):
        off = pl.multiple_of(kt * TK, TK)
        return lambda h: ksp_s[pl.ds(off, TK), (h % 2) * LANES:(h % 2 + 1) * LANES]

    def bias_fn(kt, first):
        return jnp.where(causal, block_bias(kt), NEG_INF) if first else block_bias(kt)

    _flash_reverse(qi, qi + 1, kp_fn, bias_fn, lambda kt: (lambda h: vst_s[kt]), qp_s, s_s, acc_s, oslc_ref)


def _nsa(p2, kc, vc, gq, gks, batch, seq):
    nq = seq // TQ
    nc = seq // CMP_STRIDE
    t = batch * seq
    return pl.pallas_call(
        functools.partial(_nsa_kernel, seq=seq),
        grid=(batch, nq),
        in_specs=[pl.BlockSpec((TQ, GROUP_W), lambda b, i: (b * nq + i, COL_QA // GROUP_W)),
                  pl.BlockSpec((1, nc, 2 * LANES), lambda b, i: (b, 0, 0)),
                  pl.BlockSpec((1, HEAD_DIM, nc), lambda b, i: (b, 0, 0)),
                  pl.BlockSpec((seq, LANES), lambda b, i: (b, COL_KSV // LANES)),
                  pl.BlockSpec((1, GROUP_W), lambda b, i: (0, 0)),
                  pl.BlockSpec((1, LANES), lambda b, i: (0, 0))],
        out_specs=[pl.BlockSpec((1, GROUP_W, TQ), lambda b, i: (b * nq + i, 0, 0)),
                   pl.BlockSpec((1, GROUP_W, TQ), lambda b, i: (b * nq + i, 0, 0))],
        out_shape=[jax.ShapeDtypeStruct((t // TQ, GROUP_W, TQ), F32),
                   jax.ShapeDtypeStruct((t // TQ, GROUP_W, TQ), F32)],
        scratch_shapes=[pltpu.VMEM((seq, 2 * LANES), BF16),
                        pltpu.VMEM((seq // TK, HEAD_DIM, TK), BF16),
                        pltpu.VMEM((N_HEADS, TQ, LANES), BF16),
                        pltpu.VMEM((TQ // LANES, 8 + nc, LANES), F32),
                        pltpu.VMEM((seq // SEL_BLOCK, TQ), F32),
                        pltpu.VMEM((N_HEADS, TK, TQ), F32),
                        pltpu.VMEM((N_HEADS, HEAD_DIM, TQ), F32)],
        compiler_params=pltpu.CompilerParams(dimension_semantics=("parallel", "arbitrary"),
                                             vmem_limit_bytes=VMEM_LIMIT),
        name="nsa_cmp_slc",
    )(p2, kc, vc, p2, gq, gks)


def _band_bias_tiles(window, mult_fn):
    nd = (window + TK - 1) // TK + 1
    kk = np.arange(TK)[:, None]
    qq = np.arange(TQ)[None, :]
    tiles = np.empty((nd, TK, TQ), np.float32)
    for di in range(nd):
        d = di * TK + qq - kk
        mult = mult_fn(d)
        tiles[di] = np.where(mult > 0, np.log(np.maximum(mult, 1)), NEG_INF)
    return tiles


def _window_mult(d):
    return ((d >= 0) & (d <= NSA_WINDOW - 1)).astype(np.float64)


def _dilated_mult(d):
    m = np.zeros(d.shape, np.float64)
    for window, dil in DILATED_CONFIGS:
        m += ((d >= 0) & (d <= window) & (d % dil == 0)).astype(np.float64)
    return m


def _banded_kernel(*refs, seq, window, slopes, shared_kv):
    if shared_kv:
        q_ref, kv_ref, gq_ref, gk_ref, bias_ref, o_ref, kp_s, vt_s, qp_s, s_s, acc_s = refs
    else:
        q_ref, k_ref, v_ref, gq_ref, gk_ref, bias_ref, o_ref, kp_s, vt_s, qp_s, s_s, acc_s = refs
    qi = pl.program_id(1)
    nkt = seq // TK

    @pl.when(qi == 0)
    def _prep():
        for c in range(nkt):
            rows = slice(c * TK, (c + 1) * TK)
            pos = c * TK + lax.broadcasted_iota(jnp.int32, (TK, 1), 0)
            if shared_kv:
                blk = kv_ref[rows, :]
                kn = _group_rms(blk, gk_ref[...])
                kp_s[rows, 0:LANES], _ = _pair_keys(kn, pos)
                _, kp_s[rows, LANES:2 * LANES] = _pair_keys(pltpu.roll(kn, HEAD_DIM, 1), pos)
                vt_s[c] = blk.T[HEAD_DIM:, :].astype(BF16)
            else:
                kn = _group_rms(k_ref[rows, :], gk_ref[...])
                for i in range(N_HEADS // 2):
                    even, odd = _pair_keys(kn[:, i * LANES:(i + 1) * LANES], pos)
                    kp_s[rows, 2 * i * LANES:(2 * i + 1) * LANES] = even
                    kp_s[rows, (2 * i + 1) * LANES:(2 * i + 2) * LANES] = odd
                vt_s[c] = v_ref[rows, :].T.astype(BF16)

    t0 = qi * TQ
    _prep_queries(q_ref[...], gq_ref[...], t0, slopes, qp_s)
    kt_lo = jnp.maximum(t0 - window, 0) // TK

    def kp_fn(kt):
        off = pl.multiple_of(kt * TK, TK)
        if shared_kv:
            return lambda h: kp_s[pl.ds(off, TK), (h % 2) * LANES:(h % 2 + 1) * LANES]
        return lambda h: kp_s[pl.ds(off, TK), h * LANES:(h + 1) * LANES]

    def vt_fn(kt):
        if shared_kv:
            return lambda h: vt_s[kt]
        return lambda h: vt_s[kt, h * HEAD_DIM:(h + 1) * HEAD_DIM, :]

    _flash_reverse(qi, qi + 1 - kt_lo, kp_fn, lambda kt, first: bias_ref[qi - kt], vt_fn, qp_s, s_s, acc_s, o_ref)


def _banded(p2, gq, gk, bias_tiles, batch, seq, *, window, slopes, shared_kv, col_q, col_k, col_v, name):
    nq = seq // TQ
    t = batch * seq
    nd = bias_tiles.shape[0]
    n_kv = 1 if shared_kv else N_HEADS
    q_spec = pl.BlockSpec((TQ, GROUP_W), lambda b, i: (b * nq + i, col_q // GROUP_W))
    gain = lambda g: pl.BlockSpec(g.shape, lambda b, i: (0, 0))
    bias_spec = pl.BlockSpec((nd, TK, TQ), lambda b, i: (0, 0, 0))
    if shared_kv:
        kv_specs = [pl.BlockSpec((seq, LANES), lambda b, i: (b, col_k // LANES))]
        operands = (p2, p2, gq, gk, bias_tiles)
    else:
        kv_specs = [pl.BlockSpec((seq, GROUP_W), lambda b, i: (b, col_k // GROUP_W)),
                    pl.BlockSpec((seq, GROUP_W), lambda b, i: (b, col_v // GROUP_W))]
        operands = (p2, p2, p2, gq, gk, bias_tiles)
    return pl.pallas_call(
        functools.partial(_banded_kernel, seq=seq, window=window, slopes=slopes, shared_kv=shared_kv),
        grid=(batch, nq),
        in_specs=[q_spec] + kv_specs + [gain(gq), gain(gk), bias_spec],
        out_specs=pl.BlockSpec((1, GROUP_W, TQ), lambda b, i: (b * nq + i, 0, 0)),
        out_shape=jax.ShapeDtypeStruct((t // TQ, GROUP_W, TQ), F32),
        scratch_shapes=[pltpu.VMEM((seq, (2 if shared_kv else N_HEADS) * LANES), BF16),
                        pltpu.VMEM((seq // TK, n_kv * HEAD_DIM, TK), BF16),
                        pltpu.VMEM((N_HEADS, TQ, LANES), BF16),
                        pltpu.VMEM((N_HEADS, TK, TQ), F32),
                        pltpu.VMEM((N_HEADS, HEAD_DIM, TQ), F32)],
        compiler_params=pltpu.CompilerParams(dimension_semantics=("parallel", "arbitrary"),
                                             vmem_limit_bytes=VMEM_LIMIT),
        name=name,
    )(*operands)


def _stick_kernel(q_ref, k_ref, v_ref, o_ref, kp_s, vt_s, qp_s, hl_s, e_s, acc_s, *, seq):
    qi = pl.program_id(1)
    nkt = seq // TK

    @pl.when(qi == 0)
    def _prep():
        for c in range(nkt):
            rows = slice(c * TK, (c + 1) * TK)
            kb = k_ref[rows, :]
            for i in range(N_HEADS // 2):
                even, odd = _pair_keys(kb[:, i * LANES:(i + 1) * LANES], None)
                kp_s[rows, 2 * i * LANES:(2 * i + 1) * LANES] = even
                kp_s[rows, (2 * i + 1) * LANES:(2 * i + 2) * LANES] = odd
            vt_s[c] = v_ref[rows, :].T.astype(BF16)

    _prep_queries(q_ref[...], None, qi * TQ, None, qp_s)
    half = TK // 2
    col = lax.broadcasted_iota(jnp.int32, (half + SUM_PAD, TK), 1) & (half - 1)
    srow = lax.broadcasted_iota(jnp.int32, (half + SUM_PAD, TK), 0)
    sums = ((srow == half) | ((srow < half) & (col > srow))).astype(BF16)
    past = lax.broadcasted_iota(jnp.int32, (TK, TQ), 0) < lax.broadcasted_iota(jnp.int32, (TK, TQ), 1)
    acc_s[...] = jnp.zeros(acc_s.shape, F32)

    def logits(kt, h):
        off = pl.multiple_of(kt * TK, TK)
        return lax.dot_general(kp_s[pl.ds(off, TK), h * LANES:(h + 1) * LANES], qp_s[h // 2], _NT,
                               preferred_element_type=F32)

    def stage(h, z, masked):
        sp = jnp.maximum(z, 0.0) + jnp.log(1.0 + jnp.exp2(jnp.abs(z) * (-LOG2E)))
        spm = jnp.where(past, sp, 0.0) if masked else sp
        hi = spm.astype(BF16)
        lo = (spm - hi.astype(F32)).astype(BF16)
        for b in range(2):
            hl_s[h, b, 0:half, :] = hi[b * half:(b + 1) * half, :]
            hl_s[h, b, half:TK, :] = lo[b * half:(b + 1) * half, :]
        logsig = z - sp
        e_s[h] = jnp.where(past, logsig, NEG_INF) if masked else logsig

    def tails(h):
        return [jnp.dot(sums, hl_s[h, b], preferred_element_type=F32) for b in range(2)]

    def consume(kt, h, w, later):
        later_lo = later + w[1][half:half + 1, :]
        attn = jnp.concatenate([jnp.exp(e_s[h, 0:half, :] - w[0][0:half, :] - later_lo),
                                jnp.exp(e_s[h, half:TK, :] - w[1][0:half, :] - later)], axis=0)
        acc_s[h] += jnp.dot(vt_s[kt, h * HEAD_DIM:(h + 1) * HEAD_DIM, :], attn.astype(BF16),
                            preferred_element_type=F32)
        return later_lo + w[0][half:half + 1, :]

    ahead = 2
    for h in range(N_HEADS):
        stage(h, logits(qi, h), True)

    def body(i, laters):
        kt = qi - i
        w = {h: tails(h) for h in range(ahead)}
        z = {h: logits(kt - 1, h) for h in range(ahead)}
        out = []
        for h in range(N_HEADS):
            out.append(consume(kt, h, w.pop(h), laters[h]))
            stage(h, z.pop(h), False)
            if h + ahead < N_HEADS:
                w[h + ahead] = tails(h + ahead)
                z[h + ahead] = logits(kt - 1, h + ahead)
        return tuple(out)

    laters = lax.fori_loop(0, qi, body, tuple(jnp.zeros((1, TQ), F32) for _ in range(N_HEADS)))
    w = {h: tails(h) for h in range(ahead)}
    for h in range(N_HEADS):
        consume(0, h, w.pop(h), laters[h])
        if h + ahead < N_HEADS:
            w[h + ahead] = tails(h + ahead)
    o_ref[0] = acc_s[...].reshape(GROUP_W, TQ)


def _stick(p2, batch, seq):
    nq = seq // TQ
    t = batch * seq
    return pl.pallas_call(
        functools.partial(_stick_kernel, seq=seq),
        grid=(batch, nq),
        in_specs=[pl.BlockSpec((TQ, GROUP_W), lambda b, i: (b * nq + i, COL_QC // GROUP_W)),
                  pl.BlockSpec((seq, GROUP_W), lambda b, i: (b, COL_KC // GROUP_W)),
                  pl.BlockSpec((seq, GROUP_W), lambda b, i: (b, COL_VC // GROUP_W))],
        out_specs=pl.BlockSpec((1, GROUP_W, TQ), lambda b, i: (b * nq + i, 0, 0)),
        out_shape=jax.ShapeDtypeStruct((t // TQ, GROUP_W, TQ), F32),
        scratch_shapes=[pltpu.VMEM((seq, N_HEADS * LANES), BF16),
                        pltpu.VMEM((seq // TK, GROUP_W, TK), BF16),
                        pltpu.VMEM((N_HEADS // 2, TQ, LANES), BF16),
                        pltpu.VMEM((N_HEADS, 2, TK, TQ), BF16),
                        pltpu.VMEM((N_HEADS, TK, TQ), F32),
                        pltpu.VMEM((N_HEADS, HEAD_DIM, TQ), F32)],
        compiler_params=pltpu.CompilerParams(dimension_semantics=("parallel", "arbitrary"),
                                             vmem_limit_bytes=VMEM_LIMIT),
        name="stick_breaking",
    )(p2, p2, p2)


def _mixout_kernel(x_ref, ocmp_ref, oslc_ref, owin_ref, gate_ref, cvb_ref, cvc_ref, cvu_ref, pc_ref, pu_ref,
                   oc_ref, od_ref, bg_ref, cw_ref, gout_ref, wout_ref, o_ref, *, tiles_per_seq):
    i = pl.program_id(0)
    d = x_ref.shape[1]

    gates = jax.nn.sigmoid(gate_ref[...] + bg_ref[...])
    src = lax.broadcasted_iota(jnp.int32, (LANES, GROUP_W), 0)
    head3 = (lax.broadcasted_iota(jnp.int32, (LANES, GROUP_W), 1) >> 6) * 3
    oa = (_exact_dot(gates, (src == head3).astype(BF16)) * ocmp_ref[0].T
          + _exact_dot(gates, (src == head3 + 1).astype(BF16)) * oslc_ref[0].T
          + _exact_dot(gates, (src == head3 + 2).astype(BF16)) * owin_ref[0].T)
    cu = cvc_ref[...] * cvu_ref[...]
    prev = jnp.where(i % tiles_per_seq == 0, 0.0, pc_ref[...] * pu_ref[...])
    full = jnp.concatenate([prev, cu], axis=0)
    back1 = pltpu.roll(full, 1, 0)[8:, :]
    back2 = pltpu.roll(full, 2, 0)[8:, :]
    cw = cw_ref[...]
    ob = cvb_ref[...] * (cw[0:1, :] * back2 + cw[1:2, :] * back1 + cw[2:3, :] * cu)
    groups = jnp.concatenate([oa, ob, oc_ref[0].T, od_ref[0].T], axis=1)
    n_groups = d // HEAD_DIM
    gather = ((lax.broadcasted_iota(jnp.int32, (d, LANES), 0) >> 6)
              == lax.broadcasted_iota(jnp.int32, (d, LANES), 1)).astype(BF16)
    spread = (lax.broadcasted_iota(jnp.int32, (LANES, d), 0)
              == (lax.broadcasted_iota(jnp.int32, (LANES, d), 1) >> 6)).astype(BF16)
    ssq = _exact_dot(groups * groups, gather)
    inv = lax.rsqrt(ssq * (1.0 / HEAD_DIM) + RMS_EPS)
    mixed = (groups * _exact_dot(inv, spread) * gout_ref[...]).astype(BF16)
    assert n_groups <= LANES
    o_ref[...] = x_ref[...] + jnp.dot(mixed, wout_ref[...], preferred_element_type=F32)


def _mixout(x2, p2, ocmp, oslc, owin, oc, od, bg, cw, gout, wout, seq):
    t, d = x2.shape
    tm = TQ
    rows = lambda w, col: pl.BlockSpec((tm, w), lambda i: (i, col // w))
    heads_t = pl.BlockSpec((1, GROUP_W, tm), lambda i: (i, 0, 0))
    prev8 = lambda col: pl.BlockSpec((8, GROUP_W), lambda i: (jnp.maximum(i * (tm // 8) - 1, 0), col // GROUP_W))
    const = lambda shape: pl.BlockSpec(shape, lambda i: (0, 0))
    return pl.pallas_call(
        functools.partial(_mixout_kernel, tiles_per_seq=seq // tm),
        grid=(t // tm,),
        in_specs=[rows(d, 0), heads_t, heads_t, heads_t,
                  rows(LANES, COL_GATE), rows(GROUP_W, COL_CVB), rows(GROUP_W, COL_CVC), rows(GROUP_W, COL_CVU),
                  prev8(COL_CVC), prev8(COL_CVU),
                  heads_t, heads_t,
                  const((1, LANES)), const((8, GROUP_W)), const((1, d)), const((d, d))],
        out_specs=rows(d, 0),
        out_shape=jax.ShapeDtypeStruct((t, d), F32),
        compiler_params=pltpu.CompilerParams(dimension_semantics=("parallel",), vmem_limit_bytes=VMEM_LIMIT),
        name="mixout",
    )(x2, ocmp, oslc, owin, p2, p2, p2, p2, p2, p2, oc, od, bg, cw, gout, wout)


def _ffn_kernel(x_ref, g_ref, wg_ref, wu_ref, wd_ref, o_ref, act_s, *, ff_chunk, out_chunk):
    h = _rms(x_ref[...], g_ref[...]).astype(BF16)
    dff = wg_ref.shape[1]
    for c in range(dff // ff_chunk):
        cols = slice(c * ff_chunk, (c + 1) * ff_chunk)
        a = jnp.dot(h, wg_ref[:, cols], preferred_element_type=F32)
        u = jnp.dot(h, wu_ref[:, cols], preferred_element_type=F32)
        act_s[:, cols] = (a * jax.nn.sigmoid(a) * u).astype(BF16)
    d = o_ref.shape[1]
    for c in range(d // out_chunk):
        cols = slice(c * out_chunk, (c + 1) * out_chunk)
        o_ref[:, cols] = x_ref[:, cols] + jnp.dot(act_s[...], wd_ref[:, cols], preferred_element_type=F32)


def _ffn(x2, g, wg, wu, wd):
    t, d = x2.shape
    dff = wg.shape[1]
    tm = 512
    resident = lambda shape: pl.BlockSpec(shape, lambda i: (0, 0), pipeline_mode=pl.Buffered(1))
    return pl.pallas_call(
        functools.partial(_ffn_kernel, ff_chunk=256, out_chunk=256),
        grid=(t // tm,),
        in_specs=[pl.BlockSpec((tm, d), lambda i: (i, 0)),
                  pl.BlockSpec((1, d), lambda i: (0, 0)),
                  resident((d, dff)), resident((d, dff)), resident((dff, d))],
        out_specs=pl.BlockSpec((tm, d), lambda i: (i, 0)),
        out_shape=jax.ShapeDtypeStruct((t, d), F32),
        scratch_shapes=[pltpu.VMEM((tm, dff), BF16)],
        compiler_params=pltpu.CompilerParams(dimension_semantics=("parallel",), vmem_limit_bytes=VMEM_LIMIT),
        name="ffn",
    )(x2, g, wg, wu, wd)


def _permute_w_in_kernel(w_ref, o_ref):
    rows = w_ref.shape[0]
    n_rest = P_COLS - COL_CVB
    o_ref[:, 0:COL_GATE] = w_ref[:, 0:COL_GATE].astype(BF16)
    lane = lax.broadcasted_iota(jnp.int32, (rows, LANES), 1)
    o_ref[:, COL_GATE:COL_GATE + LANES] = jnp.where(lane < N_GATES, w_ref[:, COL_GATE:COL_GATE + LANES], 0.0).astype(BF16)
    o_ref[:, COL_CVB:P_COLS] = w_ref[:, COL_GATE + N_GATES:COL_GATE + N_GATES + n_rest].astype(BF16)


def _permute_w_in(w_in):
    d, cols = w_in.shape
    tr = 256
    return pl.pallas_call(
        _permute_w_in_kernel,
        grid=(d // tr,),
        in_specs=[pl.BlockSpec((tr, cols), lambda i: (i, 0))],
        out_specs=pl.BlockSpec((tr, P_COLS), lambda i: (i, 0)),
        out_shape=jax.ShapeDtypeStruct((d, P_COLS), BF16),
        compiler_params=pltpu.CompilerParams(dimension_semantics=("parallel",), vmem_limit_bytes=VMEM_LIMIT),
        name="permute_w_in",
    )(w_in)


def _layer(x2, batch, seq, g_mix, w_in, b_gate, g_q_nsa, g_k_cmp, g_k_slc, g_k_win, pe_k_cmp, pe_v_cmp,
           w1_k_cmp, w2_k_cmp, w1_v_cmp, w2_v_cmp, conv_w, g_q_dil, g_k_dil, g_out, w_out,
           g_ffn, w_gate, w_up, w_down):
    row = lambda v: v.reshape(1, -1)
    per_group = lambda g, n: jnp.tile(g, n).reshape(1, -1)
    p2 = _inproj(x2, row(g_mix), _permute_w_in(w_in))

    pe, w1, w2 = _compress_weights(pe_k_cmp, pe_v_cmp, w1_k_cmp, w2_k_cmp, w1_v_cmp, w2_v_cmp)
    kc, vc = _compress(p2, pe, w1, w2, per_group(g_k_cmp, 2), batch, seq)
    ocmp, oslc = _nsa(p2, kc, vc, per_group(g_q_nsa, N_HEADS), per_group(g_k_slc, 2), batch, seq)
    owin = _banded(p2, per_group(g_q_nsa, N_HEADS), per_group(g_k_win, 2), jnp.asarray(_band_bias_tiles(NSA_WINDOW - 1, _window_mult)),
                   batch, seq, window=NSA_WINDOW - 1, slopes=SLOPES_A, shared_kv=True,
                   col_q=COL_QA, col_k=COL_KWV, col_v=COL_KWV, name="nsa_window")
    od = _banded(p2, per_group(g_q_dil, N_HEADS), per_group(g_k_dil, N_HEADS),
                 jnp.asarray(_band_bias_tiles(DILATED_CONFIGS[-1][0], _dilated_mult)),
                 batch, seq, window=DILATED_CONFIGS[-1][0], slopes=SLOPES_D, shared_kv=False,
                 col_q=COL_QD, col_k=COL_KD, col_v=COL_VD, name="dilated")
    oc = _stick(p2, batch, seq)

    bg = jnp.zeros((1, LANES), F32).at[0, :N_GATES].set(b_gate)
    cwp = jnp.zeros((8, GROUP_W), F32).at[:CONV_K].set(conv_w)
    x1 = _mixout(x2, p2, ocmp, oslc, owin, oc, od, bg, cwp, row(g_out), w_out.astype(BF16), seq)
    return _ffn(x1, row(g_ffn), w_gate.astype(BF16), w_up.astype(BF16), w_down.astype(BF16))


def kernel(x, g_mix, w_in, b_gate, g_q_nsa, g_k_cmp, g_k_slc, g_k_win, pe_k_cmp, pe_v_cmp, w1_k_cmp, w2_k_cmp,
           w1_v_cmp, w2_v_cmp, conv_w, g_q_dil, g_k_dil, g_out, w_out, g_ffn, w_gate, w_up, w_down):
    batch, seq, d = x.shape
    assert seq % (CMP_STRIDE * LANES) == 0 and d % LANES == 0
    x2 = x.reshape(batch * seq, d)
    params = (g_mix, w_in, b_gate, g_q_nsa, g_k_cmp, g_k_slc, g_k_win, pe_k_cmp, pe_v_cmp, w1_k_cmp, w2_k_cmp,
              w1_v_cmp, w2_v_cmp, conv_w, g_q_dil, g_k_dil, g_out, w_out, g_ffn, w_gate, w_up, w_down)
    for layer in range(g_mix.shape[0]):
        x2 = _layer(x2, batch, seq, *[p[layer] for p in params])
    return x2.reshape(batch, seq, d)
```

```python
import functools

import numpy as np
import jax
import jax.numpy as jnp
from jax import lax
from jax.experimental import pallas as pl
from jax.experimental.pallas import tpu as pltpu

F32 = jnp.float32
BF16 = jnp.bfloat16

HEAD_DIM = 64
N_HEADS = 4
GROUP_W = N_HEADS * HEAD_DIM
CONV_K = 3
CMP_LEN = 32
CMP_STRIDE = 16
SEL_BLOCK = 64
N_SELECT = 16
NSA_WINDOW = 512
DILATED_CONFIGS = ((128, 1), (512, 4), (2048, 16))
NEG_INF = -1e30
FORCE_SCORE = 1e6
RMS_EPS = 1e-6

TQ = 256
TK = 256
LANES = 128
MXU_AHEAD = 2
FLASH_AHEAD = 2
SUM_PAD = 16
LOG2E = 1.4426950408889634
VMEM_LIMIT = 52 * 1024 * 1024

COL_QA, COL_KVC, COL_KSV, COL_KWV, COL_GATE = 0, 256, 384, 512, 640
COL_CVB, COL_CVC, COL_CVU = 768, 1024, 1280
COL_QC, COL_KC, COL_VC = 1536, 1792, 2048
COL_QD, COL_KD, COL_VD = 2304, 2560, 2816
P_COLS = 3072
N_GATES = 12

_NT = (((1,), (1,)), ((), ()))


def _alibi_slopes():
    s = [2.0 ** (-8.0 * i / 8) for i in range(1, 9)]
    return tuple(s[0::2]), tuple(s[1::2])


SLOPES_A, SLOPES_D = _alibi_slopes()


def _rms(x, g):
    return x * lax.rsqrt(jnp.mean(x * x, axis=-1, keepdims=True) + RMS_EPS) * g


def _exact_dot(a, sel):
    hi = a.astype(BF16)
    lo = (a - hi.astype(F32)).astype(BF16)
    return jnp.dot(hi, sel, preferred_element_type=F32) + jnp.dot(lo, sel, preferred_element_type=F32)


def _group_rms(x, g):
    w = x.shape[1]
    same = ((lax.broadcasted_iota(jnp.int32, (w, w), 0) >> 6)
            == (lax.broadcasted_iota(jnp.int32, (w, w), 1) >> 6)).astype(BF16)
    ssq = _exact_dot(x * x, same)
    return x * lax.rsqrt(ssq * (1.0 / HEAD_DIM) + RMS_EPS) * g


EVEN_AUG, ODD_AUG = HEAD_DIM, 0


def _key_aug(pos, first):
    lane = lax.broadcasted_iota(jnp.int32, (pos.shape[0], LANES), 1) - first
    hi = (pos >> 6).astype(F32)
    lo = (pos & 63).astype(F32)
    return jnp.where(lane == 0, hi, jnp.where(lane == 1, lo, jnp.where((lane == 2) | (lane == 3), 1.0, 0.0)))


def _query_aug(t, slope, first):
    lane = lax.broadcasted_iota(jnp.int32, (t.shape[0], LANES), 1) - first
    hi = (t >> 6).astype(F32) * (-64.0 * slope)
    lo = (t & 63).astype(F32) * (-slope)
    return jnp.where(lane == 0, 64.0 * slope,
                     jnp.where(lane == 1, slope, jnp.where(lane == 2, hi, jnp.where(lane == 3, lo, 0.0))))


def _pair_keys(pair, pos):
    lane = lax.broadcasted_iota(jnp.int32, pair.shape, 1)
    aug_e = 0.0 if pos is None else _key_aug(pos, EVEN_AUG)
    aug_o = 0.0 if pos is None else _key_aug(pos, ODD_AUG)
    return (jnp.where(lane < HEAD_DIM, pair, aug_e).astype(BF16),
            jnp.where(lane >= HEAD_DIM, pair, aug_o).astype(BF16))


def _prep_queries(q, g, t0, slopes, qp_s):
    qn = (q if g is None else _group_rms(q, g)) * (HEAD_DIM ** -0.5)
    tpos = t0 + lax.broadcasted_iota(jnp.int32, (q.shape[0], 1), 0)
    lane = lax.broadcasted_iota(jnp.int32, (q.shape[0], LANES), 1)
    for i in range(N_HEADS // 2):
        pair = qn[:, i * LANES:(i + 1) * LANES]
        if slopes is None:
            qp_s[i] = pair.astype(BF16)
            continue
        even = jnp.where((lane >= EVEN_AUG) & (lane < EVEN_AUG + 4), _query_aug(tpos, slopes[2 * i], EVEN_AUG), pair)
        odd = jnp.where(lane < ODD_AUG + 4, _query_aug(tpos, slopes[2 * i + 1], ODD_AUG), pair)
        qp_s[2 * i] = even.astype(BF16)
        qp_s[2 * i + 1] = odd.astype(BF16)


def _inproj_kernel(x_ref, g_ref, w_ref, o_ref, *, nchunk):
    h = _rms(x_ref[...], g_ref[...]).astype(BF16)
    cw = P_COLS // nchunk
    for c in range(nchunk):
        o_ref[:, c * cw:(c + 1) * cw] = jnp.dot(h, w_ref[:, c * cw:(c + 1) * cw], preferred_element_type=F32)


def _inproj(x2, g, w):
    t, d = x2.shape
    tm = 512
    return pl.pallas_call(
        functools.partial(_inproj_kernel, nchunk=6),
        grid=(t // tm,),
        in_specs=[pl.BlockSpec((tm, d), lambda i: (i, 0)),
                  pl.BlockSpec((1, d), lambda i: (0, 0)),
                  pl.BlockSpec((d, P_COLS), lambda i: (0, 0))],
        out_specs=pl.BlockSpec((tm, P_COLS), lambda i: (i, 0)),
        out_shape=jax.ShapeDtypeStruct((t, P_COLS), F32),
        compiler_params=pltpu.CompilerParams(dimension_semantics=("parallel",), vmem_limit_bytes=VMEM_LIMIT),
        name="inproj",
    )(x2, g, w)


def _gelu_tanh(x):
    return x * (0.5 * (1.0 + jnp.tanh(np.sqrt(2.0 / np.pi).astype(np.float32) * (x + 0.044715 * (x * x * x)))))


def _compress_kernel(kv_ref, pe_ref, w1_ref, w2_ref, gk_ref, kc_ref, vct_ref):
    nc = kv_ref.shape[0] // CMP_STRIDE
    first = None
    second = None
    for j in range(CMP_STRIDE):
        tok = kv_ref[pl.ds(j, nc, stride=CMP_STRIDE), :]
        a = jnp.dot((tok + pe_ref[j:j + 1, :]).astype(BF16), w1_ref[j], preferred_element_type=F32)
        b = jnp.dot((tok + pe_ref[CMP_STRIDE + j:CMP_STRIDE + j + 1, :]).astype(BF16), w1_ref[CMP_STRIDE + j],
                    preferred_element_type=F32)
        first = a if first is None else first + a
        second = b if second is None else second + b
    hid = first + pltpu.roll(second, nc - 1, 0)
    out = jnp.dot(_gelu_tanh(hid).astype(BF16), w2_ref[...], preferred_element_type=F32)
    kn = _group_rms(out, gk_ref[...])
    end = lax.broadcasted_iota(jnp.int32, (nc, 1), 0) * CMP_STRIDE + (CMP_LEN - 1)
    even, _ = _pair_keys(kn, end)
    _, odd = _pair_keys(pltpu.roll(kn, HEAD_DIM, 1), end)
    kc_ref[0] = jnp.concatenate([even, odd], axis=1)
    vct_ref[0] = out.T[HEAD_DIM:, :].astype(BF16)


def _compress(p2, pe, w1, w2, gk, batch, seq):
    nc = seq // CMP_STRIDE
    hid2 = w1.shape[2]
    return pl.pallas_call(
        _compress_kernel,
        grid=(batch,),
        in_specs=[pl.BlockSpec((seq, LANES), lambda i: (i, COL_KVC // LANES)),
                  pl.BlockSpec((CMP_LEN, LANES), lambda i: (0, 0)),
                  pl.BlockSpec((CMP_LEN, LANES, hid2), lambda i: (0, 0, 0)),
                  pl.BlockSpec((hid2, LANES), lambda i: (0, 0)),
                  pl.BlockSpec((1, LANES), lambda i: (0, 0))],
        out_specs=[pl.BlockSpec((1, nc, 2 * LANES), lambda i: (i, 0, 0)),
                   pl.BlockSpec((1, HEAD_DIM, nc), lambda i: (i, 0, 0))],
        out_shape=[jax.ShapeDtypeStruct((batch, nc, 2 * LANES), BF16),
                   jax.ShapeDtypeStruct((batch, HEAD_DIM, nc), BF16)],
        compiler_params=pltpu.CompilerParams(dimension_semantics=("parallel",), vmem_limit_bytes=VMEM_LIMIT),
        name="nsa_compress",
    )(p2, pe, w1, w2, gk)


def _compress_weights(pe_k, pe_v, w1_k, w2_k, w1_v, w2_v):
    hid = w1_k.shape[1]
    w1k = w1_k.reshape(CMP_LEN, HEAD_DIM, hid)
    w1v = w1_v.reshape(CMP_LEN, HEAD_DIM, hid)
    z1 = jnp.zeros_like(w1k)
    w1 = jnp.concatenate([jnp.concatenate([w1k, z1], axis=2), jnp.concatenate([z1, w1v], axis=2)], axis=1)
    z2 = jnp.zeros_like(w2_k)
    w2 = jnp.concatenate([jnp.concatenate([w2_k, z2], axis=1), jnp.concatenate([z2, w2_v], axis=1)], axis=0)
    pe = jnp.concatenate([pe_k, pe_v], axis=1)
    return pe, w1.astype(BF16), w2.astype(BF16)


def _flash_scores(kp_of, bias, qp_s, s_s, m_cur):
    m_next = []
    for h in range(N_HEADS):
        s = lax.dot_general(kp_of(h), qp_s[h], _NT, preferred_element_type=F32) + bias
        s_s[h] = s
        m_next.append(jnp.maximum(m_cur[h], jnp.max(s, axis=0, keepdims=True)))
    return tuple(m_next)


def _flash_accum(vt_of, s_s, acc_s, m_prev, m_cur, ls):
    new_l = []
    for h in range(N_HEADS):
        alpha = jnp.exp(m_prev[h] - m_cur[h])
        p = jnp.exp(s_s[h] - m_cur[h])
        new_l.append(alpha * ls[h] + jnp.sum(p, axis=0, keepdims=True))
        acc_s[h] = alpha * acc_s[h] + jnp.dot(vt_of(h), p.astype(BF16), preferred_element_type=F32)
    return tuple(new_l)


def _flash_accum_and_scores(kp_of, bias, vt_of, qp_s, s_s, acc_s, m_prev, m_cur, ls):
    def scores(h):
        return lax.dot_general(kp_of(h), qp_s[h], _NT, preferred_element_type=F32) + bias

    m_next, new_l = [], []
    s_new = {h: scores(h) for h in range(FLASH_AHEAD)}
    for h in range(N_HEADS):
        alpha = jnp.exp(m_prev[h] - m_cur[h])
        p = jnp.exp(s_s[h] - m_cur[h])
        new_l.append(alpha * ls[h] + jnp.sum(p, axis=0, keepdims=True))
        acc_s[h] = alpha * acc_s[h] + jnp.dot(vt_of(h), p.astype(BF16), preferred_element_type=F32)
        s_h = s_new.pop(h)
        s_s[h] = s_h
        m_next.append(jnp.maximum(m_cur[h], jnp.max(s_h, axis=0, keepdims=True)))
        if h + FLASH_AHEAD < N_HEADS:
            s_new[h + FLASH_AHEAD] = scores(h + FLASH_AHEAD)
    return tuple(m_next), tuple(new_l)


def _flash_pipelined(tile_of, n_tiles, kp_fn, bias_fn, vt_fn, qp_s, s_s, acc_s, o_ref):
    acc_s[...] = jnp.zeros(acc_s.shape, F32)
    m0 = tuple(jnp.full((1, TQ), NEG_INF, F32) for _ in range(N_HEADS))
    l0 = tuple(jnp.zeros((1, TQ), F32) for _ in range(N_HEADS))
    first_kt = tile_of(0)
    m1 = _flash_scores(kp_fn(first_kt), bias_fn(first_kt, True), qp_s, s_s, m0)

    def body(i, carry):
        m_prev, m_cur, ls = carry
        nxt = tile_of(i + 1)
        m_next, ls = _flash_accum_and_scores(kp_fn(nxt), bias_fn(nxt, False), vt_fn(tile_of(i)), qp_s, s_s, acc_s,
                                             m_prev, m_cur, ls)
        return m_cur, m_next, ls

    m_prev, m_cur, ls = lax.fori_loop(0, n_tiles - 1, body, (m0, m1, l0))
    ls = _flash_accum(vt_fn(tile_of(n_tiles - 1)), s_s, acc_s, m_prev, m_cur, ls)
    o_ref[0] = jnp.concatenate([acc_s[h] * (1.0 / ls[h]) for h in range(N_HEADS)], axis=0)


def _nsa_kernel(q_ref, kc_ref, vct_ref, ksv_ref, gq_ref, gks_ref, ocmp_ref, oslc_ref,
                ksp_s, vst_s, qp_s, imp_s, sel_s, s_s, acc_s, hit_s, *, seq):
    qi = pl.program_id(1)
    nc = seq // CMP_STRIDE
    nsel = seq // SEL_BLOCK
    nkt = seq // TK

    @pl.when(qi == 0)
    def _prep():
        for c in range(nkt):
            rows = slice(c * TK, (c + 1) * TK)
            blk = ksv_ref[rows, :]
            pos = c * TK + lax.broadcasted_iota(jnp.int32, (TK, 1), 0)
            kn = _group_rms(blk, gks_ref[...])
            ksp_s[rows, 0:LANES], _ = _pair_keys(kn, pos)
            _, ksp_s[rows, LANES:2 * LANES] = _pair_keys(pltpu.roll(kn, HEAD_DIM, 1), pos)
            vst_s[c] = blk.T[HEAD_DIM:, :].astype(BF16)
        imp_s[:, 0:8, :] = jnp.zeros((TQ // LANES, 8, LANES), F32)

    t0 = qi * TQ
    _prep_queries(q_ref[...], gq_ref[...], t0, SLOPES_A, qp_s)

    kc = [kc_ref[0, :, 0:LANES], kc_ref[0, :, LANES:2 * LANES]]
    vct = vct_ref[0]
    tq_row = t0 + lax.broadcasted_iota(jnp.int32, (nc, TQ), 1)
    n_idx = lax.broadcasted_iota(jnp.int32, (nc, TQ), 0)
    vis = (tq_row >= n_idx * CMP_STRIDE + (CMP_LEN - 1)) & (n_idx < nc - 1)
    vis_bias = jnp.where(vis, 0.0, NEG_INF)
    sees_any = (t0 + lax.broadcasted_iota(jnp.int32, (1, TQ), 1) >= CMP_LEN - 1).astype(F32)
    imp = jnp.zeros((nc, TQ), F32)
    ocmp_t = []
    scores = [lax.dot_general(kc[h % 2], qp_s[h], _NT, preferred_element_type=F32) + vis_bias
              for h in range(N_HEADS)]
    for h in range(N_HEADS):
        sc = scores[h]
        e = jnp.exp(sc - jnp.max(sc, axis=0, keepdims=True))
        p = e * (sees_any / jnp.sum(e, axis=0, keepdims=True))
        ocmp_t.append(jnp.dot(vct, p.astype(BF16), preferred_element_type=F32))
        imp = imp + p
    ocmp_ref[0] = jnp.concatenate(ocmp_t, axis=0)

    halves = []
    for half in range(TQ // LANES):
        imp_s[half, 8:8 + nc, :] = imp[:, half * LANES:(half + 1) * LANES]
        r = [imp_s[half, pl.ds(8 + k, nsel, stride=4), :] for k in range(4)]
        rm1 = imp_s[half, pl.ds(7, nsel, stride=4), :]
        halves.append(rm1 + 2.0 * (r[0] + r[1] + r[2]) + r[3])
    imp_blk = jnp.concatenate(halves, axis=1)
    blk = lax.broadcasted_iota(jnp.int32, (nsel, TQ), 0)
    tl = t0 + lax.broadcasted_iota(jnp.int32, (nsel, TQ), 1)
    cur = tl >> 6
    forced = (blk == 0) | (blk == cur) | (blk == cur - 1)
    valid = blk * SEL_BLOCK <= tl
    score = jnp.where(forced, FORCE_SCORE, jnp.where(valid, imp_blk, -FORCE_SCORE))
    sub = 8
    per_tile = TK // SEL_BLOCK
    tile_hit = []
    for g in range(nsel // sub):
        mine = score[g * sub:(g + 1) * sub, :]
        blk_g = g * sub + lax.broadcasted_iota(jnp.int32, (sub, TQ), 0)
        rank = jnp.zeros((sub, TQ), F32)
        for j in range(nsel):
            row = score[j:j + 1, :]
            if j < g * sub:
                beats = row >= mine
            elif j >= (g + 1) * sub:
                beats = row > mine
            else:
                beats = (row > mine) | ((row == mine) & (blk_g > j))
            rank = rank + jnp.where(beats, 1.0, 0.0)
        sel_g = jnp.where(rank < min(N_SELECT, nsel), 0.0, NEG_INF)
        sel_s[g * sub:(g + 1) * sub, :] = sel_g
        for part in range(sub // per_tile):
            tile_hit.append(jnp.max(sel_g[part * per_tile:(part + 1) * per_tile, :]))

    n_hit = jnp.int32(0)
    for kt in reversed(range(nkt)):
        hit_s[n_hit] = kt
        n_hit = n_hit + ((tile_hit[kt] > -1.0) & (kt < qi)).astype(jnp.int32)

    def block_bias(kt):
        return jnp.concatenate([jnp.broadcast_to(sel_s[pl.ds(kt * per_tile + j, 1), :], (SEL_BLOCK, TQ))
                                for j in range(per_tile)], axis=0)

    causal = lax.broadcasted_iota(jnp.int32, (TK, TQ), 0) <= lax.broadcasted_iota(jnp.int32, (TK, TQ), 1)

    def kp_fn(kt):
        off = pl.multiple_of(kt * TK, TK)
        return lambda h: ksp_s[pl.ds(off, TK), (h % 2) * LANES:(h % 2 + 1) * LANES]

    def bias_fn(kt, first):
        return jnp.where(causal, block_bias(kt), NEG_INF) if first else block_bias(kt)

    tile_of = lambda i: jnp.where(i == 0, qi, hit_s[jnp.maximum(i - 1, 0)])
    _flash_pipelined(tile_of, n_hit + 1, kp_fn, bias_fn, lambda kt: (lambda h: vst_s[kt]), qp_s, s_s, acc_s, oslc_ref)


def _nsa(p2, kc, vc, gq, gks, batch, seq):
    nq = seq // TQ
    nc = seq // CMP_STRIDE
    t = batch * seq
    return pl.pallas_call(
        functools.partial(_nsa_kernel, seq=seq),
        grid=(batch, nq),
        in_specs=[pl.BlockSpec((TQ, GROUP_W), lambda b, i: (b * nq + i, COL_QA // GROUP_W)),
                  pl.BlockSpec((1, nc, 2 * LANES), lambda b, i: (b, 0, 0)),
                  pl.BlockSpec((1, HEAD_DIM, nc), lambda b, i: (b, 0, 0)),
                  pl.BlockSpec((seq, LANES), lambda b, i: (b, COL_KSV // LANES)),
                  pl.BlockSpec((1, GROUP_W), lambda b, i: (0, 0)),
                  pl.BlockSpec((1, LANES), lambda b, i: (0, 0))],
        out_specs=[pl.BlockSpec((1, GROUP_W, TQ), lambda b, i: (b * nq + i, 0, 0)),
                   pl.BlockSpec((1, GROUP_W, TQ), lambda b, i: (b * nq + i, 0, 0))],
        out_shape=[jax.ShapeDtypeStruct((t // TQ, GROUP_W, TQ), F32),
                   jax.ShapeDtypeStruct((t // TQ, GROUP_W, TQ), F32)],
        scratch_shapes=[pltpu.VMEM((seq, 2 * LANES), BF16),
                        pltpu.VMEM((seq // TK, HEAD_DIM, TK), BF16),
                        pltpu.VMEM((N_HEADS, TQ, LANES), BF16),
                        pltpu.VMEM((TQ // LANES, 8 + nc, LANES), F32),
                        pltpu.VMEM((seq // SEL_BLOCK, TQ), F32),
                        pltpu.VMEM((N_HEADS, TK, TQ), F32),
                        pltpu.VMEM((N_HEADS, HEAD_DIM, TQ), F32),
                        pltpu.SMEM((seq // TK + 1,), jnp.int32)],
        compiler_params=pltpu.CompilerParams(dimension_semantics=("parallel", "arbitrary"),
                                             vmem_limit_bytes=VMEM_LIMIT),
        name="nsa_cmp_slc",
    )(p2, kc, vc, p2, gq, gks)


def _band_bias_tiles(window, mult_fn):
    nd = (window + TK - 1) // TK + 1
    kk = np.arange(TK)[:, None]
    qq = np.arange(TQ)[None, :]
    tiles = np.empty((nd, TK, TQ), np.float32)
    for di in range(nd):
        d = di * TK + qq - kk
        mult = mult_fn(d)
        tiles[di] = np.where(mult > 0, np.log(np.maximum(mult, 1)), NEG_INF)
    return tiles


def _window_mult(d):
    return ((d >= 0) & (d <= NSA_WINDOW - 1)).astype(np.float64)


def _dilated_mult(d):
    m = np.zeros(d.shape, np.float64)
    for window, dil in DILATED_CONFIGS:
        m += ((d >= 0) & (d <= window) & (d % dil == 0)).astype(np.float64)
    return m


def _banded_kernel(*refs, seq, window, slopes, shared_kv):
    if shared_kv:
        q_ref, kv_ref, gq_ref, gk_ref, bias_ref, o_ref, kp_s, vt_s, qp_s, s_s, acc_s = refs
    else:
        q_ref, k_ref, v_ref, gq_ref, gk_ref, bias_ref, o_ref, kp_s, vt_s, qp_s, s_s, acc_s = refs
    qi = pl.program_id(1)
    nkt = seq // TK

    @pl.when(qi == 0)
    def _prep():
        for c in range(nkt):
            rows = slice(c * TK, (c + 1) * TK)
            pos = c * TK + lax.broadcasted_iota(jnp.int32, (TK, 1), 0)
            if shared_kv:
                blk = kv_ref[rows, :]
                kn = _group_rms(blk, gk_ref[...])
                kp_s[rows, 0:LANES], _ = _pair_keys(kn, pos)
                _, kp_s[rows, LANES:2 * LANES] = _pair_keys(pltpu.roll(kn, HEAD_DIM, 1), pos)
                vt_s[c] = blk.T[HEAD_DIM:, :].astype(BF16)
            else:
                kn = _group_rms(k_ref[rows, :], gk_ref[...])
                for i in range(N_HEADS // 2):
                    even, odd = _pair_keys(kn[:, i * LANES:(i + 1) * LANES], pos)
                    kp_s[rows, 2 * i * LANES:(2 * i + 1) * LANES] = even
                    kp_s[rows, (2 * i + 1) * LANES:(2 * i + 2) * LANES] = odd
                vt_s[c] = v_ref[rows, :].T.astype(BF16)

    t0 = qi * TQ
    _prep_queries(q_ref[...], gq_ref[...], t0, slopes, qp_s)
    kt_lo = jnp.maximum(t0 - window, 0) // TK

    def kp_fn(kt):
        off = pl.multiple_of(kt * TK, TK)
        if shared_kv:
            return lambda h: kp_s[pl.ds(off, TK), (h % 2) * LANES:(h % 2 + 1) * LANES]
        return lambda h: kp_s[pl.ds(off, TK), h * LANES:(h + 1) * LANES]

    def vt_fn(kt):
        if shared_kv:
            return lambda h: vt_s[kt]
        return lambda h: vt_s[kt, h * HEAD_DIM:(h + 1) * HEAD_DIM, :]

    _flash_pipelined(lambda i: qi - i, qi + 1 - kt_lo, kp_fn, lambda kt, first: bias_ref[qi - kt], vt_fn,
                     qp_s, s_s, acc_s, o_ref)


def _banded(p2, gq, gk, bias_tiles, batch, seq, *, window, slopes, shared_kv, col_q, col_k, col_v, name):
    nq = seq // TQ
    t = batch * seq
    nd = bias_tiles.shape[0]
    n_kv = 1 if shared_kv else N_HEADS
    q_spec = pl.BlockSpec((TQ, GROUP_W), lambda b, i: (b * nq + i, col_q // GROUP_W))
    gain = lambda g: pl.BlockSpec(g.shape, lambda b, i: (0, 0))
    bias_spec = pl.BlockSpec((nd, TK, TQ), lambda b, i: (0, 0, 0))
    if shared_kv:
        kv_specs = [pl.BlockSpec((seq, LANES), lambda b, i: (b, col_k // LANES))]
        operands = (p2, p2, gq, gk, bias_tiles)
    else:
        kv_specs = [pl.BlockSpec((seq, GROUP_W), lambda b, i: (b, col_k // GROUP_W)),
                    pl.BlockSpec((seq, GROUP_W), lambda b, i: (b, col_v // GROUP_W))]
        operands = (p2, p2, p2, gq, gk, bias_tiles)
    return pl.pallas_call(
        functools.partial(_banded_kernel, seq=seq, window=window, slopes=slopes, shared_kv=shared_kv),
        grid=(batch, nq),
        in_specs=[q_spec] + kv_specs + [gain(gq), gain(gk), bias_spec],
        out_specs=pl.BlockSpec((1, GROUP_W, TQ), lambda b, i: (b * nq + i, 0, 0)),
        out_shape=jax.ShapeDtypeStruct((t // TQ, GROUP_W, TQ), F32),
        scratch_shapes=[pltpu.VMEM((seq, (2 if shared_kv else N_HEADS) * LANES), BF16),
                        pltpu.VMEM((seq // TK, n_kv * HEAD_DIM, TK), BF16),
                        pltpu.VMEM((N_HEADS, TQ, LANES), BF16),
                        pltpu.VMEM((N_HEADS, TK, TQ), F32),
                        pltpu.VMEM((N_HEADS, HEAD_DIM, TQ), F32)],
        compiler_params=pltpu.CompilerParams(dimension_semantics=("parallel", "arbitrary"),
                                             vmem_limit_bytes=VMEM_LIMIT),
        name=name,
    )(*operands)


def _stick_kernel(q_ref, k_ref, v_ref, o_ref, kp_s, vt_s, qp_s, hl_s, e_s, acc_s, *, seq):
    qi = pl.program_id(1)
    nkt = seq // TK

    @pl.when(qi == 0)
    def _prep():
        for c in range(nkt):
            rows = slice(c * TK, (c + 1) * TK)
            kb = k_ref[rows, :]
            for i in range(N_HEADS // 2):
                even, odd = _pair_keys(kb[:, i * LANES:(i + 1) * LANES], None)
                kp_s[rows, 2 * i * LANES:(2 * i + 1) * LANES] = even
                kp_s[rows, (2 * i + 1) * LANES:(2 * i + 2) * LANES] = odd
            vt_s[c] = v_ref[rows, :].T.astype(BF16)

    _prep_queries(q_ref[...], None, qi * TQ, None, qp_s)
    half = TK // 2
    col = lax.broadcasted_iota(jnp.int32, (half + SUM_PAD, TK), 1) & (half - 1)
    srow = lax.broadcasted_iota(jnp.int32, (half + SUM_PAD, TK), 0)
    sums = ((srow == half) | ((srow < half) & (col > srow))).astype(BF16)
    past = lax.broadcasted_iota(jnp.int32, (TK, TQ), 0) < lax.broadcasted_iota(jnp.int32, (TK, TQ), 1)
    acc_s[...] = jnp.zeros(acc_s.shape, F32)

    def logits(kt, h):
        off = pl.multiple_of(kt * TK, TK)
        return lax.dot_general(kp_s[pl.ds(off, TK), h * LANES:(h + 1) * LANES], qp_s[h // 2], _NT,
                               preferred_element_type=F32)

    def stage(h, z, masked):
        sp = jnp.maximum(z, 0.0) + jnp.log(1.0 + jnp.exp2(jnp.abs(z) * (-LOG2E)))
        spm = jnp.where(past, sp, 0.0) if masked else sp
        hi = spm.astype(BF16)
        lo = (spm - hi.astype(F32)).astype(BF16)
        for b in range(2):
            hl_s[h, b, 0:half, :] = hi[b * half:(b + 1) * half, :]
            hl_s[h, b, half:TK, :] = lo[b * half:(b + 1) * half, :]
        logsig = z - sp
        e_s[h] = jnp.where(past, logsig, NEG_INF) if masked else logsig

    def tails(h):
        return [jnp.dot(sums, hl_s[h, b], preferred_element_type=F32) for b in range(2)]

    def consume(kt, h, w, later):
        later_lo = later + w[1][half:half + 1, :]
        attn = jnp.concatenate([jnp.exp(e_s[h, 0:half, :] - w[0][0:half, :] - later_lo),
                                jnp.exp(e_s[h, half:TK, :] - w[1][0:half, :] - later)], axis=0)
        acc_s[h] += jnp.dot(vt_s[kt, h * HEAD_DIM:(h + 1) * HEAD_DIM, :], attn.astype(BF16),
                            preferred_element_type=F32)
        return later_lo + w[0][half:half + 1, :]

    ahead = MXU_AHEAD
    for h in range(N_HEADS):
        stage(h, logits(qi, h), True)

    def body(i, laters):
        kt = qi - i
        w = {h: tails(h) for h in range(ahead)}
        z = {h: logits(kt - 1, h) for h in range(ahead)}
        out = []
        for h in range(N_HEADS):
            out.append(consume(kt, h, w.pop(h), laters[h]))
            stage(h, z.pop(h), False)
            if h + ahead < N_HEADS:
                w[h + ahead] = tails(h + ahead)
                z[h + ahead] = logits(kt - 1, h + ahead)
        return tuple(out)

    laters = lax.fori_loop(0, qi, body, tuple(jnp.zeros((1, TQ), F32) for _ in range(N_HEADS)))
    w = {h: tails(h) for h in range(ahead)}
    for h in range(N_HEADS):
        consume(0, h, w.pop(h), laters[h])
        if h + ahead < N_HEADS:
            w[h + ahead] = tails(h + ahead)
    o_ref[0] = acc_s[...].reshape(GROUP_W, TQ)


def _stick(p2, batch, seq):
    nq = seq // TQ
    t = batch * seq
    return pl.pallas_call(
        functools.partial(_stick_kernel, seq=seq),
        grid=(batch, nq),
        in_specs=[pl.BlockSpec((TQ, GROUP_W), lambda b, i: (b * nq + i, COL_QC // GROUP_W)),
                  pl.BlockSpec((seq, GROUP_W), lambda b, i: (b, COL_KC // GROUP_W)),
                  pl.BlockSpec((seq, GROUP_W), lambda b, i: (b, COL_VC // GROUP_W))],
        out_specs=pl.BlockSpec((1, GROUP_W, TQ), lambda b, i: (b * nq + i, 0, 0)),
        out_shape=jax.ShapeDtypeStruct((t // TQ, GROUP_W, TQ), F32),
        scratch_shapes=[pltpu.VMEM((seq, N_HEADS * LANES), BF16),
                        pltpu.VMEM((seq // TK, GROUP_W, TK), BF16),
                        pltpu.VMEM((N_HEADS // 2, TQ, LANES), BF16),
                        pltpu.VMEM((N_HEADS, 2, TK, TQ), BF16),
                        pltpu.VMEM((N_HEADS, TK, TQ), F32),
                        pltpu.VMEM((N_HEADS, HEAD_DIM, TQ), F32)],
        compiler_params=pltpu.CompilerParams(dimension_semantics=("parallel", "arbitrary"),
                                             vmem_limit_bytes=VMEM_LIMIT),
        name="stick_breaking",
    )(p2, p2, p2)


def _mixout_kernel(x_ref, ocmp_ref, oslc_ref, owin_ref, gate_ref, cvb_ref, cvc_ref, cvu_ref, pc_ref, pu_ref,
                   oc_ref, od_ref, bg_ref, cw_ref, gout_ref, wout_ref, o_ref, *, tiles_per_seq):
    i = pl.program_id(0)
    d = x_ref.shape[1]

    gates = jax.nn.sigmoid(gate_ref[...] + bg_ref[...])
    src = lax.broadcasted_iota(jnp.int32, (LANES, GROUP_W), 0)
    head3 = (lax.broadcasted_iota(jnp.int32, (LANES, GROUP_W), 1) >> 6) * 3
    oa = (_exact_dot(gates, (src == head3).astype(BF16)) * ocmp_ref[0].T
          + _exact_dot(gates, (src == head3 + 1).astype(BF16)) * oslc_ref[0].T
          + _exact_dot(gates, (src == head3 + 2).astype(BF16)) * owin_ref[0].T)
    cu = cvc_ref[...] * cvu_ref[...]
    prev = jnp.where(i % tiles_per_seq == 0, 0.0, pc_ref[...] * pu_ref[...])
    full = jnp.concatenate([prev, cu], axis=0)
    back1 = pltpu.roll(full, 1, 0)[8:, :]
    back2 = pltpu.roll(full, 2, 0)[8:, :]
    cw = cw_ref[...]
    ob = cvb_ref[...] * (cw[0:1, :] * back2 + cw[1:2, :] * back1 + cw[2:3, :] * cu)
    groups = jnp.concatenate([oa, ob, oc_ref[0].T, od_ref[0].T], axis=1)
    n_groups = d // HEAD_DIM
    gather = ((lax.broadcasted_iota(jnp.int32, (d, LANES), 0) >> 6)
              == lax.broadcasted_iota(jnp.int32, (d, LANES), 1)).astype(BF16)
    spread = (lax.broadcasted_iota(jnp.int32, (LANES, d), 0)
              == (lax.broadcasted_iota(jnp.int32, (LANES, d), 1) >> 6)).astype(BF16)
    ssq = _exact_dot(groups * groups, gather)
    inv = lax.rsqrt(ssq * (1.0 / HEAD_DIM) + RMS_EPS)
    mixed = (groups * _exact_dot(inv, spread) * gout_ref[...]).astype(BF16)
    assert n_groups <= LANES
    o_ref[...] = x_ref[...] + jnp.dot(mixed, wout_ref[...], preferred_element_type=F32)


def _mixout(x2, p2, ocmp, oslc, owin, oc, od, bg, cw, gout, wout, seq):
    t, d = x2.shape
    tm = TQ
    rows = lambda w, col: pl.BlockSpec((tm, w), lambda i: (i, col // w))
    heads_t = pl.BlockSpec((1, GROUP_W, tm), lambda i: (i, 0, 0))
    prev8 = lambda col: pl.BlockSpec((8, GROUP_W), lambda i: (jnp.maximum(i * (tm // 8) - 1, 0), col // GROUP_W))
    const = lambda shape: pl.BlockSpec(shape, lambda i: (0, 0))
    return pl.pallas_call(
        functools.partial(_mixout_kernel, tiles_per_seq=seq // tm),
        grid=(t // tm,),
        in_specs=[rows(d, 0), heads_t, heads_t, heads_t,
                  rows(LANES, COL_GATE), rows(GROUP_W, COL_CVB), rows(GROUP_W, COL_CVC), rows(GROUP_W, COL_CVU),
                  prev8(COL_CVC), prev8(COL_CVU),
                  heads_t, heads_t,
                  const((1, LANES)), const((8, GROUP_W)), const((1, d)), const((d, d))],
        out_specs=rows(d, 0),
        out_shape=jax.ShapeDtypeStruct((t, d), F32),
        compiler_params=pltpu.CompilerParams(dimension_semantics=("parallel",), vmem_limit_bytes=VMEM_LIMIT),
        name="mixout",
    )(x2, ocmp, oslc, owin, p2, p2, p2, p2, p2, p2, oc, od, bg, cw, gout, wout)


def _ffn_kernel(x_ref, g_ref, wg_ref, wu_ref, wd_ref, o_ref, act_s, *, ff_chunk, out_chunk):
    h = _rms(x_ref[...], g_ref[...]).astype(BF16)
    dff = wg_ref.shape[1]
    for c in range(dff // ff_chunk):
        cols = slice(c * ff_chunk, (c + 1) * ff_chunk)
        a = jnp.dot(h, wg_ref[:, cols], preferred_element_type=F32)
        u = jnp.dot(h, wu_ref[:, cols], preferred_element_type=F32)
        act_s[:, cols] = (a * jax.nn.sigmoid(a) * u).astype(BF16)
    d = o_ref.shape[1]
    for c in range(d // out_chunk):
        cols = slice(c * out_chunk, (c + 1) * out_chunk)
        o_ref[:, cols] = x_ref[:, cols] + jnp.dot(act_s[...], wd_ref[:, cols], preferred_element_type=F32)


def _ffn(x2, g, wg, wu, wd):
    t, d = x2.shape
    dff = wg.shape[1]
    tm = 512
    resident = lambda shape: pl.BlockSpec(shape, lambda i: (0, 0), pipeline_mode=pl.Buffered(1))
    return pl.pallas_call(
        functools.partial(_ffn_kernel, ff_chunk=256, out_chunk=256),
        grid=(t // tm,),
        in_specs=[pl.BlockSpec((tm, d), lambda i: (i, 0)),
                  pl.BlockSpec((1, d), lambda i: (0, 0)),
                  resident((d, dff)), resident((d, dff)), resident((dff, d))],
        out_specs=pl.BlockSpec((tm, d), lambda i: (i, 0)),
        out_shape=jax.ShapeDtypeStruct((t, d), F32),
        scratch_shapes=[pltpu.VMEM((tm, dff), BF16)],
        compiler_params=pltpu.CompilerParams(dimension_semantics=("parallel",), vmem_limit_bytes=VMEM_LIMIT),
        name="ffn",
    )(x2, g, wg, wu, wd)


def _permute_w_in_kernel(w_ref, o_ref):
    rows = w_ref.shape[0]
    n_rest = P_COLS - COL_CVB
    o_ref[:, 0:COL_GATE] = w_ref[:, 0:COL_GATE].astype(BF16)
    lane = lax.broadcasted_iota(jnp.int32, (rows, LANES), 1)
    o_ref[:, COL_GATE:COL_GATE + LANES] = jnp.where(lane < N_GATES, w_ref[:, COL_GATE:COL_GATE + LANES], 0.0).astype(BF16)
    o_ref[:, COL_CVB:P_COLS] = w_ref[:, COL_GATE + N_GATES:COL_GATE + N_GATES + n_rest].astype(BF16)


def _permute_w_in(w_in):
    d, cols = w_in.shape
    tr = 256
    return pl.pallas_call(
        _permute_w_in_kernel,
        grid=(d // tr,),
        in_specs=[pl.BlockSpec((tr, cols), lambda i: (i, 0))],
        out_specs=pl.BlockSpec((tr, P_COLS), lambda i: (i, 0)),
        out_shape=jax.ShapeDtypeStruct((d, P_COLS), BF16),
        compiler_params=pltpu.CompilerParams(dimension_semantics=("parallel",), vmem_limit_bytes=VMEM_LIMIT),
        name="permute_w_in",
    )(w_in)


def _layer(x2, batch, seq, g_mix, w_in, b_gate, g_q_nsa, g_k_cmp, g_k_slc, g_k_win, pe_k_cmp, pe_v_cmp,
           w1_k_cmp, w2_k_cmp, w1_v_cmp, w2_v_cmp, conv_w, g_q_dil, g_k_dil, g_out, w_out,
           g_ffn, w_gate, w_up, w_down):
    row = lambda v: v.reshape(1, -1)
    per_group = lambda g, n: jnp.tile(g, n).reshape(1, -1)
    p2 = _inproj(x2, row(g_mix), _permute_w_in(w_in))

    pe, w1, w2 = _compress_weights(pe_k_cmp, pe_v_cmp, w1_k_cmp, w2_k_cmp, w1_v_cmp, w2_v_cmp)
    kc, vc = _compress(p2, pe, w1, w2, per_group(g_k_cmp, 2), batch, seq)
    ocmp, oslc = _nsa(p2, kc, vc, per_group(g_q_nsa, N_HEADS), per_group(g_k_slc, 2), batch, seq)
    owin = _banded(p2, per_group(g_q_nsa, N_HEADS), per_group(g_k_win, 2), jnp.asarray(_band_bias_tiles(NSA_WINDOW - 1, _window_mult)),
                   batch, seq, window=NSA_WINDOW - 1, slopes=SLOPES_A, shared_kv=True,
                   col_q=COL_QA, col_k=COL_KWV, col_v=COL_KWV, name="nsa_window")
    od = _banded(p2, per_group(g_q_dil, N_HEADS), per_group(g_k_dil, N_HEADS),
                 jnp.asarray(_band_bias_tiles(DILATED_CONFIGS[-1][0], _dilated_mult)),
                 batch, seq, window=DILATED_CONFIGS[-1][0], slopes=SLOPES_D, shared_kv=False,
                 col_q=COL_QD, col_k=COL_KD, col_v=COL_VD, name="dilated")
    oc = _stick(p2, batch, seq)

    bg = jnp.zeros((1, LANES), F32).at[0, :N_GATES].set(b_gate)
    cwp = jnp.zeros((8, GROUP_W), F32).at[:CONV_K].set(conv_w)
    x1 = _mixout(x2, p2, ocmp, oslc, owin, oc, od, bg, cwp, row(g_out), w_out.astype(BF16), seq)
    return _ffn(x1, row(g_ffn), w_gate.astype(BF16), w_up.astype(BF16), w_down.astype(BF16))


def kernel(x, g_mix, w_in, b_gate, g_q_nsa, g_k_cmp, g_k_slc, g_k_win, pe_k_cmp, pe_v_cmp, w1_k_cmp, w2_k_cmp,
           w1_v_cmp, w2_v_cmp, conv_w, g_q_dil, g_k_dil, g_out, w_out, g_ffn, w_gate, w_up, w_down):
    batch, seq, d = x.shape
    assert seq % (CMP_STRIDE * LANES) == 0 and d % LANES == 0
    x2 = x.reshape(batch * seq, d)
    params = (g_mix, w_in, b_gate, g_q_nsa, g_k_cmp, g_k_slc, g_k_win, pe_k_cmp, pe_v_cmp, w1_k_cmp, w2_k_cmp,
              w1_v_cmp, w2_v_cmp, conv_w, g_q_dil, g_k_dil, g_out, w_out, g_ffn, w_gate, w_up, w_down)
    for layer in range(g_mix.shape[0]):
        x2 = _layer(x2, batch, seq, *[p[layer] for p in params])
    return x2.reshape(batch, seq, d)
```

```python
import functools

import numpy as np
import jax
import jax.numpy as jnp
from jax import lax
from jax.experimental import pallas as pl
from jax.experimental.pallas import tpu as pltpu

F32 = jnp.float32
BF16 = jnp.bfloat16

HEAD_DIM = 64
N_HEADS = 4
GROUP_W = N_HEADS * HEAD_DIM
CONV_K = 3
CMP_LEN = 32
CMP_STRIDE = 16
SEL_BLOCK = 64
N_SELECT = 16
NSA_WINDOW = 512
DILATED_CONFIGS = ((128, 1), (512, 4), (2048, 16))
NEG_INF = -1e30
FORCE_SCORE = 1e6
RMS_EPS = 1e-6

TQ = 256
TK = 256
LANES = 128
MXU_AHEAD = 2
FLASH_AHEAD = 2
SUM_PAD = 16
LOG2E = 1.4426950408889634
VMEM_LIMIT = 52 * 1024 * 1024

COL_QA, COL_KVC, COL_KSV, COL_KWV, COL_GATE = 0, 256, 384, 512, 640
COL_CVB, COL_CVC, COL_CVU = 768, 1024, 1280
COL_QC, COL_KC, COL_VC = 1536, 1792, 2048
COL_QD, COL_KD, COL_VD = 2304, 2560, 2816
P_COLS = 3072
N_GATES = 12

_NT = (((1,), (1,)), ((), ()))


def _alibi_slopes():
    s = [2.0 ** (-8.0 * i / 8) for i in range(1, 9)]
    return tuple(s[0::2]), tuple(s[1::2])


SLOPES_A, SLOPES_D = _alibi_slopes()


def _rms(x, g):
    return x * lax.rsqrt(jnp.mean(x * x, axis=-1, keepdims=True) + RMS_EPS) * g


def _exact_dot(a, sel):
    hi = a.astype(BF16)
    lo = (a - hi.astype(F32)).astype(BF16)
    return jnp.dot(hi, sel, preferred_element_type=F32) + jnp.dot(lo, sel, preferred_element_type=F32)


def _group_rms(x, g):
    w = x.shape[1]
    same = ((lax.broadcasted_iota(jnp.int32, (w, w), 0) >> 6)
            == (lax.broadcasted_iota(jnp.int32, (w, w), 1) >> 6)).astype(BF16)
    ssq = _exact_dot(x * x, same)
    return x * lax.rsqrt(ssq * (1.0 / HEAD_DIM) + RMS_EPS) * g


EVEN_AUG, ODD_AUG = HEAD_DIM, 0


def _key_aug(pos, first):
    lane = lax.broadcasted_iota(jnp.int32, (pos.shape[0], LANES), 1) - first
    hi = (pos >> 6).astype(F32)
    lo = (pos & 63).astype(F32)
    return jnp.where(lane == 0, hi, jnp.where(lane == 1, lo, jnp.where((lane == 2) | (lane == 3), 1.0, 0.0)))


def _query_aug(t, slope, first):
    lane = lax.broadcasted_iota(jnp.int32, (t.shape[0], LANES), 1) - first
    hi = (t >> 6).astype(F32) * (-64.0 * slope)
    lo = (t & 63).astype(F32) * (-slope)
    return jnp.where(lane == 0, 64.0 * slope,
                     jnp.where(lane == 1, slope, jnp.where(lane == 2, hi, jnp.where(lane == 3, lo, 0.0))))


def _pair_keys(pair, pos):
    lane = lax.broadcasted_iota(jnp.int32, pair.shape, 1)
    aug_e = 0.0 if pos is None else _key_aug(pos, EVEN_AUG)
    aug_o = 0.0 if pos is None else _key_aug(pos, ODD_AUG)
    return (jnp.where(lane < HEAD_DIM, pair, aug_e).astype(BF16),
            jnp.where(lane >= HEAD_DIM, pair, aug_o).astype(BF16))


def _prep_queries(q, g, t0, slopes, qp_s):
    qn = (q if g is None else _group_rms(q, g)) * (HEAD_DIM ** -0.5)
    tpos = t0 + lax.broadcasted_iota(jnp.int32, (q.shape[0], 1), 0)
    lane = lax.broadcasted_iota(jnp.int32, (q.shape[0], LANES), 1)
    for i in range(N_HEADS // 2):
        pair = qn[:, i * LANES:(i + 1) * LANES]
        if slopes is None:
            qp_s[i] = pair.astype(BF16)
            continue
        even = jnp.where((lane >= EVEN_AUG) & (lane < EVEN_AUG + 4), _query_aug(tpos, slopes[2 * i], EVEN_AUG), pair)
        odd = jnp.where(lane < ODD_AUG + 4, _query_aug(tpos, slopes[2 * i + 1], ODD_AUG), pair)
        qp_s[2 * i] = even.astype(BF16)
        qp_s[2 * i + 1] = odd.astype(BF16)


def _inproj_kernel(x_ref, g_ref, w_ref, o_ref, *, nchunk):
    h = _rms(x_ref[...], g_ref[...]).astype(BF16)
    cw = P_COLS // nchunk
    for c in range(nchunk):
        o_ref[:, c * cw:(c + 1) * cw] = jnp.dot(h, w_ref[:, c * cw:(c + 1) * cw], preferred_element_type=F32)


def _inproj(x2, g, w):
    t, d = x2.shape
    tm = 512
    return pl.pallas_call(
        functools.partial(_inproj_kernel, nchunk=6),
        grid=(t // tm,),
        in_specs=[pl.BlockSpec((tm, d), lambda i: (i, 0)),
                  pl.BlockSpec((1, d), lambda i: (0, 0)),
                  pl.BlockSpec((d, P_COLS), lambda i: (0, 0))],
        out_specs=pl.BlockSpec((tm, P_COLS), lambda i: (i, 0)),
        out_shape=jax.ShapeDtypeStruct((t, P_COLS), F32),
        compiler_params=pltpu.CompilerParams(dimension_semantics=("parallel",), vmem_limit_bytes=VMEM_LIMIT),
        name="inproj",
    )(x2, g, w)


def _gelu_tanh(x):
    return x * (0.5 * (1.0 + jnp.tanh(np.sqrt(2.0 / np.pi).astype(np.float32) * (x + 0.044715 * (x * x * x)))))


def _compress_kernel(kv_ref, pe_ref, w1_ref, w2_ref, gk_ref, kc_ref, vct_ref):
    nc = kv_ref.shape[0] // CMP_STRIDE
    first = None
    second = None
    for j in range(CMP_STRIDE):
        tok = kv_ref[pl.ds(j, nc, stride=CMP_STRIDE), :]
        a = jnp.dot((tok + pe_ref[j:j + 1, :]).astype(BF16), w1_ref[j], preferred_element_type=F32)
        b = jnp.dot((tok + pe_ref[CMP_STRIDE + j:CMP_STRIDE + j + 1, :]).astype(BF16), w1_ref[CMP_STRIDE + j],
                    preferred_element_type=F32)
        first = a if first is None else first + a
        second = b if second is None else second + b
    hid = first + pltpu.roll(second, nc - 1, 0)
    out = jnp.dot(_gelu_tanh(hid).astype(BF16), w2_ref[...], preferred_element_type=F32)
    kn = _group_rms(out, gk_ref[...])
    end = lax.broadcasted_iota(jnp.int32, (nc, 1), 0) * CMP_STRIDE + (CMP_LEN - 1)
    even, _ = _pair_keys(kn, end)
    _, odd = _pair_keys(pltpu.roll(kn, HEAD_DIM, 1), end)
    kc_ref[0] = jnp.concatenate([even, odd], axis=1)
    vct_ref[0] = out.T[HEAD_DIM:, :].astype(BF16)


def _compress(p2, pe, w1, w2, gk, batch, seq):
    nc = seq // CMP_STRIDE
    hid2 = w1.shape[2]
    return pl.pallas_call(
        _compress_kernel,
        grid=(batch,),
        in_specs=[pl.BlockSpec((seq, LANES), lambda i: (i, COL_KVC // LANES)),
                  pl.BlockSpec((CMP_LEN, LANES), lambda i: (0, 0)),
                  pl.BlockSpec((CMP_LEN, LANES, hid2), lambda i: (0, 0, 0)),
                  pl.BlockSpec((hid2, LANES), lambda i: (0, 0)),
                  pl.BlockSpec((1, LANES), lambda i: (0, 0))],
        out_specs=[pl.BlockSpec((1, nc, 2 * LANES), lambda i: (i, 0, 0)),
                   pl.BlockSpec((1, HEAD_DIM, nc), lambda i: (i, 0, 0))],
        out_shape=[jax.ShapeDtypeStruct((batch, nc, 2 * LANES), BF16),
                   jax.ShapeDtypeStruct((batch, HEAD_DIM, nc), BF16)],
        compiler_params=pltpu.CompilerParams(dimension_semantics=("parallel",), vmem_limit_bytes=VMEM_LIMIT),
        name="nsa_compress",
    )(p2, pe, w1, w2, gk)


def _compress_weights(pe_k, pe_v, w1_k, w2_k, w1_v, w2_v):
    hid = w1_k.shape[1]
    w1k = w1_k.reshape(CMP_LEN, HEAD_DIM, hid)
    w1v = w1_v.reshape(CMP_LEN, HEAD_DIM, hid)
    z1 = jnp.zeros_like(w1k)
    w1 = jnp.concatenate([jnp.concatenate([w1k, z1], axis=2), jnp.concatenate([z1, w1v], axis=2)], axis=1)
    z2 = jnp.zeros_like(w2_k)
    w2 = jnp.concatenate([jnp.concatenate([w2_k, z2], axis=1), jnp.concatenate([z2, w2_v], axis=1)], axis=0)
    pe = jnp.concatenate([pe_k, pe_v], axis=1)
    return pe, w1.astype(BF16), w2.astype(BF16)


def _flash_scores(kp_of, bias, qp_s, s_s, m_cur):
    m_next = []
    for h in range(N_HEADS):
        s = lax.dot_general(kp_of(h), qp_s[h], _NT, preferred_element_type=F32) + bias
        s_s[h] = s
        m_next.append(jnp.maximum(m_cur[h], jnp.max(s, axis=0, keepdims=True)))
    return tuple(m_next)


def _flash_accum(vt_of, s_s, acc_s, m_prev, m_cur, ls):
    new_l = []
    for h in range(N_HEADS):
        alpha = jnp.exp(m_prev[h] - m_cur[h])
        p = jnp.exp(s_s[h] - m_cur[h])
        new_l.append(alpha * ls[h] + jnp.sum(p, axis=0, keepdims=True))
        acc_s[h] = alpha * acc_s[h] + jnp.dot(vt_of(h), p.astype(BF16), preferred_element_type=F32)
    return tuple(new_l)


def _flash_accum_and_scores(kp_of, bias, vt_of, qp_s, s_s, acc_s, m_prev, m_cur, ls):
    def scores(h):
        return lax.dot_general(kp_of(h), qp_s[h], _NT, preferred_element_type=F32) + bias

    m_next, new_l = [], []
    s_new = {h: scores(h) for h in range(FLASH_AHEAD)}
    for h in range(N_HEADS):
        alpha = jnp.exp(m_prev[h] - m_cur[h])
        p = jnp.exp(s_s[h] - m_cur[h])
        new_l.append(alpha * ls[h] + jnp.sum(p, axis=0, keepdims=True))
        acc_s[h] = alpha * acc_s[h] + jnp.dot(vt_of(h), p.astype(BF16), preferred_element_type=F32)
        s_h = s_new.pop(h)
        s_s[h] = s_h
        m_next.append(jnp.maximum(m_cur[h], jnp.max(s_h, axis=0, keepdims=True)))
        if h + FLASH_AHEAD < N_HEADS:
            s_new[h + FLASH_AHEAD] = scores(h + FLASH_AHEAD)
    return tuple(m_next), tuple(new_l)


def _flash_pipelined(tile_of, n_tiles, kp_fn, bias_fn, vt_fn, qp_s, s_s, acc_s, o_ref):
    acc_s[...] = jnp.zeros(acc_s.shape, F32)
    m0 = tuple(jnp.full((1, TQ), NEG_INF, F32) for _ in range(N_HEADS))
    l0 = tuple(jnp.zeros((1, TQ), F32) for _ in range(N_HEADS))
    first_kt = tile_of(0)
    m1 = _flash_scores(kp_fn(first_kt), bias_fn(first_kt, True), qp_s, s_s, m0)

    def body(i, carry):
        m_prev, m_cur, ls = carry
        nxt = tile_of(i + 1)
        m_next, ls = _flash_accum_and_scores(kp_fn(nxt), bias_fn(nxt, False), vt_fn(tile_of(i)), qp_s, s_s, acc_s,
                                             m_prev, m_cur, ls)
        return m_cur, m_next, ls

    m_prev, m_cur, ls = lax.fori_loop(0, n_tiles - 1, body, (m0, m1, l0))
    ls = _flash_accum(vt_fn(tile_of(n_tiles - 1)), s_s, acc_s, m_prev, m_cur, ls)
    o_ref[0] = jnp.concatenate([acc_s[h] * (1.0 / ls[h]) for h in range(N_HEADS)], axis=0)


def _nsa_kernel(q_ref, kc_ref, vct_ref, ksv_ref, gq_ref, gks_ref, ocmp_ref, oslc_ref,
                ksp_s, vst_s, qp_s, imp_s, sel_s, s_s, acc_s, hit_s, *, seq):
    qi = pl.program_id(1)
    nc = seq // CMP_STRIDE
    nsel = seq // SEL_BLOCK
    nkt = seq // TK

    @pl.when(qi == 0)
    def _prep():
        for c in range(nkt):
            rows = slice(c * TK, (c + 1) * TK)
            blk = ksv_ref[rows, :]
            pos = c * TK + lax.broadcasted_iota(jnp.int32, (TK, 1), 0)
            kn = _group_rms(blk, gks_ref[...])
            ksp_s[rows, 0:LANES], _ = _pair_keys(kn, pos)
            _, ksp_s[rows, LANES:2 * LANES] = _pair_keys(pltpu.roll(kn, HEAD_DIM, 1), pos)
            vst_s[c] = blk.T[HEAD_DIM:, :].astype(BF16)
        imp_s[:, 0:8, :] = jnp.zeros((TQ // LANES, 8, LANES), F32)

    t0 = qi * TQ
    _prep_queries(q_ref[...], gq_ref[...], t0, SLOPES_A, qp_s)

    kc = [kc_ref[0, :, 0:LANES], kc_ref[0, :, LANES:2 * LANES]]
    vct = vct_ref[0]
    tq_row = t0 + lax.broadcasted_iota(jnp.int32, (nc, TQ), 1)
    n_idx = lax.broadcasted_iota(jnp.int32, (nc, TQ), 0)
    vis = (tq_row >= n_idx * CMP_STRIDE + (CMP_LEN - 1)) & (n_idx < nc - 1)
    vis_bias = jnp.where(vis, 0.0, NEG_INF)
    sees_any = (t0 + lax.broadcasted_iota(jnp.int32, (1, TQ), 1) >= CMP_LEN - 1).astype(F32)
    imp = jnp.zeros((nc, TQ), F32)
    ocmp_t = []
    scores = [lax.dot_general(kc[h % 2], qp_s[h], _NT, preferred_element_type=F32) + vis_bias
              for h in range(N_HEADS)]
    for h in range(N_HEADS):
        sc = scores[h]
        e = jnp.exp(sc - jnp.max(sc, axis=0, keepdims=True))
        p = e * (sees_any / jnp.sum(e, axis=0, keepdims=True))
        ocmp_t.append(jnp.dot(vct, p.astype(BF16), preferred_element_type=F32))
        imp = imp + p
    ocmp_ref[0] = jnp.concatenate(ocmp_t, axis=0)

    halves = []
    for half in range(TQ // LANES):
        imp_s[half, 8:8 + nc, :] = imp[:, half * LANES:(half + 1) * LANES]
        r = [imp_s[half, pl.ds(8 + k, nsel, stride=4), :] for k in range(4)]
        rm1 = imp_s[half, pl.ds(7, nsel, stride=4), :]
        halves.append(rm1 + 2.0 * (r[0] + r[1] + r[2]) + r[3])
    imp_blk = jnp.concatenate(halves, axis=1)
    blk = lax.broadcasted_iota(jnp.int32, (nsel, TQ), 0)
    tl = t0 + lax.broadcasted_iota(jnp.int32, (nsel, TQ), 1)
    cur = tl >> 6
    forced = (blk == 0) | (blk == cur) | (blk == cur - 1)
    valid = blk * SEL_BLOCK <= tl
    score = jnp.where(forced, FORCE_SCORE, jnp.where(valid, imp_blk, -FORCE_SCORE))
    sub = 8
    per_tile = TK // SEL_BLOCK
    tile_hit = []
    for g in range(nsel // sub):
        mine = score[g * sub:(g + 1) * sub, :]
        blk_g = g * sub + lax.broadcasted_iota(jnp.int32, (sub, TQ), 0)
        rank = jnp.zeros((sub, TQ), F32)
        for j in range(nsel):
            row = score[j:j + 1, :]
            if j < g * sub:
                beats = row >= mine
            elif j >= (g + 1) * sub:
                beats = row > mine
            else:
                beats = (row > mine) | ((row == mine) & (blk_g > j))
            rank = rank + jnp.where(beats, 1.0, 0.0)
        sel_g = jnp.where(rank < min(N_SELECT, nsel), 0.0, NEG_INF)
        sel_s[g * sub:(g + 1) * sub, :] = sel_g
        for part in range(sub // per_tile):
            tile_hit.append(jnp.max(sel_g[part * per_tile:(part + 1) * per_tile, :]))

    n_hit = jnp.int32(0)
    for kt in reversed(range(nkt)):
        hit_s[n_hit] = kt
        n_hit = n_hit + ((tile_hit[kt] > -1.0) & (kt < qi)).astype(jnp.int32)

    def block_bias(kt):
        return jnp.concatenate([jnp.broadcast_to(sel_s[pl.ds(kt * per_tile + j, 1), :], (SEL_BLOCK, TQ))
                                for j in range(per_tile)], axis=0)

    causal = lax.broadcasted_iota(jnp.int32, (TK, TQ), 0) <= lax.broadcasted_iota(jnp.int32, (TK, TQ), 1)

    def kp_fn(kt):
        off = pl.multiple_of(kt * TK, TK)
        return lambda h: ksp_s[pl.ds(off, TK), (h % 2) * LANES:(h % 2 + 1) * LANES]

    def bias_fn(kt, first):
        return jnp.where(causal, block_bias(kt), NEG_INF) if first else block_bias(kt)

    tile_of = lambda i: jnp.where(i == 0, qi, hit_s[jnp.maximum(i - 1, 0)])
    _flash_pipelined(tile_of, n_hit + 1, kp_fn, bias_fn, lambda kt: (lambda h: vst_s[kt]), qp_s, s_s, acc_s, oslc_ref)


def _nsa(p2, kc, vc, gq, gks, batch, seq):
    nq = seq // TQ
    nc = seq // CMP_STRIDE
    t = batch * seq
    return pl.pallas_call(
        functools.partial(_nsa_kernel, seq=seq),
        grid=(batch, nq),
        in_specs=[pl.BlockSpec((TQ, GROUP_W), lambda b, i: (b * nq + i, COL_QA // GROUP_W)),
                  pl.BlockSpec((1, nc, 2 * LANES), lambda b, i: (b, 0, 0)),
                  pl.BlockSpec((1, HEAD_DIM, nc), lambda b, i: (b, 0, 0)),
                  pl.BlockSpec((seq, LANES), lambda b, i: (b, COL_KSV // LANES)),
                  pl.BlockSpec((1, GROUP_W), lambda b, i: (0, 0)),
                  pl.BlockSpec((1, LANES), lambda b, i: (0, 0))],
        out_specs=[pl.BlockSpec((1, GROUP_W, TQ), lambda b, i: (b * nq + i, 0, 0)),
                   pl.BlockSpec((1, GROUP_W, TQ), lambda b, i: (b * nq + i, 0, 0))],
        out_shape=[jax.ShapeDtypeStruct((t // TQ, GROUP_W, TQ), F32),
                   jax.ShapeDtypeStruct((t // TQ, GROUP_W, TQ), F32)],
        scratch_shapes=[pltpu.VMEM((seq, 2 * LANES), BF16),
                        pltpu.VMEM((seq // TK, HEAD_DIM, TK), BF16),
                        pltpu.VMEM((N_HEADS, TQ, LANES), BF16),
                        pltpu.VMEM((TQ // LANES, 8 + nc, LANES), F32),
                        pltpu.VMEM((seq // SEL_BLOCK, TQ), F32),
                        pltpu.VMEM((N_HEADS, TK, TQ), F32),
                        pltpu.VMEM((N_HEADS, HEAD_DIM, TQ), F32),
                        pltpu.SMEM((seq // TK + 1,), jnp.int32)],
        compiler_params=pltpu.CompilerParams(dimension_semantics=("parallel", "arbitrary"),
                                             vmem_limit_bytes=VMEM_LIMIT),
        name="nsa_cmp_slc",
    )(p2, kc, vc, p2, gq, gks)


def _band_bias_tiles(window, mult_fn):
    nd = (window + TK - 1) // TK + 1
    kk = np.arange(TK)[:, None]
    qq = np.arange(TQ)[None, :]
    tiles = np.empty((nd, TK, TQ), np.float32)
    for di in range(nd):
        d = di * TK + qq - kk
        mult = mult_fn(d)
        tiles[di] = np.where(mult > 0, np.log(np.maximum(mult, 1)), NEG_INF)
    return tiles


def _window_mult(d):
    return ((d >= 0) & (d <= NSA_WINDOW - 1)).astype(np.float64)


def _dilated_mult(d):
    m = np.zeros(d.shape, np.float64)
    for window, dil in DILATED_CONFIGS:
        m += ((d >= 0) & (d <= window) & (d % dil == 0)).astype(np.float64)
    return m


def _banded_kernel(*refs, seq, window, slopes, shared_kv):
    if shared_kv:
        q_ref, kv_ref, gq_ref, gk_ref, bias_ref, o_ref, kp_s, vt_s, qp_s, s_s, acc_s = refs
    else:
        q_ref, k_ref, v_ref, gq_ref, gk_ref, bias_ref, o_ref, kp_s, vt_s, qp_s, s_s, acc_s = refs
    qi = pl.program_id(1)
    nkt = seq // TK

    @pl.when(qi == 0)
    def _prep():
        for c in range(nkt):
            rows = slice(c * TK, (c + 1) * TK)
            pos = c * TK + lax.broadcasted_iota(jnp.int32, (TK, 1), 0)
            if shared_kv:
                blk = kv_ref[rows, :]
                kn = _group_rms(blk, gk_ref[...])
                kp_s[rows, 0:LANES], _ = _pair_keys(kn, pos)
                _, kp_s[rows, LANES:2 * LANES] = _pair_keys(pltpu.roll(kn, HEAD_DIM, 1), pos)
                vt_s[c] = blk.T[HEAD_DIM:, :].astype(BF16)
            else:
                kn = _group_rms(k_ref[rows, :], gk_ref[...])
                for i in range(N_HEADS // 2):
                    even, odd = _pair_keys(kn[:, i * LANES:(i + 1) * LANES], pos)
                    kp_s[rows, 2 * i * LANES:(2 * i + 1) * LANES] = even
                    kp_s[rows, (2 * i + 1) * LANES:(2 * i + 2) * LANES] = odd
                vt_s[c] = v_ref[rows, :].T.astype(BF16)

    t0 = qi * TQ
    _prep_queries(q_ref[...], gq_ref[...], t0, slopes, qp_s)
    kt_lo = jnp.maximum(t0 - window, 0) // TK

    def kp_fn(kt):
        off = pl.multiple_of(kt * TK, TK)
        if shared_kv:
            return lambda h: kp_s[pl.ds(off, TK), (h % 2) * LANES:(h % 2 + 1) * LANES]
        return lambda h: kp_s[pl.ds(off, TK), h * LANES:(h + 1) * LANES]

    def vt_fn(kt):
        if shared_kv:
            return lambda h: vt_s[kt]
        return lambda h: vt_s[kt, h * HEAD_DIM:(h + 1) * HEAD_DIM, :]

    _flash_pipelined(lambda i: qi - i, qi + 1 - kt_lo, kp_fn, lambda kt, first: bias_ref[qi - kt], vt_fn,
                     qp_s, s_s, acc_s, o_ref)


def _banded(p2, gq, gk, bias_tiles, batch, seq, *, window, slopes, shared_kv, col_q, col_k, col_v, name):
    nq = seq // TQ
    t = batch * seq
    nd = bias_tiles.shape[0]
    n_kv = 1 if shared_kv else N_HEADS
    q_spec = pl.BlockSpec((TQ, GROUP_W), lambda b, i: (b * nq + i, col_q // GROUP_W))
    gain = lambda g: pl.BlockSpec(g.shape, lambda b, i: (0, 0))
    bias_spec = pl.BlockSpec((nd, TK, TQ), lambda b, i: (0, 0, 0))
    if shared_kv:
        kv_specs = [pl.BlockSpec((seq, LANES), lambda b, i: (b, col_k // LANES))]
        operands = (p2, p2, gq, gk, bias_tiles)
    else:
        kv_specs = [pl.BlockSpec((seq, GROUP_W), lambda b, i: (b, col_k // GROUP_W)),
                    pl.BlockSpec((seq, GROUP_W), lambda b, i: (b, col_v // GROUP_W))]
        operands = (p2, p2, p2, gq, gk, bias_tiles)
    return pl.pallas_call(
        functools.partial(_banded_kernel, seq=seq, window=window, slopes=slopes, shared_kv=shared_kv),
        grid=(batch, nq),
        in_specs=[q_spec] + kv_specs + [gain(gq), gain(gk), bias_spec],
        out_specs=pl.BlockSpec((1, GROUP_W, TQ), lambda b, i: (b * nq + i, 0, 0)),
        out_shape=jax.ShapeDtypeStruct((t // TQ, GROUP_W, TQ), F32),
        scratch_shapes=[pltpu.VMEM((seq, (2 if shared_kv else N_HEADS) * LANES), BF16),
                        pltpu.VMEM((seq // TK, n_kv * HEAD_DIM, TK), BF16),
                        pltpu.VMEM((N_HEADS, TQ, LANES), BF16),
                        pltpu.VMEM((N_HEADS, TK, TQ), F32),
                        pltpu.VMEM((N_HEADS, HEAD_DIM, TQ), F32)],
        compiler_params=pltpu.CompilerParams(dimension_semantics=("parallel", "arbitrary"),
                                             vmem_limit_bytes=VMEM_LIMIT),
        name=name,
    )(*operands)


def _stick_kernel(q_ref, k_ref, v_ref, o_ref, kp_s, vt_s, qp_s, hl_s, e_s, acc_s, *, seq):
    qi = pl.program_id(1)
    nkt = seq // TK

    @pl.when(qi == 0)
    def _prep():
        for c in range(nkt):
            rows = slice(c * TK, (c + 1) * TK)
            kb = k_ref[rows, :]
            for i in range(N_HEADS // 2):
                even, odd = _pair_keys(kb[:, i * LANES:(i + 1) * LANES], None)
                kp_s[rows, 2 * i * LANES:(2 * i + 1) * LANES] = even
                kp_s[rows, (2 * i + 1) * LANES:(2 * i + 2) * LANES] = odd
            vt_s[c] = v_ref[rows, :].T.astype(BF16)

    _prep_queries(q_ref[...], None, qi * TQ, None, qp_s)
    half = TK // 2
    col = lax.broadcasted_iota(jnp.int32, (half + SUM_PAD, TK), 1) & (half - 1)
    srow = lax.broadcasted_iota(jnp.int32, (half + SUM_PAD, TK), 0)
    sums = ((srow == half) | ((srow < half) & (col > srow))).astype(BF16)
    past = lax.broadcasted_iota(jnp.int32, (TK, TQ), 0) < lax.broadcasted_iota(jnp.int32, (TK, TQ), 1)
    acc_s[...] = jnp.zeros(acc_s.shape, F32)

    def logits(kt, h):
        off = pl.multiple_of(kt * TK, TK)
        return lax.dot_general(kp_s[pl.ds(off, TK), h * LANES:(h + 1) * LANES], qp_s[h // 2], _NT,
                               preferred_element_type=F32)

    def stage(h, z, masked):
        sp = jnp.maximum(z, 0.0) + jnp.log(1.0 + jnp.exp2(jnp.abs(z) * (-LOG2E)))
        spm = jnp.where(past, sp, 0.0) if masked else sp
        hi = spm.astype(BF16)
        lo = (spm - hi.astype(F32)).astype(BF16)
        for b in range(2):
            hl_s[h, b, 0:half, :] = hi[b * half:(b + 1) * half, :]
            hl_s[h, b, half:TK, :] = lo[b * half:(b + 1) * half, :]
        logsig = z - sp
        e_s[h] = jnp.where(past, logsig, NEG_INF) if masked else logsig

    def tails(h):
        return [jnp.dot(sums, hl_s[h, b], preferred_element_type=F32) for b in range(2)]

    def consume(kt, h, w, later):
        later_lo = later + w[1][half:half + 1, :]
        attn = jnp.concatenate([jnp.exp(e_s[h, 0:half, :] - w[0][0:half, :] - later_lo),
                                jnp.exp(e_s[h, half:TK, :] - w[1][0:half, :] - later)], axis=0)
        acc_s[h] += jnp.dot(vt_s[kt, h * HEAD_DIM:(h + 1) * HEAD_DIM, :], attn.astype(BF16),
                            preferred_element_type=F32)
        return later_lo + w[0][half:half + 1, :]

    ahead = MXU_AHEAD
    for h in range(N_HEADS):
        stage(h, logits(qi, h), True)

    def body(i, carry):
        laters, w_first, z_first = carry
        kt = qi - i
        w = {0: list(w_first)}
        z = {0: z_first}
        for h in range(1, ahead):
            w[h] = tails(h)
            z[h] = logits(kt - 1, h)
        out = []
        for h in range(N_HEADS):
            out.append(consume(kt, h, w.pop(h), laters[h]))
            stage(h, z.pop(h), False)
            if h + ahead < N_HEADS:
                w[h + ahead] = tails(h + ahead)
                z[h + ahead] = logits(kt - 1, h + ahead)
            elif h + ahead == N_HEADS:
                nxt = (tuple(tails(0)), logits(jnp.maximum(kt - 2, 0), 0))
        return tuple(out), nxt[0], nxt[1]

    init = (tuple(jnp.zeros((1, TQ), F32) for _ in range(N_HEADS)),
            tuple(tails(0)), logits(jnp.maximum(qi - 1, 0), 0))
    laters, w_first, _ = lax.fori_loop(0, qi, body, init)
    w = {0: list(w_first)}
    for h in range(1, ahead):
        w[h] = tails(h)
    for h in range(N_HEADS):
        consume(0, h, w.pop(h), laters[h])
        if h + ahead < N_HEADS:
            w[h + ahead] = tails(h + ahead)
    o_ref[0] = acc_s[...].reshape(GROUP_W, TQ)


def _stick(p2, batch, seq):
    nq = seq // TQ
    t = batch * seq
    return pl.pallas_call(
        functools.partial(_stick_kernel, seq=seq),
        grid=(batch, nq),
        in_specs=[pl.BlockSpec((TQ, GROUP_W), lambda b, i: (b * nq + i, COL_QC // GROUP_W)),
                  pl.BlockSpec((seq, GROUP_W), lambda b, i: (b, COL_KC // GROUP_W)),
                  pl.BlockSpec((seq, GROUP_W), lambda b, i: (b, COL_VC // GROUP_W))],
        out_specs=pl.BlockSpec((1, GROUP_W, TQ), lambda b, i: (b * nq + i, 0, 0)),
        out_shape=jax.ShapeDtypeStruct((t // TQ, GROUP_W, TQ), F32),
        scratch_shapes=[pltpu.VMEM((seq, N_HEADS * LANES), BF16),
                        pltpu.VMEM((seq // TK, GROUP_W, TK), BF16),
                        pltpu.VMEM((N_HEADS // 2, TQ, LANES), BF16),
                        pltpu.VMEM((N_HEADS, 2, TK, TQ), BF16),
                        pltpu.VMEM((N_HEADS, TK, TQ), F32),
                        pltpu.VMEM((N_HEADS, HEAD_DIM, TQ), F32)],
        compiler_params=pltpu.CompilerParams(dimension_semantics=("parallel", "arbitrary"),
                                             vmem_limit_bytes=VMEM_LIMIT),
        name="stick_breaking",
    )(p2, p2, p2)


def _mixout_kernel(x_ref, ocmp_ref, oslc_ref, owin_ref, gate_ref, cvb_ref, cvc_ref, cvu_ref, pc_ref, pu_ref,
                   oc_ref, od_ref, bg_ref, cw_ref, gout_ref, wout_ref, o_ref, *, tiles_per_seq):
    i = pl.program_id(0)
    d = x_ref.shape[1]

    gates = jax.nn.sigmoid(gate_ref[...] + bg_ref[...])
    src = lax.broadcasted_iota(jnp.int32, (LANES, GROUP_W), 0)
    head3 = (lax.broadcasted_iota(jnp.int32, (LANES, GROUP_W), 1) >> 6) * 3
    oa = (_exact_dot(gates, (src == head3).astype(BF16)) * ocmp_ref[0].T
          + _exact_dot(gates, (src == head3 + 1).astype(BF16)) * oslc_ref[0].T
          + _exact_dot(gates, (src == head3 + 2).astype(BF16)) * owin_ref[0].T)
    cu = cvc_ref[...] * cvu_ref[...]
    prev = jnp.where(i % tiles_per_seq == 0, 0.0, pc_ref[...] * pu_ref[...])
    full = jnp.concatenate([prev, cu], axis=0)
    back1 = pltpu.roll(full, 1, 0)[8:, :]
    back2 = pltpu.roll(full, 2, 0)[8:, :]
    cw = cw_ref[...]
    ob = cvb_ref[...] * (cw[0:1, :] * back2 + cw[1:2, :] * back1 + cw[2:3, :] * cu)
    groups = jnp.concatenate([oa, ob, oc_ref[0].T, od_ref[0].T], axis=1)
    n_groups = d // HEAD_DIM
    gather = ((lax.broadcasted_iota(jnp.int32, (d, LANES), 0) >> 6)
              == lax.broadcasted_iota(jnp.int32, (d, LANES), 1)).astype(BF16)
    spread = (lax.broadcasted_iota(jnp.int32, (LANES, d), 0)
              == (lax.broadcasted_iota(jnp.int32, (LANES, d), 1) >> 6)).astype(BF16)
    ssq = _exact_dot(groups * groups, gather)
    inv = lax.rsqrt(ssq * (1.0 / HEAD_DIM) + RMS_EPS)
    mixed = (groups * _exact_dot(inv, spread) * gout_ref[...]).astype(BF16)
    assert n_groups <= LANES
    o_ref[...] = x_ref[...] + jnp.dot(mixed, wout_ref[...], preferred_element_type=F32)


def _mixout(x2, p2, ocmp, oslc, owin, oc, od, bg, cw, gout, wout, seq):
    t, d = x2.shape
    tm = TQ
    rows = lambda w, col: pl.BlockSpec((tm, w), lambda i: (i, col // w))
    heads_t = pl.BlockSpec((1, GROUP_W, tm), lambda i: (i, 0, 0))
    prev8 = lambda col: pl.BlockSpec((8, GROUP_W), lambda i: (jnp.maximum(i * (tm // 8) - 1, 0), col // GROUP_W))
    const = lambda shape: pl.BlockSpec(shape, lambda i: (0, 0))
    return pl.pallas_call(
        functools.partial(_mixout_kernel, tiles_per_seq=seq // tm),
        grid=(t // tm,),
        in_specs=[rows(d, 0), heads_t, heads_t, heads_t,
                  rows(LANES, COL_GATE), rows(GROUP_W, COL_CVB), rows(GROUP_W, COL_CVC), rows(GROUP_W, COL_CVU),
                  prev8(COL_CVC), prev8(COL_CVU),
                  heads_t, heads_t,
                  const((1, LANES)), const((8, GROUP_W)), const((1, d)), const((d, d))],
        out_specs=rows(d, 0),
        out_shape=jax.ShapeDtypeStruct((t, d), F32),
        compiler_params=pltpu.CompilerParams(dimension_semantics=("parallel",), vmem_limit_bytes=VMEM_LIMIT),
        name="mixout",
    )(x2, ocmp, oslc, owin, p2, p2, p2, p2, p2, p2, oc, od, bg, cw, gout, wout)


def _ffn_kernel(x_ref, g_ref, wg_ref, wu_ref, wd_ref, o_ref, act_s, *, ff_chunk, out_chunk):
    h = _rms(x_ref[...], g_ref[...]).astype(BF16)
    dff = wg_ref.shape[1]
    for c in range(dff // ff_chunk):
        cols = slice(c * ff_chunk, (c + 1) * ff_chunk)
        a = jnp.dot(h, wg_ref[:, cols], preferred_element_type=F32)
        u = jnp.dot(h, wu_ref[:, cols], preferred_element_type=F32)
        act_s[:, cols] = (a * jax.nn.sigmoid(a) * u).astype(BF16)
    d = o_ref.shape[1]
    for c in range(d // out_chunk):
        cols = slice(c * out_chunk, (c + 1) * out_chunk)
        o_ref[:, cols] = x_ref[:, cols] + jnp.dot(act_s[...], wd_ref[:, cols], preferred_element_type=F32)


def _ffn(x2, g, wg, wu, wd):
    t, d = x2.shape
    dff = wg.shape[1]
    tm = 512
    resident = lambda shape: pl.BlockSpec(shape, lambda i: (0, 0), pipeline_mode=pl.Buffered(1))
    return pl.pallas_call(
        functools.partial(_ffn_kernel, ff_chunk=256, out_chunk=256),
        grid=(t // tm,),
        in_specs=[pl.BlockSpec((tm, d), lambda i: (i, 0)),
                  pl.BlockSpec((1, d), lambda i: (0, 0)),
                  resident((d, dff)), resident((d, dff)), resident((dff, d))],
        out_specs=pl.BlockSpec((tm, d), lambda i: (i, 0)),
        out_shape=jax.ShapeDtypeStruct((t, d), F32),
        scratch_shapes=[pltpu.VMEM((tm, dff), BF16)],
        compiler_params=pltpu.CompilerParams(dimension_semantics=("parallel",), vmem_limit_bytes=VMEM_LIMIT),
        name="ffn",
    )(x2, g, wg, wu, wd)


def _permute_w_in_kernel(w_ref, o_ref):
    rows = w_ref.shape[0]
    n_rest = P_COLS - COL_CVB
    o_ref[:, 0:COL_GATE] = w_ref[:, 0:COL_GATE].astype(BF16)
    lane = lax.broadcasted_iota(jnp.int32, (rows, LANES), 1)
    o_ref[:, COL_GATE:COL_GATE + LANES] = jnp.where(lane < N_GATES, w_ref[:, COL_GATE:COL_GATE + LANES], 0.0).astype(BF16)
    o_ref[:, COL_CVB:P_COLS] = w_ref[:, COL_GATE + N_GATES:COL_GATE + N_GATES + n_rest].astype(BF16)


def _permute_w_in(w_in):
    d, cols = w_in.shape
    tr = 256
    return pl.pallas_call(
        _permute_w_in_kernel,
        grid=(d // tr,),
        in_specs=[pl.BlockSpec((tr, cols), lambda i: (i, 0))],
        out_specs=pl.BlockSpec((tr, P_COLS), lambda i: (i, 0)),
        out_shape=jax.ShapeDtypeStruct((d, P_COLS), BF16),
        compiler_params=pltpu.CompilerParams(dimension_semantics=("parallel",), vmem_limit_bytes=VMEM_LIMIT),
        name="permute_w_in",
    )(w_in)


def _layer(x2, batch, seq, g_mix, w_in, b_gate, g_q_nsa, g_k_cmp, g_k_slc, g_k_win, pe_k_cmp, pe_v_cmp,
           w1_k_cmp, w2_k_cmp, w1_v_cmp, w2_v_cmp, conv_w, g_q_dil, g_k_dil, g_out, w_out,
           g_ffn, w_gate, w_up, w_down):
    row = lambda v: v.reshape(1, -1)
    per_group = lambda g, n: jnp.tile(g, n).reshape(1, -1)
    p2 = _inproj(x2, row(g_mix), _permute_w_in(w_in))

    pe, w1, w2 = _compress_weights(pe_k_cmp, pe_v_cmp, w1_k_cmp, w2_k_cmp, w1_v_cmp, w2_v_cmp)
    kc, vc = _compress(p2, pe, w1, w2, per_group(g_k_cmp, 2), batch, seq)
    ocmp, oslc = _nsa(p2, kc, vc, per_group(g_q_nsa, N_HEADS), per_group(g_k_slc, 2), batch, seq)
    owin = _banded(p2, per_group(g_q_nsa, N_HEADS), per_group(g_k_win, 2), jnp.asarray(_band_bias_tiles(NSA_WINDOW - 1, _window_mult)),
                   batch, seq, window=NSA_WINDOW - 1, slopes=SLOPES_A, shared_kv=True,
                   col_q=COL_QA, col_k=COL_KWV, col_v=COL_KWV, name="nsa_window")
    od = _banded(p2, per_group(g_q_dil, N_HEADS), per_group(g_k_dil, N_HEADS),
                 jnp.asarray(_band_bias_tiles(DILATED_CONFIGS[-1][0], _dilated_mult)),
                 batch, seq, window=DILATED_CONFIGS[-1][0], slopes=SLOPES_D, shared_kv=False,
                 col_q=COL_QD, col_k=COL_KD, col_v=COL_VD, name="dilated")
    oc = _stick(p2, batch, seq)

    bg = jnp.zeros((1, LANES), F32).at[0, :N_GATES].set(b_gate)
    cwp = jnp.zeros((8, GROUP_W), F32).at[:CONV_K].set(conv_w)
    x1 = _mixout(x2, p2, ocmp, oslc, owin, oc, od, bg, cwp, row(g_out), w_out.astype(BF16), seq)
    return _ffn(x1, row(g_ffn), w_gate.astype(BF16), w_up.astype(BF16), w_down.astype(BF16))


def kernel(x, g_mix, w_in, b_gate, g_q_nsa, g_k_cmp, g_k_slc, g_k_win, pe_k_cmp, pe_v_cmp, w1_k_cmp, w2_k_cmp,
           w1_v_cmp, w2_v_cmp, conv_w, g_q_dil, g_k_dil, g_out, w_out, g_ffn, w_gate, w_up, w_down):
    batch, seq, d = x.shape
    assert seq % (CMP_STRIDE * LANES) == 0 and d % LANES == 0
    x2 = x.reshape(batch * seq, d)
    params = (g_mix, w_in, b_gate, g_q_nsa, g_k_cmp, g_k_slc, g_k_win, pe_k_cmp, pe_v_cmp, w1_k_cmp, w2_k_cmp,
              w1_v_cmp, w2_v_cmp, conv_w, g_q_dil, g_k_dil, g_out, w_out, g_ffn, w_gate, w_up, w_down)
    for layer in range(g_mix.shape[0]):
        x2 = _layer(x2, batch, seq, *[p[layer] for p in params])
    return x2.reshape(batch, seq, d)
```

```python
import functools

import numpy as np
import jax
import jax.numpy as jnp
from jax import lax
from jax.experimental import pallas as pl
from jax.experimental.pallas import tpu as pltpu

F32 = jnp.float32
BF16 = jnp.bfloat16

HEAD_DIM = 64
N_HEADS = 4
GROUP_W = N_HEADS * HEAD_DIM
CONV_K = 3
CMP_LEN = 32
CMP_STRIDE = 16
SEL_BLOCK = 64
N_SELECT = 16
NSA_WINDOW = 512
DILATED_CONFIGS = ((128, 1), (512, 4), (2048, 16))
NEG_INF = -1e30
FORCE_SCORE = 1e6
RMS_EPS = 1e-6

TQ = 512
TK = 256
N_TOP = TQ // TK
LANES = 128
MXU_AHEAD = 2
FLASH_AHEAD = 2
SUM_PAD = 16
LOG2E = 1.4426950408889634
VMEM_LIMIT = 52 * 1024 * 1024

COL_QA, COL_KVC, COL_KSV, COL_KWV, COL_GATE = 0, 256, 384, 512, 640
COL_CVB, COL_CVC, COL_CVU = 768, 1024, 1280
COL_QC, COL_KC, COL_VC = 1536, 1792, 2048
COL_QD, COL_KD, COL_VD = 2304, 2560, 2816
P_COLS = 3072
N_GATES = 12

_NT = (((1,), (1,)), ((), ()))


def _alibi_slopes():
    s = [2.0 ** (-8.0 * i / 8) for i in range(1, 9)]
    return tuple(s[0::2]), tuple(s[1::2])


SLOPES_A, SLOPES_D = _alibi_slopes()


def _rms(x, g):
    return x * lax.rsqrt(jnp.mean(x * x, axis=-1, keepdims=True) + RMS_EPS) * g


def _exact_dot(a, sel):
    hi = a.astype(BF16)
    lo = (a - hi.astype(F32)).astype(BF16)
    return jnp.dot(hi, sel, preferred_element_type=F32) + jnp.dot(lo, sel, preferred_element_type=F32)


def _group_rms(x, g):
    w = x.shape[1]
    same = ((lax.broadcasted_iota(jnp.int32, (w, w), 0) >> 6)
            == (lax.broadcasted_iota(jnp.int32, (w, w), 1) >> 6)).astype(BF16)
    ssq = _exact_dot(x * x, same)
    return x * lax.rsqrt(ssq * (1.0 / HEAD_DIM) + RMS_EPS) * g


EVEN_AUG, ODD_AUG = HEAD_DIM, 0


def _key_aug(pos, first):
    lane = lax.broadcasted_iota(jnp.int32, (pos.shape[0], LANES), 1) - first
    hi = (pos >> 6).astype(F32)
    lo = (pos & 63).astype(F32)
    return jnp.where(lane == 0, hi, jnp.where(lane == 1, lo, jnp.where((lane == 2) | (lane == 3), 1.0, 0.0)))


def _query_aug(t, slope, first):
    lane = lax.broadcasted_iota(jnp.int32, (t.shape[0], LANES), 1) - first
    hi = (t >> 6).astype(F32) * (-64.0 * slope)
    lo = (t & 63).astype(F32) * (-slope)
    return jnp.where(lane == 0, 64.0 * slope,
                     jnp.where(lane == 1, slope, jnp.where(lane == 2, hi, jnp.where(lane == 3, lo, 0.0))))


def _pair_keys(pair, pos):
    lane = lax.broadcasted_iota(jnp.int32, pair.shape, 1)
    aug_e = 0.0 if pos is None else _key_aug(pos, EVEN_AUG)
    aug_o = 0.0 if pos is None else _key_aug(pos, ODD_AUG)
    return (jnp.where(lane < HEAD_DIM, pair, aug_e).astype(BF16),
            jnp.where(lane >= HEAD_DIM, pair, aug_o).astype(BF16))


def _prep_queries(q, g, t0, slopes, qp_s):
    qn = (q if g is None else _group_rms(q, g)) * (HEAD_DIM ** -0.5)
    tpos = t0 + lax.broadcasted_iota(jnp.int32, (q.shape[0], 1), 0)
    lane = lax.broadcasted_iota(jnp.int32, (q.shape[0], LANES), 1)
    for i in range(N_HEADS // 2):
        pair = qn[:, i * LANES:(i + 1) * LANES]
        if slopes is None:
            qp_s[i] = pair.astype(BF16)
            continue
        even = jnp.where((lane >= EVEN_AUG) & (lane < EVEN_AUG + 4), _query_aug(tpos, slopes[2 * i], EVEN_AUG), pair)
        odd = jnp.where(lane < ODD_AUG + 4, _query_aug(tpos, slopes[2 * i + 1], ODD_AUG), pair)
        qp_s[2 * i] = even.astype(BF16)
        qp_s[2 * i + 1] = odd.astype(BF16)


def _inproj_kernel(x_ref, g_ref, w_ref, o_ref, *, nchunk):
    h = _rms(x_ref[...], g_ref[...]).astype(BF16)
    cw = P_COLS // nchunk
    for c in range(nchunk):
        o_ref[:, c * cw:(c + 1) * cw] = jnp.dot(h, w_ref[:, c * cw:(c + 1) * cw], preferred_element_type=F32)


def _inproj(x2, g, w):
    t, d = x2.shape
    tm = 512
    return pl.pallas_call(
        functools.partial(_inproj_kernel, nchunk=6),
        grid=(t // tm,),
        in_specs=[pl.BlockSpec((tm, d), lambda i: (i, 0)),
                  pl.BlockSpec((1, d), lambda i: (0, 0)),
                  pl.BlockSpec((d, P_COLS), lambda i: (0, 0))],
        out_specs=pl.BlockSpec((tm, P_COLS), lambda i: (i, 0)),
        out_shape=jax.ShapeDtypeStruct((t, P_COLS), F32),
        compiler_params=pltpu.CompilerParams(dimension_semantics=("parallel",), vmem_limit_bytes=VMEM_LIMIT),
        name="inproj",
    )(x2, g, w)


def _gelu_tanh(x):
    return x * (0.5 * (1.0 + jnp.tanh(np.sqrt(2.0 / np.pi).astype(np.float32) * (x + 0.044715 * (x * x * x)))))


def _compress_kernel(kv_ref, pe_ref, w1_ref, w2_ref, gk_ref, kc_ref, vct_ref):
    nc = kv_ref.shape[0] // CMP_STRIDE
    first = None
    second = None
    for j in range(CMP_STRIDE):
        tok = kv_ref[pl.ds(j, nc, stride=CMP_STRIDE), :]
        a = jnp.dot((tok + pe_ref[j:j + 1, :]).astype(BF16), w1_ref[j], preferred_element_type=F32)
        b = jnp.dot((tok + pe_ref[CMP_STRIDE + j:CMP_STRIDE + j + 1, :]).astype(BF16), w1_ref[CMP_STRIDE + j],
                    preferred_element_type=F32)
        first = a if first is None else first + a
        second = b if second is None else second + b
    hid = first + pltpu.roll(second, nc - 1, 0)
    out = jnp.dot(_gelu_tanh(hid).astype(BF16), w2_ref[...], preferred_element_type=F32)
    kn = _group_rms(out, gk_ref[...])
    end = lax.broadcasted_iota(jnp.int32, (nc, 1), 0) * CMP_STRIDE + (CMP_LEN - 1)
    even, _ = _pair_keys(kn, end)
    _, odd = _pair_keys(pltpu.roll(kn, HEAD_DIM, 1), end)
    kc_ref[0] = jnp.concatenate([even, odd], axis=1)
    vct_ref[0] = out.T[HEAD_DIM:, :].astype(BF16)


def _compress(p2, pe, w1, w2, gk, batch, seq):
    nc = seq // CMP_STRIDE
    hid2 = w1.shape[2]
    return pl.pallas_call(
        _compress_kernel,
        grid=(batch,),
        in_specs=[pl.BlockSpec((seq, LANES), lambda i: (i, COL_KVC // LANES)),
                  pl.BlockSpec((CMP_LEN, LANES), lambda i: (0, 0)),
                  pl.BlockSpec((CMP_LEN, LANES, hid2), lambda i: (0, 0, 0)),
                  pl.BlockSpec((hid2, LANES), lambda i: (0, 0)),
                  pl.BlockSpec((1, LANES), lambda i: (0, 0))],
        out_specs=[pl.BlockSpec((1, nc, 2 * LANES), lambda i: (i, 0, 0)),
                   pl.BlockSpec((1, HEAD_DIM, nc), lambda i: (i, 0, 0))],
        out_shape=[jax.ShapeDtypeStruct((batch, nc, 2 * LANES), BF16),
                   jax.ShapeDtypeStruct((batch, HEAD_DIM, nc), BF16)],
        compiler_params=pltpu.CompilerParams(dimension_semantics=("parallel",), vmem_limit_bytes=VMEM_LIMIT),
        name="nsa_compress",
    )(p2, pe, w1, w2, gk)


def _compress_weights(pe_k, pe_v, w1_k, w2_k, w1_v, w2_v):
    hid = w1_k.shape[1]
    w1k = w1_k.reshape(CMP_LEN, HEAD_DIM, hid)
    w1v = w1_v.reshape(CMP_LEN, HEAD_DIM, hid)
    z1 = jnp.zeros_like(w1k)
    w1 = jnp.concatenate([jnp.concatenate([w1k, z1], axis=2), jnp.concatenate([z1, w1v], axis=2)], axis=1)
    z2 = jnp.zeros_like(w2_k)
    w2 = jnp.concatenate([jnp.concatenate([w2_k, z2], axis=1), jnp.concatenate([z2, w2_v], axis=1)], axis=0)
    pe = jnp.concatenate([pe_k, pe_v], axis=1)
    return pe, w1.astype(BF16), w2.astype(BF16)


def _flash_scores(kp_of, bias, qp_s, s_s, m_cur):
    m_next = []
    for h in range(N_HEADS):
        s = lax.dot_general(kp_of(h), qp_s[h], _NT, preferred_element_type=F32) + bias
        s_s[h] = s
        m_next.append(jnp.maximum(m_cur[h], jnp.max(s, axis=0, keepdims=True)))
    return tuple(m_next)


def _flash_accum(vt_of, s_s, acc_s, m_prev, m_cur, ls):
    new_l = []
    for h in range(N_HEADS):
        alpha = jnp.exp(m_prev[h] - m_cur[h])
        p = jnp.exp(s_s[h] - m_cur[h])
        new_l.append(alpha * ls[h] + jnp.sum(p, axis=0, keepdims=True))
        acc_s[h] = alpha * acc_s[h] + jnp.dot(vt_of(h), p.astype(BF16), preferred_element_type=F32)
    return tuple(new_l)


def _flash_accum_and_scores(kp_of, bias, vt_of, qp_s, s_s, acc_s, m_prev, m_cur, ls):
    def scores(h):
        return lax.dot_general(kp_of(h), qp_s[h], _NT, preferred_element_type=F32) + bias

    m_next, new_l = [], []
    s_new = {h: scores(h) for h in range(FLASH_AHEAD)}
    for h in range(N_HEADS):
        alpha = jnp.exp(m_prev[h] - m_cur[h])
        p = jnp.exp(s_s[h] - m_cur[h])
        new_l.append(alpha * ls[h] + jnp.sum(p, axis=0, keepdims=True))
        acc_s[h] = alpha * acc_s[h] + jnp.dot(vt_of(h), p.astype(BF16), preferred_element_type=F32)
        s_h = s_new.pop(h)
        s_s[h] = s_h
        m_next.append(jnp.maximum(m_cur[h], jnp.max(s_h, axis=0, keepdims=True)))
        if h + FLASH_AHEAD < N_HEADS:
            s_new[h + FLASH_AHEAD] = scores(h + FLASH_AHEAD)
    return tuple(m_next), tuple(new_l)


def _flash_pipelined(tile_of, n_tiles, kp_fn, bias_fn, vt_fn, qp_s, s_s, acc_s, o_ref):
    acc_s[...] = jnp.zeros(acc_s.shape, F32)
    m0 = tuple(jnp.full((1, TQ), NEG_INF, F32) for _ in range(N_HEADS))
    l0 = tuple(jnp.zeros((1, TQ), F32) for _ in range(N_HEADS))
    first_kt = tile_of(0)
    m1 = _flash_scores(kp_fn(first_kt), bias_fn(first_kt, True), qp_s, s_s, m0)

    def step(i, carry, top):
        m_prev, m_cur, ls = carry
        nxt = tile_of(i + 1)
        m_next, ls = _flash_accum_and_scores(kp_fn(nxt), bias_fn(nxt, top), vt_fn(tile_of(i)), qp_s, s_s, acc_s,
                                             m_prev, m_cur, ls)
        return m_cur, m_next, ls

    carry = (m0, m1, l0)
    for i in range(N_TOP - 1):
        carry = step(i, carry, True)
    m_prev, m_cur, ls = lax.fori_loop(N_TOP - 1, n_tiles - 1, lambda i, c: step(i, c, False), carry)
    ls = _flash_accum(vt_fn(tile_of(n_tiles - 1)), s_s, acc_s, m_prev, m_cur, ls)
    o_ref[0] = jnp.concatenate([acc_s[h] * (1.0 / ls[h]) for h in range(N_HEADS)], axis=0)


def _nsa_kernel(q_ref, kc_ref, vct_ref, ksv_ref, gq_ref, gks_ref, ocmp_ref, oslc_ref,
                ksp_s, vst_s, qp_s, imp_s, sel_s, s_s, acc_s, hit_s, *, seq):
    qi = pl.program_id(1)
    nc = seq // CMP_STRIDE
    nsel = seq // SEL_BLOCK
    nkt = seq // TK

    @pl.when(qi == 0)
    def _prep():
        for c in range(nkt):
            rows = slice(c * TK, (c + 1) * TK)
            blk = ksv_ref[rows, :]
            pos = c * TK + lax.broadcasted_iota(jnp.int32, (TK, 1), 0)
            kn = _group_rms(blk, gks_ref[...])
            ksp_s[rows, 0:LANES], _ = _pair_keys(kn, pos)
            _, ksp_s[rows, LANES:2 * LANES] = _pair_keys(pltpu.roll(kn, HEAD_DIM, 1), pos)
            vst_s[c] = blk.T[HEAD_DIM:, :].astype(BF16)
        imp_s[:, 0:8, :] = jnp.zeros((TQ // LANES, 8, LANES), F32)

    t0 = qi * TQ
    _prep_queries(q_ref[...], gq_ref[...], t0, SLOPES_A, qp_s)

    kc = [kc_ref[0, :, 0:LANES], kc_ref[0, :, LANES:2 * LANES]]
    vct = vct_ref[0]
    tq_row = t0 + lax.broadcasted_iota(jnp.int32, (nc, TQ), 1)
    n_idx = lax.broadcasted_iota(jnp.int32, (nc, TQ), 0)
    vis = (tq_row >= n_idx * CMP_STRIDE + (CMP_LEN - 1)) & (n_idx < nc - 1)
    vis_bias = jnp.where(vis, 0.0, NEG_INF)
    sees_any = (t0 + lax.broadcasted_iota(jnp.int32, (1, TQ), 1) >= CMP_LEN - 1).astype(F32)
    imp = jnp.zeros((nc, TQ), F32)
    ocmp_t = []
    scores = [lax.dot_general(kc[h % 2], qp_s[h], _NT, preferred_element_type=F32) + vis_bias
              for h in range(N_HEADS)]
    for h in range(N_HEADS):
        sc = scores[h]
        e = jnp.exp(sc - jnp.max(sc, axis=0, keepdims=True))
        p = e * (sees_any / jnp.sum(e, axis=0, keepdims=True))
        ocmp_t.append(jnp.dot(vct, p.astype(BF16), preferred_element_type=F32))
        imp = imp + p
    ocmp_ref[0] = jnp.concatenate(ocmp_t, axis=0)

    halves = []
    for half in range(TQ // LANES):
        imp_s[half, 8:8 + nc, :] = imp[:, half * LANES:(half + 1) * LANES]
        r = [imp_s[half, pl.ds(8 + k, nsel, stride=4), :] for k in range(4)]
        rm1 = imp_s[half, pl.ds(7, nsel, stride=4), :]
        halves.append(rm1 + 2.0 * (r[0] + r[1] + r[2]) + r[3])
    imp_blk = jnp.concatenate(halves, axis=1)
    blk = lax.broadcasted_iota(jnp.int32, (nsel, TQ), 0)
    tl = t0 + lax.broadcasted_iota(jnp.int32, (nsel, TQ), 1)
    cur = tl >> 6
    forced = (blk == 0) | (blk == cur) | (blk == cur - 1)
    valid = blk * SEL_BLOCK <= tl
    score = jnp.where(forced, FORCE_SCORE, jnp.where(valid, imp_blk, -FORCE_SCORE))
    sub = 8
    per_tile = TK // SEL_BLOCK
    tile_hit = []
    for g in range(nsel // sub):
        mine = score[g * sub:(g + 1) * sub, :]
        blk_g = g * sub + lax.broadcasted_iota(jnp.int32, (sub, TQ), 0)
        rank = jnp.zeros((sub, TQ), F32)
        for j in range(nsel):
            row = score[j:j + 1, :]
            if j < g * sub:
                beats = row >= mine
            elif j >= (g + 1) * sub:
                beats = row > mine
            else:
                beats = (row > mine) | ((row == mine) & (blk_g > j))
            rank = rank + jnp.where(beats, 1.0, 0.0)
        sel_g = jnp.where(rank < min(N_SELECT, nsel), 0.0, NEG_INF)
        sel_s[g * sub:(g + 1) * sub, :] = sel_g
        for part in range(sub // per_tile):
            tile_hit.append(jnp.max(sel_g[part * per_tile:(part + 1) * per_tile, :]))

    n_hit = jnp.int32(0)
    for kt in reversed(range(nkt)):
        hit_s[n_hit] = kt
        n_hit = n_hit + ((tile_hit[kt] > -1.0) & (kt * TK < t0)).astype(jnp.int32)

    def block_bias(kt):
        return jnp.concatenate([jnp.broadcast_to(sel_s[pl.ds(kt * per_tile + j, 1), :], (SEL_BLOCK, TQ))
                                for j in range(per_tile)], axis=0)

    k_local = lax.broadcasted_iota(jnp.int32, (TK, TQ), 0)
    q_local = lax.broadcasted_iota(jnp.int32, (TK, TQ), 1)
    top_kt = qi * N_TOP + (N_TOP - 1)

    def kp_fn(kt):
        off = pl.multiple_of(kt * TK, TK)
        return lambda h: ksp_s[pl.ds(off, TK), (h % 2) * LANES:(h % 2 + 1) * LANES]

    def bias_fn(kt, top):
        if top:
            return jnp.where(kt * TK - t0 + k_local <= q_local, block_bias(kt), NEG_INF)
        return block_bias(kt)

    tile_of = lambda i: jnp.where(i < N_TOP, top_kt - i, hit_s[jnp.maximum(i - N_TOP, 0)])
    _flash_pipelined(tile_of, n_hit + N_TOP, kp_fn, bias_fn, lambda kt: (lambda h: vst_s[kt]), qp_s, s_s, acc_s,
                     oslc_ref)


def _nsa(p2, kc, vc, gq, gks, batch, seq):
    nq = seq // TQ
    nc = seq // CMP_STRIDE
    t = batch * seq
    return pl.pallas_call(
        functools.partial(_nsa_kernel, seq=seq),
        grid=(batch, nq),
        in_specs=[pl.BlockSpec((TQ, GROUP_W), lambda b, i: (b * nq + i, COL_QA // GROUP_W)),
                  pl.BlockSpec((1, nc, 2 * LANES), lambda b, i: (b, 0, 0)),
                  pl.BlockSpec((1, HEAD_DIM, nc), lambda b, i: (b, 0, 0)),
                  pl.BlockSpec((seq, LANES), lambda b, i: (b, COL_KSV // LANES)),
                  pl.BlockSpec((1, GROUP_W), lambda b, i: (0, 0)),
                  pl.BlockSpec((1, LANES), lambda b, i: (0, 0))],
        out_specs=[pl.BlockSpec((1, GROUP_W, TQ), lambda b, i: (b * nq + i, 0, 0)),
                   pl.BlockSpec((1, GROUP_W, TQ), lambda b, i: (b * nq + i, 0, 0))],
        out_shape=[jax.ShapeDtypeStruct((t // TQ, GROUP_W, TQ), F32),
                   jax.ShapeDtypeStruct((t // TQ, GROUP_W, TQ), F32)],
        scratch_shapes=[pltpu.VMEM((seq, 2 * LANES), BF16),
                        pltpu.VMEM((seq // TK, HEAD_DIM, TK), BF16),
                        pltpu.VMEM((N_HEADS, TQ, LANES), BF16),
                        pltpu.VMEM((TQ // LANES, 8 + nc, LANES), F32),
                        pltpu.VMEM((seq // SEL_BLOCK, TQ), F32),
                        pltpu.VMEM((N_HEADS, TK, TQ), F32),
                        pltpu.VMEM((N_HEADS, HEAD_DIM, TQ), F32),
                        pltpu.SMEM((seq // TK + 1,), jnp.int32)],
        compiler_params=pltpu.CompilerParams(dimension_semantics=("parallel", "arbitrary"),
                                             vmem_limit_bytes=VMEM_LIMIT),
        name="nsa_cmp_slc",
    )(p2, kc, vc, p2, gq, gks)


def _band_bias_tiles(window, mult_fn):
    nd = (window + TQ - 1) // TK + 1
    kk = np.arange(TK)[:, None]
    qq = np.arange(TQ)[None, :]
    tiles = np.empty((nd, TK, TQ), np.float32)
    for di in range(nd):
        d = (di - (N_TOP - 1)) * TK + qq - kk
        mult = mult_fn(d)
        tiles[di] = np.where(mult > 0, np.log(np.maximum(mult, 1)), NEG_INF)
    return tiles


def _window_mult(d):
    return ((d >= 0) & (d <= NSA_WINDOW - 1)).astype(np.float64)


def _dilated_mult(d):
    m = np.zeros(d.shape, np.float64)
    for window, dil in DILATED_CONFIGS:
        m += ((d >= 0) & (d <= window) & (d % dil == 0)).astype(np.float64)
    return m


def _banded_kernel(*refs, seq, window, slopes, shared_kv):
    if shared_kv:
        q_ref, kv_ref, gq_ref, gk_ref, bias_ref, o_ref, kp_s, vt_s, qp_s, s_s, acc_s = refs
    else:
        q_ref, k_ref, v_ref, gq_ref, gk_ref, bias_ref, o_ref, kp_s, vt_s, qp_s, s_s, acc_s = refs
    qi = pl.program_id(1)
    nkt = seq // TK

    @pl.when(qi == 0)
    def _prep():
        for c in range(nkt):
            rows = slice(c * TK, (c + 1) * TK)
            pos = c * TK + lax.broadcasted_iota(jnp.int32, (TK, 1), 0)
            if shared_kv:
                blk = kv_ref[rows, :]
                kn = _group_rms(blk, gk_ref[...])
                kp_s[rows, 0:LANES], _ = _pair_keys(kn, pos)
                _, kp_s[rows, LANES:2 * LANES] = _pair_keys(pltpu.roll(kn, HEAD_DIM, 1), pos)
                vt_s[c] = blk.T[HEAD_DIM:, :].astype(BF16)
            else:
                kn = _group_rms(k_ref[rows, :], gk_ref[...])
                for i in range(N_HEADS // 2):
                    even, odd = _pair_keys(kn[:, i * LANES:(i + 1) * LANES], pos)
                    kp_s[rows, 2 * i * LANES:(2 * i + 1) * LANES] = even
                    kp_s[rows, (2 * i + 1) * LANES:(2 * i + 2) * LANES] = odd
                vt_s[c] = v_ref[rows, :].T.astype(BF16)

    t0 = qi * TQ
    _prep_queries(q_ref[...], gq_ref[...], t0, slopes, qp_s)
    kt_lo = jnp.maximum(t0 - window, 0) // TK

    def kp_fn(kt):
        off = pl.multiple_of(kt * TK, TK)
        if shared_kv:
            return lambda h: kp_s[pl.ds(off, TK), (h % 2) * LANES:(h % 2 + 1) * LANES]
        return lambda h: kp_s[pl.ds(off, TK), h * LANES:(h + 1) * LANES]

    def vt_fn(kt):
        if shared_kv:
            return lambda h: vt_s[kt]
        return lambda h: vt_s[kt, h * HEAD_DIM:(h + 1) * HEAD_DIM, :]

    top_kt = qi * N_TOP + (N_TOP - 1)
    _flash_pipelined(lambda i: top_kt - i, top_kt + 1 - kt_lo, kp_fn, lambda kt, top: bias_ref[top_kt - kt], vt_fn,
                     qp_s, s_s, acc_s, o_ref)


def _banded(p2, gq, gk, bias_tiles, batch, seq, *, window, slopes, shared_kv, col_q, col_k, col_v, name):
    nq = seq // TQ
    t = batch * seq
    nd = bias_tiles.shape[0]
    n_kv = 1 if shared_kv else N_HEADS
    q_spec = pl.BlockSpec((TQ, GROUP_W), lambda b, i: (b * nq + i, col_q // GROUP_W))
    gain = lambda g: pl.BlockSpec(g.shape, lambda b, i: (0, 0))
    bias_spec = pl.BlockSpec((nd, TK, TQ), lambda b, i: (0, 0, 0))
    if shared_kv:
        kv_specs = [pl.BlockSpec((seq, LANES), lambda b, i: (b, col_k // LANES))]
        operands = (p2, p2, gq, gk, bias_tiles)
    else:
        kv_specs = [pl.BlockSpec((seq, GROUP_W), lambda b, i: (b, col_k // GROUP_W)),
                    pl.BlockSpec((seq, GROUP_W), lambda b, i: (b, col_v // GROUP_W))]
        operands = (p2, p2, p2, gq, gk, bias_tiles)
    return pl.pallas_call(
        functools.partial(_banded_kernel, seq=seq, window=window, slopes=slopes, shared_kv=shared_kv),
        grid=(batch, nq),
        in_specs=[q_spec] + kv_specs + [gain(gq), gain(gk), bias_spec],
        out_specs=pl.BlockSpec((1, GROUP_W, TQ), lambda b, i: (b * nq + i, 0, 0)),
        out_shape=jax.ShapeDtypeStruct((t // TQ, GROUP_W, TQ), F32),
        scratch_shapes=[pltpu.VMEM((seq, (2 if shared_kv else N_HEADS) * LANES), BF16),
                        pltpu.VMEM((seq // TK, n_kv * HEAD_DIM, TK), BF16),
                        pltpu.VMEM((N_HEADS, TQ, LANES), BF16),
                        pltpu.VMEM((N_HEADS, TK, TQ), F32),
                        pltpu.VMEM((N_HEADS, HEAD_DIM, TQ), F32)],
        compiler_params=pltpu.CompilerParams(dimension_semantics=("parallel", "arbitrary"),
                                             vmem_limit_bytes=VMEM_LIMIT),
        name=name,
    )(*operands)


def _stick_kernel(q_ref, k_ref, v_ref, o_ref, kp_s, vt_s, qp_s, hl_s, e_s, acc_s, *, seq):
    qi = pl.program_id(1)
    nkt = seq // TK

    @pl.when(qi == 0)
    def _prep():
        for c in range(nkt):
            rows = slice(c * TK, (c + 1) * TK)
            kb = k_ref[rows, :]
            for i in range(N_HEADS // 2):
                even, odd = _pair_keys(kb[:, i * LANES:(i + 1) * LANES], None)
                kp_s[rows, 2 * i * LANES:(2 * i + 1) * LANES] = even
                kp_s[rows, (2 * i + 1) * LANES:(2 * i + 2) * LANES] = odd
            vt_s[c] = v_ref[rows, :].T.astype(BF16)

    _prep_queries(q_ref[...], None, qi * TQ, None, qp_s)
    half = TK // 2
    col = lax.broadcasted_iota(jnp.int32, (half + SUM_PAD, TK), 1) & (half - 1)
    srow = lax.broadcasted_iota(jnp.int32, (half + SUM_PAD, TK), 0)
    sums = ((srow == half) | ((srow < half) & (col > srow))).astype(BF16)
    k_local = lax.broadcasted_iota(jnp.int32, (TK, TQ), 0)
    q_local = lax.broadcasted_iota(jnp.int32, (TK, TQ), 1)
    top_kt = qi * N_TOP + (N_TOP - 1)
    acc_s[...] = jnp.zeros(acc_s.shape, F32)

    def past_mask(tiles_above):
        return tiles_above * TK + k_local < q_local

    def logits(kt, h):
        off = pl.multiple_of(kt * TK, TK)
        return lax.dot_general(kp_s[pl.ds(off, TK), h * LANES:(h + 1) * LANES], qp_s[h // 2], _NT,
                               preferred_element_type=F32)

    def stage(h, z, past):
        sp = jnp.maximum(z, 0.0) + jnp.log(1.0 + jnp.exp2(jnp.abs(z) * (-LOG2E)))
        spm = sp if past is None else jnp.where(past, sp, 0.0)
        hi = spm.astype(BF16)
        lo = (spm - hi.astype(F32)).astype(BF16)
        for b in range(2):
            hl_s[h, b, 0:half, :] = hi[b * half:(b + 1) * half, :]
            hl_s[h, b, half:TK, :] = lo[b * half:(b + 1) * half, :]
        logsig = z - sp
        e_s[h] = logsig if past is None else jnp.where(past, logsig, NEG_INF)

    def tails(h):
        return [jnp.dot(sums, hl_s[h, b], preferred_element_type=F32) for b in range(2)]

    def consume(kt, h, w, later):
        later_lo = later + w[1][half:half + 1, :]
        attn = jnp.concatenate([jnp.exp(e_s[h, 0:half, :] - w[0][0:half, :] - later_lo),
                                jnp.exp(e_s[h, half:TK, :] - w[1][0:half, :] - later)], axis=0)
        acc_s[h] += jnp.dot(vt_s[kt, h * HEAD_DIM:(h + 1) * HEAD_DIM, :], attn.astype(BF16),
                            preferred_element_type=F32)
        return later_lo + w[0][half:half + 1, :]

    ahead = MXU_AHEAD
    for h in range(N_HEADS):
        stage(h, logits(top_kt, h), past_mask(N_TOP - 1))

    def step(kt, carry, past):
        laters, w_first, z_first = carry
        w = {0: list(w_first)}
        z = {0: z_first}
        for h in range(1, ahead):
            w[h] = tails(h)
            z[h] = logits(kt - 1, h)
        out = []
        for h in range(N_HEADS):
            out.append(consume(kt, h, w.pop(h), laters[h]))
            stage(h, z.pop(h), past)
            if h + ahead < N_HEADS:
                w[h + ahead] = tails(h + ahead)
                z[h + ahead] = logits(kt - 1, h + ahead)
            elif h + ahead == N_HEADS:
                nxt = (tuple(tails(0)), logits(jnp.maximum(kt - 2, 0), 0))
        return tuple(out), nxt[0], nxt[1]

    carry = (tuple(jnp.zeros((1, TQ), F32) for _ in range(N_HEADS)),
             tuple(tails(0)), logits(jnp.maximum(top_kt - 1, 0), 0))
    for j in range(N_TOP - 1):
        carry = step(top_kt - j, carry, past_mask(N_TOP - 2 - j))
    below = qi * N_TOP
    laters, w_first, _ = lax.fori_loop(0, below, lambda i, c: step(below - i, c, None), carry)
    w = {0: list(w_first)}
    for h in range(1, ahead):
        w[h] = tails(h)
    for h in range(N_HEADS):
        consume(0, h, w.pop(h), laters[h])
        if h + ahead < N_HEADS:
            w[h + ahead] = tails(h + ahead)
    o_ref[0] = acc_s[...].reshape(GROUP_W, TQ)


def _stick(p2, batch, seq):
    nq = seq // TQ
    t = batch * seq
    return pl.pallas_call(
        functools.partial(_stick_kernel, seq=seq),
        grid=(batch, nq),
        in_specs=[pl.BlockSpec((TQ, GROUP_W), lambda b, i: (b * nq + i, COL_QC // GROUP_W)),
                  pl.BlockSpec((seq, GROUP_W), lambda b, i: (b, COL_KC // GROUP_W)),
                  pl.BlockSpec((seq, GROUP_W), lambda b, i: (b, COL_VC // GROUP_W))],
        out_specs=pl.BlockSpec((1, GROUP_W, TQ), lambda b, i: (b * nq + i, 0, 0)),
        out_shape=jax.ShapeDtypeStruct((t // TQ, GROUP_W, TQ), F32),
        scratch_shapes=[pltpu.VMEM((seq, N_HEADS * LANES), BF16),
                        pltpu.VMEM((seq // TK, GROUP_W, TK), BF16),
                        pltpu.VMEM((N_HEADS // 2, TQ, LANES), BF16),
                        pltpu.VMEM((N_HEADS, 2, TK, TQ), BF16),
                        pltpu.VMEM((N_HEADS, TK, TQ), F32),
                        pltpu.VMEM((N_HEADS, HEAD_DIM, TQ), F32)],
        compiler_params=pltpu.CompilerParams(dimension_semantics=("parallel", "arbitrary"),
                                             vmem_limit_bytes=VMEM_LIMIT),
        name="stick_breaking",
    )(p2, p2, p2)


def _mixout_kernel(x_ref, ocmp_ref, oslc_ref, owin_ref, gate_ref, cvb_ref, cvc_ref, cvu_ref, pc_ref, pu_ref,
                   oc_ref, od_ref, bg_ref, cw_ref, gout_ref, wout_ref, o_ref, *, tiles_per_seq):
    i = pl.program_id(0)
    d = x_ref.shape[1]

    gates = jax.nn.sigmoid(gate_ref[...] + bg_ref[...])
    src = lax.broadcasted_iota(jnp.int32, (LANES, GROUP_W), 0)
    head3 = (lax.broadcasted_iota(jnp.int32, (LANES, GROUP_W), 1) >> 6) * 3
    oa = (_exact_dot(gates, (src == head3).astype(BF16)) * ocmp_ref[0].T
          + _exact_dot(gates, (src == head3 + 1).astype(BF16)) * oslc_ref[0].T
          + _exact_dot(gates, (src == head3 + 2).astype(BF16)) * owin_ref[0].T)
    cu = cvc_ref[...] * cvu_ref[...]
    prev = jnp.where(i % tiles_per_seq == 0, 0.0, pc_ref[...] * pu_ref[...])
    full = jnp.concatenate([prev, cu], axis=0)
    back1 = pltpu.roll(full, 1, 0)[8:, :]
    back2 = pltpu.roll(full, 2, 0)[8:, :]
    cw = cw_ref[...]
    ob = cvb_ref[...] * (cw[0:1, :] * back2 + cw[1:2, :] * back1 + cw[2:3, :] * cu)
    groups = jnp.concatenate([oa, ob, oc_ref[0].T, od_ref[0].T], axis=1)
    n_groups = d // HEAD_DIM
    gather = ((lax.broadcasted_iota(jnp.int32, (d, LANES), 0) >> 6)
              == lax.broadcasted_iota(jnp.int32, (d, LANES), 1)).astype(BF16)
    spread = (lax.broadcasted_iota(jnp.int32, (LANES, d), 0)
              == (lax.broadcasted_iota(jnp.int32, (LANES, d), 1) >> 6)).astype(BF16)
    ssq = _exact_dot(groups * groups, gather)
    inv = lax.rsqrt(ssq * (1.0 / HEAD_DIM) + RMS_EPS)
    mixed = (groups * _exact_dot(inv, spread) * gout_ref[...]).astype(BF16)
    assert n_groups <= LANES
    o_ref[...] = x_ref[...] + jnp.dot(mixed, wout_ref[...], preferred_element_type=F32)


def _mixout(x2, p2, ocmp, oslc, owin, oc, od, bg, cw, gout, wout, seq):
    t, d = x2.shape
    tm = 256
    rows = lambda w, col: pl.BlockSpec((tm, w), lambda i: (i, col // w))
    per_q = TQ // tm
    heads_t = pl.BlockSpec((1, GROUP_W, tm), lambda i: (i // per_q, 0, i % per_q))
    prev8 = lambda col: pl.BlockSpec((8, GROUP_W), lambda i: (jnp.maximum(i * (tm // 8) - 1, 0), col // GROUP_W))
    const = lambda shape: pl.BlockSpec(shape, lambda i: (0, 0))
    return pl.pallas_call(
        functools.partial(_mixout_kernel, tiles_per_seq=seq // tm),
        grid=(t // tm,),
        in_specs=[rows(d, 0), heads_t, heads_t, heads_t,
                  rows(LANES, COL_GATE), rows(GROUP_W, COL_CVB), rows(GROUP_W, COL_CVC), rows(GROUP_W, COL_CVU),
                  prev8(COL_CVC), prev8(COL_CVU),
                  heads_t, heads_t,
                  const((1, LANES)), const((8, GROUP_W)), const((1, d)), const((d, d))],
        out_specs=rows(d, 0),
        out_shape=jax.ShapeDtypeStruct((t, d), F32),
        compiler_params=pltpu.CompilerParams(dimension_semantics=("parallel",), vmem_limit_bytes=VMEM_LIMIT),
        name="mixout",
    )(x2, ocmp, oslc, owin, p2, p2, p2, p2, p2, p2, oc, od, bg, cw, gout, wout)


def _ffn_kernel(x_ref, g_ref, wg_ref, wu_ref, wd_ref, o_ref, act_s, *, ff_chunk, out_chunk):
    h = _rms(x_ref[...], g_ref[...]).astype(BF16)
    dff = wg_ref.shape[1]
    for c in range(dff // ff_chunk):
        cols = slice(c * ff_chunk, (c + 1) * ff_chunk)
        a = jnp.dot(h, wg_ref[:, cols], preferred_element_type=F32)
        u = jnp.dot(h, wu_ref[:, cols], preferred_element_type=F32)
        act_s[:, cols] = (a * jax.nn.sigmoid(a) * u).astype(BF16)
    d = o_ref.shape[1]
    for c in range(d // out_chunk):
        cols = slice(c * out_chunk, (c + 1) * out_chunk)
        o_ref[:, cols] = x_ref[:, cols] + jnp.dot(act_s[...], wd_ref[:, cols], preferred_element_type=F32)


def _ffn(x2, g, wg, wu, wd):
    t, d = x2.shape
    dff = wg.shape[1]
    tm = 512
    resident = lambda shape: pl.BlockSpec(shape, lambda i: (0, 0), pipeline_mode=pl.Buffered(1))
    return pl.pallas_call(
        functools.partial(_ffn_kernel, ff_chunk=256, out_chunk=256),
        grid=(t // tm,),
        in_specs=[pl.BlockSpec((tm, d), lambda i: (i, 0)),
                  pl.BlockSpec((1, d), lambda i: (0, 0)),
                  resident((d, dff)), resident((d, dff)), resident((dff, d))],
        out_specs=pl.BlockSpec((tm, d), lambda i: (i, 0)),
        out_shape=jax.ShapeDtypeStruct((t, d), F32),
        scratch_shapes=[pltpu.VMEM((tm, dff), BF16)],
        compiler_params=pltpu.CompilerParams(dimension_semantics=("parallel",), vmem_limit_bytes=VMEM_LIMIT),
        name="ffn",
    )(x2, g, wg, wu, wd)


def _permute_w_in_kernel(w_ref, o_ref):
    rows = w_ref.shape[0]
    n_rest = P_COLS - COL_CVB
    o_ref[:, 0:COL_GATE] = w_ref[:, 0:COL_GATE].astype(BF16)
    lane = lax.broadcasted_iota(jnp.int32, (rows, LANES), 1)
    o_ref[:, COL_GATE:COL_GATE + LANES] = jnp.where(lane < N_GATES, w_ref[:, COL_GATE:COL_GATE + LANES], 0.0).astype(BF16)
    o_ref[:, COL_CVB:P_COLS] = w_ref[:, COL_GATE + N_GATES:COL_GATE + N_GATES + n_rest].astype(BF16)


def _permute_w_in(w_in):
    d, cols = w_in.shape
    tr = 256
    return pl.pallas_call(
        _permute_w_in_kernel,
        grid=(d // tr,),
        in_specs=[pl.BlockSpec((tr, cols), lambda i: (i, 0))],
        out_specs=pl.BlockSpec((tr, P_COLS), lambda i: (i, 0)),
        out_shape=jax.ShapeDtypeStruct((d, P_COLS), BF16),
        compiler_params=pltpu.CompilerParams(dimension_semantics=("parallel",), vmem_limit_bytes=VMEM_LIMIT),
        name="permute_w_in",
    )(w_in)


def _layer(x2, batch, seq, g_mix, w_in, b_gate, g_q_nsa, g_k_cmp, g_k_slc, g_k_win, pe_k_cmp, pe_v_cmp,
           w1_k_cmp, w2_k_cmp, w1_v_cmp, w2_v_cmp, conv_w, g_q_dil, g_k_dil, g_out, w_out,
           g_ffn, w_gate, w_up, w_down):
    row = lambda v: v.reshape(1, -1)
    per_group = lambda g, n: jnp.tile(g, n).reshape(1, -1)
    p2 = _inproj(x2, row(g_mix), _permute_w_in(w_in))

    pe, w1, w2 = _compress_weights(pe_k_cmp, pe_v_cmp, w1_k_cmp, w2_k_cmp, w1_v_cmp, w2_v_cmp)
    kc, vc = _compress(p2, pe, w1, w2, per_group(g_k_cmp, 2), batch, seq)
    ocmp, oslc = _nsa(p2, kc, vc, per_group(g_q_nsa, N_HEADS), per_group(g_k_slc, 2), batch, seq)
    owin = _banded(p2, per_group(g_q_nsa, N_HEADS), per_group(g_k_win, 2), jnp.asarray(_band_bias_tiles(NSA_WINDOW - 1, _window_mult)),
                   batch, seq, window=NSA_WINDOW - 1, slopes=SLOPES_A, shared_kv=True,
                   col_q=COL_QA, col_k=COL_KWV, col_v=COL_KWV, name="nsa_window")
    od = _banded(p2, per_group(g_q_dil, N_HEADS), per_group(g_k_dil, N_HEADS),
                 jnp.asarray(_band_bias_tiles(DILATED_CONFIGS[-1][0], _dilated_mult)),
                 batch, seq, window=DILATED_CONFIGS[-1][0], slopes=SLOPES_D, shared_kv=False,
                 col_q=COL_QD, col_k=COL_KD, col_v=COL_VD, name="dilated")
    oc = _stick(p2, batch, seq)

    bg = jnp.zeros((1, LANES), F32).at[0, :N_GATES].set(b_gate)
    cwp = jnp.zeros((8, GROUP_W), F32).at[:CONV_K].set(conv_w)
    x1 = _mixout(x2, p2, ocmp, oslc, owin, oc, od, bg, cwp, row(g_out), w_out.astype(BF16), seq)
    return _ffn(x1, row(g_ffn), w_gate.astype(BF16), w_up.astype(BF16), w_down.astype(BF16))


def kernel(x, g_mix, w_in, b_gate, g_q_nsa, g_k_cmp, g_k_slc, g_k_win, pe_k_cmp, pe_v_cmp, w1_k_cmp, w2_k_cmp,
           w1_v_cmp, w2_v_cmp, conv_w, g_q_dil, g_k_dil, g_out, w_out, g_ffn, w_gate, w_up, w_down):
    batch, seq, d = x.shape
    assert seq % (CMP_STRIDE * LANES) == 0 and d % LANES == 0
    x2 = x.reshape(batch * seq, d)
    params = (g_mix, w_in, b_gate, g_q_nsa, g_k_cmp, g_k_slc, g_k_win, pe_k_cmp, pe_v_cmp, w1_k_cmp, w2_k_cmp,
              w1_v_cmp, w2_v_cmp, conv_w, g_q_dil, g_k_dil, g_out, w_out, g_ffn, w_gate, w_up, w_down)
    for layer in range(g_mix.shape[0]):
        x2 = _layer(x2, batch, seq, *[p[layer] for p in params])
    return x2.reshape(batch, seq, d)
```

```python
import functools

import numpy as np
import jax
import jax.numpy as jnp
from jax import lax
from jax.experimental import pallas as pl
from jax.experimental.pallas import tpu as pltpu

F32 = jnp.float32
BF16 = jnp.bfloat16

HEAD_DIM = 64
N_HEADS = 4
GROUP_W = N_HEADS * HEAD_DIM
CONV_K = 3
CMP_LEN = 32
CMP_STRIDE = 16
SEL_BLOCK = 64
N_SELECT = 16
NSA_WINDOW = 512
DILATED_CONFIGS = ((128, 1), (512, 4), (2048, 16))
NEG_INF = -1e30
FORCE_SCORE = 1e6
RMS_EPS = 1e-6

TQ = 512
TK = 256
N_TOP = TQ // TK
STICK_TQ = 256
LANES = 128
MXU_DEPTH = 256
MXU_AHEAD = 2
FLASH_AHEAD = 2
SUM_PAD = 16
LOG2E = 1.4426950408889634
VMEM_LIMIT = 52 * 1024 * 1024

COL_QA, COL_KVC, COL_KSV, COL_KWV, COL_GATE = 0, 256, 384, 512, 640
COL_CVB, COL_CVC, COL_CVU = 768, 1024, 1280
COL_QC, COL_KC, COL_VC = 1536, 1792, 2048
COL_QD, COL_KD, COL_VD = 2304, 2560, 2816
P_COLS = 3072
N_GATES = 12

_NT = (((1,), (1,)), ((), ()))


def _alibi_slopes():
    s = [2.0 ** (-8.0 * i / 8) for i in range(1, 9)]
    return tuple(s[0::2]), tuple(s[1::2])


SLOPES_A, SLOPES_D = _alibi_slopes()


def _rms(x, g):
    return x * lax.rsqrt(jnp.mean(x * x, axis=-1, keepdims=True) + RMS_EPS) * g


def _exact_dot(a, sel):
    hi = a.astype(BF16)
    lo = (a - hi.astype(F32)).astype(BF16)
    if 2 * a.shape[1] <= MXU_DEPTH:
        return jnp.dot(jnp.concatenate([hi, lo], axis=1), jnp.concatenate([sel, sel], axis=0),
                       preferred_element_type=F32)
    return jnp.dot(hi, sel, preferred_element_type=F32) + jnp.dot(lo, sel, preferred_element_type=F32)


def _group_rms(x, g):
    w = x.shape[1]
    same = ((lax.broadcasted_iota(jnp.int32, (w, w), 0) >> 6)
            == (lax.broadcasted_iota(jnp.int32, (w, w), 1) >> 6)).astype(BF16)
    ssq = _exact_dot(x * x, same)
    return x * lax.rsqrt(ssq * (1.0 / HEAD_DIM) + RMS_EPS) * g


EVEN_AUG, ODD_AUG = HEAD_DIM, 0


def _key_aug(pos, first):
    lane = lax.broadcasted_iota(jnp.int32, (pos.shape[0], LANES), 1) - first
    hi = (pos >> 6).astype(F32)
    lo = (pos & 63).astype(F32)
    return jnp.where(lane == 0, hi, jnp.where(lane == 1, lo, jnp.where((lane == 2) | (lane == 3), 1.0, 0.0)))


def _query_aug(t, slope, first):
    lane = lax.broadcasted_iota(jnp.int32, (t.shape[0], LANES), 1) - first
    hi = (t >> 6).astype(F32) * (-64.0 * slope)
    lo = (t & 63).astype(F32) * (-slope)
    return jnp.where(lane == 0, 64.0 * slope,
                     jnp.where(lane == 1, slope, jnp.where(lane == 2, hi, jnp.where(lane == 3, lo, 0.0))))


def _pair_keys(pair, pos):
    lane = lax.broadcasted_iota(jnp.int32, pair.shape, 1)
    aug_e = 0.0 if pos is None else _key_aug(pos, EVEN_AUG)
    aug_o = 0.0 if pos is None else _key_aug(pos, ODD_AUG)
    return (jnp.where(lane < HEAD_DIM, pair, aug_e).astype(BF16),
            jnp.where(lane >= HEAD_DIM, pair, aug_o).astype(BF16))


def _prep_queries(q, g, t0, slopes, qp_s):
    qn = (q if g is None else _group_rms(q, g)) * (HEAD_DIM ** -0.5)
    tpos = t0 + lax.broadcasted_iota(jnp.int32, (q.shape[0], 1), 0)
    lane = lax.broadcasted_iota(jnp.int32, (q.shape[0], LANES), 1)
    for i in range(N_HEADS // 2):
        pair = qn[:, i * LANES:(i + 1) * LANES]
        if slopes is None:
            qp_s[i] = pair.astype(BF16)
            continue
        even = jnp.where((lane >= EVEN_AUG) & (lane < EVEN_AUG + 4), _query_aug(tpos, slopes[2 * i], EVEN_AUG), pair)
        odd = jnp.where(lane < ODD_AUG + 4, _query_aug(tpos, slopes[2 * i + 1], ODD_AUG), pair)
        qp_s[2 * i] = even.astype(BF16)
        qp_s[2 * i + 1] = odd.astype(BF16)


def _inproj_kernel(x_ref, g_ref, w_ref, o_ref, *, nchunk):
    h = _rms(x_ref[...], g_ref[...]).astype(BF16)
    cw = P_COLS // nchunk
    for c in range(nchunk):
        o_ref[:, c * cw:(c + 1) * cw] = jnp.dot(h, w_ref[:, c * cw:(c + 1) * cw], preferred_element_type=F32)


def _inproj(x2, g, w):
    t, d = x2.shape
    tm = 512
    return pl.pallas_call(
        functools.partial(_inproj_kernel, nchunk=6),
        grid=(t // tm,),
        in_specs=[pl.BlockSpec((tm, d), lambda i: (i, 0)),
                  pl.BlockSpec((1, d), lambda i: (0, 0)),
                  pl.BlockSpec((d, P_COLS), lambda i: (0, 0))],
        out_specs=pl.BlockSpec((tm, P_COLS), lambda i: (i, 0)),
        out_shape=jax.ShapeDtypeStruct((t, P_COLS), F32),
        compiler_params=pltpu.CompilerParams(dimension_semantics=("parallel",), vmem_limit_bytes=VMEM_LIMIT),
        name="inproj",
    )(x2, g, w)


def _gelu_tanh(x):
    return x * (0.5 * (1.0 + jnp.tanh(np.sqrt(2.0 / np.pi).astype(np.float32) * (x + 0.044715 * (x * x * x)))))


def _compress_kernel(kv_ref, pe_ref, w1_ref, w2_ref, gk_ref, kc_ref, vct_ref):
    nc = kv_ref.shape[0] // CMP_STRIDE
    first = None
    second = None
    for j in range(CMP_STRIDE):
        tok = kv_ref[pl.ds(j, nc, stride=CMP_STRIDE), :]
        a = jnp.dot((tok + pe_ref[j:j + 1, :]).astype(BF16), w1_ref[j], preferred_element_type=F32)
        b = jnp.dot((tok + pe_ref[CMP_STRIDE + j:CMP_STRIDE + j + 1, :]).astype(BF16), w1_ref[CMP_STRIDE + j],
                    preferred_element_type=F32)
        first = a if first is None else first + a
        second = b if second is None else second + b
    hid = first + pltpu.roll(second, nc - 1, 0)
    out = jnp.dot(_gelu_tanh(hid).astype(BF16), w2_ref[...], preferred_element_type=F32)
    kn = _group_rms(out, gk_ref[...])
    end = lax.broadcasted_iota(jnp.int32, (nc, 1), 0) * CMP_STRIDE + (CMP_LEN - 1)
    even, _ = _pair_keys(kn, end)
    _, odd = _pair_keys(pltpu.roll(kn, HEAD_DIM, 1), end)
    kc_ref[0] = jnp.concatenate([even, odd], axis=1)
    vct_ref[0] = out.T[HEAD_DIM:, :].astype(BF16)


def _compress(p2, pe, w1, w2, gk, batch, seq):
    nc = seq // CMP_STRIDE
    hid2 = w1.shape[2]
    return pl.pallas_call(
        _compress_kernel,
        grid=(batch,),
        in_specs=[pl.BlockSpec((seq, LANES), lambda i: (i, COL_KVC // LANES)),
                  pl.BlockSpec((CMP_LEN, LANES), lambda i: (0, 0)),
                  pl.BlockSpec((CMP_LEN, LANES, hid2), lambda i: (0, 0, 0)),
                  pl.BlockSpec((hid2, LANES), lambda i: (0, 0)),
                  pl.BlockSpec((1, LANES), lambda i: (0, 0))],
        out_specs=[pl.BlockSpec((1, nc, 2 * LANES), lambda i: (i, 0, 0)),
                   pl.BlockSpec((1, HEAD_DIM, nc), lambda i: (i, 0, 0))],
        out_shape=[jax.ShapeDtypeStruct((batch, nc, 2 * LANES), BF16),
                   jax.ShapeDtypeStruct((batch, HEAD_DIM, nc), BF16)],
        compiler_params=pltpu.CompilerParams(dimension_semantics=("parallel",), vmem_limit_bytes=VMEM_LIMIT),
        name="nsa_compress",
    )(p2, pe, w1, w2, gk)


def _compress_weights(pe_k, pe_v, w1_k, w2_k, w1_v, w2_v):
    hid = w1_k.shape[1]
    w1k = w1_k.reshape(CMP_LEN, HEAD_DIM, hid)
    w1v = w1_v.reshape(CMP_LEN, HEAD_DIM, hid)
    z1 = jnp.zeros_like(w1k)
    w1 = jnp.concatenate([jnp.concatenate([w1k, z1], axis=2), jnp.concatenate([z1, w1v], axis=2)], axis=1)
    z2 = jnp.zeros_like(w2_k)
    w2 = jnp.concatenate([jnp.concatenate([w2_k, z2], axis=1), jnp.concatenate([z2, w2_v], axis=1)], axis=0)
    pe = jnp.concatenate([pe_k, pe_v], axis=1)
    return pe, w1.astype(BF16), w2.astype(BF16)


def _flash_scores(kp_of, bias, qp_s, s_s, m_cur):
    m_next = []
    for h in range(N_HEADS):
        s = lax.dot_general(kp_of(h), qp_s[h], _NT, preferred_element_type=F32) + bias
        s_s[h] = s
        m_next.append(jnp.maximum(m_cur[h], jnp.max(s, axis=0, keepdims=True)))
    return tuple(m_next)


def _flash_accum(vt_of, s_s, acc_s, m_prev, m_cur, ls):
    new_l = []
    for h in range(N_HEADS):
        alpha = jnp.exp(m_prev[h] - m_cur[h])
        p = jnp.exp(s_s[h] - m_cur[h])
        new_l.append(alpha * ls[h] + jnp.sum(p, axis=0, keepdims=True))
        acc_s[h] = alpha * acc_s[h] + jnp.dot(vt_of(h), p.astype(BF16), preferred_element_type=F32)
    return tuple(new_l)


def _flash_accum_and_scores(kp_of, bias, vt_of, qp_s, s_s, acc_s, m_prev, m_cur, ls):
    def scores(h):
        return lax.dot_general(kp_of(h), qp_s[h], _NT, preferred_element_type=F32) + bias

    m_next, new_l = [], []
    s_new = {h: scores(h) for h in range(FLASH_AHEAD)}
    for h in range(N_HEADS):
        alpha = jnp.exp(m_prev[h] - m_cur[h])
        p = jnp.exp(s_s[h] - m_cur[h])
        new_l.append(alpha * ls[h] + jnp.sum(p, axis=0, keepdims=True))
        acc_s[h] = alpha * acc_s[h] + jnp.dot(vt_of(h), p.astype(BF16), preferred_element_type=F32)
        s_h = s_new.pop(h)
        s_s[h] = s_h
        m_next.append(jnp.maximum(m_cur[h], jnp.max(s_h, axis=0, keepdims=True)))
        if h + FLASH_AHEAD < N_HEADS:
            s_new[h + FLASH_AHEAD] = scores(h + FLASH_AHEAD)
    return tuple(m_next), tuple(new_l)


def _flash_pipelined(tile_of, n_tiles, kp_fn, bias_fn, vt_fn, qp_s, s_s, acc_s, o_ref):
    acc_s[...] = jnp.zeros(acc_s.shape, F32)
    m0 = tuple(jnp.full((1, TQ), NEG_INF, F32) for _ in range(N_HEADS))
    l0 = tuple(jnp.zeros((1, TQ), F32) for _ in range(N_HEADS))
    first_kt = tile_of(0)
    m1 = _flash_scores(kp_fn(first_kt), bias_fn(first_kt, True), qp_s, s_s, m0)

    def step(i, carry, top):
        m_prev, m_cur, ls = carry
        nxt = tile_of(i + 1)
        m_next, ls = _flash_accum_and_scores(kp_fn(nxt), bias_fn(nxt, top), vt_fn(tile_of(i)), qp_s, s_s, acc_s,
                                             m_prev, m_cur, ls)
        return m_cur, m_next, ls

    carry = (m0, m1, l0)
    for i in range(N_TOP - 1):
        carry = step(i, carry, True)
    m_prev, m_cur, ls = lax.fori_loop(N_TOP - 1, n_tiles - 1, lambda i, c: step(i, c, False), carry)
    ls = _flash_accum(vt_fn(tile_of(n_tiles - 1)), s_s, acc_s, m_prev, m_cur, ls)
    o_ref[0] = jnp.concatenate([acc_s[h] * (1.0 / ls[h]) for h in range(N_HEADS)], axis=0)


def _nsa_kernel(q_ref, kc_ref, vct_ref, ksv_ref, gq_ref, gks_ref, ocmp_ref, oslc_ref,
                ksp_s, vst_s, qp_s, imp_s, sel_s, s_s, acc_s, hit_s, *, seq):
    qi = pl.program_id(1)
    nc = seq // CMP_STRIDE
    nsel = seq // SEL_BLOCK
    nkt = seq // TK

    @pl.when(qi == 0)
    def _prep():
        for c in range(nkt):
            rows = slice(c * TK, (c + 1) * TK)
            blk = ksv_ref[rows, :]
            pos = c * TK + lax.broadcasted_iota(jnp.int32, (TK, 1), 0)
            kn = _group_rms(blk, gks_ref[...])
            ksp_s[rows, 0:LANES], _ = _pair_keys(kn, pos)
            _, ksp_s[rows, LANES:2 * LANES] = _pair_keys(pltpu.roll(kn, HEAD_DIM, 1), pos)
            vst_s[c] = blk.T[HEAD_DIM:, :].astype(BF16)
        imp_s[:, 0:8, :] = jnp.zeros((TQ // LANES, 8, LANES), F32)

    t0 = qi * TQ
    _prep_queries(q_ref[...], gq_ref[...], t0, SLOPES_A, qp_s)

    kc = [kc_ref[0, :, 0:LANES], kc_ref[0, :, LANES:2 * LANES]]
    vct = vct_ref[0]
    tq_row = t0 + lax.broadcasted_iota(jnp.int32, (nc, TQ), 1)
    n_idx = lax.broadcasted_iota(jnp.int32, (nc, TQ), 0)
    vis = (tq_row >= n_idx * CMP_STRIDE + (CMP_LEN - 1)) & (n_idx < nc - 1)
    vis_bias = jnp.where(vis, 0.0, NEG_INF)
    sees_any = (t0 + lax.broadcasted_iota(jnp.int32, (1, TQ), 1) >= CMP_LEN - 1).astype(F32)
    imp = jnp.zeros((nc, TQ), F32)
    ocmp_t = []
    scores = [lax.dot_general(kc[h % 2], qp_s[h], _NT, preferred_element_type=F32) + vis_bias
              for h in range(N_HEADS)]
    for h in range(N_HEADS):
        sc = scores[h]
        e = jnp.exp(sc - jnp.max(sc, axis=0, keepdims=True))
        p = e * (sees_any / jnp.sum(e, axis=0, keepdims=True))
        ocmp_t.append(jnp.dot(vct, p.astype(BF16), preferred_element_type=F32))
        imp = imp + p
    ocmp_ref[0] = jnp.concatenate(ocmp_t, axis=0)

    halves = []
    for half in range(TQ // LANES):
        imp_s[half, 8:8 + nc, :] = imp[:, half * LANES:(half + 1) * LANES]
        r = [imp_s[half, pl.ds(8 + k, nsel, stride=4), :] for k in range(4)]
        rm1 = imp_s[half, pl.ds(7, nsel, stride=4), :]
        halves.append(rm1 + 2.0 * (r[0] + r[1] + r[2]) + r[3])
    imp_blk = jnp.concatenate(halves, axis=1)
    blk = lax.broadcasted_iota(jnp.int32, (nsel, TQ), 0)
    tl = t0 + lax.broadcasted_iota(jnp.int32, (nsel, TQ), 1)
    cur = tl >> 6
    forced = (blk == 0) | (blk == cur) | (blk == cur - 1)
    valid = blk * SEL_BLOCK <= tl
    score = jnp.where(forced, FORCE_SCORE, jnp.where(valid, imp_blk, -FORCE_SCORE))
    sub = 8
    per_tile = TK // SEL_BLOCK
    tile_hit = []
    for g in range(nsel // sub):
        mine = score[g * sub:(g + 1) * sub, :]
        blk_g = g * sub + lax.broadcasted_iota(jnp.int32, (sub, TQ), 0)
        rank = jnp.zeros((sub, TQ), F32)
        for j in range(nsel):
            row = score[j:j + 1, :]
            if j < g * sub:
                beats = row >= mine
            elif j >= (g + 1) * sub:
                beats = row > mine
            else:
                beats = (row > mine) | ((row == mine) & (blk_g > j))
            rank = rank + jnp.where(beats, 1.0, 0.0)
        sel_g = jnp.where(rank < min(N_SELECT, nsel), 0.0, NEG_INF)
        sel_s[g * sub:(g + 1) * sub, :] = sel_g
        for part in range(sub // per_tile):
            tile_hit.append(jnp.max(sel_g[part * per_tile:(part + 1) * per_tile, :]))

    n_hit = jnp.int32(0)
    for kt in reversed(range(nkt)):
        hit_s[n_hit] = kt
        n_hit = n_hit + ((tile_hit[kt] > -1.0) & (kt * TK < t0)).astype(jnp.int32)

    def block_bias(kt):
        return jnp.concatenate([jnp.broadcast_to(sel_s[pl.ds(kt * per_tile + j, 1), :], (SEL_BLOCK, TQ))
                                for j in range(per_tile)], axis=0)

    k_local = lax.broadcasted_iota(jnp.int32, (TK, TQ), 0)
    q_local = lax.broadcasted_iota(jnp.int32, (TK, TQ), 1)
    top_kt = qi * N_TOP + (N_TOP - 1)

    def kp_fn(kt):
        off = pl.multiple_of(kt * TK, TK)
        return lambda h: ksp_s[pl.ds(off, TK), (h % 2) * LANES:(h % 2 + 1) * LANES]

    def bias_fn(kt, top):
        if top:
            return jnp.where(kt * TK - t0 + k_local <= q_local, block_bias(kt), NEG_INF)
        return block_bias(kt)

    tile_of = lambda i: jnp.where(i < N_TOP, top_kt - i, hit_s[jnp.maximum(i - N_TOP, 0)])
    _flash_pipelined(tile_of, n_hit + N_TOP, kp_fn, bias_fn, lambda kt: (lambda h: vst_s[kt]), qp_s, s_s, acc_s,
                     oslc_ref)


def _nsa(p2, kc, vc, gq, gks, batch, seq):
    nq = seq // TQ
    nc = seq // CMP_STRIDE
    t = batch * seq
    return pl.pallas_call(
        functools.partial(_nsa_kernel, seq=seq),
        grid=(batch, nq),
        in_specs=[pl.BlockSpec((TQ, GROUP_W), lambda b, i: (b * nq + i, COL_QA // GROUP_W)),
                  pl.BlockSpec((1, nc, 2 * LANES), lambda b, i: (b, 0, 0)),
                  pl.BlockSpec((1, HEAD_DIM, nc), lambda b, i: (b, 0, 0)),
                  pl.BlockSpec((seq, LANES), lambda b, i: (b, COL_KSV // LANES)),
                  pl.BlockSpec((1, GROUP_W), lambda b, i: (0, 0)),
                  pl.BlockSpec((1, LANES), lambda b, i: (0, 0))],
        out_specs=[pl.BlockSpec((1, GROUP_W, TQ), lambda b, i: (b * nq + i, 0, 0)),
                   pl.BlockSpec((1, GROUP_W, TQ), lambda b, i: (b * nq + i, 0, 0))],
        out_shape=[jax.ShapeDtypeStruct((t // TQ, GROUP_W, TQ), F32),
                   jax.ShapeDtypeStruct((t // TQ, GROUP_W, TQ), F32)],
        scratch_shapes=[pltpu.VMEM((seq, 2 * LANES), BF16),
                        pltpu.VMEM((seq // TK, HEAD_DIM, TK), BF16),
                        pltpu.VMEM((N_HEADS, TQ, LANES), BF16),
                        pltpu.VMEM((TQ // LANES, 8 + nc, LANES), F32),
                        pltpu.VMEM((seq // SEL_BLOCK, TQ), F32),
                        pltpu.VMEM((N_HEADS, TK, TQ), F32),
                        pltpu.VMEM((N_HEADS, HEAD_DIM, TQ), F32),
                        pltpu.SMEM((seq // TK + 1,), jnp.int32)],
        compiler_params=pltpu.CompilerParams(dimension_semantics=("parallel", "arbitrary"),
                                             vmem_limit_bytes=VMEM_LIMIT),
        name="nsa_cmp_slc",
    )(p2, kc, vc, p2, gq, gks)


def _band_bias_tiles(window, mult_fn):
    nd = (window + TQ - 1) // TK + 1
    kk = np.arange(TK)[:, None]
    qq = np.arange(TQ)[None, :]
    tiles = np.empty((nd, TK, TQ), np.float32)
    for di in range(nd):
        d = (di - (N_TOP - 1)) * TK + qq - kk
        mult = mult_fn(d)
        tiles[di] = np.where(mult > 0, np.log(np.maximum(mult, 1)), NEG_INF)
    return tiles


def _window_mult(d):
    return ((d >= 0) & (d <= NSA_WINDOW - 1)).astype(np.float64)


def _dilated_mult(d):
    m = np.zeros(d.shape, np.float64)
    for window, dil in DILATED_CONFIGS:
        m += ((d >= 0) & (d <= window) & (d % dil == 0)).astype(np.float64)
    return m


def _banded_kernel(*refs, seq, window, slopes, shared_kv):
    if shared_kv:
        q_ref, kv_ref, gq_ref, gk_ref, bias_ref, o_ref, kp_s, vt_s, qp_s, s_s, acc_s = refs
    else:
        q_ref, k_ref, v_ref, gq_ref, gk_ref, bias_ref, o_ref, kp_s, vt_s, qp_s, s_s, acc_s = refs
    qi = pl.program_id(1)
    nkt = seq // TK

    @pl.when(qi == 0)
    def _prep():
        for c in range(nkt):
            rows = slice(c * TK, (c + 1) * TK)
            pos = c * TK + lax.broadcasted_iota(jnp.int32, (TK, 1), 0)
            if shared_kv:
                blk = kv_ref[rows, :]
                kn = _group_rms(blk, gk_ref[...])
                kp_s[rows, 0:LANES], _ = _pair_keys(kn, pos)
                _, kp_s[rows, LANES:2 * LANES] = _pair_keys(pltpu.roll(kn, HEAD_DIM, 1), pos)
                vt_s[c] = blk.T[HEAD_DIM:, :].astype(BF16)
            else:
                kn = _group_rms(k_ref[rows, :], gk_ref[...])
                for i in range(N_HEADS // 2):
                    even, odd = _pair_keys(kn[:, i * LANES:(i + 1) * LANES], pos)
                    kp_s[rows, 2 * i * LANES:(2 * i + 1) * LANES] = even
                    kp_s[rows, (2 * i + 1) * LANES:(2 * i + 2) * LANES] = odd
                vt_s[c] = v_ref[rows, :].T.astype(BF16)

    t0 = qi * TQ
    _prep_queries(q_ref[...], gq_ref[...], t0, slopes, qp_s)
    kt_lo = jnp.maximum(t0 - window, 0) // TK

    def kp_fn(kt):
        off = pl.multiple_of(kt * TK, TK)
        if shared_kv:
            return lambda h: kp_s[pl.ds(off, TK), (h % 2) * LANES:(h % 2 + 1) * LANES]
        return lambda h: kp_s[pl.ds(off, TK), h * LANES:(h + 1) * LANES]

    def vt_fn(kt):
        if shared_kv:
            return lambda h: vt_s[kt]
        return lambda h: vt_s[kt, h * HEAD_DIM:(h + 1) * HEAD_DIM, :]

    top_kt = qi * N_TOP + (N_TOP - 1)
    _flash_pipelined(lambda i: top_kt - i, top_kt + 1 - kt_lo, kp_fn, lambda kt, top: bias_ref[top_kt - kt], vt_fn,
                     qp_s, s_s, acc_s, o_ref)


def _banded(p2, gq, gk, bias_tiles, batch, seq, *, window, slopes, shared_kv, col_q, col_k, col_v, name):
    nq = seq // TQ
    t = batch * seq
    nd = bias_tiles.shape[0]
    n_kv = 1 if shared_kv else N_HEADS
    q_spec = pl.BlockSpec((TQ, GROUP_W), lambda b, i: (b * nq + i, col_q // GROUP_W))
    gain = lambda g: pl.BlockSpec(g.shape, lambda b, i: (0, 0))
    bias_spec = pl.BlockSpec((nd, TK, TQ), lambda b, i: (0, 0, 0))
    if shared_kv:
        kv_specs = [pl.BlockSpec((seq, LANES), lambda b, i: (b, col_k // LANES))]
        operands = (p2, p2, gq, gk, bias_tiles)
    else:
        kv_specs = [pl.BlockSpec((seq, GROUP_W), lambda b, i: (b, col_k // GROUP_W)),
                    pl.BlockSpec((seq, GROUP_W), lambda b, i: (b, col_v // GROUP_W))]
        operands = (p2, p2, p2, gq, gk, bias_tiles)
    return pl.pallas_call(
        functools.partial(_banded_kernel, seq=seq, window=window, slopes=slopes, shared_kv=shared_kv),
        grid=(batch, nq),
        in_specs=[q_spec] + kv_specs + [gain(gq), gain(gk), bias_spec],
        out_specs=pl.BlockSpec((1, GROUP_W, TQ), lambda b, i: (b * nq + i, 0, 0)),
        out_shape=jax.ShapeDtypeStruct((t // TQ, GROUP_W, TQ), F32),
        scratch_shapes=[pltpu.VMEM((seq, (2 if shared_kv else N_HEADS) * LANES), BF16),
                        pltpu.VMEM((seq // TK, n_kv * HEAD_DIM, TK), BF16),
                        pltpu.VMEM((N_HEADS, TQ, LANES), BF16),
                        pltpu.VMEM((N_HEADS, TK, TQ), F32),
                        pltpu.VMEM((N_HEADS, HEAD_DIM, TQ), F32)],
        compiler_params=pltpu.CompilerParams(dimension_semantics=("parallel", "arbitrary"),
                                             vmem_limit_bytes=VMEM_LIMIT),
        name=name,
    )(*operands)


def _stick_kernel(q_ref, k_ref, v_ref, o_ref, kp_s, vt_s, qp_s, hl_s, e_s, acc_s, *, seq):
    TQ, N_TOP = STICK_TQ, STICK_TQ // TK
    qi = pl.program_id(1)
    nkt = seq // TK

    @pl.when(qi == 0)
    def _prep():
        for c in range(nkt):
            rows = slice(c * TK, (c + 1) * TK)
            kb = k_ref[rows, :]
            for i in range(N_HEADS // 2):
                even, odd = _pair_keys(kb[:, i * LANES:(i + 1) * LANES], None)
                kp_s[rows, 2 * i * LANES:(2 * i + 1) * LANES] = even
                kp_s[rows, (2 * i + 1) * LANES:(2 * i + 2) * LANES] = odd
            vt_s[c] = v_ref[rows, :].T.astype(BF16)

    _prep_queries(q_ref[...], None, qi * TQ, None, qp_s)
    half = TK // 2
    col = lax.broadcasted_iota(jnp.int32, (half + SUM_PAD, TK), 1) & (half - 1)
    srow = lax.broadcasted_iota(jnp.int32, (half + SUM_PAD, TK), 0)
    sums = ((srow == half) | ((srow < half) & (col > srow))).astype(BF16)
    k_local = lax.broadcasted_iota(jnp.int32, (TK, TQ), 0)
    q_local = lax.broadcasted_iota(jnp.int32, (TK, TQ), 1)
    top_kt = qi * N_TOP + (N_TOP - 1)
    acc_s[...] = jnp.zeros(acc_s.shape, F32)

    def past_mask(tiles_above):
        return tiles_above * TK + k_local < q_local

    def logits(kt, h):
        off = pl.multiple_of(kt * TK, TK)
        return lax.dot_general(kp_s[pl.ds(off, TK), h * LANES:(h + 1) * LANES], qp_s[h // 2], _NT,
                               preferred_element_type=F32)

    def stage(h, z, past):
        sp = jnp.maximum(z, 0.0) + jnp.log(1.0 + jnp.exp2(jnp.abs(z) * (-LOG2E)))
        spm = sp if past is None else jnp.where(past, sp, 0.0)
        hi = spm.astype(BF16)
        lo = (spm - hi.astype(F32)).astype(BF16)
        for b in range(2):
            hl_s[h, b, 0:half, :] = hi[b * half:(b + 1) * half, :]
            hl_s[h, b, half:TK, :] = lo[b * half:(b + 1) * half, :]
        logsig = z - sp
        e_s[h] = logsig if past is None else jnp.where(past, logsig, NEG_INF)

    def tails(h):
        return [jnp.dot(sums, hl_s[h, b], preferred_element_type=F32) for b in range(2)]

    def consume(kt, h, w, later):
        later_lo = later + w[1][half:half + 1, :]
        attn = jnp.concatenate([jnp.exp(e_s[h, 0:half, :] - w[0][0:half, :] - later_lo),
                                jnp.exp(e_s[h, half:TK, :] - w[1][0:half, :] - later)], axis=0)
        acc_s[h] += jnp.dot(vt_s[kt, h * HEAD_DIM:(h + 1) * HEAD_DIM, :], attn.astype(BF16),
                            preferred_element_type=F32)
        return later_lo + w[0][half:half + 1, :]

    ahead = MXU_AHEAD
    for h in range(N_HEADS):
        stage(h, logits(top_kt, h), past_mask(N_TOP - 1))

    def step(kt, carry, past):
        laters, w_first, z_first = carry
        w = {0: list(w_first)}
        z = {0: z_first}
        for h in range(1, ahead):
            w[h] = tails(h)
            z[h] = logits(kt - 1, h)
        out = []
        for h in range(N_HEADS):
            out.append(consume(kt, h, w.pop(h), laters[h]))
            stage(h, z.pop(h), past)
            if h + ahead < N_HEADS:
                w[h + ahead] = tails(h + ahead)
                z[h + ahead] = logits(kt - 1, h + ahead)
            elif h + ahead == N_HEADS:
                nxt = (tuple(tails(0)), logits(jnp.maximum(kt - 2, 0), 0))
        return tuple(out), nxt[0], nxt[1]

    carry = (tuple(jnp.zeros((1, TQ), F32) for _ in range(N_HEADS)),
             tuple(tails(0)), logits(jnp.maximum(top_kt - 1, 0), 0))
    for j in range(N_TOP - 1):
        carry = step(top_kt - j, carry, past_mask(N_TOP - 2 - j))
    below = qi * N_TOP
    laters, w_first, _ = lax.fori_loop(0, below, lambda i, c: step(below - i, c, None), carry)
    w = {0: list(w_first)}
    for h in range(1, ahead):
        w[h] = tails(h)
    for h in range(N_HEADS):
        consume(0, h, w.pop(h), laters[h])
        if h + ahead < N_HEADS:
            w[h + ahead] = tails(h + ahead)
    o_ref[0] = acc_s[...].reshape(GROUP_W, TQ)


def _stick(p2, batch, seq):
    TQ = STICK_TQ
    nq = seq // TQ
    t = batch * seq
    return pl.pallas_call(
        functools.partial(_stick_kernel, seq=seq),
        grid=(batch, nq),
        in_specs=[pl.BlockSpec((TQ, GROUP_W), lambda b, i: (b * nq + i, COL_QC // GROUP_W)),
                  pl.BlockSpec((seq, GROUP_W), lambda b, i: (b, COL_KC // GROUP_W)),
                  pl.BlockSpec((seq, GROUP_W), lambda b, i: (b, COL_VC // GROUP_W))],
        out_specs=pl.BlockSpec((1, GROUP_W, TQ), lambda b, i: (b * nq + i, 0, 0)),
        out_shape=jax.ShapeDtypeStruct((t // TQ, GROUP_W, TQ), F32),
        scratch_shapes=[pltpu.VMEM((seq, N_HEADS * LANES), BF16),
                        pltpu.VMEM((seq // TK, GROUP_W, TK), BF16),
                        pltpu.VMEM((N_HEADS // 2, TQ, LANES), BF16),
                        pltpu.VMEM((N_HEADS, 2, TK, TQ), BF16),
                        pltpu.VMEM((N_HEADS, TK, TQ), F32),
                        pltpu.VMEM((N_HEADS, HEAD_DIM, TQ), F32)],
        compiler_params=pltpu.CompilerParams(dimension_semantics=("parallel", "arbitrary"),
                                             vmem_limit_bytes=VMEM_LIMIT),
        name="stick_breaking",
    )(p2, p2, p2)


def _mixout_kernel(x_ref, ocmp_ref, oslc_ref, owin_ref, gate_ref, cvb_ref, cvc_ref, cvu_ref, pc_ref, pu_ref,
                   oc_ref, od_ref, bg_ref, cw_ref, gout_ref, wout_ref, o_ref, *, tiles_per_seq):
    i = pl.program_id(0)
    d = x_ref.shape[1]

    gates = jax.nn.sigmoid(gate_ref[...] + bg_ref[...])
    src = lax.broadcasted_iota(jnp.int32, (LANES, GROUP_W), 0)
    head3 = (lax.broadcasted_iota(jnp.int32, (LANES, GROUP_W), 1) >> 6) * 3
    oa = (_exact_dot(gates, (src == head3).astype(BF16)) * ocmp_ref[0].T
          + _exact_dot(gates, (src == head3 + 1).astype(BF16)) * oslc_ref[0].T
          + _exact_dot(gates, (src == head3 + 2).astype(BF16)) * owin_ref[0].T)
    cu = cvc_ref[...] * cvu_ref[...]
    prev = jnp.where(i % tiles_per_seq == 0, 0.0, pc_ref[...] * pu_ref[...])
    full = jnp.concatenate([prev, cu], axis=0)
    back1 = pltpu.roll(full, 1, 0)[8:, :]
    back2 = pltpu.roll(full, 2, 0)[8:, :]
    cw = cw_ref[...]
    ob = cvb_ref[...] * (cw[0:1, :] * back2 + cw[1:2, :] * back1 + cw[2:3, :] * cu)
    groups = jnp.concatenate([oa, ob, oc_ref[0].T, od_ref[0].T], axis=1)
    n_groups = d // HEAD_DIM
    gather = ((lax.broadcasted_iota(jnp.int32, (d, LANES), 0) >> 6)
              == lax.broadcasted_iota(jnp.int32, (d, LANES), 1)).astype(BF16)
    spread = (lax.broadcasted_iota(jnp.int32, (LANES, d), 0)
              == (lax.broadcasted_iota(jnp.int32, (LANES, d), 1) >> 6)).astype(BF16)
    ssq = _exact_dot(groups * groups, gather)
    inv = lax.rsqrt(ssq * (1.0 / HEAD_DIM) + RMS_EPS)
    mixed = (groups * _exact_dot(inv, spread) * gout_ref[...]).astype(BF16)
    assert n_groups <= LANES
    o_ref[...] = x_ref[...] + jnp.dot(mixed, wout_ref[...], preferred_element_type=F32)


def _mixout(x2, p2, ocmp, oslc, owin, oc, od, bg, cw, gout, wout, seq):
    t, d = x2.shape
    tm = 256
    rows = lambda w, col: pl.BlockSpec((tm, w), lambda i: (i, col // w))
    heads_t = lambda tq: pl.BlockSpec((1, GROUP_W, tm), lambda i: (i // (tq // tm), 0, i % (tq // tm)))
    prev8 = lambda col: pl.BlockSpec((8, GROUP_W), lambda i: (jnp.maximum(i * (tm // 8) - 1, 0), col // GROUP_W))
    const = lambda shape: pl.BlockSpec(shape, lambda i: (0, 0))
    return pl.pallas_call(
        functools.partial(_mixout_kernel, tiles_per_seq=seq // tm),
        grid=(t // tm,),
        in_specs=[rows(d, 0), heads_t(TQ), heads_t(TQ), heads_t(TQ),
                  rows(LANES, COL_GATE), rows(GROUP_W, COL_CVB), rows(GROUP_W, COL_CVC), rows(GROUP_W, COL_CVU),
                  prev8(COL_CVC), prev8(COL_CVU),
                  heads_t(STICK_TQ), heads_t(TQ),
                  const((1, LANES)), const((8, GROUP_W)), const((1, d)), const((d, d))],
        out_specs=rows(d, 0),
        out_shape=jax.ShapeDtypeStruct((t, d), F32),
        compiler_params=pltpu.CompilerParams(dimension_semantics=("parallel",), vmem_limit_bytes=VMEM_LIMIT),
        name="mixout",
    )(x2, ocmp, oslc, owin, p2, p2, p2, p2, p2, p2, oc, od, bg, cw, gout, wout)


def _ffn_kernel(x_ref, g_ref, wg_ref, wu_ref, wd_ref, o_ref, act_s, *, ff_chunk, out_chunk):
    h = _rms(x_ref[...], g_ref[...]).astype(BF16)
    dff = wg_ref.shape[1]
    for c in range(dff // ff_chunk):
        cols = slice(c * ff_chunk, (c + 1) * ff_chunk)
        a = jnp.dot(h, wg_ref[:, cols], preferred_element_type=F32)
        u = jnp.dot(h, wu_ref[:, cols], preferred_element_type=F32)
        act_s[:, cols] = (a * jax.nn.sigmoid(a) * u).astype(BF16)
    d = o_ref.shape[1]
    for c in range(d // out_chunk):
        cols = slice(c * out_chunk, (c + 1) * out_chunk)
        o_ref[:, cols] = x_ref[:, cols] + jnp.dot(act_s[...], wd_ref[:, cols], preferred_element_type=F32)


def _ffn(x2, g, wg, wu, wd):
    t, d = x2.shape
    dff = wg.shape[1]
    tm = 512
    resident = lambda shape: pl.BlockSpec(shape, lambda i: (0, 0), pipeline_mode=pl.Buffered(1))
    return pl.pallas_call(
        functools.partial(_ffn_kernel, ff_chunk=256, out_chunk=256),
        grid=(t // tm,),
        in_specs=[pl.BlockSpec((tm, d), lambda i: (i, 0)),
                  pl.BlockSpec((1, d), lambda i: (0, 0)),
                  resident((d, dff)), resident((d, dff)), resident((dff, d))],
        out_specs=pl.BlockSpec((tm, d), lambda i: (i, 0)),
        out_shape=jax.ShapeDtypeStruct((t, d), F32),
        scratch_shapes=[pltpu.VMEM((tm, dff), BF16)],
        compiler_params=pltpu.CompilerParams(dimension_semantics=("parallel",), vmem_limit_bytes=VMEM_LIMIT),
        name="ffn",
    )(x2, g, wg, wu, wd)


def _permute_w_in_kernel(w_ref, o_ref):
    rows = w_ref.shape[0]
    n_rest = P_COLS - COL_CVB
    o_ref[:, 0:COL_GATE] = w_ref[:, 0:COL_GATE].astype(BF16)
    lane = lax.broadcasted_iota(jnp.int32, (rows, LANES), 1)
    o_ref[:, COL_GATE:COL_GATE + LANES] = jnp.where(lane < N_GATES, w_ref[:, COL_GATE:COL_GATE + LANES], 0.0).astype(BF16)
    o_ref[:, COL_CVB:P_COLS] = w_ref[:, COL_GATE + N_GATES:COL_GATE + N_GATES + n_rest].astype(BF16)


def _permute_w_in(w_in):
    d, cols = w_in.shape
    tr = 256
    return pl.pallas_call(
        _permute_w_in_kernel,
        grid=(d // tr,),
        in_specs=[pl.BlockSpec((tr, cols), lambda i: (i, 0))],
        out_specs=pl.BlockSpec((tr, P_COLS), lambda i: (i, 0)),
        out_shape=jax.ShapeDtypeStruct((d, P_COLS), BF16),
        compiler_params=pltpu.CompilerParams(dimension_semantics=("parallel",), vmem_limit_bytes=VMEM_LIMIT),
        name="permute_w_in",
    )(w_in)


def _layer(x2, batch, seq, g_mix, w_in, b_gate, g_q_nsa, g_k_cmp, g_k_slc, g_k_win, pe_k_cmp, pe_v_cmp,
           w1_k_cmp, w2_k_cmp, w1_v_cmp, w2_v_cmp, conv_w, g_q_dil, g_k_dil, g_out, w_out,
           g_ffn, w_gate, w_up, w_down):
    row = lambda v: v.reshape(1, -1)
    per_group = lambda g, n: jnp.tile(g, n).reshape(1, -1)
    p2 = _inproj(x2, row(g_mix), _permute_w_in(w_in))

    pe, w1, w2 = _compress_weights(pe_k_cmp, pe_v_cmp, w1_k_cmp, w2_k_cmp, w1_v_cmp, w2_v_cmp)
    kc, vc = _compress(p2, pe, w1, w2, per_group(g_k_cmp, 2), batch, seq)
    ocmp, oslc = _nsa(p2, kc, vc, per_group(g_q_nsa, N_HEADS), per_group(g_k_slc, 2), batch, seq)
    owin = _banded(p2, per_group(g_q_nsa, N_HEADS), per_group(g_k_win, 2), jnp.asarray(_band_bias_tiles(NSA_WINDOW - 1, _window_mult)),
                   batch, seq, window=NSA_WINDOW - 1, slopes=SLOPES_A, shared_kv=True,
                   col_q=COL_QA, col_k=COL_KWV, col_v=COL_KWV, name="nsa_window")
    od = _banded(p2, per_group(g_q_dil, N_HEADS), per_group(g_k_dil, N_HEADS),
                 jnp.asarray(_band_bias_tiles(DILATED_CONFIGS[-1][0], _dilated_mult)),
                 batch, seq, window=DILATED_CONFIGS[-1][0], slopes=SLOPES_D, shared_kv=False,
                 col_q=COL_QD, col_k=COL_KD, col_v=COL_VD, name="dilated")
    oc = _stick(p2, batch, seq)

    bg = jnp.zeros((1, LANES), F32).at[0, :N_GATES].set(b_gate)
    cwp = jnp.zeros((8, GROUP_W), F32).at[:CONV_K].set(conv_w)
    x1 = _mixout(x2, p2, ocmp, oslc, owin, oc, od, bg, cwp, row(g_out), w_out.astype(BF16), seq)
    return _ffn(x1, row(g_ffn), w_gate.astype(BF16), w_up.astype(BF16), w_down.astype(BF16))


def kernel(x, g_mix, w_in, b_gate, g_q_nsa, g_k_cmp, g_k_slc, g_k_win, pe_k_cmp, pe_v_cmp, w1_k_cmp, w2_k_cmp,
           w1_v_cmp, w2_v_cmp, conv_w, g_q_dil, g_k_dil, g_out, w_out, g_ffn, w_gate, w_up, w_down):
    batch, seq, d = x.shape
    assert seq % (CMP_STRIDE * LANES) == 0 and d % LANES == 0
    x2 = x.reshape(batch * seq, d)
    params = (g_mix, w_in, b_gate, g_q_nsa, g_k_cmp, g_k_slc, g_k_win, pe_k_cmp, pe_v_cmp, w1_k_cmp, w2_k_cmp,
              w1_v_cmp, w2_v_cmp, conv_w, g_q_dil, g_k_dil, g_out, w_out, g_ffn, w_gate, w_up, w_down)
    for layer in range(g_mix.shape[0]):
        x2 = _layer(x2, batch, seq, *[p[layer] for p in params])
    return x2.reshape(batch, seq, d)
```

```python
import functools

import numpy as np
import jax
import jax.numpy as jnp
from jax import lax
from jax.experimental import pallas as pl
from jax.experimental.pallas import tpu as pltpu

F32 = jnp.float32
BF16 = jnp.bfloat16

HEAD_DIM = 64
N_HEADS = 4
GROUP_W = N_HEADS * HEAD_DIM
CONV_K = 3
CMP_LEN = 32
CMP_STRIDE = 16
SEL_BLOCK = 64
N_SELECT = 16
NSA_WINDOW = 512
DILATED_CONFIGS = ((128, 1), (512, 4), (2048, 16))
NEG_INF = -1e30
FORCE_SCORE = 1e6
RMS_EPS = 1e-6

TQ = 512
TK = 256
N_TOP = TQ // TK
STICK_TQ = 256
LANES = 128
MXU_DEPTH = 256
MXU_AHEAD = 2
FLASH_AHEAD = 2
SUM_PAD = 16
LOG2E = 1.4426950408889634
VMEM_LIMIT = 52 * 1024 * 1024

COL_QA, COL_KVC, COL_KSV, COL_KWV, COL_GATE = 0, 256, 384, 512, 640
COL_CVB, COL_CVC, COL_CVU = 768, 1024, 1280
COL_QC, COL_KC, COL_VC = 1536, 1792, 2048
COL_QD, COL_KD, COL_VD = 2304, 2560, 2816
P_COLS = 3072
N_GATES = 12

_NT = (((1,), (1,)), ((), ()))


def _alibi_slopes():
    s = [2.0 ** (-8.0 * i / 8) for i in range(1, 9)]
    return tuple(s[0::2]), tuple(s[1::2])


SLOPES_A, SLOPES_D = _alibi_slopes()


def _rms(x, g):
    return x * lax.rsqrt(jnp.mean(x * x, axis=-1, keepdims=True) + RMS_EPS) * g


def _exact_dot(a, sel):
    hi = a.astype(BF16)
    lo = (a - hi.astype(F32)).astype(BF16)
    if 2 * a.shape[1] <= MXU_DEPTH:
        return jnp.dot(jnp.concatenate([hi, lo], axis=1), jnp.concatenate([sel, sel], axis=0),
                       preferred_element_type=F32)
    return jnp.dot(hi, sel, preferred_element_type=F32) + jnp.dot(lo, sel, preferred_element_type=F32)


def _group_rms(x, g):
    w = x.shape[1]
    same = ((lax.broadcasted_iota(jnp.int32, (w, w), 0) >> 6)
            == (lax.broadcasted_iota(jnp.int32, (w, w), 1) >> 6)).astype(BF16)
    ssq = _exact_dot(x * x, same)
    return x * lax.rsqrt(ssq * (1.0 / HEAD_DIM) + RMS_EPS) * g


EVEN_AUG, ODD_AUG = HEAD_DIM, 0


def _key_aug(pos, first):
    lane = lax.broadcasted_iota(jnp.int32, (pos.shape[0], LANES), 1) - first
    hi = (pos >> 6).astype(F32)
    lo = (pos & 63).astype(F32)
    return jnp.where(lane == 0, hi, jnp.where(lane == 1, lo, jnp.where((lane == 2) | (lane == 3), 1.0, 0.0)))


def _query_aug(t, slope, first):
    lane = lax.broadcasted_iota(jnp.int32, (t.shape[0], LANES), 1) - first
    hi = (t >> 6).astype(F32) * (-64.0 * slope)
    lo = (t & 63).astype(F32) * (-slope)
    return jnp.where(lane == 0, 64.0 * slope,
                     jnp.where(lane == 1, slope, jnp.where(lane == 2, hi, jnp.where(lane == 3, lo, 0.0))))


def _pair_keys(pair, pos):
    lane = lax.broadcasted_iota(jnp.int32, pair.shape, 1)
    aug_e = 0.0 if pos is None else _key_aug(pos, EVEN_AUG)
    aug_o = 0.0 if pos is None else _key_aug(pos, ODD_AUG)
    return (jnp.where(lane < HEAD_DIM, pair, aug_e).astype(BF16),
            jnp.where(lane >= HEAD_DIM, pair, aug_o).astype(BF16))


def _prep_queries(q, g, t0, slopes, qp_s):
    qn = (q if g is None else _group_rms(q, g)) * (HEAD_DIM ** -0.5)
    tpos = t0 + lax.broadcasted_iota(jnp.int32, (q.shape[0], 1), 0)
    lane = lax.broadcasted_iota(jnp.int32, (q.shape[0], LANES), 1)
    for i in range(N_HEADS // 2):
        pair = qn[:, i * LANES:(i + 1) * LANES]
        if slopes is None:
            qp_s[i] = pair.astype(BF16)
            continue
        even = jnp.where((lane >= EVEN_AUG) & (lane < EVEN_AUG + 4), _query_aug(tpos, slopes[2 * i], EVEN_AUG), pair)
        odd = jnp.where(lane < ODD_AUG + 4, _query_aug(tpos, slopes[2 * i + 1], ODD_AUG), pair)
        qp_s[2 * i] = even.astype(BF16)
        qp_s[2 * i + 1] = odd.astype(BF16)


def _inproj_kernel(x_ref, g_ref, w_ref, o_ref, *, nchunk):
    h = _rms(x_ref[...], g_ref[...]).astype(BF16)
    cw = P_COLS // nchunk
    for c in range(nchunk):
        o_ref[:, c * cw:(c + 1) * cw] = jnp.dot(h, w_ref[:, c * cw:(c + 1) * cw], preferred_element_type=F32)


def _inproj(x2, g, w):
    t, d = x2.shape
    tm = 512
    return pl.pallas_call(
        functools.partial(_inproj_kernel, nchunk=6),
        grid=(t // tm,),
        in_specs=[pl.BlockSpec((tm, d), lambda i: (i, 0)),
                  pl.BlockSpec((1, d), lambda i: (0, 0)),
                  pl.BlockSpec((d, P_COLS), lambda i: (0, 0))],
        out_specs=pl.BlockSpec((tm, P_COLS), lambda i: (i, 0)),
        out_shape=jax.ShapeDtypeStruct((t, P_COLS), F32),
        compiler_params=pltpu.CompilerParams(dimension_semantics=("parallel",), vmem_limit_bytes=VMEM_LIMIT),
        name="inproj",
    )(x2, g, w)


def _gelu_tanh(x):
    return x * (0.5 * (1.0 + jnp.tanh(np.sqrt(2.0 / np.pi).astype(np.float32) * (x + 0.044715 * (x * x * x)))))


def _compress_kernel(kv_ref, pe_ref, w1_ref, w2_ref, gk_ref, kc_ref, vct_ref):
    nc = kv_ref.shape[0] // CMP_STRIDE
    first = None
    second = None
    for j in range(CMP_STRIDE):
        tok = kv_ref[pl.ds(j, nc, stride=CMP_STRIDE), :]
        a = jnp.dot((tok + pe_ref[j:j + 1, :]).astype(BF16), w1_ref[j], preferred_element_type=F32)
        b = jnp.dot((tok + pe_ref[CMP_STRIDE + j:CMP_STRIDE + j + 1, :]).astype(BF16), w1_ref[CMP_STRIDE + j],
                    preferred_element_type=F32)
        first = a if first is None else first + a
        second = b if second is None else second + b
    hid = first + pltpu.roll(second, nc - 1, 0)
    out = jnp.dot(_gelu_tanh(hid).astype(BF16), w2_ref[...], preferred_element_type=F32)
    kn = _group_rms(out, gk_ref[...])
    end = lax.broadcasted_iota(jnp.int32, (nc, 1), 0) * CMP_STRIDE + (CMP_LEN - 1)
    even, _ = _pair_keys(kn, end)
    _, odd = _pair_keys(pltpu.roll(kn, HEAD_DIM, 1), end)
    kc_ref[0] = jnp.concatenate([even, odd], axis=1)
    vct_ref[0] = out.T[HEAD_DIM:, :].astype(BF16)


def _compress(p2, pe, w1, w2, gk, batch, seq):
    nc = seq // CMP_STRIDE
    hid2 = w1.shape[2]
    return pl.pallas_call(
        _compress_kernel,
        grid=(batch,),
        in_specs=[pl.BlockSpec((seq, LANES), lambda i: (i, COL_KVC // LANES)),
                  pl.BlockSpec((CMP_LEN, LANES), lambda i: (0, 0)),
                  pl.BlockSpec((CMP_LEN, LANES, hid2), lambda i: (0, 0, 0)),
                  pl.BlockSpec((hid2, LANES), lambda i: (0, 0)),
                  pl.BlockSpec((1, LANES), lambda i: (0, 0))],
        out_specs=[pl.BlockSpec((1, nc, 2 * LANES), lambda i: (i, 0, 0)),
                   pl.BlockSpec((1, HEAD_DIM, nc), lambda i: (i, 0, 0))],
        out_shape=[jax.ShapeDtypeStruct((batch, nc, 2 * LANES), BF16),
                   jax.ShapeDtypeStruct((batch, HEAD_DIM, nc), BF16)],
        compiler_params=pltpu.CompilerParams(dimension_semantics=("parallel",), vmem_limit_bytes=VMEM_LIMIT),
        name="nsa_compress",
    )(p2, pe, w1, w2, gk)


def _compress_weights(pe_k, pe_v, w1_k, w2_k, w1_v, w2_v):
    hid = w1_k.shape[1]
    w1k = w1_k.reshape(CMP_LEN, HEAD_DIM, hid)
    w1v = w1_v.reshape(CMP_LEN, HEAD_DIM, hid)
    z1 = jnp.zeros_like(w1k)
    w1 = jnp.concatenate([jnp.concatenate([w1k, z1], axis=2), jnp.concatenate([z1, w1v], axis=2)], axis=1)
    z2 = jnp.zeros_like(w2_k)
    w2 = jnp.concatenate([jnp.concatenate([w2_k, z2], axis=1), jnp.concatenate([z2, w2_v], axis=1)], axis=0)
    pe = jnp.concatenate([pe_k, pe_v], axis=1)
    return pe, w1.astype(BF16), w2.astype(BF16)


def _flash_scores(kp_of, bias, qp_s, s_s, m_cur):
    m_next = []
    for h in range(N_HEADS):
        s = lax.dot_general(kp_of(h), qp_s[h], _NT, preferred_element_type=F32) + bias
        s_s[h] = s
        m_next.append(jnp.maximum(m_cur[h], jnp.max(s, axis=0, keepdims=True)))
    return tuple(m_next)


def _flash_accum(vt_of, s_s, acc_s, m_prev, m_cur, ls):
    new_l = []
    for h in range(N_HEADS):
        alpha = jnp.exp(m_prev[h] - m_cur[h])
        p = jnp.exp(s_s[h] - m_cur[h])
        new_l.append(alpha * ls[h] + jnp.sum(p, axis=0, keepdims=True))
        acc_s[h] = alpha * acc_s[h] + jnp.dot(vt_of(h), p.astype(BF16), preferred_element_type=F32)
    return tuple(new_l)


def _flash_accum_and_scores(kp_of, bias, vt_of, qp_s, s_s, acc_s, m_prev, m_cur, ls):
    def scores(h):
        return lax.dot_general(kp_of(h), qp_s[h], _NT, preferred_element_type=F32) + bias

    m_next, new_l = [], []
    s_new = {h: scores(h) for h in range(FLASH_AHEAD)}
    for h in range(N_HEADS):
        alpha = jnp.exp(m_prev[h] - m_cur[h])
        p = jnp.exp(s_s[h] - m_cur[h])
        new_l.append(alpha * ls[h] + jnp.sum(p, axis=0, keepdims=True))
        acc_s[h] = alpha * acc_s[h] + jnp.dot(vt_of(h), p.astype(BF16), preferred_element_type=F32)
        s_h = s_new.pop(h)
        s_s[h] = s_h
        m_next.append(jnp.maximum(m_cur[h], jnp.max(s_h, axis=0, keepdims=True)))
        if h + FLASH_AHEAD < N_HEADS:
            s_new[h + FLASH_AHEAD] = scores(h + FLASH_AHEAD)
    return tuple(m_next), tuple(new_l)


def _flash_pipelined(tile_of, n_tiles, kp_fn, bias_fn, vt_fn, qp_s, s_s, acc_s, o_ref):
    acc_s[...] = jnp.zeros(acc_s.shape, F32)
    m0 = tuple(jnp.full((1, TQ), NEG_INF, F32) for _ in range(N_HEADS))
    l0 = tuple(jnp.zeros((1, TQ), F32) for _ in range(N_HEADS))
    first_kt = tile_of(0)
    m1 = _flash_scores(kp_fn(first_kt), bias_fn(first_kt, True), qp_s, s_s, m0)

    def step(i, carry, top):
        m_prev, m_cur, ls = carry
        nxt = tile_of(i + 1)
        m_next, ls = _flash_accum_and_scores(kp_fn(nxt), bias_fn(nxt, top), vt_fn(tile_of(i)), qp_s, s_s, acc_s,
                                             m_prev, m_cur, ls)
        return m_cur, m_next, ls

    carry = (m0, m1, l0)
    for i in range(N_TOP - 1):
        carry = step(i, carry, True)
    m_prev, m_cur, ls = lax.fori_loop(N_TOP - 1, n_tiles - 1, lambda i, c: step(i, c, False), carry)
    ls = _flash_accum(vt_fn(tile_of(n_tiles - 1)), s_s, acc_s, m_prev, m_cur, ls)
    o_ref[0] = jnp.concatenate([acc_s[h] * (1.0 / ls[h]) for h in range(N_HEADS)], axis=0)


def _nsa_kernel(q_ref, kc_ref, vct_ref, ksv_ref, gq_ref, gks_ref, ocmp_ref, oslc_ref,
                ksp_s, vst_s, qp_s, imp_s, sel_s, s_s, acc_s, hit_s, *, seq):
    qi = pl.program_id(1)
    nc = seq // CMP_STRIDE
    nsel = seq // SEL_BLOCK
    nkt = seq // TK

    @pl.when(qi == 0)
    def _prep():
        for c in range(nkt):
            rows = slice(c * TK, (c + 1) * TK)
            blk = ksv_ref[rows, :]
            pos = c * TK + lax.broadcasted_iota(jnp.int32, (TK, 1), 0)
            kn = _group_rms(blk, gks_ref[...])
            ksp_s[rows, 0:LANES], _ = _pair_keys(kn, pos)
            _, ksp_s[rows, LANES:2 * LANES] = _pair_keys(pltpu.roll(kn, HEAD_DIM, 1), pos)
            vst_s[c] = blk.T[HEAD_DIM:, :].astype(BF16)
        imp_s[:, 0:8, :] = jnp.zeros((TQ // LANES, 8, LANES), F32)

    t0 = qi * TQ
    _prep_queries(q_ref[...], gq_ref[...], t0, SLOPES_A, qp_s)

    kc = [kc_ref[0, :, 0:LANES], kc_ref[0, :, LANES:2 * LANES]]
    vct = vct_ref[0]
    tq_row = t0 + lax.broadcasted_iota(jnp.int32, (nc, TQ), 1)
    n_idx = lax.broadcasted_iota(jnp.int32, (nc, TQ), 0)
    vis = (tq_row >= n_idx * CMP_STRIDE + (CMP_LEN - 1)) & (n_idx < nc - 1)
    vis_bias = jnp.where(vis, 0.0, NEG_INF)
    sees_any = (t0 + lax.broadcasted_iota(jnp.int32, (1, TQ), 1) >= CMP_LEN - 1).astype(F32)
    imp = jnp.zeros((nc, TQ), F32)
    ocmp_t = []
    scores = [lax.dot_general(kc[h % 2], qp_s[h], _NT, preferred_element_type=F32) + vis_bias
              for h in range(N_HEADS)]
    for h in range(N_HEADS):
        sc = scores[h]
        e = jnp.exp(sc - jnp.max(sc, axis=0, keepdims=True))
        p = e * (sees_any / jnp.sum(e, axis=0, keepdims=True))
        ocmp_t.append(jnp.dot(vct, p.astype(BF16), preferred_element_type=F32))
        imp = imp + p
    ocmp_ref[0] = jnp.concatenate(ocmp_t, axis=0)

    halves = []
    for half in range(TQ // LANES):
        imp_s[half, 8:8 + nc, :] = imp[:, half * LANES:(half + 1) * LANES]
        r = [imp_s[half, pl.ds(8 + k, nsel, stride=4), :] for k in range(4)]
        rm1 = imp_s[half, pl.ds(7, nsel, stride=4), :]
        halves.append(rm1 + 2.0 * (r[0] + r[1] + r[2]) + r[3])
    imp_blk = jnp.concatenate(halves, axis=1)
    blk = lax.broadcasted_iota(jnp.int32, (nsel, TQ), 0)
    tl = t0 + lax.broadcasted_iota(jnp.int32, (nsel, TQ), 1)
    cur = tl >> 6
    forced = (blk == 0) | (blk == cur) | (blk == cur - 1)
    valid = blk * SEL_BLOCK <= tl
    score = jnp.where(forced, FORCE_SCORE, jnp.where(valid, imp_blk, -FORCE_SCORE))
    sub = 8
    per_tile = TK // SEL_BLOCK
    tile_hit = []
    for g in range(nsel // sub):
        mine = score[g * sub:(g + 1) * sub, :]
        blk_g = g * sub + lax.broadcasted_iota(jnp.int32, (sub, TQ), 0)
        rank = jnp.zeros((sub, TQ), F32)
        for j in range(nsel):
            row = score[j:j + 1, :]
            if j < g * sub:
                beats = row >= mine
            elif j >= (g + 1) * sub:
                beats = row > mine
            else:
                beats = (row > mine) | ((row == mine) & (blk_g > j))
            rank = rank + jnp.where(beats, 1.0, 0.0)
        sel_g = jnp.where(rank < min(N_SELECT, nsel), 0.0, NEG_INF)
        sel_s[g * sub:(g + 1) * sub, :] = sel_g
        for part in range(sub // per_tile):
            tile_hit.append(jnp.max(sel_g[part * per_tile:(part + 1) * per_tile, :]))

    n_hit = jnp.int32(0)
    for kt in reversed(range(nkt)):
        hit_s[n_hit] = kt
        n_hit = n_hit + ((tile_hit[kt] > -1.0) & (kt * TK < t0)).astype(jnp.int32)

    def block_bias(kt):
        return jnp.concatenate([jnp.broadcast_to(sel_s[pl.ds(kt * per_tile + j, 1), :], (SEL_BLOCK, TQ))
                                for j in range(per_tile)], axis=0)

    k_local = lax.broadcasted_iota(jnp.int32, (TK, TQ), 0)
    q_local = lax.broadcasted_iota(jnp.int32, (TK, TQ), 1)
    top_kt = qi * N_TOP + (N_TOP - 1)

    def kp_fn(kt):
        off = pl.multiple_of(kt * TK, TK)
        return lambda h: ksp_s[pl.ds(off, TK), (h % 2) * LANES:(h % 2 + 1) * LANES]

    def bias_fn(kt, top):
        if top:
            return jnp.where(kt * TK - t0 + k_local <= q_local, block_bias(kt), NEG_INF)
        return block_bias(kt)

    tile_of = lambda i: jnp.where(i < N_TOP, top_kt - i, hit_s[jnp.maximum(i - N_TOP, 0)])
    _flash_pipelined(tile_of, n_hit + N_TOP, kp_fn, bias_fn, lambda kt: (lambda h: vst_s[kt]), qp_s, s_s, acc_s,
                     oslc_ref)


def _nsa(p2, kc, vc, gq, gks, batch, seq):
    nq = seq // TQ
    nc = seq // CMP_STRIDE
    t = batch * seq
    return pl.pallas_call(
        functools.partial(_nsa_kernel, seq=seq),
        grid=(batch, nq),
        in_specs=[pl.BlockSpec((TQ, GROUP_W), lambda b, i: (b * nq + i, COL_QA // GROUP_W)),
                  pl.BlockSpec((1, nc, 2 * LANES), lambda b, i: (b, 0, 0)),
                  pl.BlockSpec((1, HEAD_DIM, nc), lambda b, i: (b, 0, 0)),
                  pl.BlockSpec((seq, LANES), lambda b, i: (b, COL_KSV // LANES)),
                  pl.BlockSpec((1, GROUP_W), lambda b, i: (0, 0)),
                  pl.BlockSpec((1, LANES), lambda b, i: (0, 0))],
        out_specs=[pl.BlockSpec((1, GROUP_W, TQ), lambda b, i: (b * nq + i, 0, 0)),
                   pl.BlockSpec((1, GROUP_W, TQ), lambda b, i: (b * nq + i, 0, 0))],
        out_shape=[jax.ShapeDtypeStruct((t // TQ, GROUP_W, TQ), F32),
                   jax.ShapeDtypeStruct((t // TQ, GROUP_W, TQ), F32)],
        scratch_shapes=[pltpu.VMEM((seq, 2 * LANES), BF16),
                        pltpu.VMEM((seq // TK, HEAD_DIM, TK), BF16),
                        pltpu.VMEM((N_HEADS, TQ, LANES), BF16),
                        pltpu.VMEM((TQ // LANES, 8 + nc, LANES), F32),
                        pltpu.VMEM((seq // SEL_BLOCK, TQ), F32),
                        pltpu.VMEM((N_HEADS, TK, TQ), F32),
                        pltpu.VMEM((N_HEADS, HEAD_DIM, TQ), F32),
                        pltpu.SMEM((seq // TK + 1,), jnp.int32)],
        compiler_params=pltpu.CompilerParams(dimension_semantics=("parallel", "arbitrary"),
                                             vmem_limit_bytes=VMEM_LIMIT),
        name="nsa_cmp_slc",
    )(p2, kc, vc, p2, gq, gks)


def _band_bias_tiles(window, mult_fn):
    nd = (window + TQ - 1) // TK + 1
    kk = np.arange(TK)[:, None]
    qq = np.arange(TQ)[None, :]
    tiles = np.empty((nd, TK, TQ), np.float32)
    for di in range(nd):
        d = (di - (N_TOP - 1)) * TK + qq - kk
        mult = mult_fn(d)
        tiles[di] = np.where(mult > 0, np.log(np.maximum(mult, 1)), NEG_INF)
    return tiles


def _window_mult(d):
    return ((d >= 0) & (d <= NSA_WINDOW - 1)).astype(np.float64)


def _dilated_mult(d):
    m = np.zeros(d.shape, np.float64)
    for window, dil in DILATED_CONFIGS:
        m += ((d >= 0) & (d <= window) & (d % dil == 0)).astype(np.float64)
    return m


def _banded_kernel(*refs, seq, window, slopes, shared_kv):
    if shared_kv:
        q_ref, kv_ref, gq_ref, gk_ref, bias_ref, o_ref, kp_s, vt_s, qp_s, s_s, acc_s = refs
    else:
        q_ref, k_ref, v_ref, gq_ref, gk_ref, bias_ref, o_ref, kp_s, vt_s, qp_s, s_s, acc_s = refs
    qi = pl.program_id(1)
    nkt = seq // TK

    @pl.when(qi == 0)
    def _prep():
        for c in range(nkt):
            rows = slice(c * TK, (c + 1) * TK)
            pos = c * TK + lax.broadcasted_iota(jnp.int32, (TK, 1), 0)
            if shared_kv:
                blk = kv_ref[rows, :]
                kn = _group_rms(blk, gk_ref[...])
                kp_s[rows, 0:LANES], _ = _pair_keys(kn, pos)
                _, kp_s[rows, LANES:2 * LANES] = _pair_keys(pltpu.roll(kn, HEAD_DIM, 1), pos)
                vt_s[c] = blk.T[HEAD_DIM:, :].astype(BF16)
            else:
                kn = _group_rms(k_ref[rows, :], gk_ref[...])
                for i in range(N_HEADS // 2):
                    even, odd = _pair_keys(kn[:, i * LANES:(i + 1) * LANES], pos)
                    kp_s[rows, 2 * i * LANES:(2 * i + 1) * LANES] = even
                    kp_s[rows, (2 * i + 1) * LANES:(2 * i + 2) * LANES] = odd
                vt_s[c] = v_ref[rows, :].T.astype(BF16)

    t0 = qi * TQ
    _prep_queries(q_ref[...], gq_ref[...], t0, slopes, qp_s)
    kt_lo = jnp.maximum(t0 - window, 0) // TK

    def kp_fn(kt):
        off = pl.multiple_of(kt * TK, TK)
        if shared_kv:
            return lambda h: kp_s[pl.ds(off, TK), (h % 2) * LANES:(h % 2 + 1) * LANES]
        return lambda h: kp_s[pl.ds(off, TK), h * LANES:(h + 1) * LANES]

    def vt_fn(kt):
        if shared_kv:
            return lambda h: vt_s[kt]
        return lambda h: vt_s[kt, h * HEAD_DIM:(h + 1) * HEAD_DIM, :]

    top_kt = qi * N_TOP + (N_TOP - 1)
    _flash_pipelined(lambda i: top_kt - i, top_kt + 1 - kt_lo, kp_fn, lambda kt, top: bias_ref[top_kt - kt], vt_fn,
                     qp_s, s_s, acc_s, o_ref)


def _banded(p2, gq, gk, bias_tiles, batch, seq, *, window, slopes, shared_kv, col_q, col_k, col_v, name):
    nq = seq // TQ
    t = batch * seq
    nd = bias_tiles.shape[0]
    n_kv = 1 if shared_kv else N_HEADS
    q_spec = pl.BlockSpec((TQ, GROUP_W), lambda b, i: (b * nq + i, col_q // GROUP_W))
    gain = lambda g: pl.BlockSpec(g.shape, lambda b, i: (0, 0))
    bias_spec = pl.BlockSpec((nd, TK, TQ), lambda b, i: (0, 0, 0))
    if shared_kv:
        kv_specs = [pl.BlockSpec((seq, LANES), lambda b, i: (b, col_k // LANES))]
        operands = (p2, p2, gq, gk, bias_tiles)
    else:
        kv_specs = [pl.BlockSpec((seq, GROUP_W), lambda b, i: (b, col_k // GROUP_W)),
                    pl.BlockSpec((seq, GROUP_W), lambda b, i: (b, col_v // GROUP_W))]
        operands = (p2, p2, p2, gq, gk, bias_tiles)
    return pl.pallas_call(
        functools.partial(_banded_kernel, seq=seq, window=window, slopes=slopes, shared_kv=shared_kv),
        grid=(batch, nq),
        in_specs=[q_spec] + kv_specs + [gain(gq), gain(gk), bias_spec],
        out_specs=pl.BlockSpec((1, GROUP_W, TQ), lambda b, i: (b * nq + i, 0, 0)),
        out_shape=jax.ShapeDtypeStruct((t // TQ, GROUP_W, TQ), F32),
        scratch_shapes=[pltpu.VMEM((seq, (2 if shared_kv else N_HEADS) * LANES), BF16),
                        pltpu.VMEM((seq // TK, n_kv * HEAD_DIM, TK), BF16),
                        pltpu.VMEM((N_HEADS, TQ, LANES), BF16),
                        pltpu.VMEM((N_HEADS, TK, TQ), F32),
                        pltpu.VMEM((N_HEADS, HEAD_DIM, TQ), F32)],
        compiler_params=pltpu.CompilerParams(dimension_semantics=("parallel", "arbitrary"),
                                             vmem_limit_bytes=VMEM_LIMIT),
        name=name,
    )(*operands)


def _stick_kernel(q_ref, k_ref, v_ref, o_ref, kp_s, vt_s, qp_s, sp_s, e_s, acc_s, *, seq):
    TQ, N_TOP = STICK_TQ, STICK_TQ // TK
    qi = pl.program_id(1)
    nkt = seq // TK

    @pl.when(qi == 0)
    def _prep():
        for c in range(nkt):
            rows = slice(c * TK, (c + 1) * TK)
            kb = k_ref[rows, :]
            for i in range(N_HEADS // 2):
                even, odd = _pair_keys(kb[:, i * LANES:(i + 1) * LANES], None)
                kp_s[rows, 2 * i * LANES:(2 * i + 1) * LANES] = even
                kp_s[rows, (2 * i + 1) * LANES:(2 * i + 2) * LANES] = odd
            vt_s[c] = v_ref[rows, :].T.astype(BF16)

    _prep_queries(q_ref[...], None, qi * TQ, None, qp_s)
    half = TK // 2
    col = lax.broadcasted_iota(jnp.int32, (half + SUM_PAD, half), 1)
    srow = lax.broadcasted_iota(jnp.int32, (half + SUM_PAD, half), 0)
    sums = ((srow == half) | ((srow < half) & (col > srow))).astype(BF16)
    k_local = lax.broadcasted_iota(jnp.int32, (TK, TQ), 0)
    q_local = lax.broadcasted_iota(jnp.int32, (TK, TQ), 1)
    top_kt = qi * N_TOP + (N_TOP - 1)
    acc_s[...] = jnp.zeros(acc_s.shape, F32)

    def past_mask(tiles_above):
        return tiles_above * TK + k_local < q_local

    def logits(kt, h):
        off = pl.multiple_of(kt * TK, TK)
        return lax.dot_general(kp_s[pl.ds(off, TK), h * LANES:(h + 1) * LANES], qp_s[h // 2], _NT,
                               preferred_element_type=F32)

    def stage(h, z, past):
        sp = jnp.maximum(z, 0.0) + jnp.log(1.0 + jnp.exp2(jnp.abs(z) * (-LOG2E)))
        spm = sp if past is None else jnp.where(past, sp, 0.0)
        sp_s[h] = spm.astype(BF16)
        logsig = z - sp
        e_s[h] = logsig if past is None else jnp.where(past, logsig, NEG_INF)

    def tails(h):
        return [jnp.dot(sums, sp_s[h, b * half:(b + 1) * half, :], preferred_element_type=F32) for b in range(2)]

    def consume(kt, h, w, later):
        later_lo = later + w[1][half:half + 1, :]
        attn = jnp.concatenate([jnp.exp(e_s[h, 0:half, :] - w[0][0:half, :] - later_lo),
                                jnp.exp(e_s[h, half:TK, :] - w[1][0:half, :] - later)], axis=0)
        acc_s[h] += jnp.dot(vt_s[kt, h * HEAD_DIM:(h + 1) * HEAD_DIM, :], attn.astype(BF16),
                            preferred_element_type=F32)
        return later_lo + w[0][half:half + 1, :]

    ahead = MXU_AHEAD
    for h in range(N_HEADS):
        stage(h, logits(top_kt, h), past_mask(N_TOP - 1))

    def step(kt, carry, past):
        laters, w_first, z_first = carry
        w = {0: list(w_first)}
        z = {0: z_first}
        for h in range(1, ahead):
            w[h] = tails(h)
            z[h] = logits(kt - 1, h)
        out = []
        for h in range(N_HEADS):
            out.append(consume(kt, h, w.pop(h), laters[h]))
            stage(h, z.pop(h), past)
            if h + ahead < N_HEADS:
                w[h + ahead] = tails(h + ahead)
                z[h + ahead] = logits(kt - 1, h + ahead)
            elif h + ahead == N_HEADS:
                nxt = (tuple(tails(0)), logits(jnp.maximum(kt - 2, 0), 0))
        return tuple(out), nxt[0], nxt[1]

    carry = (tuple(jnp.zeros((1, TQ), F32) for _ in range(N_HEADS)),
             tuple(tails(0)), logits(jnp.maximum(top_kt - 1, 0), 0))
    for j in range(N_TOP - 1):
        carry = step(top_kt - j, carry, past_mask(N_TOP - 2 - j))
    below = qi * N_TOP
    laters, w_first, _ = lax.fori_loop(0, below, lambda i, c: step(below - i, c, None), carry)
    w = {0: list(w_first)}
    for h in range(1, ahead):
        w[h] = tails(h)
    for h in range(N_HEADS):
        consume(0, h, w.pop(h), laters[h])
        if h + ahead < N_HEADS:
            w[h + ahead] = tails(h + ahead)
    o_ref[0] = acc_s[...].reshape(GROUP_W, TQ)


def _stick(p2, batch, seq):
    TQ = STICK_TQ
    nq = seq // TQ
    t = batch * seq
    return pl.pallas_call(
        functools.partial(_stick_kernel, seq=seq),
        grid=(batch, nq),
        in_specs=[pl.BlockSpec((TQ, GROUP_W), lambda b, i: (b * nq + i, COL_QC // GROUP_W)),
                  pl.BlockSpec((seq, GROUP_W), lambda b, i: (b, COL_KC // GROUP_W)),
                  pl.BlockSpec((seq, GROUP_W), lambda b, i: (b, COL_VC // GROUP_W))],
        out_specs=pl.BlockSpec((1, GROUP_W, TQ), lambda b, i: (b * nq + i, 0, 0)),
        out_shape=jax.ShapeDtypeStruct((t // TQ, GROUP_W, TQ), F32),
        scratch_shapes=[pltpu.VMEM((seq, N_HEADS * LANES), BF16),
                        pltpu.VMEM((seq // TK, GROUP_W, TK), BF16),
                        pltpu.VMEM((N_HEADS // 2, TQ, LANES), BF16),
                        pltpu.VMEM((N_HEADS, TK, TQ), BF16),
                        pltpu.VMEM((N_HEADS, TK, TQ), F32),
                        pltpu.VMEM((N_HEADS, HEAD_DIM, TQ), F32)],
        compiler_params=pltpu.CompilerParams(dimension_semantics=("parallel", "arbitrary"),
                                             vmem_limit_bytes=VMEM_LIMIT),
        name="stick_breaking",
    )(p2, p2, p2)


def _mixout_kernel(x_ref, ocmp_ref, oslc_ref, owin_ref, gate_ref, cvb_ref, cvc_ref, cvu_ref, pc_ref, pu_ref,
                   oc_ref, od_ref, bg_ref, cw_ref, gout_ref, wout_ref, o_ref, *, tiles_per_seq):
    i = pl.program_id(0)
    d = x_ref.shape[1]

    gates = jax.nn.sigmoid(gate_ref[...] + bg_ref[...])
    src = lax.broadcasted_iota(jnp.int32, (LANES, GROUP_W), 0)
    head3 = (lax.broadcasted_iota(jnp.int32, (LANES, GROUP_W), 1) >> 6) * 3
    oa = (_exact_dot(gates, (src == head3).astype(BF16)) * ocmp_ref[0].T
          + _exact_dot(gates, (src == head3 + 1).astype(BF16)) * oslc_ref[0].T
          + _exact_dot(gates, (src == head3 + 2).astype(BF16)) * owin_ref[0].T)
    cu = cvc_ref[...] * cvu_ref[...]
    prev = jnp.where(i % tiles_per_seq == 0, 0.0, pc_ref[...] * pu_ref[...])
    full = jnp.concatenate([prev, cu], axis=0)
    back1 = pltpu.roll(full, 1, 0)[8:, :]
    back2 = pltpu.roll(full, 2, 0)[8:, :]
    cw = cw_ref[...]
    ob = cvb_ref[...] * (cw[0:1, :] * back2 + cw[1:2, :] * back1 + cw[2:3, :] * cu)
    groups = jnp.concatenate([oa, ob, oc_ref[0].T, od_ref[0].T], axis=1)
    n_groups = d // HEAD_DIM
    gather = ((lax.broadcasted_iota(jnp.int32, (d, LANES), 0) >> 6)
              == lax.broadcasted_iota(jnp.int32, (d, LANES), 1)).astype(BF16)
    spread = (lax.broadcasted_iota(jnp.int32, (LANES, d), 0)
              == (lax.broadcasted_iota(jnp.int32, (LANES, d), 1) >> 6)).astype(BF16)
    ssq = _exact_dot(groups * groups, gather)
    inv = lax.rsqrt(ssq * (1.0 / HEAD_DIM) + RMS_EPS)
    mixed = (groups * _exact_dot(inv, spread) * gout_ref[...]).astype(BF16)
    assert n_groups <= LANES
    o_ref[...] = x_ref[...] + jnp.dot(mixed, wout_ref[...], preferred_element_type=F32)


def _mixout(x2, p2, ocmp, oslc, owin, oc, od, bg, cw, gout, wout, seq):
    t, d = x2.shape
    tm = 256
    rows = lambda w, col: pl.BlockSpec((tm, w), lambda i: (i, col // w))
    heads_t = lambda tq: pl.BlockSpec((1, GROUP_W, tm), lambda i: (i // (tq // tm), 0, i % (tq // tm)))
    prev8 = lambda col: pl.BlockSpec((8, GROUP_W), lambda i: (jnp.maximum(i * (tm // 8) - 1, 0), col // GROUP_W))
    const = lambda shape: pl.BlockSpec(shape, lambda i: (0, 0))
    return pl.pallas_call(
        functools.partial(_mixout_kernel, tiles_per_seq=seq // tm),
        grid=(t // tm,),
        in_specs=[rows(d, 0), heads_t(TQ), heads_t(TQ), heads_t(TQ),
                  rows(LANES, COL_GATE), rows(GROUP_W, COL_CVB), rows(GROUP_W, COL_CVC), rows(GROUP_W, COL_CVU),
                  prev8(COL_CVC), prev8(COL_CVU),
                  heads_t(STICK_TQ), heads_t(TQ),
                  const((1, LANES)), const((8, GROUP_W)), const((1, d)), const((d, d))],
        out_specs=rows(d, 0),
        out_shape=jax.ShapeDtypeStruct((t, d), F32),
        compiler_params=pltpu.CompilerParams(dimension_semantics=("parallel",), vmem_limit_bytes=VMEM_LIMIT),
        name="mixout",
    )(x2, ocmp, oslc, owin, p2, p2, p2, p2, p2, p2, oc, od, bg, cw, gout, wout)


def _ffn_kernel(x_ref, g_ref, wg_ref, wu_ref, wd_ref, o_ref, act_s, *, ff_chunk, out_chunk):
    h = _rms(x_ref[...], g_ref[...]).astype(BF16)
    dff = wg_ref.shape[1]
    for c in range(dff // ff_chunk):
        cols = slice(c * ff_chunk, (c + 1) * ff_chunk)
        a = jnp.dot(h, wg_ref[:, cols], preferred_element_type=F32)
        u = jnp.dot(h, wu_ref[:, cols], preferred_element_type=F32)
        act_s[:, cols] = (a * jax.nn.sigmoid(a) * u).astype(BF16)
    d = o_ref.shape[1]
    for c in range(d // out_chunk):
        cols = slice(c * out_chunk, (c + 1) * out_chunk)
        o_ref[:, cols] = x_ref[:, cols] + jnp.dot(act_s[...], wd_ref[:, cols], preferred_element_type=F32)


def _ffn(x2, g, wg, wu, wd):
    t, d = x2.shape
    dff = wg.shape[1]
    tm = 512
    resident = lambda shape: pl.BlockSpec(shape, lambda i: (0, 0), pipeline_mode=pl.Buffered(1))
    return pl.pallas_call(
        functools.partial(_ffn_kernel, ff_chunk=256, out_chunk=256),
        grid=(t // tm,),
        in_specs=[pl.BlockSpec((tm, d), lambda i: (i, 0)),
                  pl.BlockSpec((1, d), lambda i: (0, 0)),
                  resident((d, dff)), resident((d, dff)), resident((dff, d))],
        out_specs=pl.BlockSpec((tm, d), lambda i: (i, 0)),
        out_shape=jax.ShapeDtypeStruct((t, d), F32),
        scratch_shapes=[pltpu.VMEM((tm, dff), BF16)],
        compiler_params=pltpu.CompilerParams(dimension_semantics=("parallel",), vmem_limit_bytes=VMEM_LIMIT),
        name="ffn",
    )(x2, g, wg, wu, wd)


def _permute_w_in_kernel(w_ref, o_ref):
    rows = w_ref.shape[0]
    n_rest = P_COLS - COL_CVB
    o_ref[:, 0:COL_GATE] = w_ref[:, 0:COL_GATE].astype(BF16)
    lane = lax.broadcasted_iota(jnp.int32, (rows, LANES), 1)
    o_ref[:, COL_GATE:COL_GATE + LANES] = jnp.where(lane < N_GATES, w_ref[:, COL_GATE:COL_GATE + LANES], 0.0).astype(BF16)
    o_ref[:, COL_CVB:P_COLS] = w_ref[:, COL_GATE + N_GATES:COL_GATE + N_GATES + n_rest].astype(BF16)


def _permute_w_in(w_in):
    d, cols = w_in.shape
    tr = 256
    return pl.pallas_call(
        _permute_w_in_kernel,
        grid=(d // tr,),
        in_specs=[pl.BlockSpec((tr, cols), lambda i: (i, 0))],
        out_specs=pl.BlockSpec((tr, P_COLS), lambda i: (i, 0)),
        out_shape=jax.ShapeDtypeStruct((d, P_COLS), BF16),
        compiler_params=pltpu.CompilerParams(dimension_semantics=("parallel",), vmem_limit_bytes=VMEM_LIMIT),
        name="permute_w_in",
    )(w_in)


def _layer(x2, batch, seq, g_mix, w_in, b_gate, g_q_nsa, g_k_cmp, g_k_slc, g_k_win, pe_k_cmp, pe_v_cmp,
           w1_k_cmp, w2_k_cmp, w1_v_cmp, w2_v_cmp, conv_w, g_q_dil, g_k_dil, g_out, w_out,
           g_ffn, w_gate, w_up, w_down):
    row = lambda v: v.reshape(1, -1)
    per_group = lambda g, n: jnp.tile(g, n).reshape(1, -1)
    p2 = _inproj(x2, row(g_mix), _permute_w_in(w_in))

    pe, w1, w2 = _compress_weights(pe_k_cmp, pe_v_cmp, w1_k_cmp, w2_k_cmp, w1_v_cmp, w2_v_cmp)
    kc, vc = _compress(p2, pe, w1, w2, per_group(g_k_cmp, 2), batch, seq)
    ocmp, oslc = _nsa(p2, kc, vc, per_group(g_q_nsa, N_HEADS), per_group(g_k_slc, 2), batch, seq)
    owin = _banded(p2, per_group(g_q_nsa, N_HEADS), per_group(g_k_win, 2), jnp.asarray(_band_bias_tiles(NSA_WINDOW - 1, _window_mult)),
                   batch, seq, window=NSA_WINDOW - 1, slopes=SLOPES_A, shared_kv=True,
                   col_q=COL_QA, col_k=COL_KWV, col_v=COL_KWV, name="nsa_window")
    od = _banded(p2, per_group(g_q_dil, N_HEADS), per_group(g_k_dil, N_HEADS),
                 jnp.asarray(_band_bias_tiles(DILATED_CONFIGS[-1][0], _dilated_mult)),
                 batch, seq, window=DILATED_CONFIGS[-1][0], slopes=SLOPES_D, shared_kv=False,
                 col_q=COL_QD, col_k=COL_KD, col_v=COL_VD, name="dilated")
    oc = _stick(p2, batch, seq)

    bg = jnp.zeros((1, LANES), F32).at[0, :N_GATES].set(b_gate)
    cwp = jnp.zeros((8, GROUP_W), F32).at[:CONV_K].set(conv_w)
    x1 = _mixout(x2, p2, ocmp, oslc, owin, oc, od, bg, cwp, row(g_out), w_out.astype(BF16), seq)
    return _ffn(x1, row(g_ffn), w_gate.astype(BF16), w_up.astype(BF16), w_down.astype(BF16))


def kernel(x, g_mix, w_in, b_gate, g_q_nsa, g_k_cmp, g_k_slc, g_k_win, pe_k_cmp, pe_v_cmp, w1_k_cmp, w2_k_cmp,
           w1_v_cmp, w2_v_cmp, conv_w, g_q_dil, g_k_dil, g_out, w_out, g_ffn, w_gate, w_up, w_down):
    batch, seq, d = x.shape
    assert seq % (CMP_STRIDE * LANES) == 0 and d % LANES == 0
    x2 = x.reshape(batch * seq, d)
    params = (g_mix, w_in, b_gate, g_q_nsa, g_k_cmp, g_k_slc, g_k_win, pe_k_cmp, pe_v_cmp, w1_k_cmp, w2_k_cmp,
              w1_v_cmp, w2_v_cmp, conv_w, g_q_dil, g_k_dil, g_out, w_out, g_ffn, w_gate, w_up, w_down)
    for layer in range(g_mix.shape[0]):
        x2 = _layer(x2, batch, seq, *[p[layer] for p in params])
    return x2.reshape(batch, seq, d)
```

```python
import functools

import numpy as np
import jax
import jax.numpy as jnp
from jax import lax
from jax.experimental import pallas as pl
from jax.experimental.pallas import tpu as pltpu

F32 = jnp.float32
BF16 = jnp.bfloat16

HEAD_DIM = 64
N_HEADS = 4
GROUP_W = N_HEADS * HEAD_DIM
CONV_K = 3
CMP_LEN = 32
CMP_STRIDE = 16
SEL_BLOCK = 64
N_SELECT = 16
NSA_WINDOW = 512
DILATED_CONFIGS = ((128, 1), (512, 4), (2048, 16))
NEG_INF = -1e30
FORCE_SCORE = 1e6
RMS_EPS = 1e-6

TQ = 512
TK = 256
N_TOP = TQ // TK
STICK_TQ = 256
LANES = 128
MXU_DEPTH = 256
MIX_CHAIN = 256
MXU_AHEAD = 2
FLASH_AHEAD = 2
SUM_PAD = 16
LOG2E = 1.4426950408889634
VMEM_LIMIT = 52 * 1024 * 1024

COL_QA, COL_KVC, COL_KSV, COL_KWV, COL_GATE = 0, 256, 384, 512, 640
COL_CVB, COL_CVC, COL_CVU = 768, 1024, 1280
COL_QC, COL_KC, COL_VC = 1536, 1792, 2048
COL_QD, COL_KD, COL_VD = 2304, 2560, 2816
P_COLS = 3072
N_GATES = 12

_NT = (((1,), (1,)), ((), ()))


def _alibi_slopes():
    s = [2.0 ** (-8.0 * i / 8) for i in range(1, 9)]
    return tuple(s[0::2]), tuple(s[1::2])


SLOPES_A, SLOPES_D = _alibi_slopes()


def _rms(x, g):
    return x * lax.rsqrt(jnp.mean(x * x, axis=-1, keepdims=True) + RMS_EPS) * g


def _exact_dot(a, sel):
    hi = a.astype(BF16)
    lo = (a - hi.astype(F32)).astype(BF16)
    if 2 * a.shape[1] <= MXU_DEPTH:
        return jnp.dot(jnp.concatenate([hi, lo], axis=1), jnp.concatenate([sel, sel], axis=0),
                       preferred_element_type=F32)
    return jnp.dot(hi, sel, preferred_element_type=F32) + jnp.dot(lo, sel, preferred_element_type=F32)


def _group_rms(x, g):
    w = x.shape[1]
    same = ((lax.broadcasted_iota(jnp.int32, (w, w), 0) >> 6)
            == (lax.broadcasted_iota(jnp.int32, (w, w), 1) >> 6)).astype(BF16)
    ssq = _exact_dot(x * x, same)
    return x * lax.rsqrt(ssq * (1.0 / HEAD_DIM) + RMS_EPS) * g


EVEN_AUG, ODD_AUG = HEAD_DIM, 0


def _key_aug(pos, first):
    lane = lax.broadcasted_iota(jnp.int32, (pos.shape[0], LANES), 1) - first
    hi = (pos >> 6).astype(F32)
    lo = (pos & 63).astype(F32)
    return jnp.where(lane == 0, hi, jnp.where(lane == 1, lo, jnp.where((lane == 2) | (lane == 3), 1.0, 0.0)))


def _query_aug(t, slope, first):
    lane = lax.broadcasted_iota(jnp.int32, (t.shape[0], LANES), 1) - first
    hi = (t >> 6).astype(F32) * (-64.0 * slope)
    lo = (t & 63).astype(F32) * (-slope)
    return jnp.where(lane == 0, 64.0 * slope,
                     jnp.where(lane == 1, slope, jnp.where(lane == 2, hi, jnp.where(lane == 3, lo, 0.0))))


def _pair_keys(pair, pos):
    lane = lax.broadcasted_iota(jnp.int32, pair.shape, 1)
    aug_e = 0.0 if pos is None else _key_aug(pos, EVEN_AUG)
    aug_o = 0.0 if pos is None else _key_aug(pos, ODD_AUG)
    return (jnp.where(lane < HEAD_DIM, pair, aug_e).astype(BF16),
            jnp.where(lane >= HEAD_DIM, pair, aug_o).astype(BF16))


def _prep_queries(q, g, t0, slopes, qp_s):
    qn = (q if g is None else _group_rms(q, g)) * (HEAD_DIM ** -0.5)
    tpos = t0 + lax.broadcasted_iota(jnp.int32, (q.shape[0], 1), 0)
    lane = lax.broadcasted_iota(jnp.int32, (q.shape[0], LANES), 1)
    for i in range(N_HEADS // 2):
        pair = qn[:, i * LANES:(i + 1) * LANES]
        if slopes is None:
            qp_s[i] = pair.astype(BF16)
            continue
        even = jnp.where((lane >= EVEN_AUG) & (lane < EVEN_AUG + 4), _query_aug(tpos, slopes[2 * i], EVEN_AUG), pair)
        odd = jnp.where(lane < ODD_AUG + 4, _query_aug(tpos, slopes[2 * i + 1], ODD_AUG), pair)
        qp_s[2 * i] = even.astype(BF16)
        qp_s[2 * i + 1] = odd.astype(BF16)


def _inproj_kernel(x_ref, g_ref, w_ref, o_ref, *, nchunk):
    h = _rms(x_ref[...], g_ref[...]).astype(BF16)
    cw = P_COLS // nchunk
    for c in range(nchunk):
        o_ref[:, c * cw:(c + 1) * cw] = jnp.dot(h, w_ref[:, c * cw:(c + 1) * cw], preferred_element_type=F32)


def _inproj(x2, g, w):
    t, d = x2.shape
    tm = 512
    return pl.pallas_call(
        functools.partial(_inproj_kernel, nchunk=6),
        grid=(t // tm,),
        in_specs=[pl.BlockSpec((tm, d), lambda i: (i, 0)),
                  pl.BlockSpec((1, d), lambda i: (0, 0)),
                  pl.BlockSpec((d, P_COLS), lambda i: (0, 0))],
        out_specs=pl.BlockSpec((tm, P_COLS), lambda i: (i, 0)),
        out_shape=jax.ShapeDtypeStruct((t, P_COLS), F32),
        compiler_params=pltpu.CompilerParams(dimension_semantics=("parallel",), vmem_limit_bytes=VMEM_LIMIT),
        name="inproj",
    )(x2, g, w)


def _gelu_tanh(x):
    return x * (0.5 * (1.0 + jnp.tanh(np.sqrt(2.0 / np.pi).astype(np.float32) * (x + 0.044715 * (x * x * x)))))


def _compress_kernel(kv_ref, pe_ref, w1_ref, w2_ref, gk_ref, kc_ref, vct_ref):
    nc = kv_ref.shape[0] // CMP_STRIDE
    first = None
    second = None
    for j in range(CMP_STRIDE):
        tok = kv_ref[pl.ds(j, nc, stride=CMP_STRIDE), :]
        a = jnp.dot((tok + pe_ref[j:j + 1, :]).astype(BF16), w1_ref[j], preferred_element_type=F32)
        b = jnp.dot((tok + pe_ref[CMP_STRIDE + j:CMP_STRIDE + j + 1, :]).astype(BF16), w1_ref[CMP_STRIDE + j],
                    preferred_element_type=F32)
        first = a if first is None else first + a
        second = b if second is None else second + b
    hid = first + pltpu.roll(second, nc - 1, 0)
    out = jnp.dot(_gelu_tanh(hid).astype(BF16), w2_ref[...], preferred_element_type=F32)
    kn = _group_rms(out, gk_ref[...])
    end = lax.broadcasted_iota(jnp.int32, (nc, 1), 0) * CMP_STRIDE + (CMP_LEN - 1)
    even, _ = _pair_keys(kn, end)
    _, odd = _pair_keys(pltpu.roll(kn, HEAD_DIM, 1), end)
    kc_ref[0] = jnp.concatenate([even, odd], axis=1)
    vct_ref[0] = out.T[HEAD_DIM:, :].astype(BF16)


def _compress(p2, pe, w1, w2, gk, batch, seq):
    nc = seq // CMP_STRIDE
    hid2 = w1.shape[2]
    return pl.pallas_call(
        _compress_kernel,
        grid=(batch,),
        in_specs=[pl.BlockSpec((seq, LANES), lambda i: (i, COL_KVC // LANES)),
                  pl.BlockSpec((CMP_LEN, LANES), lambda i: (0, 0)),
                  pl.BlockSpec((CMP_LEN, LANES, hid2), lambda i: (0, 0, 0)),
                  pl.BlockSpec((hid2, LANES), lambda i: (0, 0)),
                  pl.BlockSpec((1, LANES), lambda i: (0, 0))],
        out_specs=[pl.BlockSpec((1, nc, 2 * LANES), lambda i: (i, 0, 0)),
                   pl.BlockSpec((1, HEAD_DIM, nc), lambda i: (i, 0, 0))],
        out_shape=[jax.ShapeDtypeStruct((batch, nc, 2 * LANES), BF16),
                   jax.ShapeDtypeStruct((batch, HEAD_DIM, nc), BF16)],
        compiler_params=pltpu.CompilerParams(dimension_semantics=("parallel",), vmem_limit_bytes=VMEM_LIMIT),
        name="nsa_compress",
    )(p2, pe, w1, w2, gk)


def _compress_weights(pe_k, pe_v, w1_k, w2_k, w1_v, w2_v):
    hid = w1_k.shape[1]
    w1k = w1_k.reshape(CMP_LEN, HEAD_DIM, hid)
    w1v = w1_v.reshape(CMP_LEN, HEAD_DIM, hid)
    z1 = jnp.zeros_like(w1k)
    w1 = jnp.concatenate([jnp.concatenate([w1k, z1], axis=2), jnp.concatenate([z1, w1v], axis=2)], axis=1)
    z2 = jnp.zeros_like(w2_k)
    w2 = jnp.concatenate([jnp.concatenate([w2_k, z2], axis=1), jnp.concatenate([z2, w2_v], axis=1)], axis=0)
    pe = jnp.concatenate([pe_k, pe_v], axis=1)
    return pe, w1.astype(BF16), w2.astype(BF16)


def _flash_scores(kp_of, bias, qp_s, s_s, m_cur):
    m_next = []
    for h in range(N_HEADS):
        s = lax.dot_general(kp_of(h), qp_s[h], _NT, preferred_element_type=F32) + bias
        s_s[h] = s
        m_next.append(jnp.maximum(m_cur[h], jnp.max(s, axis=0, keepdims=True)))
    return tuple(m_next)


def _flash_accum(vt_of, s_s, acc_s, m_prev, m_cur, ls):
    new_l = []
    for h in range(N_HEADS):
        alpha = jnp.exp(m_prev[h] - m_cur[h])
        p = jnp.exp(s_s[h] - m_cur[h])
        new_l.append(alpha * ls[h] + jnp.sum(p, axis=0, keepdims=True))
        acc_s[h] = alpha * acc_s[h] + jnp.dot(vt_of(h), p.astype(BF16), preferred_element_type=F32)
    return tuple(new_l)


def _flash_accum_and_scores(kp_of, bias, vt_of, qp_s, s_s, acc_s, m_prev, m_cur, ls):
    def scores(h):
        return lax.dot_general(kp_of(h), qp_s[h], _NT, preferred_element_type=F32) + bias

    m_next, new_l = [], []
    s_new = {h: scores(h) for h in range(FLASH_AHEAD)}
    for h in range(N_HEADS):
        alpha = jnp.exp(m_prev[h] - m_cur[h])
        p = jnp.exp(s_s[h] - m_cur[h])
        new_l.append(alpha * ls[h] + jnp.sum(p, axis=0, keepdims=True))
        acc_s[h] = alpha * acc_s[h] + jnp.dot(vt_of(h), p.astype(BF16), preferred_element_type=F32)
        s_h = s_new.pop(h)
        s_s[h] = s_h
        m_next.append(jnp.maximum(m_cur[h], jnp.max(s_h, axis=0, keepdims=True)))
        if h + FLASH_AHEAD < N_HEADS:
            s_new[h + FLASH_AHEAD] = scores(h + FLASH_AHEAD)
    return tuple(m_next), tuple(new_l)


def _flash_pipelined(tile_of, n_tiles, kp_fn, bias_fn, vt_fn, qp_s, s_s, acc_s, o_ref):
    acc_s[...] = jnp.zeros(acc_s.shape, F32)
    m0 = tuple(jnp.full((1, TQ), NEG_INF, F32) for _ in range(N_HEADS))
    l0 = tuple(jnp.zeros((1, TQ), F32) for _ in range(N_HEADS))
    first_kt = tile_of(0)
    m1 = _flash_scores(kp_fn(first_kt), bias_fn(first_kt, True), qp_s, s_s, m0)

    def step(i, carry, top):
        m_prev, m_cur, ls = carry
        nxt = tile_of(i + 1)
        m_next, ls = _flash_accum_and_scores(kp_fn(nxt), bias_fn(nxt, top), vt_fn(tile_of(i)), qp_s, s_s, acc_s,
                                             m_prev, m_cur, ls)
        return m_cur, m_next, ls

    carry = (m0, m1, l0)
    for i in range(N_TOP - 1):
        carry = step(i, carry, True)
    m_prev, m_cur, ls = lax.fori_loop(N_TOP - 1, n_tiles - 1, lambda i, c: step(i, c, False), carry)
    ls = _flash_accum(vt_fn(tile_of(n_tiles - 1)), s_s, acc_s, m_prev, m_cur, ls)
    o_ref[0] = jnp.concatenate([acc_s[h] * (1.0 / ls[h]) for h in range(N_HEADS)], axis=0)


def _nsa_kernel(q_ref, kc_ref, vct_ref, ksv_ref, gq_ref, gks_ref, ocmp_ref, oslc_ref,
                ksp_s, vst_s, qp_s, imp_s, sel_s, s_s, acc_s, hit_s, *, seq):
    qi = pl.program_id(1)
    nc = seq // CMP_STRIDE
    nsel = seq // SEL_BLOCK
    nkt = seq // TK

    @pl.when(qi == 0)
    def _prep():
        for c in range(nkt):
            rows = slice(c * TK, (c + 1) * TK)
            blk = ksv_ref[rows, :]
            pos = c * TK + lax.broadcasted_iota(jnp.int32, (TK, 1), 0)
            kn = _group_rms(blk, gks_ref[...])
            ksp_s[rows, 0:LANES], _ = _pair_keys(kn, pos)
            _, ksp_s[rows, LANES:2 * LANES] = _pair_keys(pltpu.roll(kn, HEAD_DIM, 1), pos)
            vst_s[c] = blk.T[HEAD_DIM:, :].astype(BF16)
        imp_s[:, 0:8, :] = jnp.zeros((TQ // LANES, 8, LANES), F32)

    t0 = qi * TQ
    _prep_queries(q_ref[...], gq_ref[...], t0, SLOPES_A, qp_s)

    kc = [kc_ref[0, :, 0:LANES], kc_ref[0, :, LANES:2 * LANES]]
    vct = vct_ref[0]
    tq_row = t0 + lax.broadcasted_iota(jnp.int32, (nc, TQ), 1)
    n_idx = lax.broadcasted_iota(jnp.int32, (nc, TQ), 0)
    vis = (tq_row >= n_idx * CMP_STRIDE + (CMP_LEN - 1)) & (n_idx < nc - 1)
    vis_bias = jnp.where(vis, 0.0, NEG_INF)
    sees_any = (t0 + lax.broadcasted_iota(jnp.int32, (1, TQ), 1) >= CMP_LEN - 1).astype(F32)
    imp = jnp.zeros((nc, TQ), F32)
    ocmp_t = []
    scores = [lax.dot_general(kc[h % 2], qp_s[h], _NT, preferred_element_type=F32) + vis_bias
              for h in range(N_HEADS)]
    for h in range(N_HEADS):
        sc = scores[h]
        e = jnp.exp(sc - jnp.max(sc, axis=0, keepdims=True))
        p = e * (sees_any / jnp.sum(e, axis=0, keepdims=True))
        ocmp_t.append(jnp.dot(vct, p.astype(BF16), preferred_element_type=F32))
        imp = imp + p
    ocmp_ref[0] = jnp.concatenate(ocmp_t, axis=0)

    halves = []
    for half in range(TQ // LANES):
        imp_s[half, 8:8 + nc, :] = imp[:, half * LANES:(half + 1) * LANES]
        r = [imp_s[half, pl.ds(8 + k, nsel, stride=4), :] for k in range(4)]
        rm1 = imp_s[half, pl.ds(7, nsel, stride=4), :]
        halves.append(rm1 + 2.0 * (r[0] + r[1] + r[2]) + r[3])
    imp_blk = jnp.concatenate(halves, axis=1)
    blk = lax.broadcasted_iota(jnp.int32, (nsel, TQ), 0)
    tl = t0 + lax.broadcasted_iota(jnp.int32, (nsel, TQ), 1)
    cur = tl >> 6
    forced = (blk == 0) | (blk == cur) | (blk == cur - 1)
    valid = blk * SEL_BLOCK <= tl
    score = jnp.where(forced, FORCE_SCORE, jnp.where(valid, imp_blk, -FORCE_SCORE))
    sub = 8
    per_tile = TK // SEL_BLOCK
    tile_hit = []
    for g in range(nsel // sub):
        mine = score[g * sub:(g + 1) * sub, :]
        blk_g = g * sub + lax.broadcasted_iota(jnp.int32, (sub, TQ), 0)
        rank = jnp.zeros((sub, TQ), F32)
        for j in range(nsel):
            row = score[j:j + 1, :]
            if j < g * sub:
                beats = row >= mine
            elif j >= (g + 1) * sub:
                beats = row > mine
            else:
                beats = (row > mine) | ((row == mine) & (blk_g > j))
            rank = rank + jnp.where(beats, 1.0, 0.0)
        sel_g = jnp.where(rank < min(N_SELECT, nsel), 0.0, NEG_INF)
        sel_s[g * sub:(g + 1) * sub, :] = sel_g
        for part in range(sub // per_tile):
            tile_hit.append(jnp.max(sel_g[part * per_tile:(part + 1) * per_tile, :]))

    n_hit = jnp.int32(0)
    for kt in reversed(range(nkt)):
        hit_s[n_hit] = kt
        n_hit = n_hit + ((tile_hit[kt] > -1.0) & (kt * TK < t0)).astype(jnp.int32)

    def block_bias(kt):
        return jnp.concatenate([jnp.broadcast_to(sel_s[pl.ds(kt * per_tile + j, 1), :], (SEL_BLOCK, TQ))
                                for j in range(per_tile)], axis=0)

    k_local = lax.broadcasted_iota(jnp.int32, (TK, TQ), 0)
    q_local = lax.broadcasted_iota(jnp.int32, (TK, TQ), 1)
    top_kt = qi * N_TOP + (N_TOP - 1)

    def kp_fn(kt):
        off = pl.multiple_of(kt * TK, TK)
        return lambda h: ksp_s[pl.ds(off, TK), (h % 2) * LANES:(h % 2 + 1) * LANES]

    def bias_fn(kt, top):
        if top:
            return jnp.where(kt * TK - t0 + k_local <= q_local, block_bias(kt), NEG_INF)
        return block_bias(kt)

    tile_of = lambda i: jnp.where(i < N_TOP, top_kt - i, hit_s[jnp.maximum(i - N_TOP, 0)])
    _flash_pipelined(tile_of, n_hit + N_TOP, kp_fn, bias_fn, lambda kt: (lambda h: vst_s[kt]), qp_s, s_s, acc_s,
                     oslc_ref)


def _nsa(p2, kc, vc, gq, gks, batch, seq):
    nq = seq // TQ
    nc = seq // CMP_STRIDE
    t = batch * seq
    return pl.pallas_call(
        functools.partial(_nsa_kernel, seq=seq),
        grid=(batch, nq),
        in_specs=[pl.BlockSpec((TQ, GROUP_W), lambda b, i: (b * nq + i, COL_QA // GROUP_W)),
                  pl.BlockSpec((1, nc, 2 * LANES), lambda b, i: (b, 0, 0)),
                  pl.BlockSpec((1, HEAD_DIM, nc), lambda b, i: (b, 0, 0)),
                  pl.BlockSpec((seq, LANES), lambda b, i: (b, COL_KSV // LANES)),
                  pl.BlockSpec((1, GROUP_W), lambda b, i: (0, 0)),
                  pl.BlockSpec((1, LANES), lambda b, i: (0, 0))],
        out_specs=[pl.BlockSpec((1, GROUP_W, TQ), lambda b, i: (b * nq + i, 0, 0)),
                   pl.BlockSpec((1, GROUP_W, TQ), lambda b, i: (b * nq + i, 0, 0))],
        out_shape=[jax.ShapeDtypeStruct((t // TQ, GROUP_W, TQ), F32),
                   jax.ShapeDtypeStruct((t // TQ, GROUP_W, TQ), F32)],
        scratch_shapes=[pltpu.VMEM((seq, 2 * LANES), BF16),
                        pltpu.VMEM((seq // TK, HEAD_DIM, TK), BF16),
                        pltpu.VMEM((N_HEADS, TQ, LANES), BF16),
                        pltpu.VMEM((TQ // LANES, 8 + nc, LANES), F32),
                        pltpu.VMEM((seq // SEL_BLOCK, TQ), F32),
                        pltpu.VMEM((N_HEADS, TK, TQ), F32),
                        pltpu.VMEM((N_HEADS, HEAD_DIM, TQ), F32),
                        pltpu.SMEM((seq // TK + 1,), jnp.int32)],
        compiler_params=pltpu.CompilerParams(dimension_semantics=("parallel", "arbitrary"),
                                             vmem_limit_bytes=VMEM_LIMIT),
        name="nsa_cmp_slc",
    )(p2, kc, vc, p2, gq, gks)


def _band_bias_tiles(window, mult_fn):
    nd = (window + TQ - 1) // TK + 1
    kk = np.arange(TK)[:, None]
    qq = np.arange(TQ)[None, :]
    tiles = np.empty((nd, TK, TQ), np.float32)
    for di in range(nd):
        d = (di - (N_TOP - 1)) * TK + qq - kk
        mult = mult_fn(d)
        tiles[di] = np.where(mult > 0, np.log(np.maximum(mult, 1)), NEG_INF)
    return tiles


def _window_mult(d):
    return ((d >= 0) & (d <= NSA_WINDOW - 1)).astype(np.float64)


def _dilated_mult(d):
    m = np.zeros(d.shape, np.float64)
    for window, dil in DILATED_CONFIGS:
        m += ((d >= 0) & (d <= window) & (d % dil == 0)).astype(np.float64)
    return m


def _banded_kernel(*refs, seq, window, slopes, shared_kv):
    if shared_kv:
        q_ref, kv_ref, gq_ref, gk_ref, bias_ref, o_ref, kp_s, vt_s, qp_s, s_s, acc_s = refs
    else:
        q_ref, k_ref, v_ref, gq_ref, gk_ref, bias_ref, o_ref, kp_s, vt_s, qp_s, s_s, acc_s = refs
    qi = pl.program_id(1)
    nkt = seq // TK

    @pl.when(qi == 0)
    def _prep():
        for c in range(nkt):
            rows = slice(c * TK, (c + 1) * TK)
            pos = c * TK + lax.broadcasted_iota(jnp.int32, (TK, 1), 0)
            if shared_kv:
                blk = kv_ref[rows, :]
                kn = _group_rms(blk, gk_ref[...])
                kp_s[rows, 0:LANES], _ = _pair_keys(kn, pos)
                _, kp_s[rows, LANES:2 * LANES] = _pair_keys(pltpu.roll(kn, HEAD_DIM, 1), pos)
                vt_s[c] = blk.T[HEAD_DIM:, :].astype(BF16)
            else:
                kn = _group_rms(k_ref[rows, :], gk_ref[...])
                for i in range(N_HEADS // 2):
                    even, odd = _pair_keys(kn[:, i * LANES:(i + 1) * LANES], pos)
                    kp_s[rows, 2 * i * LANES:(2 * i + 1) * LANES] = even
                    kp_s[rows, (2 * i + 1) * LANES:(2 * i + 2) * LANES] = odd
                vt_s[c] = v_ref[rows, :].T.astype(BF16)

    t0 = qi * TQ
    _prep_queries(q_ref[...], gq_ref[...], t0, slopes, qp_s)
    kt_lo = jnp.maximum(t0 - window, 0) // TK

    def kp_fn(kt):
        off = pl.multiple_of(kt * TK, TK)
        if shared_kv:
            return lambda h: kp_s[pl.ds(off, TK), (h % 2) * LANES:(h % 2 + 1) * LANES]
        return lambda h: kp_s[pl.ds(off, TK), h * LANES:(h + 1) * LANES]

    def vt_fn(kt):
        if shared_kv:
            return lambda h: vt_s[kt]
        return lambda h: vt_s[kt, h * HEAD_DIM:(h + 1) * HEAD_DIM, :]

    top_kt = qi * N_TOP + (N_TOP - 1)
    _flash_pipelined(lambda i: top_kt - i, top_kt + 1 - kt_lo, kp_fn, lambda kt, top: bias_ref[top_kt - kt], vt_fn,
                     qp_s, s_s, acc_s, o_ref)


def _banded(p2, gq, gk, bias_tiles, batch, seq, *, window, slopes, shared_kv, col_q, col_k, col_v, name):
    nq = seq // TQ
    t = batch * seq
    nd = bias_tiles.shape[0]
    n_kv = 1 if shared_kv else N_HEADS
    q_spec = pl.BlockSpec((TQ, GROUP_W), lambda b, i: (b * nq + i, col_q // GROUP_W))
    gain = lambda g: pl.BlockSpec(g.shape, lambda b, i: (0, 0))
    bias_spec = pl.BlockSpec((nd, TK, TQ), lambda b, i: (0, 0, 0))
    if shared_kv:
        kv_specs = [pl.BlockSpec((seq, LANES), lambda b, i: (b, col_k // LANES))]
        operands = (p2, p2, gq, gk, bias_tiles)
    else:
        kv_specs = [pl.BlockSpec((seq, GROUP_W), lambda b, i: (b, col_k // GROUP_W)),
                    pl.BlockSpec((seq, GROUP_W), lambda b, i: (b, col_v // GROUP_W))]
        operands = (p2, p2, p2, gq, gk, bias_tiles)
    return pl.pallas_call(
        functools.partial(_banded_kernel, seq=seq, window=window, slopes=slopes, shared_kv=shared_kv),
        grid=(batch, nq),
        in_specs=[q_spec] + kv_specs + [gain(gq), gain(gk), bias_spec],
        out_specs=pl.BlockSpec((1, GROUP_W, TQ), lambda b, i: (b * nq + i, 0, 0)),
        out_shape=jax.ShapeDtypeStruct((t // TQ, GROUP_W, TQ), F32),
        scratch_shapes=[pltpu.VMEM((seq, (2 if shared_kv else N_HEADS) * LANES), BF16),
                        pltpu.VMEM((seq // TK, n_kv * HEAD_DIM, TK), BF16),
                        pltpu.VMEM((N_HEADS, TQ, LANES), BF16),
                        pltpu.VMEM((N_HEADS, TK, TQ), F32),
                        pltpu.VMEM((N_HEADS, HEAD_DIM, TQ), F32)],
        compiler_params=pltpu.CompilerParams(dimension_semantics=("parallel", "arbitrary"),
                                             vmem_limit_bytes=VMEM_LIMIT),
        name=name,
    )(*operands)


def _stick_kernel(q_ref, k_ref, v_ref, o_ref, kp_s, vt_s, qp_s, sp_s, e_s, acc_s, *, seq):
    TQ, N_TOP = STICK_TQ, STICK_TQ // TK
    qi = pl.program_id(1)
    nkt = seq // TK

    @pl.when(qi == 0)
    def _prep():
        for c in range(nkt):
            rows = slice(c * TK, (c + 1) * TK)
            kb = k_ref[rows, :]
            for i in range(N_HEADS // 2):
                even, odd = _pair_keys(kb[:, i * LANES:(i + 1) * LANES], None)
                kp_s[rows, 2 * i * LANES:(2 * i + 1) * LANES] = even
                kp_s[rows, (2 * i + 1) * LANES:(2 * i + 2) * LANES] = odd
            vt_s[c] = v_ref[rows, :].T.astype(BF16)

    _prep_queries(q_ref[...], None, qi * TQ, None, qp_s)
    half = TK // 2
    col = lax.broadcasted_iota(jnp.int32, (half + SUM_PAD, half), 1)
    srow = lax.broadcasted_iota(jnp.int32, (half + SUM_PAD, half), 0)
    sums = ((srow == half) | ((srow < half) & (col > srow))).astype(BF16)
    k_local = lax.broadcasted_iota(jnp.int32, (TK, TQ), 0)
    q_local = lax.broadcasted_iota(jnp.int32, (TK, TQ), 1)
    top_kt = qi * N_TOP + (N_TOP - 1)
    acc_s[...] = jnp.zeros(acc_s.shape, F32)

    def past_mask(tiles_above):
        return tiles_above * TK + k_local < q_local

    def logits(kt, h):
        off = pl.multiple_of(kt * TK, TK)
        return lax.dot_general(kp_s[pl.ds(off, TK), h * LANES:(h + 1) * LANES], qp_s[h // 2], _NT,
                               preferred_element_type=F32)

    def stage(h, z, past):
        sp = jnp.maximum(z, 0.0) + jnp.log(1.0 + jnp.exp2(jnp.abs(z) * (-LOG2E)))
        spm = sp if past is None else jnp.where(past, sp, 0.0)
        sp_s[h] = spm.astype(BF16)
        logsig = z - sp
        e_s[h] = logsig if past is None else jnp.where(past, logsig, NEG_INF)

    def tails(h):
        return [jnp.dot(sums, sp_s[h, b * half:(b + 1) * half, :], preferred_element_type=F32) for b in range(2)]

    def consume(kt, h, w, later):
        later_lo = later + w[1][half:half + 1, :]
        attn = jnp.concatenate([jnp.exp(e_s[h, 0:half, :] - w[0][0:half, :] - later_lo),
                                jnp.exp(e_s[h, half:TK, :] - w[1][0:half, :] - later)], axis=0)
        acc_s[h] += jnp.dot(vt_s[kt, h * HEAD_DIM:(h + 1) * HEAD_DIM, :], attn.astype(BF16),
                            preferred_element_type=F32)
        return later_lo + w[0][half:half + 1, :]

    ahead = MXU_AHEAD
    for h in range(N_HEADS):
        stage(h, logits(top_kt, h), past_mask(N_TOP - 1))

    def step(kt, carry, past):
        laters, w_first, z_first = carry
        w = {0: list(w_first)}
        z = {0: z_first}
        for h in range(1, ahead):
            w[h] = tails(h)
            z[h] = logits(kt - 1, h)
        out = []
        for h in range(N_HEADS):
            out.append(consume(kt, h, w.pop(h), laters[h]))
            stage(h, z.pop(h), past)
            if h + ahead < N_HEADS:
                w[h + ahead] = tails(h + ahead)
                z[h + ahead] = logits(kt - 1, h + ahead)
            elif h + ahead == N_HEADS:
                nxt = (tuple(tails(0)), logits(jnp.maximum(kt - 2, 0), 0))
        return tuple(out), nxt[0], nxt[1]

    carry = (tuple(jnp.zeros((1, TQ), F32) for _ in range(N_HEADS)),
             tuple(tails(0)), logits(jnp.maximum(top_kt - 1, 0), 0))
    for j in range(N_TOP - 1):
        carry = step(top_kt - j, carry, past_mask(N_TOP - 2 - j))
    below = qi * N_TOP
    laters, w_first, _ = lax.fori_loop(0, below, lambda i, c: step(below - i, c, None), carry)
    w = {0: list(w_first)}
    for h in range(1, ahead):
        w[h] = tails(h)
    for h in range(N_HEADS):
        consume(0, h, w.pop(h), laters[h])
        if h + ahead < N_HEADS:
            w[h + ahead] = tails(h + ahead)
    o_ref[0] = acc_s[...].reshape(GROUP_W, TQ)


def _stick(p2, batch, seq):
    TQ = STICK_TQ
    nq = seq // TQ
    t = batch * seq
    return pl.pallas_call(
        functools.partial(_stick_kernel, seq=seq),
        grid=(batch, nq),
        in_specs=[pl.BlockSpec((TQ, GROUP_W), lambda b, i: (b * nq + i, COL_QC // GROUP_W)),
                  pl.BlockSpec((seq, GROUP_W), lambda b, i: (b, COL_KC // GROUP_W)),
                  pl.BlockSpec((seq, GROUP_W), lambda b, i: (b, COL_VC // GROUP_W))],
        out_specs=pl.BlockSpec((1, GROUP_W, TQ), lambda b, i: (b * nq + i, 0, 0)),
        out_shape=jax.ShapeDtypeStruct((t // TQ, GROUP_W, TQ), F32),
        scratch_shapes=[pltpu.VMEM((seq, N_HEADS * LANES), BF16),
                        pltpu.VMEM((seq // TK, GROUP_W, TK), BF16),
                        pltpu.VMEM((N_HEADS // 2, TQ, LANES), BF16),
                        pltpu.VMEM((N_HEADS, TK, TQ), BF16),
                        pltpu.VMEM((N_HEADS, TK, TQ), F32),
                        pltpu.VMEM((N_HEADS, HEAD_DIM, TQ), F32)],
        compiler_params=pltpu.CompilerParams(dimension_semantics=("parallel", "arbitrary"),
                                             vmem_limit_bytes=VMEM_LIMIT),
        name="stick_breaking",
    )(p2, p2, p2)


def _mixout_kernel(x_ref, ocmp_ref, oslc_ref, owin_ref, gate_ref, cvb_ref, cvc_ref, cvu_ref, pc_ref, pu_ref,
                   oc_ref, od_ref, bg_ref, cw_ref, gout_ref, wout_ref, o_ref, *, tiles_per_seq):
    i = pl.program_id(0)
    tm, d = x_ref.shape
    chains = [slice(c * MIX_CHAIN, (c + 1) * MIX_CHAIN) for c in range(tm // MIX_CHAIN)]

    def heads(ref, rows):
        tq = ref.shape[2]
        return ref[rows.start // tq, :, rows.start % tq:rows.start % tq + MIX_CHAIN].T

    cu = cvc_ref[...] * cvu_ref[...]
    prev = jnp.where(i % tiles_per_seq == 0, 0.0, pc_ref[...] * pu_ref[...])
    full = jnp.concatenate([prev, cu], axis=0)
    back1 = pltpu.roll(full, 1, 0)[8:, :]
    back2 = pltpu.roll(full, 2, 0)[8:, :]
    cw = cw_ref[...]
    ob = cvb_ref[...] * (cw[0:1, :] * back2 + cw[1:2, :] * back1 + cw[2:3, :] * cu)

    gates = jax.nn.sigmoid(gate_ref[...] + bg_ref[...])
    src = lax.broadcasted_iota(jnp.int32, (LANES, GROUP_W), 0)
    head3 = (lax.broadcasted_iota(jnp.int32, (LANES, GROUP_W), 1) >> 6) * 3
    branch = [(src == head3 + r).astype(BF16) for r in range(3)]
    assert d // HEAD_DIM <= LANES
    gather = ((lax.broadcasted_iota(jnp.int32, (d, LANES), 0) >> 6)
              == lax.broadcasted_iota(jnp.int32, (d, LANES), 1)).astype(BF16)
    spread = (lax.broadcasted_iota(jnp.int32, (LANES, d), 0)
              == (lax.broadcasted_iota(jnp.int32, (LANES, d), 1) >> 6)).astype(BF16)

    groups, ssq = [], []
    for rows in chains:
        oa = (_exact_dot(gates[rows], branch[0]) * heads(ocmp_ref, rows)
              + _exact_dot(gates[rows], branch[1]) * heads(oslc_ref, rows)
              + _exact_dot(gates[rows], branch[2]) * heads(owin_ref, rows))
        g = jnp.concatenate([oa, ob[rows], heads(oc_ref, rows), heads(od_ref, rows)], axis=1)
        groups.append(g)
        ssq.append(_exact_dot(g * g, gather))
    scale = [_exact_dot(lax.rsqrt(s * (1.0 / HEAD_DIM) + RMS_EPS), spread) for s in ssq]
    for rows, g, sc in zip(chains, groups, scale):
        mixed = (g * sc * gout_ref[...]).astype(BF16)
        o_ref[rows, :] = x_ref[rows, :] + jnp.dot(mixed, wout_ref[...], preferred_element_type=F32)


def _mixout(x2, p2, ocmp, oslc, owin, oc, od, bg, cw, gout, wout, seq):
    t, d = x2.shape
    tm = 2 * MIX_CHAIN
    rows = lambda w, col: pl.BlockSpec((tm, w), lambda i: (i, col // w))
    heads_t = lambda tq: pl.BlockSpec((tm // tq, GROUP_W, tq), lambda i: (i, 0, 0))
    prev8 = lambda col: pl.BlockSpec((8, GROUP_W), lambda i: (jnp.maximum(i * (tm // 8) - 1, 0), col // GROUP_W))
    const = lambda shape: pl.BlockSpec(shape, lambda i: (0, 0))
    assert tm % TQ == 0 and tm % STICK_TQ == 0
    return pl.pallas_call(
        functools.partial(_mixout_kernel, tiles_per_seq=seq // tm),
        grid=(t // tm,),
        in_specs=[rows(d, 0), heads_t(TQ), heads_t(TQ), heads_t(TQ),
                  rows(LANES, COL_GATE), rows(GROUP_W, COL_CVB), rows(GROUP_W, COL_CVC), rows(GROUP_W, COL_CVU),
                  prev8(COL_CVC), prev8(COL_CVU),
                  heads_t(STICK_TQ), heads_t(TQ),
                  const((1, LANES)), const((8, GROUP_W)), const((1, d)), const((d, d))],
        out_specs=rows(d, 0),
        out_shape=jax.ShapeDtypeStruct((t, d), F32),
        compiler_params=pltpu.CompilerParams(dimension_semantics=("parallel",), vmem_limit_bytes=VMEM_LIMIT),
        name="mixout",
    )(x2, ocmp, oslc, owin, p2, p2, p2, p2, p2, p2, oc, od, bg, cw, gout, wout)


def _ffn_kernel(x_ref, g_ref, wg_ref, wu_ref, wd_ref, o_ref, act_s, *, ff_chunk, out_chunk):
    h = _rms(x_ref[...], g_ref[...]).astype(BF16)
    dff = wg_ref.shape[1]
    for c in range(dff // ff_chunk):
        cols = slice(c * ff_chunk, (c + 1) * ff_chunk)
        a = jnp.dot(h, wg_ref[:, cols], preferred_element_type=F32)
        u = jnp.dot(h, wu_ref[:, cols], preferred_element_type=F32)
        act_s[:, cols] = (a * jax.nn.sigmoid(a) * u).astype(BF16)
    d = o_ref.shape[1]
    for c in range(d // out_chunk):
        cols = slice(c * out_chunk, (c + 1) * out_chunk)
        o_ref[:, cols] = x_ref[:, cols] + jnp.dot(act_s[...], wd_ref[:, cols], preferred_element_type=F32)


def _ffn(x2, g, wg, wu, wd):
    t, d = x2.shape
    dff = wg.shape[1]
    tm = 512
    resident = lambda shape: pl.BlockSpec(shape, lambda i: (0, 0), pipeline_mode=pl.Buffered(1))
    return pl.pallas_call(
        functools.partial(_ffn_kernel, ff_chunk=256, out_chunk=256),
        grid=(t // tm,),
        in_specs=[pl.BlockSpec((tm, d), lambda i: (i, 0)),
                  pl.BlockSpec((1, d), lambda i: (0, 0)),
                  resident((d, dff)), resident((d, dff)), resident((dff, d))],
        out_specs=pl.BlockSpec((tm, d), lambda i: (i, 0)),
        out_shape=jax.ShapeDtypeStruct((t, d), F32),
        scratch_shapes=[pltpu.VMEM((tm, dff), BF16)],
        compiler_params=pltpu.CompilerParams(dimension_semantics=("parallel",), vmem_limit_bytes=VMEM_LIMIT),
        name="ffn",
    )(x2, g, wg, wu, wd)


def _permute_w_in_kernel(w_ref, o_ref):
    rows = w_ref.shape[0]
    n_rest = P_COLS - COL_CVB
    o_ref[:, 0:COL_GATE] = w_ref[:, 0:COL_GATE].astype(BF16)
    lane = lax.broadcasted_iota(jnp.int32, (rows, LANES), 1)
    o_ref[:, COL_GATE:COL_GATE + LANES] = jnp.where(lane < N_GATES, w_ref[:, COL_GATE:COL_GATE + LANES], 0.0).astype(BF16)
    o_ref[:, COL_CVB:P_COLS] = w_ref[:, COL_GATE + N_GATES:COL_GATE + N_GATES + n_rest].astype(BF16)


def _permute_w_in(w_in):
    d, cols = w_in.shape
    tr = 256
    return pl.pallas_call(
        _permute_w_in_kernel,
        grid=(d // tr,),
        in_specs=[pl.BlockSpec((tr, cols), lambda i: (i, 0))],
        out_specs=pl.BlockSpec((tr, P_COLS), lambda i: (i, 0)),
        out_shape=jax.ShapeDtypeStruct((d, P_COLS), BF16),
        compiler_params=pltpu.CompilerParams(dimension_semantics=("parallel",), vmem_limit_bytes=VMEM_LIMIT),
        name="permute_w_in",
    )(w_in)


def _layer(x2, batch, seq, g_mix, w_in, b_gate, g_q_nsa, g_k_cmp, g_k_slc, g_k_win, pe_k_cmp, pe_v_cmp,
           w1_k_cmp, w2_k_cmp, w1_v_cmp, w2_v_cmp, conv_w, g_q_dil, g_k_dil, g_out, w_out,
           g_ffn, w_gate, w_up, w_down):
    row = lambda v: v.reshape(1, -1)
    per_group = lambda g, n: jnp.tile(g, n).reshape(1, -1)
    p2 = _inproj(x2, row(g_mix), _permute_w_in(w_in))

    pe, w1, w2 = _compress_weights(pe_k_cmp, pe_v_cmp, w1_k_cmp, w2_k_cmp, w1_v_cmp, w2_v_cmp)
    kc, vc = _compress(p2, pe, w1, w2, per_group(g_k_cmp, 2), batch, seq)
    ocmp, oslc = _nsa(p2, kc, vc, per_group(g_q_nsa, N_HEADS), per_group(g_k_slc, 2), batch, seq)
    owin = _banded(p2, per_group(g_q_nsa, N_HEADS), per_group(g_k_win, 2), jnp.asarray(_band_bias_tiles(NSA_WINDOW - 1, _window_mult)),
                   batch, seq, window=NSA_WINDOW - 1, slopes=SLOPES_A, shared_kv=True,
                   col_q=COL_QA, col_k=COL_KWV, col_v=COL_KWV, name="nsa_window")
    od = _banded(p2, per_group(g_q_dil, N_HEADS), per_group(g_k_dil, N_HEADS),
                 jnp.asarray(_band_bias_tiles(DILATED_CONFIGS[-1][0], _dilated_mult)),
                 batch, seq, window=DILATED_CONFIGS[-1][0], slopes=SLOPES_D, shared_kv=False,
                 col_q=COL_QD, col_k=COL_KD, col_v=COL_VD, name="dilated")
    oc = _stick(p2, batch, seq)

    bg = jnp.zeros((1, LANES), F32).at[0, :N_GATES].set(b_gate)
    cwp = jnp.zeros((8, GROUP_W), F32).at[:CONV_K].set(conv_w)
    x1 = _mixout(x2, p2, ocmp, oslc, owin, oc, od, bg, cwp, row(g_out), w_out.astype(BF16), seq)
    return _ffn(x1, row(g_ffn), w_gate.astype(BF16), w_up.astype(BF16), w_down.astype(BF16))


def kernel(x, g_mix, w_in, b_gate, g_q_nsa, g_k_cmp, g_k_slc, g_k_win, pe_k_cmp, pe_v_cmp, w1_k_cmp, w2_k_cmp,
           w1_v_cmp, w2_v_cmp, conv_w, g_q_dil, g_k_dil, g_out, w_out, g_ffn, w_gate, w_up, w_down):
    batch, seq, d = x.shape
    assert seq % (CMP_STRIDE * LANES) == 0 and d % LANES == 0
    x2 = x.reshape(batch * seq, d)
    params = (g_mix, w_in, b_gate, g_q_nsa, g_k_cmp, g_k_slc, g_k_win, pe_k_cmp, pe_v_cmp, w1_k_cmp, w2_k_cmp,
              w1_v_cmp, w2_v_cmp, conv_w, g_q_dil, g_k_dil, g_out, w_out, g_ffn, w_gate, w_up, w_down)
    for layer in range(g_mix.shape[0]):
        x2 = _layer(x2, batch, seq, *[p[layer] for p in params])
    return x2.reshape(batch, seq, d)
```

```python
import functools

import numpy as np
import jax
import jax.numpy as jnp
from jax import lax
from jax.experimental import pallas as pl
from jax.experimental.pallas import tpu as pltpu

F32 = jnp.float32
BF16 = jnp.bfloat16

HEAD_DIM = 64
N_HEADS = 4
GROUP_W = N_HEADS * HEAD_DIM
CONV_K = 3
CMP_LEN = 32
CMP_STRIDE = 16
SEL_BLOCK = 64
N_SELECT = 16
NSA_WINDOW = 512
DILATED_CONFIGS = ((128, 1), (512, 4), (2048, 16))
NEG_INF = -1e30
FORCE_SCORE = 1e6
RMS_EPS = 1e-6

TQ = 512
TK = 256
N_TOP = TQ // TK
STICK_TQ = 256
LANES = 128
MXU_DEPTH = 256
MIX_CHAIN = 256
MXU_AHEAD = 2
FLASH_AHEAD = 2
SUM_PAD = 16
LOG2E = 1.4426950408889634
VMEM_LIMIT = 52 * 1024 * 1024

COL_QA, COL_KVC, COL_KSV, COL_KWV, COL_GATE = 0, 256, 384, 512, 640
COL_CVB, COL_CVC, COL_CVU = 768, 1024, 1280
COL_QC, COL_KC, COL_VC = 1536, 1792, 2048
COL_QD, COL_KD, COL_VD = 2304, 2560, 2816
P_COLS = 3072
N_GATES = 12

_NT = (((1,), (1,)), ((), ()))


def _alibi_slopes():
    s = [2.0 ** (-8.0 * i / 8) for i in range(1, 9)]
    return tuple(s[0::2]), tuple(s[1::2])


SLOPES_A, SLOPES_D = _alibi_slopes()


def _rms(x, g):
    return x * lax.rsqrt(jnp.mean(x * x, axis=-1, keepdims=True) + RMS_EPS) * g


def _exact_dot(a, sel):
    hi = a.astype(BF16)
    lo = (a - hi.astype(F32)).astype(BF16)
    if 2 * a.shape[1] <= MXU_DEPTH:
        return jnp.dot(jnp.concatenate([hi, lo], axis=1), jnp.concatenate([sel, sel], axis=0),
                       preferred_element_type=F32)
    return jnp.dot(hi, sel, preferred_element_type=F32) + jnp.dot(lo, sel, preferred_element_type=F32)


def _group_rms(x, g):
    w = x.shape[1]
    same = ((lax.broadcasted_iota(jnp.int32, (w, w), 0) >> 6)
            == (lax.broadcasted_iota(jnp.int32, (w, w), 1) >> 6)).astype(BF16)
    ssq = _exact_dot(x * x, same)
    return x * lax.rsqrt(ssq * (1.0 / HEAD_DIM) + RMS_EPS) * g


EVEN_AUG, ODD_AUG = HEAD_DIM, 0


def _key_aug(pos, first):
    lane = lax.broadcasted_iota(jnp.int32, (pos.shape[0], LANES), 1) - first
    hi = (pos >> 6).astype(F32)
    lo = (pos & 63).astype(F32)
    return jnp.where(lane == 0, hi, jnp.where(lane == 1, lo, jnp.where((lane == 2) | (lane == 3), 1.0, 0.0)))


def _query_aug(t, slope, first):
    lane = lax.broadcasted_iota(jnp.int32, (t.shape[0], LANES), 1) - first
    hi = (t >> 6).astype(F32) * (-64.0 * slope)
    lo = (t & 63).astype(F32) * (-slope)
    return jnp.where(lane == 0, 64.0 * slope,
                     jnp.where(lane == 1, slope, jnp.where(lane == 2, hi, jnp.where(lane == 3, lo, 0.0))))


def _pair_keys(pair, pos):
    lane = lax.broadcasted_iota(jnp.int32, pair.shape, 1)
    aug_e = 0.0 if pos is None else _key_aug(pos, EVEN_AUG)
    aug_o = 0.0 if pos is None else _key_aug(pos, ODD_AUG)
    return (jnp.where(lane < HEAD_DIM, pair, aug_e).astype(BF16),
            jnp.where(lane >= HEAD_DIM, pair, aug_o).astype(BF16))


def _prep_queries(q, g, t0, slopes, qp_s):
    qn = (q if g is None else _group_rms(q, g)) * (HEAD_DIM ** -0.5)
    tpos = t0 + lax.broadcasted_iota(jnp.int32, (q.shape[0], 1), 0)
    lane = lax.broadcasted_iota(jnp.int32, (q.shape[0], LANES), 1)
    for i in range(N_HEADS // 2):
        pair = qn[:, i * LANES:(i + 1) * LANES]
        if slopes is None:
            qp_s[i] = pair.astype(BF16)
            continue
        even = jnp.where((lane >= EVEN_AUG) & (lane < EVEN_AUG + 4), _query_aug(tpos, slopes[2 * i], EVEN_AUG), pair)
        odd = jnp.where(lane < ODD_AUG + 4, _query_aug(tpos, slopes[2 * i + 1], ODD_AUG), pair)
        qp_s[2 * i] = even.astype(BF16)
        qp_s[2 * i + 1] = odd.astype(BF16)


def _inproj_kernel(x_ref, g_ref, w_ref, o_ref, *, nchunk):
    h = _rms(x_ref[...], g_ref[...]).astype(BF16)
    cw = P_COLS // nchunk
    for c in range(nchunk):
        o_ref[:, c * cw:(c + 1) * cw] = jnp.dot(h, w_ref[:, c * cw:(c + 1) * cw], preferred_element_type=F32)


def _inproj(x2, g, w):
    t, d = x2.shape
    tm = 512
    return pl.pallas_call(
        functools.partial(_inproj_kernel, nchunk=6),
        grid=(t // tm,),
        in_specs=[pl.BlockSpec((tm, d), lambda i: (i, 0)),
                  pl.BlockSpec((1, d), lambda i: (0, 0)),
                  pl.BlockSpec((d, P_COLS), lambda i: (0, 0))],
        out_specs=pl.BlockSpec((tm, P_COLS), lambda i: (i, 0)),
        out_shape=jax.ShapeDtypeStruct((t, P_COLS), F32),
        compiler_params=pltpu.CompilerParams(dimension_semantics=("parallel",), vmem_limit_bytes=VMEM_LIMIT),
        name="inproj",
    )(x2, g, w)


def _gelu_tanh(x):
    return x * (0.5 * (1.0 + jnp.tanh(np.sqrt(2.0 / np.pi).astype(np.float32) * (x + 0.044715 * (x * x * x)))))


def _compress_kernel(kv_ref, pe_ref, w1_ref, w2_ref, gk_ref, kc_ref, vct_ref):
    nc = kv_ref.shape[0] // CMP_STRIDE
    first = None
    second = None
    for j in range(CMP_STRIDE):
        tok = kv_ref[pl.ds(j, nc, stride=CMP_STRIDE), :]
        a = jnp.dot((tok + pe_ref[j:j + 1, :]).astype(BF16), w1_ref[j], preferred_element_type=F32)
        b = jnp.dot((tok + pe_ref[CMP_STRIDE + j:CMP_STRIDE + j + 1, :]).astype(BF16), w1_ref[CMP_STRIDE + j],
                    preferred_element_type=F32)
        first = a if first is None else first + a
        second = b if second is None else second + b
    hid = first + pltpu.roll(second, nc - 1, 0)
    out = jnp.dot(_gelu_tanh(hid).astype(BF16), w2_ref[...], preferred_element_type=F32)
    kn = _group_rms(out, gk_ref[...])
    end = lax.broadcasted_iota(jnp.int32, (nc, 1), 0) * CMP_STRIDE + (CMP_LEN - 1)
    even, _ = _pair_keys(kn, end)
    _, odd = _pair_keys(pltpu.roll(kn, HEAD_DIM, 1), end)
    kc_ref[0] = jnp.concatenate([even, odd], axis=1)
    vct_ref[0] = out.T[HEAD_DIM:, :].astype(BF16)


def _compress(p2, pe, w1, w2, gk, batch, seq):
    nc = seq // CMP_STRIDE
    hid2 = w1.shape[2]
    return pl.pallas_call(
        _compress_kernel,
        grid=(batch,),
        in_specs=[pl.BlockSpec((seq, LANES), lambda i: (i, COL_KVC // LANES)),
                  pl.BlockSpec((CMP_LEN, LANES), lambda i: (0, 0)),
                  pl.BlockSpec((CMP_LEN, LANES, hid2), lambda i: (0, 0, 0)),
                  pl.BlockSpec((hid2, LANES), lambda i: (0, 0)),
                  pl.BlockSpec((1, LANES), lambda i: (0, 0))],
        out_specs=[pl.BlockSpec((1, nc, 2 * LANES), lambda i: (i, 0, 0)),
                   pl.BlockSpec((1, HEAD_DIM, nc), lambda i: (i, 0, 0))],
        out_shape=[jax.ShapeDtypeStruct((batch, nc, 2 * LANES), BF16),
                   jax.ShapeDtypeStruct((batch, HEAD_DIM, nc), BF16)],
        compiler_params=pltpu.CompilerParams(dimension_semantics=("parallel",), vmem_limit_bytes=VMEM_LIMIT),
        name="nsa_compress",
    )(p2, pe, w1, w2, gk)


def _compress_weights(pe_k, pe_v, w1_k, w2_k, w1_v, w2_v):
    hid = w1_k.shape[1]
    w1k = w1_k.reshape(CMP_LEN, HEAD_DIM, hid)
    w1v = w1_v.reshape(CMP_LEN, HEAD_DIM, hid)
    z1 = jnp.zeros_like(w1k)
    w1 = jnp.concatenate([jnp.concatenate([w1k, z1], axis=2), jnp.concatenate([z1, w1v], axis=2)], axis=1)
    z2 = jnp.zeros_like(w2_k)
    w2 = jnp.concatenate([jnp.concatenate([w2_k, z2], axis=1), jnp.concatenate([z2, w2_v], axis=1)], axis=0)
    pe = jnp.concatenate([pe_k, pe_v], axis=1)
    return pe, w1.astype(BF16), w2.astype(BF16)


def _flash_scores(kp_of, bias, qp_s, s_s, m_cur):
    m_next = []
    for h in range(N_HEADS):
        s = lax.dot_general(kp_of(h), qp_s[h], _NT, preferred_element_type=F32) + bias
        s_s[h] = s
        m_next.append(jnp.maximum(m_cur[h], jnp.max(s, axis=0, keepdims=True)))
    return tuple(m_next)


def _flash_accum(vt_of, s_s, acc_s, m_prev, m_cur, ls):
    new_l = []
    for h in range(N_HEADS):
        alpha = jnp.exp(m_prev[h] - m_cur[h])
        p = jnp.exp(s_s[h] - m_cur[h])
        new_l.append(alpha * ls[h] + jnp.sum(p, axis=0, keepdims=True))
        acc_s[h] = alpha * acc_s[h] + jnp.dot(vt_of(h), p.astype(BF16), preferred_element_type=F32)
    return tuple(new_l)


def _flash_accum_and_scores(kp_of, bias, vt_of, qp_s, s_s, acc_s, m_prev, m_cur, ls):
    def scores(h):
        return lax.dot_general(kp_of(h), qp_s[h], _NT, preferred_element_type=F32) + bias

    m_next, new_l = [], []
    s_new = {h: scores(h) for h in range(FLASH_AHEAD)}
    for h in range(N_HEADS):
        alpha = jnp.exp(m_prev[h] - m_cur[h])
        p = jnp.exp(s_s[h] - m_cur[h])
        new_l.append(alpha * ls[h] + jnp.sum(p, axis=0, keepdims=True))
        acc_s[h] = alpha * acc_s[h] + jnp.dot(vt_of(h), p.astype(BF16), preferred_element_type=F32)
        s_h = s_new.pop(h)
        s_s[h] = s_h
        m_next.append(jnp.maximum(m_cur[h], jnp.max(s_h, axis=0, keepdims=True)))
        if h + FLASH_AHEAD < N_HEADS:
            s_new[h + FLASH_AHEAD] = scores(h + FLASH_AHEAD)
    return tuple(m_next), tuple(new_l)


def _flash_pipelined(tile_of, n_tiles, kp_fn, bias_fn, vt_fn, qp_s, s_s, acc_s, o_ref):
    acc_s[...] = jnp.zeros(acc_s.shape, F32)
    m0 = tuple(jnp.full((1, TQ), NEG_INF, F32) for _ in range(N_HEADS))
    l0 = tuple(jnp.zeros((1, TQ), F32) for _ in range(N_HEADS))
    first_kt = tile_of(0)
    m1 = _flash_scores(kp_fn(first_kt), bias_fn(first_kt, True), qp_s, s_s, m0)

    def step(i, carry, top):
        m_prev, m_cur, ls = carry
        nxt = tile_of(i + 1)
        m_next, ls = _flash_accum_and_scores(kp_fn(nxt), bias_fn(nxt, top), vt_fn(tile_of(i)), qp_s, s_s, acc_s,
                                             m_prev, m_cur, ls)
        return m_cur, m_next, ls

    carry = (m0, m1, l0)
    for i in range(N_TOP - 1):
        carry = step(i, carry, True)
    m_prev, m_cur, ls = lax.fori_loop(N_TOP - 1, n_tiles - 1, lambda i, c: step(i, c, False), carry)
    ls = _flash_accum(vt_fn(tile_of(n_tiles - 1)), s_s, acc_s, m_prev, m_cur, ls)
    o_ref[0] = jnp.concatenate([acc_s[h] * (1.0 / ls[h]) for h in range(N_HEADS)], axis=0)


def _nsa_kernel(q_ref, kc_ref, vct_ref, ksv_ref, gq_ref, gks_ref, ocmp_ref, oslc_ref,
                ksp_s, vst_s, qp_s, imp_s, sel_s, s_s, acc_s, rank_s, hit_s, flag_s, *, seq):
    qi = pl.program_id(1)
    nc = seq // CMP_STRIDE
    nsel = seq // SEL_BLOCK
    nkt = seq // TK

    @pl.when(qi == 0)
    def _prep():
        for c in range(nkt):
            rows = slice(c * TK, (c + 1) * TK)
            blk = ksv_ref[rows, :]
            pos = c * TK + lax.broadcasted_iota(jnp.int32, (TK, 1), 0)
            kn = _group_rms(blk, gks_ref[...])
            ksp_s[rows, 0:LANES], _ = _pair_keys(kn, pos)
            _, ksp_s[rows, LANES:2 * LANES] = _pair_keys(pltpu.roll(kn, HEAD_DIM, 1), pos)
            vst_s[c] = blk.T[HEAD_DIM:, :].astype(BF16)
        imp_s[:, 0:8, :] = jnp.zeros((TQ // LANES, 8, LANES), F32)

    t0 = qi * TQ
    _prep_queries(q_ref[...], gq_ref[...], t0, SLOPES_A, qp_s)

    kc = [kc_ref[0, :, 0:LANES], kc_ref[0, :, LANES:2 * LANES]]
    vct = vct_ref[0]
    tq_row = t0 + lax.broadcasted_iota(jnp.int32, (nc, TQ), 1)
    n_idx = lax.broadcasted_iota(jnp.int32, (nc, TQ), 0)
    vis = (tq_row >= n_idx * CMP_STRIDE + (CMP_LEN - 1)) & (n_idx < nc - 1)
    vis_bias = jnp.where(vis, 0.0, NEG_INF)
    sees_any = (t0 + lax.broadcasted_iota(jnp.int32, (1, TQ), 1) >= CMP_LEN - 1).astype(F32)
    imp = jnp.zeros((nc, TQ), F32)
    ocmp_t = []
    scores = [lax.dot_general(kc[h % 2], qp_s[h], _NT, preferred_element_type=F32) + vis_bias
              for h in range(N_HEADS)]
    for h in range(N_HEADS):
        sc = scores[h]
        e = jnp.exp(sc - jnp.max(sc, axis=0, keepdims=True))
        p = e * (sees_any / jnp.sum(e, axis=0, keepdims=True))
        ocmp_t.append(jnp.dot(vct, p.astype(BF16), preferred_element_type=F32))
        imp = imp + p
    ocmp_ref[0] = jnp.concatenate(ocmp_t, axis=0)

    halves = []
    for half in range(TQ // LANES):
        imp_s[half, 8:8 + nc, :] = imp[:, half * LANES:(half + 1) * LANES]
        r = [imp_s[half, pl.ds(8 + k, nsel, stride=4), :] for k in range(4)]
        rm1 = imp_s[half, pl.ds(7, nsel, stride=4), :]
        halves.append(rm1 + 2.0 * (r[0] + r[1] + r[2]) + r[3])
    imp_blk = jnp.concatenate(halves, axis=1)
    blk = lax.broadcasted_iota(jnp.int32, (nsel, TQ), 0)
    tl = t0 + lax.broadcasted_iota(jnp.int32, (nsel, TQ), 1)
    cur = tl >> 6
    forced = (blk == 0) | (blk == cur) | (blk == cur - 1)
    valid = blk * SEL_BLOCK <= tl
    score = jnp.where(forced, FORCE_SCORE, jnp.where(valid, imp_blk, -FORCE_SCORE))
    sub = 8
    per_tile = TK // SEL_BLOCK
    n_live = (t0 + TQ - 1) // (SEL_BLOCK * sub) + 1
    for kt in range(nkt):
        flag_s[kt] = NEG_INF

    def count_beats(g, jg):
        mine = score[g * sub:(g + 1) * sub, :]
        blk_g = g * sub + lax.broadcasted_iota(jnp.int32, (sub, TQ), 0)
        rank = rank_s[...]
        for j in range(jg * sub, (jg + 1) * sub):
            row = score[j:j + 1, :]
            if jg < g:
                beats = row >= mine
            elif jg > g:
                beats = row > mine
            else:
                beats = (row > mine) | ((row == mine) & (blk_g > j))
            rank = rank + jnp.where(beats, 1.0, 0.0)
        rank_s[...] = rank

    def rank_group(g):
        rank_s[...] = jnp.zeros((sub, TQ), F32)
        for jg in range(nsel // sub):
            pl.when(jg < n_live)(functools.partial(count_beats, g, jg))
        sel_g = jnp.where(rank_s[...] < min(N_SELECT, nsel), 0.0, NEG_INF)
        sel_s[g * sub:(g + 1) * sub, :] = sel_g
        for part in range(sub // per_tile):
            flag_s[g * (sub // per_tile) + part] = jnp.max(sel_g[part * per_tile:(part + 1) * per_tile, :])

    for g in range(nsel // sub):
        pl.when(g < n_live)(functools.partial(rank_group, g))

    n_hit = jnp.int32(0)
    for kt in reversed(range(nkt)):
        hit_s[n_hit] = kt
        n_hit = n_hit + ((kt * TK < t0) & (flag_s[kt] > -1.0)).astype(jnp.int32)

    def block_bias(kt):
        return jnp.concatenate([jnp.broadcast_to(sel_s[pl.ds(kt * per_tile + j, 1), :], (SEL_BLOCK, TQ))
                                for j in range(per_tile)], axis=0)

    k_local = lax.broadcasted_iota(jnp.int32, (TK, TQ), 0)
    q_local = lax.broadcasted_iota(jnp.int32, (TK, TQ), 1)
    top_kt = qi * N_TOP + (N_TOP - 1)

    def kp_fn(kt):
        off = pl.multiple_of(kt * TK, TK)
        return lambda h: ksp_s[pl.ds(off, TK), (h % 2) * LANES:(h % 2 + 1) * LANES]

    def bias_fn(kt, top):
        if top:
            return jnp.where(kt * TK - t0 + k_local <= q_local, block_bias(kt), NEG_INF)
        return block_bias(kt)

    tile_of = lambda i: jnp.where(i < N_TOP, top_kt - i, hit_s[jnp.maximum(i - N_TOP, 0)])
    _flash_pipelined(tile_of, n_hit + N_TOP, kp_fn, bias_fn, lambda kt: (lambda h: vst_s[kt]), qp_s, s_s, acc_s,
                     oslc_ref)


def _nsa(p2, kc, vc, gq, gks, batch, seq):
    nq = seq // TQ
    nc = seq // CMP_STRIDE
    t = batch * seq
    return pl.pallas_call(
        functools.partial(_nsa_kernel, seq=seq),
        grid=(batch, nq),
        in_specs=[pl.BlockSpec((TQ, GROUP_W), lambda b, i: (b * nq + i, COL_QA // GROUP_W)),
                  pl.BlockSpec((1, nc, 2 * LANES), lambda b, i: (b, 0, 0)),
                  pl.BlockSpec((1, HEAD_DIM, nc), lambda b, i: (b, 0, 0)),
                  pl.BlockSpec((seq, LANES), lambda b, i: (b, COL_KSV // LANES)),
                  pl.BlockSpec((1, GROUP_W), lambda b, i: (0, 0)),
                  pl.BlockSpec((1, LANES), lambda b, i: (0, 0))],
        out_specs=[pl.BlockSpec((1, GROUP_W, TQ), lambda b, i: (b * nq + i, 0, 0)),
                   pl.BlockSpec((1, GROUP_W, TQ), lambda b, i: (b * nq + i, 0, 0))],
        out_shape=[jax.ShapeDtypeStruct((t // TQ, GROUP_W, TQ), F32),
                   jax.ShapeDtypeStruct((t // TQ, GROUP_W, TQ), F32)],
        scratch_shapes=[pltpu.VMEM((seq, 2 * LANES), BF16),
                        pltpu.VMEM((seq // TK, HEAD_DIM, TK), BF16),
                        pltpu.VMEM((N_HEADS, TQ, LANES), BF16),
                        pltpu.VMEM((TQ // LANES, 8 + nc, LANES), F32),
                        pltpu.VMEM((seq // SEL_BLOCK, TQ), F32),
                        pltpu.VMEM((N_HEADS, TK, TQ), F32),
                        pltpu.VMEM((N_HEADS, HEAD_DIM, TQ), F32),
                        pltpu.VMEM((8, TQ), F32),
                        pltpu.SMEM((seq // TK + 1,), jnp.int32),
                        pltpu.SMEM((seq // TK,), F32)],
        compiler_params=pltpu.CompilerParams(dimension_semantics=("parallel", "arbitrary"),
                                             vmem_limit_bytes=VMEM_LIMIT),
        name="nsa_cmp_slc",
    )(p2, kc, vc, p2, gq, gks)


def _band_bias_tiles(window, mult_fn):
    nd = (window + TQ - 1) // TK + 1
    kk = np.arange(TK)[:, None]
    qq = np.arange(TQ)[None, :]
    tiles = np.empty((nd, TK, TQ), np.float32)
    for di in range(nd):
        d = (di - (N_TOP - 1)) * TK + qq - kk
        mult = mult_fn(d)
        tiles[di] = np.where(mult > 0, np.log(np.maximum(mult, 1)), NEG_INF)
    return tiles


def _window_mult(d):
    return ((d >= 0) & (d <= NSA_WINDOW - 1)).astype(np.float64)


def _dilated_mult(d):
    m = np.zeros(d.shape, np.float64)
    for window, dil in DILATED_CONFIGS:
        m += ((d >= 0) & (d <= window) & (d % dil == 0)).astype(np.float64)
    return m


def _banded_kernel(*refs, seq, window, slopes, shared_kv):
    if shared_kv:
        q_ref, kv_ref, gq_ref, gk_ref, bias_ref, o_ref, kp_s, vt_s, qp_s, s_s, acc_s = refs
    else:
        q_ref, k_ref, v_ref, gq_ref, gk_ref, bias_ref, o_ref, kp_s, vt_s, qp_s, s_s, acc_s = refs
    qi = pl.program_id(1)
    nkt = seq // TK

    @pl.when(qi == 0)
    def _prep():
        for c in range(nkt):
            rows = slice(c * TK, (c + 1) * TK)
            pos = c * TK + lax.broadcasted_iota(jnp.int32, (TK, 1), 0)
            if shared_kv:
                blk = kv_ref[rows, :]
                kn = _group_rms(blk, gk_ref[...])
                kp_s[rows, 0:LANES], _ = _pair_keys(kn, pos)
                _, kp_s[rows, LANES:2 * LANES] = _pair_keys(pltpu.roll(kn, HEAD_DIM, 1), pos)
                vt_s[c] = blk.T[HEAD_DIM:, :].astype(BF16)
            else:
                kn = _group_rms(k_ref[rows, :], gk_ref[...])
                for i in range(N_HEADS // 2):
                    even, odd = _pair_keys(kn[:, i * LANES:(i + 1) * LANES], pos)
                    kp_s[rows, 2 * i * LANES:(2 * i + 1) * LANES] = even
                    kp_s[rows, (2 * i + 1) * LANES:(2 * i + 2) * LANES] = odd
                vt_s[c] = v_ref[rows, :].T.astype(BF16)

    t0 = qi * TQ
    _prep_queries(q_ref[...], gq_ref[...], t0, slopes, qp_s)
    kt_lo = jnp.maximum(t0 - window, 0) // TK

    def kp_fn(kt):
        off = pl.multiple_of(kt * TK, TK)
        if shared_kv:
            return lambda h: kp_s[pl.ds(off, TK), (h % 2) * LANES:(h % 2 + 1) * LANES]
        return lambda h: kp_s[pl.ds(off, TK), h * LANES:(h + 1) * LANES]

    def vt_fn(kt):
        if shared_kv:
            return lambda h: vt_s[kt]
        return lambda h: vt_s[kt, h * HEAD_DIM:(h + 1) * HEAD_DIM, :]

    top_kt = qi * N_TOP + (N_TOP - 1)
    _flash_pipelined(lambda i: top_kt - i, top_kt + 1 - kt_lo, kp_fn, lambda kt, top: bias_ref[top_kt - kt], vt_fn,
                     qp_s, s_s, acc_s, o_ref)


def _banded(p2, gq, gk, bias_tiles, batch, seq, *, window, slopes, shared_kv, col_q, col_k, col_v, name):
    nq = seq // TQ
    t = batch * seq
    nd = bias_tiles.shape[0]
    n_kv = 1 if shared_kv else N_HEADS
    q_spec = pl.BlockSpec((TQ, GROUP_W), lambda b, i: (b * nq + i, col_q // GROUP_W))
    gain = lambda g: pl.BlockSpec(g.shape, lambda b, i: (0, 0))
    bias_spec = pl.BlockSpec((nd, TK, TQ), lambda b, i: (0, 0, 0))
    if shared_kv:
        kv_specs = [pl.BlockSpec((seq, LANES), lambda b, i: (b, col_k // LANES))]
        operands = (p2, p2, gq, gk, bias_tiles)
    else:
        kv_specs = [pl.BlockSpec((seq, GROUP_W), lambda b, i: (b, col_k // GROUP_W)),
                    pl.BlockSpec((seq, GROUP_W), lambda b, i: (b, col_v // GROUP_W))]
        operands = (p2, p2, p2, gq, gk, bias_tiles)
    return pl.pallas_call(
        functools.partial(_banded_kernel, seq=seq, window=window, slopes=slopes, shared_kv=shared_kv),
        grid=(batch, nq),
        in_specs=[q_spec] + kv_specs + [gain(gq), gain(gk), bias_spec],
        out_specs=pl.BlockSpec((1, GROUP_W, TQ), lambda b, i: (b * nq + i, 0, 0)),
        out_shape=jax.ShapeDtypeStruct((t // TQ, GROUP_W, TQ), F32),
        scratch_shapes=[pltpu.VMEM((seq, (2 if shared_kv else N_HEADS) * LANES), BF16),
                        pltpu.VMEM((seq // TK, n_kv * HEAD_DIM, TK), BF16),
                        pltpu.VMEM((N_HEADS, TQ, LANES), BF16),
                        pltpu.VMEM((N_HEADS, TK, TQ), F32),
                        pltpu.VMEM((N_HEADS, HEAD_DIM, TQ), F32)],
        compiler_params=pltpu.CompilerParams(dimension_semantics=("parallel", "arbitrary"),
                                             vmem_limit_bytes=VMEM_LIMIT),
        name=name,
    )(*operands)


def _stick_kernel(q_ref, k_ref, v_ref, o_ref, kp_s, vt_s, qp_s, sp_s, e_s, acc_s, *, seq):
    TQ, N_TOP = STICK_TQ, STICK_TQ // TK
    qi = pl.program_id(1)
    nkt = seq // TK

    @pl.when(qi == 0)
    def _prep():
        for c in range(nkt):
            rows = slice(c * TK, (c + 1) * TK)
            kb = k_ref[rows, :]
            for i in range(N_HEADS // 2):
                even, odd = _pair_keys(kb[:, i * LANES:(i + 1) * LANES], None)
                kp_s[rows, 2 * i * LANES:(2 * i + 1) * LANES] = even
                kp_s[rows, (2 * i + 1) * LANES:(2 * i + 2) * LANES] = odd
            vt_s[c] = v_ref[rows, :].T.astype(BF16)

    _prep_queries(q_ref[...], None, qi * TQ, None, qp_s)
    half = TK // 2
    col = lax.broadcasted_iota(jnp.int32, (half + SUM_PAD, half), 1)
    srow = lax.broadcasted_iota(jnp.int32, (half + SUM_PAD, half), 0)
    sums = ((srow == half) | ((srow < half) & (col > srow))).astype(BF16)
    k_local = lax.broadcasted_iota(jnp.int32, (TK, TQ), 0)
    q_local = lax.broadcasted_iota(jnp.int32, (TK, TQ), 1)
    top_kt = qi * N_TOP + (N_TOP - 1)
    acc_s[...] = jnp.zeros(acc_s.shape, F32)

    def past_mask(tiles_above):
        return tiles_above * TK + k_local < q_local

    def logits(kt, h):
        off = pl.multiple_of(kt * TK, TK)
        return lax.dot_general(kp_s[pl.ds(off, TK), h * LANES:(h + 1) * LANES], qp_s[h // 2], _NT,
                               preferred_element_type=F32)

    def stage(h, z, past):
        sp = jnp.maximum(z, 0.0) + jnp.log(1.0 + jnp.exp2(jnp.abs(z) * (-LOG2E)))
        spm = sp if past is None else jnp.where(past, sp, 0.0)
        sp_s[h] = spm.astype(BF16)
        logsig = z - sp
        e_s[h] = logsig if past is None else jnp.where(past, logsig, NEG_INF)

    def tails(h):
        return [jnp.dot(sums, sp_s[h, b * half:(b + 1) * half, :], preferred_element_type=F32) for b in range(2)]

    def consume(kt, h, w, later):
        later_lo = later + w[1][half:half + 1, :]
        attn = jnp.concatenate([jnp.exp(e_s[h, 0:half, :] - w[0][0:half, :] - later_lo),
                                jnp.exp(e_s[h, half:TK, :] - w[1][0:half, :] - later)], axis=0)
        acc_s[h] += jnp.dot(vt_s[kt, h * HEAD_DIM:(h + 1) * HEAD_DIM, :], attn.astype(BF16),
                            preferred_element_type=F32)
        return later_lo + w[0][half:half + 1, :]

    ahead = MXU_AHEAD
    for h in range(N_HEADS):
        stage(h, logits(top_kt, h), past_mask(N_TOP - 1))

    def step(kt, carry, past):
        laters, w_first, z_first = carry
        w = {0: list(w_first)}
        z = {0: z_first}
        for h in range(1, ahead):
            w[h] = tails(h)
            z[h] = logits(kt - 1, h)
        out = []
        for h in range(N_HEADS):
            out.append(consume(kt, h, w.pop(h), laters[h]))
            stage(h, z.pop(h), past)
            if h + ahead < N_HEADS:
                w[h + ahead] = tails(h + ahead)
                z[h + ahead] = logits(kt - 1, h + ahead)
            elif h + ahead == N_HEADS:
                nxt = (tuple(tails(0)), logits(jnp.maximum(kt - 2, 0), 0))
        return tuple(out), nxt[0], nxt[1]

    carry = (tuple(jnp.zeros((1, TQ), F32) for _ in range(N_HEADS)),
             tuple(tails(0)), logits(jnp.maximum(top_kt - 1, 0), 0))
    for j in range(N_TOP - 1):
        carry = step(top_kt - j, carry, past_mask(N_TOP - 2 - j))
    below = qi * N_TOP
    laters, w_first, _ = lax.fori_loop(0, below, lambda i, c: step(below - i, c, None), carry)
    w = {0: list(w_first)}
    for h in range(1, ahead):
        w[h] = tails(h)
    for h in range(N_HEADS):
        consume(0, h, w.pop(h), laters[h])
        if h + ahead < N_HEADS:
            w[h + ahead] = tails(h + ahead)
    o_ref[0] = acc_s[...].reshape(GROUP_W, TQ)


def _stick(p2, batch, seq):
    TQ = STICK_TQ
    nq = seq // TQ
    t = batch * seq
    return pl.pallas_call(
        functools.partial(_stick_kernel, seq=seq),
        grid=(batch, nq),
        in_specs=[pl.BlockSpec((TQ, GROUP_W), lambda b, i: (b * nq + i, COL_QC // GROUP_W)),
                  pl.BlockSpec((seq, GROUP_W), lambda b, i: (b, COL_KC // GROUP_W)),
                  pl.BlockSpec((seq, GROUP_W), lambda b, i: (b, COL_VC // GROUP_W))],
        out_specs=pl.BlockSpec((1, GROUP_W, TQ), lambda b, i: (b * nq + i, 0, 0)),
        out_shape=jax.ShapeDtypeStruct((t // TQ, GROUP_W, TQ), F32),
        scratch_shapes=[pltpu.VMEM((seq, N_HEADS * LANES), BF16),
                        pltpu.VMEM((seq // TK, GROUP_W, TK), BF16),
                        pltpu.VMEM((N_HEADS // 2, TQ, LANES), BF16),
                        pltpu.VMEM((N_HEADS, TK, TQ), BF16),
                        pltpu.VMEM((N_HEADS, TK, TQ), F32),
                        pltpu.VMEM((N_HEADS, HEAD_DIM, TQ), F32)],
        compiler_params=pltpu.CompilerParams(dimension_semantics=("parallel", "arbitrary"),
                                             vmem_limit_bytes=VMEM_LIMIT),
        name="stick_breaking",
    )(p2, p2, p2)


def _mixout_kernel(x_ref, ocmp_ref, oslc_ref, owin_ref, gate_ref, cvb_ref, cvc_ref, cvu_ref, pc_ref, pu_ref,
                   oc_ref, od_ref, bg_ref, cw_ref, gout_ref, wout_ref, o_ref, *, tiles_per_seq):
    i = pl.program_id(0)
    tm, d = x_ref.shape
    chains = [slice(c * MIX_CHAIN, (c + 1) * MIX_CHAIN) for c in range(tm // MIX_CHAIN)]

    def heads(ref, rows):
        tq = ref.shape[2]
        return ref[rows.start // tq, :, rows.start % tq:rows.start % tq + MIX_CHAIN].T

    cu = cvc_ref[...] * cvu_ref[...]
    prev = jnp.where(i % tiles_per_seq == 0, 0.0, pc_ref[...] * pu_ref[...])
    full = jnp.concatenate([prev, cu], axis=0)
    back1 = pltpu.roll(full, 1, 0)[8:, :]
    back2 = pltpu.roll(full, 2, 0)[8:, :]
    cw = cw_ref[...]
    ob = cvb_ref[...] * (cw[0:1, :] * back2 + cw[1:2, :] * back1 + cw[2:3, :] * cu)

    gates = jax.nn.sigmoid(gate_ref[...] + bg_ref[...])
    src = lax.broadcasted_iota(jnp.int32, (LANES, GROUP_W), 0)
    head3 = (lax.broadcasted_iota(jnp.int32, (LANES, GROUP_W), 1) >> 6) * 3
    branch = [(src == head3 + r).astype(BF16) for r in range(3)]
    assert d // HEAD_DIM <= LANES
    gather = ((lax.broadcasted_iota(jnp.int32, (d, LANES), 0) >> 6)
              == lax.broadcasted_iota(jnp.int32, (d, LANES), 1)).astype(BF16)
    spread = (lax.broadcasted_iota(jnp.int32, (LANES, d), 0)
              == (lax.broadcasted_iota(jnp.int32, (LANES, d), 1) >> 6)).astype(BF16)

    groups, ssq = [], []
    for rows in chains:
        oa = (_exact_dot(gates[rows], branch[0]) * heads(ocmp_ref, rows)
              + _exact_dot(gates[rows], branch[1]) * heads(oslc_ref, rows)
              + _exact_dot(gates[rows], branch[2]) * heads(owin_ref, rows))
        g = jnp.concatenate([oa, ob[rows], heads(oc_ref, rows), heads(od_ref, rows)], axis=1)
        groups.append(g)
        ssq.append(_exact_dot(g * g, gather))
    scale = [_exact_dot(lax.rsqrt(s * (1.0 / HEAD_DIM) + RMS_EPS), spread) for s in ssq]
    for rows, g, sc in zip(chains, groups, scale):
        mixed = (g * sc * gout_ref[...]).astype(BF16)
        o_ref[rows, :] = x_ref[rows, :] + jnp.dot(mixed, wout_ref[...], preferred_element_type=F32)


def _mixout(x2, p2, ocmp, oslc, owin, oc, od, bg, cw, gout, wout, seq):
    t, d = x2.shape
    tm = 2 * MIX_CHAIN
    rows = lambda w, col: pl.BlockSpec((tm, w), lambda i: (i, col // w))
    heads_t = lambda tq: pl.BlockSpec((tm // tq, GROUP_W, tq), lambda i: (i, 0, 0))
    prev8 = lambda col: pl.BlockSpec((8, GROUP_W), lambda i: (jnp.maximum(i * (tm // 8) - 1, 0), col // GROUP_W))
    const = lambda shape: pl.BlockSpec(shape, lambda i: (0, 0))
    assert tm % TQ == 0 and tm % STICK_TQ == 0
    return pl.pallas_call(
        functools.partial(_mixout_kernel, tiles_per_seq=seq // tm),
        grid=(t // tm,),
        in_specs=[rows(d, 0), heads_t(TQ), heads_t(TQ), heads_t(TQ),
                  rows(LANES, COL_GATE), rows(GROUP_W, COL_CVB), rows(GROUP_W, COL_CVC), rows(GROUP_W, COL_CVU),
                  prev8(COL_CVC), prev8(COL_CVU),
                  heads_t(STICK_TQ), heads_t(TQ),
                  const((1, LANES)), const((8, GROUP_W)), const((1, d)), const((d, d))],
        out_specs=rows(d, 0),
        out_shape=jax.ShapeDtypeStruct((t, d), F32),
        compiler_params=pltpu.CompilerParams(dimension_semantics=("parallel",), vmem_limit_bytes=VMEM_LIMIT),
        name="mixout",
    )(x2, ocmp, oslc, owin, p2, p2, p2, p2, p2, p2, oc, od, bg, cw, gout, wout)


def _ffn_kernel(x_ref, g_ref, wg_ref, wu_ref, wd_ref, o_ref, act_s, *, ff_chunk, out_chunk):
    h = _rms(x_ref[...], g_ref[...]).astype(BF16)
    dff = wg_ref.shape[1]
    for c in range(dff // ff_chunk):
        cols = slice(c * ff_chunk, (c + 1) * ff_chunk)
        a = jnp.dot(h, wg_ref[:, cols], preferred_element_type=F32)
        u = jnp.dot(h, wu_ref[:, cols], preferred_element_type=F32)
        act_s[:, cols] = (a * jax.nn.sigmoid(a) * u).astype(BF16)
    d = o_ref.shape[1]
    for c in range(d // out_chunk):
        cols = slice(c * out_chunk, (c + 1) * out_chunk)
        o_ref[:, cols] = x_ref[:, cols] + jnp.dot(act_s[...], wd_ref[:, cols], preferred_element_type=F32)


def _ffn(x2, g, wg, wu, wd):
    t, d = x2.shape
    dff = wg.shape[1]
    tm = 512
    resident = lambda shape: pl.BlockSpec(shape, lambda i: (0, 0), pipeline_mode=pl.Buffered(1))
    return pl.pallas_call(
        functools.partial(_ffn_kernel, ff_chunk=256, out_chunk=256),
        grid=(t // tm,),
        in_specs=[pl.BlockSpec((tm, d), lambda i: (i, 0)),
                  pl.BlockSpec((1, d), lambda i: (0, 0)),
                  resident((d, dff)), resident((d, dff)), resident((dff, d))],
        out_specs=pl.BlockSpec((tm, d), lambda i: (i, 0)),
        out_shape=jax.ShapeDtypeStruct((t, d), F32),
        scratch_shapes=[pltpu.VMEM((tm, dff), BF16)],
        compiler_params=pltpu.CompilerParams(dimension_semantics=("parallel",), vmem_limit_bytes=VMEM_LIMIT),
        name="ffn",
    )(x2, g, wg, wu, wd)


def _permute_w_in_kernel(w_ref, o_ref):
    rows = w_ref.shape[0]
    n_rest = P_COLS - COL_CVB
    o_ref[:, 0:COL_GATE] = w_ref[:, 0:COL_GATE].astype(BF16)
    lane = lax.broadcasted_iota(jnp.int32, (rows, LANES), 1)
    o_ref[:, COL_GATE:COL_GATE + LANES] = jnp.where(lane < N_GATES, w_ref[:, COL_GATE:COL_GATE + LANES], 0.0).astype(BF16)
    o_ref[:, COL_CVB:P_COLS] = w_ref[:, COL_GATE + N_GATES:COL_GATE + N_GATES + n_rest].astype(BF16)


def _permute_w_in(w_in):
    d, cols = w_in.shape
    tr = 256
    return pl.pallas_call(
        _permute_w_in_kernel,
        grid=(d // tr,),
        in_specs=[pl.BlockSpec((tr, cols), lambda i: (i, 0))],
        out_specs=pl.BlockSpec((tr, P_COLS), lambda i: (i, 0)),
        out_shape=jax.ShapeDtypeStruct((d, P_COLS), BF16),
        compiler_params=pltpu.CompilerParams(dimension_semantics=("parallel",), vmem_limit_bytes=VMEM_LIMIT),
        name="permute_w_in",
    )(w_in)


def _layer(x2, batch, seq, g_mix, w_in, b_gate, g_q_nsa, g_k_cmp, g_k_slc, g_k_win, pe_k_cmp, pe_v_cmp,
           w1_k_cmp, w2_k_cmp, w1_v_cmp, w2_v_cmp, conv_w, g_q_dil, g_k_dil, g_out, w_out,
           g_ffn, w_gate, w_up, w_down):
    row = lambda v: v.reshape(1, -1)
    per_group = lambda g, n: jnp.tile(g, n).reshape(1, -1)
    p2 = _inproj(x2, row(g_mix), _permute_w_in(w_in))

    pe, w1, w2 = _compress_weights(pe_k_cmp, pe_v_cmp, w1_k_cmp, w2_k_cmp, w1_v_cmp, w2_v_cmp)
    kc, vc = _compress(p2, pe, w1, w2, per_group(g_k_cmp, 2), batch, seq)
    ocmp, oslc = _nsa(p2, kc, vc, per_group(g_q_nsa, N_HEADS), per_group(g_k_slc, 2), batch, seq)
    owin = _banded(p2, per_group(g_q_nsa, N_HEADS), per_group(g_k_win, 2), jnp.asarray(_band_bias_tiles(NSA_WINDOW - 1, _window_mult)),
                   batch, seq, window=NSA_WINDOW - 1, slopes=SLOPES_A, shared_kv=True,
                   col_q=COL_QA, col_k=COL_KWV, col_v=COL_KWV, name="nsa_window")
    od = _banded(p2, per_group(g_q_dil, N_HEADS), per_group(g_k_dil, N_HEADS),
                 jnp.asarray(_band_bias_tiles(DILATED_CONFIGS[-1][0], _dilated_mult)),
                 batch, seq, window=DILATED_CONFIGS[-1][0], slopes=SLOPES_D, shared_kv=False,
                 col_q=COL_QD, col_k=COL_KD, col_v=COL_VD, name="dilated")
    oc = _stick(p2, batch, seq)

    bg = jnp.zeros((1, LANES), F32).at[0, :N_GATES].set(b_gate)
    cwp = jnp.zeros((8, GROUP_W), F32).at[:CONV_K].set(conv_w)
    x1 = _mixout(x2, p2, ocmp, oslc, owin, oc, od, bg, cwp, row(g_out), w_out.astype(BF16), seq)
    return _ffn(x1, row(g_ffn), w_gate.astype(BF16), w_up.astype(BF16), w_down.astype(BF16))


def kernel(x, g_mix, w_in, b_gate, g_q_nsa, g_k_cmp, g_k_slc, g_k_win, pe_k_cmp, pe_v_cmp, w1_k_cmp, w2_k_cmp,
           w1_v_cmp, w2_v_cmp, conv_w, g_q_dil, g_k_dil, g_out, w_out, g_ffn, w_gate, w_up, w_down):
    batch, seq, d = x.shape
    assert seq % (CMP_STRIDE * LANES) == 0 and d % LANES == 0
    x2 = x.reshape(batch * seq, d)
    params = (g_mix, w_in, b_gate, g_q_nsa, g_k_cmp, g_k_slc, g_k_win, pe_k_cmp, pe_v_cmp, w1_k_cmp, w2_k_cmp,
              w1_v_cmp, w2_v_cmp, conv_w, g_q_dil, g_k_dil, g_out, w_out, g_ffn, w_gate, w_up, w_down)
    for layer in range(g_mix.shape[0]):
        x2 = _layer(x2, batch, seq, *[p[layer] for p in params])
    return x2.reshape(batch, seq, d)
```

```python
import functools

import numpy as np
import jax
import jax.numpy as jnp
from jax import lax
from jax.experimental import pallas as pl
from jax.experimental.pallas import tpu as pltpu

F32 = jnp.float32
BF16 = jnp.bfloat16

HEAD_DIM = 64
N_HEADS = 4
GROUP_W = N_HEADS * HEAD_DIM
CONV_K = 3
CMP_LEN = 32
CMP_STRIDE = 16
SEL_BLOCK = 64
N_SELECT = 16
NSA_WINDOW = 512
DILATED_CONFIGS = ((128, 1), (512, 4), (2048, 16))
NEG_INF = -1e30
FORCE_SCORE = 1e6
RMS_EPS = 1e-6

TQ = 512
TK = 256
N_TOP = TQ // TK
STICK_TQ = 256
LANES = 128
MXU_DEPTH = 256
MIX_CHAIN = 256
MXU_AHEAD = 2
FLASH_AHEAD = 2
SUM_PAD = 16
V_ROWS = HEAD_DIM + SUM_PAD
LOG2E = 1.4426950408889634
VMEM_LIMIT = 52 * 1024 * 1024

COL_QA, COL_KVC, COL_KSV, COL_KWV, COL_GATE = 0, 256, 384, 512, 640
COL_CVB, COL_CVC, COL_CVU = 768, 1024, 1280
COL_QC, COL_KC, COL_VC = 1536, 1792, 2048
COL_QD, COL_KD, COL_VD = 2304, 2560, 2816
P_COLS = 3072
N_GATES = 12

_NT = (((1,), (1,)), ((), ()))


def _alibi_slopes():
    s = [2.0 ** (-8.0 * i / 8) for i in range(1, 9)]
    return tuple(s[0::2]), tuple(s[1::2])


SLOPES_A, SLOPES_D = _alibi_slopes()


def _rms(x, g):
    return x * lax.rsqrt(jnp.mean(x * x, axis=-1, keepdims=True) + RMS_EPS) * g


def _exact_dot(a, sel):
    hi = a.astype(BF16)
    lo = (a - hi.astype(F32)).astype(BF16)
    if 2 * a.shape[1] <= MXU_DEPTH:
        return jnp.dot(jnp.concatenate([hi, lo], axis=1), jnp.concatenate([sel, sel], axis=0),
                       preferred_element_type=F32)
    return jnp.dot(hi, sel, preferred_element_type=F32) + jnp.dot(lo, sel, preferred_element_type=F32)


def _group_rms(x, g):
    w = x.shape[1]
    same = ((lax.broadcasted_iota(jnp.int32, (w, w), 0) >> 6)
            == (lax.broadcasted_iota(jnp.int32, (w, w), 1) >> 6)).astype(BF16)
    ssq = _exact_dot(x * x, same)
    return x * lax.rsqrt(ssq * (1.0 / HEAD_DIM) + RMS_EPS) * g


EVEN_AUG, ODD_AUG = HEAD_DIM, 0


def _key_aug(pos, first):
    lane = lax.broadcasted_iota(jnp.int32, (pos.shape[0], LANES), 1) - first
    hi = (pos >> 6).astype(F32)
    lo = (pos & 63).astype(F32)
    return jnp.where(lane == 0, hi, jnp.where(lane == 1, lo, jnp.where((lane == 2) | (lane == 3), 1.0, 0.0)))


def _query_aug(t, slope, first):
    lane = lax.broadcasted_iota(jnp.int32, (t.shape[0], LANES), 1) - first
    hi = (t >> 6).astype(F32) * (-64.0 * slope)
    lo = (t & 63).astype(F32) * (-slope)
    return jnp.where(lane == 0, 64.0 * slope,
                     jnp.where(lane == 1, slope, jnp.where(lane == 2, hi, jnp.where(lane == 3, lo, 0.0))))


def _pair_keys(pair, pos):
    lane = lax.broadcasted_iota(jnp.int32, pair.shape, 1)
    aug_e = 0.0 if pos is None else _key_aug(pos, EVEN_AUG)
    aug_o = 0.0 if pos is None else _key_aug(pos, ODD_AUG)
    return (jnp.where(lane < HEAD_DIM, pair, aug_e).astype(BF16),
            jnp.where(lane >= HEAD_DIM, pair, aug_o).astype(BF16))


def _prep_queries(q, g, t0, slopes, qp_s):
    qn = (q if g is None else _group_rms(q, g)) * (HEAD_DIM ** -0.5)
    tpos = t0 + lax.broadcasted_iota(jnp.int32, (q.shape[0], 1), 0)
    lane = lax.broadcasted_iota(jnp.int32, (q.shape[0], LANES), 1)
    for i in range(N_HEADS // 2):
        pair = qn[:, i * LANES:(i + 1) * LANES]
        if slopes is None:
            qp_s[i] = pair.astype(BF16)
            continue
        even = jnp.where((lane >= EVEN_AUG) & (lane < EVEN_AUG + 4), _query_aug(tpos, slopes[2 * i], EVEN_AUG), pair)
        odd = jnp.where(lane < ODD_AUG + 4, _query_aug(tpos, slopes[2 * i + 1], ODD_AUG), pair)
        qp_s[2 * i] = even.astype(BF16)
        qp_s[2 * i + 1] = odd.astype(BF16)


def _inproj_kernel(x_ref, g_ref, w_ref, o_ref, *, nchunk):
    h = _rms(x_ref[...], g_ref[...]).astype(BF16)
    cw = P_COLS // nchunk
    for c in range(nchunk):
        o_ref[:, c * cw:(c + 1) * cw] = jnp.dot(h, w_ref[:, c * cw:(c + 1) * cw], preferred_element_type=F32)


def _inproj(x2, g, w):
    t, d = x2.shape
    tm = 512
    return pl.pallas_call(
        functools.partial(_inproj_kernel, nchunk=6),
        grid=(t // tm,),
        in_specs=[pl.BlockSpec((tm, d), lambda i: (i, 0)),
                  pl.BlockSpec((1, d), lambda i: (0, 0)),
                  pl.BlockSpec((d, P_COLS), lambda i: (0, 0))],
        out_specs=pl.BlockSpec((tm, P_COLS), lambda i: (i, 0)),
        out_shape=jax.ShapeDtypeStruct((t, P_COLS), F32),
        compiler_params=pltpu.CompilerParams(dimension_semantics=("parallel",), vmem_limit_bytes=VMEM_LIMIT),
        name="inproj",
    )(x2, g, w)


def _gelu_tanh(x):
    return x * (0.5 * (1.0 + jnp.tanh(np.sqrt(2.0 / np.pi).astype(np.float32) * (x + 0.044715 * (x * x * x)))))


def _compress_kernel(kv_ref, pe_ref, w1_ref, w2_ref, gk_ref, kc_ref, vct_ref):
    nc = kv_ref.shape[0] // CMP_STRIDE
    first = None
    second = None
    for j in range(CMP_STRIDE):
        tok = kv_ref[pl.ds(j, nc, stride=CMP_STRIDE), :]
        a = jnp.dot((tok + pe_ref[j:j + 1, :]).astype(BF16), w1_ref[j], preferred_element_type=F32)
        b = jnp.dot((tok + pe_ref[CMP_STRIDE + j:CMP_STRIDE + j + 1, :]).astype(BF16), w1_ref[CMP_STRIDE + j],
                    preferred_element_type=F32)
        first = a if first is None else first + a
        second = b if second is None else second + b
    hid = first + pltpu.roll(second, nc - 1, 0)
    out = jnp.dot(_gelu_tanh(hid).astype(BF16), w2_ref[...], preferred_element_type=F32)
    kn = _group_rms(out, gk_ref[...])
    end = lax.broadcasted_iota(jnp.int32, (nc, 1), 0) * CMP_STRIDE + (CMP_LEN - 1)
    even, _ = _pair_keys(kn, end)
    _, odd = _pair_keys(pltpu.roll(kn, HEAD_DIM, 1), end)
    kc_ref[0] = jnp.concatenate([even, odd], axis=1)
    vct_ref[0] = out.T[HEAD_DIM:, :].astype(BF16)


def _compress(p2, pe, w1, w2, gk, batch, seq):
    nc = seq // CMP_STRIDE
    hid2 = w1.shape[2]
    return pl.pallas_call(
        _compress_kernel,
        grid=(batch,),
        in_specs=[pl.BlockSpec((seq, LANES), lambda i: (i, COL_KVC // LANES)),
                  pl.BlockSpec((CMP_LEN, LANES), lambda i: (0, 0)),
                  pl.BlockSpec((CMP_LEN, LANES, hid2), lambda i: (0, 0, 0)),
                  pl.BlockSpec((hid2, LANES), lambda i: (0, 0)),
                  pl.BlockSpec((1, LANES), lambda i: (0, 0))],
        out_specs=[pl.BlockSpec((1, nc, 2 * LANES), lambda i: (i, 0, 0)),
                   pl.BlockSpec((1, HEAD_DIM, nc), lambda i: (i, 0, 0))],
        out_shape=[jax.ShapeDtypeStruct((batch, nc, 2 * LANES), BF16),
                   jax.ShapeDtypeStruct((batch, HEAD_DIM, nc), BF16)],
        compiler_params=pltpu.CompilerParams(dimension_semantics=("parallel",), vmem_limit_bytes=VMEM_LIMIT),
        name="nsa_compress",
    )(p2, pe, w1, w2, gk)


def _compress_weights(pe_k, pe_v, w1_k, w2_k, w1_v, w2_v):
    hid = w1_k.shape[1]
    w1k = w1_k.reshape(CMP_LEN, HEAD_DIM, hid)
    w1v = w1_v.reshape(CMP_LEN, HEAD_DIM, hid)
    z1 = jnp.zeros_like(w1k)
    w1 = jnp.concatenate([jnp.concatenate([w1k, z1], axis=2), jnp.concatenate([z1, w1v], axis=2)], axis=1)
    z2 = jnp.zeros_like(w2_k)
    w2 = jnp.concatenate([jnp.concatenate([w2_k, z2], axis=1), jnp.concatenate([z2, w2_v], axis=1)], axis=0)
    pe = jnp.concatenate([pe_k, pe_v], axis=1)
    return pe, w1.astype(BF16), w2.astype(BF16)


def _flash_scores(kp_of, bias, qp_s, s_s, m_cur):
    m_next = []
    for h in range(N_HEADS):
        s = lax.dot_general(kp_of(h), qp_s[h], _NT, preferred_element_type=F32) + bias
        s_s[h] = s
        m_next.append(jnp.maximum(m_cur[h], jnp.max(s, axis=0, keepdims=True)))
    return tuple(m_next)


def _flash_accum(vt_of, s_s, acc_s, m_prev, m_cur):
    for h in range(N_HEADS):
        alpha = jnp.exp(m_prev[h] - m_cur[h])
        p = jnp.exp(s_s[h] - m_cur[h])
        acc_s[h] = alpha * acc_s[h] + jnp.dot(vt_of(h), p.astype(BF16), preferred_element_type=F32)


def _flash_accum_and_scores(kp_of, bias, vt_of, qp_s, s_s, acc_s, m_prev, m_cur):
    def scores(h):
        return lax.dot_general(kp_of(h), qp_s[h], _NT, preferred_element_type=F32) + bias

    m_next = []
    s_new = {h: scores(h) for h in range(FLASH_AHEAD)}
    for h in range(N_HEADS):
        alpha = jnp.exp(m_prev[h] - m_cur[h])
        p = jnp.exp(s_s[h] - m_cur[h])
        acc_s[h] = alpha * acc_s[h] + jnp.dot(vt_of(h), p.astype(BF16), preferred_element_type=F32)
        s_h = s_new.pop(h)
        s_s[h] = s_h
        m_next.append(jnp.maximum(m_cur[h], jnp.max(s_h, axis=0, keepdims=True)))
        if h + FLASH_AHEAD < N_HEADS:
            s_new[h + FLASH_AHEAD] = scores(h + FLASH_AHEAD)
    return tuple(m_next)


def _flash_pipelined(tile_of, n_tiles, kp_fn, bias_fn, vt_fn, qp_s, s_s, acc_s, o_ref):
    acc_s[...] = jnp.zeros(acc_s.shape, F32)
    m0 = tuple(jnp.full((1, TQ), NEG_INF, F32) for _ in range(N_HEADS))
    first_kt = tile_of(0)
    m1 = _flash_scores(kp_fn(first_kt), bias_fn(first_kt, True), qp_s, s_s, m0)

    def step(i, carry, top):
        m_prev, m_cur = carry
        nxt = tile_of(i + 1)
        m_next = _flash_accum_and_scores(kp_fn(nxt), bias_fn(nxt, top), vt_fn(tile_of(i)), qp_s, s_s, acc_s,
                                         m_prev, m_cur)
        return m_cur, m_next

    carry = (m0, m1)
    for i in range(N_TOP - 1):
        carry = step(i, carry, True)
    m_prev, m_cur = lax.fori_loop(N_TOP - 1, n_tiles - 1, lambda i, c: step(i, c, False), carry)
    _flash_accum(vt_fn(tile_of(n_tiles - 1)), s_s, acc_s, m_prev, m_cur)
    o_ref[0] = jnp.concatenate([acc_s[h, 0:HEAD_DIM, :] * (1.0 / acc_s[h, HEAD_DIM:HEAD_DIM + 1, :])
                                for h in range(N_HEADS)], axis=0)


def _values_t(v_t):
    row = lax.broadcasted_iota(jnp.int32, (V_ROWS - HEAD_DIM, v_t.shape[1]), 0)
    return jnp.concatenate([v_t, jnp.where(row == 0, 1.0, 0.0)], axis=0).astype(BF16)


def _nsa_kernel(q_ref, kc_ref, vct_ref, ksv_ref, gq_ref, gks_ref, ocmp_ref, oslc_ref,
                ksp_s, vst_s, qp_s, imp_s, sel_s, s_s, acc_s, hit_s, *, seq):
    qi = pl.program_id(1)
    nc = seq // CMP_STRIDE
    nsel = seq // SEL_BLOCK
    nkt = seq // TK

    @pl.when(qi == 0)
    def _prep():
        for c in range(nkt):
            rows = slice(c * TK, (c + 1) * TK)
            blk = ksv_ref[rows, :]
            pos = c * TK + lax.broadcasted_iota(jnp.int32, (TK, 1), 0)
            kn = _group_rms(blk, gks_ref[...])
            ksp_s[rows, 0:LANES], _ = _pair_keys(kn, pos)
            _, ksp_s[rows, LANES:2 * LANES] = _pair_keys(pltpu.roll(kn, HEAD_DIM, 1), pos)
            vst_s[c] = _values_t(blk.T[HEAD_DIM:, :])
        imp_s[:, 0:8, :] = jnp.zeros((TQ // LANES, 8, LANES), F32)

    t0 = qi * TQ
    _prep_queries(q_ref[...], gq_ref[...], t0, SLOPES_A, qp_s)

    kc = [kc_ref[0, :, 0:LANES], kc_ref[0, :, LANES:2 * LANES]]
    vct = vct_ref[0]
    tq_row = t0 + lax.broadcasted_iota(jnp.int32, (nc, TQ), 1)
    n_idx = lax.broadcasted_iota(jnp.int32, (nc, TQ), 0)
    vis = (tq_row >= n_idx * CMP_STRIDE + (CMP_LEN - 1)) & (n_idx < nc - 1)
    vis_bias = jnp.where(vis, 0.0, NEG_INF)
    sees_any = (t0 + lax.broadcasted_iota(jnp.int32, (1, TQ), 1) >= CMP_LEN - 1).astype(F32)
    imp = jnp.zeros((nc, TQ), F32)
    ocmp_t = []
    scores = [lax.dot_general(kc[h % 2], qp_s[h], _NT, preferred_element_type=F32) + vis_bias
              for h in range(N_HEADS)]
    for h in range(N_HEADS):
        sc = scores[h]
        e = jnp.exp(sc - jnp.max(sc, axis=0, keepdims=True))
        p = e * (sees_any / jnp.sum(e, axis=0, keepdims=True))
        ocmp_t.append(jnp.dot(vct, p.astype(BF16), preferred_element_type=F32))
        imp = imp + p
    ocmp_ref[0] = jnp.concatenate(ocmp_t, axis=0)

    halves = []
    for half in range(TQ // LANES):
        imp_s[half, 8:8 + nc, :] = imp[:, half * LANES:(half + 1) * LANES]
        r = [imp_s[half, pl.ds(8 + k, nsel, stride=4), :] for k in range(4)]
        rm1 = imp_s[half, pl.ds(7, nsel, stride=4), :]
        halves.append(rm1 + 2.0 * (r[0] + r[1] + r[2]) + r[3])
    imp_blk = jnp.concatenate(halves, axis=1)
    blk = lax.broadcasted_iota(jnp.int32, (nsel, TQ), 0)
    tl = t0 + lax.broadcasted_iota(jnp.int32, (nsel, TQ), 1)
    cur = tl >> 6
    forced = (blk == 0) | (blk == cur) | (blk == cur - 1)
    valid = blk * SEL_BLOCK <= tl
    score = jnp.where(forced, FORCE_SCORE, jnp.where(valid, imp_blk, -FORCE_SCORE))
    sub = 8
    per_tile = TK // SEL_BLOCK
    tile_hit = []
    for g in range(nsel // sub):
        mine = score[g * sub:(g + 1) * sub, :]
        blk_g = g * sub + lax.broadcasted_iota(jnp.int32, (sub, TQ), 0)
        rank = jnp.zeros((sub, TQ), F32)
        for j in range(nsel):
            row = score[j:j + 1, :]
            if j < g * sub:
                beats = row >= mine
            elif j >= (g + 1) * sub:
                beats = row > mine
            else:
                beats = (row > mine) | ((row == mine) & (blk_g > j))
            rank = rank + jnp.where(beats, 1.0, 0.0)
        sel_g = jnp.where(rank < min(N_SELECT, nsel), 0.0, NEG_INF)
        sel_s[g * sub:(g + 1) * sub, :] = sel_g
        for part in range(sub // per_tile):
            tile_hit.append(jnp.max(sel_g[part * per_tile:(part + 1) * per_tile, :]))

    n_hit = jnp.int32(0)
    for kt in reversed(range(nkt)):
        hit_s[n_hit] = kt
        n_hit = n_hit + ((tile_hit[kt] > -1.0) & (kt * TK < t0)).astype(jnp.int32)

    def block_bias(kt):
        return jnp.concatenate([jnp.broadcast_to(sel_s[pl.ds(kt * per_tile + j, 1), :], (SEL_BLOCK, TQ))
                                for j in range(per_tile)], axis=0)

    k_local = lax.broadcasted_iota(jnp.int32, (TK, TQ), 0)
    q_local = lax.broadcasted_iota(jnp.int32, (TK, TQ), 1)
    top_kt = qi * N_TOP + (N_TOP - 1)

    def kp_fn(kt):
        off = pl.multiple_of(kt * TK, TK)
        return lambda h: ksp_s[pl.ds(off, TK), (h % 2) * LANES:(h % 2 + 1) * LANES]

    def bias_fn(kt, top):
        if top:
            return jnp.where(kt * TK - t0 + k_local <= q_local, block_bias(kt), NEG_INF)
        return block_bias(kt)

    tile_of = lambda i: jnp.where(i < N_TOP, top_kt - i, hit_s[jnp.maximum(i - N_TOP, 0)])
    _flash_pipelined(tile_of, n_hit + N_TOP, kp_fn, bias_fn, lambda kt: (lambda h: vst_s[kt]), qp_s, s_s, acc_s,
                     oslc_ref)


def _nsa(p2, kc, vc, gq, gks, batch, seq):
    nq = seq // TQ
    nc = seq // CMP_STRIDE
    t = batch * seq
    return pl.pallas_call(
        functools.partial(_nsa_kernel, seq=seq),
        grid=(batch, nq),
        in_specs=[pl.BlockSpec((TQ, GROUP_W), lambda b, i: (b * nq + i, COL_QA // GROUP_W)),
                  pl.BlockSpec((1, nc, 2 * LANES), lambda b, i: (b, 0, 0)),
                  pl.BlockSpec((1, HEAD_DIM, nc), lambda b, i: (b, 0, 0)),
                  pl.BlockSpec((seq, LANES), lambda b, i: (b, COL_KSV // LANES)),
                  pl.BlockSpec((1, GROUP_W), lambda b, i: (0, 0)),
                  pl.BlockSpec((1, LANES), lambda b, i: (0, 0))],
        out_specs=[pl.BlockSpec((1, GROUP_W, TQ), lambda b, i: (b * nq + i, 0, 0)),
                   pl.BlockSpec((1, GROUP_W, TQ), lambda b, i: (b * nq + i, 0, 0))],
        out_shape=[jax.ShapeDtypeStruct((t // TQ, GROUP_W, TQ), F32),
                   jax.ShapeDtypeStruct((t // TQ, GROUP_W, TQ), F32)],
        scratch_shapes=[pltpu.VMEM((seq, 2 * LANES), BF16),
                        pltpu.VMEM((seq // TK, V_ROWS, TK), BF16),
                        pltpu.VMEM((N_HEADS, TQ, LANES), BF16),
                        pltpu.VMEM((TQ // LANES, 8 + nc, LANES), F32),
                        pltpu.VMEM((seq // SEL_BLOCK, TQ), F32),
                        pltpu.VMEM((N_HEADS, TK, TQ), F32),
                        pltpu.VMEM((N_HEADS, V_ROWS, TQ), F32),
                        pltpu.SMEM((seq // TK + 1,), jnp.int32)],
        compiler_params=pltpu.CompilerParams(dimension_semantics=("parallel", "arbitrary"),
                                             vmem_limit_bytes=VMEM_LIMIT),
        name="nsa_cmp_slc",
    )(p2, kc, vc, p2, gq, gks)


def _band_bias_tiles(window, mult_fn):
    nd = (window + TQ - 1) // TK + 1
    kk = np.arange(TK)[:, None]
    qq = np.arange(TQ)[None, :]
    tiles = np.empty((nd, TK, TQ), np.float32)
    for di in range(nd):
        d = (di - (N_TOP - 1)) * TK + qq - kk
        mult = mult_fn(d)
        tiles[di] = np.where(mult > 0, np.log(np.maximum(mult, 1)), NEG_INF)
    return tiles


def _window_mult(d):
    return ((d >= 0) & (d <= NSA_WINDOW - 1)).astype(np.float64)


def _dilated_mult(d):
    m = np.zeros(d.shape, np.float64)
    for window, dil in DILATED_CONFIGS:
        m += ((d >= 0) & (d <= window) & (d % dil == 0)).astype(np.float64)
    return m


def _banded_kernel(*refs, seq, window, slopes, shared_kv):
    if shared_kv:
        q_ref, kv_ref, gq_ref, gk_ref, bias_ref, o_ref, kp_s, vt_s, qp_s, s_s, acc_s = refs
    else:
        q_ref, k_ref, v_ref, gq_ref, gk_ref, bias_ref, o_ref, kp_s, vt_s, qp_s, s_s, acc_s = refs
    qi = pl.program_id(1)
    nkt = seq // TK

    @pl.when(qi == 0)
    def _prep():
        for c in range(nkt):
            rows = slice(c * TK, (c + 1) * TK)
            pos = c * TK + lax.broadcasted_iota(jnp.int32, (TK, 1), 0)
            if shared_kv:
                blk = kv_ref[rows, :]
                kn = _group_rms(blk, gk_ref[...])
                kp_s[rows, 0:LANES], _ = _pair_keys(kn, pos)
                _, kp_s[rows, LANES:2 * LANES] = _pair_keys(pltpu.roll(kn, HEAD_DIM, 1), pos)
                vt_s[c] = _values_t(blk.T[HEAD_DIM:, :])
            else:
                kn = _group_rms(k_ref[rows, :], gk_ref[...])
                for i in range(N_HEADS // 2):
                    even, odd = _pair_keys(kn[:, i * LANES:(i + 1) * LANES], pos)
                    kp_s[rows, 2 * i * LANES:(2 * i + 1) * LANES] = even
                    kp_s[rows, (2 * i + 1) * LANES:(2 * i + 2) * LANES] = odd
                v_t = v_ref[rows, :].T
                for h in range(N_HEADS):
                    vt_s[c, h * V_ROWS:(h + 1) * V_ROWS, :] = _values_t(v_t[h * HEAD_DIM:(h + 1) * HEAD_DIM, :])

    t0 = qi * TQ
    _prep_queries(q_ref[...], gq_ref[...], t0, slopes, qp_s)
    kt_lo = jnp.maximum(t0 - window, 0) // TK

    def kp_fn(kt):
        off = pl.multiple_of(kt * TK, TK)
        if shared_kv:
            return lambda h: kp_s[pl.ds(off, TK), (h % 2) * LANES:(h % 2 + 1) * LANES]
        return lambda h: kp_s[pl.ds(off, TK), h * LANES:(h + 1) * LANES]

    def vt_fn(kt):
        if shared_kv:
            return lambda h: vt_s[kt]
        return lambda h: vt_s[kt, h * V_ROWS:(h + 1) * V_ROWS, :]

    top_kt = qi * N_TOP + (N_TOP - 1)
    _flash_pipelined(lambda i: top_kt - i, top_kt + 1 - kt_lo, kp_fn, lambda kt, top: bias_ref[top_kt - kt], vt_fn,
                     qp_s, s_s, acc_s, o_ref)


def _banded(p2, gq, gk, bias_tiles, batch, seq, *, window, slopes, shared_kv, col_q, col_k, col_v, name):
    nq = seq // TQ
    t = batch * seq
    nd = bias_tiles.shape[0]
    n_kv = 1 if shared_kv else N_HEADS
    q_spec = pl.BlockSpec((TQ, GROUP_W), lambda b, i: (b * nq + i, col_q // GROUP_W))
    gain = lambda g: pl.BlockSpec(g.shape, lambda b, i: (0, 0))
    bias_spec = pl.BlockSpec((nd, TK, TQ), lambda b, i: (0, 0, 0))
    if shared_kv:
        kv_specs = [pl.BlockSpec((seq, LANES), lambda b, i: (b, col_k // LANES))]
        operands = (p2, p2, gq, gk, bias_tiles)
    else:
        kv_specs = [pl.BlockSpec((seq, GROUP_W), lambda b, i: (b, col_k // GROUP_W)),
                    pl.BlockSpec((seq, GROUP_W), lambda b, i: (b, col_v // GROUP_W))]
        operands = (p2, p2, p2, gq, gk, bias_tiles)
    return pl.pallas_call(
        functools.partial(_banded_kernel, seq=seq, window=window, slopes=slopes, shared_kv=shared_kv),
        grid=(batch, nq),
        in_specs=[q_spec] + kv_specs + [gain(gq), gain(gk), bias_spec],
        out_specs=pl.BlockSpec((1, GROUP_W, TQ), lambda b, i: (b * nq + i, 0, 0)),
        out_shape=jax.ShapeDtypeStruct((t // TQ, GROUP_W, TQ), F32),
        scratch_shapes=[pltpu.VMEM((seq, (2 if shared_kv else N_HEADS) * LANES), BF16),
                        pltpu.VMEM((seq // TK, n_kv * V_ROWS, TK), BF16),
                        pltpu.VMEM((N_HEADS, TQ, LANES), BF16),
                        pltpu.VMEM((N_HEADS, TK, TQ), F32),
                        pltpu.VMEM((N_HEADS, V_ROWS, TQ), F32)],
        compiler_params=pltpu.CompilerParams(dimension_semantics=("parallel", "arbitrary"),
                                             vmem_limit_bytes=VMEM_LIMIT),
        name=name,
    )(*operands)


def _stick_kernel(q_ref, k_ref, v_ref, o_ref, kp_s, vt_s, qp_s, sp_s, e_s, acc_s, *, seq):
    TQ, N_TOP = STICK_TQ, STICK_TQ // TK
    qi = pl.program_id(1)
    nkt = seq // TK

    @pl.when(qi == 0)
    def _prep():
        for c in range(nkt):
            rows = slice(c * TK, (c + 1) * TK)
            kb = k_ref[rows, :]
            for i in range(N_HEADS // 2):
                even, odd = _pair_keys(kb[:, i * LANES:(i + 1) * LANES], None)
                kp_s[rows, 2 * i * LANES:(2 * i + 1) * LANES] = even
                kp_s[rows, (2 * i + 1) * LANES:(2 * i + 2) * LANES] = odd
            vt_s[c] = v_ref[rows, :].T.astype(BF16)

    _prep_queries(q_ref[...], None, qi * TQ, None, qp_s)
    half = TK // 2
    col = lax.broadcasted_iota(jnp.int32, (half + SUM_PAD, half), 1)
    srow = lax.broadcasted_iota(jnp.int32, (half + SUM_PAD, half), 0)
    sums = ((srow == half) | ((srow < half) & (col > srow))).astype(BF16)
    k_local = lax.broadcasted_iota(jnp.int32, (TK, TQ), 0)
    q_local = lax.broadcasted_iota(jnp.int32, (TK, TQ), 1)
    top_kt = qi * N_TOP + (N_TOP - 1)
    acc_s[...] = jnp.zeros(acc_s.shape, F32)

    def past_mask(tiles_above):
        return tiles_above * TK + k_local < q_local

    def logits(kt, h):
        off = pl.multiple_of(kt * TK, TK)
        return lax.dot_general(kp_s[pl.ds(off, TK), h * LANES:(h + 1) * LANES], qp_s[h // 2], _NT,
                               preferred_element_type=F32)

    def stage(h, z, past):
        sp = jnp.maximum(z, 0.0) + jnp.log(1.0 + jnp.exp2(jnp.abs(z) * (-LOG2E)))
        spm = sp if past is None else jnp.where(past, sp, 0.0)
        sp_s[h] = spm.astype(BF16)
        logsig = z - sp
        e_s[h] = logsig if past is None else jnp.where(past, logsig, NEG_INF)

    def tails(h):
        return [jnp.dot(sums, sp_s[h, b * half:(b + 1) * half, :], preferred_element_type=F32) for b in range(2)]

    def consume(kt, h, w, later):
        later_lo = later + w[1][half:half + 1, :]
        attn = jnp.concatenate([jnp.exp(e_s[h, 0:half, :] - w[0][0:half, :] - later_lo),
                                jnp.exp(e_s[h, half:TK, :] - w[1][0:half, :] - later)], axis=0)
        acc_s[h] += jnp.dot(vt_s[kt, h * HEAD_DIM:(h + 1) * HEAD_DIM, :], attn.astype(BF16),
                            preferred_element_type=F32)
        return later_lo + w[0][half:half + 1, :]

    ahead = MXU_AHEAD
    for h in range(N_HEADS):
        stage(h, logits(top_kt, h), past_mask(N_TOP - 1))

    def step(kt, carry, past):
        laters, w_first, z_first = carry
        w = {0: list(w_first)}
        z = {0: z_first}
        for h in range(1, ahead):
            w[h] = tails(h)
            z[h] = logits(kt - 1, h)
        out = []
        for h in range(N_HEADS):
            out.append(consume(kt, h, w.pop(h), laters[h]))
            stage(h, z.pop(h), past)
            if h + ahead < N_HEADS:
                w[h + ahead] = tails(h + ahead)
                z[h + ahead] = logits(kt - 1, h + ahead)
            elif h + ahead == N_HEADS:
                nxt = (tuple(tails(0)), logits(jnp.maximum(kt - 2, 0), 0))
        return tuple(out), nxt[0], nxt[1]

    carry = (tuple(jnp.zeros((1, TQ), F32) for _ in range(N_HEADS)),
             tuple(tails(0)), logits(jnp.maximum(top_kt - 1, 0), 0))
    for j in range(N_TOP - 1):
        carry = step(top_kt - j, carry, past_mask(N_TOP - 2 - j))
    below = qi * N_TOP
    laters, w_first, _ = lax.fori_loop(0, below, lambda i, c: step(below - i, c, None), carry)
    w = {0: list(w_first)}
    for h in range(1, ahead):
        w[h] = tails(h)
    for h in range(N_HEADS):
        consume(0, h, w.pop(h), laters[h])
        if h + ahead < N_HEADS:
            w[h + ahead] = tails(h + ahead)
    o_ref[0] = acc_s[...].reshape(GROUP_W, TQ)


def _stick(p2, batch, seq):
    TQ = STICK_TQ
    nq = seq // TQ
    t = batch * seq
    return pl.pallas_call(
        functools.partial(_stick_kernel, seq=seq),
        grid=(batch, nq),
        in_specs=[pl.BlockSpec((TQ, GROUP_W), lambda b, i: (b * nq + i, COL_QC // GROUP_W)),
                  pl.BlockSpec((seq, GROUP_W), lambda b, i: (b, COL_KC // GROUP_W)),
                  pl.BlockSpec((seq, GROUP_W), lambda b, i: (b, COL_VC // GROUP_W))],
        out_specs=pl.BlockSpec((1, GROUP_W, TQ), lambda b, i: (b * nq + i, 0, 0)),
        out_shape=jax.ShapeDtypeStruct((t // TQ, GROUP_W, TQ), F32),
        scratch_shapes=[pltpu.VMEM((seq, N_HEADS * LANES), BF16),
                        pltpu.VMEM((seq // TK, GROUP_W, TK), BF16),
                        pltpu.VMEM((N_HEADS // 2, TQ, LANES), BF16),
                        pltpu.VMEM((N_HEADS, TK, TQ), BF16),
                        pltpu.VMEM((N_HEADS, TK, TQ), F32),
                        pltpu.VMEM((N_HEADS, HEAD_DIM, TQ), F32)],
        compiler_params=pltpu.CompilerParams(dimension_semantics=("parallel", "arbitrary"),
                                             vmem_limit_bytes=VMEM_LIMIT),
        name="stick_breaking",
    )(p2, p2, p2)


def _mixout_kernel(x_ref, ocmp_ref, oslc_ref, owin_ref, gate_ref, cvb_ref, cvc_ref, cvu_ref, pc_ref, pu_ref,
                   oc_ref, od_ref, bg_ref, cw_ref, gout_ref, wout_ref, o_ref, *, tiles_per_seq):
    i = pl.program_id(0)
    tm, d = x_ref.shape
    chains = [slice(c * MIX_CHAIN, (c + 1) * MIX_CHAIN) for c in range(tm // MIX_CHAIN)]

    def heads(ref, rows):
        tq = ref.shape[2]
        return ref[rows.start // tq, :, rows.start % tq:rows.start % tq + MIX_CHAIN].T

    cu = cvc_ref[...] * cvu_ref[...]
    prev = jnp.where(i % tiles_per_seq == 0, 0.0, pc_ref[...] * pu_ref[...])
    full = jnp.concatenate([prev, cu], axis=0)
    back1 = pltpu.roll(full, 1, 0)[8:, :]
    back2 = pltpu.roll(full, 2, 0)[8:, :]
    cw = cw_ref[...]
    ob = cvb_ref[...] * (cw[0:1, :] * back2 + cw[1:2, :] * back1 + cw[2:3, :] * cu)

    gates = jax.nn.sigmoid(gate_ref[...] + bg_ref[...])
    src = lax.broadcasted_iota(jnp.int32, (LANES, GROUP_W), 0)
    head3 = (lax.broadcasted_iota(jnp.int32, (LANES, GROUP_W), 1) >> 6) * 3
    branch = [(src == head3 + r).astype(BF16) for r in range(3)]
    assert d // HEAD_DIM <= LANES
    gather = ((lax.broadcasted_iota(jnp.int32, (d, LANES), 0) >> 6)
              == lax.broadcasted_iota(jnp.int32, (d, LANES), 1)).astype(BF16)
    spread = (lax.broadcasted_iota(jnp.int32, (LANES, d), 0)
              == (lax.broadcasted_iota(jnp.int32, (LANES, d), 1) >> 6)).astype(BF16)

    groups, ssq = [], []
    for rows in chains:
        oa = (_exact_dot(gates[rows], branch[0]) * heads(ocmp_ref, rows)
              + _exact_dot(gates[rows], branch[1]) * heads(oslc_ref, rows)
              + _exact_dot(gates[rows], branch[2]) * heads(owin_ref, rows))
        g = jnp.concatenate([oa, ob[rows], heads(oc_ref, rows), heads(od_ref, rows)], axis=1)
        groups.append(g)
        ssq.append(_exact_dot(g * g, gather))
    scale = [_exact_dot(lax.rsqrt(s * (1.0 / HEAD_DIM) + RMS_EPS), spread) for s in ssq]
    for rows, g, sc in zip(chains, groups, scale):
        mixed = (g * sc * gout_ref[...]).astype(BF16)
        o_ref[rows, :] = x_ref[rows, :] + jnp.dot(mixed, wout_ref[...], preferred_element_type=F32)


def _mixout(x2, p2, ocmp, oslc, owin, oc, od, bg, cw, gout, wout, seq):
    t, d = x2.shape
    tm = 2 * MIX_CHAIN
    rows = lambda w, col: pl.BlockSpec((tm, w), lambda i: (i, col // w))
    heads_t = lambda tq: pl.BlockSpec((tm // tq, GROUP_W, tq), lambda i: (i, 0, 0))
    prev8 = lambda col: pl.BlockSpec((8, GROUP_W), lambda i: (jnp.maximum(i * (tm // 8) - 1, 0), col // GROUP_W))
    const = lambda shape: pl.BlockSpec(shape, lambda i: (0, 0))
    assert tm % TQ == 0 and tm % STICK_TQ == 0
    return pl.pallas_call(
        functools.partial(_mixout_kernel, tiles_per_seq=seq // tm),
        grid=(t // tm,),
        in_specs=[rows(d, 0), heads_t(TQ), heads_t(TQ), heads_t(TQ),
                  rows(LANES, COL_GATE), rows(GROUP_W, COL_CVB), rows(GROUP_W, COL_CVC), rows(GROUP_W, COL_CVU),
                  prev8(COL_CVC), prev8(COL_CVU),
                  heads_t(STICK_TQ), heads_t(TQ),
                  const((1, LANES)), const((8, GROUP_W)), const((1, d)), const((d, d))],
        out_specs=rows(d, 0),
        out_shape=jax.ShapeDtypeStruct((t, d), F32),
        compiler_params=pltpu.CompilerParams(dimension_semantics=("parallel",), vmem_limit_bytes=VMEM_LIMIT),
        name="mixout",
    )(x2, ocmp, oslc, owin, p2, p2, p2, p2, p2, p2, oc, od, bg, cw, gout, wout)


def _ffn_kernel(x_ref, g_ref, wg_ref, wu_ref, wd_ref, o_ref, act_s, *, ff_chunk, out_chunk):
    h = _rms(x_ref[...], g_ref[...]).astype(BF16)
    dff = wg_ref.shape[1]
    for c in range(dff // ff_chunk):
        cols = slice(c * ff_chunk, (c + 1) * ff_chunk)
        a = jnp.dot(h, wg_ref[:, cols], preferred_element_type=F32)
        u = jnp.dot(h, wu_ref[:, cols], preferred_element_type=F32)
        act_s[:, cols] = (a * jax.nn.sigmoid(a) * u).astype(BF16)
    d = o_ref.shape[1]
    for c in range(d // out_chunk):
        cols = slice(c * out_chunk, (c + 1) * out_chunk)
        o_ref[:, cols] = x_ref[:, cols] + jnp.dot(act_s[...], wd_ref[:, cols], preferred_element_type=F32)


def _ffn(x2, g, wg, wu, wd):
    t, d = x2.shape
    dff = wg.shape[1]
    tm = 512
    resident = lambda shape: pl.BlockSpec(shape, lambda i: (0, 0), pipeline_mode=pl.Buffered(1))
    return pl.pallas_call(
        functools.partial(_ffn_kernel, ff_chunk=256, out_chunk=256),
        grid=(t // tm,),
        in_specs=[pl.BlockSpec((tm, d), lambda i: (i, 0)),
                  pl.BlockSpec((1, d), lambda i: (0, 0)),
                  resident((d, dff)), resident((d, dff)), resident((dff, d))],
        out_specs=pl.BlockSpec((tm, d), lambda i: (i, 0)),
        out_shape=jax.ShapeDtypeStruct((t, d), F32),
        scratch_shapes=[pltpu.VMEM((tm, dff), BF16)],
        compiler_params=pltpu.CompilerParams(dimension_semantics=("parallel",), vmem_limit_bytes=VMEM_LIMIT),
        name="ffn",
    )(x2, g, wg, wu, wd)


def _permute_w_in_kernel(w_ref, o_ref):
    rows = w_ref.shape[0]
    n_rest = P_COLS - COL_CVB
    o_ref[:, 0:COL_GATE] = w_ref[:, 0:COL_GATE].astype(BF16)
    lane = lax.broadcasted_iota(jnp.int32, (rows, LANES), 1)
    o_ref[:, COL_GATE:COL_GATE + LANES] = jnp.where(lane < N_GATES, w_ref[:, COL_GATE:COL_GATE + LANES], 0.0).astype(BF16)
    o_ref[:, COL_CVB:P_COLS] = w_ref[:, COL_GATE + N_GATES:COL_GATE + N_GATES + n_rest].astype(BF16)


def _permute_w_in(w_in):
    d, cols = w_in.shape
    tr = 256
    return pl.pallas_call(
        _permute_w_in_kernel,
        grid=(d // tr,),
        in_specs=[pl.BlockSpec((tr, cols), lambda i: (i, 0))],
        out_specs=pl.BlockSpec((tr, P_COLS), lambda i: (i, 0)),
        out_shape=jax.ShapeDtypeStruct((d, P_COLS), BF16),
        compiler_params=pltpu.CompilerParams(dimension_semantics=("parallel",), vmem_limit_bytes=VMEM_LIMIT),
        name="permute_w_in",
    )(w_in)


def _layer(x2, batch, seq, g_mix, w_in, b_gate, g_q_nsa, g_k_cmp, g_k_slc, g_k_win, pe_k_cmp, pe_v_cmp,
           w1_k_cmp, w2_k_cmp, w1_v_cmp, w2_v_cmp, conv_w, g_q_dil, g_k_dil, g_out, w_out,
           g_ffn, w_gate, w_up, w_down):
    row = lambda v: v.reshape(1, -1)
    per_group = lambda g, n: jnp.tile(g, n).reshape(1, -1)
    p2 = _inproj(x2, row(g_mix), _permute_w_in(w_in))

    pe, w1, w2 = _compress_weights(pe_k_cmp, pe_v_cmp, w1_k_cmp, w2_k_cmp, w1_v_cmp, w2_v_cmp)
    kc, vc = _compress(p2, pe, w1, w2, per_group(g_k_cmp, 2), batch, seq)
    ocmp, oslc = _nsa(p2, kc, vc, per_group(g_q_nsa, N_HEADS), per_group(g_k_slc, 2), batch, seq)
    owin = _banded(p2, per_group(g_q_nsa, N_HEADS), per_group(g_k_win, 2), jnp.asarray(_band_bias_tiles(NSA_WINDOW - 1, _window_mult)),
                   batch, seq, window=NSA_WINDOW - 1, slopes=SLOPES_A, shared_kv=True,
                   col_q=COL_QA, col_k=COL_KWV, col_v=COL_KWV, name="nsa_window")
    od = _banded(p2, per_group(g_q_dil, N_HEADS), per_group(g_k_dil, N_HEADS),
                 jnp.asarray(_band_bias_tiles(DILATED_CONFIGS[-1][0], _dilated_mult)),
                 batch, seq, window=DILATED_CONFIGS[-1][0], slopes=SLOPES_D, shared_kv=False,
                 col_q=COL_QD, col_k=COL_KD, col_v=COL_VD, name="dilated")
    oc = _stick(p2, batch, seq)

    bg = jnp.zeros((1, LANES), F32).at[0, :N_GATES].set(b_gate)
    cwp = jnp.zeros((8, GROUP_W), F32).at[:CONV_K].set(conv_w)
    x1 = _mixout(x2, p2, ocmp, oslc, owin, oc, od, bg, cwp, row(g_out), w_out.astype(BF16), seq)
    return _ffn(x1, row(g_ffn), w_gate.astype(BF16), w_up.astype(BF16), w_down.astype(BF16))


def kernel(x, g_mix, w_in, b_gate, g_q_nsa, g_k_cmp, g_k_slc, g_k_win, pe_k_cmp, pe_v_cmp, w1_k_cmp, w2_k_cmp,
           w1_v_cmp, w2_v_cmp, conv_w, g_q_dil, g_k_dil, g_out, w_out, g_ffn, w_gate, w_up, w_down):
    batch, seq, d = x.shape
    assert seq % (CMP_STRIDE * LANES) == 0 and d % LANES == 0
    x2 = x.reshape(batch * seq, d)
    params = (g_mix, w_in, b_gate, g_q_nsa, g_k_cmp, g_k_slc, g_k_win, pe_k_cmp, pe_v_cmp, w1_k_cmp, w2_k_cmp,
              w1_v_cmp, w2_v_cmp, conv_w, g_q_dil, g_k_dil, g_out, w_out, g_ffn, w_gate, w_up, w_down)
    for layer in range(g_mix.shape[0]):
        x2 = _layer(x2, batch, seq, *[p[layer] for p in params])
    return x2.reshape(batch, seq, d)
```

```python
import functools

import numpy as np
import jax
import jax.numpy as jnp
from jax import lax
from jax.experimental import pallas as pl
from jax.experimental.pallas import tpu as pltpu

F32 = jnp.float32
BF16 = jnp.bfloat16

HEAD_DIM = 64
N_HEADS = 4
GROUP_W = N_HEADS * HEAD_DIM
CONV_K = 3
CMP_LEN = 32
CMP_STRIDE = 16
SEL_BLOCK = 64
N_SELECT = 16
NSA_WINDOW = 512
DILATED_CONFIGS = ((128, 1), (512, 4), (2048, 16))
NEG_INF = -1e30
FORCE_SCORE = 1e6
RMS_EPS = 1e-6

TQ = 512
TK = 256
N_TOP = TQ // TK
STICK_TQ = 256
LANES = 128
MXU_DEPTH = 256
MIX_CHAIN = 256
MXU_AHEAD = 2
FLASH_AHEAD = 2
SUM_PAD = 16
V_ROWS = HEAD_DIM + SUM_PAD
LOG2E = 1.4426950408889634
VMEM_LIMIT = 52 * 1024 * 1024

COL_QA, COL_KVC, COL_KSV, COL_KWV, COL_GATE = 0, 256, 384, 512, 640
COL_CVB, COL_CVC, COL_CVU = 768, 1024, 1280
COL_QC, COL_KC, COL_VC = 1536, 1792, 2048
COL_QD, COL_KD, COL_VD = 2304, 2560, 2816
P_COLS = 3072
N_GATES = 12

_NT = (((1,), (1,)), ((), ()))


def _alibi_slopes():
    s = [2.0 ** (-8.0 * i / 8) for i in range(1, 9)]
    return tuple(s[0::2]), tuple(s[1::2])


SLOPES_A, SLOPES_D = _alibi_slopes()


def _rms(x, g):
    return x * lax.rsqrt(jnp.mean(x * x, axis=-1, keepdims=True) + RMS_EPS) * g


def _exact_dot(a, sel):
    hi = a.astype(BF16)
    lo = (a - hi.astype(F32)).astype(BF16)
    if 2 * a.shape[1] <= MXU_DEPTH:
        return jnp.dot(jnp.concatenate([hi, lo], axis=1), jnp.concatenate([sel, sel], axis=0),
                       preferred_element_type=F32)
    return jnp.dot(hi, sel, preferred_element_type=F32) + jnp.dot(lo, sel, preferred_element_type=F32)


def _group_rms(x, g):
    w = x.shape[1]
    same = ((lax.broadcasted_iota(jnp.int32, (w, w), 0) >> 6)
            == (lax.broadcasted_iota(jnp.int32, (w, w), 1) >> 6)).astype(BF16)
    ssq = _exact_dot(x * x, same)
    return x * lax.rsqrt(ssq * (1.0 / HEAD_DIM) + RMS_EPS) * g


EVEN_AUG, ODD_AUG = HEAD_DIM, 0


def _key_aug(pos, first):
    lane = lax.broadcasted_iota(jnp.int32, (pos.shape[0], LANES), 1) - first
    hi = (pos >> 6).astype(F32)
    lo = (pos & 63).astype(F32)
    return jnp.where(lane == 0, hi, jnp.where(lane == 1, lo, jnp.where((lane == 2) | (lane == 3), 1.0, 0.0)))


def _query_aug(t, slope, first):
    lane = lax.broadcasted_iota(jnp.int32, (t.shape[0], LANES), 1) - first
    hi = (t >> 6).astype(F32) * (-64.0 * slope)
    lo = (t & 63).astype(F32) * (-slope)
    return jnp.where(lane == 0, 64.0 * slope,
                     jnp.where(lane == 1, slope, jnp.where(lane == 2, hi, jnp.where(lane == 3, lo, 0.0))))


def _pair_keys(pair, pos):
    lane = lax.broadcasted_iota(jnp.int32, pair.shape, 1)
    aug_e = 0.0 if pos is None else _key_aug(pos, EVEN_AUG)
    aug_o = 0.0 if pos is None else _key_aug(pos, ODD_AUG)
    return (jnp.where(lane < HEAD_DIM, pair, aug_e).astype(BF16),
            jnp.where(lane >= HEAD_DIM, pair, aug_o).astype(BF16))


def _prep_queries(q, g, t0, slopes, qp_s):
    qn = (q if g is None else _group_rms(q, g)) * (HEAD_DIM ** -0.5)
    tpos = t0 + lax.broadcasted_iota(jnp.int32, (q.shape[0], 1), 0)
    lane = lax.broadcasted_iota(jnp.int32, (q.shape[0], LANES), 1)
    for i in range(N_HEADS // 2):
        pair = qn[:, i * LANES:(i + 1) * LANES]
        if slopes is None:
            qp_s[i] = pair.astype(BF16)
            continue
        even = jnp.where((lane >= EVEN_AUG) & (lane < EVEN_AUG + 4), _query_aug(tpos, slopes[2 * i], EVEN_AUG), pair)
        odd = jnp.where(lane < ODD_AUG + 4, _query_aug(tpos, slopes[2 * i + 1], ODD_AUG), pair)
        qp_s[2 * i] = even.astype(BF16)
        qp_s[2 * i + 1] = odd.astype(BF16)


def _inproj_kernel(x_ref, g_ref, w_ref, o_ref, *, nchunk):
    h = _rms(x_ref[...], g_ref[...]).astype(BF16)
    cw = P_COLS // nchunk
    for c in range(nchunk):
        o_ref[:, c * cw:(c + 1) * cw] = jnp.dot(h, w_ref[:, c * cw:(c + 1) * cw], preferred_element_type=F32)


def _inproj(x2, g, w):
    t, d = x2.shape
    tm = 1024
    return pl.pallas_call(
        functools.partial(_inproj_kernel, nchunk=6),
        grid=(t // tm,),
        in_specs=[pl.BlockSpec((tm, d), lambda i: (i, 0)),
                  pl.BlockSpec((1, d), lambda i: (0, 0)),
                  pl.BlockSpec((d, P_COLS), lambda i: (0, 0), pipeline_mode=pl.Buffered(1))],
        out_specs=pl.BlockSpec((tm, P_COLS), lambda i: (i, 0)),
        out_shape=jax.ShapeDtypeStruct((t, P_COLS), F32),
        compiler_params=pltpu.CompilerParams(dimension_semantics=("parallel",), vmem_limit_bytes=VMEM_LIMIT),
        name="inproj",
    )(x2, g, w)


def _gelu_tanh(x):
    return x * (0.5 * (1.0 + jnp.tanh(np.sqrt(2.0 / np.pi).astype(np.float32) * (x + 0.044715 * (x * x * x)))))


def _compress_kernel(kv_ref, pe_ref, w1_ref, w2_ref, gk_ref, kc_ref, vct_ref):
    nc = kv_ref.shape[0] // CMP_STRIDE
    first = None
    second = None
    for j in range(CMP_STRIDE):
        tok = kv_ref[pl.ds(j, nc, stride=CMP_STRIDE), :]
        a = jnp.dot((tok + pe_ref[j:j + 1, :]).astype(BF16), w1_ref[j], preferred_element_type=F32)
        b = jnp.dot((tok + pe_ref[CMP_STRIDE + j:CMP_STRIDE + j + 1, :]).astype(BF16), w1_ref[CMP_STRIDE + j],
                    preferred_element_type=F32)
        first = a if first is None else first + a
        second = b if second is None else second + b
    hid = first + pltpu.roll(second, nc - 1, 0)
    out = jnp.dot(_gelu_tanh(hid).astype(BF16), w2_ref[...], preferred_element_type=F32)
    kn = _group_rms(out, gk_ref[...])
    end = lax.broadcasted_iota(jnp.int32, (nc, 1), 0) * CMP_STRIDE + (CMP_LEN - 1)
    even, _ = _pair_keys(kn, end)
    _, odd = _pair_keys(pltpu.roll(kn, HEAD_DIM, 1), end)
    kc_ref[0] = jnp.concatenate([even, odd], axis=1)
    vct_ref[0] = out.T[HEAD_DIM:, :].astype(BF16)


def _compress(p2, pe, w1, w2, gk, batch, seq):
    nc = seq // CMP_STRIDE
    hid2 = w1.shape[2]
    return pl.pallas_call(
        _compress_kernel,
        grid=(batch,),
        in_specs=[pl.BlockSpec((seq, LANES), lambda i: (i, COL_KVC // LANES)),
                  pl.BlockSpec((CMP_LEN, LANES), lambda i: (0, 0)),
                  pl.BlockSpec((CMP_LEN, LANES, hid2), lambda i: (0, 0, 0)),
                  pl.BlockSpec((hid2, LANES), lambda i: (0, 0)),
                  pl.BlockSpec((1, LANES), lambda i: (0, 0))],
        out_specs=[pl.BlockSpec((1, nc, 2 * LANES), lambda i: (i, 0, 0)),
                   pl.BlockSpec((1, HEAD_DIM, nc), lambda i: (i, 0, 0))],
        out_shape=[jax.ShapeDtypeStruct((batch, nc, 2 * LANES), BF16),
                   jax.ShapeDtypeStruct((batch, HEAD_DIM, nc), BF16)],
        compiler_params=pltpu.CompilerParams(dimension_semantics=("parallel",), vmem_limit_bytes=VMEM_LIMIT),
        name="nsa_compress",
    )(p2, pe, w1, w2, gk)


def _compress_weights(pe_k, pe_v, w1_k, w2_k, w1_v, w2_v):
    hid = w1_k.shape[1]
    w1k = w1_k.reshape(CMP_LEN, HEAD_DIM, hid)
    w1v = w1_v.reshape(CMP_LEN, HEAD_DIM, hid)
    z1 = jnp.zeros_like(w1k)
    w1 = jnp.concatenate([jnp.concatenate([w1k, z1], axis=2), jnp.concatenate([z1, w1v], axis=2)], axis=1)
    z2 = jnp.zeros_like(w2_k)
    w2 = jnp.concatenate([jnp.concatenate([w2_k, z2], axis=1), jnp.concatenate([z2, w2_v], axis=1)], axis=0)
    pe = jnp.concatenate([pe_k, pe_v], axis=1)
    return pe, w1.astype(BF16), w2.astype(BF16)


def _flash_scores(kp_of, bias, qp_s, s_s, m_cur):
    m_next = []
    for h in range(N_HEADS):
        s = lax.dot_general(kp_of(h), qp_s[h], _NT, preferred_element_type=F32) + bias
        s_s[h] = s
        m_next.append(jnp.maximum(m_cur[h], jnp.max(s, axis=0, keepdims=True)))
    return tuple(m_next)


def _flash_accum(vt_of, s_s, acc_s, m_prev, m_cur):
    for h in range(N_HEADS):
        alpha = jnp.exp(m_prev[h] - m_cur[h])
        p = jnp.exp(s_s[h] - m_cur[h])
        acc_s[h] = alpha * acc_s[h] + jnp.dot(vt_of(h), p.astype(BF16), preferred_element_type=F32)


def _flash_accum_and_scores(kp_of, bias, vt_of, qp_s, s_s, acc_s, m_prev, m_cur):
    def scores(h):
        return lax.dot_general(kp_of(h), qp_s[h], _NT, preferred_element_type=F32) + bias

    m_next = []
    s_new = {h: scores(h) for h in range(FLASH_AHEAD)}
    for h in range(N_HEADS):
        alpha = jnp.exp(m_prev[h] - m_cur[h])
        p = jnp.exp(s_s[h] - m_cur[h])
        acc_s[h] = alpha * acc_s[h] + jnp.dot(vt_of(h), p.astype(BF16), preferred_element_type=F32)
        s_h = s_new.pop(h)
        s_s[h] = s_h
        m_next.append(jnp.maximum(m_cur[h], jnp.max(s_h, axis=0, keepdims=True)))
        if h + FLASH_AHEAD < N_HEADS:
            s_new[h + FLASH_AHEAD] = scores(h + FLASH_AHEAD)
    return tuple(m_next)


def _flash_pipelined(tile_of, n_tiles, kp_fn, bias_fn, vt_fn, qp_s, s_s, acc_s, o_ref):
    acc_s[...] = jnp.zeros(acc_s.shape, F32)
    m0 = tuple(jnp.full((1, TQ), NEG_INF, F32) for _ in range(N_HEADS))
    first_kt = tile_of(0)
    m1 = _flash_scores(kp_fn(first_kt), bias_fn(first_kt, True), qp_s, s_s, m0)

    def step(i, carry, top):
        m_prev, m_cur = carry
        nxt = tile_of(i + 1)
        m_next = _flash_accum_and_scores(kp_fn(nxt), bias_fn(nxt, top), vt_fn(tile_of(i)), qp_s, s_s, acc_s,
                                         m_prev, m_cur)
        return m_cur, m_next

    carry = (m0, m1)
    for i in range(N_TOP - 1):
        carry = step(i, carry, True)
    m_prev, m_cur = lax.fori_loop(N_TOP - 1, n_tiles - 1, lambda i, c: step(i, c, False), carry)
    _flash_accum(vt_fn(tile_of(n_tiles - 1)), s_s, acc_s, m_prev, m_cur)
    o_ref[0] = jnp.concatenate([acc_s[h, 0:HEAD_DIM, :] * (1.0 / acc_s[h, HEAD_DIM:HEAD_DIM + 1, :])
                                for h in range(N_HEADS)], axis=0)


def _values_t(v_t):
    row = lax.broadcasted_iota(jnp.int32, (V_ROWS - HEAD_DIM, v_t.shape[1]), 0)
    return jnp.concatenate([v_t, jnp.where(row == 0, 1.0, 0.0)], axis=0).astype(BF16)


def _nsa_kernel(q_ref, kc_ref, vct_ref, ksv_ref, gq_ref, gks_ref, ocmp_ref, oslc_ref,
                ksp_s, vst_s, qp_s, imp_s, sel_s, s_s, acc_s, hit_s, *, seq):
    qi = pl.program_id(1)
    nc = seq // CMP_STRIDE
    nsel = seq // SEL_BLOCK
    nkt = seq // TK

    @pl.when(qi == 0)
    def _prep():
        for c in range(nkt):
            rows = slice(c * TK, (c + 1) * TK)
            blk = ksv_ref[rows, :]
            pos = c * TK + lax.broadcasted_iota(jnp.int32, (TK, 1), 0)
            kn = _group_rms(blk, gks_ref[...])
            ksp_s[rows, 0:LANES], _ = _pair_keys(kn, pos)
            _, ksp_s[rows, LANES:2 * LANES] = _pair_keys(pltpu.roll(kn, HEAD_DIM, 1), pos)
            vst_s[c] = _values_t(blk.T[HEAD_DIM:, :])
        imp_s[:, 0:8, :] = jnp.zeros((TQ // LANES, 8, LANES), F32)

    t0 = qi * TQ
    _prep_queries(q_ref[...], gq_ref[...], t0, SLOPES_A, qp_s)

    kc = [kc_ref[0, :, 0:LANES], kc_ref[0, :, LANES:2 * LANES]]
    vct = vct_ref[0]
    tq_row = t0 + lax.broadcasted_iota(jnp.int32, (nc, TQ), 1)
    n_idx = lax.broadcasted_iota(jnp.int32, (nc, TQ), 0)
    vis = (tq_row >= n_idx * CMP_STRIDE + (CMP_LEN - 1)) & (n_idx < nc - 1)
    vis_bias = jnp.where(vis, 0.0, NEG_INF)
    sees_any = (t0 + lax.broadcasted_iota(jnp.int32, (1, TQ), 1) >= CMP_LEN - 1).astype(F32)
    imp = jnp.zeros((nc, TQ), F32)
    ocmp_t = []
    scores = [lax.dot_general(kc[h % 2], qp_s[h], _NT, preferred_element_type=F32) + vis_bias
              for h in range(N_HEADS)]
    for h in range(N_HEADS):
        sc = scores[h]
        e = jnp.exp(sc - jnp.max(sc, axis=0, keepdims=True))
        p = e * (sees_any / jnp.sum(e, axis=0, keepdims=True))
        ocmp_t.append(jnp.dot(vct, p.astype(BF16), preferred_element_type=F32))
        imp = imp + p
    ocmp_ref[0] = jnp.concatenate(ocmp_t, axis=0)

    halves = []
    for half in range(TQ // LANES):
        imp_s[half, 8:8 + nc, :] = imp[:, half * LANES:(half + 1) * LANES]
        r = [imp_s[half, pl.ds(8 + k, nsel, stride=4), :] for k in range(4)]
        rm1 = imp_s[half, pl.ds(7, nsel, stride=4), :]
        halves.append(rm1 + 2.0 * (r[0] + r[1] + r[2]) + r[3])
    imp_blk = jnp.concatenate(halves, axis=1)
    blk = lax.broadcasted_iota(jnp.int32, (nsel, TQ), 0)
    tl = t0 + lax.broadcasted_iota(jnp.int32, (nsel, TQ), 1)
    cur = tl >> 6
    forced = (blk == 0) | (blk == cur) | (blk == cur - 1)
    valid = blk * SEL_BLOCK <= tl
    score = jnp.where(forced, FORCE_SCORE, jnp.where(valid, imp_blk, -FORCE_SCORE))
    sub = 8
    per_tile = TK // SEL_BLOCK
    tile_hit = []
    for g in range(nsel // sub):
        mine = score[g * sub:(g + 1) * sub, :]
        blk_g = g * sub + lax.broadcasted_iota(jnp.int32, (sub, TQ), 0)
        rank = jnp.zeros((sub, TQ), F32)
        for j in range(nsel):
            row = score[j:j + 1, :]
            if j < g * sub:
                beats = row >= mine
            elif j >= (g + 1) * sub:
                beats = row > mine
            else:
                beats = (row > mine) | ((row == mine) & (blk_g > j))
            rank = rank + jnp.where(beats, 1.0, 0.0)
        sel_g = jnp.where(rank < min(N_SELECT, nsel), 0.0, NEG_INF)
        sel_s[g * sub:(g + 1) * sub, :] = sel_g
        for part in range(sub // per_tile):
            tile_hit.append(jnp.max(sel_g[part * per_tile:(part + 1) * per_tile, :]))

    n_hit = jnp.int32(0)
    for kt in reversed(range(nkt)):
        hit_s[n_hit] = kt
        n_hit = n_hit + ((tile_hit[kt] > -1.0) & (kt * TK < t0)).astype(jnp.int32)

    def block_bias(kt):
        return jnp.concatenate([jnp.broadcast_to(sel_s[pl.ds(kt * per_tile + j, 1), :], (SEL_BLOCK, TQ))
                                for j in range(per_tile)], axis=0)

    k_local = lax.broadcasted_iota(jnp.int32, (TK, TQ), 0)
    q_local = lax.broadcasted_iota(jnp.int32, (TK, TQ), 1)
    top_kt = qi * N_TOP + (N_TOP - 1)

    def kp_fn(kt):
        off = pl.multiple_of(kt * TK, TK)
        return lambda h: ksp_s[pl.ds(off, TK), (h % 2) * LANES:(h % 2 + 1) * LANES]

    def bias_fn(kt, top):
        if top:
            return jnp.where(kt * TK - t0 + k_local <= q_local, block_bias(kt), NEG_INF)
        return block_bias(kt)

    tile_of = lambda i: jnp.where(i < N_TOP, top_kt - i, hit_s[jnp.maximum(i - N_TOP, 0)])
    _flash_pipelined(tile_of, n_hit + N_TOP, kp_fn, bias_fn, lambda kt: (lambda h: vst_s[kt]), qp_s, s_s, acc_s,
                     oslc_ref)


def _nsa(p2, kc, vc, gq, gks, batch, seq):
    nq = seq // TQ
    nc = seq // CMP_STRIDE
    t = batch * seq
    return pl.pallas_call(
        functools.partial(_nsa_kernel, seq=seq),
        grid=(batch, nq),
        in_specs=[pl.BlockSpec((TQ, GROUP_W), lambda b, i: (b * nq + i, COL_QA // GROUP_W)),
                  pl.BlockSpec((1, nc, 2 * LANES), lambda b, i: (b, 0, 0)),
                  pl.BlockSpec((1, HEAD_DIM, nc), lambda b, i: (b, 0, 0)),
                  pl.BlockSpec((seq, LANES), lambda b, i: (b, COL_KSV // LANES)),
                  pl.BlockSpec((1, GROUP_W), lambda b, i: (0, 0)),
                  pl.BlockSpec((1, LANES), lambda b, i: (0, 0))],
        out_specs=[pl.BlockSpec((1, GROUP_W, TQ), lambda b, i: (b * nq + i, 0, 0)),
                   pl.BlockSpec((1, GROUP_W, TQ), lambda b, i: (b * nq + i, 0, 0))],
        out_shape=[jax.ShapeDtypeStruct((t // TQ, GROUP_W, TQ), F32),
                   jax.ShapeDtypeStruct((t // TQ, GROUP_W, TQ), F32)],
        scratch_shapes=[pltpu.VMEM((seq, 2 * LANES), BF16),
                        pltpu.VMEM((seq // TK, V_ROWS, TK), BF16),
                        pltpu.VMEM((N_HEADS, TQ, LANES), BF16),
                        pltpu.VMEM((TQ // LANES, 8 + nc, LANES), F32),
                        pltpu.VMEM((seq // SEL_BLOCK, TQ), F32),
                        pltpu.VMEM((N_HEADS, TK, TQ), F32),
                        pltpu.VMEM((N_HEADS, V_ROWS, TQ), F32),
                        pltpu.SMEM((seq // TK + 1,), jnp.int32)],
        compiler_params=pltpu.CompilerParams(dimension_semantics=("parallel", "arbitrary"),
                                             vmem_limit_bytes=VMEM_LIMIT),
        name="nsa_cmp_slc",
    )(p2, kc, vc, p2, gq, gks)


def _band_bias_tiles(window, mult_fn):
    nd = (window + TQ - 1) // TK + 1
    kk = np.arange(TK)[:, None]
    qq = np.arange(TQ)[None, :]
    tiles = np.empty((nd, TK, TQ), np.float32)
    for di in range(nd):
        d = (di - (N_TOP - 1)) * TK + qq - kk
        mult = mult_fn(d)
        tiles[di] = np.where(mult > 0, np.log(np.maximum(mult, 1)), NEG_INF)
    return tiles


def _window_mult(d):
    return ((d >= 0) & (d <= NSA_WINDOW - 1)).astype(np.float64)


def _dilated_mult(d):
    m = np.zeros(d.shape, np.float64)
    for window, dil in DILATED_CONFIGS:
        m += ((d >= 0) & (d <= window) & (d % dil == 0)).astype(np.float64)
    return m


def _banded_kernel(*refs, seq, window, slopes, shared_kv):
    if shared_kv:
        q_ref, kv_ref, gq_ref, gk_ref, bias_ref, o_ref, kp_s, vt_s, qp_s, s_s, acc_s = refs
    else:
        q_ref, k_ref, v_ref, gq_ref, gk_ref, bias_ref, o_ref, kp_s, vt_s, qp_s, s_s, acc_s = refs
    qi = pl.program_id(1)
    nkt = seq // TK

    @pl.when(qi == 0)
    def _prep():
        for c in range(nkt):
            rows = slice(c * TK, (c + 1) * TK)
            pos = c * TK + lax.broadcasted_iota(jnp.int32, (TK, 1), 0)
            if shared_kv:
                blk = kv_ref[rows, :]
                kn = _group_rms(blk, gk_ref[...])
                kp_s[rows, 0:LANES], _ = _pair_keys(kn, pos)
                _, kp_s[rows, LANES:2 * LANES] = _pair_keys(pltpu.roll(kn, HEAD_DIM, 1), pos)
                vt_s[c] = _values_t(blk.T[HEAD_DIM:, :])
            else:
                kn = _group_rms(k_ref[rows, :], gk_ref[...])
                for i in range(N_HEADS // 2):
                    even, odd = _pair_keys(kn[:, i * LANES:(i + 1) * LANES], pos)
                    kp_s[rows, 2 * i * LANES:(2 * i + 1) * LANES] = even
                    kp_s[rows, (2 * i + 1) * LANES:(2 * i + 2) * LANES] = odd
                v_t = v_ref[rows, :].T
                for h in range(N_HEADS):
                    vt_s[c, h * V_ROWS:(h + 1) * V_ROWS, :] = _values_t(v_t[h * HEAD_DIM:(h + 1) * HEAD_DIM, :])

    t0 = qi * TQ
    _prep_queries(q_ref[...], gq_ref[...], t0, slopes, qp_s)
    kt_lo = jnp.maximum(t0 - window, 0) // TK

    def kp_fn(kt):
        off = pl.multiple_of(kt * TK, TK)
        if shared_kv:
            return lambda h: kp_s[pl.ds(off, TK), (h % 2) * LANES:(h % 2 + 1) * LANES]
        return lambda h: kp_s[pl.ds(off, TK), h * LANES:(h + 1) * LANES]

    def vt_fn(kt):
        if shared_kv:
            return lambda h: vt_s[kt]
        return lambda h: vt_s[kt, h * V_ROWS:(h + 1) * V_ROWS, :]

    top_kt = qi * N_TOP + (N_TOP - 1)
    _flash_pipelined(lambda i: top_kt - i, top_kt + 1 - kt_lo, kp_fn, lambda kt, top: bias_ref[top_kt - kt], vt_fn,
                     qp_s, s_s, acc_s, o_ref)


def _banded(p2, gq, gk, bias_tiles, batch, seq, *, window, slopes, shared_kv, col_q, col_k, col_v, name):
    nq = seq // TQ
    t = batch * seq
    nd = bias_tiles.shape[0]
    n_kv = 1 if shared_kv else N_HEADS
    q_spec = pl.BlockSpec((TQ, GROUP_W), lambda b, i: (b * nq + i, col_q // GROUP_W))
    gain = lambda g: pl.BlockSpec(g.shape, lambda b, i: (0, 0))
    bias_spec = pl.BlockSpec((nd, TK, TQ), lambda b, i: (0, 0, 0))
    if shared_kv:
        kv_specs = [pl.BlockSpec((seq, LANES), lambda b, i: (b, col_k // LANES))]
        operands = (p2, p2, gq, gk, bias_tiles)
    else:
        kv_specs = [pl.BlockSpec((seq, GROUP_W), lambda b, i: (b, col_k // GROUP_W)),
                    pl.BlockSpec((seq, GROUP_W), lambda b, i: (b, col_v // GROUP_W))]
        operands = (p2, p2, p2, gq, gk, bias_tiles)
    return pl.pallas_call(
        functools.partial(_banded_kernel, seq=seq, window=window, slopes=slopes, shared_kv=shared_kv),
        grid=(batch, nq),
        in_specs=[q_spec] + kv_specs + [gain(gq), gain(gk), bias_spec],
        out_specs=pl.BlockSpec((1, GROUP_W, TQ), lambda b, i: (b * nq + i, 0, 0)),
        out_shape=jax.ShapeDtypeStruct((t // TQ, GROUP_W, TQ), F32),
        scratch_shapes=[pltpu.VMEM((seq, (2 if shared_kv else N_HEADS) * LANES), BF16),
                        pltpu.VMEM((seq // TK, n_kv * V_ROWS, TK), BF16),
                        pltpu.VMEM((N_HEADS, TQ, LANES), BF16),
                        pltpu.VMEM((N_HEADS, TK, TQ), F32),
                        pltpu.VMEM((N_HEADS, V_ROWS, TQ), F32)],
        compiler_params=pltpu.CompilerParams(dimension_semantics=("parallel", "arbitrary"),
                                             vmem_limit_bytes=VMEM_LIMIT),
        name=name,
    )(*operands)


def _stick_kernel(q_ref, k_ref, v_ref, o_ref, kp_s, vt_s, qp_s, sp_s, e_s, acc_s, *, seq):
    TQ, N_TOP = STICK_TQ, STICK_TQ // TK
    qi = pl.program_id(1)
    nkt = seq // TK

    @pl.when(qi == 0)
    def _prep():
        for c in range(nkt):
            rows = slice(c * TK, (c + 1) * TK)
            kb = k_ref[rows, :]
            for i in range(N_HEADS // 2):
                even, odd = _pair_keys(kb[:, i * LANES:(i + 1) * LANES], None)
                kp_s[rows, 2 * i * LANES:(2 * i + 1) * LANES] = even
                kp_s[rows, (2 * i + 1) * LANES:(2 * i + 2) * LANES] = odd
            vt_s[c] = v_ref[rows, :].T.astype(BF16)

    _prep_queries(q_ref[...], None, qi * TQ, None, qp_s)
    half = TK // 2
    col = lax.broadcasted_iota(jnp.int32, (half + SUM_PAD, half), 1)
    srow = lax.broadcasted_iota(jnp.int32, (half + SUM_PAD, half), 0)
    sums = ((srow == half) | ((srow < half) & (col > srow))).astype(BF16)
    k_local = lax.broadcasted_iota(jnp.int32, (TK, TQ), 0)
    q_local = lax.broadcasted_iota(jnp.int32, (TK, TQ), 1)
    top_kt = qi * N_TOP + (N_TOP - 1)
    acc_s[...] = jnp.zeros(acc_s.shape, F32)

    def past_mask(tiles_above):
        return tiles_above * TK + k_local < q_local

    def logits(kt, h):
        off = pl.multiple_of(kt * TK, TK)
        return lax.dot_general(kp_s[pl.ds(off, TK), h * LANES:(h + 1) * LANES], qp_s[h // 2], _NT,
                               preferred_element_type=F32)

    def stage(h, z, past):
        sp = jnp.maximum(z, 0.0) + jnp.log(1.0 + jnp.exp2(jnp.abs(z) * (-LOG2E)))
        spm = sp if past is None else jnp.where(past, sp, 0.0)
        sp_s[h] = spm.astype(BF16)
        logsig = z - sp
        e_s[h] = logsig if past is None else jnp.where(past, logsig, NEG_INF)

    def tails(h):
        return [jnp.dot(sums, sp_s[h, b * half:(b + 1) * half, :], preferred_element_type=F32) for b in range(2)]

    def consume(kt, h, w, later):
        later_lo = later + w[1][half:half + 1, :]
        attn = jnp.concatenate([jnp.exp(e_s[h, 0:half, :] - w[0][0:half, :] - later_lo),
                                jnp.exp(e_s[h, half:TK, :] - w[1][0:half, :] - later)], axis=0)
        acc_s[h] += jnp.dot(vt_s[kt, h * HEAD_DIM:(h + 1) * HEAD_DIM, :], attn.astype(BF16),
                            preferred_element_type=F32)
        return later_lo + w[0][half:half + 1, :]

    ahead = MXU_AHEAD
    for h in range(N_HEADS):
        stage(h, logits(top_kt, h), past_mask(N_TOP - 1))

    def step(kt, carry, past):
        laters, w_first, z_first = carry
        w = {0: list(w_first)}
        z = {0: z_first}
        for h in range(1, ahead):
            w[h] = tails(h)
            z[h] = logits(kt - 1, h)
        out = []
        for h in range(N_HEADS):
            out.append(consume(kt, h, w.pop(h), laters[h]))
            stage(h, z.pop(h), past)
            if h + ahead < N_HEADS:
                w[h + ahead] = tails(h + ahead)
                z[h + ahead] = logits(kt - 1, h + ahead)
            elif h + ahead == N_HEADS:
                nxt = (tuple(tails(0)), logits(jnp.maximum(kt - 2, 0), 0))
        return tuple(out), nxt[0], nxt[1]

    carry = (tuple(jnp.zeros((1, TQ), F32) for _ in range(N_HEADS)),
             tuple(tails(0)), logits(jnp.maximum(top_kt - 1, 0), 0))
    for j in range(N_TOP - 1):
        carry = step(top_kt - j, carry, past_mask(N_TOP - 2 - j))
    below = qi * N_TOP
    laters, w_first, _ = lax.fori_loop(0, below, lambda i, c: step(below - i, c, None), carry)
    w = {0: list(w_first)}
    for h in range(1, ahead):
        w[h] = tails(h)
    for h in range(N_HEADS):
        consume(0, h, w.pop(h), laters[h])
        if h + ahead < N_HEADS:
            w[h + ahead] = tails(h + ahead)
    o_ref[0] = acc_s[...].reshape(GROUP_W, TQ)


def _stick(p2, batch, seq):
    TQ = STICK_TQ
    nq = seq // TQ
    t = batch * seq
    return pl.pallas_call(
        functools.partial(_stick_kernel, seq=seq),
        grid=(batch, nq),
        in_specs=[pl.BlockSpec((TQ, GROUP_W), lambda b, i: (b * nq + i, COL_QC // GROUP_W)),
                  pl.BlockSpec((seq, GROUP_W), lambda b, i: (b, COL_KC // GROUP_W)),
                  pl.BlockSpec((seq, GROUP_W), lambda b, i: (b, COL_VC // GROUP_W))],
        out_specs=pl.BlockSpec((1, GROUP_W, TQ), lambda b, i: (b * nq + i, 0, 0)),
        out_shape=jax.ShapeDtypeStruct((t // TQ, GROUP_W, TQ), F32),
        scratch_shapes=[pltpu.VMEM((seq, N_HEADS * LANES), BF16),
                        pltpu.VMEM((seq // TK, GROUP_W, TK), BF16),
                        pltpu.VMEM((N_HEADS // 2, TQ, LANES), BF16),
                        pltpu.VMEM((N_HEADS, TK, TQ), BF16),
                        pltpu.VMEM((N_HEADS, TK, TQ), F32),
                        pltpu.VMEM((N_HEADS, HEAD_DIM, TQ), F32)],
        compiler_params=pltpu.CompilerParams(dimension_semantics=("parallel", "arbitrary"),
                                             vmem_limit_bytes=VMEM_LIMIT),
        name="stick_breaking",
    )(p2, p2, p2)


def _mixout_kernel(x_ref, ocmp_ref, oslc_ref, owin_ref, gate_ref, cvb_ref, cvc_ref, cvu_ref, pc_ref, pu_ref,
                   oc_ref, od_ref, bg_ref, cw_ref, gout_ref, wout_ref, o_ref, *, tiles_per_seq):
    i = pl.program_id(0)
    tm, d = x_ref.shape
    chains = [slice(c * MIX_CHAIN, (c + 1) * MIX_CHAIN) for c in range(tm // MIX_CHAIN)]

    def heads(ref, rows):
        tq = ref.shape[2]
        return ref[rows.start // tq, :, rows.start % tq:rows.start % tq + MIX_CHAIN].T

    cu = cvc_ref[...] * cvu_ref[...]
    prev = jnp.where(i % tiles_per_seq == 0, 0.0, pc_ref[...] * pu_ref[...])
    full = jnp.concatenate([prev, cu], axis=0)
    back1 = pltpu.roll(full, 1, 0)[8:, :]
    back2 = pltpu.roll(full, 2, 0)[8:, :]
    cw = cw_ref[...]
    ob = cvb_ref[...] * (cw[0:1, :] * back2 + cw[1:2, :] * back1 + cw[2:3, :] * cu)

    gates = jax.nn.sigmoid(gate_ref[...] + bg_ref[...])
    src = lax.broadcasted_iota(jnp.int32, (LANES, GROUP_W), 0)
    head3 = (lax.broadcasted_iota(jnp.int32, (LANES, GROUP_W), 1) >> 6) * 3
    branch = [(src == head3 + r).astype(BF16) for r in range(3)]
    assert d // HEAD_DIM <= LANES
    gather = ((lax.broadcasted_iota(jnp.int32, (d, LANES), 0) >> 6)
              == lax.broadcasted_iota(jnp.int32, (d, LANES), 1)).astype(BF16)
    spread = (lax.broadcasted_iota(jnp.int32, (LANES, d), 0)
              == (lax.broadcasted_iota(jnp.int32, (LANES, d), 1) >> 6)).astype(BF16)

    groups, ssq = [], []
    for rows in chains:
        oa = (_exact_dot(gates[rows], branch[0]) * heads(ocmp_ref, rows)
              + _exact_dot(gates[rows], branch[1]) * heads(oslc_ref, rows)
              + _exact_dot(gates[rows], branch[2]) * heads(owin_ref, rows))
        g = jnp.concatenate([oa, ob[rows], heads(oc_ref, rows), heads(od_ref, rows)], axis=1)
        groups.append(g)
        ssq.append(_exact_dot(g * g, gather))
    scale = [_exact_dot(lax.rsqrt(s * (1.0 / HEAD_DIM) + RMS_EPS), spread) for s in ssq]
    for rows, g, sc in zip(chains, groups, scale):
        mixed = (g * sc * gout_ref[...]).astype(BF16)
        o_ref[rows, :] = x_ref[rows, :] + jnp.dot(mixed, wout_ref[...], preferred_element_type=F32)


def _mixout(x2, p2, ocmp, oslc, owin, oc, od, bg, cw, gout, wout, seq):
    t, d = x2.shape
    tm = 2 * MIX_CHAIN
    rows = lambda w, col: pl.BlockSpec((tm, w), lambda i: (i, col // w))
    heads_t = lambda tq: pl.BlockSpec((tm // tq, GROUP_W, tq), lambda i: (i, 0, 0))
    prev8 = lambda col: pl.BlockSpec((8, GROUP_W), lambda i: (jnp.maximum(i * (tm // 8) - 1, 0), col // GROUP_W))
    const = lambda shape: pl.BlockSpec(shape, lambda i: (0, 0))
    assert tm % TQ == 0 and tm % STICK_TQ == 0
    return pl.pallas_call(
        functools.partial(_mixout_kernel, tiles_per_seq=seq // tm),
        grid=(t // tm,),
        in_specs=[rows(d, 0), heads_t(TQ), heads_t(TQ), heads_t(TQ),
                  rows(LANES, COL_GATE), rows(GROUP_W, COL_CVB), rows(GROUP_W, COL_CVC), rows(GROUP_W, COL_CVU),
                  prev8(COL_CVC), prev8(COL_CVU),
                  heads_t(STICK_TQ), heads_t(TQ),
                  const((1, LANES)), const((8, GROUP_W)), const((1, d)), const((d, d))],
        out_specs=rows(d, 0),
        out_shape=jax.ShapeDtypeStruct((t, d), F32),
        compiler_params=pltpu.CompilerParams(dimension_semantics=("parallel",), vmem_limit_bytes=VMEM_LIMIT),
        name="mixout",
    )(x2, ocmp, oslc, owin, p2, p2, p2, p2, p2, p2, oc, od, bg, cw, gout, wout)


def _ffn_kernel(x_ref, g_ref, wg_ref, wu_ref, wd_ref, o_ref, act_s, *, ff_chunk, out_chunk):
    h = _rms(x_ref[...], g_ref[...]).astype(BF16)
    dff = wg_ref.shape[1]
    for c in range(dff // ff_chunk):
        cols = slice(c * ff_chunk, (c + 1) * ff_chunk)
        a = jnp.dot(h, wg_ref[:, cols], preferred_element_type=F32)
        u = jnp.dot(h, wu_ref[:, cols], preferred_element_type=F32)
        act_s[:, cols] = (a * jax.nn.sigmoid(a) * u).astype(BF16)
    d = o_ref.shape[1]
    for c in range(d // out_chunk):
        cols = slice(c * out_chunk, (c + 1) * out_chunk)
        o_ref[:, cols] = x_ref[:, cols] + jnp.dot(act_s[...], wd_ref[:, cols], preferred_element_type=F32)


def _ffn(x2, g, wg, wu, wd):
    t, d = x2.shape
    dff = wg.shape[1]
    tm = 1024
    resident = lambda shape: pl.BlockSpec(shape, lambda i: (0, 0), pipeline_mode=pl.Buffered(1))
    return pl.pallas_call(
        functools.partial(_ffn_kernel, ff_chunk=256, out_chunk=256),
        grid=(t // tm,),
        in_specs=[pl.BlockSpec((tm, d), lambda i: (i, 0)),
                  pl.BlockSpec((1, d), lambda i: (0, 0)),
                  resident((d, dff)), resident((d, dff)), resident((dff, d))],
        out_specs=pl.BlockSpec((tm, d), lambda i: (i, 0)),
        out_shape=jax.ShapeDtypeStruct((t, d), F32),
        scratch_shapes=[pltpu.VMEM((tm, dff), BF16)],
        compiler_params=pltpu.CompilerParams(dimension_semantics=("parallel",), vmem_limit_bytes=VMEM_LIMIT),
        name="ffn",
    )(x2, g, wg, wu, wd)


def _permute_w_in_kernel(w_ref, o_ref):
    rows = w_ref.shape[0]
    n_rest = P_COLS - COL_CVB
    o_ref[:, 0:COL_GATE] = w_ref[:, 0:COL_GATE].astype(BF16)
    lane = lax.broadcasted_iota(jnp.int32, (rows, LANES), 1)
    o_ref[:, COL_GATE:COL_GATE + LANES] = jnp.where(lane < N_GATES, w_ref[:, COL_GATE:COL_GATE + LANES], 0.0).astype(BF16)
    o_ref[:, COL_CVB:P_COLS] = w_ref[:, COL_GATE + N_GATES:COL_GATE + N_GATES + n_rest].astype(BF16)


def _permute_w_in(w_in):
    d, cols = w_in.shape
    tr = 256
    return pl.pallas_call(
        _permute_w_in_kernel,
        grid=(d // tr,),
        in_specs=[pl.BlockSpec((tr, cols), lambda i: (i, 0))],
        out_specs=pl.BlockSpec((tr, P_COLS), lambda i: (i, 0)),
        out_shape=jax.ShapeDtypeStruct((d, P_COLS), BF16),
        compiler_params=pltpu.CompilerParams(dimension_semantics=("parallel",), vmem_limit_bytes=VMEM_LIMIT),
        name="permute_w_in",
    )(w_in)


def _layer(x2, batch, seq, g_mix, w_in, b_gate, g_q_nsa, g_k_cmp, g_k_slc, g_k_win, pe_k_cmp, pe_v_cmp,
           w1_k_cmp, w2_k_cmp, w1_v_cmp, w2_v_cmp, conv_w, g_q_dil, g_k_dil, g_out, w_out,
           g_ffn, w_gate, w_up, w_down):
    row = lambda v: v.reshape(1, -1)
    per_group = lambda g, n: jnp.tile(g, n).reshape(1, -1)
    p2 = _inproj(x2, row(g_mix), _permute_w_in(w_in))

    pe, w1, w2 = _compress_weights(pe_k_cmp, pe_v_cmp, w1_k_cmp, w2_k_cmp, w1_v_cmp, w2_v_cmp)
    kc, vc = _compress(p2, pe, w1, w2, per_group(g_k_cmp, 2), batch, seq)
    ocmp, oslc = _nsa(p2, kc, vc, per_group(g_q_nsa, N_HEADS), per_group(g_k_slc, 2), batch, seq)
    owin = _banded(p2, per_group(g_q_nsa, N_HEADS), per_group(g_k_win, 2), jnp.asarray(_band_bias_tiles(NSA_WINDOW - 1, _window_mult)),
                   batch, seq, window=NSA_WINDOW - 1, slopes=SLOPES_A, shared_kv=True,
                   col_q=COL_QA, col_k=COL_KWV, col_v=COL_KWV, name="nsa_window")
    od = _banded(p2, per_group(g_q_dil, N_HEADS), per_group(g_k_dil, N_HEADS),
                 jnp.asarray(_band_bias_tiles(DILATED_CONFIGS[-1][0], _dilated_mult)),
                 batch, seq, window=DILATED_CONFIGS[-1][0], slopes=SLOPES_D, shared_kv=False,
                 col_q=COL_QD, col_k=COL_KD, col_v=COL_VD, name="dilated")
    oc = _stick(p2, batch, seq)

    bg = jnp.zeros((1, LANES), F32).at[0, :N_GATES].set(b_gate)
    cwp = jnp.zeros((8, GROUP_W), F32).at[:CONV_K].set(conv_w)
    x1 = _mixout(x2, p2, ocmp, oslc, owin, oc, od, bg, cwp, row(g_out), w_out.astype(BF16), seq)
    return _ffn(x1, row(g_ffn), w_gate.astype(BF16), w_up.astype(BF16), w_down.astype(BF16))


def kernel(x, g_mix, w_in, b_gate, g_q_nsa, g_k_cmp, g_k_slc, g_k_win, pe_k_cmp, pe_v_cmp, w1_k_cmp, w2_k_cmp,
           w1_v_cmp, w2_v_cmp, conv_w, g_q_dil, g_k_dil, g_out, w_out, g_ffn, w_gate, w_up, w_down):
    batch, seq, d = x.shape
    assert seq % (CMP_STRIDE * LANES) == 0 and d % LANES == 0
    x2 = x.reshape(batch * seq, d)
    params = (g_mix, w_in, b_gate, g_q_nsa, g_k_cmp, g_k_slc, g_k_win, pe_k_cmp, pe_v_cmp, w1_k_cmp, w2_k_cmp,
              w1_v_cmp, w2_v_cmp, conv_w, g_q_dil, g_k_dil, g_out, w_out, g_ffn, w_gate, w_up, w_down)
    for layer in range(g_mix.shape[0]):
        x2 = _layer(x2, batch, seq, *[p[layer] for p in params])
    return x2.reshape(batch, seq, d)
```

```python
import functools

import numpy as np
import jax
import jax.numpy as jnp
from jax import lax
from jax.experimental import pallas as pl
from jax.experimental.pallas import tpu as pltpu

F32 = jnp.float32
BF16 = jnp.bfloat16

HEAD_DIM = 64
N_HEADS = 4
GROUP_W = N_HEADS * HEAD_DIM
CONV_K = 3
CMP_LEN = 32
CMP_STRIDE = 16
SEL_BLOCK = 64
N_SELECT = 16
NSA_WINDOW = 512
DILATED_CONFIGS = ((128, 1), (512, 4), (2048, 16))
NEG_INF = -1e30
FORCE_SCORE = 1e6
RMS_EPS = 1e-6

TQ = 512
TK = 256
N_TOP = TQ // TK
STICK_TQ = 256
LANES = 128
MXU_DEPTH = 256
MIX_CHAIN = 256
MXU_AHEAD = 2
FLASH_AHEAD = 2
SUM_PAD = 16
V_ROWS = HEAD_DIM + SUM_PAD
LOG2E = 1.4426950408889634
VMEM_LIMIT = 52 * 1024 * 1024

COL_QA, COL_KVC, COL_KSV, COL_KWV, COL_GATE = 0, 256, 384, 512, 640
COL_CVB, COL_CVC, COL_CVU = 768, 1024, 1280
COL_QC, COL_KC, COL_VC = 1536, 1792, 2048
COL_QD, COL_KD, COL_VD = 2304, 2560, 2816
P_COLS = 3072
N_GATES = 12

_NT = (((1,), (1,)), ((), ()))


def _alibi_slopes():
    s = [2.0 ** (-8.0 * i / 8) for i in range(1, 9)]
    return tuple(s[0::2]), tuple(s[1::2])


SLOPES_A, SLOPES_D = _alibi_slopes()


def _rms(x, g):
    return x * lax.rsqrt(jnp.mean(x * x, axis=-1, keepdims=True) + RMS_EPS) * g


def _exact_dot(a, sel):
    hi = a.astype(BF16)
    lo = (a - hi.astype(F32)).astype(BF16)
    if 2 * a.shape[1] <= MXU_DEPTH:
        return jnp.dot(jnp.concatenate([hi, lo], axis=1), jnp.concatenate([sel, sel], axis=0),
                       preferred_element_type=F32)
    return jnp.dot(hi, sel, preferred_element_type=F32) + jnp.dot(lo, sel, preferred_element_type=F32)


def _group_rms(x, g):
    w = x.shape[1]
    same = ((lax.broadcasted_iota(jnp.int32, (w, w), 0) >> 6)
            == (lax.broadcasted_iota(jnp.int32, (w, w), 1) >> 6)).astype(BF16)
    ssq = _exact_dot(x * x, same)
    return x * lax.rsqrt(ssq * (1.0 / HEAD_DIM) + RMS_EPS) * g


EVEN_AUG, ODD_AUG = HEAD_DIM, 0


def _key_aug(pos, first):
    lane = lax.broadcasted_iota(jnp.int32, (pos.shape[0], LANES), 1) - first
    hi = (pos >> 6).astype(F32)
    lo = (pos & 63).astype(F32)
    return jnp.where(lane == 0, hi, jnp.where(lane == 1, lo, jnp.where((lane == 2) | (lane == 3), 1.0, 0.0)))


def _query_aug(t, slope, first):
    lane = lax.broadcasted_iota(jnp.int32, (t.shape[0], LANES), 1) - first
    hi = (t >> 6).astype(F32) * (-64.0 * slope)
    lo = (t & 63).astype(F32) * (-slope)
    return jnp.where(lane == 0, 64.0 * slope,
                     jnp.where(lane == 1, slope, jnp.where(lane == 2, hi, jnp.where(lane == 3, lo, 0.0))))


def _pair_keys(pair, pos):
    lane = lax.broadcasted_iota(jnp.int32, pair.shape, 1)
    aug_e = 0.0 if pos is None else _key_aug(pos, EVEN_AUG)
    aug_o = 0.0 if pos is None else _key_aug(pos, ODD_AUG)
    return (jnp.where(lane < HEAD_DIM, pair, aug_e).astype(BF16),
            jnp.where(lane >= HEAD_DIM, pair, aug_o).astype(BF16))


def _prep_queries(q, g, t0, slopes, qp_s):
    qn = (q if g is None else _group_rms(q, g)) * (HEAD_DIM ** -0.5)
    tpos = t0 + lax.broadcasted_iota(jnp.int32, (q.shape[0], 1), 0)
    lane = lax.broadcasted_iota(jnp.int32, (q.shape[0], LANES), 1)
    for i in range(N_HEADS // 2):
        pair = qn[:, i * LANES:(i + 1) * LANES]
        if slopes is None:
            qp_s[i] = pair.astype(BF16)
            continue
        even = jnp.where((lane >= EVEN_AUG) & (lane < EVEN_AUG + 4), _query_aug(tpos, slopes[2 * i], EVEN_AUG), pair)
        odd = jnp.where(lane < ODD_AUG + 4, _query_aug(tpos, slopes[2 * i + 1], ODD_AUG), pair)
        qp_s[2 * i] = even.astype(BF16)
        qp_s[2 * i + 1] = odd.astype(BF16)


def _inproj_kernel(x_ref, g_ref, w_ref, o_ref, *, nchunk):
    h = _rms(x_ref[...], g_ref[...]).astype(BF16)
    cw = P_COLS // nchunk
    for c in range(nchunk):
        o_ref[:, c * cw:(c + 1) * cw] = jnp.dot(h, w_ref[:, c * cw:(c + 1) * cw], preferred_element_type=F32)


def _inproj(x2, g, w):
    t, d = x2.shape
    tm = 1024
    return pl.pallas_call(
        functools.partial(_inproj_kernel, nchunk=6),
        grid=(t // tm,),
        in_specs=[pl.BlockSpec((tm, d), lambda i: (i, 0)),
                  pl.BlockSpec((1, d), lambda i: (0, 0)),
                  pl.BlockSpec((d, P_COLS), lambda i: (0, 0), pipeline_mode=pl.Buffered(1))],
        out_specs=pl.BlockSpec((tm, P_COLS), lambda i: (i, 0)),
        out_shape=jax.ShapeDtypeStruct((t, P_COLS), F32),
        compiler_params=pltpu.CompilerParams(dimension_semantics=("parallel",), vmem_limit_bytes=VMEM_LIMIT),
        name="inproj",
    )(x2, g, w)


def _gelu_tanh(x):
    return x * (0.5 * (1.0 + jnp.tanh(np.sqrt(2.0 / np.pi).astype(np.float32) * (x + 0.044715 * (x * x * x)))))


def _compress_kernel(kv_ref, pe_ref, w1_ref, w2_ref, gk_ref, kc_ref, vct_ref):
    nc = kv_ref.shape[0] // CMP_STRIDE
    first = None
    second = None
    for j in range(CMP_STRIDE):
        tok = kv_ref[pl.ds(j, nc, stride=CMP_STRIDE), :]
        a = jnp.dot((tok + pe_ref[j:j + 1, :]).astype(BF16), w1_ref[j], preferred_element_type=F32)
        b = jnp.dot((tok + pe_ref[CMP_STRIDE + j:CMP_STRIDE + j + 1, :]).astype(BF16), w1_ref[CMP_STRIDE + j],
                    preferred_element_type=F32)
        first = a if first is None else first + a
        second = b if second is None else second + b
    hid = first + pltpu.roll(second, nc - 1, 0)
    out = jnp.dot(_gelu_tanh(hid).astype(BF16), w2_ref[...], preferred_element_type=F32)
    kn = _group_rms(out, gk_ref[...])
    end = lax.broadcasted_iota(jnp.int32, (nc, 1), 0) * CMP_STRIDE + (CMP_LEN - 1)
    even, _ = _pair_keys(kn, end)
    _, odd = _pair_keys(pltpu.roll(kn, HEAD_DIM, 1), end)
    kc_ref[0] = jnp.concatenate([even, odd], axis=1)
    vct_ref[0] = out.T[HEAD_DIM:, :].astype(BF16)


def _compress(p2, pe, w1, w2, gk, batch, seq):
    nc = seq // CMP_STRIDE
    hid2 = w1.shape[2]
    return pl.pallas_call(
        _compress_kernel,
        grid=(batch,),
        in_specs=[pl.BlockSpec((seq, LANES), lambda i: (i, COL_KVC // LANES)),
                  pl.BlockSpec((CMP_LEN, LANES), lambda i: (0, 0)),
                  pl.BlockSpec((CMP_LEN, LANES, hid2), lambda i: (0, 0, 0)),
                  pl.BlockSpec((hid2, LANES), lambda i: (0, 0)),
                  pl.BlockSpec((1, LANES), lambda i: (0, 0))],
        out_specs=[pl.BlockSpec((1, nc, 2 * LANES), lambda i: (i, 0, 0)),
                   pl.BlockSpec((1, HEAD_DIM, nc), lambda i: (i, 0, 0))],
        out_shape=[jax.ShapeDtypeStruct((batch, nc, 2 * LANES), BF16),
                   jax.ShapeDtypeStruct((batch, HEAD_DIM, nc), BF16)],
        compiler_params=pltpu.CompilerParams(dimension_semantics=("parallel",), vmem_limit_bytes=VMEM_LIMIT),
        name="nsa_compress",
    )(p2, pe, w1, w2, gk)


def _compress_weights(pe_k, pe_v, w1_k, w2_k, w1_v, w2_v):
    hid = w1_k.shape[1]
    w1k = w1_k.reshape(CMP_LEN, HEAD_DIM, hid)
    w1v = w1_v.reshape(CMP_LEN, HEAD_DIM, hid)
    z1 = jnp.zeros_like(w1k)
    w1 = jnp.concatenate([jnp.concatenate([w1k, z1], axis=2), jnp.concatenate([z1, w1v], axis=2)], axis=1)
    z2 = jnp.zeros_like(w2_k)
    w2 = jnp.concatenate([jnp.concatenate([w2_k, z2], axis=1), jnp.concatenate([z2, w2_v], axis=1)], axis=0)
    pe = jnp.concatenate([pe_k, pe_v], axis=1)
    return pe, w1.astype(BF16), w2.astype(BF16)


def _flash_scores(kp_of, bias, qp_s, s_s, m_cur):
    m_next = []
    for h in range(N_HEADS):
        s = lax.dot_general(kp_of(h), qp_s[h], _NT, preferred_element_type=F32) + bias
        s_s[h] = s
        m_next.append(jnp.maximum(m_cur[h], jnp.max(s, axis=0, keepdims=True)))
    return tuple(m_next)


def _flash_accum(vt_of, s_s, acc_s, m_prev, m_cur):
    for h in range(N_HEADS):
        alpha = jnp.exp(m_prev[h] - m_cur[h])
        p = jnp.exp(s_s[h] - m_cur[h])
        acc_s[h] = alpha * acc_s[h] + jnp.dot(vt_of(h), p.astype(BF16), preferred_element_type=F32)


def _flash_accum_and_scores(kp_of, bias, vt_of, qp_s, s_s, acc_s, m_prev, m_cur):
    def scores(h):
        return lax.dot_general(kp_of(h), qp_s[h], _NT, preferred_element_type=F32) + bias

    m_next = []
    s_new = {h: scores(h) for h in range(FLASH_AHEAD)}
    for h in range(N_HEADS):
        alpha = jnp.exp(m_prev[h] - m_cur[h])
        p = jnp.exp(s_s[h] - m_cur[h])
        acc_s[h] = alpha * acc_s[h] + jnp.dot(vt_of(h), p.astype(BF16), preferred_element_type=F32)
        s_h = s_new.pop(h)
        s_s[h] = s_h
        m_next.append(jnp.maximum(m_cur[h], jnp.max(s_h, axis=0, keepdims=True)))
        if h + FLASH_AHEAD < N_HEADS:
            s_new[h + FLASH_AHEAD] = scores(h + FLASH_AHEAD)
    return tuple(m_next)


def _flash_pipelined(tile_of, n_tiles, kp_fn, bias_fn, vt_fn, qp_s, s_s, acc_s, o_ref):
    acc_s[...] = jnp.zeros(acc_s.shape, F32)
    m0 = tuple(jnp.full((1, TQ), NEG_INF, F32) for _ in range(N_HEADS))
    first_kt = tile_of(0)
    m1 = _flash_scores(kp_fn(first_kt), bias_fn(first_kt, True), qp_s, s_s, m0)

    def step(i, carry, top):
        m_prev, m_cur = carry
        nxt = tile_of(i + 1)
        m_next = _flash_accum_and_scores(kp_fn(nxt), bias_fn(nxt, top), vt_fn(tile_of(i)), qp_s, s_s, acc_s,
                                         m_prev, m_cur)
        return m_cur, m_next

    carry = (m0, m1)
    for i in range(N_TOP - 1):
        carry = step(i, carry, True)
    m_prev, m_cur = lax.fori_loop(N_TOP - 1, n_tiles - 1, lambda i, c: step(i, c, False), carry)
    _flash_accum(vt_fn(tile_of(n_tiles - 1)), s_s, acc_s, m_prev, m_cur)
    o_ref[0] = jnp.concatenate([acc_s[h, 0:HEAD_DIM, :] * (1.0 / acc_s[h, HEAD_DIM:HEAD_DIM + 1, :])
                                for h in range(N_HEADS)], axis=0)


def _values_t(v_t):
    row = lax.broadcasted_iota(jnp.int32, (V_ROWS - HEAD_DIM, v_t.shape[1]), 0)
    return jnp.concatenate([v_t, jnp.where(row == 0, 1.0, 0.0)], axis=0).astype(BF16)


def _nsa_kernel(q_ref, kc_ref, vct_ref, ksv_ref, kwv_ref, gq_ref, gks_ref, gkw_ref, wbias_ref,
                ocmp_ref, oslc_ref, owin_ref,
                ksp_s, vst_s, kwp_s, vwt_s, qp_s, imp_s, sel_s, s_s, acc_s, hit_s, *, seq):
    qi = pl.program_id(1)
    nc = seq // CMP_STRIDE
    nsel = seq // SEL_BLOCK
    nkt = seq // TK

    @pl.when(qi == 0)
    def _prep():
        for c in range(nkt):
            rows = slice(c * TK, (c + 1) * TK)
            pos = c * TK + lax.broadcasted_iota(jnp.int32, (TK, 1), 0)
            for kv_ref, g_ref, kp_s, vt_s in ((ksv_ref, gks_ref, ksp_s, vst_s), (kwv_ref, gkw_ref, kwp_s, vwt_s)):
                blk = kv_ref[rows, :]
                kn = _group_rms(blk, g_ref[...])
                kp_s[rows, 0:LANES], _ = _pair_keys(kn, pos)
                _, kp_s[rows, LANES:2 * LANES] = _pair_keys(pltpu.roll(kn, HEAD_DIM, 1), pos)
                vt_s[c] = _values_t(blk.T[HEAD_DIM:, :])
        imp_s[:, 0:8, :] = jnp.zeros((TQ // LANES, 8, LANES), F32)

    t0 = qi * TQ
    _prep_queries(q_ref[...], gq_ref[...], t0, SLOPES_A, qp_s)

    kc = [kc_ref[0, :, 0:LANES], kc_ref[0, :, LANES:2 * LANES]]
    vct = vct_ref[0]
    tq_row = t0 + lax.broadcasted_iota(jnp.int32, (nc, TQ), 1)
    n_idx = lax.broadcasted_iota(jnp.int32, (nc, TQ), 0)
    vis = (tq_row >= n_idx * CMP_STRIDE + (CMP_LEN - 1)) & (n_idx < nc - 1)
    vis_bias = jnp.where(vis, 0.0, NEG_INF)
    sees_any = (t0 + lax.broadcasted_iota(jnp.int32, (1, TQ), 1) >= CMP_LEN - 1).astype(F32)
    imp = jnp.zeros((nc, TQ), F32)
    ocmp_t = []
    scores = [lax.dot_general(kc[h % 2], qp_s[h], _NT, preferred_element_type=F32) + vis_bias
              for h in range(N_HEADS)]
    for h in range(N_HEADS):
        sc = scores[h]
        e = jnp.exp(sc - jnp.max(sc, axis=0, keepdims=True))
        p = e * (sees_any / jnp.sum(e, axis=0, keepdims=True))
        ocmp_t.append(jnp.dot(vct, p.astype(BF16), preferred_element_type=F32))
        imp = imp + p
    ocmp_ref[0] = jnp.concatenate(ocmp_t, axis=0)

    halves = []
    for half in range(TQ // LANES):
        imp_s[half, 8:8 + nc, :] = imp[:, half * LANES:(half + 1) * LANES]
        r = [imp_s[half, pl.ds(8 + k, nsel, stride=4), :] for k in range(4)]
        rm1 = imp_s[half, pl.ds(7, nsel, stride=4), :]
        halves.append(rm1 + 2.0 * (r[0] + r[1] + r[2]) + r[3])
    imp_blk = jnp.concatenate(halves, axis=1)
    blk = lax.broadcasted_iota(jnp.int32, (nsel, TQ), 0)
    tl = t0 + lax.broadcasted_iota(jnp.int32, (nsel, TQ), 1)
    cur = tl >> 6
    forced = (blk == 0) | (blk == cur) | (blk == cur - 1)
    valid = blk * SEL_BLOCK <= tl
    score = jnp.where(forced, FORCE_SCORE, jnp.where(valid, imp_blk, -FORCE_SCORE))
    sub = 8
    per_tile = TK // SEL_BLOCK
    tile_hit = []
    for g in range(nsel // sub):
        mine = score[g * sub:(g + 1) * sub, :]
        blk_g = g * sub + lax.broadcasted_iota(jnp.int32, (sub, TQ), 0)
        rank = jnp.zeros((sub, TQ), F32)
        for j in range(nsel):
            row = score[j:j + 1, :]
            if j < g * sub:
                beats = row >= mine
            elif j >= (g + 1) * sub:
                beats = row > mine
            else:
                beats = (row > mine) | ((row == mine) & (blk_g > j))
            rank = rank + jnp.where(beats, 1.0, 0.0)
        sel_g = jnp.where(rank < min(N_SELECT, nsel), 0.0, NEG_INF)
        sel_s[g * sub:(g + 1) * sub, :] = sel_g
        for part in range(sub // per_tile):
            tile_hit.append(jnp.max(sel_g[part * per_tile:(part + 1) * per_tile, :]))

    n_hit = jnp.int32(0)
    for kt in reversed(range(nkt)):
        hit_s[n_hit] = kt
        n_hit = n_hit + ((tile_hit[kt] > -1.0) & (kt * TK < t0)).astype(jnp.int32)

    def block_bias(kt):
        return jnp.concatenate([jnp.broadcast_to(sel_s[pl.ds(kt * per_tile + j, 1), :], (SEL_BLOCK, TQ))
                                for j in range(per_tile)], axis=0)

    k_local = lax.broadcasted_iota(jnp.int32, (TK, TQ), 0)
    q_local = lax.broadcasted_iota(jnp.int32, (TK, TQ), 1)
    top_kt = qi * N_TOP + (N_TOP - 1)

    def kp_fn(kt):
        off = pl.multiple_of(kt * TK, TK)
        return lambda h: ksp_s[pl.ds(off, TK), (h % 2) * LANES:(h % 2 + 1) * LANES]

    def bias_fn(kt, top):
        if top:
            return jnp.where(kt * TK - t0 + k_local <= q_local, block_bias(kt), NEG_INF)
        return block_bias(kt)

    tile_of = lambda i: jnp.where(i < N_TOP, top_kt - i, hit_s[jnp.maximum(i - N_TOP, 0)])
    _flash_pipelined(tile_of, n_hit + N_TOP, kp_fn, bias_fn, lambda kt: (lambda h: vst_s[kt]), qp_s, s_s, acc_s,
                     oslc_ref)

    def window_kp_fn(kt):
        off = pl.multiple_of(kt * TK, TK)
        return lambda h: kwp_s[pl.ds(off, TK), (h % 2) * LANES:(h % 2 + 1) * LANES]

    kt_lo = jnp.maximum(t0 - (NSA_WINDOW - 1), 0) // TK
    _flash_pipelined(lambda i: top_kt - i, top_kt + 1 - kt_lo, window_kp_fn, lambda kt, top: wbias_ref[top_kt - kt],
                     lambda kt: (lambda h: vwt_s[kt]), qp_s, s_s, acc_s, owin_ref)


def _nsa(p2, kc, vc, gq, gks, gkw, batch, seq):
    nq = seq // TQ
    nc = seq // CMP_STRIDE
    t = batch * seq
    wbias = jnp.asarray(_band_bias_tiles(NSA_WINDOW - 1, _window_mult))
    head_tiles = pl.BlockSpec((1, GROUP_W, TQ), lambda b, i: (b * nq + i, 0, 0))
    head_shape = jax.ShapeDtypeStruct((t // TQ, GROUP_W, TQ), F32)
    return pl.pallas_call(
        functools.partial(_nsa_kernel, seq=seq),
        grid=(batch, nq),
        in_specs=[pl.BlockSpec((TQ, GROUP_W), lambda b, i: (b * nq + i, COL_QA // GROUP_W)),
                  pl.BlockSpec((1, nc, 2 * LANES), lambda b, i: (b, 0, 0)),
                  pl.BlockSpec((1, HEAD_DIM, nc), lambda b, i: (b, 0, 0)),
                  pl.BlockSpec((seq, LANES), lambda b, i: (b, COL_KSV // LANES)),
                  pl.BlockSpec((seq, LANES), lambda b, i: (b, COL_KWV // LANES)),
                  pl.BlockSpec((1, GROUP_W), lambda b, i: (0, 0)),
                  pl.BlockSpec((1, LANES), lambda b, i: (0, 0)),
                  pl.BlockSpec((1, LANES), lambda b, i: (0, 0)),
                  pl.BlockSpec(wbias.shape, lambda b, i: (0, 0, 0))],
        out_specs=[head_tiles, head_tiles, head_tiles],
        out_shape=[head_shape, head_shape, head_shape],
        scratch_shapes=[pltpu.VMEM((seq, 2 * LANES), BF16),
                        pltpu.VMEM((seq // TK, V_ROWS, TK), BF16),
                        pltpu.VMEM((seq, 2 * LANES), BF16),
                        pltpu.VMEM((seq // TK, V_ROWS, TK), BF16),
                        pltpu.VMEM((N_HEADS, TQ, LANES), BF16),
                        pltpu.VMEM((TQ // LANES, 8 + nc, LANES), F32),
                        pltpu.VMEM((seq // SEL_BLOCK, TQ), F32),
                        pltpu.VMEM((N_HEADS, TK, TQ), F32),
                        pltpu.VMEM((N_HEADS, V_ROWS, TQ), F32),
                        pltpu.SMEM((seq // TK + 1,), jnp.int32)],
        compiler_params=pltpu.CompilerParams(dimension_semantics=("parallel", "arbitrary"),
                                             vmem_limit_bytes=VMEM_LIMIT),
        name="nsa",
    )(p2, kc, vc, p2, p2, gq, gks, gkw, wbias)


def _band_bias_tiles(window, mult_fn):
    nd = (window + TQ - 1) // TK + 1
    kk = np.arange(TK)[:, None]
    qq = np.arange(TQ)[None, :]
    tiles = np.empty((nd, TK, TQ), np.float32)
    for di in range(nd):
        d = (di - (N_TOP - 1)) * TK + qq - kk
        mult = mult_fn(d)
        tiles[di] = np.where(mult > 0, np.log(np.maximum(mult, 1)), NEG_INF)
    return tiles


def _window_mult(d):
    return ((d >= 0) & (d <= NSA_WINDOW - 1)).astype(np.float64)


def _dilated_mult(d):
    m = np.zeros(d.shape, np.float64)
    for window, dil in DILATED_CONFIGS:
        m += ((d >= 0) & (d <= window) & (d % dil == 0)).astype(np.float64)
    return m


def _banded_kernel(*refs, seq, window, slopes, shared_kv):
    if shared_kv:
        q_ref, kv_ref, gq_ref, gk_ref, bias_ref, o_ref, kp_s, vt_s, qp_s, s_s, acc_s = refs
    else:
        q_ref, k_ref, v_ref, gq_ref, gk_ref, bias_ref, o_ref, kp_s, vt_s, qp_s, s_s, acc_s = refs
    qi = pl.program_id(1)
    nkt = seq // TK

    @pl.when(qi == 0)
    def _prep():
        for c in range(nkt):
            rows = slice(c * TK, (c + 1) * TK)
            pos = c * TK + lax.broadcasted_iota(jnp.int32, (TK, 1), 0)
            if shared_kv:
                blk = kv_ref[rows, :]
                kn = _group_rms(blk, gk_ref[...])
                kp_s[rows, 0:LANES], _ = _pair_keys(kn, pos)
                _, kp_s[rows, LANES:2 * LANES] = _pair_keys(pltpu.roll(kn, HEAD_DIM, 1), pos)
                vt_s[c] = _values_t(blk.T[HEAD_DIM:, :])
            else:
                kn = _group_rms(k_ref[rows, :], gk_ref[...])
                for i in range(N_HEADS // 2):
                    even, odd = _pair_keys(kn[:, i * LANES:(i + 1) * LANES], pos)
                    kp_s[rows, 2 * i * LANES:(2 * i + 1) * LANES] = even
                    kp_s[rows, (2 * i + 1) * LANES:(2 * i + 2) * LANES] = odd
                v_t = v_ref[rows, :].T
                for h in range(N_HEADS):
                    vt_s[c, h * V_ROWS:(h + 1) * V_ROWS, :] = _values_t(v_t[h * HEAD_DIM:(h + 1) * HEAD_DIM, :])

    t0 = qi * TQ
    _prep_queries(q_ref[...], gq_ref[...], t0, slopes, qp_s)
    kt_lo = jnp.maximum(t0 - window, 0) // TK

    def kp_fn(kt):
        off = pl.multiple_of(kt * TK, TK)
        if shared_kv:
            return lambda h: kp_s[pl.ds(off, TK), (h % 2) * LANES:(h % 2 + 1) * LANES]
        return lambda h: kp_s[pl.ds(off, TK), h * LANES:(h + 1) * LANES]

    def vt_fn(kt):
        if shared_kv:
            return lambda h: vt_s[kt]
        return lambda h: vt_s[kt, h * V_ROWS:(h + 1) * V_ROWS, :]

    top_kt = qi * N_TOP + (N_TOP - 1)
    _flash_pipelined(lambda i: top_kt - i, top_kt + 1 - kt_lo, kp_fn, lambda kt, top: bias_ref[top_kt - kt], vt_fn,
                     qp_s, s_s, acc_s, o_ref)


def _banded(p2, gq, gk, bias_tiles, batch, seq, *, window, slopes, shared_kv, col_q, col_k, col_v, name):
    nq = seq // TQ
    t = batch * seq
    nd = bias_tiles.shape[0]
    n_kv = 1 if shared_kv else N_HEADS
    q_spec = pl.BlockSpec((TQ, GROUP_W), lambda b, i: (b * nq + i, col_q // GROUP_W))
    gain = lambda g: pl.BlockSpec(g.shape, lambda b, i: (0, 0))
    bias_spec = pl.BlockSpec((nd, TK, TQ), lambda b, i: (0, 0, 0))
    if shared_kv:
        kv_specs = [pl.BlockSpec((seq, LANES), lambda b, i: (b, col_k // LANES))]
        operands = (p2, p2, gq, gk, bias_tiles)
    else:
        kv_specs = [pl.BlockSpec((seq, GROUP_W), lambda b, i: (b, col_k // GROUP_W)),
                    pl.BlockSpec((seq, GROUP_W), lambda b, i: (b, col_v // GROUP_W))]
        operands = (p2, p2, p2, gq, gk, bias_tiles)
    return pl.pallas_call(
        functools.partial(_banded_kernel, seq=seq, window=window, slopes=slopes, shared_kv=shared_kv),
        grid=(batch, nq),
        in_specs=[q_spec] + kv_specs + [gain(gq), gain(gk), bias_spec],
        out_specs=pl.BlockSpec((1, GROUP_W, TQ), lambda b, i: (b * nq + i, 0, 0)),
        out_shape=jax.ShapeDtypeStruct((t // TQ, GROUP_W, TQ), F32),
        scratch_shapes=[pltpu.VMEM((seq, (2 if shared_kv else N_HEADS) * LANES), BF16),
                        pltpu.VMEM((seq // TK, n_kv * V_ROWS, TK), BF16),
                        pltpu.VMEM((N_HEADS, TQ, LANES), BF16),
                        pltpu.VMEM((N_HEADS, TK, TQ), F32),
                        pltpu.VMEM((N_HEADS, V_ROWS, TQ), F32)],
        compiler_params=pltpu.CompilerParams(dimension_semantics=("parallel", "arbitrary"),
                                             vmem_limit_bytes=VMEM_LIMIT),
        name=name,
    )(*operands)


def _stick_kernel(q_ref, k_ref, v_ref, o_ref, kp_s, vt_s, qp_s, sp_s, e_s, acc_s, *, seq):
    TQ, N_TOP = STICK_TQ, STICK_TQ // TK
    qi = pl.program_id(1)
    nkt = seq // TK

    @pl.when(qi == 0)
    def _prep():
        for c in range(nkt):
            rows = slice(c * TK, (c + 1) * TK)
            kb = k_ref[rows, :]
            for i in range(N_HEADS // 2):
                even, odd = _pair_keys(kb[:, i * LANES:(i + 1) * LANES], None)
                kp_s[rows, 2 * i * LANES:(2 * i + 1) * LANES] = even
                kp_s[rows, (2 * i + 1) * LANES:(2 * i + 2) * LANES] = odd
            vt_s[c] = v_ref[rows, :].T.astype(BF16)

    _prep_queries(q_ref[...], None, qi * TQ, None, qp_s)
    half = TK // 2
    col = lax.broadcasted_iota(jnp.int32, (half + SUM_PAD, half), 1)
    srow = lax.broadcasted_iota(jnp.int32, (half + SUM_PAD, half), 0)
    sums = ((srow == half) | ((srow < half) & (col > srow))).astype(BF16)
    k_local = lax.broadcasted_iota(jnp.int32, (TK, TQ), 0)
    q_local = lax.broadcasted_iota(jnp.int32, (TK, TQ), 1)
    top_kt = qi * N_TOP + (N_TOP - 1)
    acc_s[...] = jnp.zeros(acc_s.shape, F32)

    def past_mask(tiles_above):
        return tiles_above * TK + k_local < q_local

    def logits(kt, h):
        off = pl.multiple_of(kt * TK, TK)
        return lax.dot_general(kp_s[pl.ds(off, TK), h * LANES:(h + 1) * LANES], qp_s[h // 2], _NT,
                               preferred_element_type=F32)

    def stage(h, z, past):
        sp = jnp.maximum(z, 0.0) + jnp.log(1.0 + jnp.exp2(jnp.abs(z) * (-LOG2E)))
        spm = sp if past is None else jnp.where(past, sp, 0.0)
        sp_s[h] = spm.astype(BF16)
        logsig = z - sp
        e_s[h] = logsig if past is None else jnp.where(past, logsig, NEG_INF)

    def tails(h):
        return [jnp.dot(sums, sp_s[h, b * half:(b + 1) * half, :], preferred_element_type=F32) for b in range(2)]

    def consume(kt, h, w, later):
        later_lo = later + w[1][half:half + 1, :]
        attn = jnp.concatenate([jnp.exp(e_s[h, 0:half, :] - w[0][0:half, :] - later_lo),
                                jnp.exp(e_s[h, half:TK, :] - w[1][0:half, :] - later)], axis=0)
        acc_s[h] += jnp.dot(vt_s[kt, h * HEAD_DIM:(h + 1) * HEAD_DIM, :], attn.astype(BF16),
                            preferred_element_type=F32)
        return later_lo + w[0][half:half + 1, :]

    ahead = MXU_AHEAD
    for h in range(N_HEADS):
        stage(h, logits(top_kt, h), past_mask(N_TOP - 1))

    def step(kt, carry, past):
        laters, w_first, z_first = carry
        w = {0: list(w_first)}
        z = {0: z_first}
        for h in range(1, ahead):
            w[h] = tails(h)
            z[h] = logits(kt - 1, h)
        out = []
        for h in range(N_HEADS):
            out.append(consume(kt, h, w.pop(h), laters[h]))
            stage(h, z.pop(h), past)
            if h + ahead < N_HEADS:
                w[h + ahead] = tails(h + ahead)
                z[h + ahead] = logits(kt - 1, h + ahead)
            elif h + ahead == N_HEADS:
                nxt = (tuple(tails(0)), logits(jnp.maximum(kt - 2, 0), 0))
        return tuple(out), nxt[0], nxt[1]

    carry = (tuple(jnp.zeros((1, TQ), F32) for _ in range(N_HEADS)),
             tuple(tails(0)), logits(jnp.maximum(top_kt - 1, 0), 0))
    for j in range(N_TOP - 1):
        carry = step(top_kt - j, carry, past_mask(N_TOP - 2 - j))
    below = qi * N_TOP
    laters, w_first, _ = lax.fori_loop(0, below, lambda i, c: step(below - i, c, None), carry)
    w = {0: list(w_first)}
    for h in range(1, ahead):
        w[h] = tails(h)
    for h in range(N_HEADS):
        consume(0, h, w.pop(h), laters[h])
        if h + ahead < N_HEADS:
            w[h + ahead] = tails(h + ahead)
    o_ref[0] = acc_s[...].reshape(GROUP_W, TQ)


def _stick(p2, batch, seq):
    TQ = STICK_TQ
    nq = seq // TQ
    t = batch * seq
    return pl.pallas_call(
        functools.partial(_stick_kernel, seq=seq),
        grid=(batch, nq),
        in_specs=[pl.BlockSpec((TQ, GROUP_W), lambda b, i: (b * nq + i, COL_QC // GROUP_W)),
                  pl.BlockSpec((seq, GROUP_W), lambda b, i: (b, COL_KC // GROUP_W)),
                  pl.BlockSpec((seq, GROUP_W), lambda b, i: (b, COL_VC // GROUP_W))],
        out_specs=pl.BlockSpec((1, GROUP_W, TQ), lambda b, i: (b * nq + i, 0, 0)),
        out_shape=jax.ShapeDtypeStruct((t // TQ, GROUP_W, TQ), F32),
        scratch_shapes=[pltpu.VMEM((seq, N_HEADS * LANES), BF16),
                        pltpu.VMEM((seq // TK, GROUP_W, TK), BF16),
                        pltpu.VMEM((N_HEADS // 2, TQ, LANES), BF16),
                        pltpu.VMEM((N_HEADS, TK, TQ), BF16),
                        pltpu.VMEM((N_HEADS, TK, TQ), F32),
                        pltpu.VMEM((N_HEADS, HEAD_DIM, TQ), F32)],
        compiler_params=pltpu.CompilerParams(dimension_semantics=("parallel", "arbitrary"),
                                             vmem_limit_bytes=VMEM_LIMIT),
        name="stick_breaking",
    )(p2, p2, p2)


def _mixout_kernel(x_ref, ocmp_ref, oslc_ref, owin_ref, gate_ref, cvb_ref, cvc_ref, cvu_ref, pc_ref, pu_ref,
                   oc_ref, od_ref, bg_ref, cw_ref, gout_ref, wout_ref, o_ref, *, tiles_per_seq):
    i = pl.program_id(0)
    tm, d = x_ref.shape
    chains = [slice(c * MIX_CHAIN, (c + 1) * MIX_CHAIN) for c in range(tm // MIX_CHAIN)]

    def heads(ref, rows):
        tq = ref.shape[2]
        return ref[rows.start // tq, :, rows.start % tq:rows.start % tq + MIX_CHAIN].T

    cu = cvc_ref[...] * cvu_ref[...]
    prev = jnp.where(i % tiles_per_seq == 0, 0.0, pc_ref[...] * pu_ref[...])
    full = jnp.concatenate([prev, cu], axis=0)
    back1 = pltpu.roll(full, 1, 0)[8:, :]
    back2 = pltpu.roll(full, 2, 0)[8:, :]
    cw = cw_ref[...]
    ob = cvb_ref[...] * (cw[0:1, :] * back2 + cw[1:2, :] * back1 + cw[2:3, :] * cu)

    gates = jax.nn.sigmoid(gate_ref[...] + bg_ref[...])
    src = lax.broadcasted_iota(jnp.int32, (LANES, GROUP_W), 0)
    head3 = (lax.broadcasted_iota(jnp.int32, (LANES, GROUP_W), 1) >> 6) * 3
    branch = [(src == head3 + r).astype(BF16) for r in range(3)]
    assert d // HEAD_DIM <= LANES
    gather = ((lax.broadcasted_iota(jnp.int32, (d, LANES), 0) >> 6)
              == lax.broadcasted_iota(jnp.int32, (d, LANES), 1)).astype(BF16)
    spread = (lax.broadcasted_iota(jnp.int32, (LANES, d), 0)
              == (lax.broadcasted_iota(jnp.int32, (LANES, d), 1) >> 6)).astype(BF16)

    groups, ssq = [], []
    for rows in chains:
        oa = (_exact_dot(gates[rows], branch[0]) * heads(ocmp_ref, rows)
              + _exact_dot(gates[rows], branch[1]) * heads(oslc_ref, rows)
              + _exact_dot(gates[rows], branch[2]) * heads(owin_ref, rows))
        g = jnp.concatenate([oa, ob[rows], heads(oc_ref, rows), heads(od_ref, rows)], axis=1)
        groups.append(g)
        ssq.append(_exact_dot(g * g, gather))
    scale = [_exact_dot(lax.rsqrt(s * (1.0 / HEAD_DIM) + RMS_EPS), spread) for s in ssq]
    for rows, g, sc in zip(chains, groups, scale):
        mixed = (g * sc * gout_ref[...]).astype(BF16)
        o_ref[rows, :] = x_ref[rows, :] + jnp.dot(mixed, wout_ref[...], preferred_element_type=F32)


def _mixout(x2, p2, ocmp, oslc, owin, oc, od, bg, cw, gout, wout, seq):
    t, d = x2.shape
    tm = 2 * MIX_CHAIN
    rows = lambda w, col: pl.BlockSpec((tm, w), lambda i: (i, col // w))
    heads_t = lambda tq: pl.BlockSpec((tm // tq, GROUP_W, tq), lambda i: (i, 0, 0))
    prev8 = lambda col: pl.BlockSpec((8, GROUP_W), lambda i: (jnp.maximum(i * (tm // 8) - 1, 0), col // GROUP_W))
    const = lambda shape: pl.BlockSpec(shape, lambda i: (0, 0))
    assert tm % TQ == 0 and tm % STICK_TQ == 0
    return pl.pallas_call(
        functools.partial(_mixout_kernel, tiles_per_seq=seq // tm),
        grid=(t // tm,),
        in_specs=[rows(d, 0), heads_t(TQ), heads_t(TQ), heads_t(TQ),
                  rows(LANES, COL_GATE), rows(GROUP_W, COL_CVB), rows(GROUP_W, COL_CVC), rows(GROUP_W, COL_CVU),
                  prev8(COL_CVC), prev8(COL_CVU),
                  heads_t(STICK_TQ), heads_t(TQ),
                  const((1, LANES)), const((8, GROUP_W)), const((1, d)), const((d, d))],
        out_specs=rows(d, 0),
        out_shape=jax.ShapeDtypeStruct((t, d), F32),
        compiler_params=pltpu.CompilerParams(dimension_semantics=("parallel",), vmem_limit_bytes=VMEM_LIMIT),
        name="mixout",
    )(x2, ocmp, oslc, owin, p2, p2, p2, p2, p2, p2, oc, od, bg, cw, gout, wout)


def _ffn_kernel(x_ref, g_ref, wg_ref, wu_ref, wd_ref, o_ref, act_s, *, ff_chunk, out_chunk):
    h = _rms(x_ref[...], g_ref[...]).astype(BF16)
    dff = wg_ref.shape[1]
    for c in range(dff // ff_chunk):
        cols = slice(c * ff_chunk, (c + 1) * ff_chunk)
        a = jnp.dot(h, wg_ref[:, cols], preferred_element_type=F32)
        u = jnp.dot(h, wu_ref[:, cols], preferred_element_type=F32)
        act_s[:, cols] = (a * jax.nn.sigmoid(a) * u).astype(BF16)
    d = o_ref.shape[1]
    for c in range(d // out_chunk):
        cols = slice(c * out_chunk, (c + 1) * out_chunk)
        o_ref[:, cols] = x_ref[:, cols] + jnp.dot(act_s[...], wd_ref[:, cols], preferred_element_type=F32)


def _ffn(x2, g, wg, wu, wd):
    t, d = x2.shape
    dff = wg.shape[1]
    tm = 1024
    resident = lambda shape: pl.BlockSpec(shape, lambda i: (0, 0), pipeline_mode=pl.Buffered(1))
    return pl.pallas_call(
        functools.partial(_ffn_kernel, ff_chunk=256, out_chunk=256),
        grid=(t // tm,),
        in_specs=[pl.BlockSpec((tm, d), lambda i: (i, 0)),
                  pl.BlockSpec((1, d), lambda i: (0, 0)),
                  resident((d, dff)), resident((d, dff)), resident((dff, d))],
        out_specs=pl.BlockSpec((tm, d), lambda i: (i, 0)),
        out_shape=jax.ShapeDtypeStruct((t, d), F32),
        scratch_shapes=[pltpu.VMEM((tm, dff), BF16)],
        compiler_params=pltpu.CompilerParams(dimension_semantics=("parallel",), vmem_limit_bytes=VMEM_LIMIT),
        name="ffn",
    )(x2, g, wg, wu, wd)


def _permute_w_in_kernel(w_ref, o_ref):
    rows = w_ref.shape[0]
    n_rest = P_COLS - COL_CVB
    o_ref[:, 0:COL_GATE] = w_ref[:, 0:COL_GATE].astype(BF16)
    lane = lax.broadcasted_iota(jnp.int32, (rows, LANES), 1)
    o_ref[:, COL_GATE:COL_GATE + LANES] = jnp.where(lane < N_GATES, w_ref[:, COL_GATE:COL_GATE + LANES], 0.0).astype(BF16)
    o_ref[:, COL_CVB:P_COLS] = w_ref[:, COL_GATE + N_GATES:COL_GATE + N_GATES + n_rest].astype(BF16)


def _permute_w_in(w_in):
    d, cols = w_in.shape
    tr = 256
    return pl.pallas_call(
        _permute_w_in_kernel,
        grid=(d // tr,),
        in_specs=[pl.BlockSpec((tr, cols), lambda i: (i, 0))],
        out_specs=pl.BlockSpec((tr, P_COLS), lambda i: (i, 0)),
        out_shape=jax.ShapeDtypeStruct((d, P_COLS), BF16),
        compiler_params=pltpu.CompilerParams(dimension_semantics=("parallel",), vmem_limit_bytes=VMEM_LIMIT),
        name="permute_w_in",
    )(w_in)


def _layer(x2, batch, seq, g_mix, w_in, b_gate, g_q_nsa, g_k_cmp, g_k_slc, g_k_win, pe_k_cmp, pe_v_cmp,
           w1_k_cmp, w2_k_cmp, w1_v_cmp, w2_v_cmp, conv_w, g_q_dil, g_k_dil, g_out, w_out,
           g_ffn, w_gate, w_up, w_down):
    row = lambda v: v.reshape(1, -1)
    per_group = lambda g, n: jnp.tile(g, n).reshape(1, -1)
    p2 = _inproj(x2, row(g_mix), _permute_w_in(w_in))

    pe, w1, w2 = _compress_weights(pe_k_cmp, pe_v_cmp, w1_k_cmp, w2_k_cmp, w1_v_cmp, w2_v_cmp)
    kc, vc = _compress(p2, pe, w1, w2, per_group(g_k_cmp, 2), batch, seq)
    ocmp, oslc, owin = _nsa(p2, kc, vc, per_group(g_q_nsa, N_HEADS), per_group(g_k_slc, 2), per_group(g_k_win, 2),
                            batch, seq)
    od = _banded(p2, per_group(g_q_dil, N_HEADS), per_group(g_k_dil, N_HEADS),
                 jnp.asarray(_band_bias_tiles(DILATED_CONFIGS[-1][0], _dilated_mult)),
                 batch, seq, window=DILATED_CONFIGS[-1][0], slopes=SLOPES_D, shared_kv=False,
                 col_q=COL_QD, col_k=COL_KD, col_v=COL_VD, name="dilated")
    oc = _stick(p2, batch, seq)

    bg = jnp.zeros((1, LANES), F32).at[0, :N_GATES].set(b_gate)
    cwp = jnp.zeros((8, GROUP_W), F32).at[:CONV_K].set(conv_w)
    x1 = _mixout(x2, p2, ocmp, oslc, owin, oc, od, bg, cwp, row(g_out), w_out.astype(BF16), seq)
    return _ffn(x1, row(g_ffn), w_gate.astype(BF16), w_up.astype(BF16), w_down.astype(BF16))


def kernel(x, g_mix, w_in, b_gate, g_q_nsa, g_k_cmp, g_k_slc, g_k_win, pe_k_cmp, pe_v_cmp, w1_k_cmp, w2_k_cmp,
           w1_v_cmp, w2_v_cmp, conv_w, g_q_dil, g_k_dil, g_out, w_out, g_ffn, w_gate, w_up, w_down):
    batch, seq, d = x.shape
    assert seq % (CMP_STRIDE * LANES) == 0 and d % LANES == 0
    x2 = x.reshape(batch * seq, d)
    params = (g_mix, w_in, b_gate, g_q_nsa, g_k_cmp, g_k_slc, g_k_win, pe_k_cmp, pe_v_cmp, w1_k_cmp, w2_k_cmp,
              w1_v_cmp, w2_v_cmp, conv_w, g_q_dil, g_k_dil, g_out, w_out, g_ffn, w_gate, w_up, w_down)
    for layer in range(g_mix.shape[0]):
        x2 = _layer(x2, batch, seq, *[p[layer] for p in params])
    return x2.reshape(batch, seq, d)
```

```python
import functools

import numpy as np
import jax
import jax.numpy as jnp
from jax import lax
from jax.experimental import pallas as pl
from jax.experimental.pallas import tpu as pltpu

F32 = jnp.float32
BF16 = jnp.bfloat16

HEAD_DIM = 64
N_HEADS = 4
GROUP_W = N_HEADS * HEAD_DIM
CONV_K = 3
CMP_LEN = 32
CMP_STRIDE = 16
SEL_BLOCK = 64
N_SELECT = 16
NSA_WINDOW = 512
DILATED_CONFIGS = ((128, 1), (512, 4), (2048, 16))
DILATED_WINDOW = max(w for w, _ in DILATED_CONFIGS)
NEG_INF = -1e30
FORCE_SCORE = 1e6
RMS_EPS = 1e-6

TQ = 512
TK = 256
N_TOP = TQ // TK
STICK_TQ = 256
LANES = 128
MXU_DEPTH = 256
MIX_CHAIN = 256
MXU_AHEAD = 2
FLASH_AHEAD = 2
SUM_PAD = 16
V_ROWS = HEAD_DIM + SUM_PAD
LOG2E = 1.4426950408889634
VMEM_LIMIT = 52 * 1024 * 1024

COL_QA, COL_KVC, COL_KSV, COL_KWV, COL_GATE = 0, 256, 384, 512, 640
COL_CVB, COL_CVC, COL_CVU = 768, 1024, 1280
COL_QC, COL_KC, COL_VC = 1536, 1792, 2048
COL_QD, COL_KD, COL_VD = 2304, 2560, 2816
P_COLS = 3072
N_GATES = 12

_NT = (((1,), (1,)), ((), ()))


def _alibi_slopes():
    s = [2.0 ** (-8.0 * i / 8) for i in range(1, 9)]
    return tuple(s[0::2]), tuple(s[1::2])


SLOPES_A, SLOPES_D = _alibi_slopes()


def _rms(x, g):
    return x * lax.rsqrt(jnp.mean(x * x, axis=-1, keepdims=True) + RMS_EPS) * g


def _exact_dot(a, sel):
    hi = a.astype(BF16)
    lo = (a - hi.astype(F32)).astype(BF16)
    if 2 * a.shape[1] <= MXU_DEPTH:
        return jnp.dot(jnp.concatenate([hi, lo], axis=1), jnp.concatenate([sel, sel], axis=0),
                       preferred_element_type=F32)
    return jnp.dot(hi, sel, preferred_element_type=F32) + jnp.dot(lo, sel, preferred_element_type=F32)


def _group_rms(x, g):
    w = x.shape[1]
    same = ((lax.broadcasted_iota(jnp.int32, (w, w), 0) >> 6)
            == (lax.broadcasted_iota(jnp.int32, (w, w), 1) >> 6)).astype(BF16)
    ssq = _exact_dot(x * x, same)
    return x * lax.rsqrt(ssq * (1.0 / HEAD_DIM) + RMS_EPS) * g


EVEN_AUG, ODD_AUG = HEAD_DIM, 0


def _key_aug(pos, first):
    lane = lax.broadcasted_iota(jnp.int32, (pos.shape[0], LANES), 1) - first
    hi = (pos >> 6).astype(F32)
    lo = (pos & 63).astype(F32)
    return jnp.where(lane == 0, hi, jnp.where(lane == 1, lo, jnp.where((lane == 2) | (lane == 3), 1.0, 0.0)))


def _query_aug(t, slope, first):
    lane = lax.broadcasted_iota(jnp.int32, (t.shape[0], LANES), 1) - first
    hi = (t >> 6).astype(F32) * (-64.0 * slope)
    lo = (t & 63).astype(F32) * (-slope)
    return jnp.where(lane == 0, 64.0 * slope,
                     jnp.where(lane == 1, slope, jnp.where(lane == 2, hi, jnp.where(lane == 3, lo, 0.0))))


def _pair_keys(pair, pos):
    lane = lax.broadcasted_iota(jnp.int32, pair.shape, 1)
    aug_e = 0.0 if pos is None else _key_aug(pos, EVEN_AUG)
    aug_o = 0.0 if pos is None else _key_aug(pos, ODD_AUG)
    return (jnp.where(lane < HEAD_DIM, pair, aug_e).astype(BF16),
            jnp.where(lane >= HEAD_DIM, pair, aug_o).astype(BF16))


def _prep_queries(q, g, t0, slopes, qp_s):
    qn = (q if g is None else _group_rms(q, g)) * (HEAD_DIM ** -0.5)
    tpos = t0 + lax.broadcasted_iota(jnp.int32, (q.shape[0], 1), 0)
    lane = lax.broadcasted_iota(jnp.int32, (q.shape[0], LANES), 1)
    for i in range(N_HEADS // 2):
        pair = qn[:, i * LANES:(i + 1) * LANES]
        if slopes is None:
            qp_s[i] = pair.astype(BF16)
            continue
        even = jnp.where((lane >= EVEN_AUG) & (lane < EVEN_AUG + 4), _query_aug(tpos, slopes[2 * i], EVEN_AUG), pair)
        odd = jnp.where(lane < ODD_AUG + 4, _query_aug(tpos, slopes[2 * i + 1], ODD_AUG), pair)
        qp_s[2 * i] = even.astype(BF16)
        qp_s[2 * i + 1] = odd.astype(BF16)


def _inproj_kernel(x_ref, g_ref, w_ref, o_ref, *, nchunk):
    h = _rms(x_ref[...], g_ref[...]).astype(BF16)
    cw = P_COLS // nchunk
    for c in range(nchunk):
        o_ref[:, c * cw:(c + 1) * cw] = jnp.dot(h, w_ref[:, c * cw:(c + 1) * cw], preferred_element_type=F32)


def _inproj(x2, g, w):
    t, d = x2.shape
    tm = 1024
    return pl.pallas_call(
        functools.partial(_inproj_kernel, nchunk=6),
        grid=(t // tm,),
        in_specs=[pl.BlockSpec((tm, d), lambda i: (i, 0)),
                  pl.BlockSpec((1, d), lambda i: (0, 0)),
                  pl.BlockSpec((d, P_COLS), lambda i: (0, 0), pipeline_mode=pl.Buffered(1))],
        out_specs=pl.BlockSpec((tm, P_COLS), lambda i: (i, 0)),
        out_shape=jax.ShapeDtypeStruct((t, P_COLS), F32),
        compiler_params=pltpu.CompilerParams(dimension_semantics=("parallel",), vmem_limit_bytes=VMEM_LIMIT),
        name="inproj",
    )(x2, g, w)


def _gelu_tanh(x):
    return x * (0.5 * (1.0 + jnp.tanh(np.sqrt(2.0 / np.pi).astype(np.float32) * (x + 0.044715 * (x * x * x)))))


def _compress_kernel(kv_ref, pe_ref, w1_ref, w2_ref, gk_ref, kc_ref, vct_ref):
    nc = kv_ref.shape[0] // CMP_STRIDE
    first = None
    second = None
    for j in range(CMP_STRIDE):
        tok = kv_ref[pl.ds(j, nc, stride=CMP_STRIDE), :]
        a = jnp.dot((tok + pe_ref[j:j + 1, :]).astype(BF16), w1_ref[j], preferred_element_type=F32)
        b = jnp.dot((tok + pe_ref[CMP_STRIDE + j:CMP_STRIDE + j + 1, :]).astype(BF16), w1_ref[CMP_STRIDE + j],
                    preferred_element_type=F32)
        first = a if first is None else first + a
        second = b if second is None else second + b
    hid = first + pltpu.roll(second, nc - 1, 0)
    out = jnp.dot(_gelu_tanh(hid).astype(BF16), w2_ref[...], preferred_element_type=F32)
    kn = _group_rms(out, gk_ref[...])
    end = lax.broadcasted_iota(jnp.int32, (nc, 1), 0) * CMP_STRIDE + (CMP_LEN - 1)
    even, _ = _pair_keys(kn, end)
    _, odd = _pair_keys(pltpu.roll(kn, HEAD_DIM, 1), end)
    kc_ref[0] = jnp.concatenate([even, odd], axis=1)
    vct_ref[0] = out.T[HEAD_DIM:, :].astype(BF16)


def _compress(p2, pe, w1, w2, gk, batch, seq):
    nc = seq // CMP_STRIDE
    hid2 = w1.shape[2]
    return pl.pallas_call(
        _compress_kernel,
        grid=(batch,),
        in_specs=[pl.BlockSpec((seq, LANES), lambda i: (i, COL_KVC // LANES)),
                  pl.BlockSpec((CMP_LEN, LANES), lambda i: (0, 0)),
                  pl.BlockSpec((CMP_LEN, LANES, hid2), lambda i: (0, 0, 0)),
                  pl.BlockSpec((hid2, LANES), lambda i: (0, 0)),
                  pl.BlockSpec((1, LANES), lambda i: (0, 0))],
        out_specs=[pl.BlockSpec((1, nc, 2 * LANES), lambda i: (i, 0, 0)),
                   pl.BlockSpec((1, HEAD_DIM, nc), lambda i: (i, 0, 0))],
        out_shape=[jax.ShapeDtypeStruct((batch, nc, 2 * LANES), BF16),
                   jax.ShapeDtypeStruct((batch, HEAD_DIM, nc), BF16)],
        compiler_params=pltpu.CompilerParams(dimension_semantics=("parallel",), vmem_limit_bytes=VMEM_LIMIT),
        name="nsa_compress",
    )(p2, pe, w1, w2, gk)


def _compress_weights(pe_k, pe_v, w1_k, w2_k, w1_v, w2_v):
    hid = w1_k.shape[1]
    w1k = w1_k.reshape(CMP_LEN, HEAD_DIM, hid)
    w1v = w1_v.reshape(CMP_LEN, HEAD_DIM, hid)
    z1 = jnp.zeros_like(w1k)
    w1 = jnp.concatenate([jnp.concatenate([w1k, z1], axis=2), jnp.concatenate([z1, w1v], axis=2)], axis=1)
    z2 = jnp.zeros_like(w2_k)
    w2 = jnp.concatenate([jnp.concatenate([w2_k, z2], axis=1), jnp.concatenate([z2, w2_v], axis=1)], axis=0)
    pe = jnp.concatenate([pe_k, pe_v], axis=1)
    return pe, w1.astype(BF16), w2.astype(BF16)


def _flash_scores(kp_of, bias, qp_s, s_s, m_cur):
    m_next = []
    for h in range(N_HEADS):
        s = lax.dot_general(kp_of(h), qp_s[h], _NT, preferred_element_type=F32) + bias
        s_s[h] = s
        m_next.append(jnp.maximum(m_cur[h], jnp.max(s, axis=0, keepdims=True)))
    return tuple(m_next)


def _flash_accum(vt_of, s_s, acc_s, m_prev, m_cur):
    for h in range(N_HEADS):
        alpha = jnp.exp(m_prev[h] - m_cur[h])
        p = jnp.exp(s_s[h] - m_cur[h])
        acc_s[h] = alpha * acc_s[h] + jnp.dot(vt_of(h), p.astype(BF16), preferred_element_type=F32)


def _flash_accum_and_scores(kp_of, bias, vt_of, qp_s, s_s, acc_s, m_prev, m_cur):
    def scores(h):
        return lax.dot_general(kp_of(h), qp_s[h], _NT, preferred_element_type=F32) + bias

    m_next = []
    s_new = {h: scores(h) for h in range(FLASH_AHEAD)}
    for h in range(N_HEADS):
        alpha = jnp.exp(m_prev[h] - m_cur[h])
        p = jnp.exp(s_s[h] - m_cur[h])
        acc_s[h] = alpha * acc_s[h] + jnp.dot(vt_of(h), p.astype(BF16), preferred_element_type=F32)
        s_h = s_new.pop(h)
        s_s[h] = s_h
        m_next.append(jnp.maximum(m_cur[h], jnp.max(s_h, axis=0, keepdims=True)))
        if h + FLASH_AHEAD < N_HEADS:
            s_new[h + FLASH_AHEAD] = scores(h + FLASH_AHEAD)
    return tuple(m_next)


def _flash_pipelined(tile_of, n_tiles, kp_fn, bias_fn, vt_fn, qp_s, s_s, acc_s, o_ref):
    acc_s[...] = jnp.zeros(acc_s.shape, F32)
    m0 = tuple(jnp.full((1, TQ), NEG_INF, F32) for _ in range(N_HEADS))
    first_kt = tile_of(0)
    m1 = _flash_scores(kp_fn(first_kt), bias_fn(first_kt, True), qp_s, s_s, m0)

    def step(i, carry, top):
        m_prev, m_cur = carry
        nxt = tile_of(i + 1)
        m_next = _flash_accum_and_scores(kp_fn(nxt), bias_fn(nxt, top), vt_fn(tile_of(i)), qp_s, s_s, acc_s,
                                         m_prev, m_cur)
        return m_cur, m_next

    carry = (m0, m1)
    for i in range(N_TOP - 1):
        carry = step(i, carry, True)
    m_prev, m_cur = lax.fori_loop(N_TOP - 1, n_tiles - 1, lambda i, c: step(i, c, False), carry)
    _flash_accum(vt_fn(tile_of(n_tiles - 1)), s_s, acc_s, m_prev, m_cur)
    o_ref[0] = jnp.concatenate([acc_s[h, 0:HEAD_DIM, :] * (1.0 / acc_s[h, HEAD_DIM:HEAD_DIM + 1, :])
                                for h in range(N_HEADS)], axis=0)


def _values_t(v_t):
    row = lax.broadcasted_iota(jnp.int32, (V_ROWS - HEAD_DIM, v_t.shape[1]), 0)
    return jnp.concatenate([v_t, jnp.where(row == 0, 1.0, 0.0)], axis=0).astype(BF16)


def _nsa_kernel(q_ref, kc_ref, vct_ref, ksv_ref, kwv_ref, gq_ref, gks_ref, gkw_ref, wbias_ref,
                ocmp_ref, oslc_ref, owin_ref,
                ksp_s, vst_s, kwp_s, vwt_s, qp_s, imp_s, sel_s, s_s, acc_s, hit_s, *, seq):
    qi = pl.program_id(1)
    nc = seq // CMP_STRIDE
    nsel = seq // SEL_BLOCK
    nkt = seq // TK

    @pl.when(qi == 0)
    def _prep():
        for c in range(nkt):
            rows = slice(c * TK, (c + 1) * TK)
            pos = c * TK + lax.broadcasted_iota(jnp.int32, (TK, 1), 0)
            for kv_ref, g_ref, kp_s, vt_s in ((ksv_ref, gks_ref, ksp_s, vst_s), (kwv_ref, gkw_ref, kwp_s, vwt_s)):
                blk = kv_ref[rows, :]
                kn = _group_rms(blk, g_ref[...])
                kp_s[rows, 0:LANES], _ = _pair_keys(kn, pos)
                _, kp_s[rows, LANES:2 * LANES] = _pair_keys(pltpu.roll(kn, HEAD_DIM, 1), pos)
                vt_s[c] = _values_t(blk.T[HEAD_DIM:, :])
        imp_s[:, 0:8, :] = jnp.zeros((TQ // LANES, 8, LANES), F32)

    t0 = qi * TQ
    _prep_queries(q_ref[...], gq_ref[...], t0, SLOPES_A, qp_s)

    kc = [kc_ref[0, :, 0:LANES], kc_ref[0, :, LANES:2 * LANES]]
    vct = vct_ref[0]
    tq_row = t0 + lax.broadcasted_iota(jnp.int32, (nc, TQ), 1)
    n_idx = lax.broadcasted_iota(jnp.int32, (nc, TQ), 0)
    vis = (tq_row >= n_idx * CMP_STRIDE + (CMP_LEN - 1)) & (n_idx < nc - 1)
    vis_bias = jnp.where(vis, 0.0, NEG_INF)
    sees_any = (t0 + lax.broadcasted_iota(jnp.int32, (1, TQ), 1) >= CMP_LEN - 1).astype(F32)
    imp = jnp.zeros((nc, TQ), F32)
    ocmp_t = []
    scores = [lax.dot_general(kc[h % 2], qp_s[h], _NT, preferred_element_type=F32) + vis_bias
              for h in range(N_HEADS)]
    for h in range(N_HEADS):
        sc = scores[h]
        e = jnp.exp(sc - jnp.max(sc, axis=0, keepdims=True))
        p = e * (sees_any / jnp.sum(e, axis=0, keepdims=True))
        ocmp_t.append(jnp.dot(vct, p.astype(BF16), preferred_element_type=F32))
        imp = imp + p
    ocmp_ref[0] = jnp.concatenate(ocmp_t, axis=0)

    halves = []
    for half in range(TQ // LANES):
        imp_s[half, 8:8 + nc, :] = imp[:, half * LANES:(half + 1) * LANES]
        r = [imp_s[half, pl.ds(8 + k, nsel, stride=4), :] for k in range(4)]
        rm1 = imp_s[half, pl.ds(7, nsel, stride=4), :]
        halves.append(rm1 + 2.0 * (r[0] + r[1] + r[2]) + r[3])
    imp_blk = jnp.concatenate(halves, axis=1)
    blk = lax.broadcasted_iota(jnp.int32, (nsel, TQ), 0)
    tl = t0 + lax.broadcasted_iota(jnp.int32, (nsel, TQ), 1)
    cur = tl >> 6
    forced = (blk == 0) | (blk == cur) | (blk == cur - 1)
    valid = blk * SEL_BLOCK <= tl
    score = jnp.where(forced, FORCE_SCORE, jnp.where(valid, imp_blk, -FORCE_SCORE))
    sub = 8
    per_tile = TK // SEL_BLOCK
    tile_hit = []
    for g in range(nsel // sub):
        mine = score[g * sub:(g + 1) * sub, :]
        blk_g = g * sub + lax.broadcasted_iota(jnp.int32, (sub, TQ), 0)
        rank = jnp.zeros((sub, TQ), F32)
        for j in range(nsel):
            row = score[j:j + 1, :]
            if j < g * sub:
                beats = row >= mine
            elif j >= (g + 1) * sub:
                beats = row > mine
            else:
                beats = (row > mine) | ((row == mine) & (blk_g > j))
            rank = rank + jnp.where(beats, 1.0, 0.0)
        sel_g = jnp.where(rank < min(N_SELECT, nsel), 0.0, NEG_INF)
        sel_s[g * sub:(g + 1) * sub, :] = sel_g
        for part in range(sub // per_tile):
            tile_hit.append(jnp.max(sel_g[part * per_tile:(part + 1) * per_tile, :]))

    n_hit = jnp.int32(0)
    for kt in reversed(range(nkt)):
        hit_s[n_hit] = kt
        n_hit = n_hit + ((tile_hit[kt] > -1.0) & (kt * TK < t0)).astype(jnp.int32)

    def block_bias(kt):
        return jnp.concatenate([jnp.broadcast_to(sel_s[pl.ds(kt * per_tile + j, 1), :], (SEL_BLOCK, TQ))
                                for j in range(per_tile)], axis=0)

    k_local = lax.broadcasted_iota(jnp.int32, (TK, TQ), 0)
    q_local = lax.broadcasted_iota(jnp.int32, (TK, TQ), 1)
    top_kt = qi * N_TOP + (N_TOP - 1)

    def kp_fn(kt):
        off = pl.multiple_of(kt * TK, TK)
        return lambda h: ksp_s[pl.ds(off, TK), (h % 2) * LANES:(h % 2 + 1) * LANES]

    def bias_fn(kt, top):
        if top:
            return jnp.where(kt * TK - t0 + k_local <= q_local, block_bias(kt), NEG_INF)
        return block_bias(kt)

    tile_of = lambda i: jnp.where(i < N_TOP, top_kt - i, hit_s[jnp.maximum(i - N_TOP, 0)])
    _flash_pipelined(tile_of, n_hit + N_TOP, kp_fn, bias_fn, lambda kt: (lambda h: vst_s[kt]), qp_s, s_s, acc_s,
                     oslc_ref)

    def window_kp_fn(kt):
        off = pl.multiple_of(kt * TK, TK)
        return lambda h: kwp_s[pl.ds(off, TK), (h % 2) * LANES:(h % 2 + 1) * LANES]

    kt_lo = jnp.maximum(t0 - (NSA_WINDOW - 1), 0) // TK
    _flash_pipelined(lambda i: top_kt - i, top_kt + 1 - kt_lo, window_kp_fn, lambda kt, top: wbias_ref[top_kt - kt],
                     lambda kt: (lambda h: vwt_s[kt]), qp_s, s_s, acc_s, owin_ref)


def _nsa(p2, kc, vc, gq, gks, gkw, batch, seq):
    nq = seq // TQ
    nc = seq // CMP_STRIDE
    t = batch * seq
    wbias = jnp.asarray(_band_bias_tiles(NSA_WINDOW - 1, _window_mult))
    head_tiles = pl.BlockSpec((1, GROUP_W, TQ), lambda b, i: (b * nq + i, 0, 0))
    head_shape = jax.ShapeDtypeStruct((t // TQ, GROUP_W, TQ), F32)
    return pl.pallas_call(
        functools.partial(_nsa_kernel, seq=seq),
        grid=(batch, nq),
        in_specs=[pl.BlockSpec((TQ, GROUP_W), lambda b, i: (b * nq + i, COL_QA // GROUP_W)),
                  pl.BlockSpec((1, nc, 2 * LANES), lambda b, i: (b, 0, 0)),
                  pl.BlockSpec((1, HEAD_DIM, nc), lambda b, i: (b, 0, 0)),
                  pl.BlockSpec((seq, LANES), lambda b, i: (b, COL_KSV // LANES)),
                  pl.BlockSpec((seq, LANES), lambda b, i: (b, COL_KWV // LANES)),
                  pl.BlockSpec((1, GROUP_W), lambda b, i: (0, 0)),
                  pl.BlockSpec((1, LANES), lambda b, i: (0, 0)),
                  pl.BlockSpec((1, LANES), lambda b, i: (0, 0)),
                  pl.BlockSpec(wbias.shape, lambda b, i: (0, 0, 0))],
        out_specs=[head_tiles, head_tiles, head_tiles],
        out_shape=[head_shape, head_shape, head_shape],
        scratch_shapes=[pltpu.VMEM((seq, 2 * LANES), BF16),
                        pltpu.VMEM((seq // TK, V_ROWS, TK), BF16),
                        pltpu.VMEM((seq, 2 * LANES), BF16),
                        pltpu.VMEM((seq // TK, V_ROWS, TK), BF16),
                        pltpu.VMEM((N_HEADS, TQ, LANES), BF16),
                        pltpu.VMEM((TQ // LANES, 8 + nc, LANES), F32),
                        pltpu.VMEM((seq // SEL_BLOCK, TQ), F32),
                        pltpu.VMEM((N_HEADS, TK, TQ), F32),
                        pltpu.VMEM((N_HEADS, V_ROWS, TQ), F32),
                        pltpu.SMEM((seq // TK + 1,), jnp.int32)],
        compiler_params=pltpu.CompilerParams(dimension_semantics=("parallel", "arbitrary"),
                                             vmem_limit_bytes=VMEM_LIMIT),
        name="nsa",
    )(p2, kc, vc, p2, p2, gq, gks, gkw, wbias)


def _band_bias_tiles(window, mult_fn):
    nd = (window + TQ - 1) // TK + 1
    kk = np.arange(TK)[:, None]
    qq = np.arange(TQ)[None, :]
    tiles = np.empty((nd, TK, TQ), np.float32)
    for di in range(nd):
        d = (di - (N_TOP - 1)) * TK + qq - kk
        mult = mult_fn(d)
        tiles[di] = np.where(mult > 0, np.log(np.maximum(mult, 1)), NEG_INF)
    return tiles


def _window_mult(d):
    return ((d >= 0) & (d <= NSA_WINDOW - 1)).astype(np.float64)


def _dilated_mult(d):
    m = np.zeros(d.shape, np.float64)
    for window, dil in DILATED_CONFIGS:
        m += ((d >= 0) & (d <= window) & (d % dil == 0)).astype(np.float64)
    return m


def _dilated_kernel(q_ref, k_ref, v_ref, gq_ref, gk_ref, bias_ref, o_ref, kp_s, vt_s, qp_s, s_s, acc_s, *, seq):
    qi = pl.program_id(1)

    @pl.when(qi == 0)
    def _prep():
        for c in range(seq // TK):
            rows = slice(c * TK, (c + 1) * TK)
            pos = c * TK + lax.broadcasted_iota(jnp.int32, (TK, 1), 0)
            kn = _group_rms(k_ref[rows, :], gk_ref[...])
            for i in range(N_HEADS // 2):
                even, odd = _pair_keys(kn[:, i * LANES:(i + 1) * LANES], pos)
                kp_s[rows, 2 * i * LANES:(2 * i + 1) * LANES] = even
                kp_s[rows, (2 * i + 1) * LANES:(2 * i + 2) * LANES] = odd
            v_t = v_ref[rows, :].T
            for h in range(N_HEADS):
                vt_s[c, h * V_ROWS:(h + 1) * V_ROWS, :] = _values_t(v_t[h * HEAD_DIM:(h + 1) * HEAD_DIM, :])

    t0 = qi * TQ
    _prep_queries(q_ref[...], gq_ref[...], t0, SLOPES_D, qp_s)
    kt_lo = jnp.maximum(t0 - DILATED_WINDOW, 0) // TK
    top_kt = qi * N_TOP + (N_TOP - 1)

    def kp_fn(kt):
        off = pl.multiple_of(kt * TK, TK)
        return lambda h: kp_s[pl.ds(off, TK), h * LANES:(h + 1) * LANES]

    _flash_pipelined(lambda i: top_kt - i, top_kt + 1 - kt_lo, kp_fn, lambda kt, top: bias_ref[top_kt - kt],
                     lambda kt: (lambda h: vt_s[kt, h * V_ROWS:(h + 1) * V_ROWS, :]), qp_s, s_s, acc_s, o_ref)


def _dilated(p2, gq, gk, batch, seq):
    nq = seq // TQ
    t = batch * seq
    bias_tiles = jnp.asarray(_band_bias_tiles(DILATED_WINDOW, _dilated_mult))
    cols = lambda rows, col: pl.BlockSpec((rows, GROUP_W), lambda b, i: (b if rows == seq else b * nq + i,
                                                                         col // GROUP_W))
    gain = pl.BlockSpec((1, GROUP_W), lambda b, i: (0, 0))
    return pl.pallas_call(
        functools.partial(_dilated_kernel, seq=seq),
        grid=(batch, nq),
        in_specs=[cols(TQ, COL_QD), cols(seq, COL_KD), cols(seq, COL_VD), gain, gain,
                  pl.BlockSpec(bias_tiles.shape, lambda b, i: (0, 0, 0))],
        out_specs=pl.BlockSpec((1, GROUP_W, TQ), lambda b, i: (b * nq + i, 0, 0)),
        out_shape=jax.ShapeDtypeStruct((t // TQ, GROUP_W, TQ), F32),
        scratch_shapes=[pltpu.VMEM((seq, N_HEADS * LANES), BF16),
                        pltpu.VMEM((seq // TK, N_HEADS * V_ROWS, TK), BF16),
                        pltpu.VMEM((N_HEADS, TQ, LANES), BF16),
                        pltpu.VMEM((N_HEADS, TK, TQ), F32),
                        pltpu.VMEM((N_HEADS, V_ROWS, TQ), F32)],
        compiler_params=pltpu.CompilerParams(dimension_semantics=("parallel", "arbitrary"),
                                             vmem_limit_bytes=VMEM_LIMIT),
        name="dilated",
    )(p2, p2, p2, gq, gk, bias_tiles)


def _stick_kernel(q_ref, k_ref, v_ref, o_ref, kp_s, vt_s, qp_s, sp_s, e_s, acc_s, *, seq):
    TQ, N_TOP = STICK_TQ, STICK_TQ // TK
    qi = pl.program_id(1)
    nkt = seq // TK

    @pl.when(qi == 0)
    def _prep():
        for c in range(nkt):
            rows = slice(c * TK, (c + 1) * TK)
            kb = k_ref[rows, :]
            for i in range(N_HEADS // 2):
                even, odd = _pair_keys(kb[:, i * LANES:(i + 1) * LANES], None)
                kp_s[rows, 2 * i * LANES:(2 * i + 1) * LANES] = even
                kp_s[rows, (2 * i + 1) * LANES:(2 * i + 2) * LANES] = odd
            vt_s[c] = v_ref[rows, :].T.astype(BF16)

    _prep_queries(q_ref[...], None, qi * TQ, None, qp_s)
    half = TK // 2
    col = lax.broadcasted_iota(jnp.int32, (half + SUM_PAD, half), 1)
    srow = lax.broadcasted_iota(jnp.int32, (half + SUM_PAD, half), 0)
    sums = ((srow == half) | ((srow < half) & (col > srow))).astype(BF16)
    k_local = lax.broadcasted_iota(jnp.int32, (TK, TQ), 0)
    q_local = lax.broadcasted_iota(jnp.int32, (TK, TQ), 1)
    top_kt = qi * N_TOP + (N_TOP - 1)
    acc_s[...] = jnp.zeros(acc_s.shape, F32)

    def past_mask(tiles_above):
        return tiles_above * TK + k_local < q_local

    def logits(kt, h):
        off = pl.multiple_of(kt * TK, TK)
        return lax.dot_general(kp_s[pl.ds(off, TK), h * LANES:(h + 1) * LANES], qp_s[h // 2], _NT,
                               preferred_element_type=F32)

    def stage(h, z, past):
        sp = jnp.maximum(z, 0.0) + jnp.log(1.0 + jnp.exp2(jnp.abs(z) * (-LOG2E)))
        spm = sp if past is None else jnp.where(past, sp, 0.0)
        sp_s[h] = spm.astype(BF16)
        logsig = z - sp
        e_s[h] = logsig if past is None else jnp.where(past, logsig, NEG_INF)

    def tails(h):
        return [jnp.dot(sums, sp_s[h, b * half:(b + 1) * half, :], preferred_element_type=F32) for b in range(2)]

    def consume(kt, h, w, later):
        later_lo = later + w[1][half:half + 1, :]
        attn = jnp.concatenate([jnp.exp(e_s[h, 0:half, :] - w[0][0:half, :] - later_lo),
                                jnp.exp(e_s[h, half:TK, :] - w[1][0:half, :] - later)], axis=0)
        acc_s[h] += jnp.dot(vt_s[kt, h * HEAD_DIM:(h + 1) * HEAD_DIM, :], attn.astype(BF16),
                            preferred_element_type=F32)
        return later_lo + w[0][half:half + 1, :]

    ahead = MXU_AHEAD
    for h in range(N_HEADS):
        stage(h, logits(top_kt, h), past_mask(N_TOP - 1))

    def step(kt, carry, past):
        laters, w_first, z_first = carry
        w = {0: list(w_first)}
        z = {0: z_first}
        for h in range(1, ahead):
            w[h] = tails(h)
            z[h] = logits(kt - 1, h)
        out = []
        for h in range(N_HEADS):
            out.append(consume(kt, h, w.pop(h), laters[h]))
            stage(h, z.pop(h), past)
            if h + ahead < N_HEADS:
                w[h + ahead] = tails(h + ahead)
                z[h + ahead] = logits(kt - 1, h + ahead)
            elif h + ahead == N_HEADS:
                nxt = (tuple(tails(0)), logits(jnp.maximum(kt - 2, 0), 0))
        return tuple(out), nxt[0], nxt[1]

    carry = (tuple(jnp.zeros((1, TQ), F32) for _ in range(N_HEADS)),
             tuple(tails(0)), logits(jnp.maximum(top_kt - 1, 0), 0))
    for j in range(N_TOP - 1):
        carry = step(top_kt - j, carry, past_mask(N_TOP - 2 - j))
    below = qi * N_TOP
    laters, w_first, _ = lax.fori_loop(0, below, lambda i, c: step(below - i, c, None), carry)
    w = {0: list(w_first)}
    for h in range(1, ahead):
        w[h] = tails(h)
    for h in range(N_HEADS):
        consume(0, h, w.pop(h), laters[h])
        if h + ahead < N_HEADS:
            w[h + ahead] = tails(h + ahead)
    o_ref[0] = acc_s[...].reshape(GROUP_W, TQ)


def _stick(p2, batch, seq):
    TQ = STICK_TQ
    nq = seq // TQ
    t = batch * seq
    return pl.pallas_call(
        functools.partial(_stick_kernel, seq=seq),
        grid=(batch, nq),
        in_specs=[pl.BlockSpec((TQ, GROUP_W), lambda b, i: (b * nq + i, COL_QC // GROUP_W)),
                  pl.BlockSpec((seq, GROUP_W), lambda b, i: (b, COL_KC // GROUP_W)),
                  pl.BlockSpec((seq, GROUP_W), lambda b, i: (b, COL_VC // GROUP_W))],
        out_specs=pl.BlockSpec((1, GROUP_W, TQ), lambda b, i: (b * nq + i, 0, 0)),
        out_shape=jax.ShapeDtypeStruct((t // TQ, GROUP_W, TQ), F32),
        scratch_shapes=[pltpu.VMEM((seq, N_HEADS * LANES), BF16),
                        pltpu.VMEM((seq // TK, GROUP_W, TK), BF16),
                        pltpu.VMEM((N_HEADS // 2, TQ, LANES), BF16),
                        pltpu.VMEM((N_HEADS, TK, TQ), BF16),
                        pltpu.VMEM((N_HEADS, TK, TQ), F32),
                        pltpu.VMEM((N_HEADS, HEAD_DIM, TQ), F32)],
        compiler_params=pltpu.CompilerParams(dimension_semantics=("parallel", "arbitrary"),
                                             vmem_limit_bytes=VMEM_LIMIT),
        name="stick_breaking",
    )(p2, p2, p2)


def _mixout_kernel(x_ref, ocmp_ref, oslc_ref, owin_ref, gate_ref, cvb_ref, cvc_ref, cvu_ref, pc_ref, pu_ref,
                   oc_ref, od_ref, bg_ref, cw_ref, gout_ref, wout_ref, o_ref, *, tiles_per_seq):
    i = pl.program_id(0)
    tm, d = x_ref.shape
    chains = [slice(c * MIX_CHAIN, (c + 1) * MIX_CHAIN) for c in range(tm // MIX_CHAIN)]

    def heads(ref, rows):
        tq = ref.shape[2]
        return ref[rows.start // tq, :, rows.start % tq:rows.start % tq + MIX_CHAIN].T

    cu = cvc_ref[...] * cvu_ref[...]
    prev = jnp.where(i % tiles_per_seq == 0, 0.0, pc_ref[...] * pu_ref[...])
    full = jnp.concatenate([prev, cu], axis=0)
    back1 = pltpu.roll(full, 1, 0)[8:, :]
    back2 = pltpu.roll(full, 2, 0)[8:, :]
    cw = cw_ref[...]
    ob = cvb_ref[...] * (cw[0:1, :] * back2 + cw[1:2, :] * back1 + cw[2:3, :] * cu)

    gates = jax.nn.sigmoid(gate_ref[...] + bg_ref[...])
    src = lax.broadcasted_iota(jnp.int32, (LANES, GROUP_W), 0)
    head3 = (lax.broadcasted_iota(jnp.int32, (LANES, GROUP_W), 1) >> 6) * 3
    branch = [(src == head3 + r).astype(BF16) for r in range(3)]
    assert d // HEAD_DIM <= LANES
    gather = ((lax.broadcasted_iota(jnp.int32, (d, LANES), 0) >> 6)
              == lax.broadcasted_iota(jnp.int32, (d, LANES), 1)).astype(BF16)
    spread = (lax.broadcasted_iota(jnp.int32, (LANES, d), 0)
              == (lax.broadcasted_iota(jnp.int32, (LANES, d), 1) >> 6)).astype(BF16)

    groups, ssq = [], []
    for rows in chains:
        oa = (_exact_dot(gates[rows], branch[0]) * heads(ocmp_ref, rows)
              + _exact_dot(gates[rows], branch[1]) * heads(oslc_ref, rows)
              + _exact_dot(gates[rows], branch[2]) * heads(owin_ref, rows))
        g = jnp.concatenate([oa, ob[rows], heads(oc_ref, rows), heads(od_ref, rows)], axis=1)
        groups.append(g)
        ssq.append(_exact_dot(g * g, gather))
    scale = [_exact_dot(lax.rsqrt(s * (1.0 / HEAD_DIM) + RMS_EPS), spread) for s in ssq]
    for rows, g, sc in zip(chains, groups, scale):
        mixed = (g * sc * gout_ref[...]).astype(BF16)
        o_ref[rows, :] = x_ref[rows, :] + jnp.dot(mixed, wout_ref[...], preferred_element_type=F32)


def _mixout(x2, p2, ocmp, oslc, owin, oc, od, bg, cw, gout, wout, seq):
    t, d = x2.shape
    tm = 2 * MIX_CHAIN
    rows = lambda w, col: pl.BlockSpec((tm, w), lambda i: (i, col // w))
    heads_t = lambda tq: pl.BlockSpec((tm // tq, GROUP_W, tq), lambda i: (i, 0, 0))
    prev8 = lambda col: pl.BlockSpec((8, GROUP_W), lambda i: (jnp.maximum(i * (tm // 8) - 1, 0), col // GROUP_W))
    const = lambda shape: pl.BlockSpec(shape, lambda i: (0, 0))
    assert tm % TQ == 0 and tm % STICK_TQ == 0
    return pl.pallas_call(
        functools.partial(_mixout_kernel, tiles_per_seq=seq // tm),
        grid=(t // tm,),
        in_specs=[rows(d, 0), heads_t(TQ), heads_t(TQ), heads_t(TQ),
                  rows(LANES, COL_GATE), rows(GROUP_W, COL_CVB), rows(GROUP_W, COL_CVC), rows(GROUP_W, COL_CVU),
                  prev8(COL_CVC), prev8(COL_CVU),
                  heads_t(STICK_TQ), heads_t(TQ),
                  const((1, LANES)), const((8, GROUP_W)), const((1, d)), const((d, d))],
        out_specs=rows(d, 0),
        out_shape=jax.ShapeDtypeStruct((t, d), F32),
        compiler_params=pltpu.CompilerParams(dimension_semantics=("parallel",), vmem_limit_bytes=VMEM_LIMIT),
        name="mixout",
    )(x2, ocmp, oslc, owin, p2, p2, p2, p2, p2, p2, oc, od, bg, cw, gout, wout)


def _ffn_kernel(x_ref, g_ref, wg_ref, wu_ref, wd_ref, o_ref, act_s, *, ff_chunk, out_chunk):
    h = _rms(x_ref[...], g_ref[...]).astype(BF16)
    dff = wg_ref.shape[1]
    for c in range(dff // ff_chunk):
        cols = slice(c * ff_chunk, (c + 1) * ff_chunk)
        a = jnp.dot(h, wg_ref[:, cols], preferred_element_type=F32)
        u = jnp.dot(h, wu_ref[:, cols], preferred_element_type=F32)
        act_s[:, cols] = (a * jax.nn.sigmoid(a) * u).astype(BF16)
    d = o_ref.shape[1]
    for c in range(d // out_chunk):
        cols = slice(c * out_chunk, (c + 1) * out_chunk)
        o_ref[:, cols] = x_ref[:, cols] + jnp.dot(act_s[...], wd_ref[:, cols], preferred_element_type=F32)


def _ffn(x2, g, wg, wu, wd):
    t, d = x2.shape
    dff = wg.shape[1]
    tm = 1024
    resident = lambda shape: pl.BlockSpec(shape, lambda i: (0, 0), pipeline_mode=pl.Buffered(1))
    return pl.pallas_call(
        functools.partial(_ffn_kernel, ff_chunk=256, out_chunk=256),
        grid=(t // tm,),
        in_specs=[pl.BlockSpec((tm, d), lambda i: (i, 0)),
                  pl.BlockSpec((1, d), lambda i: (0, 0)),
                  resident((d, dff)), resident((d, dff)), resident((dff, d))],
        out_specs=pl.BlockSpec((tm, d), lambda i: (i, 0)),
        out_shape=jax.ShapeDtypeStruct((t, d), F32),
        scratch_shapes=[pltpu.VMEM((tm, dff), BF16)],
        compiler_params=pltpu.CompilerParams(dimension_semantics=("parallel",), vmem_limit_bytes=VMEM_LIMIT),
        name="ffn",
    )(x2, g, wg, wu, wd)


def _permute_w_in_kernel(w_ref, o_ref):
    rows = w_ref.shape[0]
    n_rest = P_COLS - COL_CVB
    o_ref[:, 0:COL_GATE] = w_ref[:, 0:COL_GATE].astype(BF16)
    lane = lax.broadcasted_iota(jnp.int32, (rows, LANES), 1)
    o_ref[:, COL_GATE:COL_GATE + LANES] = jnp.where(lane < N_GATES, w_ref[:, COL_GATE:COL_GATE + LANES], 0.0).astype(BF16)
    o_ref[:, COL_CVB:P_COLS] = w_ref[:, COL_GATE + N_GATES:COL_GATE + N_GATES + n_rest].astype(BF16)


def _permute_w_in(w_in):
    d, cols = w_in.shape
    tr = 256
    return pl.pallas_call(
        _permute_w_in_kernel,
        grid=(d // tr,),
        in_specs=[pl.BlockSpec((tr, cols), lambda i: (i, 0))],
        out_specs=pl.BlockSpec((tr, P_COLS), lambda i: (i, 0)),
        out_shape=jax.ShapeDtypeStruct((d, P_COLS), BF16),
        compiler_params=pltpu.CompilerParams(dimension_semantics=("parallel",), vmem_limit_bytes=VMEM_LIMIT),
        name="permute_w_in",
    )(w_in)


def _layer(x2, batch, seq, g_mix, w_in, b_gate, g_q_nsa, g_k_cmp, g_k_slc, g_k_win, pe_k_cmp, pe_v_cmp,
           w1_k_cmp, w2_k_cmp, w1_v_cmp, w2_v_cmp, conv_w, g_q_dil, g_k_dil, g_out, w_out,
           g_ffn, w_gate, w_up, w_down):
    row = lambda v: v.reshape(1, -1)
    per_group = lambda g, n: jnp.tile(g, n).reshape(1, -1)
    p2 = _inproj(x2, row(g_mix), _permute_w_in(w_in))

    pe, w1, w2 = _compress_weights(pe_k_cmp, pe_v_cmp, w1_k_cmp, w2_k_cmp, w1_v_cmp, w2_v_cmp)
    kc, vc = _compress(p2, pe, w1, w2, per_group(g_k_cmp, 2), batch, seq)
    ocmp, oslc, owin = _nsa(p2, kc, vc, per_group(g_q_nsa, N_HEADS), per_group(g_k_slc, 2), per_group(g_k_win, 2),
                            batch, seq)
    od = _dilated(p2, per_group(g_q_dil, N_HEADS), per_group(g_k_dil, N_HEADS), batch, seq)
    oc = _stick(p2, batch, seq)

    bg = jnp.zeros((1, LANES), F32).at[0, :N_GATES].set(b_gate)
    cwp = jnp.zeros((8, GROUP_W), F32).at[:CONV_K].set(conv_w)
    x1 = _mixout(x2, p2, ocmp, oslc, owin, oc, od, bg, cwp, row(g_out), w_out.astype(BF16), seq)
    return _ffn(x1, row(g_ffn), w_gate.astype(BF16), w_up.astype(BF16), w_down.astype(BF16))


def kernel(x, g_mix, w_in, b_gate, g_q_nsa, g_k_cmp, g_k_slc, g_k_win, pe_k_cmp, pe_v_cmp, w1_k_cmp, w2_k_cmp,
           w1_v_cmp, w2_v_cmp, conv_w, g_q_dil, g_k_dil, g_out, w_out, g_ffn, w_gate, w_up, w_down):
    batch, seq, d = x.shape
    assert seq % (CMP_STRIDE * LANES) == 0 and d % LANES == 0
    assert seq <= 64 * 256
    x2 = x.reshape(batch * seq, d)
    params = (g_mix, w_in, b_gate, g_q_nsa, g_k_cmp, g_k_slc, g_k_win, pe_k_cmp, pe_v_cmp, w1_k_cmp, w2_k_cmp,
              w1_v_cmp, w2_v_cmp, conv_w, g_q_dil, g_k_dil, g_out, w_out, g_ffn, w_gate, w_up, w_down)
    for layer in range(g_mix.shape[0]):
        x2 = _layer(x2, batch, seq, *[p[layer] for p in params])
    return x2.reshape(batch, seq, d)
```

```python
import functools

import numpy as np
import jax
import jax.numpy as jnp
from jax import lax
from jax.experimental import pallas as pl
from jax.experimental.pallas import tpu as pltpu

F32 = jnp.float32
BF16 = jnp.bfloat16

HEAD_DIM = 64
N_HEADS = 4
GROUP_W = N_HEADS * HEAD_DIM
CONV_K = 3
CMP_LEN = 32
CMP_STRIDE = 16
SEL_BLOCK = 64
N_SELECT = 16
NSA_WINDOW = 512
DILATED_CONFIGS = ((128, 1), (512, 4), (2048, 16))
DILATED_WINDOW = max(w for w, _ in DILATED_CONFIGS)
FOLD = 16
FAR_FROM = 512
FAR_ROWS = 128
NEG_INF = -1e30
FORCE_SCORE = 1e6
RMS_EPS = 1e-6

TQ = 512
TK = 256
N_TOP = TQ // TK
CLS_PER_GROUP = 128 * FOLD // TQ
STICK_TQ = 256
LANES = 128
MXU_DEPTH = 256
MIX_CHAIN = 256
MXU_AHEAD = 2
FLASH_AHEAD = 2
SUM_PAD = 16
V_ROWS = HEAD_DIM + SUM_PAD
LOG2E = 1.4426950408889634
VMEM_LIMIT = 52 * 1024 * 1024

COL_QA, COL_KVC, COL_KSV, COL_KWV, COL_GATE = 0, 256, 384, 512, 640
COL_CVB, COL_CVC, COL_CVU = 768, 1024, 1280
COL_QC, COL_KC, COL_VC = 1536, 1792, 2048
COL_QD, COL_KD, COL_VD = 2304, 2560, 2816
P_COLS = 3072
N_GATES = 12

_NT = (((1,), (1,)), ((), ()))


def _alibi_slopes():
    s = [2.0 ** (-8.0 * i / 8) for i in range(1, 9)]
    return tuple(s[0::2]), tuple(s[1::2])


SLOPES_A, SLOPES_D = _alibi_slopes()


def _rms(x, g):
    return x * lax.rsqrt(jnp.mean(x * x, axis=-1, keepdims=True) + RMS_EPS) * g


def _exact_dot(a, sel):
    hi = a.astype(BF16)
    lo = (a - hi.astype(F32)).astype(BF16)
    if 2 * a.shape[1] <= MXU_DEPTH:
        return jnp.dot(jnp.concatenate([hi, lo], axis=1), jnp.concatenate([sel, sel], axis=0),
                       preferred_element_type=F32)
    return jnp.dot(hi, sel, preferred_element_type=F32) + jnp.dot(lo, sel, preferred_element_type=F32)


def _group_rms(x, g):
    w = x.shape[1]
    same = ((lax.broadcasted_iota(jnp.int32, (w, w), 0) >> 6)
            == (lax.broadcasted_iota(jnp.int32, (w, w), 1) >> 6)).astype(BF16)
    ssq = _exact_dot(x * x, same)
    return x * lax.rsqrt(ssq * (1.0 / HEAD_DIM) + RMS_EPS) * g


EVEN_AUG, ODD_AUG = HEAD_DIM, 0


def _key_aug(pos, first):
    lane = lax.broadcasted_iota(jnp.int32, (pos.shape[0], LANES), 1) - first
    hi = (pos >> 6).astype(F32)
    lo = (pos & 63).astype(F32)
    return jnp.where(lane == 0, hi, jnp.where(lane == 1, lo, jnp.where((lane == 2) | (lane == 3), 1.0, 0.0)))


def _query_aug(t, slope, first):
    lane = lax.broadcasted_iota(jnp.int32, (t.shape[0], LANES), 1) - first
    hi = (t >> 6).astype(F32) * (-64.0 * slope)
    lo = (t & 63).astype(F32) * (-slope)
    return jnp.where(lane == 0, 64.0 * slope,
                     jnp.where(lane == 1, slope, jnp.where(lane == 2, hi, jnp.where(lane == 3, lo, 0.0))))


def _pair_keys(pair, pos):
    lane = lax.broadcasted_iota(jnp.int32, pair.shape, 1)
    aug_e = 0.0 if pos is None else _key_aug(pos, EVEN_AUG)
    aug_o = 0.0 if pos is None else _key_aug(pos, ODD_AUG)
    return (jnp.where(lane < HEAD_DIM, pair, aug_e).astype(BF16),
            jnp.where(lane >= HEAD_DIM, pair, aug_o).astype(BF16))


def _prep_queries(q, g, t0, slopes, qp_s):
    qn = (q if g is None else _group_rms(q, g)) * (HEAD_DIM ** -0.5)
    tpos = t0 + lax.broadcasted_iota(jnp.int32, (q.shape[0], 1), 0)
    lane = lax.broadcasted_iota(jnp.int32, (q.shape[0], LANES), 1)
    for i in range(N_HEADS // 2):
        pair = qn[:, i * LANES:(i + 1) * LANES]
        if slopes is None:
            qp_s[i] = pair.astype(BF16)
            continue
        even = jnp.where((lane >= EVEN_AUG) & (lane < EVEN_AUG + 4), _query_aug(tpos, slopes[2 * i], EVEN_AUG), pair)
        odd = jnp.where(lane < ODD_AUG + 4, _query_aug(tpos, slopes[2 * i + 1], ODD_AUG), pair)
        qp_s[2 * i] = even.astype(BF16)
        qp_s[2 * i + 1] = odd.astype(BF16)


def _inproj_kernel(x_ref, g_ref, w_ref, o_ref, *, nchunk):
    h = _rms(x_ref[...], g_ref[...]).astype(BF16)
    cw = P_COLS // nchunk
    for c in range(nchunk):
        o_ref[:, c * cw:(c + 1) * cw] = jnp.dot(h, w_ref[:, c * cw:(c + 1) * cw], preferred_element_type=F32)


def _inproj(x2, g, w):
    t, d = x2.shape
    tm = 1024
    return pl.pallas_call(
        functools.partial(_inproj_kernel, nchunk=6),
        grid=(t // tm,),
        in_specs=[pl.BlockSpec((tm, d), lambda i: (i, 0)),
                  pl.BlockSpec((1, d), lambda i: (0, 0)),
                  pl.BlockSpec((d, P_COLS), lambda i: (0, 0), pipeline_mode=pl.Buffered(1))],
        out_specs=pl.BlockSpec((tm, P_COLS), lambda i: (i, 0)),
        out_shape=jax.ShapeDtypeStruct((t, P_COLS), F32),
        compiler_params=pltpu.CompilerParams(dimension_semantics=("parallel",), vmem_limit_bytes=VMEM_LIMIT),
        name="inproj",
    )(x2, g, w)


def _gelu_tanh(x):
    return x * (0.5 * (1.0 + jnp.tanh(np.sqrt(2.0 / np.pi).astype(np.float32) * (x + 0.044715 * (x * x * x)))))


def _compress_kernel(kv_ref, pe_ref, w1_ref, w2_ref, gk_ref, kc_ref, vct_ref):
    nc = kv_ref.shape[0] // CMP_STRIDE
    first = None
    second = None
    for j in range(CMP_STRIDE):
        tok = kv_ref[pl.ds(j, nc, stride=CMP_STRIDE), :]
        a = jnp.dot((tok + pe_ref[j:j + 1, :]).astype(BF16), w1_ref[j], preferred_element_type=F32)
        b = jnp.dot((tok + pe_ref[CMP_STRIDE + j:CMP_STRIDE + j + 1, :]).astype(BF16), w1_ref[CMP_STRIDE + j],
                    preferred_element_type=F32)
        first = a if first is None else first + a
        second = b if second is None else second + b
    hid = first + pltpu.roll(second, nc - 1, 0)
    out = jnp.dot(_gelu_tanh(hid).astype(BF16), w2_ref[...], preferred_element_type=F32)
    kn = _group_rms(out, gk_ref[...])
    end = lax.broadcasted_iota(jnp.int32, (nc, 1), 0) * CMP_STRIDE + (CMP_LEN - 1)
    even, _ = _pair_keys(kn, end)
    _, odd = _pair_keys(pltpu.roll(kn, HEAD_DIM, 1), end)
    kc_ref[0] = jnp.concatenate([even, odd], axis=1)
    vct_ref[0] = out.T[HEAD_DIM:, :].astype(BF16)


def _compress(p2, pe, w1, w2, gk, batch, seq):
    nc = seq // CMP_STRIDE
    hid2 = w1.shape[2]
    return pl.pallas_call(
        _compress_kernel,
        grid=(batch,),
        in_specs=[pl.BlockSpec((seq, LANES), lambda i: (i, COL_KVC // LANES)),
                  pl.BlockSpec((CMP_LEN, LANES), lambda i: (0, 0)),
                  pl.BlockSpec((CMP_LEN, LANES, hid2), lambda i: (0, 0, 0)),
                  pl.BlockSpec((hid2, LANES), lambda i: (0, 0)),
                  pl.BlockSpec((1, LANES), lambda i: (0, 0))],
        out_specs=[pl.BlockSpec((1, nc, 2 * LANES), lambda i: (i, 0, 0)),
                   pl.BlockSpec((1, HEAD_DIM, nc), lambda i: (i, 0, 0))],
        out_shape=[jax.ShapeDtypeStruct((batch, nc, 2 * LANES), BF16),
                   jax.ShapeDtypeStruct((batch, HEAD_DIM, nc), BF16)],
        compiler_params=pltpu.CompilerParams(dimension_semantics=("parallel",), vmem_limit_bytes=VMEM_LIMIT),
        name="nsa_compress",
    )(p2, pe, w1, w2, gk)


def _compress_weights(pe_k, pe_v, w1_k, w2_k, w1_v, w2_v):
    hid = w1_k.shape[1]
    w1k = w1_k.reshape(CMP_LEN, HEAD_DIM, hid)
    w1v = w1_v.reshape(CMP_LEN, HEAD_DIM, hid)
    z1 = jnp.zeros_like(w1k)
    w1 = jnp.concatenate([jnp.concatenate([w1k, z1], axis=2), jnp.concatenate([z1, w1v], axis=2)], axis=1)
    z2 = jnp.zeros_like(w2_k)
    w2 = jnp.concatenate([jnp.concatenate([w2_k, z2], axis=1), jnp.concatenate([z2, w2_v], axis=1)], axis=0)
    pe = jnp.concatenate([pe_k, pe_v], axis=1)
    return pe, w1.astype(BF16), w2.astype(BF16)


def _flash_scores(kp_of, bias, qp_s, s_s, m_cur):
    m_next = []
    for h in range(N_HEADS):
        s = lax.dot_general(kp_of(h), qp_s[h], _NT, preferred_element_type=F32) + bias
        s_s[h] = s
        m_next.append(jnp.maximum(m_cur[h], jnp.max(s, axis=0, keepdims=True)))
    return tuple(m_next)


def _flash_accum(vt_of, s_s, acc_s, m_prev, m_cur):
    for h in range(N_HEADS):
        alpha = jnp.exp(m_prev[h] - m_cur[h])
        p = jnp.exp(s_s[h] - m_cur[h])
        acc_s[h] = alpha * acc_s[h] + jnp.dot(vt_of(h), p.astype(BF16), preferred_element_type=F32)


def _flash_accum_and_scores(kp_of, bias, vt_of, qp_s, s_s, acc_s, m_prev, m_cur):
    def scores(h):
        return lax.dot_general(kp_of(h), qp_s[h], _NT, preferred_element_type=F32) + bias

    m_next = []
    s_new = {h: scores(h) for h in range(FLASH_AHEAD)}
    for h in range(N_HEADS):
        alpha = jnp.exp(m_prev[h] - m_cur[h])
        p = jnp.exp(s_s[h] - m_cur[h])
        acc_s[h] = alpha * acc_s[h] + jnp.dot(vt_of(h), p.astype(BF16), preferred_element_type=F32)
        s_h = s_new.pop(h)
        s_s[h] = s_h
        m_next.append(jnp.maximum(m_cur[h], jnp.max(s_h, axis=0, keepdims=True)))
        if h + FLASH_AHEAD < N_HEADS:
            s_new[h + FLASH_AHEAD] = scores(h + FLASH_AHEAD)
    return tuple(m_next)


def _flash_pipelined(tile_of, n_tiles, kp_fn, bias_fn, vt_fn, qp_s, s_s, acc_s, o_ref):
    acc_s[...] = jnp.zeros(acc_s.shape, F32)
    m0 = tuple(jnp.full((1, TQ), NEG_INF, F32) for _ in range(N_HEADS))
    first_kt = tile_of(0)
    m1 = _flash_scores(kp_fn(first_kt), bias_fn(first_kt, True), qp_s, s_s, m0)

    def step(i, carry, top):
        m_prev, m_cur = carry
        nxt = tile_of(i + 1)
        m_next = _flash_accum_and_scores(kp_fn(nxt), bias_fn(nxt, top), vt_fn(tile_of(i)), qp_s, s_s, acc_s,
                                         m_prev, m_cur)
        return m_cur, m_next

    carry = (m0, m1)
    for i in range(N_TOP - 1):
        carry = step(i, carry, True)
    m_prev, m_cur = lax.fori_loop(N_TOP - 1, n_tiles - 1, lambda i, c: step(i, c, False), carry)
    _flash_accum(vt_fn(tile_of(n_tiles - 1)), s_s, acc_s, m_prev, m_cur)
    if o_ref is None:
        return m_cur
    _flash_write(acc_s, o_ref)


def _flash_write(acc_s, o_ref):
    o_ref[0] = jnp.concatenate([acc_s[h, 0:HEAD_DIM, :] * (1.0 / acc_s[h, HEAD_DIM:HEAD_DIM + 1, :])
                                for h in range(N_HEADS)], axis=0)


def _values_t(v_t):
    row = lax.broadcasted_iota(jnp.int32, (V_ROWS - HEAD_DIM, v_t.shape[1]), 0)
    return jnp.concatenate([v_t, jnp.where(row == 0, 1.0, 0.0)], axis=0).astype(BF16)


def _nsa_kernel(q_ref, kc_ref, vct_ref, ksv_ref, kwv_ref, gq_ref, gks_ref, gkw_ref, wbias_ref,
                ocmp_ref, oslc_ref, owin_ref,
                ksp_s, vst_s, kwp_s, vwt_s, qp_s, imp_s, sel_s, s_s, acc_s, hit_s, *, seq):
    qi = pl.program_id(1)
    nc = seq // CMP_STRIDE
    nsel = seq // SEL_BLOCK
    nkt = seq // TK

    @pl.when(qi == 0)
    def _prep():
        for c in range(nkt):
            rows = slice(c * TK, (c + 1) * TK)
            pos = c * TK + lax.broadcasted_iota(jnp.int32, (TK, 1), 0)
            for kv_ref, g_ref, kp_s, vt_s in ((ksv_ref, gks_ref, ksp_s, vst_s), (kwv_ref, gkw_ref, kwp_s, vwt_s)):
                blk = kv_ref[rows, :]
                kn = _group_rms(blk, g_ref[...])
                kp_s[rows, 0:LANES], _ = _pair_keys(kn, pos)
                _, kp_s[rows, LANES:2 * LANES] = _pair_keys(pltpu.roll(kn, HEAD_DIM, 1), pos)
                vt_s[c] = _values_t(blk.T[HEAD_DIM:, :])
        imp_s[:, 0:8, :] = jnp.zeros((TQ // LANES, 8, LANES), F32)

    t0 = qi * TQ
    _prep_queries(q_ref[...], gq_ref[...], t0, SLOPES_A, qp_s)

    kc = [kc_ref[0, :, 0:LANES], kc_ref[0, :, LANES:2 * LANES]]
    vct = vct_ref[0]
    tq_row = t0 + lax.broadcasted_iota(jnp.int32, (nc, TQ), 1)
    n_idx = lax.broadcasted_iota(jnp.int32, (nc, TQ), 0)
    vis = (tq_row >= n_idx * CMP_STRIDE + (CMP_LEN - 1)) & (n_idx < nc - 1)
    vis_bias = jnp.where(vis, 0.0, NEG_INF)
    sees_any = (t0 + lax.broadcasted_iota(jnp.int32, (1, TQ), 1) >= CMP_LEN - 1).astype(F32)
    imp = jnp.zeros((nc, TQ), F32)
    ocmp_t = []
    scores = [lax.dot_general(kc[h % 2], qp_s[h], _NT, preferred_element_type=F32) + vis_bias
              for h in range(N_HEADS)]
    for h in range(N_HEADS):
        sc = scores[h]
        e = jnp.exp(sc - jnp.max(sc, axis=0, keepdims=True))
        p = e * (sees_any / jnp.sum(e, axis=0, keepdims=True))
        ocmp_t.append(jnp.dot(vct, p.astype(BF16), preferred_element_type=F32))
        imp = imp + p
    ocmp_ref[0] = jnp.concatenate(ocmp_t, axis=0)

    halves = []
    for half in range(TQ // LANES):
        imp_s[half, 8:8 + nc, :] = imp[:, half * LANES:(half + 1) * LANES]
        r = [imp_s[half, pl.ds(8 + k, nsel, stride=4), :] for k in range(4)]
        rm1 = imp_s[half, pl.ds(7, nsel, stride=4), :]
        halves.append(rm1 + 2.0 * (r[0] + r[1] + r[2]) + r[3])
    imp_blk = jnp.concatenate(halves, axis=1)
    blk = lax.broadcasted_iota(jnp.int32, (nsel, TQ), 0)
    tl = t0 + lax.broadcasted_iota(jnp.int32, (nsel, TQ), 1)
    cur = tl >> 6
    forced = (blk == 0) | (blk == cur) | (blk == cur - 1)
    valid = blk * SEL_BLOCK <= tl
    score = jnp.where(forced, FORCE_SCORE, jnp.where(valid, imp_blk, -FORCE_SCORE))
    sub = 8
    per_tile = TK // SEL_BLOCK
    tile_hit = []
    for g in range(nsel // sub):
        mine = score[g * sub:(g + 1) * sub, :]
        blk_g = g * sub + lax.broadcasted_iota(jnp.int32, (sub, TQ), 0)
        rank = jnp.zeros((sub, TQ), F32)
        for j in range(nsel):
            row = score[j:j + 1, :]
            if j < g * sub:
                beats = row >= mine
            elif j >= (g + 1) * sub:
                beats = row > mine
            else:
                beats = (row > mine) | ((row == mine) & (blk_g > j))
            rank = rank + jnp.where(beats, 1.0, 0.0)
        sel_g = jnp.where(rank < min(N_SELECT, nsel), 0.0, NEG_INF)
        sel_s[g * sub:(g + 1) * sub, :] = sel_g
        for part in range(sub // per_tile):
            tile_hit.append(jnp.max(sel_g[part * per_tile:(part + 1) * per_tile, :]))

    n_hit = jnp.int32(0)
    for kt in reversed(range(nkt)):
        hit_s[n_hit] = kt
        n_hit = n_hit + ((tile_hit[kt] > -1.0) & (kt * TK < t0)).astype(jnp.int32)

    def block_bias(kt):
        return jnp.concatenate([jnp.broadcast_to(sel_s[pl.ds(kt * per_tile + j, 1), :], (SEL_BLOCK, TQ))
                                for j in range(per_tile)], axis=0)

    k_local = lax.broadcasted_iota(jnp.int32, (TK, TQ), 0)
    q_local = lax.broadcasted_iota(jnp.int32, (TK, TQ), 1)
    top_kt = qi * N_TOP + (N_TOP - 1)

    def kp_fn(kt):
        off = pl.multiple_of(kt * TK, TK)
        return lambda h: ksp_s[pl.ds(off, TK), (h % 2) * LANES:(h % 2 + 1) * LANES]

    def bias_fn(kt, top):
        if top:
            return jnp.where(kt * TK - t0 + k_local <= q_local, block_bias(kt), NEG_INF)
        return block_bias(kt)

    tile_of = lambda i: jnp.where(i < N_TOP, top_kt - i, hit_s[jnp.maximum(i - N_TOP, 0)])
    _flash_pipelined(tile_of, n_hit + N_TOP, kp_fn, bias_fn, lambda kt: (lambda h: vst_s[kt]), qp_s, s_s, acc_s,
                     oslc_ref)

    def window_kp_fn(kt):
        off = pl.multiple_of(kt * TK, TK)
        return lambda h: kwp_s[pl.ds(off, TK), (h % 2) * LANES:(h % 2 + 1) * LANES]

    kt_lo = jnp.maximum(t0 - (NSA_WINDOW - 1), 0) // TK
    _flash_pipelined(lambda i: top_kt - i, top_kt + 1 - kt_lo, window_kp_fn, lambda kt, top: wbias_ref[top_kt - kt],
                     lambda kt: (lambda h: vwt_s[kt]), qp_s, s_s, acc_s, owin_ref)


def _nsa(p2, kc, vc, gq, gks, gkw, batch, seq):
    nq = seq // TQ
    nc = seq // CMP_STRIDE
    t = batch * seq
    wbias = jnp.asarray(_band_bias_tiles(NSA_WINDOW - 1, _window_mult))
    head_tiles = pl.BlockSpec((1, GROUP_W, TQ), lambda b, i: (b * nq + i, 0, 0))
    head_shape = jax.ShapeDtypeStruct((t // TQ, GROUP_W, TQ), F32)
    return pl.pallas_call(
        functools.partial(_nsa_kernel, seq=seq),
        grid=(batch, nq),
        in_specs=[pl.BlockSpec((TQ, GROUP_W), lambda b, i: (b * nq + i, COL_QA // GROUP_W)),
                  pl.BlockSpec((1, nc, 2 * LANES), lambda b, i: (b, 0, 0)),
                  pl.BlockSpec((1, HEAD_DIM, nc), lambda b, i: (b, 0, 0)),
                  pl.BlockSpec((seq, LANES), lambda b, i: (b, COL_KSV // LANES)),
                  pl.BlockSpec((seq, LANES), lambda b, i: (b, COL_KWV // LANES)),
                  pl.BlockSpec((1, GROUP_W), lambda b, i: (0, 0)),
                  pl.BlockSpec((1, LANES), lambda b, i: (0, 0)),
                  pl.BlockSpec((1, LANES), lambda b, i: (0, 0)),
                  pl.BlockSpec(wbias.shape, lambda b, i: (0, 0, 0))],
        out_specs=[head_tiles, head_tiles, head_tiles],
        out_shape=[head_shape, head_shape, head_shape],
        scratch_shapes=[pltpu.VMEM((seq, 2 * LANES), BF16),
                        pltpu.VMEM((seq // TK, V_ROWS, TK), BF16),
                        pltpu.VMEM((seq, 2 * LANES), BF16),
                        pltpu.VMEM((seq // TK, V_ROWS, TK), BF16),
                        pltpu.VMEM((N_HEADS, TQ, LANES), BF16),
                        pltpu.VMEM((TQ // LANES, 8 + nc, LANES), F32),
                        pltpu.VMEM((seq // SEL_BLOCK, TQ), F32),
                        pltpu.VMEM((N_HEADS, TK, TQ), F32),
                        pltpu.VMEM((N_HEADS, V_ROWS, TQ), F32),
                        pltpu.SMEM((seq // TK + 1,), jnp.int32)],
        compiler_params=pltpu.CompilerParams(dimension_semantics=("parallel", "arbitrary"),
                                             vmem_limit_bytes=VMEM_LIMIT),
        name="nsa",
    )(p2, kc, vc, p2, p2, gq, gks, gkw, wbias)


def _band_bias_tiles(window, mult_fn, folded=False):
    nd = (window + TQ - 1) // TK + 1
    kk = np.arange(TK)[:, None]
    qq = np.arange(TQ)[None, :]
    if folded:
        qq = FOLD * (qq % (TQ // FOLD)) + qq // (TQ // FOLD)
    tiles = np.empty((nd, TK, TQ), np.float32)
    for di in range(nd):
        d = (di - (N_TOP - 1)) * TK + qq - kk
        mult = mult_fn(d)
        tiles[di] = np.where(mult > 0, np.log(np.maximum(mult, 1)), NEG_INF)
    return tiles


def _window_mult(d):
    return ((d >= 0) & (d <= NSA_WINDOW - 1)).astype(np.float64)


def _dilated_mult(d):
    m = np.zeros(d.shape, np.float64)
    for window, dil in DILATED_CONFIGS:
        m += ((d >= 0) & (d <= window) & (d % dil == 0)).astype(np.float64)
    return m


def _near_mult(d):
    return np.where(d <= FAR_FROM, _dilated_mult(d), 0.0)


def _dilated_kernel(q0_ref, q1_ref, k0_ref, k1_ref, v0_ref, v1_ref, gq_ref, gk_ref, bias_ref, o_ref,
                    kp_s, vt_s, kf_s, vf_s, qp_s, qcat_s, s_s, acc_s, m_s, *, seq):
    qi = pl.program_id(1)
    per = TQ // FOLD
    cls_rows = seq // FOLD
    pairs = ((q0_ref, k0_ref, v0_ref), (q1_ref, k1_ref, v1_ref))

    @pl.when(qi == 0)
    def _prep():
        for p, (_, k_ref, v_ref) in enumerate(pairs):
            for c in range(seq // TK):
                rows = slice(c * TK, (c + 1) * TK)
                pos = c * TK + lax.broadcasted_iota(jnp.int32, (TK, 1), 0)
                even, odd = _pair_keys(_group_rms(k_ref[rows, :], gk_ref[...]), pos)
                kp_s[rows, 2 * p * LANES:(2 * p + 1) * LANES] = even
                kp_s[rows, (2 * p + 1) * LANES:(2 * p + 2) * LANES] = odd
                v_t = v_ref[rows, :].T
                for j in range(2):
                    vt_s[c, (2 * p + j) * V_ROWS:(2 * p + j + 1) * V_ROWS, :] = _values_t(
                        v_t[j * HEAD_DIM:(j + 1) * HEAD_DIM, :])
            for r in range(FOLD):
                pos = r + FOLD * lax.broadcasted_iota(jnp.int32, (cls_rows, 1), 0)
                even, odd = _pair_keys(_group_rms(k_ref[pl.ds(r, cls_rows, stride=FOLD), :], gk_ref[...]), pos)
                kf_s[2 * p, :, r * LANES:(r + 1) * LANES] = even
                kf_s[2 * p + 1, :, r * LANES:(r + 1) * LANES] = odd
                v_t = v_ref[pl.ds(r, cls_rows, stride=FOLD), :].T
                for j in range(2):
                    vf_s[2 * p + j, :, r * cls_rows:(r + 1) * cls_rows] = _values_t(
                        v_t[j * HEAD_DIM:(j + 1) * HEAD_DIM, :])

    t0 = qi * TQ
    u = lax.broadcasted_iota(jnp.int32, (TQ, 1), 0)
    tpos = t0 + FOLD * (u % per) + u // per
    lane = lax.broadcasted_iota(jnp.int32, (TQ, LANES), 1)
    for p, (q_ref, _, _) in enumerate(pairs):
        q = jnp.concatenate([q_ref[pl.ds(r, per, stride=FOLD), :] for r in range(FOLD)], axis=0)
        pair = _group_rms(q, gq_ref[...]) * (HEAD_DIM ** -0.5)
        even = jnp.where((lane >= EVEN_AUG) & (lane < EVEN_AUG + 4), _query_aug(tpos, SLOPES_D[2 * p], EVEN_AUG), pair)
        odd = jnp.where(lane < ODD_AUG + 4, _query_aug(tpos, SLOPES_D[2 * p + 1], ODD_AUG), pair)
        qp_s[2 * p] = even.astype(BF16)
        qp_s[2 * p + 1] = odd.astype(BF16)

    kt_lo = jnp.maximum(t0 - FAR_FROM, 0) // TK
    top_kt = qi * N_TOP + (N_TOP - 1)

    def kp_fn(kt):
        off = pl.multiple_of(kt * TK, TK)
        return lambda h: kp_s[pl.ds(off, TK), h * LANES:(h + 1) * LANES]

    m = _flash_pipelined(lambda i: top_kt - i, top_kt + 1 - kt_lo, kp_fn, lambda kt, top: bias_ref[top_kt - kt],
                         lambda kt: (lambda h: vt_s[kt, h * V_ROWS:(h + 1) * V_ROWS, :]), qp_s, s_s, acc_s, None)
    for h in range(N_HEADS):
        m_s[h] = m[h]

    groups = TQ // LANES
    cls_of_row = lax.broadcasted_iota(jnp.int32, (LANES, LANES), 0) // per
    cls_of_lane = lax.broadcasted_iota(jnp.int32, (FAR_ROWS, LANES), 1) // per
    for h in range(N_HEADS):
        for g in range(groups):
            rows = qp_s[h, g * LANES:(g + 1) * LANES, :]
            qcat_s[h, g] = jnp.concatenate([jnp.where(cls_of_row == j, rows, jnp.zeros_like(rows))
                                            for j in range(CLS_PER_GROUP)], axis=1)

    a0 = t0 // FOLD
    a_lane = a0 + lax.broadcasted_iota(jnp.int32, (FAR_ROWS, TQ), 1) % per
    row = lax.broadcasted_iota(jnp.int32, (FAR_ROWS, TQ), 0)

    def far_block(blk):
        back = a_lane - (blk * FAR_ROWS + row)
        bias = jnp.where((back > FAR_FROM // FOLD) & (back <= DILATED_WINDOW // FOLD), 0.0, NEG_INF)
        keys = slice(blk * FAR_ROWS, (blk + 1) * FAR_ROWS)
        scores = [jnp.concatenate(
            [lax.dot_general(kf_s[h, keys, g * CLS_PER_GROUP * LANES:(g + 1) * CLS_PER_GROUP * LANES], qcat_s[h, g],
                             _NT, preferred_element_type=F32) for g in range(groups)], axis=1) + bias
            for h in range(N_HEADS)]
        for h in range(N_HEADS):
            m_old = m_s[h]
            m_new = jnp.maximum(m_old, jnp.max(scores[h], axis=0, keepdims=True))
            p = jnp.exp(scores[h] - m_new)
            pv = []
            for g in range(groups):
                pg = p[:, g * LANES:(g + 1) * LANES]
                rhs = jnp.concatenate([jnp.where(cls_of_lane == j, pg, 0.0) for j in range(CLS_PER_GROUP)],
                                      axis=0).astype(BF16)
                lhs = jnp.concatenate(
                    [vf_s[h, :, (g * CLS_PER_GROUP + j) * cls_rows + blk * FAR_ROWS:
                           (g * CLS_PER_GROUP + j) * cls_rows + (blk + 1) * FAR_ROWS] for j in range(CLS_PER_GROUP)],
                    axis=1)
                pv.append(jnp.dot(lhs, rhs, preferred_element_type=F32))
            acc_s[h] = jnp.exp(m_old - m_new) * acc_s[h] + jnp.concatenate(pv, axis=1)
            m_s[h] = m_new

    for blk in range(cls_rows // FAR_ROWS):
        needed = (a0 + per - 1 - blk * FAR_ROWS > FAR_FROM // FOLD) & \
                 (a0 - (blk * FAR_ROWS + FAR_ROWS - 1) <= DILATED_WINDOW // FOLD)
        pl.when(needed)(functools.partial(far_block, blk))
    _flash_write(acc_s, o_ref)


def _dilated(p2, gq, gk, batch, seq):
    nq = seq // TQ
    t = batch * seq
    assert FAR_FROM % FOLD == 0 and TQ % FOLD == 0 and (seq // FOLD) % FAR_ROWS == 0 and (TQ // FOLD) * CLS_PER_GROUP == LANES
    bias_tiles = jnp.asarray(_band_bias_tiles(FAR_FROM, _near_mult, folded=True))
    pair_cols = lambda rows, col: [pl.BlockSpec((rows, LANES), functools.partial(
        lambda b, i, c: (b if rows == seq else b * nq + i, c), c=col // LANES + j)) for j in range(2)]
    gain = pl.BlockSpec((1, LANES), lambda b, i: (0, 0))
    return pl.pallas_call(
        functools.partial(_dilated_kernel, seq=seq),
        grid=(batch, nq),
        in_specs=pair_cols(TQ, COL_QD) + pair_cols(seq, COL_KD) + pair_cols(seq, COL_VD)
                 + [gain, gain, pl.BlockSpec(bias_tiles.shape, lambda b, i: (0, 0, 0))],
        out_specs=pl.BlockSpec((1, GROUP_W, TQ), lambda b, i: (b * nq + i, 0, 0)),
        out_shape=jax.ShapeDtypeStruct((t // TQ, GROUP_W, TQ), F32),
        scratch_shapes=[pltpu.VMEM((seq, N_HEADS * LANES), BF16),
                        pltpu.VMEM((seq // TK, N_HEADS * V_ROWS, TK), BF16),
                        pltpu.VMEM((N_HEADS, seq // FOLD, FOLD * LANES), BF16),
                        pltpu.VMEM((N_HEADS, V_ROWS, seq), BF16),
                        pltpu.VMEM((N_HEADS, TQ, LANES), BF16),
                        pltpu.VMEM((N_HEADS, TQ // LANES, LANES, CLS_PER_GROUP * LANES), BF16),
                        pltpu.VMEM((N_HEADS, TK, TQ), F32),
                        pltpu.VMEM((N_HEADS, V_ROWS, TQ), F32),
                        pltpu.VMEM((N_HEADS, 1, TQ), F32)],
        compiler_params=pltpu.CompilerParams(dimension_semantics=("parallel", "arbitrary"),
                                             vmem_limit_bytes=VMEM_LIMIT),
        name="dilated",
    )(p2, p2, p2, p2, p2, p2, gq, gk, bias_tiles)


def _stick_kernel(q_ref, k_ref, v_ref, o_ref, kp_s, vt_s, qp_s, sp_s, e_s, acc_s, *, seq):
    TQ, N_TOP = STICK_TQ, STICK_TQ // TK
    qi = pl.program_id(1)
    nkt = seq // TK

    @pl.when(qi == 0)
    def _prep():
        for c in range(nkt):
            rows = slice(c * TK, (c + 1) * TK)
            kb = k_ref[rows, :]
            for i in range(N_HEADS // 2):
                even, odd = _pair_keys(kb[:, i * LANES:(i + 1) * LANES], None)
                kp_s[rows, 2 * i * LANES:(2 * i + 1) * LANES] = even
                kp_s[rows, (2 * i + 1) * LANES:(2 * i + 2) * LANES] = odd
            vt_s[c] = v_ref[rows, :].T.astype(BF16)

    _prep_queries(q_ref[...], None, qi * TQ, None, qp_s)
    half = TK // 2
    col = lax.broadcasted_iota(jnp.int32, (half + SUM_PAD, half), 1)
    srow = lax.broadcasted_iota(jnp.int32, (half + SUM_PAD, half), 0)
    sums = ((srow == half) | ((srow < half) & (col > srow))).astype(BF16)
    k_local = lax.broadcasted_iota(jnp.int32, (TK, TQ), 0)
    q_local = lax.broadcasted_iota(jnp.int32, (TK, TQ), 1)
    top_kt = qi * N_TOP + (N_TOP - 1)
    acc_s[...] = jnp.zeros(acc_s.shape, F32)

    def past_mask(tiles_above):
        return tiles_above * TK + k_local < q_local

    def logits(kt, h):
        off = pl.multiple_of(kt * TK, TK)
        return lax.dot_general(kp_s[pl.ds(off, TK), h * LANES:(h + 1) * LANES], qp_s[h // 2], _NT,
                               preferred_element_type=F32)

    def stage(h, z, past):
        sp = jnp.maximum(z, 0.0) + jnp.log(1.0 + jnp.exp2(jnp.abs(z) * (-LOG2E)))
        spm = sp if past is None else jnp.where(past, sp, 0.0)
        sp_s[h] = spm.astype(BF16)
        logsig = z - sp
        e_s[h] = logsig if past is None else jnp.where(past, logsig, NEG_INF)

    def tails(h):
        return [jnp.dot(sums, sp_s[h, b * half:(b + 1) * half, :], preferred_element_type=F32) for b in range(2)]

    def consume(kt, h, w, later):
        later_lo = later + w[1][half:half + 1, :]
        attn = jnp.concatenate([jnp.exp(e_s[h, 0:half, :] - w[0][0:half, :] - later_lo),
                                jnp.exp(e_s[h, half:TK, :] - w[1][0:half, :] - later)], axis=0)
        acc_s[h] += jnp.dot(vt_s[kt, h * HEAD_DIM:(h + 1) * HEAD_DIM, :], attn.astype(BF16),
                            preferred_element_type=F32)
        return later_lo + w[0][half:half + 1, :]

    ahead = MXU_AHEAD
    for h in range(N_HEADS):
        stage(h, logits(top_kt, h), past_mask(N_TOP - 1))

    def step(kt, carry, past):
        laters, w_first, z_first = carry
        w = {0: list(w_first)}
        z = {0: z_first}
        for h in range(1, ahead):
            w[h] = tails(h)
            z[h] = logits(kt - 1, h)
        out = []
        for h in range(N_HEADS):
            out.append(consume(kt, h, w.pop(h), laters[h]))
            stage(h, z.pop(h), past)
            if h + ahead < N_HEADS:
                w[h + ahead] = tails(h + ahead)
                z[h + ahead] = logits(kt - 1, h + ahead)
            elif h + ahead == N_HEADS:
                nxt = (tuple(tails(0)), logits(jnp.maximum(kt - 2, 0), 0))
        return tuple(out), nxt[0], nxt[1]

    carry = (tuple(jnp.zeros((1, TQ), F32) for _ in range(N_HEADS)),
             tuple(tails(0)), logits(jnp.maximum(top_kt - 1, 0), 0))
    for j in range(N_TOP - 1):
        carry = step(top_kt - j, carry, past_mask(N_TOP - 2 - j))
    below = qi * N_TOP
    laters, w_first, _ = lax.fori_loop(0, below, lambda i, c: step(below - i, c, None), carry)
    w = {0: list(w_first)}
    for h in range(1, ahead):
        w[h] = tails(h)
    for h in range(N_HEADS):
        consume(0, h, w.pop(h), laters[h])
        if h + ahead < N_HEADS:
            w[h + ahead] = tails(h + ahead)
    o_ref[0] = acc_s[...].reshape(GROUP_W, TQ)


def _stick(p2, batch, seq):
    TQ = STICK_TQ
    nq = seq // TQ
    t = batch * seq
    return pl.pallas_call(
        functools.partial(_stick_kernel, seq=seq),
        grid=(batch, nq),
        in_specs=[pl.BlockSpec((TQ, GROUP_W), lambda b, i: (b * nq + i, COL_QC // GROUP_W)),
                  pl.BlockSpec((seq, GROUP_W), lambda b, i: (b, COL_KC // GROUP_W)),
                  pl.BlockSpec((seq, GROUP_W), lambda b, i: (b, COL_VC // GROUP_W))],
        out_specs=pl.BlockSpec((1, GROUP_W, TQ), lambda b, i: (b * nq + i, 0, 0)),
        out_shape=jax.ShapeDtypeStruct((t // TQ, GROUP_W, TQ), F32),
        scratch_shapes=[pltpu.VMEM((seq, N_HEADS * LANES), BF16),
                        pltpu.VMEM((seq // TK, GROUP_W, TK), BF16),
                        pltpu.VMEM((N_HEADS // 2, TQ, LANES), BF16),
                        pltpu.VMEM((N_HEADS, TK, TQ), BF16),
                        pltpu.VMEM((N_HEADS, TK, TQ), F32),
                        pltpu.VMEM((N_HEADS, HEAD_DIM, TQ), F32)],
        compiler_params=pltpu.CompilerParams(dimension_semantics=("parallel", "arbitrary"),
                                             vmem_limit_bytes=VMEM_LIMIT),
        name="stick_breaking",
    )(p2, p2, p2)


def _mixout_kernel(x_ref, ocmp_ref, oslc_ref, owin_ref, gate_ref, cvb_ref, cvc_ref, cvu_ref, pc_ref, pu_ref,
                   oc_ref, od_ref, bg_ref, cw_ref, gout_ref, wout_ref, o_ref, od_s, *, tiles_per_seq):
    i = pl.program_id(0)
    tm, d = x_ref.shape
    chains = [slice(c * MIX_CHAIN, (c + 1) * MIX_CHAIN) for c in range(tm // MIX_CHAIN)]

    def heads(ref, rows):
        tq = ref.shape[2]
        return ref[rows.start // tq, :, rows.start % tq:rows.start % tq + MIX_CHAIN].T

    per = TQ // FOLD
    for tile in range(tm // TQ):
        folded = od_ref[tile].T
        for half in range(GROUP_W // LANES):
            od_s[tile, half] = folded[:, half * LANES:(half + 1) * LANES]

    def unfolded(rows):
        tile, first = rows.start // TQ, (rows.start % TQ) // FOLD
        return jnp.concatenate(
            [jnp.concatenate([od_s[tile, half, pl.ds(a, FOLD, stride=per), :] for a in range(first, first + MIX_CHAIN // FOLD)],
                             axis=0) for half in range(GROUP_W // LANES)], axis=1)

    cu = cvc_ref[...] * cvu_ref[...]
    prev = jnp.where(i % tiles_per_seq == 0, 0.0, pc_ref[...] * pu_ref[...])
    full = jnp.concatenate([prev, cu], axis=0)
    back1 = pltpu.roll(full, 1, 0)[8:, :]
    back2 = pltpu.roll(full, 2, 0)[8:, :]
    cw = cw_ref[...]
    ob = cvb_ref[...] * (cw[0:1, :] * back2 + cw[1:2, :] * back1 + cw[2:3, :] * cu)

    gates = jax.nn.sigmoid(gate_ref[...] + bg_ref[...])
    src = lax.broadcasted_iota(jnp.int32, (LANES, GROUP_W), 0)
    head3 = (lax.broadcasted_iota(jnp.int32, (LANES, GROUP_W), 1) >> 6) * 3
    branch = [(src == head3 + r).astype(BF16) for r in range(3)]
    assert d // HEAD_DIM <= LANES
    gather = ((lax.broadcasted_iota(jnp.int32, (d, LANES), 0) >> 6)
              == lax.broadcasted_iota(jnp.int32, (d, LANES), 1)).astype(BF16)
    spread = (lax.broadcasted_iota(jnp.int32, (LANES, d), 0)
              == (lax.broadcasted_iota(jnp.int32, (LANES, d), 1) >> 6)).astype(BF16)

    groups, ssq = [], []
    for rows in chains:
        oa = (_exact_dot(gates[rows], branch[0]) * heads(ocmp_ref, rows)
              + _exact_dot(gates[rows], branch[1]) * heads(oslc_ref, rows)
              + _exact_dot(gates[rows], branch[2]) * heads(owin_ref, rows))
        g = jnp.concatenate([oa, ob[rows], heads(oc_ref, rows), unfolded(rows)], axis=1)
        groups.append(g)
        ssq.append(_exact_dot(g * g, gather))
    scale = [_exact_dot(lax.rsqrt(s * (1.0 / HEAD_DIM) + RMS_EPS), spread) for s in ssq]
    for rows, g, sc in zip(chains, groups, scale):
        mixed = (g * sc * gout_ref[...]).astype(BF16)
        o_ref[rows, :] = x_ref[rows, :] + jnp.dot(mixed, wout_ref[...], preferred_element_type=F32)


def _mixout(x2, p2, ocmp, oslc, owin, oc, od, bg, cw, gout, wout, seq):
    t, d = x2.shape
    tm = 2 * MIX_CHAIN
    rows = lambda w, col: pl.BlockSpec((tm, w), lambda i: (i, col // w))
    heads_t = lambda tq: pl.BlockSpec((tm // tq, GROUP_W, tq), lambda i: (i, 0, 0))
    prev8 = lambda col: pl.BlockSpec((8, GROUP_W), lambda i: (jnp.maximum(i * (tm // 8) - 1, 0), col // GROUP_W))
    const = lambda shape: pl.BlockSpec(shape, lambda i: (0, 0))
    assert tm % TQ == 0 and tm % STICK_TQ == 0
    return pl.pallas_call(
        functools.partial(_mixout_kernel, tiles_per_seq=seq // tm),
        grid=(t // tm,),
        in_specs=[rows(d, 0), heads_t(TQ), heads_t(TQ), heads_t(TQ),
                  rows(LANES, COL_GATE), rows(GROUP_W, COL_CVB), rows(GROUP_W, COL_CVC), rows(GROUP_W, COL_CVU),
                  prev8(COL_CVC), prev8(COL_CVU),
                  heads_t(STICK_TQ), heads_t(TQ),
                  const((1, LANES)), const((8, GROUP_W)), const((1, d)), const((d, d))],
        out_specs=rows(d, 0),
        out_shape=jax.ShapeDtypeStruct((t, d), F32),
        scratch_shapes=[pltpu.VMEM((tm // TQ, GROUP_W // LANES, TQ, LANES), F32)],
        compiler_params=pltpu.CompilerParams(dimension_semantics=("parallel",), vmem_limit_bytes=VMEM_LIMIT),
        name="mixout",
    )(x2, ocmp, oslc, owin, p2, p2, p2, p2, p2, p2, oc, od, bg, cw, gout, wout)


def _ffn_kernel(x_ref, g_ref, wg_ref, wu_ref, wd_ref, o_ref, act_s, *, ff_chunk, out_chunk):
    h = _rms(x_ref[...], g_ref[...]).astype(BF16)
    dff = wg_ref.shape[1]
    for c in range(dff // ff_chunk):
        cols = slice(c * ff_chunk, (c + 1) * ff_chunk)
        a = jnp.dot(h, wg_ref[:, cols], preferred_element_type=F32)
        u = jnp.dot(h, wu_ref[:, cols], preferred_element_type=F32)
        act_s[:, cols] = (a * jax.nn.sigmoid(a) * u).astype(BF16)
    d = o_ref.shape[1]
    for c in range(d // out_chunk):
        cols = slice(c * out_chunk, (c + 1) * out_chunk)
        o_ref[:, cols] = x_ref[:, cols] + jnp.dot(act_s[...], wd_ref[:, cols], preferred_element_type=F32)


def _ffn(x2, g, wg, wu, wd):
    t, d = x2.shape
    dff = wg.shape[1]
    tm = 1024
    resident = lambda shape: pl.BlockSpec(shape, lambda i: (0, 0), pipeline_mode=pl.Buffered(1))
    return pl.pallas_call(
        functools.partial(_ffn_kernel, ff_chunk=256, out_chunk=256),
        grid=(t // tm,),
        in_specs=[pl.BlockSpec((tm, d), lambda i: (i, 0)),
                  pl.BlockSpec((1, d), lambda i: (0, 0)),
                  resident((d, dff)), resident((d, dff)), resident((dff, d))],
        out_specs=pl.BlockSpec((tm, d), lambda i: (i, 0)),
        out_shape=jax.ShapeDtypeStruct((t, d), F32),
        scratch_shapes=[pltpu.VMEM((tm, dff), BF16)],
        compiler_params=pltpu.CompilerParams(dimension_semantics=("parallel",), vmem_limit_bytes=VMEM_LIMIT),
        name="ffn",
    )(x2, g, wg, wu, wd)


def _permute_w_in_kernel(w_ref, o_ref):
    rows = w_ref.shape[0]
    n_rest = P_COLS - COL_CVB
    o_ref[:, 0:COL_GATE] = w_ref[:, 0:COL_GATE].astype(BF16)
    lane = lax.broadcasted_iota(jnp.int32, (rows, LANES), 1)
    o_ref[:, COL_GATE:COL_GATE + LANES] = jnp.where(lane < N_GATES, w_ref[:, COL_GATE:COL_GATE + LANES], 0.0).astype(BF16)
    o_ref[:, COL_CVB:P_COLS] = w_ref[:, COL_GATE + N_GATES:COL_GATE + N_GATES + n_rest].astype(BF16)


def _permute_w_in(w_in):
    d, cols = w_in.shape
    tr = 256
    return pl.pallas_call(
        _permute_w_in_kernel,
        grid=(d // tr,),
        in_specs=[pl.BlockSpec((tr, cols), lambda i: (i, 0))],
        out_specs=pl.BlockSpec((tr, P_COLS), lambda i: (i, 0)),
        out_shape=jax.ShapeDtypeStruct((d, P_COLS), BF16),
        compiler_params=pltpu.CompilerParams(dimension_semantics=("parallel",), vmem_limit_bytes=VMEM_LIMIT),
        name="permute_w_in",
    )(w_in)


def _layer(x2, batch, seq, g_mix, w_in, b_gate, g_q_nsa, g_k_cmp, g_k_slc, g_k_win, pe_k_cmp, pe_v_cmp,
           w1_k_cmp, w2_k_cmp, w1_v_cmp, w2_v_cmp, conv_w, g_q_dil, g_k_dil, g_out, w_out,
           g_ffn, w_gate, w_up, w_down):
    row = lambda v: v.reshape(1, -1)
    per_group = lambda g, n: jnp.tile(g, n).reshape(1, -1)
    p2 = _inproj(x2, row(g_mix), _permute_w_in(w_in))

    pe, w1, w2 = _compress_weights(pe_k_cmp, pe_v_cmp, w1_k_cmp, w2_k_cmp, w1_v_cmp, w2_v_cmp)
    kc, vc = _compress(p2, pe, w1, w2, per_group(g_k_cmp, 2), batch, seq)
    ocmp, oslc, owin = _nsa(p2, kc, vc, per_group(g_q_nsa, N_HEADS), per_group(g_k_slc, 2), per_group(g_k_win, 2),
                            batch, seq)
    od = _dilated(p2, per_group(g_q_dil, 2), per_group(g_k_dil, 2), batch, seq)
    oc = _stick(p2, batch, seq)

    bg = jnp.zeros((1, LANES), F32).at[0, :N_GATES].set(b_gate)
    cwp = jnp.zeros((8, GROUP_W), F32).at[:CONV_K].set(conv_w)
    x1 = _mixout(x2, p2, ocmp, oslc, owin, oc, od, bg, cwp, row(g_out), w_out.astype(BF16), seq)
    return _ffn(x1, row(g_ffn), w_gate.astype(BF16), w_up.astype(BF16), w_down.astype(BF16))


def kernel(x, g_mix, w_in, b_gate, g_q_nsa, g_k_cmp, g_k_slc, g_k_win, pe_k_cmp, pe_v_cmp, w1_k_cmp, w2_k_cmp,
           w1_v_cmp, w2_v_cmp, conv_w, g_q_dil, g_k_dil, g_out, w_out, g_ffn, w_gate, w_up, w_down):
    batch, seq, d = x.shape
    assert seq % (CMP_STRIDE * LANES) == 0 and d % LANES == 0
    assert seq <= 64 * 256
    x2 = x.reshape(batch * seq, d)
    params = (g_mix, w_in, b_gate, g_q_nsa, g_k_cmp, g_k_slc, g_k_win, pe_k_cmp, pe_v_cmp, w1_k_cmp, w2_k_cmp,
              w1_v_cmp, w2_v_cmp, conv_w, g_q_dil, g_k_dil, g_out, w_out, g_ffn, w_gate, w_up, w_down)
    for layer in range(g_mix.shape[0]):
        x2 = _layer(x2, batch, seq, *[p[layer] for p in params])
    return x2.reshape(batch, seq, d)
```

```python
import functools

import numpy as np
import jax
import jax.numpy as jnp
from jax import lax
from jax.experimental import pallas as pl
from jax.experimental.pallas import tpu as pltpu

F32 = jnp.float32
BF16 = jnp.bfloat16

HEAD_DIM = 64
N_HEADS = 4
GROUP_W = N_HEADS * HEAD_DIM
CONV_K = 3
CMP_LEN = 32
CMP_STRIDE = 16
SEL_BLOCK = 64
N_SELECT = 16
NSA_WINDOW = 512
DILATED_CONFIGS = ((128, 1), (512, 4), (2048, 16))
DILATED_WINDOW = max(w for w, _ in DILATED_CONFIGS)
FOLD = 16
FAR_FROM = 512
FAR_ROWS = 128
NEG_INF = -1e30
FORCE_SCORE = 1e6
RMS_EPS = 1e-6

TQ = 512
TK = 256
N_TOP = TQ // TK
CLS_PER_GROUP = 128 * FOLD // TQ
STICK_TQ = 256
LANES = 128
MXU_DEPTH = 256
MIX_CHAIN = 256
MXU_AHEAD = 2
FLASH_AHEAD = 2
SUM_PAD = 16
V_ROWS = HEAD_DIM + SUM_PAD
LOG2E = 1.4426950408889634
VMEM_LIMIT = 52 * 1024 * 1024

COL_QA, COL_KVC, COL_KSV, COL_KWV, COL_GATE = 0, 256, 384, 512, 640
COL_CVB, COL_CVC, COL_CVU = 768, 1024, 1280
COL_QC, COL_KC, COL_VC = 1536, 1792, 2048
COL_QD, COL_KD, COL_VD = 2304, 2560, 2816
P_COLS = 3072
N_GATES = 12

_NT = (((1,), (1,)), ((), ()))


def _alibi_slopes():
    s = [2.0 ** (-8.0 * i / 8) for i in range(1, 9)]
    return tuple(s[0::2]), tuple(s[1::2])


SLOPES_A, SLOPES_D = _alibi_slopes()


def _rms(x, g):
    return x * lax.rsqrt(jnp.mean(x * x, axis=-1, keepdims=True) + RMS_EPS) * g


def _exact_dot(a, sel):
    hi = a.astype(BF16)
    lo = (a - hi.astype(F32)).astype(BF16)
    if 2 * a.shape[1] <= MXU_DEPTH:
        return jnp.dot(jnp.concatenate([hi, lo], axis=1), jnp.concatenate([sel, sel], axis=0),
                       preferred_element_type=F32)
    return jnp.dot(hi, sel, preferred_element_type=F32) + jnp.dot(lo, sel, preferred_element_type=F32)


def _group_rms(x, g):
    w = x.shape[1]
    same = ((lax.broadcasted_iota(jnp.int32, (w, w), 0) >> 6)
            == (lax.broadcasted_iota(jnp.int32, (w, w), 1) >> 6)).astype(BF16)
    ssq = _exact_dot(x * x, same)
    return x * lax.rsqrt(ssq * (1.0 / HEAD_DIM) + RMS_EPS) * g


EVEN_AUG, ODD_AUG = HEAD_DIM, 0


def _key_aug(pos, first):
    lane = lax.broadcasted_iota(jnp.int32, (pos.shape[0], LANES), 1) - first
    hi = (pos >> 6).astype(F32)
    lo = (pos & 63).astype(F32)
    return jnp.where(lane == 0, hi, jnp.where(lane == 1, lo, jnp.where((lane == 2) | (lane == 3), 1.0, 0.0)))


def _query_aug(t, slope, first):
    lane = lax.broadcasted_iota(jnp.int32, (t.shape[0], LANES), 1) - first
    hi = (t >> 6).astype(F32) * (-64.0 * slope)
    lo = (t & 63).astype(F32) * (-slope)
    return jnp.where(lane == 0, 64.0 * slope,
                     jnp.where(lane == 1, slope, jnp.where(lane == 2, hi, jnp.where(lane == 3, lo, 0.0))))


def _pair_keys(pair, pos):
    lane = lax.broadcasted_iota(jnp.int32, pair.shape, 1)
    aug_e = 0.0 if pos is None else _key_aug(pos, EVEN_AUG)
    aug_o = 0.0 if pos is None else _key_aug(pos, ODD_AUG)
    return (jnp.where(lane < HEAD_DIM, pair, aug_e).astype(BF16),
            jnp.where(lane >= HEAD_DIM, pair, aug_o).astype(BF16))


def _prep_queries(q, g, t0, slopes, qp_s):
    qn = (q if g is None else _group_rms(q, g)) * (HEAD_DIM ** -0.5)
    tpos = t0 + lax.broadcasted_iota(jnp.int32, (q.shape[0], 1), 0)
    lane = lax.broadcasted_iota(jnp.int32, (q.shape[0], LANES), 1)
    for i in range(N_HEADS // 2):
        pair = qn[:, i * LANES:(i + 1) * LANES]
        if slopes is None:
            qp_s[i] = pair.astype(BF16)
            continue
        even = jnp.where((lane >= EVEN_AUG) & (lane < EVEN_AUG + 4), _query_aug(tpos, slopes[2 * i], EVEN_AUG), pair)
        odd = jnp.where(lane < ODD_AUG + 4, _query_aug(tpos, slopes[2 * i + 1], ODD_AUG), pair)
        qp_s[2 * i] = even.astype(BF16)
        qp_s[2 * i + 1] = odd.astype(BF16)


def _inproj_kernel(x_ref, g_ref, w_ref, o_ref, *, nchunk):
    h = _rms(x_ref[...], g_ref[...]).astype(BF16)
    cw = P_COLS // nchunk
    for c in range(nchunk):
        o_ref[:, c * cw:(c + 1) * cw] = jnp.dot(h, w_ref[:, c * cw:(c + 1) * cw], preferred_element_type=F32)


def _inproj(x2, g, w):
    t, d = x2.shape
    tm = 1024
    return pl.pallas_call(
        functools.partial(_inproj_kernel, nchunk=6),
        grid=(t // tm,),
        in_specs=[pl.BlockSpec((tm, d), lambda i: (i, 0)),
                  pl.BlockSpec((1, d), lambda i: (0, 0)),
                  pl.BlockSpec((d, P_COLS), lambda i: (0, 0), pipeline_mode=pl.Buffered(1))],
        out_specs=pl.BlockSpec((tm, P_COLS), lambda i: (i, 0)),
        out_shape=jax.ShapeDtypeStruct((t, P_COLS), F32),
        compiler_params=pltpu.CompilerParams(dimension_semantics=("parallel",), vmem_limit_bytes=VMEM_LIMIT),
        name="inproj",
    )(x2, g, w)


def _gelu_tanh(x):
    return x * (0.5 * (1.0 + jnp.tanh(np.sqrt(2.0 / np.pi).astype(np.float32) * (x + 0.044715 * (x * x * x)))))


def _compress_kernel(kv_ref, pe_ref, w1_ref, w2_ref, gk_ref, kc_ref, vct_ref):
    nc = kv_ref.shape[0] // CMP_STRIDE
    first = None
    second = None
    for j in range(CMP_STRIDE):
        tok = kv_ref[pl.ds(j, nc, stride=CMP_STRIDE), :]
        a = jnp.dot((tok + pe_ref[j:j + 1, :]).astype(BF16), w1_ref[j], preferred_element_type=F32)
        b = jnp.dot((tok + pe_ref[CMP_STRIDE + j:CMP_STRIDE + j + 1, :]).astype(BF16), w1_ref[CMP_STRIDE + j],
                    preferred_element_type=F32)
        first = a if first is None else first + a
        second = b if second is None else second + b
    hid = first + pltpu.roll(second, nc - 1, 0)
    out = jnp.dot(_gelu_tanh(hid).astype(BF16), w2_ref[...], preferred_element_type=F32)
    kn = _group_rms(out, gk_ref[...])
    end = lax.broadcasted_iota(jnp.int32, (nc, 1), 0) * CMP_STRIDE + (CMP_LEN - 1)
    even, _ = _pair_keys(kn, end)
    _, odd = _pair_keys(pltpu.roll(kn, HEAD_DIM, 1), end)
    kc_ref[0] = jnp.concatenate([even, odd], axis=1)
    vct_ref[0] = out.T[HEAD_DIM:, :].astype(BF16)


def _compress(p2, pe, w1, w2, gk, batch, seq):
    nc = seq // CMP_STRIDE
    hid2 = w1.shape[2]
    return pl.pallas_call(
        _compress_kernel,
        grid=(batch,),
        in_specs=[pl.BlockSpec((seq, LANES), lambda i: (i, COL_KVC // LANES)),
                  pl.BlockSpec((CMP_LEN, LANES), lambda i: (0, 0)),
                  pl.BlockSpec((CMP_LEN, LANES, hid2), lambda i: (0, 0, 0)),
                  pl.BlockSpec((hid2, LANES), lambda i: (0, 0)),
                  pl.BlockSpec((1, LANES), lambda i: (0, 0))],
        out_specs=[pl.BlockSpec((1, nc, 2 * LANES), lambda i: (i, 0, 0)),
                   pl.BlockSpec((1, HEAD_DIM, nc), lambda i: (i, 0, 0))],
        out_shape=[jax.ShapeDtypeStruct((batch, nc, 2 * LANES), BF16),
                   jax.ShapeDtypeStruct((batch, HEAD_DIM, nc), BF16)],
        compiler_params=pltpu.CompilerParams(dimension_semantics=("parallel",), vmem_limit_bytes=VMEM_LIMIT),
        name="nsa_compress",
    )(p2, pe, w1, w2, gk)


def _compress_weights(pe_k, pe_v, w1_k, w2_k, w1_v, w2_v):
    hid = w1_k.shape[1]
    w1k = w1_k.reshape(CMP_LEN, HEAD_DIM, hid)
    w1v = w1_v.reshape(CMP_LEN, HEAD_DIM, hid)
    z1 = jnp.zeros_like(w1k)
    w1 = jnp.concatenate([jnp.concatenate([w1k, z1], axis=2), jnp.concatenate([z1, w1v], axis=2)], axis=1)
    z2 = jnp.zeros_like(w2_k)
    w2 = jnp.concatenate([jnp.concatenate([w2_k, z2], axis=1), jnp.concatenate([z2, w2_v], axis=1)], axis=0)
    pe = jnp.concatenate([pe_k, pe_v], axis=1)
    return pe, w1.astype(BF16), w2.astype(BF16)


ALL_LANES = slice(0, TQ)


def _with_lanes(full, lanes, part):
    pieces = ([full[:, :lanes.start]] if lanes.start > 0 else []) + [part] + \
             ([full[:, lanes.stop:]] if lanes.stop < TQ else [])
    return jnp.concatenate(pieces, axis=1) if len(pieces) > 1 else part


def _flash_scores(kp_of, bias, qp_s, s_s, m_cur, lanes=ALL_LANES):
    m_next = []
    for h in range(N_HEADS):
        s = lax.dot_general(kp_of(h), qp_s[h, lanes, :], _NT, preferred_element_type=F32) + bias[:, lanes]
        s_s[h, :, lanes] = s
        m_next.append(_with_lanes(m_cur[h], lanes, jnp.maximum(m_cur[h][:, lanes], jnp.max(s, axis=0, keepdims=True))))
    return tuple(m_next)


def _flash_accum(vt_of, s_s, acc_s, m_prev, m_cur, lanes=ALL_LANES):
    for h in range(N_HEADS):
        alpha = jnp.exp(m_prev[h][:, lanes] - m_cur[h][:, lanes])
        p = jnp.exp(s_s[h, :, lanes] - m_cur[h][:, lanes])
        acc_s[h, :, lanes] = alpha * acc_s[h, :, lanes] + jnp.dot(vt_of(h), p.astype(BF16),
                                                                 preferred_element_type=F32)


def _flash_accum_and_scores(kp_of, bias, vt_of, qp_s, s_s, acc_s, m_prev, m_cur, acc_lanes=ALL_LANES,
                            new_lanes=ALL_LANES):
    def scores(h):
        return (lax.dot_general(kp_of(h), qp_s[h, new_lanes, :], _NT, preferred_element_type=F32)
                + bias[:, new_lanes])

    m_next = []
    s_new = {h: scores(h) for h in range(FLASH_AHEAD)}
    for h in range(N_HEADS):
        alpha = jnp.exp(m_prev[h][:, acc_lanes] - m_cur[h][:, acc_lanes])
        p = jnp.exp(s_s[h, :, acc_lanes] - m_cur[h][:, acc_lanes])
        acc_s[h, :, acc_lanes] = alpha * acc_s[h, :, acc_lanes] + jnp.dot(vt_of(h), p.astype(BF16),
                                                                         preferred_element_type=F32)
        s_h = s_new.pop(h)
        s_s[h, :, new_lanes] = s_h
        m_next.append(_with_lanes(m_cur[h], new_lanes,
                                  jnp.maximum(m_cur[h][:, new_lanes], jnp.max(s_h, axis=0, keepdims=True))))
        if h + FLASH_AHEAD < N_HEADS:
            s_new[h + FLASH_AHEAD] = scores(h + FLASH_AHEAD)
    return tuple(m_next)


def _flash_pipelined(tile_of, n_tiles, kp_fn, bias_fn, vt_fn, qp_s, s_s, acc_s, o_ref, causal_lanes=True):
    top_lanes = [slice((N_TOP - 1 - i) * TK if causal_lanes else 0, TQ) for i in range(N_TOP)]
    acc_s[...] = jnp.zeros(acc_s.shape, F32)
    m0 = tuple(jnp.full((1, TQ), NEG_INF, F32) for _ in range(N_HEADS))
    first_kt = tile_of(0)
    m1 = _flash_scores(kp_fn(first_kt), bias_fn(first_kt, True), qp_s, s_s, m0, top_lanes[0])

    def step(i, carry, top):
        m_prev, m_cur = carry
        nxt = tile_of(i + 1)
        lanes = (top_lanes[i], top_lanes[i + 1]) if top else (ALL_LANES, ALL_LANES)
        m_next = _flash_accum_and_scores(kp_fn(nxt), bias_fn(nxt, top), vt_fn(tile_of(i)), qp_s, s_s, acc_s,
                                         m_prev, m_cur, *lanes)
        return m_cur, m_next

    carry = (m0, m1)
    for i in range(N_TOP - 1):
        carry = step(i, carry, True)
    m_prev, m_cur = lax.fori_loop(N_TOP - 1, n_tiles - 1, lambda i, c: step(i, c, False), carry)
    _flash_accum(vt_fn(tile_of(n_tiles - 1)), s_s, acc_s, m_prev, m_cur)
    if o_ref is None:
        return m_cur
    _flash_write(acc_s, o_ref)


def _flash_write(acc_s, o_ref):
    o_ref[0] = jnp.concatenate([acc_s[h, 0:HEAD_DIM, :] * (1.0 / acc_s[h, HEAD_DIM:HEAD_DIM + 1, :])
                                for h in range(N_HEADS)], axis=0)


def _values_t(v_t):
    row = lax.broadcasted_iota(jnp.int32, (V_ROWS - HEAD_DIM, v_t.shape[1]), 0)
    return jnp.concatenate([v_t, jnp.where(row == 0, 1.0, 0.0)], axis=0).astype(BF16)


def _nsa_kernel(q_ref, kc_ref, vct_ref, ksv_ref, kwv_ref, gq_ref, gks_ref, gkw_ref, wbias_ref,
                ocmp_ref, oslc_ref, owin_ref,
                ksp_s, vst_s, kwp_s, vwt_s, qp_s, imp_s, sel_s, s_s, acc_s, hit_s, *, seq):
    qi = pl.program_id(1)
    nc = seq // CMP_STRIDE
    nsel = seq // SEL_BLOCK
    nkt = seq // TK

    @pl.when(qi == 0)
    def _prep():
        for c in range(nkt):
            rows = slice(c * TK, (c + 1) * TK)
            pos = c * TK + lax.broadcasted_iota(jnp.int32, (TK, 1), 0)
            for kv_ref, g_ref, kp_s, vt_s in ((ksv_ref, gks_ref, ksp_s, vst_s), (kwv_ref, gkw_ref, kwp_s, vwt_s)):
                blk = kv_ref[rows, :]
                kn = _group_rms(blk, g_ref[...])
                kp_s[rows, 0:LANES], _ = _pair_keys(kn, pos)
                _, kp_s[rows, LANES:2 * LANES] = _pair_keys(pltpu.roll(kn, HEAD_DIM, 1), pos)
                vt_s[c] = _values_t(blk.T[HEAD_DIM:, :])
        imp_s[:, 0:8, :] = jnp.zeros((TQ // LANES, 8, LANES), F32)

    t0 = qi * TQ
    _prep_queries(q_ref[...], gq_ref[...], t0, SLOPES_A, qp_s)

    kc = [kc_ref[0, :, 0:LANES], kc_ref[0, :, LANES:2 * LANES]]
    vct = vct_ref[0]
    tq_row = t0 + lax.broadcasted_iota(jnp.int32, (nc, TQ), 1)
    n_idx = lax.broadcasted_iota(jnp.int32, (nc, TQ), 0)
    vis = (tq_row >= n_idx * CMP_STRIDE + (CMP_LEN - 1)) & (n_idx < nc - 1)
    vis_bias = jnp.where(vis, 0.0, NEG_INF)
    sees_any = (t0 + lax.broadcasted_iota(jnp.int32, (1, TQ), 1) >= CMP_LEN - 1).astype(F32)
    imp = jnp.zeros((nc, TQ), F32)
    ocmp_t = []
    scores = [lax.dot_general(kc[h % 2], qp_s[h], _NT, preferred_element_type=F32) + vis_bias
              for h in range(N_HEADS)]
    for h in range(N_HEADS):
        sc = scores[h]
        e = jnp.exp(sc - jnp.max(sc, axis=0, keepdims=True))
        p = e * (sees_any / jnp.sum(e, axis=0, keepdims=True))
        ocmp_t.append(jnp.dot(vct, p.astype(BF16), preferred_element_type=F32))
        imp = imp + p
    ocmp_ref[0] = jnp.concatenate(ocmp_t, axis=0)

    halves = []
    for half in range(TQ // LANES):
        imp_s[half, 8:8 + nc, :] = imp[:, half * LANES:(half + 1) * LANES]
        r = [imp_s[half, pl.ds(8 + k, nsel, stride=4), :] for k in range(4)]
        rm1 = imp_s[half, pl.ds(7, nsel, stride=4), :]
        halves.append(rm1 + 2.0 * (r[0] + r[1] + r[2]) + r[3])
    imp_blk = jnp.concatenate(halves, axis=1)
    blk = lax.broadcasted_iota(jnp.int32, (nsel, TQ), 0)
    tl = t0 + lax.broadcasted_iota(jnp.int32, (nsel, TQ), 1)
    cur = tl >> 6
    forced = (blk == 0) | (blk == cur) | (blk == cur - 1)
    valid = blk * SEL_BLOCK <= tl
    score = jnp.where(forced, FORCE_SCORE, jnp.where(valid, imp_blk, -FORCE_SCORE))
    sub = 8
    per_tile = TK // SEL_BLOCK
    tile_hit = []
    for g in range(nsel // sub):
        mine = score[g * sub:(g + 1) * sub, :]
        blk_g = g * sub + lax.broadcasted_iota(jnp.int32, (sub, TQ), 0)
        rank = jnp.zeros((sub, TQ), F32)
        for j in range(nsel):
            row = score[j:j + 1, :]
            if j < g * sub:
                beats = row >= mine
            elif j >= (g + 1) * sub:
                beats = row > mine
            else:
                beats = (row > mine) | ((row == mine) & (blk_g > j))
            rank = rank + jnp.where(beats, 1.0, 0.0)
        sel_g = jnp.where(rank < min(N_SELECT, nsel), 0.0, NEG_INF)
        sel_s[g * sub:(g + 1) * sub, :] = sel_g
        for part in range(sub // per_tile):
            tile_hit.append(jnp.max(sel_g[part * per_tile:(part + 1) * per_tile, :]))

    n_hit = jnp.int32(0)
    for kt in reversed(range(nkt)):
        hit_s[n_hit] = kt
        n_hit = n_hit + ((tile_hit[kt] > -1.0) & (kt * TK < t0)).astype(jnp.int32)

    def block_bias(kt):
        return jnp.concatenate([jnp.broadcast_to(sel_s[pl.ds(kt * per_tile + j, 1), :], (SEL_BLOCK, TQ))
                                for j in range(per_tile)], axis=0)

    k_local = lax.broadcasted_iota(jnp.int32, (TK, TQ), 0)
    q_local = lax.broadcasted_iota(jnp.int32, (TK, TQ), 1)
    top_kt = qi * N_TOP + (N_TOP - 1)

    def kp_fn(kt):
        off = pl.multiple_of(kt * TK, TK)
        return lambda h: ksp_s[pl.ds(off, TK), (h % 2) * LANES:(h % 2 + 1) * LANES]

    def bias_fn(kt, top):
        if top:
            return jnp.where(kt * TK - t0 + k_local <= q_local, block_bias(kt), NEG_INF)
        return block_bias(kt)

    tile_of = lambda i: jnp.where(i < N_TOP, top_kt - i, hit_s[jnp.maximum(i - N_TOP, 0)])
    _flash_pipelined(tile_of, n_hit + N_TOP, kp_fn, bias_fn, lambda kt: (lambda h: vst_s[kt]), qp_s, s_s, acc_s,
                     oslc_ref)

    def window_kp_fn(kt):
        off = pl.multiple_of(kt * TK, TK)
        return lambda h: kwp_s[pl.ds(off, TK), (h % 2) * LANES:(h % 2 + 1) * LANES]

    kt_lo = jnp.maximum(t0 - (NSA_WINDOW - 1), 0) // TK
    _flash_pipelined(lambda i: top_kt - i, top_kt + 1 - kt_lo, window_kp_fn, lambda kt, top: wbias_ref[top_kt - kt],
                     lambda kt: (lambda h: vwt_s[kt]), qp_s, s_s, acc_s, owin_ref)


def _nsa(p2, kc, vc, gq, gks, gkw, batch, seq):
    nq = seq // TQ
    nc = seq // CMP_STRIDE
    t = batch * seq
    wbias = jnp.asarray(_band_bias_tiles(NSA_WINDOW - 1, _window_mult))
    head_tiles = pl.BlockSpec((1, GROUP_W, TQ), lambda b, i: (b * nq + i, 0, 0))
    head_shape = jax.ShapeDtypeStruct((t // TQ, GROUP_W, TQ), F32)
    return pl.pallas_call(
        functools.partial(_nsa_kernel, seq=seq),
        grid=(batch, nq),
        in_specs=[pl.BlockSpec((TQ, GROUP_W), lambda b, i: (b * nq + i, COL_QA // GROUP_W)),
                  pl.BlockSpec((1, nc, 2 * LANES), lambda b, i: (b, 0, 0)),
                  pl.BlockSpec((1, HEAD_DIM, nc), lambda b, i: (b, 0, 0)),
                  pl.BlockSpec((seq, LANES), lambda b, i: (b, COL_KSV // LANES)),
                  pl.BlockSpec((seq, LANES), lambda b, i: (b, COL_KWV // LANES)),
                  pl.BlockSpec((1, GROUP_W), lambda b, i: (0, 0)),
                  pl.BlockSpec((1, LANES), lambda b, i: (0, 0)),
                  pl.BlockSpec((1, LANES), lambda b, i: (0, 0)),
                  pl.BlockSpec(wbias.shape, lambda b, i: (0, 0, 0))],
        out_specs=[head_tiles, head_tiles, head_tiles],
        out_shape=[head_shape, head_shape, head_shape],
        scratch_shapes=[pltpu.VMEM((seq, 2 * LANES), BF16),
                        pltpu.VMEM((seq // TK, V_ROWS, TK), BF16),
                        pltpu.VMEM((seq, 2 * LANES), BF16),
                        pltpu.VMEM((seq // TK, V_ROWS, TK), BF16),
                        pltpu.VMEM((N_HEADS, TQ, LANES), BF16),
                        pltpu.VMEM((TQ // LANES, 8 + nc, LANES), F32),
                        pltpu.VMEM((seq // SEL_BLOCK, TQ), F32),
                        pltpu.VMEM((N_HEADS, TK, TQ), F32),
                        pltpu.VMEM((N_HEADS, V_ROWS, TQ), F32),
                        pltpu.SMEM((seq // TK + 1,), jnp.int32)],
        compiler_params=pltpu.CompilerParams(dimension_semantics=("parallel", "arbitrary"),
                                             vmem_limit_bytes=VMEM_LIMIT),
        name="nsa",
    )(p2, kc, vc, p2, p2, gq, gks, gkw, wbias)


def _band_bias_tiles(window, mult_fn, folded=False):
    nd = (window + TQ - 1) // TK + 1
    kk = np.arange(TK)[:, None]
    qq = np.arange(TQ)[None, :]
    if folded:
        qq = FOLD * (qq % (TQ // FOLD)) + qq // (TQ // FOLD)
    tiles = np.empty((nd, TK, TQ), np.float32)
    for di in range(nd):
        d = (di - (N_TOP - 1)) * TK + qq - kk
        mult = mult_fn(d)
        tiles[di] = np.where(mult > 0, np.log(np.maximum(mult, 1)), NEG_INF)
    return tiles


def _window_mult(d):
    return ((d >= 0) & (d <= NSA_WINDOW - 1)).astype(np.float64)


def _dilated_mult(d):
    m = np.zeros(d.shape, np.float64)
    for window, dil in DILATED_CONFIGS:
        m += ((d >= 0) & (d <= window) & (d % dil == 0)).astype(np.float64)
    return m


def _near_mult(d):
    return np.where(d <= FAR_FROM, _dilated_mult(d), 0.0)


def _dilated_kernel(q0_ref, q1_ref, k0_ref, k1_ref, v0_ref, v1_ref, gq_ref, gk_ref, bias_ref, o_ref,
                    kp_s, vt_s, kf_s, vf_s, qp_s, qcat_s, s_s, acc_s, m_s, *, seq):
    qi = pl.program_id(1)
    per = TQ // FOLD
    cls_rows = seq // FOLD
    pairs = ((q0_ref, k0_ref, v0_ref), (q1_ref, k1_ref, v1_ref))

    @pl.when(qi == 0)
    def _prep():
        for p, (_, k_ref, v_ref) in enumerate(pairs):
            for c in range(seq // TK):
                rows = slice(c * TK, (c + 1) * TK)
                pos = c * TK + lax.broadcasted_iota(jnp.int32, (TK, 1), 0)
                even, odd = _pair_keys(_group_rms(k_ref[rows, :], gk_ref[...]), pos)
                kp_s[rows, 2 * p * LANES:(2 * p + 1) * LANES] = even
                kp_s[rows, (2 * p + 1) * LANES:(2 * p + 2) * LANES] = odd
                v_t = v_ref[rows, :].T
                for j in range(2):
                    vt_s[c, (2 * p + j) * V_ROWS:(2 * p + j + 1) * V_ROWS, :] = _values_t(
                        v_t[j * HEAD_DIM:(j + 1) * HEAD_DIM, :])
            for r in range(FOLD):
                pos = r + FOLD * lax.broadcasted_iota(jnp.int32, (cls_rows, 1), 0)
                even, odd = _pair_keys(_group_rms(k_ref[pl.ds(r, cls_rows, stride=FOLD), :], gk_ref[...]), pos)
                kf_s[2 * p, :, r * LANES:(r + 1) * LANES] = even
                kf_s[2 * p + 1, :, r * LANES:(r + 1) * LANES] = odd
                v_t = v_ref[pl.ds(r, cls_rows, stride=FOLD), :].T
                for j in range(2):
                    vf_s[2 * p + j, :, r * cls_rows:(r + 1) * cls_rows] = _values_t(
                        v_t[j * HEAD_DIM:(j + 1) * HEAD_DIM, :])

    t0 = qi * TQ
    u = lax.broadcasted_iota(jnp.int32, (TQ, 1), 0)
    tpos = t0 + FOLD * (u % per) + u // per
    lane = lax.broadcasted_iota(jnp.int32, (TQ, LANES), 1)
    for p, (q_ref, _, _) in enumerate(pairs):
        q = jnp.concatenate([q_ref[pl.ds(r, per, stride=FOLD), :] for r in range(FOLD)], axis=0)
        pair = _group_rms(q, gq_ref[...]) * (HEAD_DIM ** -0.5)
        even = jnp.where((lane >= EVEN_AUG) & (lane < EVEN_AUG + 4), _query_aug(tpos, SLOPES_D[2 * p], EVEN_AUG), pair)
        odd = jnp.where(lane < ODD_AUG + 4, _query_aug(tpos, SLOPES_D[2 * p + 1], ODD_AUG), pair)
        qp_s[2 * p] = even.astype(BF16)
        qp_s[2 * p + 1] = odd.astype(BF16)

    kt_lo = jnp.maximum(t0 - FAR_FROM, 0) // TK
    top_kt = qi * N_TOP + (N_TOP - 1)

    def kp_fn(kt):
        off = pl.multiple_of(kt * TK, TK)
        return lambda h: kp_s[pl.ds(off, TK), h * LANES:(h + 1) * LANES]

    m = _flash_pipelined(lambda i: top_kt - i, top_kt + 1 - kt_lo, kp_fn, lambda kt, top: bias_ref[top_kt - kt],
                         lambda kt: (lambda h: vt_s[kt, h * V_ROWS:(h + 1) * V_ROWS, :]), qp_s, s_s, acc_s, None,
                         causal_lanes=False)
    for h in range(N_HEADS):
        m_s[h] = m[h]

    groups = TQ // LANES
    cls_of_row = lax.broadcasted_iota(jnp.int32, (LANES, LANES), 0) // per
    cls_of_lane = lax.broadcasted_iota(jnp.int32, (FAR_ROWS, LANES), 1) // per
    for h in range(N_HEADS):
        for g in range(groups):
            rows = qp_s[h, g * LANES:(g + 1) * LANES, :]
            qcat_s[h, g] = jnp.concatenate([jnp.where(cls_of_row == j, rows, jnp.zeros_like(rows))
                                            for j in range(CLS_PER_GROUP)], axis=1)

    a0 = t0 // FOLD
    a_lane = a0 + lax.broadcasted_iota(jnp.int32, (FAR_ROWS, TQ), 1) % per
    row = lax.broadcasted_iota(jnp.int32, (FAR_ROWS, TQ), 0)

    def far_block(blk):
        back = a_lane - (blk * FAR_ROWS + row)
        bias = jnp.where((back > FAR_FROM // FOLD) & (back <= DILATED_WINDOW // FOLD), 0.0, NEG_INF)
        keys = slice(blk * FAR_ROWS, (blk + 1) * FAR_ROWS)
        scores = [jnp.concatenate(
            [lax.dot_general(kf_s[h, keys, g * CLS_PER_GROUP * LANES:(g + 1) * CLS_PER_GROUP * LANES], qcat_s[h, g],
                             _NT, preferred_element_type=F32) for g in range(groups)], axis=1) + bias
            for h in range(N_HEADS)]
        for h in range(N_HEADS):
            m_old = m_s[h]
            m_new = jnp.maximum(m_old, jnp.max(scores[h], axis=0, keepdims=True))
            p = jnp.exp(scores[h] - m_new)
            pv = []
            for g in range(groups):
                pg = p[:, g * LANES:(g + 1) * LANES]
                rhs = jnp.concatenate([jnp.where(cls_of_lane == j, pg, 0.0) for j in range(CLS_PER_GROUP)],
                                      axis=0).astype(BF16)
                lhs = jnp.concatenate(
                    [vf_s[h, :, (g * CLS_PER_GROUP + j) * cls_rows + blk * FAR_ROWS:
                           (g * CLS_PER_GROUP + j) * cls_rows + (blk + 1) * FAR_ROWS] for j in range(CLS_PER_GROUP)],
                    axis=1)
                pv.append(jnp.dot(lhs, rhs, preferred_element_type=F32))
            acc_s[h] = jnp.exp(m_old - m_new) * acc_s[h] + jnp.concatenate(pv, axis=1)
            m_s[h] = m_new

    for blk in range(cls_rows // FAR_ROWS):
        needed = (a0 + per - 1 - blk * FAR_ROWS > FAR_FROM // FOLD) & \
                 (a0 - (blk * FAR_ROWS + FAR_ROWS - 1) <= DILATED_WINDOW // FOLD)
        pl.when(needed)(functools.partial(far_block, blk))
    _flash_write(acc_s, o_ref)


def _dilated(p2, gq, gk, batch, seq):
    nq = seq // TQ
    t = batch * seq
    assert FAR_FROM % FOLD == 0 and TQ % FOLD == 0 and (seq // FOLD) % FAR_ROWS == 0 and (TQ // FOLD) * CLS_PER_GROUP == LANES
    bias_tiles = jnp.asarray(_band_bias_tiles(FAR_FROM, _near_mult, folded=True))
    pair_cols = lambda rows, col: [pl.BlockSpec((rows, LANES), functools.partial(
        lambda b, i, c: (b if rows == seq else b * nq + i, c), c=col // LANES + j)) for j in range(2)]
    gain = pl.BlockSpec((1, LANES), lambda b, i: (0, 0))
    return pl.pallas_call(
        functools.partial(_dilated_kernel, seq=seq),
        grid=(batch, nq),
        in_specs=pair_cols(TQ, COL_QD) + pair_cols(seq, COL_KD) + pair_cols(seq, COL_VD)
                 + [gain, gain, pl.BlockSpec(bias_tiles.shape, lambda b, i: (0, 0, 0))],
        out_specs=pl.BlockSpec((1, GROUP_W, TQ), lambda b, i: (b * nq + i, 0, 0)),
        out_shape=jax.ShapeDtypeStruct((t // TQ, GROUP_W, TQ), F32),
        scratch_shapes=[pltpu.VMEM((seq, N_HEADS * LANES), BF16),
                        pltpu.VMEM((seq // TK, N_HEADS * V_ROWS, TK), BF16),
                        pltpu.VMEM((N_HEADS, seq // FOLD, FOLD * LANES), BF16),
                        pltpu.VMEM((N_HEADS, V_ROWS, seq), BF16),
                        pltpu.VMEM((N_HEADS, TQ, LANES), BF16),
                        pltpu.VMEM((N_HEADS, TQ // LANES, LANES, CLS_PER_GROUP * LANES), BF16),
                        pltpu.VMEM((N_HEADS, TK, TQ), F32),
                        pltpu.VMEM((N_HEADS, V_ROWS, TQ), F32),
                        pltpu.VMEM((N_HEADS, 1, TQ), F32)],
        compiler_params=pltpu.CompilerParams(dimension_semantics=("parallel", "arbitrary"),
                                             vmem_limit_bytes=VMEM_LIMIT),
        name="dilated",
    )(p2, p2, p2, p2, p2, p2, gq, gk, bias_tiles)


def _stick_kernel(q_ref, k_ref, v_ref, o_ref, kp_s, vt_s, qp_s, sp_s, e_s, acc_s, *, seq):
    TQ, N_TOP = STICK_TQ, STICK_TQ // TK
    qi = pl.program_id(1)
    nkt = seq // TK

    @pl.when(qi == 0)
    def _prep():
        for c in range(nkt):
            rows = slice(c * TK, (c + 1) * TK)
            kb = k_ref[rows, :]
            for i in range(N_HEADS // 2):
                even, odd = _pair_keys(kb[:, i * LANES:(i + 1) * LANES], None)
                kp_s[rows, 2 * i * LANES:(2 * i + 1) * LANES] = even
                kp_s[rows, (2 * i + 1) * LANES:(2 * i + 2) * LANES] = odd
            vt_s[c] = v_ref[rows, :].T.astype(BF16)

    _prep_queries(q_ref[...], None, qi * TQ, None, qp_s)
    half = TK // 2
    col = lax.broadcasted_iota(jnp.int32, (half + SUM_PAD, half), 1)
    srow = lax.broadcasted_iota(jnp.int32, (half + SUM_PAD, half), 0)
    sums = ((srow == half) | ((srow < half) & (col > srow))).astype(BF16)
    k_local = lax.broadcasted_iota(jnp.int32, (TK, TQ), 0)
    q_local = lax.broadcasted_iota(jnp.int32, (TK, TQ), 1)
    top_kt = qi * N_TOP + (N_TOP - 1)
    acc_s[...] = jnp.zeros(acc_s.shape, F32)

    def past_mask(tiles_above):
        return tiles_above * TK + k_local < q_local

    def logits(kt, h):
        off = pl.multiple_of(kt * TK, TK)
        return lax.dot_general(kp_s[pl.ds(off, TK), h * LANES:(h + 1) * LANES], qp_s[h // 2], _NT,
                               preferred_element_type=F32)

    def stage(h, z, past):
        sp = jnp.maximum(z, 0.0) + jnp.log(1.0 + jnp.exp2(jnp.abs(z) * (-LOG2E)))
        spm = sp if past is None else jnp.where(past, sp, 0.0)
        sp_s[h] = spm.astype(BF16)
        logsig = z - sp
        e_s[h] = logsig if past is None else jnp.where(past, logsig, NEG_INF)

    def tails(h):
        return [jnp.dot(sums, sp_s[h, b * half:(b + 1) * half, :], preferred_element_type=F32) for b in range(2)]

    def consume(kt, h, w, later):
        later_lo = later + w[1][half:half + 1, :]
        attn = jnp.concatenate([jnp.exp(e_s[h, 0:half, :] - w[0][0:half, :] - later_lo),
                                jnp.exp(e_s[h, half:TK, :] - w[1][0:half, :] - later)], axis=0)
        acc_s[h] += jnp.dot(vt_s[kt, h * HEAD_DIM:(h + 1) * HEAD_DIM, :], attn.astype(BF16),
                            preferred_element_type=F32)
        return later_lo + w[0][half:half + 1, :]

    ahead = MXU_AHEAD
    for h in range(N_HEADS):
        stage(h, logits(top_kt, h), past_mask(N_TOP - 1))

    def step(kt, carry, past):
        laters, w_first, z_first = carry
        w = {0: list(w_first)}
        z = {0: z_first}
        for h in range(1, ahead):
            w[h] = tails(h)
            z[h] = logits(kt - 1, h)
        out = []
        for h in range(N_HEADS):
            out.append(consume(kt, h, w.pop(h), laters[h]))
            stage(h, z.pop(h), past)
            if h + ahead < N_HEADS:
                w[h + ahead] = tails(h + ahead)
                z[h + ahead] = logits(kt - 1, h + ahead)
            elif h + ahead == N_HEADS:
                nxt = (tuple(tails(0)), logits(jnp.maximum(kt - 2, 0), 0))
        return tuple(out), nxt[0], nxt[1]

    carry = (tuple(jnp.zeros((1, TQ), F32) for _ in range(N_HEADS)),
             tuple(tails(0)), logits(jnp.maximum(top_kt - 1, 0), 0))
    for j in range(N_TOP - 1):
        carry = step(top_kt - j, carry, past_mask(N_TOP - 2 - j))
    below = qi * N_TOP
    laters, w_first, _ = lax.fori_loop(0, below, lambda i, c: step(below - i, c, None), carry)
    w = {0: list(w_first)}
    for h in range(1, ahead):
        w[h] = tails(h)
    for h in range(N_HEADS):
        consume(0, h, w.pop(h), laters[h])
        if h + ahead < N_HEADS:
            w[h + ahead] = tails(h + ahead)
    o_ref[0] = acc_s[...].reshape(GROUP_W, TQ)


def _stick(p2, batch, seq):
    TQ = STICK_TQ
    nq = seq // TQ
    t = batch * seq
    return pl.pallas_call(
        functools.partial(_stick_kernel, seq=seq),
        grid=(batch, nq),
        in_specs=[pl.BlockSpec((TQ, GROUP_W), lambda b, i: (b * nq + i, COL_QC // GROUP_W)),
                  pl.BlockSpec((seq, GROUP_W), lambda b, i: (b, COL_KC // GROUP_W)),
                  pl.BlockSpec((seq, GROUP_W), lambda b, i: (b, COL_VC // GROUP_W))],
        out_specs=pl.BlockSpec((1, GROUP_W, TQ), lambda b, i: (b * nq + i, 0, 0)),
        out_shape=jax.ShapeDtypeStruct((t // TQ, GROUP_W, TQ), F32),
        scratch_shapes=[pltpu.VMEM((seq, N_HEADS * LANES), BF16),
                        pltpu.VMEM((seq // TK, GROUP_W, TK), BF16),
                        pltpu.VMEM((N_HEADS // 2, TQ, LANES), BF16),
                        pltpu.VMEM((N_HEADS, TK, TQ), BF16),
                        pltpu.VMEM((N_HEADS, TK, TQ), F32),
                        pltpu.VMEM((N_HEADS, HEAD_DIM, TQ), F32)],
        compiler_params=pltpu.CompilerParams(dimension_semantics=("parallel", "arbitrary"),
                                             vmem_limit_bytes=VMEM_LIMIT),
        name="stick_breaking",
    )(p2, p2, p2)


def _mixout_kernel(x_ref, ocmp_ref, oslc_ref, owin_ref, gate_ref, cvb_ref, cvc_ref, cvu_ref, pc_ref, pu_ref,
                   oc_ref, od_ref, bg_ref, cw_ref, gout_ref, wout_ref, o_ref, od_s, *, tiles_per_seq):
    i = pl.program_id(0)
    tm, d = x_ref.shape
    chains = [slice(c * MIX_CHAIN, (c + 1) * MIX_CHAIN) for c in range(tm // MIX_CHAIN)]

    def heads(ref, rows):
        tq = ref.shape[2]
        return ref[rows.start // tq, :, rows.start % tq:rows.start % tq + MIX_CHAIN].T

    per = TQ // FOLD
    for tile in range(tm // TQ):
        folded = od_ref[tile].T
        for half in range(GROUP_W // LANES):
            od_s[tile, half] = folded[:, half * LANES:(half + 1) * LANES]

    def unfolded(rows):
        tile, first = rows.start // TQ, (rows.start % TQ) // FOLD
        return jnp.concatenate(
            [jnp.concatenate([od_s[tile, half, pl.ds(a, FOLD, stride=per), :] for a in range(first, first + MIX_CHAIN // FOLD)],
                             axis=0) for half in range(GROUP_W // LANES)], axis=1)

    cu = cvc_ref[...] * cvu_ref[...]
    prev = jnp.where(i % tiles_per_seq == 0, 0.0, pc_ref[...] * pu_ref[...])
    full = jnp.concatenate([prev, cu], axis=0)
    back1 = pltpu.roll(full, 1, 0)[8:, :]
    back2 = pltpu.roll(full, 2, 0)[8:, :]
    cw = cw_ref[...]
    ob = cvb_ref[...] * (cw[0:1, :] * back2 + cw[1:2, :] * back1 + cw[2:3, :] * cu)

    gates = jax.nn.sigmoid(gate_ref[...] + bg_ref[...])
    src = lax.broadcasted_iota(jnp.int32, (LANES, GROUP_W), 0)
    head3 = (lax.broadcasted_iota(jnp.int32, (LANES, GROUP_W), 1) >> 6) * 3
    branch = [(src == head3 + r).astype(BF16) for r in range(3)]
    assert d // HEAD_DIM <= LANES
    gather = ((lax.broadcasted_iota(jnp.int32, (d, LANES), 0) >> 6)
              == lax.broadcasted_iota(jnp.int32, (d, LANES), 1)).astype(BF16)
    spread = (lax.broadcasted_iota(jnp.int32, (LANES, d), 0)
              == (lax.broadcasted_iota(jnp.int32, (LANES, d), 1) >> 6)).astype(BF16)

    groups, ssq = [], []
    for rows in chains:
        oa = (_exact_dot(gates[rows], branch[0]) * heads(ocmp_ref, rows)
              + _exact_dot(gates[rows], branch[1]) * heads(oslc_ref, rows)
              + _exact_dot(gates[rows], branch[2]) * heads(owin_ref, rows))
        g = jnp.concatenate([oa, ob[rows], heads(oc_ref, rows), unfolded(rows)], axis=1)
        groups.append(g)
        ssq.append(_exact_dot(g * g, gather))
    scale = [_exact_dot(lax.rsqrt(s * (1.0 / HEAD_DIM) + RMS_EPS), spread) for s in ssq]
    for rows, g, sc in zip(chains, groups, scale):
        mixed = (g * sc * gout_ref[...]).astype(BF16)
        o_ref[rows, :] = x_ref[rows, :] + jnp.dot(mixed, wout_ref[...], preferred_element_type=F32)


def _mixout(x2, p2, ocmp, oslc, owin, oc, od, bg, cw, gout, wout, seq):
    t, d = x2.shape
    tm = 2 * MIX_CHAIN
    rows = lambda w, col: pl.BlockSpec((tm, w), lambda i: (i, col // w))
    heads_t = lambda tq: pl.BlockSpec((tm // tq, GROUP_W, tq), lambda i: (i, 0, 0))
    prev8 = lambda col: pl.BlockSpec((8, GROUP_W), lambda i: (jnp.maximum(i * (tm // 8) - 1, 0), col // GROUP_W))
    const = lambda shape: pl.BlockSpec(shape, lambda i: (0, 0))
    assert tm % TQ == 0 and tm % STICK_TQ == 0
    return pl.pallas_call(
        functools.partial(_mixout_kernel, tiles_per_seq=seq // tm),
        grid=(t // tm,),
        in_specs=[rows(d, 0), heads_t(TQ), heads_t(TQ), heads_t(TQ),
                  rows(LANES, COL_GATE), rows(GROUP_W, COL_CVB), rows(GROUP_W, COL_CVC), rows(GROUP_W, COL_CVU),
                  prev8(COL_CVC), prev8(COL_CVU),
                  heads_t(STICK_TQ), heads_t(TQ),
                  const((1, LANES)), const((8, GROUP_W)), const((1, d)), const((d, d))],
        out_specs=rows(d, 0),
        out_shape=jax.ShapeDtypeStruct((t, d), F32),
        scratch_shapes=[pltpu.VMEM((tm // TQ, GROUP_W // LANES, TQ, LANES), F32)],
        compiler_params=pltpu.CompilerParams(dimension_semantics=("parallel",), vmem_limit_bytes=VMEM_LIMIT),
        name="mixout",
    )(x2, ocmp, oslc, owin, p2, p2, p2, p2, p2, p2, oc, od, bg, cw, gout, wout)


def _ffn_kernel(x_ref, g_ref, wg_ref, wu_ref, wd_ref, o_ref, act_s, *, ff_chunk, out_chunk):
    h = _rms(x_ref[...], g_ref[...]).astype(BF16)
    dff = wg_ref.shape[1]
    for c in range(dff // ff_chunk):
        cols = slice(c * ff_chunk, (c + 1) * ff_chunk)
        a = jnp.dot(h, wg_ref[:, cols], preferred_element_type=F32)
        u = jnp.dot(h, wu_ref[:, cols], preferred_element_type=F32)
        act_s[:, cols] = (a * jax.nn.sigmoid(a) * u).astype(BF16)
    d = o_ref.shape[1]
    for c in range(d // out_chunk):
        cols = slice(c * out_chunk, (c + 1) * out_chunk)
        o_ref[:, cols] = x_ref[:, cols] + jnp.dot(act_s[...], wd_ref[:, cols], preferred_element_type=F32)


def _ffn(x2, g, wg, wu, wd):
    t, d = x2.shape
    dff = wg.shape[1]
    tm = 1024
    resident = lambda shape: pl.BlockSpec(shape, lambda i: (0, 0), pipeline_mode=pl.Buffered(1))
    return pl.pallas_call(
        functools.partial(_ffn_kernel, ff_chunk=256, out_chunk=256),
        grid=(t // tm,),
        in_specs=[pl.BlockSpec((tm, d), lambda i: (i, 0)),
                  pl.BlockSpec((1, d), lambda i: (0, 0)),
                  resident((d, dff)), resident((d, dff)), resident((dff, d))],
        out_specs=pl.BlockSpec((tm, d), lambda i: (i, 0)),
        out_shape=jax.ShapeDtypeStruct((t, d), F32),
        scratch_shapes=[pltpu.VMEM((tm, dff), BF16)],
        compiler_params=pltpu.CompilerParams(dimension_semantics=("parallel",), vmem_limit_bytes=VMEM_LIMIT),
        name="ffn",
    )(x2, g, wg, wu, wd)


def _permute_w_in_kernel(w_ref, o_ref):
    rows = w_ref.shape[0]
    n_rest = P_COLS - COL_CVB
    o_ref[:, 0:COL_GATE] = w_ref[:, 0:COL_GATE].astype(BF16)
    lane = lax.broadcasted_iota(jnp.int32, (rows, LANES), 1)
    o_ref[:, COL_GATE:COL_GATE + LANES] = jnp.where(lane < N_GATES, w_ref[:, COL_GATE:COL_GATE + LANES], 0.0).astype(BF16)
    o_ref[:, COL_CVB:P_COLS] = w_ref[:, COL_GATE + N_GATES:COL_GATE + N_GATES + n_rest].astype(BF16)


def _permute_w_in(w_in):
    d, cols = w_in.shape
    tr = 256
    return pl.pallas_call(
        _permute_w_in_kernel,
        grid=(d // tr,),
        in_specs=[pl.BlockSpec((tr, cols), lambda i: (i, 0))],
        out_specs=pl.BlockSpec((tr, P_COLS), lambda i: (i, 0)),
        out_shape=jax.ShapeDtypeStruct((d, P_COLS), BF16),
        compiler_params=pltpu.CompilerParams(dimension_semantics=("parallel",), vmem_limit_bytes=VMEM_LIMIT),
        name="permute_w_in",
    )(w_in)


def _layer(x2, batch, seq, g_mix, w_in, b_gate, g_q_nsa, g_k_cmp, g_k_slc, g_k_win, pe_k_cmp, pe_v_cmp,
           w1_k_cmp, w2_k_cmp, w1_v_cmp, w2_v_cmp, conv_w, g_q_dil, g_k_dil, g_out, w_out,
           g_ffn, w_gate, w_up, w_down):
    row = lambda v: v.reshape(1, -1)
    per_group = lambda g, n: jnp.tile(g, n).reshape(1, -1)
    p2 = _inproj(x2, row(g_mix), _permute_w_in(w_in))

    pe, w1, w2 = _compress_weights(pe_k_cmp, pe_v_cmp, w1_k_cmp, w2_k_cmp, w1_v_cmp, w2_v_cmp)
    kc, vc = _compress(p2, pe, w1, w2, per_group(g_k_cmp, 2), batch, seq)
    ocmp, oslc, owin = _nsa(p2, kc, vc, per_group(g_q_nsa, N_HEADS), per_group(g_k_slc, 2), per_group(g_k_win, 2),
                            batch, seq)
    od = _dilated(p2, per_group(g_q_dil, 2), per_group(g_k_dil, 2), batch, seq)
    oc = _stick(p2, batch, seq)

    bg = jnp.zeros((1, LANES), F32).at[0, :N_GATES].set(b_gate)
    cwp = jnp.zeros((8, GROUP_W), F32).at[:CONV_K].set(conv_w)
    x1 = _mixout(x2, p2, ocmp, oslc, owin, oc, od, bg, cwp, row(g_out), w_out.astype(BF16), seq)
    return _ffn(x1, row(g_ffn), w_gate.astype(BF16), w_up.astype(BF16), w_down.astype(BF16))


def kernel(x, g_mix, w_in, b_gate, g_q_nsa, g_k_cmp, g_k_slc, g_k_win, pe_k_cmp, pe_v_cmp, w1_k_cmp, w2_k_cmp,
           w1_v_cmp, w2_v_cmp, conv_w, g_q_dil, g_k_dil, g_out, w_out, g_ffn, w_gate, w_up, w_down):
    batch, seq, d = x.shape
    assert seq % (CMP_STRIDE * LANES) == 0 and d % LANES == 0
    assert seq <= 64 * 256
    x2 = x.reshape(batch * seq, d)
    params = (g_mix, w_in, b_gate, g_q_nsa, g_k_cmp, g_k_slc, g_k_win, pe_k_cmp, pe_v_cmp, w1_k_cmp, w2_k_cmp,
              w1_v_cmp, w2_v_cmp, conv_w, g_q_dil, g_k_dil, g_out, w_out, g_ffn, w_gate, w_up, w_down)
    for layer in range(g_mix.shape[0]):
        x2 = _layer(x2, batch, seq, *[p[layer] for p in params])
    return x2.reshape(batch, seq, d)
```

```python
import functools

import numpy as np
import jax
import jax.numpy as jnp
from jax import lax
from jax.experimental import pallas as pl
from jax.experimental.pallas import tpu as pltpu

F32 = jnp.float32
BF16 = jnp.bfloat16

HEAD_DIM = 64
N_HEADS = 4
GROUP_W = N_HEADS * HEAD_DIM
CONV_K = 3
CMP_LEN = 32
CMP_STRIDE = 16
SEL_BLOCK = 64
N_SELECT = 16
NSA_WINDOW = 512
DILATED_CONFIGS = ((128, 1), (512, 4), (2048, 16))
DILATED_WINDOW = max(w for w, _ in DILATED_CONFIGS)
FOLD = 16
FAR_FROM = 512
FAR_ROWS = 128
NEG_INF = -1e30
FORCE_SCORE = 1e6
RMS_EPS = 1e-6

TQ = 512
TK = 256
N_TOP = TQ // TK
CLS_PER_GROUP = 128 * FOLD // TQ
STICK_TQ = 256
LANES = 128
MXU_DEPTH = 256
MIX_CHAIN = 256
MXU_AHEAD = 2
FLASH_AHEAD = 1
SUM_PAD = 16
V_ROWS = HEAD_DIM + SUM_PAD
LOG2E = 1.4426950408889634
VMEM_LIMIT = 52 * 1024 * 1024

COL_QA, COL_KVC, COL_KSV, COL_KWV, COL_GATE = 0, 256, 384, 512, 640
COL_CVB, COL_CVC, COL_CVU = 768, 1024, 1280
COL_QC, COL_KC, COL_VC = 1536, 1792, 2048
COL_QD, COL_KD, COL_VD = 2304, 2560, 2816
P_COLS = 3072
N_GATES = 12

_NT = (((1,), (1,)), ((), ()))


def _alibi_slopes():
    s = [2.0 ** (-8.0 * i / 8) for i in range(1, 9)]
    return tuple(s[0::2]), tuple(s[1::2])


SLOPES_A, SLOPES_D = _alibi_slopes()


def _rms(x, g):
    return x * lax.rsqrt(jnp.mean(x * x, axis=-1, keepdims=True) + RMS_EPS) * g


def _exact_dot(a, sel):
    hi = a.astype(BF16)
    lo = (a - hi.astype(F32)).astype(BF16)
    if 2 * a.shape[1] <= MXU_DEPTH:
        return jnp.dot(jnp.concatenate([hi, lo], axis=1), jnp.concatenate([sel, sel], axis=0),
                       preferred_element_type=F32)
    return jnp.dot(hi, sel, preferred_element_type=F32) + jnp.dot(lo, sel, preferred_element_type=F32)


def _group_rms(x, g):
    w = x.shape[1]
    same = ((lax.broadcasted_iota(jnp.int32, (w, w), 0) >> 6)
            == (lax.broadcasted_iota(jnp.int32, (w, w), 1) >> 6)).astype(BF16)
    ssq = _exact_dot(x * x, same)
    return x * lax.rsqrt(ssq * (1.0 / HEAD_DIM) + RMS_EPS) * g


EVEN_AUG, ODD_AUG = HEAD_DIM, 0


def _key_aug(pos, first):
    lane = lax.broadcasted_iota(jnp.int32, (pos.shape[0], LANES), 1) - first
    hi = (pos >> 6).astype(F32)
    lo = (pos & 63).astype(F32)
    return jnp.where(lane == 0, hi, jnp.where(lane == 1, lo, jnp.where((lane == 2) | (lane == 3), 1.0, 0.0)))


def _query_aug(t, slope, first):
    lane = lax.broadcasted_iota(jnp.int32, (t.shape[0], LANES), 1) - first
    hi = (t >> 6).astype(F32) * (-64.0 * slope)
    lo = (t & 63).astype(F32) * (-slope)
    return jnp.where(lane == 0, 64.0 * slope,
                     jnp.where(lane == 1, slope, jnp.where(lane == 2, hi, jnp.where(lane == 3, lo, 0.0))))


def _pair_keys(pair, pos):
    lane = lax.broadcasted_iota(jnp.int32, pair.shape, 1)
    aug_e = 0.0 if pos is None else _key_aug(pos, EVEN_AUG)
    aug_o = 0.0 if pos is None else _key_aug(pos, ODD_AUG)
    return (jnp.where(lane < HEAD_DIM, pair, aug_e).astype(BF16),
            jnp.where(lane >= HEAD_DIM, pair, aug_o).astype(BF16))


def _prep_queries(q, g, t0, slopes, qp_s):
    qn = (q if g is None else _group_rms(q, g)) * (HEAD_DIM ** -0.5)
    tpos = t0 + lax.broadcasted_iota(jnp.int32, (q.shape[0], 1), 0)
    lane = lax.broadcasted_iota(jnp.int32, (q.shape[0], LANES), 1)
    for i in range(N_HEADS // 2):
        pair = qn[:, i * LANES:(i + 1) * LANES]
        if slopes is None:
            qp_s[i] = pair.astype(BF16)
            continue
        even = jnp.where((lane >= EVEN_AUG) & (lane < EVEN_AUG + 4), _query_aug(tpos, slopes[2 * i], EVEN_AUG), pair)
        odd = jnp.where(lane < ODD_AUG + 4, _query_aug(tpos, slopes[2 * i + 1], ODD_AUG), pair)
        qp_s[2 * i] = even.astype(BF16)
        qp_s[2 * i + 1] = odd.astype(BF16)


def _inproj_kernel(x_ref, g_ref, w_ref, o_ref, *, nchunk):
    h = _rms(x_ref[...], g_ref[...]).astype(BF16)
    cw = P_COLS // nchunk
    for c in range(nchunk):
        o_ref[:, c * cw:(c + 1) * cw] = jnp.dot(h, w_ref[:, c * cw:(c + 1) * cw], preferred_element_type=F32)


def _inproj(x2, g, w):
    t, d = x2.shape
    tm = 1024
    return pl.pallas_call(
        functools.partial(_inproj_kernel, nchunk=6),
        grid=(t // tm,),
        in_specs=[pl.BlockSpec((tm, d), lambda i: (i, 0)),
                  pl.BlockSpec((1, d), lambda i: (0, 0)),
                  pl.BlockSpec((d, P_COLS), lambda i: (0, 0), pipeline_mode=pl.Buffered(1))],
        out_specs=pl.BlockSpec((tm, P_COLS), lambda i: (i, 0)),
        out_shape=jax.ShapeDtypeStruct((t, P_COLS), F32),
        compiler_params=pltpu.CompilerParams(dimension_semantics=("parallel",), vmem_limit_bytes=VMEM_LIMIT),
        name="inproj",
    )(x2, g, w)


def _gelu_tanh(x):
    return x * (0.5 * (1.0 + jnp.tanh(np.sqrt(2.0 / np.pi).astype(np.float32) * (x + 0.044715 * (x * x * x)))))


def _compress_kernel(kv_ref, pe_ref, w1_ref, w2_ref, gk_ref, kc_ref, vct_ref):
    nc = kv_ref.shape[0] // CMP_STRIDE
    first = None
    second = None
    for j in range(CMP_STRIDE):
        tok = kv_ref[pl.ds(j, nc, stride=CMP_STRIDE), :]
        a = jnp.dot((tok + pe_ref[j:j + 1, :]).astype(BF16), w1_ref[j], preferred_element_type=F32)
        b = jnp.dot((tok + pe_ref[CMP_STRIDE + j:CMP_STRIDE + j + 1, :]).astype(BF16), w1_ref[CMP_STRIDE + j],
                    preferred_element_type=F32)
        first = a if first is None else first + a
        second = b if second is None else second + b
    hid = first + pltpu.roll(second, nc - 1, 0)
    out = jnp.dot(_gelu_tanh(hid).astype(BF16), w2_ref[...], preferred_element_type=F32)
    kn = _group_rms(out, gk_ref[...])
    end = lax.broadcasted_iota(jnp.int32, (nc, 1), 0) * CMP_STRIDE + (CMP_LEN - 1)
    even, _ = _pair_keys(kn, end)
    _, odd = _pair_keys(pltpu.roll(kn, HEAD_DIM, 1), end)
    kc_ref[0] = jnp.concatenate([even, odd], axis=1)
    vct_ref[0] = out.T[HEAD_DIM:, :].astype(BF16)


def _compress(p2, pe, w1, w2, gk, batch, seq):
    nc = seq // CMP_STRIDE
    hid2 = w1.shape[2]
    return pl.pallas_call(
        _compress_kernel,
        grid=(batch,),
        in_specs=[pl.BlockSpec((seq, LANES), lambda i: (i, COL_KVC // LANES)),
                  pl.BlockSpec((CMP_LEN, LANES), lambda i: (0, 0)),
                  pl.BlockSpec((CMP_LEN, LANES, hid2), lambda i: (0, 0, 0)),
                  pl.BlockSpec((hid2, LANES), lambda i: (0, 0)),
                  pl.BlockSpec((1, LANES), lambda i: (0, 0))],
        out_specs=[pl.BlockSpec((1, nc, 2 * LANES), lambda i: (i, 0, 0)),
                   pl.BlockSpec((1, HEAD_DIM, nc), lambda i: (i, 0, 0))],
        out_shape=[jax.ShapeDtypeStruct((batch, nc, 2 * LANES), BF16),
                   jax.ShapeDtypeStruct((batch, HEAD_DIM, nc), BF16)],
        compiler_params=pltpu.CompilerParams(dimension_semantics=("parallel",), vmem_limit_bytes=VMEM_LIMIT),
        name="nsa_compress",
    )(p2, pe, w1, w2, gk)


def _compress_weights(pe_k, pe_v, w1_k, w2_k, w1_v, w2_v):
    hid = w1_k.shape[1]
    w1k = w1_k.reshape(CMP_LEN, HEAD_DIM, hid)
    w1v = w1_v.reshape(CMP_LEN, HEAD_DIM, hid)
    z1 = jnp.zeros_like(w1k)
    w1 = jnp.concatenate([jnp.concatenate([w1k, z1], axis=2), jnp.concatenate([z1, w1v], axis=2)], axis=1)
    z2 = jnp.zeros_like(w2_k)
    w2 = jnp.concatenate([jnp.concatenate([w2_k, z2], axis=1), jnp.concatenate([z2, w2_v], axis=1)], axis=0)
    pe = jnp.concatenate([pe_k, pe_v], axis=1)
    return pe, w1.astype(BF16), w2.astype(BF16)


ALL_LANES = slice(0, TQ)


def _with_lanes(full, lanes, part):
    pieces = ([full[:, :lanes.start]] if lanes.start > 0 else []) + [part] + \
             ([full[:, lanes.stop:]] if lanes.stop < TQ else [])
    return jnp.concatenate(pieces, axis=1) if len(pieces) > 1 else part


def _flash_scores(kp_of, bias, qp_s, s_s, m_cur, lanes=ALL_LANES):
    m_next = []
    for h in range(N_HEADS):
        s = lax.dot_general(kp_of(h), qp_s[h, lanes, :], _NT, preferred_element_type=F32) + bias[:, lanes]
        s_s[h, :, lanes] = s
        m_next.append(_with_lanes(m_cur[h], lanes, jnp.maximum(m_cur[h][:, lanes], jnp.max(s, axis=0, keepdims=True))))
    return tuple(m_next)


def _flash_accum(vt_of, s_s, acc_s, m_prev, m_cur, lanes=ALL_LANES):
    for h in range(N_HEADS):
        alpha = jnp.exp(m_prev[h][:, lanes] - m_cur[h][:, lanes])
        p = jnp.exp(s_s[h, :, lanes] - m_cur[h][:, lanes])
        acc_s[h, :, lanes] = alpha * acc_s[h, :, lanes] + jnp.dot(vt_of(h), p.astype(BF16),
                                                                 preferred_element_type=F32)


def _flash_accum_and_scores(kp_of, bias, vt_of, qp_s, s_s, acc_s, m_prev, m_cur, acc_lanes=ALL_LANES,
                            new_lanes=ALL_LANES):
    def scores(h):
        return (lax.dot_general(kp_of(h), qp_s[h, new_lanes, :], _NT, preferred_element_type=F32)
                + bias[:, new_lanes])

    m_next = []
    s_new = {h: scores(h) for h in range(FLASH_AHEAD)}
    for h in range(N_HEADS):
        alpha = jnp.exp(m_prev[h][:, acc_lanes] - m_cur[h][:, acc_lanes])
        p = jnp.exp(s_s[h, :, acc_lanes] - m_cur[h][:, acc_lanes])
        acc_s[h, :, acc_lanes] = alpha * acc_s[h, :, acc_lanes] + jnp.dot(vt_of(h), p.astype(BF16),
                                                                         preferred_element_type=F32)
        s_h = s_new.pop(h)
        s_s[h, :, new_lanes] = s_h
        m_next.append(_with_lanes(m_cur[h], new_lanes,
                                  jnp.maximum(m_cur[h][:, new_lanes], jnp.max(s_h, axis=0, keepdims=True))))
        if h + FLASH_AHEAD < N_HEADS:
            s_new[h + FLASH_AHEAD] = scores(h + FLASH_AHEAD)
    return tuple(m_next)


def _flash_pipelined(tile_of, n_tiles, kp_fn, bias_fn, vt_fn, qp_s, s_s, acc_s, o_ref, causal_lanes=True):
    top_lanes = [slice((N_TOP - 1 - i) * TK if causal_lanes else 0, TQ) for i in range(N_TOP)]
    acc_s[...] = jnp.zeros(acc_s.shape, F32)
    m0 = tuple(jnp.full((1, TQ), NEG_INF, F32) for _ in range(N_HEADS))
    first_kt = tile_of(0)
    m1 = _flash_scores(kp_fn(first_kt), bias_fn(first_kt, True), qp_s, s_s, m0, top_lanes[0])

    def step(i, carry, top):
        m_prev, m_cur = carry
        nxt = tile_of(i + 1)
        lanes = (top_lanes[i], top_lanes[i + 1]) if top else (ALL_LANES, ALL_LANES)
        m_next = _flash_accum_and_scores(kp_fn(nxt), bias_fn(nxt, top), vt_fn(tile_of(i)), qp_s, s_s, acc_s,
                                         m_prev, m_cur, *lanes)
        return m_cur, m_next

    carry = (m0, m1)
    for i in range(N_TOP - 1):
        carry = step(i, carry, True)
    m_prev, m_cur = lax.fori_loop(N_TOP - 1, n_tiles - 1, lambda i, c: step(i, c, False), carry)
    _flash_accum(vt_fn(tile_of(n_tiles - 1)), s_s, acc_s, m_prev, m_cur)
    if o_ref is None:
        return m_cur
    _flash_write(acc_s, o_ref)


def _flash_write(acc_s, o_ref):
    o_ref[0] = jnp.concatenate([acc_s[h, 0:HEAD_DIM, :] * (1.0 / acc_s[h, HEAD_DIM:HEAD_DIM + 1, :])
                                for h in range(N_HEADS)], axis=0)


def _values_t(v_t):
    row = lax.broadcasted_iota(jnp.int32, (V_ROWS - HEAD_DIM, v_t.shape[1]), 0)
    return jnp.concatenate([v_t, jnp.where(row == 0, 1.0, 0.0)], axis=0).astype(BF16)


def _nsa_kernel(q_ref, kc_ref, vct_ref, ksv_ref, kwv_ref, gq_ref, gks_ref, gkw_ref, wbias_ref,
                ocmp_ref, oslc_ref, owin_ref,
                ksp_s, vst_s, kwp_s, vwt_s, qp_s, imp_s, sel_s, s_s, acc_s, hit_s, *, seq):
    qi = pl.program_id(1)
    nc = seq // CMP_STRIDE
    nsel = seq // SEL_BLOCK
    nkt = seq // TK

    @pl.when(qi == 0)
    def _prep():
        for c in range(nkt):
            rows = slice(c * TK, (c + 1) * TK)
            pos = c * TK + lax.broadcasted_iota(jnp.int32, (TK, 1), 0)
            for kv_ref, g_ref, kp_s, vt_s in ((ksv_ref, gks_ref, ksp_s, vst_s), (kwv_ref, gkw_ref, kwp_s, vwt_s)):
                blk = kv_ref[rows, :]
                kn = _group_rms(blk, g_ref[...])
                kp_s[rows, 0:LANES], _ = _pair_keys(kn, pos)
                _, kp_s[rows, LANES:2 * LANES] = _pair_keys(pltpu.roll(kn, HEAD_DIM, 1), pos)
                vt_s[c] = _values_t(blk.T[HEAD_DIM:, :])
        imp_s[:, 0:8, :] = jnp.zeros((TQ // LANES, 8, LANES), F32)

    t0 = qi * TQ
    _prep_queries(q_ref[...], gq_ref[...], t0, SLOPES_A, qp_s)

    kc = [kc_ref[0, :, 0:LANES], kc_ref[0, :, LANES:2 * LANES]]
    vct = vct_ref[0]
    tq_row = t0 + lax.broadcasted_iota(jnp.int32, (nc, TQ), 1)
    n_idx = lax.broadcasted_iota(jnp.int32, (nc, TQ), 0)
    vis = (tq_row >= n_idx * CMP_STRIDE + (CMP_LEN - 1)) & (n_idx < nc - 1)
    vis_bias = jnp.where(vis, 0.0, NEG_INF)
    sees_any = (t0 + lax.broadcasted_iota(jnp.int32, (1, TQ), 1) >= CMP_LEN - 1).astype(F32)
    imp = jnp.zeros((nc, TQ), F32)
    ocmp_t = []
    scores = [lax.dot_general(kc[h % 2], qp_s[h], _NT, preferred_element_type=F32) + vis_bias
              for h in range(N_HEADS)]
    for h in range(N_HEADS):
        sc = scores[h]
        e = jnp.exp(sc - jnp.max(sc, axis=0, keepdims=True))
        p = e * (sees_any / jnp.sum(e, axis=0, keepdims=True))
        ocmp_t.append(jnp.dot(vct, p.astype(BF16), preferred_element_type=F32))
        imp = imp + p
    ocmp_ref[0] = jnp.concatenate(ocmp_t, axis=0)

    halves = []
    for half in range(TQ // LANES):
        imp_s[half, 8:8 + nc, :] = imp[:, half * LANES:(half + 1) * LANES]
        r = [imp_s[half, pl.ds(8 + k, nsel, stride=4), :] for k in range(4)]
        rm1 = imp_s[half, pl.ds(7, nsel, stride=4), :]
        halves.append(rm1 + 2.0 * (r[0] + r[1] + r[2]) + r[3])
    imp_blk = jnp.concatenate(halves, axis=1)
    blk = lax.broadcasted_iota(jnp.int32, (nsel, TQ), 0)
    tl = t0 + lax.broadcasted_iota(jnp.int32, (nsel, TQ), 1)
    cur = tl >> 6
    forced = (blk == 0) | (blk == cur) | (blk == cur - 1)
    valid = blk * SEL_BLOCK <= tl
    score = jnp.where(forced, FORCE_SCORE, jnp.where(valid, imp_blk, -FORCE_SCORE))
    sub = 8
    per_tile = TK // SEL_BLOCK
    tile_hit = []
    for g in range(nsel // sub):
        mine = score[g * sub:(g + 1) * sub, :]
        blk_g = g * sub + lax.broadcasted_iota(jnp.int32, (sub, TQ), 0)
        rank = jnp.zeros((sub, TQ), F32)
        for j in range(nsel):
            row = score[j:j + 1, :]
            if j < g * sub:
                beats = row >= mine
            elif j >= (g + 1) * sub:
                beats = row > mine
            else:
                beats = (row > mine) | ((row == mine) & (blk_g > j))
            rank = rank + jnp.where(beats, 1.0, 0.0)
        sel_g = jnp.where(rank < min(N_SELECT, nsel), 0.0, NEG_INF)
        sel_s[g * sub:(g + 1) * sub, :] = sel_g
        for part in range(sub // per_tile):
            tile_hit.append(jnp.max(sel_g[part * per_tile:(part + 1) * per_tile, :]))

    n_hit = jnp.int32(0)
    for kt in reversed(range(nkt)):
        hit_s[n_hit] = kt
        n_hit = n_hit + ((tile_hit[kt] > -1.0) & (kt * TK < t0)).astype(jnp.int32)

    def block_bias(kt):
        return jnp.concatenate([jnp.broadcast_to(sel_s[pl.ds(kt * per_tile + j, 1), :], (SEL_BLOCK, TQ))
                                for j in range(per_tile)], axis=0)

    k_local = lax.broadcasted_iota(jnp.int32, (TK, TQ), 0)
    q_local = lax.broadcasted_iota(jnp.int32, (TK, TQ), 1)
    top_kt = qi * N_TOP + (N_TOP - 1)

    def kp_fn(kt):
        off = pl.multiple_of(kt * TK, TK)
        return lambda h: ksp_s[pl.ds(off, TK), (h % 2) * LANES:(h % 2 + 1) * LANES]

    def bias_fn(kt, top):
        if top:
            return jnp.where(kt * TK - t0 + k_local <= q_local, block_bias(kt), NEG_INF)
        return block_bias(kt)

    tile_of = lambda i: jnp.where(i < N_TOP, top_kt - i, hit_s[jnp.maximum(i - N_TOP, 0)])
    _flash_pipelined(tile_of, n_hit + N_TOP, kp_fn, bias_fn, lambda kt: (lambda h: vst_s[kt]), qp_s, s_s, acc_s,
                     oslc_ref)

    def window_kp_fn(kt):
        off = pl.multiple_of(kt * TK, TK)
        return lambda h: kwp_s[pl.ds(off, TK), (h % 2) * LANES:(h % 2 + 1) * LANES]

    kt_lo = jnp.maximum(t0 - (NSA_WINDOW - 1), 0) // TK
    _flash_pipelined(lambda i: top_kt - i, top_kt + 1 - kt_lo, window_kp_fn, lambda kt, top: wbias_ref[top_kt - kt],
                     lambda kt: (lambda h: vwt_s[kt]), qp_s, s_s, acc_s, owin_ref)


def _nsa(p2, kc, vc, gq, gks, gkw, batch, seq):
    nq = seq // TQ
    nc = seq // CMP_STRIDE
    t = batch * seq
    wbias = jnp.asarray(_band_bias_tiles(NSA_WINDOW - 1, _window_mult))
    head_tiles = pl.BlockSpec((1, GROUP_W, TQ), lambda b, i: (b * nq + i, 0, 0))
    head_shape = jax.ShapeDtypeStruct((t // TQ, GROUP_W, TQ), F32)
    return pl.pallas_call(
        functools.partial(_nsa_kernel, seq=seq),
        grid=(batch, nq),
        in_specs=[pl.BlockSpec((TQ, GROUP_W), lambda b, i: (b * nq + i, COL_QA // GROUP_W)),
                  pl.BlockSpec((1, nc, 2 * LANES), lambda b, i: (b, 0, 0)),
                  pl.BlockSpec((1, HEAD_DIM, nc), lambda b, i: (b, 0, 0)),
                  pl.BlockSpec((seq, LANES), lambda b, i: (b, COL_KSV // LANES)),
                  pl.BlockSpec((seq, LANES), lambda b, i: (b, COL_KWV // LANES)),
                  pl.BlockSpec((1, GROUP_W), lambda b, i: (0, 0)),
                  pl.BlockSpec((1, LANES), lambda b, i: (0, 0)),
                  pl.BlockSpec((1, LANES), lambda b, i: (0, 0)),
                  pl.BlockSpec(wbias.shape, lambda b, i: (0, 0, 0))],
        out_specs=[head_tiles, head_tiles, head_tiles],
        out_shape=[head_shape, head_shape, head_shape],
        scratch_shapes=[pltpu.VMEM((seq, 2 * LANES), BF16),
                        pltpu.VMEM((seq // TK, V_ROWS, TK), BF16),
                        pltpu.VMEM((seq, 2 * LANES), BF16),
                        pltpu.VMEM((seq // TK, V_ROWS, TK), BF16),
                        pltpu.VMEM((N_HEADS, TQ, LANES), BF16),
                        pltpu.VMEM((TQ // LANES, 8 + nc, LANES), F32),
                        pltpu.VMEM((seq // SEL_BLOCK, TQ), F32),
                        pltpu.VMEM((N_HEADS, TK, TQ), F32),
                        pltpu.VMEM((N_HEADS, V_ROWS, TQ), F32),
                        pltpu.SMEM((seq // TK + 1,), jnp.int32)],
        compiler_params=pltpu.CompilerParams(dimension_semantics=("parallel", "arbitrary"),
                                             vmem_limit_bytes=VMEM_LIMIT),
        name="nsa",
    )(p2, kc, vc, p2, p2, gq, gks, gkw, wbias)


def _band_bias_tiles(window, mult_fn, folded=False):
    nd = (window + TQ - 1) // TK + 1
    kk = np.arange(TK)[:, None]
    qq = np.arange(TQ)[None, :]
    if folded:
        qq = FOLD * (qq % (TQ // FOLD)) + qq // (TQ // FOLD)
    tiles = np.empty((nd, TK, TQ), np.float32)
    for di in range(nd):
        d = (di - (N_TOP - 1)) * TK + qq - kk
        mult = mult_fn(d)
        tiles[di] = np.where(mult > 0, np.log(np.maximum(mult, 1)), NEG_INF)
    return tiles


def _window_mult(d):
    return ((d >= 0) & (d <= NSA_WINDOW - 1)).astype(np.float64)


def _dilated_mult(d):
    m = np.zeros(d.shape, np.float64)
    for window, dil in DILATED_CONFIGS:
        m += ((d >= 0) & (d <= window) & (d % dil == 0)).astype(np.float64)
    return m


def _near_mult(d):
    return np.where(d <= FAR_FROM, _dilated_mult(d), 0.0)


def _dilated_kernel(q0_ref, q1_ref, k0_ref, k1_ref, v0_ref, v1_ref, gq_ref, gk_ref, bias_ref, o_ref,
                    kp_s, vt_s, kf_s, vf_s, qp_s, qcat_s, s_s, acc_s, m_s, *, seq):
    qi = pl.program_id(1)
    per = TQ // FOLD
    cls_rows = seq // FOLD
    pairs = ((q0_ref, k0_ref, v0_ref), (q1_ref, k1_ref, v1_ref))

    @pl.when(qi == 0)
    def _prep():
        for p, (_, k_ref, v_ref) in enumerate(pairs):
            for c in range(seq // TK):
                rows = slice(c * TK, (c + 1) * TK)
                pos = c * TK + lax.broadcasted_iota(jnp.int32, (TK, 1), 0)
                even, odd = _pair_keys(_group_rms(k_ref[rows, :], gk_ref[...]), pos)
                kp_s[rows, 2 * p * LANES:(2 * p + 1) * LANES] = even
                kp_s[rows, (2 * p + 1) * LANES:(2 * p + 2) * LANES] = odd
                v_t = v_ref[rows, :].T
                for j in range(2):
                    vt_s[c, (2 * p + j) * V_ROWS:(2 * p + j + 1) * V_ROWS, :] = _values_t(
                        v_t[j * HEAD_DIM:(j + 1) * HEAD_DIM, :])
            for r in range(FOLD):
                pos = r + FOLD * lax.broadcasted_iota(jnp.int32, (cls_rows, 1), 0)
                even, odd = _pair_keys(_group_rms(k_ref[pl.ds(r, cls_rows, stride=FOLD), :], gk_ref[...]), pos)
                kf_s[2 * p, :, r * LANES:(r + 1) * LANES] = even
                kf_s[2 * p + 1, :, r * LANES:(r + 1) * LANES] = odd
                v_t = v_ref[pl.ds(r, cls_rows, stride=FOLD), :].T
                for j in range(2):
                    vf_s[2 * p + j, :, r * cls_rows:(r + 1) * cls_rows] = _values_t(
                        v_t[j * HEAD_DIM:(j + 1) * HEAD_DIM, :])

    t0 = qi * TQ
    u = lax.broadcasted_iota(jnp.int32, (TQ, 1), 0)
    tpos = t0 + FOLD * (u % per) + u // per
    lane = lax.broadcasted_iota(jnp.int32, (TQ, LANES), 1)
    for p, (q_ref, _, _) in enumerate(pairs):
        q = jnp.concatenate([q_ref[pl.ds(r, per, stride=FOLD), :] for r in range(FOLD)], axis=0)
        pair = _group_rms(q, gq_ref[...]) * (HEAD_DIM ** -0.5)
        even = jnp.where((lane >= EVEN_AUG) & (lane < EVEN_AUG + 4), _query_aug(tpos, SLOPES_D[2 * p], EVEN_AUG), pair)
        odd = jnp.where(lane < ODD_AUG + 4, _query_aug(tpos, SLOPES_D[2 * p + 1], ODD_AUG), pair)
        qp_s[2 * p] = even.astype(BF16)
        qp_s[2 * p + 1] = odd.astype(BF16)

    kt_lo = jnp.maximum(t0 - FAR_FROM, 0) // TK
    top_kt = qi * N_TOP + (N_TOP - 1)

    def kp_fn(kt):
        off = pl.multiple_of(kt * TK, TK)
        return lambda h: kp_s[pl.ds(off, TK), h * LANES:(h + 1) * LANES]

    m = _flash_pipelined(lambda i: top_kt - i, top_kt + 1 - kt_lo, kp_fn, lambda kt, top: bias_ref[top_kt - kt],
                         lambda kt: (lambda h: vt_s[kt, h * V_ROWS:(h + 1) * V_ROWS, :]), qp_s, s_s, acc_s, None,
                         causal_lanes=False)
    for h in range(N_HEADS):
        m_s[h] = m[h]

    groups = TQ // LANES
    cls_of_row = lax.broadcasted_iota(jnp.int32, (LANES, LANES), 0) // per
    cls_of_lane = lax.broadcasted_iota(jnp.int32, (FAR_ROWS, LANES), 1) // per
    for h in range(N_HEADS):
        for g in range(groups):
            rows = qp_s[h, g * LANES:(g + 1) * LANES, :]
            qcat_s[h, g] = jnp.concatenate([jnp.where(cls_of_row == j, rows, jnp.zeros_like(rows))
                                            for j in range(CLS_PER_GROUP)], axis=1)

    a0 = t0 // FOLD
    a_lane = a0 + lax.broadcasted_iota(jnp.int32, (FAR_ROWS, TQ), 1) % per
    row = lax.broadcasted_iota(jnp.int32, (FAR_ROWS, TQ), 0)

    def far_block(blk):
        back = a_lane - (blk * FAR_ROWS + row)
        bias = jnp.where((back > FAR_FROM // FOLD) & (back <= DILATED_WINDOW // FOLD), 0.0, NEG_INF)
        keys = slice(blk * FAR_ROWS, (blk + 1) * FAR_ROWS)
        scores = [jnp.concatenate(
            [lax.dot_general(kf_s[h, keys, g * CLS_PER_GROUP * LANES:(g + 1) * CLS_PER_GROUP * LANES], qcat_s[h, g],
                             _NT, preferred_element_type=F32) for g in range(groups)], axis=1) + bias
            for h in range(N_HEADS)]
        for h in range(N_HEADS):
            m_old = m_s[h]
            m_new = jnp.maximum(m_old, jnp.max(scores[h], axis=0, keepdims=True))
            p = jnp.exp(scores[h] - m_new)
            pv = []
            for g in range(groups):
                pg = p[:, g * LANES:(g + 1) * LANES]
                rhs = jnp.concatenate([jnp.where(cls_of_lane == j, pg, 0.0) for j in range(CLS_PER_GROUP)],
                                      axis=0).astype(BF16)
                lhs = jnp.concatenate(
                    [vf_s[h, :, (g * CLS_PER_GROUP + j) * cls_rows + blk * FAR_ROWS:
                           (g * CLS_PER_GROUP + j) * cls_rows + (blk + 1) * FAR_ROWS] for j in range(CLS_PER_GROUP)],
                    axis=1)
                pv.append(jnp.dot(lhs, rhs, preferred_element_type=F32))
            acc_s[h] = jnp.exp(m_old - m_new) * acc_s[h] + jnp.concatenate(pv, axis=1)
            m_s[h] = m_new

    for blk in range(cls_rows // FAR_ROWS):
        needed = (a0 + per - 1 - blk * FAR_ROWS > FAR_FROM // FOLD) & \
                 (a0 - (blk * FAR_ROWS + FAR_ROWS - 1) <= DILATED_WINDOW // FOLD)
        pl.when(needed)(functools.partial(far_block, blk))
    _flash_write(acc_s, o_ref)


def _dilated(p2, gq, gk, batch, seq):
    nq = seq // TQ
    t = batch * seq
    assert FAR_FROM % FOLD == 0 and TQ % FOLD == 0 and (seq // FOLD) % FAR_ROWS == 0 and (TQ // FOLD) * CLS_PER_GROUP == LANES
    bias_tiles = jnp.asarray(_band_bias_tiles(FAR_FROM, _near_mult, folded=True))
    pair_cols = lambda rows, col: [pl.BlockSpec((rows, LANES), functools.partial(
        lambda b, i, c: (b if rows == seq else b * nq + i, c), c=col // LANES + j)) for j in range(2)]
    gain = pl.BlockSpec((1, LANES), lambda b, i: (0, 0))
    return pl.pallas_call(
        functools.partial(_dilated_kernel, seq=seq),
        grid=(batch, nq),
        in_specs=pair_cols(TQ, COL_QD) + pair_cols(seq, COL_KD) + pair_cols(seq, COL_VD)
                 + [gain, gain, pl.BlockSpec(bias_tiles.shape, lambda b, i: (0, 0, 0))],
        out_specs=pl.BlockSpec((1, GROUP_W, TQ), lambda b, i: (b * nq + i, 0, 0)),
        out_shape=jax.ShapeDtypeStruct((t // TQ, GROUP_W, TQ), F32),
        scratch_shapes=[pltpu.VMEM((seq, N_HEADS * LANES), BF16),
                        pltpu.VMEM((seq // TK, N_HEADS * V_ROWS, TK), BF16),
                        pltpu.VMEM((N_HEADS, seq // FOLD, FOLD * LANES), BF16),
                        pltpu.VMEM((N_HEADS, V_ROWS, seq), BF16),
                        pltpu.VMEM((N_HEADS, TQ, LANES), BF16),
                        pltpu.VMEM((N_HEADS, TQ // LANES, LANES, CLS_PER_GROUP * LANES), BF16),
                        pltpu.VMEM((N_HEADS, TK, TQ), F32),
                        pltpu.VMEM((N_HEADS, V_ROWS, TQ), F32),
                        pltpu.VMEM((N_HEADS, 1, TQ), F32)],
        compiler_params=pltpu.CompilerParams(dimension_semantics=("parallel", "arbitrary"),
                                             vmem_limit_bytes=VMEM_LIMIT),
        name="dilated",
    )(p2, p2, p2, p2, p2, p2, gq, gk, bias_tiles)


def _stick_kernel(q_ref, k_ref, v_ref, o_ref, kp_s, vt_s, qp_s, sp_s, e_s, acc_s, *, seq):
    TQ, N_TOP = STICK_TQ, STICK_TQ // TK
    qi = pl.program_id(1)
    nkt = seq // TK

    @pl.when(qi == 0)
    def _prep():
        for c in range(nkt):
            rows = slice(c * TK, (c + 1) * TK)
            kb = k_ref[rows, :]
            for i in range(N_HEADS // 2):
                even, odd = _pair_keys(kb[:, i * LANES:(i + 1) * LANES], None)
                kp_s[rows, 2 * i * LANES:(2 * i + 1) * LANES] = even
                kp_s[rows, (2 * i + 1) * LANES:(2 * i + 2) * LANES] = odd
            vt_s[c] = v_ref[rows, :].T.astype(BF16)

    _prep_queries(q_ref[...], None, qi * TQ, None, qp_s)
    half = TK // 2
    col = lax.broadcasted_iota(jnp.int32, (half + SUM_PAD, half), 1)
    srow = lax.broadcasted_iota(jnp.int32, (half + SUM_PAD, half), 0)
    sums = ((srow == half) | ((srow < half) & (col > srow))).astype(BF16)
    k_local = lax.broadcasted_iota(jnp.int32, (TK, TQ), 0)
    q_local = lax.broadcasted_iota(jnp.int32, (TK, TQ), 1)
    top_kt = qi * N_TOP + (N_TOP - 1)
    acc_s[...] = jnp.zeros(acc_s.shape, F32)

    def past_mask(tiles_above):
        return tiles_above * TK + k_local < q_local

    def logits(kt, h):
        off = pl.multiple_of(kt * TK, TK)
        return lax.dot_general(kp_s[pl.ds(off, TK), h * LANES:(h + 1) * LANES], qp_s[h // 2], _NT,
                               preferred_element_type=F32)

    def stage(h, z, past):
        sp = jnp.maximum(z, 0.0) + jnp.log(1.0 + jnp.exp2(jnp.abs(z) * (-LOG2E)))
        spm = sp if past is None else jnp.where(past, sp, 0.0)
        sp_s[h] = spm.astype(BF16)
        logsig = z - sp
        e_s[h] = logsig if past is None else jnp.where(past, logsig, NEG_INF)

    def tails(h):
        return [jnp.dot(sums, sp_s[h, b * half:(b + 1) * half, :], preferred_element_type=F32) for b in range(2)]

    def consume(kt, h, w, later):
        later_lo = later + w[1][half:half + 1, :]
        attn = jnp.concatenate([jnp.exp(e_s[h, 0:half, :] - w[0][0:half, :] - later_lo),
                                jnp.exp(e_s[h, half:TK, :] - w[1][0:half, :] - later)], axis=0)
        acc_s[h] += jnp.dot(vt_s[kt, h * HEAD_DIM:(h + 1) * HEAD_DIM, :], attn.astype(BF16),
                            preferred_element_type=F32)
        return later_lo + w[0][half:half + 1, :]

    ahead = MXU_AHEAD
    for h in range(N_HEADS):
        stage(h, logits(top_kt, h), past_mask(N_TOP - 1))

    def step(kt, carry, past):
        laters, w_first, z_first = carry
        w = {0: list(w_first)}
        z = {0: z_first}
        for h in range(1, ahead):
            w[h] = tails(h)
            z[h] = logits(kt - 1, h)
        out = []
        for h in range(N_HEADS):
            out.append(consume(kt, h, w.pop(h), laters[h]))
            stage(h, z.pop(h), past)
            if h + ahead < N_HEADS:
                w[h + ahead] = tails(h + ahead)
                z[h + ahead] = logits(kt - 1, h + ahead)
            elif h + ahead == N_HEADS:
                nxt = (tuple(tails(0)), logits(jnp.maximum(kt - 2, 0), 0))
        return tuple(out), nxt[0], nxt[1]

    carry = (tuple(jnp.zeros((1, TQ), F32) for _ in range(N_HEADS)),
             tuple(tails(0)), logits(jnp.maximum(top_kt - 1, 0), 0))
    for j in range(N_TOP - 1):
        carry = step(top_kt - j, carry, past_mask(N_TOP - 2 - j))
    below = qi * N_TOP
    laters, w_first, _ = lax.fori_loop(0, below, lambda i, c: step(below - i, c, None), carry)
    w = {0: list(w_first)}
    for h in range(1, ahead):
        w[h] = tails(h)
    for h in range(N_HEADS):
        consume(0, h, w.pop(h), laters[h])
        if h + ahead < N_HEADS:
            w[h + ahead] = tails(h + ahead)
    o_ref[0] = acc_s[...].reshape(GROUP_W, TQ)


def _stick(p2, batch, seq):
    TQ = STICK_TQ
    nq = seq // TQ
    t = batch * seq
    return pl.pallas_call(
        functools.partial(_stick_kernel, seq=seq),
        grid=(batch, nq),
        in_specs=[pl.BlockSpec((TQ, GROUP_W), lambda b, i: (b * nq + i, COL_QC // GROUP_W)),
                  pl.BlockSpec((seq, GROUP_W), lambda b, i: (b, COL_KC // GROUP_W)),
                  pl.BlockSpec((seq, GROUP_W), lambda b, i: (b, COL_VC // GROUP_W))],
        out_specs=pl.BlockSpec((1, GROUP_W, TQ), lambda b, i: (b * nq + i, 0, 0)),
        out_shape=jax.ShapeDtypeStruct((t // TQ, GROUP_W, TQ), F32),
        scratch_shapes=[pltpu.VMEM((seq, N_HEADS * LANES), BF16),
                        pltpu.VMEM((seq // TK, GROUP_W, TK), BF16),
                        pltpu.VMEM((N_HEADS // 2, TQ, LANES), BF16),
                        pltpu.VMEM((N_HEADS, TK, TQ), BF16),
                        pltpu.VMEM((N_HEADS, TK, TQ), F32),
                        pltpu.VMEM((N_HEADS, HEAD_DIM, TQ), F32)],
        compiler_params=pltpu.CompilerParams(dimension_semantics=("parallel", "arbitrary"),
                                             vmem_limit_bytes=VMEM_LIMIT),
        name="stick_breaking",
    )(p2, p2, p2)


def _mixout_kernel(x_ref, ocmp_ref, oslc_ref, owin_ref, gate_ref, cvb_ref, cvc_ref, cvu_ref, pc_ref, pu_ref,
                   oc_ref, od_ref, bg_ref, cw_ref, gout_ref, wout_ref, o_ref, od_s, *, tiles_per_seq):
    i = pl.program_id(0)
    tm, d = x_ref.shape
    chains = [slice(c * MIX_CHAIN, (c + 1) * MIX_CHAIN) for c in range(tm // MIX_CHAIN)]

    def heads(ref, rows):
        tq = ref.shape[2]
        return ref[rows.start // tq, :, rows.start % tq:rows.start % tq + MIX_CHAIN].T

    per = TQ // FOLD
    for tile in range(tm // TQ):
        folded = od_ref[tile].T
        for half in range(GROUP_W // LANES):
            od_s[tile, half] = folded[:, half * LANES:(half + 1) * LANES]

    def unfolded(rows):
        tile, first = rows.start // TQ, (rows.start % TQ) // FOLD
        return jnp.concatenate(
            [jnp.concatenate([od_s[tile, half, pl.ds(a, FOLD, stride=per), :] for a in range(first, first + MIX_CHAIN // FOLD)],
                             axis=0) for half in range(GROUP_W // LANES)], axis=1)

    cu = cvc_ref[...] * cvu_ref[...]
    prev = jnp.where(i % tiles_per_seq == 0, 0.0, pc_ref[...] * pu_ref[...])
    full = jnp.concatenate([prev, cu], axis=0)
    back1 = pltpu.roll(full, 1, 0)[8:, :]
    back2 = pltpu.roll(full, 2, 0)[8:, :]
    cw = cw_ref[...]
    ob = cvb_ref[...] * (cw[0:1, :] * back2 + cw[1:2, :] * back1 + cw[2:3, :] * cu)

    gates = jax.nn.sigmoid(gate_ref[...] + bg_ref[...])
    src = lax.broadcasted_iota(jnp.int32, (LANES, GROUP_W), 0)
    head3 = (lax.broadcasted_iota(jnp.int32, (LANES, GROUP_W), 1) >> 6) * 3
    branch = [(src == head3 + r).astype(BF16) for r in range(3)]
    assert d // HEAD_DIM <= LANES
    gather = ((lax.broadcasted_iota(jnp.int32, (d, LANES), 0) >> 6)
              == lax.broadcasted_iota(jnp.int32, (d, LANES), 1)).astype(BF16)
    spread = (lax.broadcasted_iota(jnp.int32, (LANES, d), 0)
              == (lax.broadcasted_iota(jnp.int32, (LANES, d), 1) >> 6)).astype(BF16)

    groups, ssq = [], []
    for rows in chains:
        oa = (_exact_dot(gates[rows], branch[0]) * heads(ocmp_ref, rows)
              + _exact_dot(gates[rows], branch[1]) * heads(oslc_ref, rows)
              + _exact_dot(gates[rows], branch[2]) * heads(owin_ref, rows))
        g = jnp.concatenate([oa, ob[rows], heads(oc_ref, rows), unfolded(rows)], axis=1)
        groups.append(g)
        ssq.append(_exact_dot(g * g, gather))
    scale = [_exact_dot(lax.rsqrt(s * (1.0 / HEAD_DIM) + RMS_EPS), spread) for s in ssq]
    for rows, g, sc in zip(chains, groups, scale):
        mixed = (g * sc * gout_ref[...]).astype(BF16)
        o_ref[rows, :] = x_ref[rows, :] + jnp.dot(mixed, wout_ref[...], preferred_element_type=F32)


def _mixout(x2, p2, ocmp, oslc, owin, oc, od, bg, cw, gout, wout, seq):
    t, d = x2.shape
    tm = 2 * MIX_CHAIN
    rows = lambda w, col: pl.BlockSpec((tm, w), lambda i: (i, col // w))
    heads_t = lambda tq: pl.BlockSpec((tm // tq, GROUP_W, tq), lambda i: (i, 0, 0))
    prev8 = lambda col: pl.BlockSpec((8, GROUP_W), lambda i: (jnp.maximum(i * (tm // 8) - 1, 0), col // GROUP_W))
    const = lambda shape: pl.BlockSpec(shape, lambda i: (0, 0))
    assert tm % TQ == 0 and tm % STICK_TQ == 0
    return pl.pallas_call(
        functools.partial(_mixout_kernel, tiles_per_seq=seq // tm),
        grid=(t // tm,),
        in_specs=[rows(d, 0), heads_t(TQ), heads_t(TQ), heads_t(TQ),
                  rows(LANES, COL_GATE), rows(GROUP_W, COL_CVB), rows(GROUP_W, COL_CVC), rows(GROUP_W, COL_CVU),
                  prev8(COL_CVC), prev8(COL_CVU),
                  heads_t(STICK_TQ), heads_t(TQ),
                  const((1, LANES)), const((8, GROUP_W)), const((1, d)), const((d, d))],
        out_specs=rows(d, 0),
        out_shape=jax.ShapeDtypeStruct((t, d), F32),
        scratch_shapes=[pltpu.VMEM((tm // TQ, GROUP_W // LANES, TQ, LANES), F32)],
        compiler_params=pltpu.CompilerParams(dimension_semantics=("parallel",), vmem_limit_bytes=VMEM_LIMIT),
        name="mixout",
    )(x2, ocmp, oslc, owin, p2, p2, p2, p2, p2, p2, oc, od, bg, cw, gout, wout)


def _ffn_kernel(x_ref, g_ref, wg_ref, wu_ref, wd_ref, o_ref, act_s, *, ff_chunk, out_chunk):
    h = _rms(x_ref[...], g_ref[...]).astype(BF16)
    dff = wg_ref.shape[1]
    for c in range(dff // ff_chunk):
        cols = slice(c * ff_chunk, (c + 1) * ff_chunk)
        a = jnp.dot(h, wg_ref[:, cols], preferred_element_type=F32)
        u = jnp.dot(h, wu_ref[:, cols], preferred_element_type=F32)
        act_s[:, cols] = (a * jax.nn.sigmoid(a) * u).astype(BF16)
    d = o_ref.shape[1]
    for c in range(d // out_chunk):
        cols = slice(c * out_chunk, (c + 1) * out_chunk)
        o_ref[:, cols] = x_ref[:, cols] + jnp.dot(act_s[...], wd_ref[:, cols], preferred_element_type=F32)


def _ffn(x2, g, wg, wu, wd):
    t, d = x2.shape
    dff = wg.shape[1]
    tm = 1024
    resident = lambda shape: pl.BlockSpec(shape, lambda i: (0, 0), pipeline_mode=pl.Buffered(1))
    return pl.pallas_call(
        functools.partial(_ffn_kernel, ff_chunk=256, out_chunk=256),
        grid=(t // tm,),
        in_specs=[pl.BlockSpec((tm, d), lambda i: (i, 0)),
                  pl.BlockSpec((1, d), lambda i: (0, 0)),
                  resident((d, dff)), resident((d, dff)), resident((dff, d))],
        out_specs=pl.BlockSpec((tm, d), lambda i: (i, 0)),
        out_shape=jax.ShapeDtypeStruct((t, d), F32),
        scratch_shapes=[pltpu.VMEM((tm, dff), BF16)],
        compiler_params=pltpu.CompilerParams(dimension_semantics=("parallel",), vmem_limit_bytes=VMEM_LIMIT),
        name="ffn",
    )(x2, g, wg, wu, wd)


def _permute_w_in_kernel(w_ref, o_ref):
    rows = w_ref.shape[0]
    n_rest = P_COLS - COL_CVB
    o_ref[:, 0:COL_GATE] = w_ref[:, 0:COL_GATE].astype(BF16)
    lane = lax.broadcasted_iota(jnp.int32, (rows, LANES), 1)
    o_ref[:, COL_GATE:COL_GATE + LANES] = jnp.where(lane < N_GATES, w_ref[:, COL_GATE:COL_GATE + LANES], 0.0).astype(BF16)
    o_ref[:, COL_CVB:P_COLS] = w_ref[:, COL_GATE + N_GATES:COL_GATE + N_GATES + n_rest].astype(BF16)


def _permute_w_in(w_in):
    d, cols = w_in.shape
    tr = 256
    return pl.pallas_call(
        _permute_w_in_kernel,
        grid=(d // tr,),
        in_specs=[pl.BlockSpec((tr, cols), lambda i: (i, 0))],
        out_specs=pl.BlockSpec((tr, P_COLS), lambda i: (i, 0)),
        out_shape=jax.ShapeDtypeStruct((d, P_COLS), BF16),
        compiler_params=pltpu.CompilerParams(dimension_semantics=("parallel",), vmem_limit_bytes=VMEM_LIMIT),
        name="permute_w_in",
    )(w_in)


def _layer(x2, batch, seq, g_mix, w_in, b_gate, g_q_nsa, g_k_cmp, g_k_slc, g_k_win, pe_k_cmp, pe_v_cmp,
           w1_k_cmp, w2_k_cmp, w1_v_cmp, w2_v_cmp, conv_w, g_q_dil, g_k_dil, g_out, w_out,
           g_ffn, w_gate, w_up, w_down):
    row = lambda v: v.reshape(1, -1)
    per_group = lambda g, n: jnp.tile(g, n).reshape(1, -1)
    p2 = _inproj(x2, row(g_mix), _permute_w_in(w_in))

    pe, w1, w2 = _compress_weights(pe_k_cmp, pe_v_cmp, w1_k_cmp, w2_k_cmp, w1_v_cmp, w2_v_cmp)
    kc, vc = _compress(p2, pe, w1, w2, per_group(g_k_cmp, 2), batch, seq)
    ocmp, oslc, owin = _nsa(p2, kc, vc, per_group(g_q_nsa, N_HEADS), per_group(g_k_slc, 2), per_group(g_k_win, 2),
                            batch, seq)
    od = _dilated(p2, per_group(g_q_dil, 2), per_group(g_k_dil, 2), batch, seq)
    oc = _stick(p2, batch, seq)

    bg = jnp.zeros((1, LANES), F32).at[0, :N_GATES].set(b_gate)
    cwp = jnp.zeros((8, GROUP_W), F32).at[:CONV_K].set(conv_w)
    x1 = _mixout(x2, p2, ocmp, oslc, owin, oc, od, bg, cwp, row(g_out), w_out.astype(BF16), seq)
    return _ffn(x1, row(g_ffn), w_gate.astype(BF16), w_up.astype(BF16), w_down.astype(BF16))


def kernel(x, g_mix, w_in, b_gate, g_q_nsa, g_k_cmp, g_k_slc, g_k_win, pe_k_cmp, pe_v_cmp, w1_k_cmp, w2_k_cmp,
           w1_v_cmp, w2_v_cmp, conv_w, g_q_dil, g_k_dil, g_out, w_out, g_ffn, w_gate, w_up, w_down):
    batch, seq, d = x.shape
    assert seq % (CMP_STRIDE * LANES) == 0 and d % LANES == 0
    assert seq <= 64 * 256
    x2 = x.reshape(batch * seq, d)
    params = (g_mix, w_in, b_gate, g_q_nsa, g_k_cmp, g_k_slc, g_k_win, pe_k_cmp, pe_v_cmp, w1_k_cmp, w2_k_cmp,
              w1_v_cmp, w2_v_cmp, conv_w, g_q_dil, g_k_dil, g_out, w_out, g_ffn, w_gate, w_up, w_down)
    for layer in range(g_mix.shape[0]):
        x2 = _layer(x2, batch, seq, *[p[layer] for p in params])
    return x2.reshape(batch, seq, d)
```

```python
import functools

import numpy as np
import jax
import jax.numpy as jnp
from jax import lax
from jax.experimental import pallas as pl
from jax.experimental.pallas import tpu as pltpu

F32 = jnp.float32
BF16 = jnp.bfloat16

HEAD_DIM = 64
N_HEADS = 4
GROUP_W = N_HEADS * HEAD_DIM
CONV_K = 3
CMP_LEN = 32
CMP_STRIDE = 16
SEL_BLOCK = 64
N_SELECT = 16
NSA_WINDOW = 512
DILATED_CONFIGS = ((128, 1), (512, 4), (2048, 16))
DILATED_WINDOW = max(w for w, _ in DILATED_CONFIGS)
FOLD = 16
FAR_FROM = 512
FAR_ROWS = 128
NEG_INF = -1e30
FORCE_SCORE = 1e6
RMS_EPS = 1e-6

TQ = 512
TK = 256
N_TOP = TQ // TK
CLS_PER_GROUP = 128 * FOLD // TQ
STICK_TQ = 256
LANES = 128
MXU_DEPTH = 256
MIX_CHAIN = 256
MXU_AHEAD = 2
FLASH_AHEAD = 1
SUM_PAD = 16
V_ROWS = HEAD_DIM + SUM_PAD
LOG2E = 1.4426950408889634
VMEM_LIMIT = 52 * 1024 * 1024

COL_QA, COL_KVC, COL_KSV, COL_KWV, COL_GATE = 0, 256, 384, 512, 640
COL_CVB, COL_CVC, COL_CVU = 768, 1024, 1280
COL_QC, COL_KC, COL_VC = 1536, 1792, 2048
COL_QD, COL_KD, COL_VD = 2304, 2560, 2816
P_COLS = 3072
N_GATES = 12

_NT = (((1,), (1,)), ((), ()))


def _alibi_slopes():
    s = [2.0 ** (-8.0 * i / 8) for i in range(1, 9)]
    return tuple(s[0::2]), tuple(s[1::2])


SLOPES_A, SLOPES_D = _alibi_slopes()


def _rms(x, g):
    return x * lax.rsqrt(jnp.mean(x * x, axis=-1, keepdims=True) + RMS_EPS) * g


def _exact_dot(a, sel):
    hi = a.astype(BF16)
    lo = (a - hi.astype(F32)).astype(BF16)
    if 2 * a.shape[1] <= MXU_DEPTH:
        return jnp.dot(jnp.concatenate([hi, lo], axis=1), jnp.concatenate([sel, sel], axis=0),
                       preferred_element_type=F32)
    return jnp.dot(hi, sel, preferred_element_type=F32) + jnp.dot(lo, sel, preferred_element_type=F32)


def _group_rms(x, g):
    w = x.shape[1]
    same = ((lax.broadcasted_iota(jnp.int32, (w, w), 0) >> 6)
            == (lax.broadcasted_iota(jnp.int32, (w, w), 1) >> 6)).astype(BF16)
    ssq = _exact_dot(x * x, same)
    return x * lax.rsqrt(ssq * (1.0 / HEAD_DIM) + RMS_EPS) * g


EVEN_AUG, ODD_AUG = HEAD_DIM, 0


def _key_aug(pos, first):
    lane = lax.broadcasted_iota(jnp.int32, (pos.shape[0], LANES), 1) - first
    hi = (pos >> 6).astype(F32)
    lo = (pos & 63).astype(F32)
    return jnp.where(lane == 0, hi, jnp.where(lane == 1, lo, jnp.where((lane == 2) | (lane == 3), 1.0, 0.0)))


def _query_aug(t, slope, first):
    lane = lax.broadcasted_iota(jnp.int32, (t.shape[0], LANES), 1) - first
    hi = (t >> 6).astype(F32) * (-64.0 * slope)
    lo = (t & 63).astype(F32) * (-slope)
    return jnp.where(lane == 0, 64.0 * slope,
                     jnp.where(lane == 1, slope, jnp.where(lane == 2, hi, jnp.where(lane == 3, lo, 0.0))))


def _pair_keys(pair, pos):
    lane = lax.broadcasted_iota(jnp.int32, pair.shape, 1)
    aug_e = 0.0 if pos is None else _key_aug(pos, EVEN_AUG)
    aug_o = 0.0 if pos is None else _key_aug(pos, ODD_AUG)
    return (jnp.where(lane < HEAD_DIM, pair, aug_e).astype(BF16),
            jnp.where(lane >= HEAD_DIM, pair, aug_o).astype(BF16))


def _prep_queries(q, g, t0, slopes, qp_s):
    qn = (q if g is None else _group_rms(q, g)) * (HEAD_DIM ** -0.5)
    tpos = t0 + lax.broadcasted_iota(jnp.int32, (q.shape[0], 1), 0)
    lane = lax.broadcasted_iota(jnp.int32, (q.shape[0], LANES), 1)
    for i in range(N_HEADS // 2):
        pair = qn[:, i * LANES:(i + 1) * LANES]
        if slopes is None:
            qp_s[i] = pair.astype(BF16)
            continue
        even = jnp.where((lane >= EVEN_AUG) & (lane < EVEN_AUG + 4), _query_aug(tpos, slopes[2 * i], EVEN_AUG), pair)
        odd = jnp.where(lane < ODD_AUG + 4, _query_aug(tpos, slopes[2 * i + 1], ODD_AUG), pair)
        qp_s[2 * i] = even.astype(BF16)
        qp_s[2 * i + 1] = odd.astype(BF16)


def _inproj_kernel(x_ref, g_ref, w_ref, o_ref, *, nchunk):
    h = _rms(x_ref[...], g_ref[...]).astype(BF16)
    cw = P_COLS // nchunk
    for c in range(nchunk):
        o_ref[:, c * cw:(c + 1) * cw] = jnp.dot(h, w_ref[:, c * cw:(c + 1) * cw], preferred_element_type=F32)


def _inproj(x2, g, w):
    t, d = x2.shape
    tm = 1024
    return pl.pallas_call(
        functools.partial(_inproj_kernel, nchunk=6),
        grid=(t // tm,),
        in_specs=[pl.BlockSpec((tm, d), lambda i: (i, 0)),
                  pl.BlockSpec((1, d), lambda i: (0, 0)),
                  pl.BlockSpec((d, P_COLS), lambda i: (0, 0), pipeline_mode=pl.Buffered(1))],
        out_specs=pl.BlockSpec((tm, P_COLS), lambda i: (i, 0)),
        out_shape=jax.ShapeDtypeStruct((t, P_COLS), F32),
        compiler_params=pltpu.CompilerParams(dimension_semantics=("parallel",), vmem_limit_bytes=VMEM_LIMIT),
        name="inproj",
    )(x2, g, w)


def _gelu_tanh(x):
    return x * (0.5 * (1.0 + jnp.tanh(np.sqrt(2.0 / np.pi).astype(np.float32) * (x + 0.044715 * (x * x * x)))))


def _compress_kernel(kv_ref, pe_ref, w1_ref, w2_ref, gk_ref, kc_ref, vct_ref):
    nc = kv_ref.shape[0] // CMP_STRIDE
    first = None
    second = None
    for j in range(CMP_STRIDE):
        tok = kv_ref[pl.ds(j, nc, stride=CMP_STRIDE), :]
        a = jnp.dot((tok + pe_ref[j:j + 1, :]).astype(BF16), w1_ref[j], preferred_element_type=F32)
        b = jnp.dot((tok + pe_ref[CMP_STRIDE + j:CMP_STRIDE + j + 1, :]).astype(BF16), w1_ref[CMP_STRIDE + j],
                    preferred_element_type=F32)
        first = a if first is None else first + a
        second = b if second is None else second + b
    hid = first + pltpu.roll(second, nc - 1, 0)
    out = jnp.dot(_gelu_tanh(hid).astype(BF16), w2_ref[...], preferred_element_type=F32)
    kn = _group_rms(out, gk_ref[...])
    end = lax.broadcasted_iota(jnp.int32, (nc, 1), 0) * CMP_STRIDE + (CMP_LEN - 1)
    even, _ = _pair_keys(kn, end)
    _, odd = _pair_keys(pltpu.roll(kn, HEAD_DIM, 1), end)
    kc_ref[0] = jnp.concatenate([even, odd], axis=1)
    vct_ref[0] = out.T[HEAD_DIM:, :].astype(BF16)


def _compress(p2, pe, w1, w2, gk, batch, seq):
    nc = seq // CMP_STRIDE
    hid2 = w1.shape[2]
    return pl.pallas_call(
        _compress_kernel,
        grid=(batch,),
        in_specs=[pl.BlockSpec((seq, LANES), lambda i: (i, COL_KVC // LANES)),
                  pl.BlockSpec((CMP_LEN, LANES), lambda i: (0, 0)),
                  pl.BlockSpec((CMP_LEN, LANES, hid2), lambda i: (0, 0, 0)),
                  pl.BlockSpec((hid2, LANES), lambda i: (0, 0)),
                  pl.BlockSpec((1, LANES), lambda i: (0, 0))],
        out_specs=[pl.BlockSpec((1, nc, 2 * LANES), lambda i: (i, 0, 0)),
                   pl.BlockSpec((1, HEAD_DIM, nc), lambda i: (i, 0, 0))],
        out_shape=[jax.ShapeDtypeStruct((batch, nc, 2 * LANES), BF16),
                   jax.ShapeDtypeStruct((batch, HEAD_DIM, nc), BF16)],
        compiler_params=pltpu.CompilerParams(dimension_semantics=("parallel",), vmem_limit_bytes=VMEM_LIMIT),
        name="nsa_compress",
    )(p2, pe, w1, w2, gk)


def _compress_weights(pe_k, pe_v, w1_k, w2_k, w1_v, w2_v):
    hid = w1_k.shape[1]
    w1k = w1_k.reshape(CMP_LEN, HEAD_DIM, hid)
    w1v = w1_v.reshape(CMP_LEN, HEAD_DIM, hid)
    z1 = jnp.zeros_like(w1k)
    w1 = jnp.concatenate([jnp.concatenate([w1k, z1], axis=2), jnp.concatenate([z1, w1v], axis=2)], axis=1)
    z2 = jnp.zeros_like(w2_k)
    w2 = jnp.concatenate([jnp.concatenate([w2_k, z2], axis=1), jnp.concatenate([z2, w2_v], axis=1)], axis=0)
    pe = jnp.concatenate([pe_k, pe_v], axis=1)
    return pe, w1.astype(BF16), w2.astype(BF16)


ALL_LANES = slice(0, TQ)


def _with_lanes(full, lanes, part):
    pieces = ([full[:, :lanes.start]] if lanes.start > 0 else []) + [part] + \
             ([full[:, lanes.stop:]] if lanes.stop < TQ else [])
    return jnp.concatenate(pieces, axis=1) if len(pieces) > 1 else part


def _flash_scores(kp_of, bias, qp_s, s_s, m_cur, lanes=ALL_LANES):
    m_next = []
    for h in range(N_HEADS):
        s = lax.dot_general(kp_of(h), qp_s[h, lanes, :], _NT, preferred_element_type=F32) + bias[:, lanes]
        s_s[h, :, lanes] = s
        m_next.append(_with_lanes(m_cur[h], lanes, jnp.maximum(m_cur[h][:, lanes], jnp.max(s, axis=0, keepdims=True))))
    return tuple(m_next)


def _flash_accum(vt_of, s_s, acc_s, m_prev, m_cur, lanes=ALL_LANES):
    for h in range(N_HEADS):
        alpha = jnp.exp(m_prev[h][:, lanes] - m_cur[h][:, lanes])
        p = jnp.exp(s_s[h, :, lanes] - m_cur[h][:, lanes])
        acc_s[h, :, lanes] = alpha * acc_s[h, :, lanes] + jnp.dot(vt_of(h), p.astype(BF16),
                                                                 preferred_element_type=F32)


def _flash_accum_and_scores(kp_of, bias, vt_of, qp_s, s_s, acc_s, m_prev, m_cur, acc_lanes=ALL_LANES,
                            new_lanes=ALL_LANES):
    def scores(h):
        return (lax.dot_general(kp_of(h), qp_s[h, new_lanes, :], _NT, preferred_element_type=F32)
                + bias[:, new_lanes])

    m_next = []
    s_new = {h: scores(h) for h in range(FLASH_AHEAD)}
    for h in range(N_HEADS):
        alpha = jnp.exp(m_prev[h][:, acc_lanes] - m_cur[h][:, acc_lanes])
        p = jnp.exp(s_s[h, :, acc_lanes] - m_cur[h][:, acc_lanes])
        acc_s[h, :, acc_lanes] = alpha * acc_s[h, :, acc_lanes] + jnp.dot(vt_of(h), p.astype(BF16),
                                                                         preferred_element_type=F32)
        s_h = s_new.pop(h)
        s_s[h, :, new_lanes] = s_h
        m_next.append(_with_lanes(m_cur[h], new_lanes,
                                  jnp.maximum(m_cur[h][:, new_lanes], jnp.max(s_h, axis=0, keepdims=True))))
        if h + FLASH_AHEAD < N_HEADS:
            s_new[h + FLASH_AHEAD] = scores(h + FLASH_AHEAD)
    return tuple(m_next)


def _flash_pipelined(tile_of, n_tiles, kp_fn, bias_fn, vt_fn, qp_s, s_s, acc_s, o_ref, causal_lanes=True):
    top_lanes = [slice((N_TOP - 1 - i) * TK if causal_lanes else 0, TQ) for i in range(N_TOP)]
    acc_s[...] = jnp.zeros(acc_s.shape, F32)
    m0 = tuple(jnp.full((1, TQ), NEG_INF, F32) for _ in range(N_HEADS))
    first_kt = tile_of(0)
    m1 = _flash_scores(kp_fn(first_kt), bias_fn(first_kt, True), qp_s, s_s, m0, top_lanes[0])

    def step(i, carry, top):
        m_prev, m_cur = carry
        nxt = tile_of(i + 1)
        lanes = (top_lanes[i], top_lanes[i + 1]) if top else (ALL_LANES, ALL_LANES)
        m_next = _flash_accum_and_scores(kp_fn(nxt), bias_fn(nxt, top), vt_fn(tile_of(i)), qp_s, s_s, acc_s,
                                         m_prev, m_cur, *lanes)
        return m_cur, m_next

    carry = (m0, m1)
    for i in range(N_TOP - 1):
        carry = step(i, carry, True)
    m_prev, m_cur = lax.fori_loop(N_TOP - 1, n_tiles - 1, lambda i, c: step(i, c, False), carry)
    _flash_accum(vt_fn(tile_of(n_tiles - 1)), s_s, acc_s, m_prev, m_cur)
    if o_ref is None:
        return m_cur
    _flash_write(acc_s, o_ref)


def _flash_write(acc_s, o_ref):
    o_ref[0] = jnp.concatenate([acc_s[h, 0:HEAD_DIM, :] * (1.0 / acc_s[h, HEAD_DIM:HEAD_DIM + 1, :])
                                for h in range(N_HEADS)], axis=0)


def _values_t(v_t):
    row = lax.broadcasted_iota(jnp.int32, (V_ROWS - HEAD_DIM, v_t.shape[1]), 0)
    return jnp.concatenate([v_t, jnp.where(row == 0, 1.0, 0.0)], axis=0).astype(BF16)


def _nsa_kernel(q_ref, kc_ref, vct_ref, ksv_ref, kwv_ref, gq_ref, gks_ref, gkw_ref, wbias_ref,
                ocmp_ref, oslc_ref, owin_ref,
                ksp_s, vst_s, kwp_s, vwt_s, qp_s, imp_s, sel_s, s_s, acc_s, hit_s, *, seq):
    qi = pl.program_id(1)
    nc = seq // CMP_STRIDE
    nsel = seq // SEL_BLOCK
    nkt = seq // TK

    @pl.when(qi == 0)
    def _prep():
        for c in range(nkt):
            rows = slice(c * TK, (c + 1) * TK)
            pos = c * TK + lax.broadcasted_iota(jnp.int32, (TK, 1), 0)
            for kv_ref, g_ref, kp_s, vt_s in ((ksv_ref, gks_ref, ksp_s, vst_s), (kwv_ref, gkw_ref, kwp_s, vwt_s)):
                blk = kv_ref[rows, :]
                kn = _group_rms(blk, g_ref[...])
                kp_s[rows, 0:LANES], _ = _pair_keys(kn, pos)
                _, kp_s[rows, LANES:2 * LANES] = _pair_keys(pltpu.roll(kn, HEAD_DIM, 1), pos)
                vt_s[c] = _values_t(blk.T[HEAD_DIM:, :])
        imp_s[:, 0:8, :] = jnp.zeros((TQ // LANES, 8, LANES), F32)

    t0 = qi * TQ
    _prep_queries(q_ref[...], gq_ref[...], t0, SLOPES_A, qp_s)

    kc = [kc_ref[0, :, 0:LANES], kc_ref[0, :, LANES:2 * LANES]]
    vct = vct_ref[0]
    tq_row = t0 + lax.broadcasted_iota(jnp.int32, (nc, TQ), 1)
    n_idx = lax.broadcasted_iota(jnp.int32, (nc, TQ), 0)
    vis = (tq_row >= n_idx * CMP_STRIDE + (CMP_LEN - 1)) & (n_idx < nc - 1)
    vis_bias = jnp.where(vis, 0.0, NEG_INF)
    sees_any = (t0 + lax.broadcasted_iota(jnp.int32, (1, TQ), 1) >= CMP_LEN - 1).astype(F32)
    imp = jnp.zeros((nc, TQ), F32)
    ocmp_t = []
    scores = [lax.dot_general(kc[h % 2], qp_s[h], _NT, preferred_element_type=F32) + vis_bias
              for h in range(N_HEADS)]
    for h in range(N_HEADS):
        sc = scores[h]
        e = jnp.exp(sc - jnp.max(sc, axis=0, keepdims=True))
        p = e * (sees_any / jnp.sum(e, axis=0, keepdims=True))
        ocmp_t.append(jnp.dot(vct, p.astype(BF16), preferred_element_type=F32))
        imp = imp + p
    ocmp_ref[0] = jnp.concatenate(ocmp_t, axis=0)

    halves = []
    for half in range(TQ // LANES):
        imp_s[half, 8:8 + nc, :] = imp[:, half * LANES:(half + 1) * LANES]
        r = [imp_s[half, pl.ds(8 + k, nsel, stride=4), :] for k in range(4)]
        rm1 = imp_s[half, pl.ds(7, nsel, stride=4), :]
        halves.append(rm1 + 2.0 * (r[0] + r[1] + r[2]) + r[3])
    imp_blk = jnp.concatenate(halves, axis=1)
    blk = lax.broadcasted_iota(jnp.int32, (nsel, TQ), 0)
    tl = t0 + lax.broadcasted_iota(jnp.int32, (nsel, TQ), 1)
    cur = tl >> 6
    forced = (blk == 0) | (blk == cur) | (blk == cur - 1)
    valid = blk * SEL_BLOCK <= tl
    score = jnp.where(forced, FORCE_SCORE, jnp.where(valid, imp_blk, -FORCE_SCORE))
    sub = 8
    per_tile = TK // SEL_BLOCK
    tile_hit = []
    for g in range(nsel // sub):
        mine = score[g * sub:(g + 1) * sub, :]
        blk_g = g * sub + lax.broadcasted_iota(jnp.int32, (sub, TQ), 0)
        rank = jnp.zeros((sub, TQ), F32)
        for j in range(nsel):
            row = score[j:j + 1, :]
            if j < g * sub:
                beats = row >= mine
            elif j >= (g + 1) * sub:
                beats = row > mine
            else:
                beats = (row > mine) | ((row == mine) & (blk_g > j))
            rank = rank + jnp.where(beats, 1.0, 0.0)
        sel_g = jnp.where(rank < min(N_SELECT, nsel), 0.0, NEG_INF)
        sel_s[g * sub:(g + 1) * sub, :] = sel_g
        for part in range(sub // per_tile):
            tile_hit.append(jnp.max(sel_g[part * per_tile:(part + 1) * per_tile, :]))

    n_hit = jnp.int32(0)
    for kt in reversed(range(nkt)):
        hit_s[n_hit] = kt
        n_hit = n_hit + ((tile_hit[kt] > -1.0) & (kt * TK < t0)).astype(jnp.int32)

    def block_bias(kt):
        return jnp.concatenate([jnp.broadcast_to(sel_s[pl.ds(kt * per_tile + j, 1), :], (SEL_BLOCK, TQ))
                                for j in range(per_tile)], axis=0)

    k_local = lax.broadcasted_iota(jnp.int32, (TK, TQ), 0)
    q_local = lax.broadcasted_iota(jnp.int32, (TK, TQ), 1)
    top_kt = qi * N_TOP + (N_TOP - 1)

    def kp_fn(kt):
        off = pl.multiple_of(kt * TK, TK)
        return lambda h: ksp_s[pl.ds(off, TK), (h % 2) * LANES:(h % 2 + 1) * LANES]

    def bias_fn(kt, top):
        if top:
            return jnp.where(kt * TK - t0 + k_local <= q_local, block_bias(kt), NEG_INF)
        return block_bias(kt)

    tile_of = lambda i: jnp.where(i < N_TOP, top_kt - i, hit_s[jnp.maximum(i - N_TOP, 0)])
    _flash_pipelined(tile_of, n_hit + N_TOP, kp_fn, bias_fn, lambda kt: (lambda h: vst_s[kt]), qp_s, s_s, acc_s,
                     oslc_ref)

    def window_kp_fn(kt):
        off = pl.multiple_of(kt * TK, TK)
        return lambda h: kwp_s[pl.ds(off, TK), (h % 2) * LANES:(h % 2 + 1) * LANES]

    kt_lo = jnp.maximum(t0 - (NSA_WINDOW - 1), 0) // TK
    _flash_pipelined(lambda i: top_kt - i, top_kt + 1 - kt_lo, window_kp_fn, lambda kt, top: wbias_ref[top_kt - kt],
                     lambda kt: (lambda h: vwt_s[kt]), qp_s, s_s, acc_s, owin_ref)


def _nsa(p2, kc, vc, gq, gks, gkw, batch, seq):
    nq = seq // TQ
    nc = seq // CMP_STRIDE
    t = batch * seq
    wbias = jnp.asarray(_band_bias_tiles(NSA_WINDOW - 1, _window_mult))
    head_tiles = pl.BlockSpec((1, GROUP_W, TQ), lambda b, i: (b * nq + i, 0, 0))
    head_shape = jax.ShapeDtypeStruct((t // TQ, GROUP_W, TQ), F32)
    return pl.pallas_call(
        functools.partial(_nsa_kernel, seq=seq),
        grid=(batch, nq),
        in_specs=[pl.BlockSpec((TQ, GROUP_W), lambda b, i: (b * nq + i, COL_QA // GROUP_W)),
                  pl.BlockSpec((1, nc, 2 * LANES), lambda b, i: (b, 0, 0)),
                  pl.BlockSpec((1, HEAD_DIM, nc), lambda b, i: (b, 0, 0)),
                  pl.BlockSpec((seq, LANES), lambda b, i: (b, COL_KSV // LANES)),
                  pl.BlockSpec((seq, LANES), lambda b, i: (b, COL_KWV // LANES)),
                  pl.BlockSpec((1, GROUP_W), lambda b, i: (0, 0)),
                  pl.BlockSpec((1, LANES), lambda b, i: (0, 0)),
                  pl.BlockSpec((1, LANES), lambda b, i: (0, 0)),
                  pl.BlockSpec(wbias.shape, lambda b, i: (0, 0, 0))],
        out_specs=[head_tiles, head_tiles, head_tiles],
        out_shape=[head_shape, head_shape, head_shape],
        scratch_shapes=[pltpu.VMEM((seq, 2 * LANES), BF16),
                        pltpu.VMEM((seq // TK, V_ROWS, TK), BF16),
                        pltpu.VMEM((seq, 2 * LANES), BF16),
                        pltpu.VMEM((seq // TK, V_ROWS, TK), BF16),
                        pltpu.VMEM((N_HEADS, TQ, LANES), BF16),
                        pltpu.VMEM((TQ // LANES, 8 + nc, LANES), F32),
                        pltpu.VMEM((seq // SEL_BLOCK, TQ), F32),
                        pltpu.VMEM((N_HEADS, TK, TQ), F32),
                        pltpu.VMEM((N_HEADS, V_ROWS, TQ), F32),
                        pltpu.SMEM((seq // TK + 1,), jnp.int32)],
        compiler_params=pltpu.CompilerParams(dimension_semantics=("parallel", "arbitrary"),
                                             vmem_limit_bytes=VMEM_LIMIT),
        name="nsa",
    )(p2, kc, vc, p2, p2, gq, gks, gkw, wbias)


def _band_bias_tiles(window, mult_fn, folded=False):
    nd = (window + TQ - 1) // TK + 1
    kk = np.arange(TK)[:, None]
    qq = np.arange(TQ)[None, :]
    if folded:
        qq = FOLD * (qq % (TQ // FOLD)) + qq // (TQ // FOLD)
    tiles = np.empty((nd, TK, TQ), np.float32)
    for di in range(nd):
        d = (di - (N_TOP - 1)) * TK + qq - kk
        mult = mult_fn(d)
        tiles[di] = np.where(mult > 0, np.log(np.maximum(mult, 1)), NEG_INF)
    return tiles


def _window_mult(d):
    return ((d >= 0) & (d <= NSA_WINDOW - 1)).astype(np.float64)


def _dilated_mult(d):
    m = np.zeros(d.shape, np.float64)
    for window, dil in DILATED_CONFIGS:
        m += ((d >= 0) & (d <= window) & (d % dil == 0)).astype(np.float64)
    return m


def _near_mult(d):
    return np.where(d <= FAR_FROM, _dilated_mult(d), 0.0)


def _dilated_kernel(q0_ref, q1_ref, k0_ref, k1_ref, v0_ref, v1_ref, gq_ref, gk_ref, bias_ref, o_ref,
                    kp_s, vt_s, kf_s, vf_s, qp_s, qcat_s, s_s, acc_s, m_s, *, seq):
    qi = pl.program_id(1)
    per = TQ // FOLD
    cls_rows = seq // FOLD
    pairs = ((q0_ref, k0_ref, v0_ref), (q1_ref, k1_ref, v1_ref))

    @pl.when(qi == 0)
    def _prep():
        for p, (_, k_ref, v_ref) in enumerate(pairs):
            for c in range(seq // TK):
                rows = slice(c * TK, (c + 1) * TK)
                pos = c * TK + lax.broadcasted_iota(jnp.int32, (TK, 1), 0)
                even, odd = _pair_keys(_group_rms(k_ref[rows, :], gk_ref[...]), pos)
                kp_s[rows, 2 * p * LANES:(2 * p + 1) * LANES] = even
                kp_s[rows, (2 * p + 1) * LANES:(2 * p + 2) * LANES] = odd
                v_t = v_ref[rows, :].T
                for j in range(2):
                    vt_s[c, (2 * p + j) * V_ROWS:(2 * p + j + 1) * V_ROWS, :] = _values_t(
                        v_t[j * HEAD_DIM:(j + 1) * HEAD_DIM, :])
            for r in range(FOLD):
                pos = r + FOLD * lax.broadcasted_iota(jnp.int32, (cls_rows, 1), 0)
                even, odd = _pair_keys(_group_rms(k_ref[pl.ds(r, cls_rows, stride=FOLD), :], gk_ref[...]), pos)
                kf_s[2 * p, :, r * LANES:(r + 1) * LANES] = even
                kf_s[2 * p + 1, :, r * LANES:(r + 1) * LANES] = odd
                v_t = v_ref[pl.ds(r, cls_rows, stride=FOLD), :].T
                for j in range(2):
                    vf_s[2 * p + j, :, r * cls_rows:(r + 1) * cls_rows] = _values_t(
                        v_t[j * HEAD_DIM:(j + 1) * HEAD_DIM, :])

    t0 = qi * TQ
    u = lax.broadcasted_iota(jnp.int32, (TQ, 1), 0)
    tpos = t0 + FOLD * (u % per) + u // per
    lane = lax.broadcasted_iota(jnp.int32, (TQ, LANES), 1)
    for p, (q_ref, _, _) in enumerate(pairs):
        q = jnp.concatenate([q_ref[pl.ds(r, per, stride=FOLD), :] for r in range(FOLD)], axis=0)
        pair = _group_rms(q, gq_ref[...]) * (HEAD_DIM ** -0.5)
        even = jnp.where((lane >= EVEN_AUG) & (lane < EVEN_AUG + 4), _query_aug(tpos, SLOPES_D[2 * p], EVEN_AUG), pair)
        odd = jnp.where(lane < ODD_AUG + 4, _query_aug(tpos, SLOPES_D[2 * p + 1], ODD_AUG), pair)
        qp_s[2 * p] = even.astype(BF16)
        qp_s[2 * p + 1] = odd.astype(BF16)

    kt_lo = jnp.maximum(t0 - FAR_FROM, 0) // TK
    top_kt = qi * N_TOP + (N_TOP - 1)

    def kp_fn(kt):
        off = pl.multiple_of(kt * TK, TK)
        return lambda h: kp_s[pl.ds(off, TK), h * LANES:(h + 1) * LANES]

    m = _flash_pipelined(lambda i: top_kt - i, top_kt + 1 - kt_lo, kp_fn, lambda kt, top: bias_ref[top_kt - kt],
                         lambda kt: (lambda h: vt_s[kt, h * V_ROWS:(h + 1) * V_ROWS, :]), qp_s, s_s, acc_s, None,
                         causal_lanes=False)
    for h in range(N_HEADS):
        m_s[h] = m[h]

    groups = TQ // LANES
    cls_of_row = lax.broadcasted_iota(jnp.int32, (LANES, LANES), 0) // per
    cls_of_lane = lax.broadcasted_iota(jnp.int32, (FAR_ROWS, LANES), 1) // per
    for h in range(N_HEADS):
        for g in range(groups):
            rows = qp_s[h, g * LANES:(g + 1) * LANES, :]
            qcat_s[h, g] = jnp.concatenate([jnp.where(cls_of_row == j, rows, jnp.zeros_like(rows))
                                            for j in range(CLS_PER_GROUP)], axis=1)

    a0 = t0 // FOLD
    a_lane = a0 + lax.broadcasted_iota(jnp.int32, (FAR_ROWS, TQ), 1) % per
    row = lax.broadcasted_iota(jnp.int32, (FAR_ROWS, TQ), 0)

    def far_block(blk):
        back = a_lane - (blk * FAR_ROWS + row)
        bias = jnp.where((back > FAR_FROM // FOLD) & (back <= DILATED_WINDOW // FOLD), 0.0, NEG_INF)
        keys = slice(blk * FAR_ROWS, (blk + 1) * FAR_ROWS)
        scores = [jnp.concatenate(
            [lax.dot_general(kf_s[h, keys, g * CLS_PER_GROUP * LANES:(g + 1) * CLS_PER_GROUP * LANES], qcat_s[h, g],
                             _NT, preferred_element_type=F32) for g in range(groups)], axis=1) + bias
            for h in range(N_HEADS)]
        for h in range(N_HEADS):
            m_old = m_s[h]
            m_new = jnp.maximum(m_old, jnp.max(scores[h], axis=0, keepdims=True))
            p = jnp.exp(scores[h] - m_new)
            pv = []
            for g in range(groups):
                pg = p[:, g * LANES:(g + 1) * LANES]
                rhs = jnp.concatenate([jnp.where(cls_of_lane == j, pg, 0.0) for j in range(CLS_PER_GROUP)],
                                      axis=0).astype(BF16)
                lhs = jnp.concatenate(
                    [vf_s[h, :, (g * CLS_PER_GROUP + j) * cls_rows + blk * FAR_ROWS:
                           (g * CLS_PER_GROUP + j) * cls_rows + (blk + 1) * FAR_ROWS] for j in range(CLS_PER_GROUP)],
                    axis=1)
                pv.append(jnp.dot(lhs, rhs, preferred_element_type=F32))
            acc_s[h] = jnp.exp(m_old - m_new) * acc_s[h] + jnp.concatenate(pv, axis=1)
            m_s[h] = m_new

    for blk in range(cls_rows // FAR_ROWS):
        needed = (a0 + per - 1 - blk * FAR_ROWS > FAR_FROM // FOLD) & \
                 (a0 - (blk * FAR_ROWS + FAR_ROWS - 1) <= DILATED_WINDOW // FOLD)
        pl.when(needed)(functools.partial(far_block, blk))
    _flash_write(acc_s, o_ref)


def _dilated(p2, gq, gk, batch, seq):
    nq = seq // TQ
    t = batch * seq
    assert FAR_FROM % FOLD == 0 and TQ % FOLD == 0 and (seq // FOLD) % FAR_ROWS == 0 and (TQ // FOLD) * CLS_PER_GROUP == LANES
    bias_tiles = jnp.asarray(_band_bias_tiles(FAR_FROM, _near_mult, folded=True))
    pair_cols = lambda rows, col: [pl.BlockSpec((rows, LANES), functools.partial(
        lambda b, i, c: (b if rows == seq else b * nq + i, c), c=col // LANES + j)) for j in range(2)]
    gain = pl.BlockSpec((1, LANES), lambda b, i: (0, 0))
    return pl.pallas_call(
        functools.partial(_dilated_kernel, seq=seq),
        grid=(batch, nq),
        in_specs=pair_cols(TQ, COL_QD) + pair_cols(seq, COL_KD) + pair_cols(seq, COL_VD)
                 + [gain, gain, pl.BlockSpec(bias_tiles.shape, lambda b, i: (0, 0, 0))],
        out_specs=pl.BlockSpec((1, GROUP_W, TQ), lambda b, i: (b * nq + i, 0, 0)),
        out_shape=jax.ShapeDtypeStruct((t // TQ, GROUP_W, TQ), F32),
        scratch_shapes=[pltpu.VMEM((seq, N_HEADS * LANES), BF16),
                        pltpu.VMEM((seq // TK, N_HEADS * V_ROWS, TK), BF16),
                        pltpu.VMEM((N_HEADS, seq // FOLD, FOLD * LANES), BF16),
                        pltpu.VMEM((N_HEADS, V_ROWS, seq), BF16),
                        pltpu.VMEM((N_HEADS, TQ, LANES), BF16),
                        pltpu.VMEM((N_HEADS, TQ // LANES, LANES, CLS_PER_GROUP * LANES), BF16),
                        pltpu.VMEM((N_HEADS, TK, TQ), F32),
                        pltpu.VMEM((N_HEADS, V_ROWS, TQ), F32),
                        pltpu.VMEM((N_HEADS, 1, TQ), F32)],
        compiler_params=pltpu.CompilerParams(dimension_semantics=("parallel", "arbitrary"),
                                             vmem_limit_bytes=VMEM_LIMIT),
        name="dilated",
    )(p2, p2, p2, p2, p2, p2, gq, gk, bias_tiles)


def _stick_kernel(q_ref, k_ref, v_ref, o_ref, kp_s, vt_s, qp_s, sp_s, e_s, acc_s, *, seq):
    TQ, N_TOP = STICK_TQ, STICK_TQ // TK
    qi = pl.program_id(1)
    nkt = seq // TK

    @pl.when(qi == 0)
    def _prep():
        for c in range(nkt):
            rows = slice(c * TK, (c + 1) * TK)
            kb = k_ref[rows, :]
            for i in range(N_HEADS // 2):
                even, odd = _pair_keys(kb[:, i * LANES:(i + 1) * LANES], None)
                kp_s[rows, 2 * i * LANES:(2 * i + 1) * LANES] = even
                kp_s[rows, (2 * i + 1) * LANES:(2 * i + 2) * LANES] = odd
            vt_s[c] = v_ref[rows, :].T.astype(BF16)

    _prep_queries(q_ref[...], None, qi * TQ, None, qp_s)
    col = lax.broadcasted_iota(jnp.int32, (TK + SUM_PAD, TK), 1)
    srow = lax.broadcasted_iota(jnp.int32, (TK + SUM_PAD, TK), 0)
    sums = ((srow == TK) | ((srow < TK) & (col > srow))).astype(BF16)
    k_local = lax.broadcasted_iota(jnp.int32, (TK, TQ), 0)
    q_local = lax.broadcasted_iota(jnp.int32, (TK, TQ), 1)
    top_kt = qi * N_TOP + (N_TOP - 1)
    acc_s[...] = jnp.zeros(acc_s.shape, F32)

    def past_mask(tiles_above):
        return tiles_above * TK + k_local < q_local

    def logits(kt, h):
        off = pl.multiple_of(kt * TK, TK)
        return lax.dot_general(kp_s[pl.ds(off, TK), h * LANES:(h + 1) * LANES], qp_s[h // 2], _NT,
                               preferred_element_type=F32)

    def stage(h, z, past):
        sp = jnp.maximum(z, 0.0) + jnp.log(1.0 + jnp.exp2(jnp.abs(z) * (-LOG2E)))
        spm = sp if past is None else jnp.where(past, sp, 0.0)
        sp_s[h] = spm.astype(BF16)
        logsig = z - sp
        e_s[h] = logsig if past is None else jnp.where(past, logsig, NEG_INF)

    def tails(h):
        return [jnp.dot(sums, sp_s[h], preferred_element_type=F32)]

    def consume(kt, h, w, later):
        attn = jnp.exp(e_s[h] - w[0][0:TK, :] - later)
        acc_s[h] += jnp.dot(vt_s[kt, h * HEAD_DIM:(h + 1) * HEAD_DIM, :], attn.astype(BF16),
                            preferred_element_type=F32)
        return later + w[0][TK:TK + 1, :]

    ahead = MXU_AHEAD
    for h in range(N_HEADS):
        stage(h, logits(top_kt, h), past_mask(N_TOP - 1))

    def step(kt, carry, past):
        laters, w_first, z_first = carry
        w = {0: list(w_first)}
        z = {0: z_first}
        for h in range(1, ahead):
            w[h] = tails(h)
            z[h] = logits(kt - 1, h)
        out = []
        for h in range(N_HEADS):
            out.append(consume(kt, h, w.pop(h), laters[h]))
            stage(h, z.pop(h), past)
            if h + ahead < N_HEADS:
                w[h + ahead] = tails(h + ahead)
                z[h + ahead] = logits(kt - 1, h + ahead)
            elif h + ahead == N_HEADS:
                nxt = (tuple(tails(0)), logits(jnp.maximum(kt - 2, 0), 0))
        return tuple(out), nxt[0], nxt[1]

    carry = (tuple(jnp.zeros((1, TQ), F32) for _ in range(N_HEADS)),
             tuple(tails(0)), logits(jnp.maximum(top_kt - 1, 0), 0))
    for j in range(N_TOP - 1):
        carry = step(top_kt - j, carry, past_mask(N_TOP - 2 - j))
    below = qi * N_TOP
    laters, w_first, _ = lax.fori_loop(0, below, lambda i, c: step(below - i, c, None), carry)
    w = {0: list(w_first)}
    for h in range(1, ahead):
        w[h] = tails(h)
    for h in range(N_HEADS):
        consume(0, h, w.pop(h), laters[h])
        if h + ahead < N_HEADS:
            w[h + ahead] = tails(h + ahead)
    o_ref[0] = acc_s[...].reshape(GROUP_W, TQ)


def _stick(p2, batch, seq):
    TQ = STICK_TQ
    nq = seq // TQ
    t = batch * seq
    return pl.pallas_call(
        functools.partial(_stick_kernel, seq=seq),
        grid=(batch, nq),
        in_specs=[pl.BlockSpec((TQ, GROUP_W), lambda b, i: (b * nq + i, COL_QC // GROUP_W)),
                  pl.BlockSpec((seq, GROUP_W), lambda b, i: (b, COL_KC // GROUP_W)),
                  pl.BlockSpec((seq, GROUP_W), lambda b, i: (b, COL_VC // GROUP_W))],
        out_specs=pl.BlockSpec((1, GROUP_W, TQ), lambda b, i: (b * nq + i, 0, 0)),
        out_shape=jax.ShapeDtypeStruct((t // TQ, GROUP_W, TQ), F32),
        scratch_shapes=[pltpu.VMEM((seq, N_HEADS * LANES), BF16),
                        pltpu.VMEM((seq // TK, GROUP_W, TK), BF16),
                        pltpu.VMEM((N_HEADS // 2, TQ, LANES), BF16),
                        pltpu.VMEM((N_HEADS, TK, TQ), BF16),
                        pltpu.VMEM((N_HEADS, TK, TQ), F32),
                        pltpu.VMEM((N_HEADS, HEAD_DIM, TQ), F32)],
        compiler_params=pltpu.CompilerParams(dimension_semantics=("parallel", "arbitrary"),
                                             vmem_limit_bytes=VMEM_LIMIT),
        name="stick_breaking",
    )(p2, p2, p2)


def _mixout_kernel(x_ref, ocmp_ref, oslc_ref, owin_ref, gate_ref, cvb_ref, cvc_ref, cvu_ref, pc_ref, pu_ref,
                   oc_ref, od_ref, bg_ref, cw_ref, gout_ref, wout_ref, o_ref, od_s, *, tiles_per_seq):
    i = pl.program_id(0)
    tm, d = x_ref.shape
    chains = [slice(c * MIX_CHAIN, (c + 1) * MIX_CHAIN) for c in range(tm // MIX_CHAIN)]

    def heads(ref, rows):
        tq = ref.shape[2]
        return ref[rows.start // tq, :, rows.start % tq:rows.start % tq + MIX_CHAIN].T

    per = TQ // FOLD
    for tile in range(tm // TQ):
        folded = od_ref[tile].T
        for half in range(GROUP_W // LANES):
            od_s[tile, half] = folded[:, half * LANES:(half + 1) * LANES]

    def unfolded(rows):
        tile, first = rows.start // TQ, (rows.start % TQ) // FOLD
        return jnp.concatenate(
            [jnp.concatenate([od_s[tile, half, pl.ds(a, FOLD, stride=per), :] for a in range(first, first + MIX_CHAIN // FOLD)],
                             axis=0) for half in range(GROUP_W // LANES)], axis=1)

    cu = cvc_ref[...] * cvu_ref[...]
    prev = jnp.where(i % tiles_per_seq == 0, 0.0, pc_ref[...] * pu_ref[...])
    full = jnp.concatenate([prev, cu], axis=0)
    back1 = pltpu.roll(full, 1, 0)[8:, :]
    back2 = pltpu.roll(full, 2, 0)[8:, :]
    cw = cw_ref[...]
    ob = cvb_ref[...] * (cw[0:1, :] * back2 + cw[1:2, :] * back1 + cw[2:3, :] * cu)

    gates = jax.nn.sigmoid(gate_ref[...] + bg_ref[...])
    src = lax.broadcasted_iota(jnp.int32, (LANES, GROUP_W), 0)
    head3 = (lax.broadcasted_iota(jnp.int32, (LANES, GROUP_W), 1) >> 6) * 3
    branch = [(src == head3 + r).astype(BF16) for r in range(3)]
    assert d // HEAD_DIM <= LANES
    gather = ((lax.broadcasted_iota(jnp.int32, (d, LANES), 0) >> 6)
              == lax.broadcasted_iota(jnp.int32, (d, LANES), 1)).astype(BF16)
    spread = (lax.broadcasted_iota(jnp.int32, (LANES, d), 0)
              == (lax.broadcasted_iota(jnp.int32, (LANES, d), 1) >> 6)).astype(BF16)

    groups, ssq = [], []
    for rows in chains:
        oa = (_exact_dot(gates[rows], branch[0]) * heads(ocmp_ref, rows)
              + _exact_dot(gates[rows], branch[1]) * heads(oslc_ref, rows)
              + _exact_dot(gates[rows], branch[2]) * heads(owin_ref, rows))
        g = jnp.concatenate([oa, ob[rows], heads(oc_ref, rows), unfolded(rows)], axis=1)
        groups.append(g)
        ssq.append(_exact_dot(g * g, gather))
    scale = [_exact_dot(lax.rsqrt(s * (1.0 / HEAD_DIM) + RMS_EPS), spread) for s in ssq]
    for rows, g, sc in zip(chains, groups, scale):
        mixed = (g * sc * gout_ref[...]).astype(BF16)
        o_ref[rows, :] = x_ref[rows, :] + jnp.dot(mixed, wout_ref[...], preferred_element_type=F32)


def _mixout(x2, p2, ocmp, oslc, owin, oc, od, bg, cw, gout, wout, seq):
    t, d = x2.shape
    tm = 2 * MIX_CHAIN
    rows = lambda w, col: pl.BlockSpec((tm, w), lambda i: (i, col // w))
    heads_t = lambda tq: pl.BlockSpec((tm // tq, GROUP_W, tq), lambda i: (i, 0, 0))
    prev8 = lambda col: pl.BlockSpec((8, GROUP_W), lambda i: (jnp.maximum(i * (tm // 8) - 1, 0), col // GROUP_W))
    const = lambda shape: pl.BlockSpec(shape, lambda i: (0, 0))
    assert tm % TQ == 0 and tm % STICK_TQ == 0
    return pl.pallas_call(
        functools.partial(_mixout_kernel, tiles_per_seq=seq // tm),
        grid=(t // tm,),
        in_specs=[rows(d, 0), heads_t(TQ), heads_t(TQ), heads_t(TQ),
                  rows(LANES, COL_GATE), rows(GROUP_W, COL_CVB), rows(GROUP_W, COL_CVC), rows(GROUP_W, COL_CVU),
                  prev8(COL_CVC), prev8(COL_CVU),
                  heads_t(STICK_TQ), heads_t(TQ),
                  const((1, LANES)), const((8, GROUP_W)), const((1, d)), const((d, d))],
        out_specs=rows(d, 0),
        out_shape=jax.ShapeDtypeStruct((t, d), F32),
        scratch_shapes=[pltpu.VMEM((tm // TQ, GROUP_W // LANES, TQ, LANES), F32)],
        compiler_params=pltpu.CompilerParams(dimension_semantics=("parallel",), vmem_limit_bytes=VMEM_LIMIT),
        name="mixout",
    )(x2, ocmp, oslc, owin, p2, p2, p2, p2, p2, p2, oc, od, bg, cw, gout, wout)


def _ffn_kernel(x_ref, g_ref, wg_ref, wu_ref, wd_ref, o_ref, act_s, *, ff_chunk, out_chunk):
    h = _rms(x_ref[...], g_ref[...]).astype(BF16)
    dff = wg_ref.shape[1]
    for c in range(dff // ff_chunk):
        cols = slice(c * ff_chunk, (c + 1) * ff_chunk)
        a = jnp.dot(h, wg_ref[:, cols], preferred_element_type=F32)
        u = jnp.dot(h, wu_ref[:, cols], preferred_element_type=F32)
        act_s[:, cols] = (a * jax.nn.sigmoid(a) * u).astype(BF16)
    d = o_ref.shape[1]
    for c in range(d // out_chunk):
        cols = slice(c * out_chunk, (c + 1) * out_chunk)
        o_ref[:, cols] = x_ref[:, cols] + jnp.dot(act_s[...], wd_ref[:, cols], preferred_element_type=F32)


def _ffn(x2, g, wg, wu, wd):
    t, d = x2.shape
    dff = wg.shape[1]
    tm = 1024
    resident = lambda shape: pl.BlockSpec(shape, lambda i: (0, 0), pipeline_mode=pl.Buffered(1))
    return pl.pallas_call(
        functools.partial(_ffn_kernel, ff_chunk=256, out_chunk=256),
        grid=(t // tm,),
        in_specs=[pl.BlockSpec((tm, d), lambda i: (i, 0)),
                  pl.BlockSpec((1, d), lambda i: (0, 0)),
                  resident((d, dff)), resident((d, dff)), resident((dff, d))],
        out_specs=pl.BlockSpec((tm, d), lambda i: (i, 0)),
        out_shape=jax.ShapeDtypeStruct((t, d), F32),
        scratch_shapes=[pltpu.VMEM((tm, dff), BF16)],
        compiler_params=pltpu.CompilerParams(dimension_semantics=("parallel",), vmem_limit_bytes=VMEM_LIMIT),
        name="ffn",
    )(x2, g, wg, wu, wd)


def _permute_w_in_kernel(w_ref, o_ref):
    rows = w_ref.shape[0]
    n_rest = P_COLS - COL_CVB
    o_ref[:, 0:COL_GATE] = w_ref[:, 0:COL_GATE].astype(BF16)
    lane = lax.broadcasted_iota(jnp.int32, (rows, LANES), 1)
    o_ref[:, COL_GATE:COL_GATE + LANES] = jnp.where(lane < N_GATES, w_ref[:, COL_GATE:COL_GATE + LANES], 0.0).astype(BF16)
    o_ref[:, COL_CVB:P_COLS] = w_ref[:, COL_GATE + N_GATES:COL_GATE + N_GATES + n_rest].astype(BF16)


def _permute_w_in(w_in):
    d, cols = w_in.shape
    tr = 256
    return pl.pallas_call(
        _permute_w_in_kernel,
        grid=(d // tr,),
        in_specs=[pl.BlockSpec((tr, cols), lambda i: (i, 0))],
        out_specs=pl.BlockSpec((tr, P_COLS), lambda i: (i, 0)),
        out_shape=jax.ShapeDtypeStruct((d, P_COLS), BF16),
        compiler_params=pltpu.CompilerParams(dimension_semantics=("parallel",), vmem_limit_bytes=VMEM_LIMIT),
        name="permute_w_in",
    )(w_in)


def _layer(x2, batch, seq, g_mix, w_in, b_gate, g_q_nsa, g_k_cmp, g_k_slc, g_k_win, pe_k_cmp, pe_v_cmp,
           w1_k_cmp, w2_k_cmp, w1_v_cmp, w2_v_cmp, conv_w, g_q_dil, g_k_dil, g_out, w_out,
           g_ffn, w_gate, w_up, w_down):
    row = lambda v: v.reshape(1, -1)
    per_group = lambda g, n: jnp.tile(g, n).reshape(1, -1)
    p2 = _inproj(x2, row(g_mix), _permute_w_in(w_in))

    pe, w1, w2 = _compress_weights(pe_k_cmp, pe_v_cmp, w1_k_cmp, w2_k_cmp, w1_v_cmp, w2_v_cmp)
    kc, vc = _compress(p2, pe, w1, w2, per_group(g_k_cmp, 2), batch, seq)
    ocmp, oslc, owin = _nsa(p2, kc, vc, per_group(g_q_nsa, N_HEADS), per_group(g_k_slc, 2), per_group(g_k_win, 2),
                            batch, seq)
    od = _dilated(p2, per_group(g_q_dil, 2), per_group(g_k_dil, 2), batch, seq)
    oc = _stick(p2, batch, seq)

    bg = jnp.zeros((1, LANES), F32).at[0, :N_GATES].set(b_gate)
    cwp = jnp.zeros((8, GROUP_W), F32).at[:CONV_K].set(conv_w)
    x1 = _mixout(x2, p2, ocmp, oslc, owin, oc, od, bg, cwp, row(g_out), w_out.astype(BF16), seq)
    return _ffn(x1, row(g_ffn), w_gate.astype(BF16), w_up.astype(BF16), w_down.astype(BF16))


def kernel(x, g_mix, w_in, b_gate, g_q_nsa, g_k_cmp, g_k_slc, g_k_win, pe_k_cmp, pe_v_cmp, w1_k_cmp, w2_k_cmp,
           w1_v_cmp, w2_v_cmp, conv_w, g_q_dil, g_k_dil, g_out, w_out, g_ffn, w_gate, w_up, w_down):
    batch, seq, d = x.shape
    assert seq % (CMP_STRIDE * LANES) == 0 and d % LANES == 0
    assert seq <= 64 * 256
    x2 = x.reshape(batch * seq, d)
    params = (g_mix, w_in, b_gate, g_q_nsa, g_k_cmp, g_k_slc, g_k_win, pe_k_cmp, pe_v_cmp, w1_k_cmp, w2_k_cmp,
              w1_v_cmp, w2_v_cmp, conv_w, g_q_dil, g_k_dil, g_out, w_out, g_ffn, w_gate, w_up, w_down)
    for layer in range(g_mix.shape[0]):
        x2 = _layer(x2, batch, seq, *[p[layer] for p in params])
    return x2.reshape(batch, seq, d)
```

```python
import functools

import numpy as np
import jax
import jax.numpy as jnp
from jax import lax
from jax.experimental import pallas as pl
from jax.experimental.pallas import tpu as pltpu

F32 = jnp.float32
BF16 = jnp.bfloat16

HEAD_DIM = 64
N_HEADS = 4
GROUP_W = N_HEADS * HEAD_DIM
CONV_K = 3
CMP_LEN = 32
CMP_STRIDE = 16
SEL_BLOCK = 64
N_SELECT = 16
NSA_WINDOW = 512
DILATED_CONFIGS = ((128, 1), (512, 4), (2048, 16))
DILATED_WINDOW = max(w for w, _ in DILATED_CONFIGS)
FOLD = 16
FAR_FROM = 512
FAR_ROWS = 128
NEG_INF = -1e30
FORCE_SCORE = 1e6
RMS_EPS = 1e-6

TQ = 512
TK = 256
N_TOP = TQ // TK
CLS_PER_GROUP = 128 * FOLD // TQ
STICK_TQ = 256
LANES = 128
MXU_DEPTH = 256
MIX_CHAIN = 256
MXU_AHEAD = 2
FLASH_AHEAD = 1
SUM_PAD = 16
V_ROWS = HEAD_DIM + SUM_PAD
LOG2E = 1.4426950408889634
VMEM_LIMIT = 52 * 1024 * 1024

COL_QA, COL_KVC, COL_KSV, COL_KWV, COL_GATE = 0, 256, 384, 512, 640
COL_CVB, COL_CVC, COL_CVU = 768, 1024, 1280
COL_QC, COL_KC, COL_VC = 1536, 1792, 2048
COL_QD, COL_KD, COL_VD = 2304, 2560, 2816
P_COLS = 3072
N_GATES = 12

_NT = (((1,), (1,)), ((), ()))


def _alibi_slopes():
    s = [2.0 ** (-8.0 * i / 8) for i in range(1, 9)]
    return tuple(s[0::2]), tuple(s[1::2])


SLOPES_A, SLOPES_D = _alibi_slopes()


def _rms(x, g):
    return x * lax.rsqrt(jnp.mean(x * x, axis=-1, keepdims=True) + RMS_EPS) * g


def _exact_dot(a, sel):
    hi = a.astype(BF16)
    lo = (a - hi.astype(F32)).astype(BF16)
    if 2 * a.shape[1] <= MXU_DEPTH:
        return jnp.dot(jnp.concatenate([hi, lo], axis=1), jnp.concatenate([sel, sel], axis=0),
                       preferred_element_type=F32)
    return jnp.dot(hi, sel, preferred_element_type=F32) + jnp.dot(lo, sel, preferred_element_type=F32)


def _group_rms(x, g):
    w = x.shape[1]
    same = ((lax.broadcasted_iota(jnp.int32, (w, w), 0) >> 6)
            == (lax.broadcasted_iota(jnp.int32, (w, w), 1) >> 6)).astype(BF16)
    ssq = _exact_dot(x * x, same)
    return x * lax.rsqrt(ssq * (1.0 / HEAD_DIM) + RMS_EPS) * g


EVEN_AUG, ODD_AUG = HEAD_DIM, 0


def _key_aug(pos, first):
    lane = lax.broadcasted_iota(jnp.int32, (pos.shape[0], LANES), 1) - first
    hi = (pos >> 6).astype(F32)
    lo = (pos & 63).astype(F32)
    return jnp.where(lane == 0, hi, jnp.where(lane == 1, lo, jnp.where((lane == 2) | (lane == 3), 1.0, 0.0)))


def _query_aug(t, slope, first):
    lane = lax.broadcasted_iota(jnp.int32, (t.shape[0], LANES), 1) - first
    hi = (t >> 6).astype(F32) * (-64.0 * slope)
    lo = (t & 63).astype(F32) * (-slope)
    return jnp.where(lane == 0, 64.0 * slope,
                     jnp.where(lane == 1, slope, jnp.where(lane == 2, hi, jnp.where(lane == 3, lo, 0.0))))


def _pair_keys(pair, pos):
    lane = lax.broadcasted_iota(jnp.int32, pair.shape, 1)
    aug_e = 0.0 if pos is None else _key_aug(pos, EVEN_AUG)
    aug_o = 0.0 if pos is None else _key_aug(pos, ODD_AUG)
    return (jnp.where(lane < HEAD_DIM, pair, aug_e).astype(BF16),
            jnp.where(lane >= HEAD_DIM, pair, aug_o).astype(BF16))


def _prep_queries(q, g, t0, slopes, qp_s):
    qn = (q if g is None else _group_rms(q, g)) * (HEAD_DIM ** -0.5)
    tpos = t0 + lax.broadcasted_iota(jnp.int32, (q.shape[0], 1), 0)
    lane = lax.broadcasted_iota(jnp.int32, (q.shape[0], LANES), 1)
    for i in range(N_HEADS // 2):
        pair = qn[:, i * LANES:(i + 1) * LANES]
        if slopes is None:
            qp_s[i] = pair.astype(BF16)
            continue
        even = jnp.where((lane >= EVEN_AUG) & (lane < EVEN_AUG + 4), _query_aug(tpos, slopes[2 * i], EVEN_AUG), pair)
        odd = jnp.where(lane < ODD_AUG + 4, _query_aug(tpos, slopes[2 * i + 1], ODD_AUG), pair)
        qp_s[2 * i] = even.astype(BF16)
        qp_s[2 * i + 1] = odd.astype(BF16)


def _inproj_kernel(x_ref, g_ref, w_ref, o_ref, *, nchunk):
    h = _rms(x_ref[...], g_ref[...]).astype(BF16)
    cw = P_COLS // nchunk
    for c in range(nchunk):
        o_ref[:, c * cw:(c + 1) * cw] = jnp.dot(h, w_ref[:, c * cw:(c + 1) * cw], preferred_element_type=F32)


def _inproj(x2, g, w):
    t, d = x2.shape
    tm = 1024
    return pl.pallas_call(
        functools.partial(_inproj_kernel, nchunk=6),
        grid=(t // tm,),
        in_specs=[pl.BlockSpec((tm, d), lambda i: (i, 0)),
                  pl.BlockSpec((1, d), lambda i: (0, 0)),
                  pl.BlockSpec((d, P_COLS), lambda i: (0, 0), pipeline_mode=pl.Buffered(1))],
        out_specs=pl.BlockSpec((tm, P_COLS), lambda i: (i, 0)),
        out_shape=jax.ShapeDtypeStruct((t, P_COLS), F32),
        compiler_params=pltpu.CompilerParams(dimension_semantics=("parallel",), vmem_limit_bytes=VMEM_LIMIT),
        name="inproj",
    )(x2, g, w)


def _gelu_tanh(x):
    return x * (0.5 * (1.0 + jnp.tanh(np.sqrt(2.0 / np.pi).astype(np.float32) * (x + 0.044715 * (x * x * x)))))


def _compress_kernel(kv_ref, pe_ref, w1_ref, w2_ref, gk_ref, kc_ref, vct_ref):
    nc = kv_ref.shape[0] // CMP_STRIDE
    first = None
    second = None
    for j in range(CMP_STRIDE):
        tok = kv_ref[pl.ds(j, nc, stride=CMP_STRIDE), :]
        a = jnp.dot((tok + pe_ref[j:j + 1, :]).astype(BF16), w1_ref[j], preferred_element_type=F32)
        b = jnp.dot((tok + pe_ref[CMP_STRIDE + j:CMP_STRIDE + j + 1, :]).astype(BF16), w1_ref[CMP_STRIDE + j],
                    preferred_element_type=F32)
        first = a if first is None else first + a
        second = b if second is None else second + b
    hid = first + pltpu.roll(second, nc - 1, 0)
    out = jnp.dot(_gelu_tanh(hid).astype(BF16), w2_ref[...], preferred_element_type=F32)
    kn = _group_rms(out, gk_ref[...])
    end = lax.broadcasted_iota(jnp.int32, (nc, 1), 0) * CMP_STRIDE + (CMP_LEN - 1)
    even, _ = _pair_keys(kn, end)
    _, odd = _pair_keys(pltpu.roll(kn, HEAD_DIM, 1), end)
    kc_ref[0] = jnp.concatenate([even, odd], axis=1)
    vct_ref[0] = out.T[HEAD_DIM:, :].astype(BF16)


def _compress(p2, pe, w1, w2, gk, batch, seq):
    nc = seq // CMP_STRIDE
    hid2 = w1.shape[2]
    return pl.pallas_call(
        _compress_kernel,
        grid=(batch,),
        in_specs=[pl.BlockSpec((seq, LANES), lambda i: (i, COL_KVC // LANES)),
                  pl.BlockSpec((CMP_LEN, LANES), lambda i: (0, 0)),
                  pl.BlockSpec((CMP_LEN, LANES, hid2), lambda i: (0, 0, 0)),
                  pl.BlockSpec((hid2, LANES), lambda i: (0, 0)),
                  pl.BlockSpec((1, LANES), lambda i: (0, 0))],
        out_specs=[pl.BlockSpec((1, nc, 2 * LANES), lambda i: (i, 0, 0)),
                   pl.BlockSpec((1, HEAD_DIM, nc), lambda i: (i, 0, 0))],
        out_shape=[jax.ShapeDtypeStruct((batch, nc, 2 * LANES), BF16),
                   jax.ShapeDtypeStruct((batch, HEAD_DIM, nc), BF16)],
        compiler_params=pltpu.CompilerParams(dimension_semantics=("parallel",), vmem_limit_bytes=VMEM_LIMIT),
        name="nsa_compress",
    )(p2, pe, w1, w2, gk)


def _compress_weights(pe_k, pe_v, w1_k, w2_k, w1_v, w2_v):
    hid = w1_k.shape[1]
    w1k = w1_k.reshape(CMP_LEN, HEAD_DIM, hid)
    w1v = w1_v.reshape(CMP_LEN, HEAD_DIM, hid)
    z1 = jnp.zeros_like(w1k)
    w1 = jnp.concatenate([jnp.concatenate([w1k, z1], axis=2), jnp.concatenate([z1, w1v], axis=2)], axis=1)
    z2 = jnp.zeros_like(w2_k)
    w2 = jnp.concatenate([jnp.concatenate([w2_k, z2], axis=1), jnp.concatenate([z2, w2_v], axis=1)], axis=0)
    pe = jnp.concatenate([pe_k, pe_v], axis=1)
    return pe, w1.astype(BF16), w2.astype(BF16)


ALL_LANES = slice(0, TQ)


def _with_lanes(full, lanes, part):
    pieces = ([full[:, :lanes.start]] if lanes.start > 0 else []) + [part] + \
             ([full[:, lanes.stop:]] if lanes.stop < TQ else [])
    return jnp.concatenate(pieces, axis=1) if len(pieces) > 1 else part


def _flash_scores(kp_of, bias, qp_s, s_s, m_cur, lanes=ALL_LANES):
    m_next = []
    for h in range(N_HEADS):
        s = lax.dot_general(kp_of(h), qp_s[h, lanes, :], _NT, preferred_element_type=F32) + bias[:, lanes]
        s_s[h, :, lanes] = s
        m_next.append(_with_lanes(m_cur[h], lanes, jnp.maximum(m_cur[h][:, lanes], jnp.max(s, axis=0, keepdims=True))))
    return tuple(m_next)


def _flash_accum(vt_of, s_s, acc_s, m_prev, m_cur, lanes=ALL_LANES):
    for h in range(N_HEADS):
        alpha = jnp.exp(m_prev[h][:, lanes] - m_cur[h][:, lanes])
        p = jnp.exp(s_s[h, :, lanes] - m_cur[h][:, lanes])
        acc_s[h, :, lanes] = alpha * acc_s[h, :, lanes] + jnp.dot(vt_of(h), p.astype(BF16),
                                                                 preferred_element_type=F32)


def _flash_accum_and_scores(kp_of, bias, vt_of, qp_s, s_s, acc_s, m_prev, m_cur, acc_lanes=ALL_LANES,
                            new_lanes=ALL_LANES):
    def scores(h):
        return (lax.dot_general(kp_of(h), qp_s[h, new_lanes, :], _NT, preferred_element_type=F32)
                + bias[:, new_lanes])

    m_next = []
    s_new = {h: scores(h) for h in range(FLASH_AHEAD)}
    for h in range(N_HEADS):
        alpha = jnp.exp(m_prev[h][:, acc_lanes] - m_cur[h][:, acc_lanes])
        p = jnp.exp(s_s[h, :, acc_lanes] - m_cur[h][:, acc_lanes])
        acc_s[h, :, acc_lanes] = alpha * acc_s[h, :, acc_lanes] + jnp.dot(vt_of(h), p.astype(BF16),
                                                                         preferred_element_type=F32)
        s_h = s_new.pop(h)
        s_s[h, :, new_lanes] = s_h
        m_next.append(_with_lanes(m_cur[h], new_lanes,
                                  jnp.maximum(m_cur[h][:, new_lanes], jnp.max(s_h, axis=0, keepdims=True))))
        if h + FLASH_AHEAD < N_HEADS:
            s_new[h + FLASH_AHEAD] = scores(h + FLASH_AHEAD)
    return tuple(m_next)


def _flash_pipelined(tile_of, n_tiles, kp_fn, bias_fn, vt_fn, qp_s, s_s, acc_s, o_ref, causal_lanes=True):
    top_lanes = [slice((N_TOP - 1 - i) * TK if causal_lanes else 0, TQ) for i in range(N_TOP)]
    acc_s[...] = jnp.zeros(acc_s.shape, F32)
    m0 = tuple(jnp.full((1, TQ), NEG_INF, F32) for _ in range(N_HEADS))
    first_kt = tile_of(0)
    m1 = _flash_scores(kp_fn(first_kt), bias_fn(first_kt, True), qp_s, s_s, m0, top_lanes[0])

    def step(i, carry, top):
        m_prev, m_cur = carry
        nxt = tile_of(i + 1)
        lanes = (top_lanes[i], top_lanes[i + 1]) if top else (ALL_LANES, ALL_LANES)
        m_next = _flash_accum_and_scores(kp_fn(nxt), bias_fn(nxt, top), vt_fn(tile_of(i)), qp_s, s_s, acc_s,
                                         m_prev, m_cur, *lanes)
        return m_cur, m_next

    carry = (m0, m1)
    for i in range(N_TOP - 1):
        carry = step(i, carry, True)
    m_prev, m_cur = lax.fori_loop(N_TOP - 1, n_tiles - 1, lambda i, c: step(i, c, False), carry)
    _flash_accum(vt_fn(tile_of(n_tiles - 1)), s_s, acc_s, m_prev, m_cur)
    if o_ref is None:
        return m_cur
    _flash_write(acc_s, o_ref)


def _flash_heads(acc_s):
    return [acc_s[h, 0:HEAD_DIM, :] * (1.0 / acc_s[h, HEAD_DIM:HEAD_DIM + 1, :]) for h in range(N_HEADS)]


def _flash_write(acc_s, o_ref):
    o_ref[0] = jnp.concatenate(_flash_heads(acc_s), axis=0)


def _values_t(v_t):
    row = lax.broadcasted_iota(jnp.int32, (V_ROWS - HEAD_DIM, v_t.shape[1]), 0)
    return jnp.concatenate([v_t, jnp.where(row == 0, 1.0, 0.0)], axis=0).astype(BF16)


def _nsa_kernel(q_ref, kc_ref, vct_ref, ksv_ref, kwv_ref, gq_ref, gks_ref, gkw_ref, wbias_ref, gate_ref, bg_ref,
                o_ref, ksp_s, vst_s, kwp_s, vwt_s, qp_s, imp_s, sel_s, s_s, acc_s, gate_s, hit_s, *, seq):
    qi = pl.program_id(1)
    gate_s[...] = jax.nn.sigmoid(gate_ref[...] + bg_ref[...]).T

    def add_branch(r, per_head):
        term = jnp.concatenate([gate_s[3 * h + r:3 * h + r + 1, :] * per_head[h] for h in range(N_HEADS)], axis=0)
        o_ref[0] = term if r == 0 else o_ref[0] + term

    nc = seq // CMP_STRIDE
    nsel = seq // SEL_BLOCK
    nkt = seq // TK

    @pl.when(qi == 0)
    def _prep():
        for c in range(nkt):
            rows = slice(c * TK, (c + 1) * TK)
            pos = c * TK + lax.broadcasted_iota(jnp.int32, (TK, 1), 0)
            for kv_ref, g_ref, kp_s, vt_s in ((ksv_ref, gks_ref, ksp_s, vst_s), (kwv_ref, gkw_ref, kwp_s, vwt_s)):
                blk = kv_ref[rows, :]
                kn = _group_rms(blk, g_ref[...])
                kp_s[rows, 0:LANES], _ = _pair_keys(kn, pos)
                _, kp_s[rows, LANES:2 * LANES] = _pair_keys(pltpu.roll(kn, HEAD_DIM, 1), pos)
                vt_s[c] = _values_t(blk.T[HEAD_DIM:, :])
        imp_s[:, 0:8, :] = jnp.zeros((TQ // LANES, 8, LANES), F32)

    t0 = qi * TQ
    _prep_queries(q_ref[...], gq_ref[...], t0, SLOPES_A, qp_s)

    kc = [kc_ref[0, :, 0:LANES], kc_ref[0, :, LANES:2 * LANES]]
    vct = vct_ref[0]
    tq_row = t0 + lax.broadcasted_iota(jnp.int32, (nc, TQ), 1)
    n_idx = lax.broadcasted_iota(jnp.int32, (nc, TQ), 0)
    vis = (tq_row >= n_idx * CMP_STRIDE + (CMP_LEN - 1)) & (n_idx < nc - 1)
    vis_bias = jnp.where(vis, 0.0, NEG_INF)
    sees_any = (t0 + lax.broadcasted_iota(jnp.int32, (1, TQ), 1) >= CMP_LEN - 1).astype(F32)
    imp = jnp.zeros((nc, TQ), F32)
    ocmp_t = []
    scores = [lax.dot_general(kc[h % 2], qp_s[h], _NT, preferred_element_type=F32) + vis_bias
              for h in range(N_HEADS)]
    for h in range(N_HEADS):
        sc = scores[h]
        e = jnp.exp(sc - jnp.max(sc, axis=0, keepdims=True))
        p = e * (sees_any / jnp.sum(e, axis=0, keepdims=True))
        ocmp_t.append(jnp.dot(vct, p.astype(BF16), preferred_element_type=F32))
        imp = imp + p
    add_branch(0, ocmp_t)

    halves = []
    for half in range(TQ // LANES):
        imp_s[half, 8:8 + nc, :] = imp[:, half * LANES:(half + 1) * LANES]
        r = [imp_s[half, pl.ds(8 + k, nsel, stride=4), :] for k in range(4)]
        rm1 = imp_s[half, pl.ds(7, nsel, stride=4), :]
        halves.append(rm1 + 2.0 * (r[0] + r[1] + r[2]) + r[3])
    imp_blk = jnp.concatenate(halves, axis=1)
    blk = lax.broadcasted_iota(jnp.int32, (nsel, TQ), 0)
    tl = t0 + lax.broadcasted_iota(jnp.int32, (nsel, TQ), 1)
    cur = tl >> 6
    forced = (blk == 0) | (blk == cur) | (blk == cur - 1)
    valid = blk * SEL_BLOCK <= tl
    score = jnp.where(forced, FORCE_SCORE, jnp.where(valid, imp_blk, -FORCE_SCORE))
    sub = 8
    per_tile = TK // SEL_BLOCK
    tile_hit = []
    for g in range(nsel // sub):
        mine = score[g * sub:(g + 1) * sub, :]
        blk_g = g * sub + lax.broadcasted_iota(jnp.int32, (sub, TQ), 0)
        rank = jnp.zeros((sub, TQ), F32)
        for j in range(nsel):
            row = score[j:j + 1, :]
            if j < g * sub:
                beats = row >= mine
            elif j >= (g + 1) * sub:
                beats = row > mine
            else:
                beats = (row > mine) | ((row == mine) & (blk_g > j))
            rank = rank + jnp.where(beats, 1.0, 0.0)
        sel_g = jnp.where(rank < min(N_SELECT, nsel), 0.0, NEG_INF)
        sel_s[g * sub:(g + 1) * sub, :] = sel_g
        for part in range(sub // per_tile):
            tile_hit.append(jnp.max(sel_g[part * per_tile:(part + 1) * per_tile, :]))

    n_hit = jnp.int32(0)
    for kt in reversed(range(nkt)):
        hit_s[n_hit] = kt
        n_hit = n_hit + ((tile_hit[kt] > -1.0) & (kt * TK < t0)).astype(jnp.int32)

    def block_bias(kt):
        return jnp.concatenate([jnp.broadcast_to(sel_s[pl.ds(kt * per_tile + j, 1), :], (SEL_BLOCK, TQ))
                                for j in range(per_tile)], axis=0)

    k_local = lax.broadcasted_iota(jnp.int32, (TK, TQ), 0)
    q_local = lax.broadcasted_iota(jnp.int32, (TK, TQ), 1)
    top_kt = qi * N_TOP + (N_TOP - 1)

    def kp_fn(kt):
        off = pl.multiple_of(kt * TK, TK)
        return lambda h: ksp_s[pl.ds(off, TK), (h % 2) * LANES:(h % 2 + 1) * LANES]

    def bias_fn(kt, top):
        if top:
            return jnp.where(kt * TK - t0 + k_local <= q_local, block_bias(kt), NEG_INF)
        return block_bias(kt)

    tile_of = lambda i: jnp.where(i < N_TOP, top_kt - i, hit_s[jnp.maximum(i - N_TOP, 0)])
    _flash_pipelined(tile_of, n_hit + N_TOP, kp_fn, bias_fn, lambda kt: (lambda h: vst_s[kt]), qp_s, s_s, acc_s, None)
    add_branch(1, _flash_heads(acc_s))

    def window_kp_fn(kt):
        off = pl.multiple_of(kt * TK, TK)
        return lambda h: kwp_s[pl.ds(off, TK), (h % 2) * LANES:(h % 2 + 1) * LANES]

    kt_lo = jnp.maximum(t0 - (NSA_WINDOW - 1), 0) // TK
    _flash_pipelined(lambda i: top_kt - i, top_kt + 1 - kt_lo, window_kp_fn, lambda kt, top: wbias_ref[top_kt - kt],
                     lambda kt: (lambda h: vwt_s[kt]), qp_s, s_s, acc_s, None)
    add_branch(2, _flash_heads(acc_s))


def _nsa(p2, kc, vc, gq, gks, gkw, bg, batch, seq):
    nq = seq // TQ
    nc = seq // CMP_STRIDE
    t = batch * seq
    wbias = jnp.asarray(_band_bias_tiles(NSA_WINDOW - 1, _window_mult))
    head_tiles = pl.BlockSpec((1, GROUP_W, TQ), lambda b, i: (b * nq + i, 0, 0))
    head_shape = jax.ShapeDtypeStruct((t // TQ, GROUP_W, TQ), F32)
    return pl.pallas_call(
        functools.partial(_nsa_kernel, seq=seq),
        grid=(batch, nq),
        in_specs=[pl.BlockSpec((TQ, GROUP_W), lambda b, i: (b * nq + i, COL_QA // GROUP_W)),
                  pl.BlockSpec((1, nc, 2 * LANES), lambda b, i: (b, 0, 0)),
                  pl.BlockSpec((1, HEAD_DIM, nc), lambda b, i: (b, 0, 0)),
                  pl.BlockSpec((seq, LANES), lambda b, i: (b, COL_KSV // LANES)),
                  pl.BlockSpec((seq, LANES), lambda b, i: (b, COL_KWV // LANES)),
                  pl.BlockSpec((1, GROUP_W), lambda b, i: (0, 0)),
                  pl.BlockSpec((1, LANES), lambda b, i: (0, 0)),
                  pl.BlockSpec((1, LANES), lambda b, i: (0, 0)),
                  pl.BlockSpec(wbias.shape, lambda b, i: (0, 0, 0)),
                  pl.BlockSpec((TQ, LANES), lambda b, i: (b * nq + i, COL_GATE // LANES)),
                  pl.BlockSpec((1, LANES), lambda b, i: (0, 0))],
        out_specs=head_tiles,
        out_shape=head_shape,
        scratch_shapes=[pltpu.VMEM((seq, 2 * LANES), BF16),
                        pltpu.VMEM((seq // TK, V_ROWS, TK), BF16),
                        pltpu.VMEM((seq, 2 * LANES), BF16),
                        pltpu.VMEM((seq // TK, V_ROWS, TK), BF16),
                        pltpu.VMEM((N_HEADS, TQ, LANES), BF16),
                        pltpu.VMEM((TQ // LANES, 8 + nc, LANES), F32),
                        pltpu.VMEM((seq // SEL_BLOCK, TQ), F32),
                        pltpu.VMEM((N_HEADS, TK, TQ), F32),
                        pltpu.VMEM((N_HEADS, V_ROWS, TQ), F32),
                        pltpu.VMEM((LANES, TQ), F32),
                        pltpu.SMEM((seq // TK + 1,), jnp.int32)],
        compiler_params=pltpu.CompilerParams(dimension_semantics=("parallel", "arbitrary"),
                                             vmem_limit_bytes=VMEM_LIMIT),
        name="nsa",
    )(p2, kc, vc, p2, p2, gq, gks, gkw, wbias, p2, bg)


def _band_bias_tiles(window, mult_fn, folded=False):
    nd = (window + TQ - 1) // TK + 1
    kk = np.arange(TK)[:, None]
    qq = np.arange(TQ)[None, :]
    if folded:
        qq = FOLD * (qq % (TQ // FOLD)) + qq // (TQ // FOLD)
    tiles = np.empty((nd, TK, TQ), np.float32)
    for di in range(nd):
        d = (di - (N_TOP - 1)) * TK + qq - kk
        mult = mult_fn(d)
        tiles[di] = np.where(mult > 0, np.log(np.maximum(mult, 1)), NEG_INF)
    return tiles


def _window_mult(d):
    return ((d >= 0) & (d <= NSA_WINDOW - 1)).astype(np.float64)


def _dilated_mult(d):
    m = np.zeros(d.shape, np.float64)
    for window, dil in DILATED_CONFIGS:
        m += ((d >= 0) & (d <= window) & (d % dil == 0)).astype(np.float64)
    return m


def _near_mult(d):
    return np.where(d <= FAR_FROM, _dilated_mult(d), 0.0)


def _dilated_kernel(q0_ref, q1_ref, k0_ref, k1_ref, v0_ref, v1_ref, gq_ref, gk_ref, bias_ref, o_ref,
                    kp_s, vt_s, kf_s, vf_s, qp_s, qcat_s, s_s, acc_s, m_s, *, seq):
    qi = pl.program_id(1)
    per = TQ // FOLD
    cls_rows = seq // FOLD
    pairs = ((q0_ref, k0_ref, v0_ref), (q1_ref, k1_ref, v1_ref))

    @pl.when(qi == 0)
    def _prep():
        for p, (_, k_ref, v_ref) in enumerate(pairs):
            for c in range(seq // TK):
                rows = slice(c * TK, (c + 1) * TK)
                pos = c * TK + lax.broadcasted_iota(jnp.int32, (TK, 1), 0)
                even, odd = _pair_keys(_group_rms(k_ref[rows, :], gk_ref[...]), pos)
                kp_s[rows, 2 * p * LANES:(2 * p + 1) * LANES] = even
                kp_s[rows, (2 * p + 1) * LANES:(2 * p + 2) * LANES] = odd
                v_t = v_ref[rows, :].T
                for j in range(2):
                    vt_s[c, (2 * p + j) * V_ROWS:(2 * p + j + 1) * V_ROWS, :] = _values_t(
                        v_t[j * HEAD_DIM:(j + 1) * HEAD_DIM, :])
            for r in range(FOLD):
                pos = r + FOLD * lax.broadcasted_iota(jnp.int32, (cls_rows, 1), 0)
                even, odd = _pair_keys(_group_rms(k_ref[pl.ds(r, cls_rows, stride=FOLD), :], gk_ref[...]), pos)
                kf_s[2 * p, :, r * LANES:(r + 1) * LANES] = even
                kf_s[2 * p + 1, :, r * LANES:(r + 1) * LANES] = odd
                v_t = v_ref[pl.ds(r, cls_rows, stride=FOLD), :].T
                for j in range(2):
                    vf_s[2 * p + j, :, r * cls_rows:(r + 1) * cls_rows] = _values_t(
                        v_t[j * HEAD_DIM:(j + 1) * HEAD_DIM, :])

    t0 = qi * TQ
    u = lax.broadcasted_iota(jnp.int32, (TQ, 1), 0)
    tpos = t0 + FOLD * (u % per) + u // per
    lane = lax.broadcasted_iota(jnp.int32, (TQ, LANES), 1)
    for p, (q_ref, _, _) in enumerate(pairs):
        q = jnp.concatenate([q_ref[pl.ds(r, per, stride=FOLD), :] for r in range(FOLD)], axis=0)
        pair = _group_rms(q, gq_ref[...]) * (HEAD_DIM ** -0.5)
        even = jnp.where((lane >= EVEN_AUG) & (lane < EVEN_AUG + 4), _query_aug(tpos, SLOPES_D[2 * p], EVEN_AUG), pair)
        odd = jnp.where(lane < ODD_AUG + 4, _query_aug(tpos, SLOPES_D[2 * p + 1], ODD_AUG), pair)
        qp_s[2 * p] = even.astype(BF16)
        qp_s[2 * p + 1] = odd.astype(BF16)

    kt_lo = jnp.maximum(t0 - FAR_FROM, 0) // TK
    top_kt = qi * N_TOP + (N_TOP - 1)

    def kp_fn(kt):
        off = pl.multiple_of(kt * TK, TK)
        return lambda h: kp_s[pl.ds(off, TK), h * LANES:(h + 1) * LANES]

    m = _flash_pipelined(lambda i: top_kt - i, top_kt + 1 - kt_lo, kp_fn, lambda kt, top: bias_ref[top_kt - kt],
                         lambda kt: (lambda h: vt_s[kt, h * V_ROWS:(h + 1) * V_ROWS, :]), qp_s, s_s, acc_s, None,
                         causal_lanes=False)
    for h in range(N_HEADS):
        m_s[h] = m[h]

    groups = TQ // LANES
    cls_of_row = lax.broadcasted_iota(jnp.int32, (LANES, LANES), 0) // per
    cls_of_lane = lax.broadcasted_iota(jnp.int32, (FAR_ROWS, LANES), 1) // per
    for h in range(N_HEADS):
        for g in range(groups):
            rows = qp_s[h, g * LANES:(g + 1) * LANES, :]
            qcat_s[h, g] = jnp.concatenate([jnp.where(cls_of_row == j, rows, jnp.zeros_like(rows))
                                            for j in range(CLS_PER_GROUP)], axis=1)

    a0 = t0 // FOLD
    a_lane = a0 + lax.broadcasted_iota(jnp.int32, (FAR_ROWS, TQ), 1) % per
    row = lax.broadcasted_iota(jnp.int32, (FAR_ROWS, TQ), 0)

    def far_block(blk):
        back = a_lane - (blk * FAR_ROWS + row)
        bias = jnp.where((back > FAR_FROM // FOLD) & (back <= DILATED_WINDOW // FOLD), 0.0, NEG_INF)
        keys = slice(blk * FAR_ROWS, (blk + 1) * FAR_ROWS)
        scores = [jnp.concatenate(
            [lax.dot_general(kf_s[h, keys, g * CLS_PER_GROUP * LANES:(g + 1) * CLS_PER_GROUP * LANES], qcat_s[h, g],
                             _NT, preferred_element_type=F32) for g in range(groups)], axis=1) + bias
            for h in range(N_HEADS)]
        for h in range(N_HEADS):
            m_old = m_s[h]
            m_new = jnp.maximum(m_old, jnp.max(scores[h], axis=0, keepdims=True))
            p = jnp.exp(scores[h] - m_new)
            pv = []
            for g in range(groups):
                pg = p[:, g * LANES:(g + 1) * LANES]
                rhs = jnp.concatenate([jnp.where(cls_of_lane == j, pg, 0.0) for j in range(CLS_PER_GROUP)],
                                      axis=0).astype(BF16)
                lhs = jnp.concatenate(
                    [vf_s[h, :, (g * CLS_PER_GROUP + j) * cls_rows + blk * FAR_ROWS:
                           (g * CLS_PER_GROUP + j) * cls_rows + (blk + 1) * FAR_ROWS] for j in range(CLS_PER_GROUP)],
                    axis=1)
                pv.append(jnp.dot(lhs, rhs, preferred_element_type=F32))
            acc_s[h] = jnp.exp(m_old - m_new) * acc_s[h] + jnp.concatenate(pv, axis=1)
            m_s[h] = m_new

    for blk in range(cls_rows // FAR_ROWS):
        needed = (a0 + per - 1 - blk * FAR_ROWS > FAR_FROM // FOLD) & \
                 (a0 - (blk * FAR_ROWS + FAR_ROWS - 1) <= DILATED_WINDOW // FOLD)
        pl.when(needed)(functools.partial(far_block, blk))
    _flash_write(acc_s, o_ref)


def _dilated(p2, gq, gk, batch, seq):
    nq = seq // TQ
    t = batch * seq
    assert FAR_FROM % FOLD == 0 and TQ % FOLD == 0 and (seq // FOLD) % FAR_ROWS == 0 and (TQ // FOLD) * CLS_PER_GROUP == LANES
    bias_tiles = jnp.asarray(_band_bias_tiles(FAR_FROM, _near_mult, folded=True))
    pair_cols = lambda rows, col: [pl.BlockSpec((rows, LANES), functools.partial(
        lambda b, i, c: (b if rows == seq else b * nq + i, c), c=col // LANES + j)) for j in range(2)]
    gain = pl.BlockSpec((1, LANES), lambda b, i: (0, 0))
    return pl.pallas_call(
        functools.partial(_dilated_kernel, seq=seq),
        grid=(batch, nq),
        in_specs=pair_cols(TQ, COL_QD) + pair_cols(seq, COL_KD) + pair_cols(seq, COL_VD)
                 + [gain, gain, pl.BlockSpec(bias_tiles.shape, lambda b, i: (0, 0, 0))],
        out_specs=pl.BlockSpec((1, GROUP_W, TQ), lambda b, i: (b * nq + i, 0, 0)),
        out_shape=jax.ShapeDtypeStruct((t // TQ, GROUP_W, TQ), F32),
        scratch_shapes=[pltpu.VMEM((seq, N_HEADS * LANES), BF16),
                        pltpu.VMEM((seq // TK, N_HEADS * V_ROWS, TK), BF16),
                        pltpu.VMEM((N_HEADS, seq // FOLD, FOLD * LANES), BF16),
                        pltpu.VMEM((N_HEADS, V_ROWS, seq), BF16),
                        pltpu.VMEM((N_HEADS, TQ, LANES), BF16),
                        pltpu.VMEM((N_HEADS, TQ // LANES, LANES, CLS_PER_GROUP * LANES), BF16),
                        pltpu.VMEM((N_HEADS, TK, TQ), F32),
                        pltpu.VMEM((N_HEADS, V_ROWS, TQ), F32),
                        pltpu.VMEM((N_HEADS, 1, TQ), F32)],
        compiler_params=pltpu.CompilerParams(dimension_semantics=("parallel", "arbitrary"),
                                             vmem_limit_bytes=VMEM_LIMIT),
        name="dilated",
    )(p2, p2, p2, p2, p2, p2, gq, gk, bias_tiles)


def _stick_kernel(q_ref, k_ref, v_ref, o_ref, kp_s, vt_s, qp_s, sp_s, e_s, acc_s, *, seq):
    TQ, N_TOP = STICK_TQ, STICK_TQ // TK
    qi = pl.program_id(1)
    nkt = seq // TK

    @pl.when(qi == 0)
    def _prep():
        for c in range(nkt):
            rows = slice(c * TK, (c + 1) * TK)
            kb = k_ref[rows, :]
            for i in range(N_HEADS // 2):
                even, odd = _pair_keys(kb[:, i * LANES:(i + 1) * LANES], None)
                kp_s[rows, 2 * i * LANES:(2 * i + 1) * LANES] = even
                kp_s[rows, (2 * i + 1) * LANES:(2 * i + 2) * LANES] = odd
            vt_s[c] = v_ref[rows, :].T.astype(BF16)

    _prep_queries(q_ref[...], None, qi * TQ, None, qp_s)
    col = lax.broadcasted_iota(jnp.int32, (TK + SUM_PAD, TK), 1)
    srow = lax.broadcasted_iota(jnp.int32, (TK + SUM_PAD, TK), 0)
    sums = ((srow == TK) | ((srow < TK) & (col > srow))).astype(BF16)
    k_local = lax.broadcasted_iota(jnp.int32, (TK, TQ), 0)
    q_local = lax.broadcasted_iota(jnp.int32, (TK, TQ), 1)
    top_kt = qi * N_TOP + (N_TOP - 1)
    acc_s[...] = jnp.zeros(acc_s.shape, F32)

    def past_mask(tiles_above):
        return tiles_above * TK + k_local < q_local

    def logits(kt, h):
        off = pl.multiple_of(kt * TK, TK)
        return lax.dot_general(kp_s[pl.ds(off, TK), h * LANES:(h + 1) * LANES], qp_s[h // 2], _NT,
                               preferred_element_type=F32)

    def stage(h, z, past):
        sp = jnp.maximum(z, 0.0) + jnp.log(1.0 + jnp.exp2(jnp.abs(z) * (-LOG2E)))
        spm = sp if past is None else jnp.where(past, sp, 0.0)
        sp_s[h] = spm.astype(BF16)
        logsig = z - sp
        e_s[h] = logsig if past is None else jnp.where(past, logsig, NEG_INF)

    def tails(h):
        return [jnp.dot(sums, sp_s[h], preferred_element_type=F32)]

    def consume(kt, h, w, later):
        attn = jnp.exp(e_s[h] - w[0][0:TK, :] - later)
        acc_s[h] += jnp.dot(vt_s[kt, h * HEAD_DIM:(h + 1) * HEAD_DIM, :], attn.astype(BF16),
                            preferred_element_type=F32)
        return later + w[0][TK:TK + 1, :]

    ahead = MXU_AHEAD
    for h in range(N_HEADS):
        stage(h, logits(top_kt, h), past_mask(N_TOP - 1))

    def step(kt, carry, past):
        laters, w_first, z_first = carry
        w = {0: list(w_first)}
        z = {0: z_first}
        for h in range(1, ahead):
            w[h] = tails(h)
            z[h] = logits(kt - 1, h)
        out = []
        for h in range(N_HEADS):
            out.append(consume(kt, h, w.pop(h), laters[h]))
            stage(h, z.pop(h), past)
            if h + ahead < N_HEADS:
                w[h + ahead] = tails(h + ahead)
                z[h + ahead] = logits(kt - 1, h + ahead)
            elif h + ahead == N_HEADS:
                nxt = (tuple(tails(0)), logits(jnp.maximum(kt - 2, 0), 0))
        return tuple(out), nxt[0], nxt[1]

    carry = (tuple(jnp.zeros((1, TQ), F32) for _ in range(N_HEADS)),
             tuple(tails(0)), logits(jnp.maximum(top_kt - 1, 0), 0))
    for j in range(N_TOP - 1):
        carry = step(top_kt - j, carry, past_mask(N_TOP - 2 - j))
    below = qi * N_TOP
    laters, w_first, _ = lax.fori_loop(0, below, lambda i, c: step(below - i, c, None), carry)
    w = {0: list(w_first)}
    for h in range(1, ahead):
        w[h] = tails(h)
    for h in range(N_HEADS):
        consume(0, h, w.pop(h), laters[h])
        if h + ahead < N_HEADS:
            w[h + ahead] = tails(h + ahead)
    o_ref[0] = acc_s[...].reshape(GROUP_W, TQ)


def _stick(p2, batch, seq):
    TQ = STICK_TQ
    nq = seq // TQ
    t = batch * seq
    return pl.pallas_call(
        functools.partial(_stick_kernel, seq=seq),
        grid=(batch, nq),
        in_specs=[pl.BlockSpec((TQ, GROUP_W), lambda b, i: (b * nq + i, COL_QC // GROUP_W)),
                  pl.BlockSpec((seq, GROUP_W), lambda b, i: (b, COL_KC // GROUP_W)),
                  pl.BlockSpec((seq, GROUP_W), lambda b, i: (b, COL_VC // GROUP_W))],
        out_specs=pl.BlockSpec((1, GROUP_W, TQ), lambda b, i: (b * nq + i, 0, 0)),
        out_shape=jax.ShapeDtypeStruct((t // TQ, GROUP_W, TQ), F32),
        scratch_shapes=[pltpu.VMEM((seq, N_HEADS * LANES), BF16),
                        pltpu.VMEM((seq // TK, GROUP_W, TK), BF16),
                        pltpu.VMEM((N_HEADS // 2, TQ, LANES), BF16),
                        pltpu.VMEM((N_HEADS, TK, TQ), BF16),
                        pltpu.VMEM((N_HEADS, TK, TQ), F32),
                        pltpu.VMEM((N_HEADS, HEAD_DIM, TQ), F32)],
        compiler_params=pltpu.CompilerParams(dimension_semantics=("parallel", "arbitrary"),
                                             vmem_limit_bytes=VMEM_LIMIT),
        name="stick_breaking",
    )(p2, p2, p2)


def _mixout_kernel(x_ref, oa_ref, cvb_ref, cvc_ref, cvu_ref, pc_ref, pu_ref,
                   oc_ref, od_ref, cw_ref, gout_ref, wout_ref, o_ref, od_s, *, tiles_per_seq):
    i = pl.program_id(0)
    tm, d = x_ref.shape
    chains = [slice(c * MIX_CHAIN, (c + 1) * MIX_CHAIN) for c in range(tm // MIX_CHAIN)]

    def heads(ref, rows):
        tq = ref.shape[2]
        return ref[rows.start // tq, :, rows.start % tq:rows.start % tq + MIX_CHAIN].T

    per = TQ // FOLD
    for tile in range(tm // TQ):
        folded = od_ref[tile].T
        for half in range(GROUP_W // LANES):
            od_s[tile, half] = folded[:, half * LANES:(half + 1) * LANES]

    def unfolded(rows):
        tile, first = rows.start // TQ, (rows.start % TQ) // FOLD
        return jnp.concatenate(
            [jnp.concatenate([od_s[tile, half, pl.ds(a, FOLD, stride=per), :] for a in range(first, first + MIX_CHAIN // FOLD)],
                             axis=0) for half in range(GROUP_W // LANES)], axis=1)

    cu = cvc_ref[...] * cvu_ref[...]
    prev = jnp.where(i % tiles_per_seq == 0, 0.0, pc_ref[...] * pu_ref[...])
    full = jnp.concatenate([prev, cu], axis=0)
    back1 = pltpu.roll(full, 1, 0)[8:, :]
    back2 = pltpu.roll(full, 2, 0)[8:, :]
    cw = cw_ref[...]
    ob = cvb_ref[...] * (cw[0:1, :] * back2 + cw[1:2, :] * back1 + cw[2:3, :] * cu)

    assert d // HEAD_DIM <= LANES
    gather = ((lax.broadcasted_iota(jnp.int32, (d, LANES), 0) >> 6)
              == lax.broadcasted_iota(jnp.int32, (d, LANES), 1)).astype(BF16)
    spread = (lax.broadcasted_iota(jnp.int32, (LANES, d), 0)
              == (lax.broadcasted_iota(jnp.int32, (LANES, d), 1) >> 6)).astype(BF16)

    groups, ssq = [], []
    for rows in chains:
        g = jnp.concatenate([heads(oa_ref, rows), ob[rows], heads(oc_ref, rows), unfolded(rows)], axis=1)
        groups.append(g)
        ssq.append(_exact_dot(g * g, gather))
    scale = [_exact_dot(lax.rsqrt(s * (1.0 / HEAD_DIM) + RMS_EPS), spread) for s in ssq]
    for rows, g, sc in zip(chains, groups, scale):
        mixed = (g * sc * gout_ref[...]).astype(BF16)
        o_ref[rows, :] = x_ref[rows, :] + jnp.dot(mixed, wout_ref[...], preferred_element_type=F32)


def _mixout(x2, p2, oa, oc, od, cw, gout, wout, seq):
    t, d = x2.shape
    tm = 2 * MIX_CHAIN
    rows = lambda w, col: pl.BlockSpec((tm, w), lambda i: (i, col // w))
    heads_t = lambda tq: pl.BlockSpec((tm // tq, GROUP_W, tq), lambda i: (i, 0, 0))
    prev8 = lambda col: pl.BlockSpec((8, GROUP_W), lambda i: (jnp.maximum(i * (tm // 8) - 1, 0), col // GROUP_W))
    const = lambda shape: pl.BlockSpec(shape, lambda i: (0, 0))
    assert tm % TQ == 0 and tm % STICK_TQ == 0
    return pl.pallas_call(
        functools.partial(_mixout_kernel, tiles_per_seq=seq // tm),
        grid=(t // tm,),
        in_specs=[rows(d, 0), heads_t(TQ),
                  rows(GROUP_W, COL_CVB), rows(GROUP_W, COL_CVC), rows(GROUP_W, COL_CVU),
                  prev8(COL_CVC), prev8(COL_CVU),
                  heads_t(STICK_TQ), heads_t(TQ),
                  const((8, GROUP_W)), const((1, d)), const((d, d))],
        out_specs=rows(d, 0),
        out_shape=jax.ShapeDtypeStruct((t, d), F32),
        scratch_shapes=[pltpu.VMEM((tm // TQ, GROUP_W // LANES, TQ, LANES), F32)],
        compiler_params=pltpu.CompilerParams(dimension_semantics=("parallel",), vmem_limit_bytes=VMEM_LIMIT),
        name="mixout",
    )(x2, oa, p2, p2, p2, p2, p2, oc, od, cw, gout, wout)


def _ffn_kernel(x_ref, g_ref, wg_ref, wu_ref, wd_ref, o_ref, act_s, *, ff_chunk, out_chunk):
    h = _rms(x_ref[...], g_ref[...]).astype(BF16)
    dff = wg_ref.shape[1]
    for c in range(dff // ff_chunk):
        cols = slice(c * ff_chunk, (c + 1) * ff_chunk)
        a = jnp.dot(h, wg_ref[:, cols], preferred_element_type=F32)
        u = jnp.dot(h, wu_ref[:, cols], preferred_element_type=F32)
        act_s[:, cols] = (a * jax.nn.sigmoid(a) * u).astype(BF16)
    d = o_ref.shape[1]
    for c in range(d // out_chunk):
        cols = slice(c * out_chunk, (c + 1) * out_chunk)
        o_ref[:, cols] = x_ref[:, cols] + jnp.dot(act_s[...], wd_ref[:, cols], preferred_element_type=F32)


def _ffn(x2, g, wg, wu, wd):
    t, d = x2.shape
    dff = wg.shape[1]
    tm = 1024
    resident = lambda shape: pl.BlockSpec(shape, lambda i: (0, 0), pipeline_mode=pl.Buffered(1))
    return pl.pallas_call(
        functools.partial(_ffn_kernel, ff_chunk=256, out_chunk=256),
        grid=(t // tm,),
        in_specs=[pl.BlockSpec((tm, d), lambda i: (i, 0)),
                  pl.BlockSpec((1, d), lambda i: (0, 0)),
                  resident((d, dff)), resident((d, dff)), resident((dff, d))],
        out_specs=pl.BlockSpec((tm, d), lambda i: (i, 0)),
        out_shape=jax.ShapeDtypeStruct((t, d), F32),
        scratch_shapes=[pltpu.VMEM((tm, dff), BF16)],
        compiler_params=pltpu.CompilerParams(dimension_semantics=("parallel",), vmem_limit_bytes=VMEM_LIMIT),
        name="ffn",
    )(x2, g, wg, wu, wd)


def _permute_w_in_kernel(w_ref, o_ref):
    rows = w_ref.shape[0]
    n_rest = P_COLS - COL_CVB
    o_ref[:, 0:COL_GATE] = w_ref[:, 0:COL_GATE].astype(BF16)
    lane = lax.broadcasted_iota(jnp.int32, (rows, LANES), 1)
    o_ref[:, COL_GATE:COL_GATE + LANES] = jnp.where(lane < N_GATES, w_ref[:, COL_GATE:COL_GATE + LANES], 0.0).astype(BF16)
    o_ref[:, COL_CVB:P_COLS] = w_ref[:, COL_GATE + N_GATES:COL_GATE + N_GATES + n_rest].astype(BF16)


def _permute_w_in(w_in):
    d, cols = w_in.shape
    tr = 256
    return pl.pallas_call(
        _permute_w_in_kernel,
        grid=(d // tr,),
        in_specs=[pl.BlockSpec((tr, cols), lambda i: (i, 0))],
        out_specs=pl.BlockSpec((tr, P_COLS), lambda i: (i, 0)),
        out_shape=jax.ShapeDtypeStruct((d, P_COLS), BF16),
        compiler_params=pltpu.CompilerParams(dimension_semantics=("parallel",), vmem_limit_bytes=VMEM_LIMIT),
        name="permute_w_in",
    )(w_in)


def _layer(x2, batch, seq, g_mix, w_in, b_gate, g_q_nsa, g_k_cmp, g_k_slc, g_k_win, pe_k_cmp, pe_v_cmp,
           w1_k_cmp, w2_k_cmp, w1_v_cmp, w2_v_cmp, conv_w, g_q_dil, g_k_dil, g_out, w_out,
           g_ffn, w_gate, w_up, w_down):
    row = lambda v: v.reshape(1, -1)
    per_group = lambda g, n: jnp.tile(g, n).reshape(1, -1)
    p2 = _inproj(x2, row(g_mix), _permute_w_in(w_in))

    pe, w1, w2 = _compress_weights(pe_k_cmp, pe_v_cmp, w1_k_cmp, w2_k_cmp, w1_v_cmp, w2_v_cmp)
    kc, vc = _compress(p2, pe, w1, w2, per_group(g_k_cmp, 2), batch, seq)
    bg = jnp.zeros((1, LANES), F32).at[0, :N_GATES].set(b_gate)
    oa = _nsa(p2, kc, vc, per_group(g_q_nsa, N_HEADS), per_group(g_k_slc, 2), per_group(g_k_win, 2), bg, batch, seq)
    od = _dilated(p2, per_group(g_q_dil, 2), per_group(g_k_dil, 2), batch, seq)
    oc = _stick(p2, batch, seq)

    cwp = jnp.zeros((8, GROUP_W), F32).at[:CONV_K].set(conv_w)
    x1 = _mixout(x2, p2, oa, oc, od, cwp, row(g_out), w_out.astype(BF16), seq)
    return _ffn(x1, row(g_ffn), w_gate.astype(BF16), w_up.astype(BF16), w_down.astype(BF16))


def kernel(x, g_mix, w_in, b_gate, g_q_nsa, g_k_cmp, g_k_slc, g_k_win, pe_k_cmp, pe_v_cmp, w1_k_cmp, w2_k_cmp,
           w1_v_cmp, w2_v_cmp, conv_w, g_q_dil, g_k_dil, g_out, w_out, g_ffn, w_gate, w_up, w_down):
    batch, seq, d = x.shape
    assert seq % (CMP_STRIDE * LANES) == 0 and d % LANES == 0
    assert seq <= 64 * 256
    x2 = x.reshape(batch * seq, d)
    params = (g_mix, w_in, b_gate, g_q_nsa, g_k_cmp, g_k_slc, g_k_win, pe_k_cmp, pe_v_cmp, w1_k_cmp, w2_k_cmp,
              w1_v_cmp, w2_v_cmp, conv_w, g_q_dil, g_k_dil, g_out, w_out, g_ffn, w_gate, w_up, w_down)
    for layer in range(g_mix.shape[0]):
        x2 = _layer(x2, batch, seq, *[p[layer] for p in params])
    return x2.reshape(batch, seq, d)
```

```python
import functools

import numpy as np
import jax
import jax.numpy as jnp
from jax import lax
from jax.experimental import pallas as pl
from jax.experimental.pallas import tpu as pltpu

F32 = jnp.float32
BF16 = jnp.bfloat16

HEAD_DIM = 64
N_HEADS = 4
GROUP_W = N_HEADS * HEAD_DIM
CONV_K = 3
CMP_LEN = 32
CMP_STRIDE = 16
SEL_BLOCK = 64
N_SELECT = 16
NSA_WINDOW = 512
DILATED_CONFIGS = ((128, 1), (512, 4), (2048, 16))
DILATED_WINDOW = max(w for w, _ in DILATED_CONFIGS)
FOLD = 16
FAR_FROM = 512
FAR_ROWS = 128
NEG_INF = -1e30
FORCE_SCORE = 1e6
RMS_EPS = 1e-6

TQ = 512
TK = 256
N_TOP = TQ // TK
CLS_PER_GROUP = 128 * FOLD // TQ
STICK_TQ = 256
LANES = 128
MXU_DEPTH = 256
MIX_CHAIN = 256
MXU_AHEAD = 2
FLASH_AHEAD = 1
SUM_PAD = 16
V_ROWS = HEAD_DIM + SUM_PAD
LOG2E = 1.4426950408889634
VMEM_LIMIT = 52 * 1024 * 1024

COL_QA, COL_KVC, COL_KSV, COL_KWV, COL_GATE = 0, 256, 384, 512, 640
COL_CVB, COL_CVC, COL_CVU = 768, 1024, 1280
COL_QC, COL_KC, COL_VC = 1536, 1792, 2048
COL_QD, COL_KD, COL_VD = 2304, 2560, 2816
P_COLS = 3072
N_GATES = 12

_NT = (((1,), (1,)), ((), ()))


def _alibi_slopes():
    s = [2.0 ** (-8.0 * i / 8) for i in range(1, 9)]
    return tuple(s[0::2]), tuple(s[1::2])


SLOPES_A, SLOPES_D = _alibi_slopes()


def _rms(x, g):
    return x * lax.rsqrt(jnp.mean(x * x, axis=-1, keepdims=True) + RMS_EPS) * g


def _exact_dot(a, sel):
    hi = a.astype(BF16)
    lo = (a - hi.astype(F32)).astype(BF16)
    if 2 * a.shape[1] <= MXU_DEPTH:
        return jnp.dot(jnp.concatenate([hi, lo], axis=1), jnp.concatenate([sel, sel], axis=0),
                       preferred_element_type=F32)
    return jnp.dot(hi, sel, preferred_element_type=F32) + jnp.dot(lo, sel, preferred_element_type=F32)


def _group_rms(x, g):
    w = x.shape[1]
    same = ((lax.broadcasted_iota(jnp.int32, (w, w), 0) >> 6)
            == (lax.broadcasted_iota(jnp.int32, (w, w), 1) >> 6)).astype(BF16)
    ssq = _exact_dot(x * x, same)
    return x * lax.rsqrt(ssq * (1.0 / HEAD_DIM) + RMS_EPS) * g


EVEN_AUG, ODD_AUG = HEAD_DIM, 0


def _key_aug(pos, first):
    lane = lax.broadcasted_iota(jnp.int32, (pos.shape[0], LANES), 1) - first
    hi = (pos >> 6).astype(F32)
    lo = (pos & 63).astype(F32)
    return jnp.where(lane == 0, hi, jnp.where(lane == 1, lo, jnp.where((lane == 2) | (lane == 3), 1.0, 0.0)))


def _query_aug(t, slope, first):
    lane = lax.broadcasted_iota(jnp.int32, (t.shape[0], LANES), 1) - first
    hi = (t >> 6).astype(F32) * (-64.0 * slope)
    lo = (t & 63).astype(F32) * (-slope)
    return jnp.where(lane == 0, 64.0 * slope,
                     jnp.where(lane == 1, slope, jnp.where(lane == 2, hi, jnp.where(lane == 3, lo, 0.0))))


def _pair_keys(pair, pos):
    lane = lax.broadcasted_iota(jnp.int32, pair.shape, 1)
    aug_e = 0.0 if pos is None else _key_aug(pos, EVEN_AUG)
    aug_o = 0.0 if pos is None else _key_aug(pos, ODD_AUG)
    return (jnp.where(lane < HEAD_DIM, pair, aug_e).astype(BF16),
            jnp.where(lane >= HEAD_DIM, pair, aug_o).astype(BF16))


def _prep_queries(q, g, t0, slopes, qp_s):
    qn = (q if g is None else _group_rms(q, g)) * (HEAD_DIM ** -0.5)
    tpos = t0 + lax.broadcasted_iota(jnp.int32, (q.shape[0], 1), 0)
    lane = lax.broadcasted_iota(jnp.int32, (q.shape[0], LANES), 1)
    for i in range(N_HEADS // 2):
        pair = qn[:, i * LANES:(i + 1) * LANES]
        if slopes is None:
            qp_s[i] = pair.astype(BF16)
            continue
        even = jnp.where((lane >= EVEN_AUG) & (lane < EVEN_AUG + 4), _query_aug(tpos, slopes[2 * i], EVEN_AUG), pair)
        odd = jnp.where(lane < ODD_AUG + 4, _query_aug(tpos, slopes[2 * i + 1], ODD_AUG), pair)
        qp_s[2 * i] = even.astype(BF16)
        qp_s[2 * i + 1] = odd.astype(BF16)


def _inproj_kernel(x_hbm, g_ref, w_ref, o_hbm, *, tm, nchunk):
    t, d = x_hbm.shape

    def tile(x_ref, o_ref):
        h = _rms(x_ref[...], g_ref[...]).astype(BF16)
        cw = P_COLS // nchunk
        for c in range(nchunk):
            o_ref[:, c * cw:(c + 1) * cw] = jnp.dot(h, w_ref[:, c * cw:(c + 1) * cw], preferred_element_type=F32)

    pltpu.emit_pipeline(
        tile, grid=(t // tm,),
        in_specs=[pl.BlockSpec((tm, d), lambda i: (i, 0), pipeline_mode=pl.Buffered(3))],
        out_specs=[pl.BlockSpec((tm, P_COLS), lambda i: (i, 0))],
    )(x_hbm, o_hbm)


def _inproj(x2, g, w):
    t, d = x2.shape
    return pl.pallas_call(
        functools.partial(_inproj_kernel, tm=1024, nchunk=6),
        in_specs=[pl.BlockSpec(memory_space=pl.ANY),
                  pl.BlockSpec(memory_space=pltpu.VMEM),
                  pl.BlockSpec(memory_space=pltpu.VMEM)],
        out_specs=pl.BlockSpec(memory_space=pl.ANY),
        out_shape=jax.ShapeDtypeStruct((t, P_COLS), F32),
        compiler_params=pltpu.CompilerParams(vmem_limit_bytes=VMEM_LIMIT),
        name="inproj",
    )(x2, g, w)


def _gelu_tanh(x):
    return x * (0.5 * (1.0 + jnp.tanh(np.sqrt(2.0 / np.pi).astype(np.float32) * (x + 0.044715 * (x * x * x)))))


def _compress_kernel(kv_ref, pe_ref, w1_ref, w2_ref, gk_ref, kc_ref, vct_ref):
    nc = kv_ref.shape[0] // CMP_STRIDE
    first = None
    second = None
    for j in range(CMP_STRIDE):
        tok = kv_ref[pl.ds(j, nc, stride=CMP_STRIDE), :]
        a = jnp.dot((tok + pe_ref[j:j + 1, :]).astype(BF16), w1_ref[j], preferred_element_type=F32)
        b = jnp.dot((tok + pe_ref[CMP_STRIDE + j:CMP_STRIDE + j + 1, :]).astype(BF16), w1_ref[CMP_STRIDE + j],
                    preferred_element_type=F32)
        first = a if first is None else first + a
        second = b if second is None else second + b
    hid = first + pltpu.roll(second, nc - 1, 0)
    out = jnp.dot(_gelu_tanh(hid).astype(BF16), w2_ref[...], preferred_element_type=F32)
    kn = _group_rms(out, gk_ref[...])
    end = lax.broadcasted_iota(jnp.int32, (nc, 1), 0) * CMP_STRIDE + (CMP_LEN - 1)
    even, _ = _pair_keys(kn, end)
    _, odd = _pair_keys(pltpu.roll(kn, HEAD_DIM, 1), end)
    kc_ref[0] = jnp.concatenate([even, odd], axis=1)
    vct_ref[0] = out.T[HEAD_DIM:, :].astype(BF16)


def _compress(p2, pe, w1, w2, gk, batch, seq):
    nc = seq // CMP_STRIDE
    hid2 = w1.shape[2]
    return pl.pallas_call(
        _compress_kernel,
        grid=(batch,),
        in_specs=[pl.BlockSpec((seq, LANES), lambda i: (i, COL_KVC // LANES)),
                  pl.BlockSpec((CMP_LEN, LANES), lambda i: (0, 0)),
                  pl.BlockSpec((CMP_LEN, LANES, hid2), lambda i: (0, 0, 0)),
                  pl.BlockSpec((hid2, LANES), lambda i: (0, 0)),
                  pl.BlockSpec((1, LANES), lambda i: (0, 0))],
        out_specs=[pl.BlockSpec((1, nc, 2 * LANES), lambda i: (i, 0, 0)),
                   pl.BlockSpec((1, HEAD_DIM, nc), lambda i: (i, 0, 0))],
        out_shape=[jax.ShapeDtypeStruct((batch, nc, 2 * LANES), BF16),
                   jax.ShapeDtypeStruct((batch, HEAD_DIM, nc), BF16)],
        compiler_params=pltpu.CompilerParams(dimension_semantics=("parallel",), vmem_limit_bytes=VMEM_LIMIT),
        name="nsa_compress",
    )(p2, pe, w1, w2, gk)


def _compress_weights(pe_k, pe_v, w1_k, w2_k, w1_v, w2_v):
    hid = w1_k.shape[1]
    w1k = w1_k.reshape(CMP_LEN, HEAD_DIM, hid)
    w1v = w1_v.reshape(CMP_LEN, HEAD_DIM, hid)
    z1 = jnp.zeros_like(w1k)
    w1 = jnp.concatenate([jnp.concatenate([w1k, z1], axis=2), jnp.concatenate([z1, w1v], axis=2)], axis=1)
    z2 = jnp.zeros_like(w2_k)
    w2 = jnp.concatenate([jnp.concatenate([w2_k, z2], axis=1), jnp.concatenate([z2, w2_v], axis=1)], axis=0)
    pe = jnp.concatenate([pe_k, pe_v], axis=1)
    return pe, w1.astype(BF16), w2.astype(BF16)


ALL_LANES = slice(0, TQ)


def _with_lanes(full, lanes, part):
    pieces = ([full[:, :lanes.start]] if lanes.start > 0 else []) + [part] + \
             ([full[:, lanes.stop:]] if lanes.stop < TQ else [])
    return jnp.concatenate(pieces, axis=1) if len(pieces) > 1 else part


def _flash_scores(kp_of, bias, qp_s, s_s, m_cur, lanes=ALL_LANES):
    m_next = []
    for h in range(N_HEADS):
        s = lax.dot_general(kp_of(h), qp_s[h, lanes, :], _NT, preferred_element_type=F32) + bias[:, lanes]
        s_s[h, :, lanes] = s
        m_next.append(_with_lanes(m_cur[h], lanes, jnp.maximum(m_cur[h][:, lanes], jnp.max(s, axis=0, keepdims=True))))
    return tuple(m_next)


def _flash_accum(vt_of, s_s, acc_s, m_prev, m_cur, lanes=ALL_LANES):
    for h in range(N_HEADS):
        alpha = jnp.exp(m_prev[h][:, lanes] - m_cur[h][:, lanes])
        p = jnp.exp(s_s[h, :, lanes] - m_cur[h][:, lanes])
        acc_s[h, :, lanes] = alpha * acc_s[h, :, lanes] + jnp.dot(vt_of(h), p.astype(BF16),
                                                                 preferred_element_type=F32)


def _flash_accum_and_scores(kp_of, bias, vt_of, qp_s, s_s, acc_s, m_prev, m_cur, acc_lanes=ALL_LANES,
                            new_lanes=ALL_LANES):
    def scores(h):
        return (lax.dot_general(kp_of(h), qp_s[h, new_lanes, :], _NT, preferred_element_type=F32)
                + bias[:, new_lanes])

    m_next = []
    s_new = {h: scores(h) for h in range(FLASH_AHEAD)}
    for h in range(N_HEADS):
        alpha = jnp.exp(m_prev[h][:, acc_lanes] - m_cur[h][:, acc_lanes])
        p = jnp.exp(s_s[h, :, acc_lanes] - m_cur[h][:, acc_lanes])
        acc_s[h, :, acc_lanes] = alpha * acc_s[h, :, acc_lanes] + jnp.dot(vt_of(h), p.astype(BF16),
                                                                         preferred_element_type=F32)
        s_h = s_new.pop(h)
        s_s[h, :, new_lanes] = s_h
        m_next.append(_with_lanes(m_cur[h], new_lanes,
                                  jnp.maximum(m_cur[h][:, new_lanes], jnp.max(s_h, axis=0, keepdims=True))))
        if h + FLASH_AHEAD < N_HEADS:
            s_new[h + FLASH_AHEAD] = scores(h + FLASH_AHEAD)
    return tuple(m_next)


def _flash_pipelined(tile_of, n_tiles, kp_fn, bias_fn, vt_fn, qp_s, s_s, acc_s, o_ref, causal_lanes=True):
    top_lanes = [slice((N_TOP - 1 - i) * TK if causal_lanes else 0, TQ) for i in range(N_TOP)]
    acc_s[...] = jnp.zeros(acc_s.shape, F32)
    m0 = tuple(jnp.full((1, TQ), NEG_INF, F32) for _ in range(N_HEADS))
    first_kt = tile_of(0)
    m1 = _flash_scores(kp_fn(first_kt), bias_fn(first_kt, True), qp_s, s_s, m0, top_lanes[0])

    def step(i, carry, top):
        m_prev, m_cur = carry
        nxt = tile_of(i + 1)
        lanes = (top_lanes[i], top_lanes[i + 1]) if top else (ALL_LANES, ALL_LANES)
        m_next = _flash_accum_and_scores(kp_fn(nxt), bias_fn(nxt, top), vt_fn(tile_of(i)), qp_s, s_s, acc_s,
                                         m_prev, m_cur, *lanes)
        return m_cur, m_next

    carry = (m0, m1)
    for i in range(N_TOP - 1):
        carry = step(i, carry, True)
    m_prev, m_cur = lax.fori_loop(N_TOP - 1, n_tiles - 1, lambda i, c: step(i, c, False), carry)
    _flash_accum(vt_fn(tile_of(n_tiles - 1)), s_s, acc_s, m_prev, m_cur)
    if o_ref is None:
        return m_cur
    _flash_write(acc_s, o_ref)


def _flash_heads(acc_s):
    return [acc_s[h, 0:HEAD_DIM, :] * (1.0 / acc_s[h, HEAD_DIM:HEAD_DIM + 1, :]) for h in range(N_HEADS)]


def _flash_write(acc_s, o_ref):
    o_ref[0] = jnp.concatenate(_flash_heads(acc_s), axis=0)


def _values_t(v_t):
    row = lax.broadcasted_iota(jnp.int32, (V_ROWS - HEAD_DIM, v_t.shape[1]), 0)
    return jnp.concatenate([v_t, jnp.where(row == 0, 1.0, 0.0)], axis=0).astype(BF16)


def _nsa_kernel(q_ref, kc_ref, vct_ref, ksv_ref, kwv_ref, gq_ref, gks_ref, gkw_ref, wbias_ref, gate_ref, bg_ref,
                o_ref, ksp_s, vst_s, kwp_s, vwt_s, qp_s, imp_s, sel_s, s_s, acc_s, gate_s, hit_s, *, seq):
    qi = pl.program_id(1)
    gate_s[...] = jax.nn.sigmoid(gate_ref[...] + bg_ref[...]).T

    def add_branch(r, per_head):
        term = jnp.concatenate([gate_s[3 * h + r:3 * h + r + 1, :] * per_head[h] for h in range(N_HEADS)], axis=0)
        o_ref[0] = term if r == 0 else o_ref[0] + term

    nc = seq // CMP_STRIDE
    nsel = seq // SEL_BLOCK
    nkt = seq // TK

    @pl.when(qi == 0)
    def _prep():
        for c in range(nkt):
            rows = slice(c * TK, (c + 1) * TK)
            pos = c * TK + lax.broadcasted_iota(jnp.int32, (TK, 1), 0)
            for kv_ref, g_ref, kp_s, vt_s in ((ksv_ref, gks_ref, ksp_s, vst_s), (kwv_ref, gkw_ref, kwp_s, vwt_s)):
                blk = kv_ref[rows, :]
                kn = _group_rms(blk, g_ref[...])
                kp_s[rows, 0:LANES], _ = _pair_keys(kn, pos)
                _, kp_s[rows, LANES:2 * LANES] = _pair_keys(pltpu.roll(kn, HEAD_DIM, 1), pos)
                vt_s[c] = _values_t(blk.T[HEAD_DIM:, :])
        imp_s[:, 0:8, :] = jnp.zeros((TQ // LANES, 8, LANES), F32)

    t0 = qi * TQ
    _prep_queries(q_ref[...], gq_ref[...], t0, SLOPES_A, qp_s)

    kc = [kc_ref[0, :, 0:LANES], kc_ref[0, :, LANES:2 * LANES]]
    vct = vct_ref[0]
    tq_row = t0 + lax.broadcasted_iota(jnp.int32, (nc, TQ), 1)
    n_idx = lax.broadcasted_iota(jnp.int32, (nc, TQ), 0)
    vis = (tq_row >= n_idx * CMP_STRIDE + (CMP_LEN - 1)) & (n_idx < nc - 1)
    vis_bias = jnp.where(vis, 0.0, NEG_INF)
    sees_any = (t0 + lax.broadcasted_iota(jnp.int32, (1, TQ), 1) >= CMP_LEN - 1).astype(F32)
    imp = jnp.zeros((nc, TQ), F32)
    ocmp_t = []
    scores = [lax.dot_general(kc[h % 2], qp_s[h], _NT, preferred_element_type=F32) + vis_bias
              for h in range(N_HEADS)]
    for h in range(N_HEADS):
        sc = scores[h]
        e = jnp.exp(sc - jnp.max(sc, axis=0, keepdims=True))
        p = e * (sees_any / jnp.sum(e, axis=0, keepdims=True))
        ocmp_t.append(jnp.dot(vct, p.astype(BF16), preferred_element_type=F32))
        imp = imp + p
    add_branch(0, ocmp_t)

    halves = []
    for half in range(TQ // LANES):
        imp_s[half, 8:8 + nc, :] = imp[:, half * LANES:(half + 1) * LANES]
        r = [imp_s[half, pl.ds(8 + k, nsel, stride=4), :] for k in range(4)]
        rm1 = imp_s[half, pl.ds(7, nsel, stride=4), :]
        halves.append(rm1 + 2.0 * (r[0] + r[1] + r[2]) + r[3])
    imp_blk = jnp.concatenate(halves, axis=1)
    blk = lax.broadcasted_iota(jnp.int32, (nsel, TQ), 0)
    tl = t0 + lax.broadcasted_iota(jnp.int32, (nsel, TQ), 1)
    cur = tl >> 6
    forced = (blk == 0) | (blk == cur) | (blk == cur - 1)
    valid = blk * SEL_BLOCK <= tl
    score = jnp.where(forced, FORCE_SCORE, jnp.where(valid, imp_blk, -FORCE_SCORE))
    sub = 8
    per_tile = TK // SEL_BLOCK
    tile_hit = []
    for g in range(nsel // sub):
        mine = score[g * sub:(g + 1) * sub, :]
        blk_g = g * sub + lax.broadcasted_iota(jnp.int32, (sub, TQ), 0)
        rank = jnp.zeros((sub, TQ), F32)
        for j in range(nsel):
            row = score[j:j + 1, :]
            if j < g * sub:
                beats = row >= mine
            elif j >= (g + 1) * sub:
                beats = row > mine
            else:
                beats = (row > mine) | ((row == mine) & (blk_g > j))
            rank = rank + jnp.where(beats, 1.0, 0.0)
        sel_g = jnp.where(rank < min(N_SELECT, nsel), 0.0, NEG_INF)
        sel_s[g * sub:(g + 1) * sub, :] = sel_g
        for part in range(sub // per_tile):
            tile_hit.append(jnp.max(sel_g[part * per_tile:(part + 1) * per_tile, :]))

    n_hit = jnp.int32(0)
    for kt in reversed(range(nkt)):
        hit_s[n_hit] = kt
        n_hit = n_hit + ((tile_hit[kt] > -1.0) & (kt * TK < t0)).astype(jnp.int32)

    def block_bias(kt):
        return jnp.concatenate([jnp.broadcast_to(sel_s[pl.ds(kt * per_tile + j, 1), :], (SEL_BLOCK, TQ))
                                for j in range(per_tile)], axis=0)

    k_local = lax.broadcasted_iota(jnp.int32, (TK, TQ), 0)
    q_local = lax.broadcasted_iota(jnp.int32, (TK, TQ), 1)
    top_kt = qi * N_TOP + (N_TOP - 1)

    def kp_fn(kt):
        off = pl.multiple_of(kt * TK, TK)
        return lambda h: ksp_s[pl.ds(off, TK), (h % 2) * LANES:(h % 2 + 1) * LANES]

    def bias_fn(kt, top):
        if top:
            return jnp.where(kt * TK - t0 + k_local <= q_local, block_bias(kt), NEG_INF)
        return block_bias(kt)

    tile_of = lambda i: jnp.where(i < N_TOP, top_kt - i, hit_s[jnp.maximum(i - N_TOP, 0)])
    _flash_pipelined(tile_of, n_hit + N_TOP, kp_fn, bias_fn, lambda kt: (lambda h: vst_s[kt]), qp_s, s_s, acc_s, None)
    add_branch(1, _flash_heads(acc_s))

    def window_kp_fn(kt):
        off = pl.multiple_of(kt * TK, TK)
        return lambda h: kwp_s[pl.ds(off, TK), (h % 2) * LANES:(h % 2 + 1) * LANES]

    kt_lo = jnp.maximum(t0 - (NSA_WINDOW - 1), 0) // TK
    _flash_pipelined(lambda i: top_kt - i, top_kt + 1 - kt_lo, window_kp_fn, lambda kt, top: wbias_ref[top_kt - kt],
                     lambda kt: (lambda h: vwt_s[kt]), qp_s, s_s, acc_s, None)
    add_branch(2, _flash_heads(acc_s))


def _nsa(p2, kc, vc, gq, gks, gkw, bg, batch, seq):
    nq = seq // TQ
    nc = seq // CMP_STRIDE
    t = batch * seq
    wbias = jnp.asarray(_band_bias_tiles(NSA_WINDOW - 1, _window_mult))
    head_tiles = pl.BlockSpec((1, GROUP_W, TQ), lambda b, i: (b * nq + i, 0, 0))
    head_shape = jax.ShapeDtypeStruct((t // TQ, GROUP_W, TQ), F32)
    return pl.pallas_call(
        functools.partial(_nsa_kernel, seq=seq),
        grid=(batch, nq),
        in_specs=[pl.BlockSpec((TQ, GROUP_W), lambda b, i: (b * nq + i, COL_QA // GROUP_W)),
                  pl.BlockSpec((1, nc, 2 * LANES), lambda b, i: (b, 0, 0)),
                  pl.BlockSpec((1, HEAD_DIM, nc), lambda b, i: (b, 0, 0)),
                  pl.BlockSpec((seq, LANES), lambda b, i: (b, COL_KSV // LANES)),
                  pl.BlockSpec((seq, LANES), lambda b, i: (b, COL_KWV // LANES)),
                  pl.BlockSpec((1, GROUP_W), lambda b, i: (0, 0)),
                  pl.BlockSpec((1, LANES), lambda b, i: (0, 0)),
                  pl.BlockSpec((1, LANES), lambda b, i: (0, 0)),
                  pl.BlockSpec(wbias.shape, lambda b, i: (0, 0, 0)),
                  pl.BlockSpec((TQ, LANES), lambda b, i: (b * nq + i, COL_GATE // LANES)),
                  pl.BlockSpec((1, LANES), lambda b, i: (0, 0))],
        out_specs=head_tiles,
        out_shape=head_shape,
        scratch_shapes=[pltpu.VMEM((seq, 2 * LANES), BF16),
                        pltpu.VMEM((seq // TK, V_ROWS, TK), BF16),
                        pltpu.VMEM((seq, 2 * LANES), BF16),
                        pltpu.VMEM((seq // TK, V_ROWS, TK), BF16),
                        pltpu.VMEM((N_HEADS, TQ, LANES), BF16),
                        pltpu.VMEM((TQ // LANES, 8 + nc, LANES), F32),
                        pltpu.VMEM((seq // SEL_BLOCK, TQ), F32),
                        pltpu.VMEM((N_HEADS, TK, TQ), F32),
                        pltpu.VMEM((N_HEADS, V_ROWS, TQ), F32),
                        pltpu.VMEM((LANES, TQ), F32),
                        pltpu.SMEM((seq // TK + 1,), jnp.int32)],
        compiler_params=pltpu.CompilerParams(dimension_semantics=("parallel", "arbitrary"),
                                             vmem_limit_bytes=VMEM_LIMIT),
        name="nsa",
    )(p2, kc, vc, p2, p2, gq, gks, gkw, wbias, p2, bg)


def _band_bias_tiles(window, mult_fn, folded=False):
    nd = (window + TQ - 1) // TK + 1
    kk = np.arange(TK)[:, None]
    qq = np.arange(TQ)[None, :]
    if folded:
        qq = FOLD * (qq % (TQ // FOLD)) + qq // (TQ // FOLD)
    tiles = np.empty((nd, TK, TQ), np.float32)
    for di in range(nd):
        d = (di - (N_TOP - 1)) * TK + qq - kk
        mult = mult_fn(d)
        tiles[di] = np.where(mult > 0, np.log(np.maximum(mult, 1)), NEG_INF)
    return tiles


def _window_mult(d):
    return ((d >= 0) & (d <= NSA_WINDOW - 1)).astype(np.float64)


def _dilated_mult(d):
    m = np.zeros(d.shape, np.float64)
    for window, dil in DILATED_CONFIGS:
        m += ((d >= 0) & (d <= window) & (d % dil == 0)).astype(np.float64)
    return m


def _near_mult(d):
    return np.where(d <= FAR_FROM, _dilated_mult(d), 0.0)


def _dilated_kernel(q0_ref, q1_ref, k0_ref, k1_ref, v0_ref, v1_ref, gq_ref, gk_ref, bias_ref, o_ref,
                    kp_s, vt_s, kf_s, vf_s, qp_s, qcat_s, s_s, acc_s, m_s, *, seq):
    qi = pl.program_id(1)
    per = TQ // FOLD
    cls_rows = seq // FOLD
    pairs = ((q0_ref, k0_ref, v0_ref), (q1_ref, k1_ref, v1_ref))

    @pl.when(qi == 0)
    def _prep():
        for p, (_, k_ref, v_ref) in enumerate(pairs):
            for c in range(seq // TK):
                rows = slice(c * TK, (c + 1) * TK)
                pos = c * TK + lax.broadcasted_iota(jnp.int32, (TK, 1), 0)
                even, odd = _pair_keys(_group_rms(k_ref[rows, :], gk_ref[...]), pos)
                kp_s[rows, 2 * p * LANES:(2 * p + 1) * LANES] = even
                kp_s[rows, (2 * p + 1) * LANES:(2 * p + 2) * LANES] = odd
                v_t = v_ref[rows, :].T
                for j in range(2):
                    vt_s[c, (2 * p + j) * V_ROWS:(2 * p + j + 1) * V_ROWS, :] = _values_t(
                        v_t[j * HEAD_DIM:(j + 1) * HEAD_DIM, :])
            for r in range(FOLD):
                pos = r + FOLD * lax.broadcasted_iota(jnp.int32, (cls_rows, 1), 0)
                even, odd = _pair_keys(_group_rms(k_ref[pl.ds(r, cls_rows, stride=FOLD), :], gk_ref[...]), pos)
                kf_s[2 * p, :, r * LANES:(r + 1) * LANES] = even
                kf_s[2 * p + 1, :, r * LANES:(r + 1) * LANES] = odd
                v_t = v_ref[pl.ds(r, cls_rows, stride=FOLD), :].T
                for j in range(2):
                    vf_s[2 * p + j, :, r * cls_rows:(r + 1) * cls_rows] = _values_t(
                        v_t[j * HEAD_DIM:(j + 1) * HEAD_DIM, :])

    t0 = qi * TQ
    u = lax.broadcasted_iota(jnp.int32, (TQ, 1), 0)
    tpos = t0 + FOLD * (u % per) + u // per
    lane = lax.broadcasted_iota(jnp.int32, (TQ, LANES), 1)
    for p, (q_ref, _, _) in enumerate(pairs):
        q = jnp.concatenate([q_ref[pl.ds(r, per, stride=FOLD), :] for r in range(FOLD)], axis=0)
        pair = _group_rms(q, gq_ref[...]) * (HEAD_DIM ** -0.5)
        even = jnp.where((lane >= EVEN_AUG) & (lane < EVEN_AUG + 4), _query_aug(tpos, SLOPES_D[2 * p], EVEN_AUG), pair)
        odd = jnp.where(lane < ODD_AUG + 4, _query_aug(tpos, SLOPES_D[2 * p + 1], ODD_AUG), pair)
        qp_s[2 * p] = even.astype(BF16)
        qp_s[2 * p + 1] = odd.astype(BF16)

    kt_lo = jnp.maximum(t0 - FAR_FROM, 0) // TK
    top_kt = qi * N_TOP + (N_TOP - 1)

    def kp_fn(kt):
        off = pl.multiple_of(kt * TK, TK)
        return lambda h: kp_s[pl.ds(off, TK), h * LANES:(h + 1) * LANES]

    m = _flash_pipelined(lambda i: top_kt - i, top_kt + 1 - kt_lo, kp_fn, lambda kt, top: bias_ref[top_kt - kt],
                         lambda kt: (lambda h: vt_s[kt, h * V_ROWS:(h + 1) * V_ROWS, :]), qp_s, s_s, acc_s, None,
                         causal_lanes=False)
    for h in range(N_HEADS):
        m_s[h] = m[h]

    groups = TQ // LANES
    cls_of_row = lax.broadcasted_iota(jnp.int32, (LANES, LANES), 0) // per
    cls_of_lane = lax.broadcasted_iota(jnp.int32, (FAR_ROWS, LANES), 1) // per
    for h in range(N_HEADS):
        for g in range(groups):
            rows = qp_s[h, g * LANES:(g + 1) * LANES, :]
            qcat_s[h, g] = jnp.concatenate([jnp.where(cls_of_row == j, rows, jnp.zeros_like(rows))
                                            for j in range(CLS_PER_GROUP)], axis=1)

    a0 = t0 // FOLD
    a_lane = a0 + lax.broadcasted_iota(jnp.int32, (FAR_ROWS, TQ), 1) % per
    row = lax.broadcasted_iota(jnp.int32, (FAR_ROWS, TQ), 0)

    def far_block(blk):
        back = a_lane - (blk * FAR_ROWS + row)
        bias = jnp.where((back > FAR_FROM // FOLD) & (back <= DILATED_WINDOW // FOLD), 0.0, NEG_INF)
        keys = slice(blk * FAR_ROWS, (blk + 1) * FAR_ROWS)
        scores = [jnp.concatenate(
            [lax.dot_general(kf_s[h, keys, g * CLS_PER_GROUP * LANES:(g + 1) * CLS_PER_GROUP * LANES], qcat_s[h, g],
                             _NT, preferred_element_type=F32) for g in range(groups)], axis=1) + bias
            for h in range(N_HEADS)]
        for h in range(N_HEADS):
            m_old = m_s[h]
            m_new = jnp.maximum(m_old, jnp.max(scores[h], axis=0, keepdims=True))
            p = jnp.exp(scores[h] - m_new)
            pv = []
            for g in range(groups):
                pg = p[:, g * LANES:(g + 1) * LANES]
                rhs = jnp.concatenate([jnp.where(cls_of_lane == j, pg, 0.0) for j in range(CLS_PER_GROUP)],
                                      axis=0).astype(BF16)
                lhs = jnp.concatenate(
                    [vf_s[h, :, (g * CLS_PER_GROUP + j) * cls_rows + blk * FAR_ROWS:
                           (g * CLS_PER_GROUP + j) * cls_rows + (blk + 1) * FAR_ROWS] for j in range(CLS_PER_GROUP)],
                    axis=1)
                pv.append(jnp.dot(lhs, rhs, preferred_element_type=F32))
            acc_s[h] = jnp.exp(m_old - m_new) * acc_s[h] + jnp.concatenate(pv, axis=1)
            m_s[h] = m_new

    for blk in range(cls_rows // FAR_ROWS):
        needed = (a0 + per - 1 - blk * FAR_ROWS > FAR_FROM // FOLD) & \
                 (a0 - (blk * FAR_ROWS + FAR_ROWS - 1) <= DILATED_WINDOW // FOLD)
        pl.when(needed)(functools.partial(far_block, blk))
    _flash_write(acc_s, o_ref)


def _dilated(p2, gq, gk, batch, seq):
    nq = seq // TQ
    t = batch * seq
    assert FAR_FROM % FOLD == 0 and TQ % FOLD == 0 and (seq // FOLD) % FAR_ROWS == 0 and (TQ // FOLD) * CLS_PER_GROUP == LANES
    bias_tiles = jnp.asarray(_band_bias_tiles(FAR_FROM, _near_mult, folded=True))
    pair_cols = lambda rows, col: [pl.BlockSpec((rows, LANES), functools.partial(
        lambda b, i, c: (b if rows == seq else b * nq + i, c), c=col // LANES + j)) for j in range(2)]
    gain = pl.BlockSpec((1, LANES), lambda b, i: (0, 0))
    return pl.pallas_call(
        functools.partial(_dilated_kernel, seq=seq),
        grid=(batch, nq),
        in_specs=pair_cols(TQ, COL_QD) + pair_cols(seq, COL_KD) + pair_cols(seq, COL_VD)
                 + [gain, gain, pl.BlockSpec(bias_tiles.shape, lambda b, i: (0, 0, 0))],
        out_specs=pl.BlockSpec((1, GROUP_W, TQ), lambda b, i: (b * nq + i, 0, 0)),
        out_shape=jax.ShapeDtypeStruct((t // TQ, GROUP_W, TQ), F32),
        scratch_shapes=[pltpu.VMEM((seq, N_HEADS * LANES), BF16),
                        pltpu.VMEM((seq // TK, N_HEADS * V_ROWS, TK), BF16),
                        pltpu.VMEM((N_HEADS, seq // FOLD, FOLD * LANES), BF16),
                        pltpu.VMEM((N_HEADS, V_ROWS, seq), BF16),
                        pltpu.VMEM((N_HEADS, TQ, LANES), BF16),
                        pltpu.VMEM((N_HEADS, TQ // LANES, LANES, CLS_PER_GROUP * LANES), BF16),
                        pltpu.VMEM((N_HEADS, TK, TQ), F32),
                        pltpu.VMEM((N_HEADS, V_ROWS, TQ), F32),
                        pltpu.VMEM((N_HEADS, 1, TQ), F32)],
        compiler_params=pltpu.CompilerParams(dimension_semantics=("parallel", "arbitrary"),
                                             vmem_limit_bytes=VMEM_LIMIT),
        name="dilated",
    )(p2, p2, p2, p2, p2, p2, gq, gk, bias_tiles)


def _stick_kernel(q_ref, k_ref, v_ref, o_ref, kp_s, vt_s, qp_s, sp_s, e_s, acc_s, *, seq):
    TQ, N_TOP = STICK_TQ, STICK_TQ // TK
    qi = pl.program_id(1)
    nkt = seq // TK

    @pl.when(qi == 0)
    def _prep():
        for c in range(nkt):
            rows = slice(c * TK, (c + 1) * TK)
            kb = k_ref[rows, :]
            for i in range(N_HEADS // 2):
                even, odd = _pair_keys(kb[:, i * LANES:(i + 1) * LANES], None)
                kp_s[rows, 2 * i * LANES:(2 * i + 1) * LANES] = even
                kp_s[rows, (2 * i + 1) * LANES:(2 * i + 2) * LANES] = odd
            vt_s[c] = v_ref[rows, :].T.astype(BF16)

    _prep_queries(q_ref[...], None, qi * TQ, None, qp_s)
    col = lax.broadcasted_iota(jnp.int32, (TK + SUM_PAD, TK), 1)
    srow = lax.broadcasted_iota(jnp.int32, (TK + SUM_PAD, TK), 0)
    sums = ((srow == TK) | ((srow < TK) & (col > srow))).astype(BF16)
    k_local = lax.broadcasted_iota(jnp.int32, (TK, TQ), 0)
    q_local = lax.broadcasted_iota(jnp.int32, (TK, TQ), 1)
    top_kt = qi * N_TOP + (N_TOP - 1)
    acc_s[...] = jnp.zeros(acc_s.shape, F32)

    def past_mask(tiles_above):
        return tiles_above * TK + k_local < q_local

    def logits(kt, h):
        off = pl.multiple_of(kt * TK, TK)
        return lax.dot_general(kp_s[pl.ds(off, TK), h * LANES:(h + 1) * LANES], qp_s[h // 2], _NT,
                               preferred_element_type=F32)

    def stage(h, z, past):
        sp = jnp.maximum(z, 0.0) + jnp.log(1.0 + jnp.exp2(jnp.abs(z) * (-LOG2E)))
        spm = sp if past is None else jnp.where(past, sp, 0.0)
        sp_s[h] = spm.astype(BF16)
        logsig = z - sp
        e_s[h] = logsig if past is None else jnp.where(past, logsig, NEG_INF)

    def tails(h):
        return [jnp.dot(sums, sp_s[h], preferred_element_type=F32)]

    def consume(kt, h, w, later):
        attn = jnp.exp(e_s[h] - w[0][0:TK, :] - later)
        acc_s[h] += jnp.dot(vt_s[kt, h * HEAD_DIM:(h + 1) * HEAD_DIM, :], attn.astype(BF16),
                            preferred_element_type=F32)
        return later + w[0][TK:TK + 1, :]

    ahead = MXU_AHEAD
    for h in range(N_HEADS):
        stage(h, logits(top_kt, h), past_mask(N_TOP - 1))

    def step(kt, carry, past):
        laters, w_first, z_first = carry
        w = {0: list(w_first)}
        z = {0: z_first}
        for h in range(1, ahead):
            w[h] = tails(h)
            z[h] = logits(kt - 1, h)
        out = []
        for h in range(N_HEADS):
            out.append(consume(kt, h, w.pop(h), laters[h]))
            stage(h, z.pop(h), past)
            if h + ahead < N_HEADS:
                w[h + ahead] = tails(h + ahead)
                z[h + ahead] = logits(kt - 1, h + ahead)
            elif h + ahead == N_HEADS:
                nxt = (tuple(tails(0)), logits(jnp.maximum(kt - 2, 0), 0))
        return tuple(out), nxt[0], nxt[1]

    carry = (tuple(jnp.zeros((1, TQ), F32) for _ in range(N_HEADS)),
             tuple(tails(0)), logits(jnp.maximum(top_kt - 1, 0), 0))
    for j in range(N_TOP - 1):
        carry = step(top_kt - j, carry, past_mask(N_TOP - 2 - j))
    below = qi * N_TOP
    laters, w_first, _ = lax.fori_loop(0, below, lambda i, c: step(below - i, c, None), carry)
    w = {0: list(w_first)}
    for h in range(1, ahead):
        w[h] = tails(h)
    for h in range(N_HEADS):
        consume(0, h, w.pop(h), laters[h])
        if h + ahead < N_HEADS:
            w[h + ahead] = tails(h + ahead)
    o_ref[0] = acc_s[...].reshape(GROUP_W, TQ)


def _stick(p2, batch, seq):
    TQ = STICK_TQ
    nq = seq // TQ
    t = batch * seq
    return pl.pallas_call(
        functools.partial(_stick_kernel, seq=seq),
        grid=(batch, nq),
        in_specs=[pl.BlockSpec((TQ, GROUP_W), lambda b, i: (b * nq + i, COL_QC // GROUP_W)),
                  pl.BlockSpec((seq, GROUP_W), lambda b, i: (b, COL_KC // GROUP_W)),
                  pl.BlockSpec((seq, GROUP_W), lambda b, i: (b, COL_VC // GROUP_W))],
        out_specs=pl.BlockSpec((1, GROUP_W, TQ), lambda b, i: (b * nq + i, 0, 0)),
        out_shape=jax.ShapeDtypeStruct((t // TQ, GROUP_W, TQ), F32),
        scratch_shapes=[pltpu.VMEM((seq, N_HEADS * LANES), BF16),
                        pltpu.VMEM((seq // TK, GROUP_W, TK), BF16),
                        pltpu.VMEM((N_HEADS // 2, TQ, LANES), BF16),
                        pltpu.VMEM((N_HEADS, TK, TQ), BF16),
                        pltpu.VMEM((N_HEADS, TK, TQ), F32),
                        pltpu.VMEM((N_HEADS, HEAD_DIM, TQ), F32)],
        compiler_params=pltpu.CompilerParams(dimension_semantics=("parallel", "arbitrary"),
                                             vmem_limit_bytes=VMEM_LIMIT),
        name="stick_breaking",
    )(p2, p2, p2)


def _mixout_kernel(x_ref, oa_ref, cvb_ref, cvc_ref, cvu_ref, pc_ref, pu_ref,
                   oc_ref, od_ref, cw_ref, gout_ref, wout_ref, o_ref, od_s, *, tiles_per_seq):
    i = pl.program_id(0)
    tm, d = x_ref.shape
    chains = [slice(c * MIX_CHAIN, (c + 1) * MIX_CHAIN) for c in range(tm // MIX_CHAIN)]

    def heads(ref, rows):
        tq = ref.shape[2]
        return ref[rows.start // tq, :, rows.start % tq:rows.start % tq + MIX_CHAIN].T

    per = TQ // FOLD
    for tile in range(tm // TQ):
        folded = od_ref[tile].T
        for half in range(GROUP_W // LANES):
            od_s[tile, half] = folded[:, half * LANES:(half + 1) * LANES]

    def unfolded(rows):
        tile, first = rows.start // TQ, (rows.start % TQ) // FOLD
        return jnp.concatenate(
            [jnp.concatenate([od_s[tile, half, pl.ds(a, FOLD, stride=per), :] for a in range(first, first + MIX_CHAIN // FOLD)],
                             axis=0) for half in range(GROUP_W // LANES)], axis=1)

    cu = cvc_ref[...] * cvu_ref[...]
    prev = jnp.where(i % tiles_per_seq == 0, 0.0, pc_ref[...] * pu_ref[...])
    full = jnp.concatenate([prev, cu], axis=0)
    back1 = pltpu.roll(full, 1, 0)[8:, :]
    back2 = pltpu.roll(full, 2, 0)[8:, :]
    cw = cw_ref[...]
    ob = cvb_ref[...] * (cw[0:1, :] * back2 + cw[1:2, :] * back1 + cw[2:3, :] * cu)

    assert d // HEAD_DIM <= LANES
    gather = ((lax.broadcasted_iota(jnp.int32, (d, LANES), 0) >> 6)
              == lax.broadcasted_iota(jnp.int32, (d, LANES), 1)).astype(BF16)
    spread = (lax.broadcasted_iota(jnp.int32, (LANES, d), 0)
              == (lax.broadcasted_iota(jnp.int32, (LANES, d), 1) >> 6)).astype(BF16)

    groups, ssq = [], []
    for rows in chains:
        g = jnp.concatenate([heads(oa_ref, rows), ob[rows], heads(oc_ref, rows), unfolded(rows)], axis=1)
        groups.append(g)
        ssq.append(_exact_dot(g * g, gather))
    scale = [_exact_dot(lax.rsqrt(s * (1.0 / HEAD_DIM) + RMS_EPS), spread) for s in ssq]
    for rows, g, sc in zip(chains, groups, scale):
        mixed = (g * sc * gout_ref[...]).astype(BF16)
        o_ref[rows, :] = x_ref[rows, :] + jnp.dot(mixed, wout_ref[...], preferred_element_type=F32)


def _mixout(x2, p2, oa, oc, od, cw, gout, wout, seq):
    t, d = x2.shape
    tm = 2 * MIX_CHAIN
    rows = lambda w, col: pl.BlockSpec((tm, w), lambda i: (i, col // w))
    heads_t = lambda tq: pl.BlockSpec((tm // tq, GROUP_W, tq), lambda i: (i, 0, 0))
    prev8 = lambda col: pl.BlockSpec((8, GROUP_W), lambda i: (jnp.maximum(i * (tm // 8) - 1, 0), col // GROUP_W))
    const = lambda shape: pl.BlockSpec(shape, lambda i: (0, 0))
    assert tm % TQ == 0 and tm % STICK_TQ == 0
    return pl.pallas_call(
        functools.partial(_mixout_kernel, tiles_per_seq=seq // tm),
        grid=(t // tm,),
        in_specs=[rows(d, 0), heads_t(TQ),
                  rows(GROUP_W, COL_CVB), rows(GROUP_W, COL_CVC), rows(GROUP_W, COL_CVU),
                  prev8(COL_CVC), prev8(COL_CVU),
                  heads_t(STICK_TQ), heads_t(TQ),
                  const((8, GROUP_W)), const((1, d)), const((d, d))],
        out_specs=rows(d, 0),
        out_shape=jax.ShapeDtypeStruct((t, d), F32),
        scratch_shapes=[pltpu.VMEM((tm // TQ, GROUP_W // LANES, TQ, LANES), F32)],
        compiler_params=pltpu.CompilerParams(dimension_semantics=("parallel",), vmem_limit_bytes=VMEM_LIMIT),
        name="mixout",
    )(x2, oa, p2, p2, p2, p2, p2, oc, od, cw, gout, wout)


def _ffn_kernel(x_ref, g_ref, wg_ref, wu_ref, wd_ref, o_ref, act_s, *, ff_chunk, out_chunk):
    h = _rms(x_ref[...], g_ref[...]).astype(BF16)
    dff = wg_ref.shape[1]
    for c in range(dff // ff_chunk):
        cols = slice(c * ff_chunk, (c + 1) * ff_chunk)
        a = jnp.dot(h, wg_ref[:, cols], preferred_element_type=F32)
        u = jnp.dot(h, wu_ref[:, cols], preferred_element_type=F32)
        act_s[:, cols] = (a * jax.nn.sigmoid(a) * u).astype(BF16)
    d = o_ref.shape[1]
    for c in range(d // out_chunk):
        cols = slice(c * out_chunk, (c + 1) * out_chunk)
        o_ref[:, cols] = x_ref[:, cols] + jnp.dot(act_s[...], wd_ref[:, cols], preferred_element_type=F32)


def _ffn(x2, g, wg, wu, wd):
    t, d = x2.shape
    dff = wg.shape[1]
    tm = 1024
    resident = lambda shape: pl.BlockSpec(shape, lambda i: (0, 0), pipeline_mode=pl.Buffered(1))
    return pl.pallas_call(
        functools.partial(_ffn_kernel, ff_chunk=256, out_chunk=256),
        grid=(t // tm,),
        in_specs=[pl.BlockSpec((tm, d), lambda i: (i, 0)),
                  pl.BlockSpec((1, d), lambda i: (0, 0)),
                  resident((d, dff)), resident((d, dff)), resident((dff, d))],
        out_specs=pl.BlockSpec((tm, d), lambda i: (i, 0)),
        out_shape=jax.ShapeDtypeStruct((t, d), F32),
        scratch_shapes=[pltpu.VMEM((tm, dff), BF16)],
        compiler_params=pltpu.CompilerParams(dimension_semantics=("parallel",), vmem_limit_bytes=VMEM_LIMIT),
        name="ffn",
    )(x2, g, wg, wu, wd)


def _permute_w_in_kernel(w_ref, o_ref):
    rows = w_ref.shape[0]
    n_rest = P_COLS - COL_CVB
    o_ref[:, 0:COL_GATE] = w_ref[:, 0:COL_GATE].astype(BF16)
    lane = lax.broadcasted_iota(jnp.int32, (rows, LANES), 1)
    o_ref[:, COL_GATE:COL_GATE + LANES] = jnp.where(lane < N_GATES, w_ref[:, COL_GATE:COL_GATE + LANES], 0.0).astype(BF16)
    o_ref[:, COL_CVB:P_COLS] = w_ref[:, COL_GATE + N_GATES:COL_GATE + N_GATES + n_rest].astype(BF16)


def _permute_w_in(w_in):
    d, cols = w_in.shape
    tr = 256
    return pl.pallas_call(
        _permute_w_in_kernel,
        grid=(d // tr,),
        in_specs=[pl.BlockSpec((tr, cols), lambda i: (i, 0))],
        out_specs=pl.BlockSpec((tr, P_COLS), lambda i: (i, 0)),
        out_shape=jax.ShapeDtypeStruct((d, P_COLS), BF16),
        compiler_params=pltpu.CompilerParams(dimension_semantics=("parallel",), vmem_limit_bytes=VMEM_LIMIT),
        name="permute_w_in",
    )(w_in)


def _layer(x2, batch, seq, g_mix, w_in, b_gate, g_q_nsa, g_k_cmp, g_k_slc, g_k_win, pe_k_cmp, pe_v_cmp,
           w1_k_cmp, w2_k_cmp, w1_v_cmp, w2_v_cmp, conv_w, g_q_dil, g_k_dil, g_out, w_out,
           g_ffn, w_gate, w_up, w_down):
    row = lambda v: v.reshape(1, -1)
    per_group = lambda g, n: jnp.tile(g, n).reshape(1, -1)
    p2 = _inproj(x2, row(g_mix), _permute_w_in(w_in))

    pe, w1, w2 = _compress_weights(pe_k_cmp, pe_v_cmp, w1_k_cmp, w2_k_cmp, w1_v_cmp, w2_v_cmp)
    kc, vc = _compress(p2, pe, w1, w2, per_group(g_k_cmp, 2), batch, seq)
    bg = jnp.zeros((1, LANES), F32).at[0, :N_GATES].set(b_gate)
    oa = _nsa(p2, kc, vc, per_group(g_q_nsa, N_HEADS), per_group(g_k_slc, 2), per_group(g_k_win, 2), bg, batch, seq)
    od = _dilated(p2, per_group(g_q_dil, 2), per_group(g_k_dil, 2), batch, seq)
    oc = _stick(p2, batch, seq)

    cwp = jnp.zeros((8, GROUP_W), F32).at[:CONV_K].set(conv_w)
    x1 = _mixout(x2, p2, oa, oc, od, cwp, row(g_out), w_out.astype(BF16), seq)
    return _ffn(x1, row(g_ffn), w_gate.astype(BF16), w_up.astype(BF16), w_down.astype(BF16))


def kernel(x, g_mix, w_in, b_gate, g_q_nsa, g_k_cmp, g_k_slc, g_k_win, pe_k_cmp, pe_v_cmp, w1_k_cmp, w2_k_cmp,
           w1_v_cmp, w2_v_cmp, conv_w, g_q_dil, g_k_dil, g_out, w_out, g_ffn, w_gate, w_up, w_down):
    batch, seq, d = x.shape
    assert seq % (CMP_STRIDE * LANES) == 0 and d % LANES == 0
    assert seq <= 64 * 256
    x2 = x.reshape(batch * seq, d)
    params = (g_mix, w_in, b_gate, g_q_nsa, g_k_cmp, g_k_slc, g_k_win, pe_k_cmp, pe_v_cmp, w1_k_cmp, w2_k_cmp,
              w1_v_cmp, w2_v_cmp, conv_w, g_q_dil, g_k_dil, g_out, w_out, g_ffn, w_gate, w_up, w_down)
    for layer in range(g_mix.shape[0]):
        x2 = _layer(x2, batch, seq, *[p[layer] for p in params])
    return x2.reshape(batch, seq, d)
```
